```python
import jax, jax.numpy as jnp
from jax import lax
import numpy as np

D_MODEL = 1024
BATCH = 32
SEQ = 2048
DEPTH = 1

CHUNK = 128
A_GROUPS = 8
A_GROUP_DIM = D_MODEL // A_GROUPS
A_WIDTH = A_GROUPS * A_GROUP_DIM
B_HEADS = 8
B_HEAD_DIM = D_MODEL // B_HEADS
B_WIDTH = B_HEADS * B_HEAD_DIM
Q_BLOCK = 128
EPS = 1e-6

IN_WIDTHS = (A_WIDTH, A_WIDTH, A_WIDTH, B_WIDTH, B_WIDTH, B_WIDTH, B_WIDTH, D_MODEL, D_MODEL)
IN_PROJ_WIDTH = sum(IN_WIDTHS)
SPLIT_POINTS = tuple(int(p) for p in np.cumsum(IN_WIDTHS)[:-1])

kernel_name = "hybrid_gmlp_stickbreaking_gated_block"


def rms_norm(x, gain):
    xf = x.astype(jnp.float32)
    y = xf * lax.rsqrt(jnp.mean(xf * xf, axis=-1, keepdims=True) + EPS)
    return (y * gain.astype(jnp.float32)).astype(x.dtype)


def chunked_spatial_gating(u, v, w_s, b_s):
    bsz, seq, _ = u.shape
    n_chunks = seq // CHUNK
    vr = v.reshape(bsz, n_chunks, CHUNK, A_GROUPS, A_GROUP_DIM)
    causal = jnp.tril(jnp.ones((CHUNK, CHUNK), dtype=bool))
    w = jnp.where(causal[None], w_s, 0.0).astype(v.dtype)
    mixed = jnp.einsum('gts,bnsgc->bntgc', w, vr)
    mixed = mixed + b_s.T.astype(v.dtype)[None, None, :, :, None]
    return u * mixed.reshape(bsz, seq, A_WIDTH)


def stick_breaking_attention(q, k, v):
    bsz, seq, n_heads, head_dim = q.shape
    n_blocks = seq // Q_BLOCK
    scale = head_dim ** -0.5
    q_blocks = q.reshape(bsz, n_blocks, Q_BLOCK, n_heads, head_dim).transpose(1, 0, 2, 3, 4)
    key_pos = jnp.arange(seq)

    def one_block(args):
        q_blk, blk_idx = args
        logits = jnp.einsum('bthd,bshd->bhts', q_blk, k).astype(jnp.float32) * scale
        q_pos = blk_idx * Q_BLOCK + jnp.arange(Q_BLOCK)
        causal = key_pos[None, :] < q_pos[:, None]
        log_beta = jax.nn.log_sigmoid(logits)
        log_one_minus = jnp.where(causal, log_beta - logits, 0.0)
        suffix = lax.cumsum(log_one_minus, axis=3, reverse=True) - log_one_minus
        weights = jnp.where(causal, jnp.exp(log_beta + suffix), 0.0)
        return jnp.einsum('bhts,bshd->bthd', weights.astype(v.dtype), v)

    out = lax.map(one_block, (q_blocks, jnp.arange(n_blocks)))
    return out.transpose(1, 0, 2, 3, 4).reshape(bsz, seq, n_heads, head_dim)


def _fwd_setup_inputs(seed: int = 0) -> dict:
    key = jax.random.key(seed)
    ks = jax.random.split(key, 11)
    f32 = jnp.float32
    x = jax.random.normal(ks[0], (BATCH, SEQ, D_MODEL), f32)
    norm_in = 1.0 + 0.02 * jax.random.normal(ks[1], (DEPTH, D_MODEL), f32)
    w_in = jax.random.normal(ks[2], (DEPTH, D_MODEL, IN_PROJ_WIDTH), f32) * D_MODEL ** -0.5
    norm_v = 1.0 + 0.02 * jax.random.normal(ks[3], (DEPTH, A_WIDTH), f32)
    w_s = jax.random.normal(ks[4], (DEPTH, A_GROUPS, CHUNK, CHUNK), f32) * (0.5 * CHUNK ** -0.5)
    b_s = 1.0 + 0.02 * jax.random.normal(ks[5], (DEPTH, A_GROUPS, CHUNK), f32)
    w_o_gmlp = jax.random.normal(ks[6], (DEPTH, A_WIDTH, D_MODEL), f32) * A_WIDTH ** -0.5
    w_o_sb = jax.random.normal(ks[7], (DEPTH, B_WIDTH, D_MODEL), f32) * B_WIDTH ** -0.5
    w_out = jax.random.normal(ks[8], (DEPTH, D_MODEL, D_MODEL), f32) * D_MODEL ** -0.5
    norm_final = 1.0 + 0.02 * jax.random.normal(ks[9], (D_MODEL,), f32)
    return {"x": x, "norm_in": norm_in, "w_in": w_in, "norm_v": norm_v, "w_s": w_s,
            "b_s": b_s, "w_o_gmlp": w_o_gmlp, "w_o_sb": w_o_sb, "w_out": w_out,
            "norm_final": norm_final}


def _fwd_reference(x, norm_in, w_in, norm_v, w_s, b_s, w_o_gmlp, w_o_sb, w_out, norm_final):
    bsz, seq, _ = x.shape
    for layer in range(DEPTH):
        h = rms_norm(x, norm_in[layer])
        proj = jnp.einsum('bsd,de->bse', h, w_in[layer])
        u_a, v_a, z_a, q_b, k_b, v_b, z_b, gate_a, gate_b = jnp.split(proj, SPLIT_POINTS, axis=-1)

        u_a = jax.nn.gelu(u_a)
        v_a = rms_norm(jax.nn.gelu(v_a), norm_v[layer])
        y_a = chunked_spatial_gating(u_a, v_a, w_s[layer], b_s[layer]) * jax.nn.silu(z_a)

        heads = lambda t: t.reshape(bsz, seq, B_HEADS, B_HEAD_DIM)
        y_b = stick_breaking_attention(heads(q_b), heads(k_b), heads(v_b)).reshape(bsz, seq, B_WIDTH)
        y_b = y_b * jax.nn.silu(z_b)

        p_a = jnp.einsum('bse,ed->bsd', y_a, w_o_gmlp[layer])
        p_b = jnp.einsum('bse,ed->bsd', y_b, w_o_sb[layer])
        merged = jax.nn.sigmoid(gate_a) * p_a + jax.nn.sigmoid(gate_b) * p_b
        x = x + jnp.einsum('bsd,de->bse', merged, w_out[layer])
    return rms_norm(x, norm_final)


import jax as _jax
import jax.numpy as _jnp

TWIN_FORMAT = 'train_step'
FWD_PARAMS = ['x', 'norm_in', 'w_in', 'norm_v', 'w_s', 'b_s', 'w_o_gmlp', 'w_o_sb', 'w_out', 'norm_final']
TWIN_WEIGHTS = ['norm_in', 'w_in', 'norm_v', 'w_s', 'b_s', 'w_o_gmlp', 'w_o_sb', 'w_out', 'norm_final']
TWIN_DIFF_INPUT = 'x'
TWIN_INPUTS = ['x', 'norm_in', 'w_in', 'norm_v', 'w_s', 'b_s', 'w_o_gmlp', 'w_o_sb', 'w_out', 'norm_final', 'loss_target', 'm_norm_in', 'm_w_in', 'm_norm_v', 'm_w_s', 'm_b_s', 'm_w_o_gmlp', 'm_w_o_sb', 'm_w_out', 'm_norm_final', 'v_norm_in', 'v_w_in', 'v_norm_v', 'v_w_s', 'v_b_s', 'v_w_o_gmlp', 'v_w_o_sb', 'v_w_out', 'v_norm_final']
TWIN_OUTPUTS = ['loss', 'grad_x', 'grad_norm_in', 'grad_w_in', 'grad_norm_v', 'grad_w_s', 'grad_b_s', 'grad_w_o_gmlp', 'grad_w_o_sb', 'grad_w_out', 'grad_norm_final', 'delta_norm_in', 'delta_w_in', 'delta_norm_v', 'delta_w_s', 'delta_b_s', 'delta_w_o_gmlp', 'delta_w_o_sb', 'delta_w_out', 'delta_norm_final', 'new_m_norm_in', 'new_m_w_in', 'new_m_norm_v', 'new_m_w_s', 'new_m_b_s', 'new_m_w_o_gmlp', 'new_m_w_o_sb', 'new_m_w_out', 'new_m_norm_final', 'new_v_norm_in', 'new_v_w_in', 'new_v_norm_v', 'new_v_w_s', 'new_v_b_s', 'new_v_w_o_gmlp', 'new_v_w_o_sb', 'new_v_w_out', 'new_v_norm_final']
TWIN_LEAF_KINDS = {'loss': 'loss', 'grad_x': 'grad_x', 'grad_norm_in': 'grad_w', 'grad_w_in': 'grad_w', 'grad_norm_v': 'grad_w', 'grad_w_s': 'grad_w', 'grad_b_s': 'grad_w', 'grad_w_o_gmlp': 'grad_w', 'grad_w_o_sb': 'grad_w', 'grad_w_out': 'grad_w', 'grad_norm_final': 'grad_w', 'delta_norm_in': 'delta_w', 'delta_w_in': 'delta_w', 'delta_norm_v': 'delta_w', 'delta_w_s': 'delta_w', 'delta_b_s': 'delta_w', 'delta_w_o_gmlp': 'delta_w', 'delta_w_o_sb': 'delta_w', 'delta_w_out': 'delta_w', 'delta_norm_final': 'delta_w', 'new_m_norm_in': 'new_m', 'new_m_w_in': 'new_m', 'new_m_norm_v': 'new_m', 'new_m_w_s': 'new_m', 'new_m_b_s': 'new_m', 'new_m_w_o_gmlp': 'new_m', 'new_m_w_o_sb': 'new_m', 'new_m_w_out': 'new_m', 'new_m_norm_final': 'new_m', 'new_v_norm_in': 'new_v', 'new_v_w_in': 'new_v', 'new_v_norm_v': 'new_v', 'new_v_w_s': 'new_v', 'new_v_b_s': 'new_v', 'new_v_w_o_gmlp': 'new_v', 'new_v_w_o_sb': 'new_v', 'new_v_w_out': 'new_v', 'new_v_norm_final': 'new_v'}


def _forward(args):
    return _fwd_reference(*[args[k] for k in FWD_PARAMS])


def _output_shape():
    out = _jax.eval_shape(lambda: _forward(_fwd_setup_inputs(0)))
    return out.shape, out.dtype

N_MICROBATCH = 1
ADAM_LR = 0.001
ADAM_B1 = 0.9
ADAM_B2 = 0.999
ADAM_EPS = 1e-08
ADAM_WD = 0.01
ADAM_STEP = 10
PER_EXAMPLE_BATCH_AXIS = {'x': 0, 'loss_target': 0}
SHARED_INPUTS = []
_WEIGHT_DTYPES = {'norm_in': _jnp.float32, 'w_in': _jnp.float32, 'norm_v': _jnp.float32, 'w_s': _jnp.float32, 'b_s': _jnp.float32, 'w_o_gmlp': _jnp.float32, 'w_o_sb': _jnp.float32, 'w_out': _jnp.float32, 'norm_final': _jnp.float32}
MOMENT_SCALE = {'norm_in': 1.165209e-01, 'w_in': 3.902654e-02, 'norm_v': 1.802063e-02, 'w_s': 3.547865e-02, 'b_s': 5.183438e-02, 'w_o_gmlp': 5.373648e-02, 'w_o_sb': 5.044601e-02, 'w_out': 7.346040e-02, 'norm_final': 6.387293e+01}


def _to_microbatches(a, axis):
    t = _jnp.moveaxis(a, axis, 0)
    t = t.reshape((N_MICROBATCH, t.shape[0] // N_MICROBATCH) + t.shape[1:])
    return _jnp.moveaxis(t, 1, axis + 1)


def setup_inputs(seed: int = 0) -> dict:
    inp = _fwd_setup_inputs(seed)
    key = _jax.random.fold_in(_jax.random.key(seed), 7919)
    shape, _ = _output_shape()
    out = dict(inp)
    out["loss_target"] = _jax.random.normal(_jax.random.fold_in(key, 0), shape, _jnp.float32)
    for i, name in enumerate(TWIN_WEIGHTS):
        w = inp[name].astype(_jnp.float32)
        if MOMENT_SCALE is None:
            s = _jnp.sqrt(_jnp.mean(_jnp.square(w)) + 1e-30)
        else:
            s = MOMENT_SCALE[name]
        km, kv = _jax.random.split(_jax.random.fold_in(key, i + 1))
        out[name] = w
        out["m_" + name] = s * _jax.random.normal(km, w.shape, _jnp.float32)
        out["v_" + name] = (s * s) * _jax.random.uniform(kv, w.shape, _jnp.float32, 0.5, 1.5)
    if N_MICROBATCH > 1:
        for name, axis in PER_EXAMPLE_BATCH_AXIS.items():
            out[name] = _to_microbatches(out[name], axis)
    return {'x': out['x'], 'norm_in': out['norm_in'], 'w_in': out['w_in'], 'norm_v': out['norm_v'], 'w_s': out['w_s'], 'b_s': out['b_s'], 'w_o_gmlp': out['w_o_gmlp'], 'w_o_sb': out['w_o_sb'], 'w_out': out['w_out'], 'norm_final': out['norm_final'], 'loss_target': out['loss_target'], 'm_norm_in': out['m_norm_in'], 'm_w_in': out['m_w_in'], 'm_norm_v': out['m_norm_v'], 'm_w_s': out['m_w_s'], 'm_b_s': out['m_b_s'], 'm_w_o_gmlp': out['m_w_o_gmlp'], 'm_w_o_sb': out['m_w_o_sb'], 'm_w_out': out['m_w_out'], 'm_norm_final': out['m_norm_final'], 'v_norm_in': out['v_norm_in'], 'v_w_in': out['v_w_in'], 'v_norm_v': out['v_norm_v'], 'v_w_s': out['v_w_s'], 'v_b_s': out['v_b_s'], 'v_w_o_gmlp': out['v_w_o_gmlp'], 'v_w_o_sb': out['v_w_o_sb'], 'v_w_out': out['v_w_out'], 'v_norm_final': out['v_norm_final']}


def _loss(weights, diff, rest, loss_target):
    with _jax.named_scope("forward"):
        args = {**rest, TWIN_DIFF_INPUT: diff, **{k: w.astype(_WEIGHT_DTYPES[k]) for k, w in weights.items()}}
        y = _forward(args)
    with _jax.named_scope("loss_head"):
        err = _jnp.square(y.astype(_jnp.float32) - loss_target)
        return 0.5 * _jnp.sum(_jnp.mean(err, axis=-1)) if err.ndim else 0.5 * err


def _adamw(w, g, m, v):
    m = ADAM_B1 * m + (1.0 - ADAM_B1) * g
    v = ADAM_B2 * v + (1.0 - ADAM_B2) * _jnp.square(g)
    m_hat = m / (1.0 - ADAM_B1 ** ADAM_STEP)
    v_hat = v / (1.0 - ADAM_B2 ** ADAM_STEP)
    delta = -ADAM_LR * (m_hat / (_jnp.sqrt(v_hat) + ADAM_EPS) + ADAM_WD * w)
    return delta, m, v


def reference(x, norm_in, w_in, norm_v, w_s, b_s, w_o_gmlp, w_o_sb, w_out, norm_final, loss_target, m_norm_in, m_w_in, m_norm_v, m_w_s, m_b_s, m_w_o_gmlp, m_w_o_sb, m_w_out, m_norm_final, v_norm_in, v_w_in, v_norm_v, v_w_s, v_b_s, v_w_o_gmlp, v_w_o_sb, v_w_out, v_norm_final):
    given = dict(x=x, norm_in=norm_in, w_in=w_in, norm_v=norm_v, w_s=w_s, b_s=b_s, w_o_gmlp=w_o_gmlp, w_o_sb=w_o_sb, w_out=w_out, norm_final=norm_final, loss_target=loss_target, m_norm_in=m_norm_in, m_w_in=m_w_in, m_norm_v=m_norm_v, m_w_s=m_w_s, m_b_s=m_b_s, m_w_o_gmlp=m_w_o_gmlp, m_w_o_sb=m_w_o_sb, m_w_out=m_w_out, m_norm_final=m_norm_final, v_norm_in=v_norm_in, v_w_in=v_w_in, v_norm_v=v_norm_v, v_w_s=v_w_s, v_b_s=v_b_s, v_w_o_gmlp=v_w_o_gmlp, v_w_o_sb=v_w_o_sb, v_w_out=v_w_out, v_norm_final=v_norm_final)
    weights = {n: given[n] for n in TWIN_WEIGHTS}
    shared = {n: given[n] for n in SHARED_INPUTS}
    per_example = {n: given[n] for n in ['x']}
    grad_fn = _jax.value_and_grad(_loss, argnums=(0, 1))

    def one_microbatch(ex, loss_target):
        ex = dict(ex)
        diff = ex.pop(TWIN_DIFF_INPUT)
        return grad_fn(weights, diff, {**shared, **ex}, loss_target)

    if N_MICROBATCH == 1:
        loss, (grad_w, grad_x) = one_microbatch(per_example, given["loss_target"])
    else:
        def body(carry, xs):
            loss_sum, grad_sum = carry
            l_k, (gw_k, gx_k) = one_microbatch(xs[0], xs[1])
            with _jax.named_scope("update"):
                return (loss_sum + l_k, _jax.tree.map(_jnp.add, grad_sum, gw_k)), gx_k

        init = (_jnp.zeros((), _jnp.float32), _jax.tree.map(_jnp.zeros_like, weights))
        (loss, grad_w), grad_x = _jax.lax.scan(body, init, (per_example, given["loss_target"]))
    with _jax.named_scope("update"):
        delta_w, new_m, new_v = {}, {}, {}
        for n in TWIN_WEIGHTS:
            delta_w[n], new_m[n], new_v[n] = _adamw(weights[n], grad_w[n], given["m_" + n], given["v_" + n])
    return (loss, grad_x, *[grad_w[n] for n in TWIN_WEIGHTS], *[delta_w[n] for n in TWIN_WEIGHTS],
            *[new_m[n] for n in TWIN_WEIGHTS], *[new_v[n] for n in TWIN_WEIGHTS])
```

```python
import functools
import math

import jax
import jax.numpy as jnp
from jax import lax
from jax.experimental import pallas as pl
from jax.experimental.pallas import tpu as pltpu

F32 = jnp.float32
BF16 = jnp.bfloat16
SDS = jax.ShapeDtypeStruct
MESH_ID = pl.DeviceIdType.MESH

N_DEV = 8
LANE = 128
SUBLANE = 8
VMEM_LIMIT = 56 * 1024 * 1024
RMS_EPS = 1e-6

ADAM_LR = 0.001
ADAM_B1 = 0.9
ADAM_B2 = 0.999
ADAM_EPS = 1e-08
ADAM_WD = 0.01
ADAM_STEP = 10

NT_DIMS = (((1,), (1,)), ((), ()))
TN_DIMS = (((0,), (0,)), ((), ()))


def _params(semantics=None):
    return pltpu.CompilerParams(dimension_semantics=semantics, vmem_limit_bytes=VMEM_LIMIT)


def _tile(n, preferred):
    t = min(n, preferred)
    assert n % t == 0, (n, t)
    return t


def _sigmoid(x):
    return 1.0 / (1.0 + jnp.exp(-x))


def _silu(x):
    s = _sigmoid(x)
    return x * s, s * (1.0 + x * (1.0 - s))


def _gelu(x):
    k = math.sqrt(2.0 / math.pi)
    x2 = x * x
    t = jnp.tanh(k * (x + 0.044715 * (x * x2)))
    cdf = 0.5 * (1.0 + t)
    return x * cdf, cdf + 0.5 * x * (1.0 - t * t) * (k * (1.0 + 3.0 * 0.044715 * x2))


def _rms_scale(x):
    return lax.rsqrt(jnp.mean(x * x, axis=-1, keepdims=True) + RMS_EPS)


def _iotas(n):
    return (lax.broadcasted_iota(jnp.int32, (n, n), 0), lax.broadcasted_iota(jnp.int32, (n, n), 1))


def _adamw(w, g, m, v):
    m = ADAM_B1 * m + (1.0 - ADAM_B1) * g
    v = ADAM_B2 * v + (1.0 - ADAM_B2) * (g * g)
    m_hat = m / (1.0 - ADAM_B1 ** ADAM_STEP)
    v_hat = v / (1.0 - ADAM_B2 ** ADAM_STEP)
    delta = -ADAM_LR * (m_hat / (jnp.sqrt(v_hat) + ADAM_EPS) + ADAM_WD * w)
    return delta, m, v


def _dot(a, b):
    return jnp.dot(a, b, preferred_element_type=F32)


def _dot_nt(a, b):
    return lax.dot_general(a, b, NT_DIMS, preferred_element_type=F32)


def _dot_tn(a, b):
    return lax.dot_general(a, b, TN_DIMS, preferred_element_type=F32)


def _tri_sum(x, tri):
    t = x.shape[0]
    hi = x.astype(BF16)
    rest = x - hi.astype(F32)
    mid = rest.astype(BF16)
    lo = (rest - mid.astype(F32)).astype(BF16)
    s = _dot(jnp.concatenate([hi, mid, lo], axis=0), tri)
    return s[:t] + s[t:2 * t] + s[2 * t:]


def _sb_scores(q_i, k_j, scale, valid):
    z = _dot_nt(q_i, k_j) * scale
    log_beta = jnp.minimum(z, 0.0) - jnp.log(1.0 + jnp.exp(-jnp.abs(z)))
    log_rest = log_beta - z
    if valid is not None:
        log_rest = jnp.where(valid, log_rest, 0.0)
    return log_beta, log_rest


def _me():
    return lax.axis_index("x"), lax.axis_index("y"), lax.axis_index("c")


def _slot(p):
    return 4 * p[0] + 2 * p[1] + p[2]


def _peer(me, k):
    flips = ((k >> 2) & 1, (k >> 1) & 1, k & 1)
    return tuple(1 - a if f else a for a, f in zip(me, flips))


def _gather_weights(shards):
    n = len(shards)

    def body(*refs):
        ins, outs, stage = refs[:n], refs[n:2 * n], refs[2 * n:3 * n]
        send_sems, recv_sems, local_sems = refs[3 * n:]
        x, y, c = _me()
        me, sibling = (x, y, c), (x, y, 1 - c)
        chips = [(1 - x, y), (x, 1 - y), (1 - x, 1 - y)]

        def copy(a, k, block, to, src=None):
            dst = outs[a].at[_slot(block)]
            return pltpu.make_async_remote_copy(
                src_ref=dst if src is None else src, dst_ref=dst,
                send_sem=send_sems.at[7 * a + k], recv_sem=recv_sems.at[7 * a + k],
                device_id=to, device_id_type=MESH_ID)

        started = []
        for a in range(n):
            stage[a][...] = ins[a][...].astype(BF16)
            mine = pltpu.make_async_copy(stage[a], outs[a].at[_slot(me)], local_sems.at[a])
            mine.start()
            started.append(mine)
        sends = []
        for a in range(n):
            sends.append(copy(a, 0, me, sibling, src=stage[a]))
            sends += [copy(a, 1 + j, me, (*chip, c), src=stage[a]) for j, chip in enumerate(chips)]
        for cp in sends:
            cp.start()
        for a in range(n):
            for j, chip in enumerate(chips):
                copy(a, 1 + j, (*chip, c), me).wait_recv()
                passed = copy(a, 4 + j, (*chip, c), sibling)
                passed.start()
                sends.append(passed)
        for a in range(n):
            copy(a, 0, sibling, me).wait_recv()
            for j, chip in enumerate(chips):
                copy(a, 4 + j, (*chip, 1 - c), me).wait_recv()
        for cp in sends:
            cp.wait_send()
        for mine in started:
            mine.wait()

    return pl.pallas_call(
        body, name="gather_weights",
        out_shape=[SDS((N_DEV,) + s.shape, BF16) for s in shards],
        in_specs=[pl.BlockSpec(memory_space=pltpu.VMEM)] * n,
        out_specs=[pl.BlockSpec(memory_space=pl.ANY)] * n,
        scratch_shapes=[pltpu.VMEM(s.shape, BF16) for s in shards] + [
            pltpu.SemaphoreType.DMA((7 * n,)), pltpu.SemaphoreType.DMA((7 * n,)),
            pltpu.SemaphoreType.DMA((n,))],
        compiler_params=pltpu.CompilerParams(vmem_limit_bytes=VMEM_LIMIT),
    )(*shards)


def _scatter_partials(parts):
    n = len(parts)

    def body(*refs):
        ins, outs = refs[:n], refs[n:2 * n]
        send_sems, recv_sems, local_sems = refs[2 * n:]
        me = _me()
        mine = []
        for a in range(n):
            cp = pltpu.make_async_copy(ins[a].at[_slot(me)], outs[a].at[_slot(me)], local_sems.at[a])
            cp.start()
            mine.append(cp)
        sends = []
        for a in range(n):
            for k in range(1, N_DEV):
                peer = _peer(me, k)
                cp = pltpu.make_async_remote_copy(
                    src_ref=ins[a].at[_slot(peer)], dst_ref=outs[a].at[_slot(me)],
                    send_sem=send_sems.at[7 * a + k - 1], recv_sem=recv_sems.at[7 * a + k - 1],
                    device_id=peer, device_id_type=MESH_ID)
                cp.start()
                sends.append(cp)
        for a in range(n):
            for k in range(1, N_DEV):
                peer = _peer(me, k)
                landed = outs[a].at[_slot(peer)]
                pltpu.make_async_remote_copy(
                    src_ref=landed, dst_ref=landed,
                    send_sem=send_sems.at[7 * a + k - 1], recv_sem=recv_sems.at[7 * a + k - 1],
                    device_id=peer, device_id_type=MESH_ID).wait_recv()
        for cp in sends:
            cp.wait_send()
        for cp in mine:
            cp.wait()

    return pl.pallas_call(
        body, name="scatter_partials",
        out_shape=[SDS(p.shape, F32) for p in parts],
        in_specs=[pl.BlockSpec(memory_space=pl.ANY)] * n,
        out_specs=[pl.BlockSpec(memory_space=pl.ANY)] * n,
        scratch_shapes=[pltpu.SemaphoreType.DMA((7 * n,)), pltpu.SemaphoreType.DMA((7 * n,)),
                        pltpu.SemaphoreType.DMA((n,))],
        compiler_params=pltpu.CompilerParams(vmem_limit_bytes=VMEM_LIMIT),
    )(*parts)


def _allreduce_small(packed, groups, chunk):
    rows = packed.shape[0]
    gc = groups * chunk

    def body(p_ref, sum_ref, db_ref, loss_ref, gath, send_sems, recv_sems):
        me = _me()
        gath[_slot(me)] = p_ref[...]
        sends = []
        for k in range(1, N_DEV):
            peer = _peer(me, k)
            cp = pltpu.make_async_remote_copy(
                src_ref=p_ref, dst_ref=gath.at[_slot(me)],
                send_sem=send_sems.at[k - 1], recv_sem=recv_sems.at[k - 1],
                device_id=peer, device_id_type=MESH_ID)
            cp.start()
            sends.append(cp)
        for k in range(1, N_DEV):
            peer = _peer(me, k)
            landed = gath.at[_slot(peer)]
            pltpu.make_async_remote_copy(
                src_ref=landed, dst_ref=landed, send_sem=send_sems.at[k - 1], recv_sem=recv_sems.at[k - 1],
                device_id=peer, device_id_type=MESH_ID).wait_recv()
        for cp in sends:
            cp.wait_send()

        row, col = _iotas(chunk)
        tril = col <= row
        for g in range(groups):
            rs = slice(g * chunk, (g + 1) * chunk)
            tot = gath[0, rs, :]
            for d in range(1, N_DEV):
                tot = tot + gath[d, rs, :]
            sum_ref[rs, :] = jnp.where(tril, tot, 0.0)
        for g in range(groups):
            rs = slice(gc + g * chunk, gc + (g + 1) * chunk)
            tot = gath[0, rs, :]
            for d in range(1, N_DEV):
                tot = tot + gath[d, rs, :]
            sum_ref[rs, :] = tot
            db_ref[g:g + 1, :] = jnp.sum(tot.T, axis=0, keepdims=True)
        rs = slice(2 * gc, rows)
        tot = gath[0, rs, :]
        for d in range(1, N_DEV):
            tot = tot + gath[d, rs, :]
        sum_ref[rs, :] = tot
        loss_ref[...] = jnp.full((SUBLANE, LANE), jnp.sum(tot[rows - 2 * gc - SUBLANE:, :]), F32)

    return pl.pallas_call(
        body, name="allreduce_small",
        out_shape=[SDS((rows, LANE), F32), SDS((groups, chunk), F32), SDS((SUBLANE, LANE), F32)],
        in_specs=[pl.BlockSpec(memory_space=pltpu.VMEM)],
        out_specs=[pl.BlockSpec(memory_space=pltpu.VMEM)] * 3,
        scratch_shapes=[pltpu.VMEM((N_DEV, rows, LANE), F32),
                        pltpu.SemaphoreType.DMA((7,)), pltpu.SemaphoreType.DMA((7,))],
        compiler_params=pltpu.CompilerParams(vmem_limit_bytes=VMEM_LIMIT),
    )(packed)


def _in_proj(x2d, norm_in, wg_in):
    n, d = x2d.shape
    nsh, _, esh = wg_in.shape
    tm = _tile(n, 1024)

    def body(x_ref, g_ref, w_ref, proj_ref, h_ref):
        @pl.when(pl.program_id(1) == 0)
        def _():
            x = x_ref[...]
            h_ref[...] = (x * _rms_scale(x) * g_ref[...]).astype(BF16)

        proj_ref[...] = _dot(h_ref[...], w_ref[0])

    return pl.pallas_call(
        body, name="in_proj", grid=(n // tm, nsh),
        in_specs=[pl.BlockSpec((tm, d), lambda i, j: (i, 0)),
                  pl.BlockSpec((1, d), lambda i, j: (0, 0)),
                  pl.BlockSpec((1, d, esh), lambda i, j: (j, 0, 0))],
        out_specs=[pl.BlockSpec((tm, esh), lambda i, j: (i, j)),
                   pl.BlockSpec((tm, d), lambda i, j: (i, 0))],
        out_shape=[SDS((n, nsh * esh), F32), SDS((n, d), BF16)],
        compiler_params=_params(("parallel", "arbitrary")),
    )(x2d, norm_in, wg_in)


def _branch_a_fwd(proj, norm_v, w_s, b_col):
    n = proj.shape[0]
    d = norm_v.shape[1]
    groups, chunk, _ = w_s.shape
    tr = _tile(n, 4 * chunk)

    def body(u_ref, v_ref, z_ref, gv_ref, ws_ref, b_ref, ya_ref, vn_s, pre_s):
        row, col = _iotas(chunk)
        tril = col <= row
        vg, _ = _gelu(v_ref[...])
        vn_s[...] = (vg * _rms_scale(vg) * gv_ref[...]).astype(BF16)
        ug, _ = _gelu(u_ref[...])
        sz, _ = _silu(z_ref[...])
        pre_s[...] = ug * sz
        for g in range(groups):
            wm = jnp.where(tril, ws_ref[g], 0.0).astype(BF16)
            cs = slice(g * chunk, (g + 1) * chunk)
            for c in range(tr // chunk):
                rs = slice(c * chunk, (c + 1) * chunk)
                mixed = _dot(wm, vn_s[rs, cs]) + b_ref[g]
                ya_ref[rs, cs] = (pre_s[rs, cs] * mixed).astype(BF16)

    seg = lambda k: pl.BlockSpec((tr, d), lambda i: (i, k))
    return pl.pallas_call(
        body, name="branch_a_fwd", grid=(n // tr,),
        in_specs=[seg(0), seg(1), seg(2),
                  pl.BlockSpec((1, d), lambda i: (0, 0)),
                  pl.BlockSpec((groups, chunk, chunk), lambda i: (0, 0, 0)),
                  pl.BlockSpec((groups, chunk, 1), lambda i: (0, 0, 0))],
        out_specs=pl.BlockSpec((tr, d), lambda i: (i, 0)),
        out_shape=SDS((n, d), BF16),
        scratch_shapes=[pltpu.VMEM((tr, d), BF16), pltpu.VMEM((tr, d), F32)],
        compiler_params=_params(("parallel",)),
    )(proj, proj, proj, norm_v, w_s, b_col)


def _sb_fwd(proj, batch, seq, d, hd):
    heads = d // hd
    t = _tile(seq, LANE)
    scale = hd ** -0.5
    nblk = seq // t

    def body(q_ref, k_ref, v_ref, zb_ref, yb_ref, o_ref, tot_ref, qs, ks, vs):
        qs[...] = q_ref[...].astype(BF16)
        ks[...] = k_ref[...].astype(BF16)
        vs[...] = v_ref[...].astype(BF16)
        row, col = _iotas(t)
        later = (row > col).astype(BF16)
        causal = col < row

        def qblock(i, carry):
            r0 = pl.multiple_of(i * t, t)
            q_i = qs[pl.ds(r0, t), :]

            def tile(j, acc, run, valid):
                c0 = pl.multiple_of(j * t, t)
                log_beta, log_rest = _sb_scores(q_i, ks[pl.ds(c0, t), :], scale, valid)
                w = jnp.exp(log_beta + _tri_sum(log_rest, later) + run)
                if valid is not None:
                    w = jnp.where(valid, w, 0.0)
                acc = acc + _dot(w.astype(BF16), vs[pl.ds(c0, t), :])
                return acc, run + jnp.sum(log_rest, axis=-1, keepdims=True)

            acc, run = tile(i, jnp.zeros((t, hd), F32), jnp.zeros((t, 1), F32), causal)
            acc, run = lax.fori_loop(0, i, lambda jj, cr: tile(i - 1 - jj, cr[0], cr[1], None), (acc, run))
            o_ref[pl.ds(r0, t), :] = acc
            tot_ref[0, pl.ds(r0, t), :] = run
            sz, _ = _silu(zb_ref[pl.ds(r0, t), :])
            yb_ref[pl.ds(r0, t), :] = (acc * sz).astype(BF16)
            return carry

        lax.fori_loop(0, nblk, qblock, 0)

    col0 = d // hd
    seg = lambda k: pl.BlockSpec((seq, hd), lambda b, h: (b, k * col0 + h))
    return pl.pallas_call(
        body, name="sb_fwd", grid=(batch, heads),
        in_specs=[seg(3), seg(4), seg(5), seg(6)],
        out_specs=[pl.BlockSpec((seq, hd), lambda b, h: (b, h))] * 2 + [
            pl.BlockSpec((1, seq, 1), lambda b, h: (b * heads + h, 0, 0))],
        out_shape=[SDS((batch * seq, d), BF16), SDS((batch * seq, d), F32), SDS((batch * heads, seq, 1), F32)],
        scratch_shapes=[pltpu.VMEM((seq, hd), BF16)] * 3,
        compiler_params=_params(("parallel", "parallel")),
    )(proj, proj, proj, proj)


def _tail(x2d, tgt, ya, yb, proj, w_oa, w_ob, w_out, norm_final):
    n, d = x2d.shape
    e = proj.shape[1]
    tm = _tile(n, 256)

    def body(x_ref, t_ref, ya_ref, yb_ref, ga_ref, gb_ref, woa_ref, wob_ref, wout_ref, gf_ref,
             dproj_ref, dx2_ref, dya_ref, dyb_ref, mrg_ref, dpa_ref, dpb_ref, loss_ref, dgf_ref, dgb_s):
        i, kk = pl.program_id(0), pl.program_id(1)

        @pl.when(jnp.logical_and(i == 0, kk == 0))
        def _():
            loss_ref[...] = jnp.zeros_like(loss_ref)
            dgf_ref[...] = jnp.zeros_like(dgf_ref)

        @pl.when(kk == 0)
        def _():
            pa = _dot(ya_ref[...], woa_ref[...])
            pb = _dot(yb_ref[...], wob_ref[...])
            sa = _sigmoid(ga_ref[...])
            sb = _sigmoid(gb_ref[...])
            merged = (sa * pa + sb * pb).astype(BF16)
            mrg_ref[...] = merged
            x2 = x_ref[...] + _dot(merged, wout_ref[...])
            r2 = _rms_scale(x2)
            xh = x2 * r2
            gf = gf_ref[...]
            diff = xh * gf - t_ref[...]
            loss_ref[...] += jnp.sum(diff * diff, axis=0, keepdims=True) * (0.5 / d)
            dy = diff * (1.0 / d)
            dgf_ref[...] += jnp.sum(dy * xh, axis=0, keepdims=True)
            dxh = dy * gf
            dx2 = r2 * (dxh - xh * jnp.mean(dxh * xh, axis=-1, keepdims=True))
            dx2_ref[...] = dx2
            dm = _dot_nt(dx2.astype(BF16), wout_ref[...])
            dpa = (dm * sa).astype(BF16)
            dpb = (dm * sb).astype(BF16)
            dpa_ref[...] = dpa
            dpb_ref[...] = dpb
            dproj_ref[...] = (dm * pa * (sa * (1.0 - sa))).astype(BF16)
            dgb_s[...] = (dm * pb * (sb * (1.0 - sb))).astype(BF16)
            dya_ref[...] = _dot_nt(dpa, woa_ref[...])
            dyb_ref[...] = _dot_nt(dpb, wob_ref[...])

        @pl.when(kk == 1)
        def _():
            dproj_ref[...] = dgb_s[...]

    rows = lambda k=0: pl.BlockSpec((tm, d), lambda i, kk: (i, k))
    full = pl.BlockSpec((d, d), lambda i, kk: (0, 0))
    vec = pl.BlockSpec((1, d), lambda i, kk: (0, 0))
    return pl.pallas_call(
        body, name="tail", grid=(n // tm, 2),
        in_specs=[rows(), rows(), rows(), rows(), rows(7), rows(8), full, full, full, vec],
        out_specs=[pl.BlockSpec((tm, d), lambda i, kk: (i, 7 + kk)),
                   rows(), rows(), rows(), rows(), rows(), rows(), vec, vec],
        out_shape=[SDS((n, e), BF16), SDS((n, d), F32), SDS((n, d), F32), SDS((n, d), F32),
                   SDS((n, d), BF16), SDS((n, d), BF16), SDS((n, d), BF16),
                   SDS((1, d), F32), SDS((1, d), F32)],
        scratch_shapes=[pltpu.VMEM((tm, d), BF16)],
        compiler_params=_params(("arbitrary", "arbitrary")),
    )(x2d, tgt, ya, yb, proj, proj, w_oa, w_ob, w_out, norm_final)


def _tn_matmul(a, b, name):
    n, p = a.shape
    q = b.shape[1]
    tk = _tile(n, 512)

    def body(a_ref, b_ref, o_ref):
        @pl.when(pl.program_id(0) == 0)
        def _():
            o_ref[...] = jnp.zeros_like(o_ref)

        o_ref[...] += _dot_tn(a_ref[...], b_ref[...].astype(BF16))

    return pl.pallas_call(
        body, name=name, grid=(n // tk,),
        in_specs=[pl.BlockSpec((tk, p), lambda k: (k, 0)), pl.BlockSpec((tk, q), lambda k: (k, 0))],
        out_specs=pl.BlockSpec((p, q), lambda k: (0, 0)),
        out_shape=SDS((p, q), F32),
        compiler_params=_params(("arbitrary",)),
    )(a, b)


def _sb_bwd(proj, o, dyb, tot, dproj, batch, seq, d, hd):
    heads = d // hd
    t = _tile(seq, LANE)
    scale = hd ** -0.5
    nblk = seq // t

    def compute(q_ref, k_ref, v_ref, zb_ref, o_ref, dyb_ref, tot_ref, qs, ks, vs, dos, res):
        qs[...] = q_ref[...].astype(BF16)
        ks[...] = k_ref[...].astype(BF16)
        vs[...] = v_ref[...].astype(BF16)
        sz, dsz = _silu(zb_ref[...])
        dyb_v = dyb_ref[...]
        dos[...] = (dyb_v * sz).astype(BF16)
        res[3] = dyb_v * o_ref[...] * dsz
        res[1] = jnp.zeros((seq, hd), F32)
        res[2] = jnp.zeros((seq, hd), F32)
        row, col = _iotas(t)
        upto = (row <= col).astype(BF16)
        before = (row < col).astype(BF16)
        causal = col < row

        def qblock(i, carry):
            r0 = pl.multiple_of(i * t, t)
            q_i = qs[pl.ds(r0, t), :]
            do_i = dos[pl.ds(r0, t), :]
            tot_i = tot_ref[0, pl.ds(r0, t), :]

            def tile(j, dq, run, grun, valid):
                c0 = pl.multiple_of(j * t, t)
                k_j = ks[pl.ds(c0, t), :]
                log_beta, log_rest = _sb_scores(q_i, k_j, scale, valid)
                w = jnp.exp(log_beta + ((tot_i - run) - _tri_sum(log_rest, upto)))
                if valid is not None:
                    w = jnp.where(valid, w, 0.0)
                g = _dot_nt(do_i, vs[pl.ds(c0, t), :]) * w
                g_before = _tri_sum(g, before) + grun
                beta = jnp.exp(log_beta)
                dz = (g * (1.0 - beta) - g_before * beta) * scale
                if valid is not None:
                    dz = jnp.where(valid, dz, 0.0)
                dz = dz.astype(BF16)
                res[1, pl.ds(c0, t), :] += _dot_tn(dz, q_i)
                res[2, pl.ds(c0, t), :] += _dot_tn(w.astype(BF16), do_i)
                return (dq + _dot(dz, k_j), run + jnp.sum(log_rest, axis=-1, keepdims=True),
                        grun + jnp.sum(g, axis=-1, keepdims=True))

            zero = jnp.zeros((t, 1), F32)
            st = lax.fori_loop(0, i, lambda j, s: tile(j, s[0], s[1], s[2], None),
                               (jnp.zeros((t, hd), F32), zero, zero))
            st = tile(i, st[0], st[1], st[2], causal)
            res[0, pl.ds(r0, t), :] = st[0]
            return carry

        lax.fori_loop(0, nblk, qblock, 0)

    def body(q_ref, k_ref, v_ref, zb_ref, o_ref, dyb_ref, tot_ref, dproj_in, out_ref, qs, ks, vs, dos, res):
        del dproj_in
        kk = pl.program_id(2)

        @pl.when(kk == 0)
        def _():
            compute(q_ref, k_ref, v_ref, zb_ref, o_ref, dyb_ref, tot_ref, qs, ks, vs, dos, res)

        out_ref[...] = res[kk].astype(BF16)

    col0 = d // hd
    seg = lambda k: pl.BlockSpec((seq, hd), lambda b, h, kk: (b, k * col0 + h))
    head = pl.BlockSpec((seq, hd), lambda b, h, kk: (b, h))
    return pl.pallas_call(
        body, name="sb_bwd", grid=(batch, heads, 4),
        in_specs=[seg(3), seg(4), seg(5), seg(6), head, head,
                  pl.BlockSpec((1, seq, 1), lambda b, h, kk: (b * heads + h, 0, 0)),
                  pl.BlockSpec(memory_space=pl.ANY)],
        out_specs=pl.BlockSpec((seq, hd), lambda b, h, kk: (b, (3 + kk) * col0 + h)),
        out_shape=SDS(dproj.shape, dproj.dtype),
        input_output_aliases={7: 0},
        scratch_shapes=[pltpu.VMEM((seq, hd), BF16)] * 4 + [pltpu.VMEM((4, seq, hd), F32)],
        compiler_params=_params(("arbitrary", "arbitrary", "arbitrary")),
    )(proj, proj, proj, proj, o, dyb, tot, dproj)


def _branch_a_bwd(proj, dya, norm_v, w_s, b_col, dproj):
    n = proj.shape[0]
    d = norm_v.shape[1]
    groups, chunk, _ = w_s.shape
    tr = _tile(n, 2 * chunk)

    def body(u_ref, v_ref, z_ref, dya_ref, gv_ref, ws_ref, b_ref, dproj_in,
             out_ref, dws_ref, db_ref, dgv_ref, vn_s, dmix_s, dvn_s):
        del dproj_in

        @pl.when(pl.program_id(0) == 0)
        def _():
            dws_ref[...] = jnp.zeros_like(dws_ref)
            db_ref[...] = jnp.zeros_like(db_ref)
            dgv_ref[...] = jnp.zeros_like(dgv_ref)

        row, col = _iotas(chunk)
        tril = col <= row
        u, v, z, dya_v = u_ref[...], v_ref[...], z_ref[...], dya_ref[...]
        gv = gv_ref[...]
        vg, dvg_dv = _gelu(v)
        r = _rms_scale(vg)
        vh = vg * r
        vn_s[...] = (vh * gv).astype(BF16)
        ug, dug_du = _gelu(u)
        sz, dsz = _silu(z)
        dmix_s[...] = dya_v * ug * sz
        for g in range(groups):
            wm = jnp.where(tril, ws_ref[g], 0.0).astype(BF16)
            cs = slice(g * chunk, (g + 1) * chunk)
            for c in range(tr // chunk):
                rs = slice(c * chunk, (c + 1) * chunk)
                vn = vn_s[rs, cs]
                mixed = _dot(wm, vn) + b_ref[g]
                dmix = dmix_s[rs, cs]
                dmix16 = dmix.astype(BF16)
                dws_ref[g] += _dot_nt(dmix16, vn)
                db_ref[g] += dmix
                dvn_s[rs, cs] = _dot_tn(wm, dmix16)
                t_u = dya_v[rs, cs] * mixed
                out_ref[rs, g * chunk:(g + 1) * chunk] = (t_u * sz[rs, cs] * dug_du[rs, cs]).astype(BF16)
                out_ref[rs, 2 * d + g * chunk:2 * d + (g + 1) * chunk] = (t_u * ug[rs, cs] * dsz[rs, cs]).astype(BF16)
        dvn = dvn_s[...]
        dgv_ref[...] += jnp.sum(dvn * vh, axis=0, keepdims=True)
        dvh = dvn * gv
        dvg = r * (dvh - vh * jnp.mean(dvh * vh, axis=-1, keepdims=True))
        out_ref[:, d:2 * d] = (dvg * dvg_dv).astype(BF16)

    seg = lambda k: pl.BlockSpec((tr, d), lambda i: (i, k))
    return pl.pallas_call(
        body, name="branch_a_bwd", grid=(n // tr,),
        in_specs=[seg(0), seg(1), seg(2), seg(0),
                  pl.BlockSpec((1, d), lambda i: (0, 0)),
                  pl.BlockSpec((groups, chunk, chunk), lambda i: (0, 0, 0)),
                  pl.BlockSpec((groups, chunk, 1), lambda i: (0, 0, 0)),
                  pl.BlockSpec(memory_space=pl.ANY)],
        out_specs=[pl.BlockSpec((tr, 3 * d), lambda i: (i, 0)),
                   pl.BlockSpec((groups, chunk, chunk), lambda i: (0, 0, 0)),
                   pl.BlockSpec((groups, chunk, chunk), lambda i: (0, 0, 0)),
                   pl.BlockSpec((1, d), lambda i: (0, 0))],
        out_shape=[SDS(dproj.shape, dproj.dtype), SDS((groups, chunk, chunk), F32),
                   SDS((groups, chunk, chunk), F32), SDS((1, d), F32)],
        input_output_aliases={7: 0},
        scratch_shapes=[pltpu.VMEM((tr, d), BF16), pltpu.VMEM((tr, d), F32), pltpu.VMEM((tr, d), F32)],
        compiler_params=_params(("arbitrary",)),
    )(proj, proj, proj, dya, norm_v, w_s, b_col, dproj)


def _dx(dproj, wg_in, x2d, dx2, norm_in):
    n, d = x2d.shape
    nsh, _, esh = wg_in.shape
    tm = _tile(n, 1024)

    def body(dp_ref, w_ref, x_ref, dx2_ref, g_ref, gx_ref, dg_ref, acc):
        i, k = pl.program_id(0), pl.program_id(1)

        @pl.when(jnp.logical_and(i == 0, k == 0))
        def _():
            dg_ref[...] = jnp.zeros_like(dg_ref)

        @pl.when(k == 0)
        def _():
            acc[...] = jnp.zeros_like(acc)

        acc[...] += _dot_nt(dp_ref[...], w_ref[0])

        @pl.when(k == nsh - 1)
        def _():
            dh = acc[...]
            x = x_ref[...]
            r = _rms_scale(x)
            xh = x * r
            dg_ref[...] += jnp.sum(dh * xh, axis=0, keepdims=True)
            dxh = dh * g_ref[...]
            gx_ref[...] = dx2_ref[...] + r * (dxh - xh * jnp.mean(dxh * xh, axis=-1, keepdims=True))

    rows = pl.BlockSpec((tm, d), lambda i, k: (i, 0))
    vec = pl.BlockSpec((1, d), lambda i, k: (0, 0))
    return pl.pallas_call(
        body, name="dx", grid=(n // tm, nsh),
        in_specs=[pl.BlockSpec((tm, esh), lambda i, k: (i, k)),
                  pl.BlockSpec((1, d, esh), lambda i, k: (k, 0, 0)), rows, rows, vec],
        out_specs=[rows, vec],
        out_shape=[SDS((n, d), F32), SDS((1, d), F32)],
        scratch_shapes=[pltpu.VMEM((tm, d), F32)],
        compiler_params=_params(("arbitrary", "arbitrary")),
    )(dproj, wg_in, x2d, dx2, norm_in)


def _dw_in(h, dproj, nsh):
    n, d = h.shape
    esh = dproj.shape[1] // nsh
    tk = _tile(n, 512)

    def body(h_ref, dp_ref, o_ref):
        @pl.when(pl.program_id(1) == 0)
        def _():
            o_ref[...] = jnp.zeros_like(o_ref)

        o_ref[0] += _dot_tn(h_ref[...], dp_ref[...])

    return pl.pallas_call(
        body, name="dw_in", grid=(nsh, n // tk),
        in_specs=[pl.BlockSpec((tk, d), lambda j, k: (k, 0)), pl.BlockSpec((tk, esh), lambda j, k: (k, j))],
        out_specs=pl.BlockSpec((1, d, esh), lambda j, k: (j, 0, 0)),
        out_shape=SDS((nsh, d, esh), F32),
        compiler_params=_params(("parallel", "arbitrary")),
    )(h, dproj)


def _adamw_outputs(g_ref, d_ref, m_ref, v_ref, g, w, m, v):
    delta, m2, v2 = _adamw(w, g, m, v)
    g_ref[...] = g
    d_ref[...] = delta
    m_ref[...] = m2
    v_ref[...] = v2


def _reduce_adamw(slots, w, m, v, name):
    _, r, c = slots.shape
    tr = _tile(r, 128)

    def body(s_ref, w_ref, m_ref, v_ref, g_out, d_out, m_out, v_out):
        g = s_ref[0]
        for k in range(1, N_DEV):
            g = g + s_ref[k]
        _adamw_outputs(g_out, d_out, m_out, v_out, g, w_ref[...], m_ref[...], v_ref[...])

    blk = pl.BlockSpec((tr, c), lambda i: (i, 0))
    return pl.pallas_call(
        body, name=name, grid=(r // tr,),
        in_specs=[pl.BlockSpec((N_DEV, tr, c), lambda i: (0, i, 0)), blk, blk, blk],
        out_specs=[blk] * 4,
        out_shape=[SDS((r, c), F32)] * 4,
        compiler_params=_params(("parallel",)),
    )(slots, w, m, v)


def _adamw_small(g, w, m, v, name):
    def body(g_ref, w_ref, m_ref, v_ref, g_out, d_out, m_out, v_out):
        _adamw_outputs(g_out, d_out, m_out, v_out, g_ref[...], w_ref[...], m_ref[...], v_ref[...])

    return pl.pallas_call(
        body, name=name,
        out_shape=[SDS(g.shape, F32)] * 4,
        in_specs=[pl.BlockSpec(memory_space=pltpu.VMEM)] * 4,
        out_specs=[pl.BlockSpec(memory_space=pltpu.VMEM)] * 4,
    )(g, w, m, v)


def kernel(x, norm_in, w_in, norm_v, w_s, b_s, w_o_gmlp, w_o_sb, w_out, norm_final, loss_target, m_norm_in, m_w_in, m_norm_v, m_w_s, m_b_s, m_w_o_gmlp, m_w_o_sb, m_w_out, m_norm_final, v_norm_in, v_w_in, v_norm_v, v_w_s, v_b_s, v_w_o_gmlp, v_w_o_sb, v_w_out, v_norm_final):
    batch, seq, d = x.shape
    n = batch * seq
    groups, chunk = w_s.shape[1], w_s.shape[2]
    hd = LANE
    x2d = x.reshape(n, d)
    tgt = loss_target.reshape(n, d)
    b_col = b_s[0].reshape(groups, chunk, 1)
    norm_final2 = norm_final.reshape(1, d)

    wg_in, wg_oa, wg_ob, wg_out = _gather_weights([w_in[0], w_o_gmlp[0], w_o_sb[0], w_out[0]])
    rsh = wg_oa.shape[1]
    wf_oa, wf_ob, wf_out = (w.reshape(N_DEV * rsh, d) for w in (wg_oa, wg_ob, wg_out))

    proj, h = _in_proj(x2d, norm_in, wg_in)
    ya = _branch_a_fwd(proj, norm_v, w_s[0], b_col)
    yb, o, sb_tot = _sb_fwd(proj, batch, seq, d, hd)
    dproj, dx2, dya, dyb, merged, dpa, dpb, loss_vec, dgf = _tail(
        x2d, tgt, ya, yb, proj, wf_oa, wf_ob, wf_out, norm_final2)
    gp_oa = _tn_matmul(ya, dpa, "dw_o_gmlp")
    gp_ob = _tn_matmul(yb, dpb, "dw_o_sb")
    gp_out = _tn_matmul(merged, dx2, "dw_out")
    dproj = _sb_bwd(proj, o, dyb, sb_tot, dproj, batch, seq, d, hd)
    dproj, gp_ws, gp_b, gp_nv = _branch_a_bwd(proj, dya, norm_v, w_s[0], b_col, dproj)
    grad_x, gp_nin = _dx(dproj, wg_in, x2d, dx2, norm_in)
    gp_win = _dw_in(h, dproj, N_DEV)

    s_win, s_oa, s_ob, s_out = _scatter_partials(
        [gp_win] + [g.reshape(N_DEV, rsh, d) for g in (gp_oa, gp_ob, gp_out)])

    slab = lambda a: a.reshape(d // LANE, LANE)
    gc = groups * chunk
    packed = jnp.concatenate(
        [gp_ws.reshape(gc, chunk), gp_b.reshape(gc, chunk), slab(gp_nin), slab(gp_nv), slab(dgf), slab(loss_vec)],
        axis=0)
    tot, g_b, loss_slab = _allreduce_small(packed, groups, chunk)
    ns = d // LANE
    g_ws = tot[:gc]
    g_nin, g_nv, g_nf = (tot[2 * gc + k * ns:2 * gc + (k + 1) * ns] for k in range(3))
    loss = loss_slab[0, 0]

    res = {}
    res["w_in"] = _reduce_adamw(s_win, w_in[0], m_w_in[0], v_w_in[0], "adamw_w_in")
    res["w_o_gmlp"] = _reduce_adamw(s_oa, w_o_gmlp[0], m_w_o_gmlp[0], v_w_o_gmlp[0], "adamw_w_o_gmlp")
    res["w_o_sb"] = _reduce_adamw(s_ob, w_o_sb[0], m_w_o_sb[0], v_w_o_sb[0], "adamw_w_o_sb")
    res["w_out"] = _reduce_adamw(s_out, w_out[0], m_w_out[0], v_w_out[0], "adamw_w_out")
    res["norm_in"] = _adamw_small(g_nin, slab(norm_in), slab(m_norm_in), slab(v_norm_in), "adamw_norm_in")
    res["norm_v"] = _adamw_small(g_nv, slab(norm_v), slab(m_norm_v), slab(v_norm_v), "adamw_norm_v")
    res["norm_final"] = _adamw_small(g_nf, slab(norm_final), slab(m_norm_final), slab(v_norm_final), "adamw_norm_final")
    res["w_s"] = _adamw_small(g_ws, w_s.reshape(gc, chunk), m_w_s.reshape(gc, chunk), v_w_s.reshape(gc, chunk), "adamw_w_s")
    res["b_s"] = _adamw_small(g_b, b_s[0], m_b_s[0], v_b_s[0], "adamw_b_s")

    shapes = {"norm_in": norm_in.shape, "w_in": w_in.shape, "norm_v": norm_v.shape, "w_s": w_s.shape,
              "b_s": b_s.shape, "w_o_gmlp": w_o_gmlp.shape, "w_o_sb": w_o_sb.shape, "w_out": w_out.shape,
              "norm_final": norm_final.shape}
    names = list(shapes)
    outs = [loss, grad_x.reshape(batch, seq, d)]
    for kind in range(4):
        outs += [res[name][kind].reshape(shapes[name]) for name in names]
    return tuple(outs)
```

```python
import functools
import math

import jax
import jax.numpy as jnp
from jax import lax
from jax.experimental import pallas as pl
from jax.experimental.pallas import tpu as pltpu

F32 = jnp.float32
BF16 = jnp.bfloat16
SDS = jax.ShapeDtypeStruct
MESH_ID = pl.DeviceIdType.MESH

N_DEV = 8
LANE = 128
SUBLANE = 8
VMEM_LIMIT = 56 * 1024 * 1024
SB_TILE = 256
SB_HEADS = 2
RMS_EPS = 1e-6

ADAM_LR = 0.001
ADAM_B1 = 0.9
ADAM_B2 = 0.999
ADAM_EPS = 1e-08
ADAM_WD = 0.01
ADAM_STEP = 10

NT_DIMS = (((1,), (1,)), ((), ()))
TN_DIMS = (((0,), (0,)), ((), ()))


def _params(semantics=None):
    return pltpu.CompilerParams(dimension_semantics=semantics, vmem_limit_bytes=VMEM_LIMIT)


def _tile(n, preferred):
    t = min(n, preferred)
    assert n % t == 0, (n, t)
    return t


def _sigmoid(x):
    return 1.0 / (1.0 + jnp.exp(-x))


def _silu(x):
    s = _sigmoid(x)
    return x * s, s * (1.0 + x * (1.0 - s))


def _gelu(x):
    k = math.sqrt(2.0 / math.pi)
    x2 = x * x
    t = jnp.tanh(k * (x + 0.044715 * (x * x2)))
    cdf = 0.5 * (1.0 + t)
    return x * cdf, cdf + 0.5 * x * (1.0 - t * t) * (k * (1.0 + 3.0 * 0.044715 * x2))


def _rms_scale(x):
    return lax.rsqrt(jnp.mean(x * x, axis=-1, keepdims=True) + RMS_EPS)


def _iotas(n):
    return (lax.broadcasted_iota(jnp.int32, (n, n), 0), lax.broadcasted_iota(jnp.int32, (n, n), 1))


def _adamw(w, g, m, v):
    m = ADAM_B1 * m + (1.0 - ADAM_B1) * g
    v = ADAM_B2 * v + (1.0 - ADAM_B2) * (g * g)
    m_hat = m / (1.0 - ADAM_B1 ** ADAM_STEP)
    v_hat = v / (1.0 - ADAM_B2 ** ADAM_STEP)
    delta = -ADAM_LR * (m_hat / (jnp.sqrt(v_hat) + ADAM_EPS) + ADAM_WD * w)
    return delta, m, v


def _dot(a, b):
    return jnp.dot(a, b, preferred_element_type=F32)


def _dot_nt(a, b):
    return lax.dot_general(a, b, NT_DIMS, preferred_element_type=F32)


def _dot_tn(a, b):
    return lax.dot_general(a, b, TN_DIMS, preferred_element_type=F32)


def _tri_sum(x, tri):
    t = x.shape[0]
    hi = x.astype(BF16)
    lo = (x - hi.astype(F32)).astype(BF16)
    s = _dot(jnp.concatenate([hi, lo], axis=0), tri)
    return s[:t] + s[t:]


def _sb_scores(q_i, k_j, scale, valid):
    z = _dot_nt(q_i, k_j) * scale
    log_beta = jnp.minimum(z, 0.0) - jnp.log(1.0 + jnp.exp(-jnp.abs(z)))
    log_rest = log_beta - z
    if valid is not None:
        log_rest = jnp.where(valid, log_rest, 0.0)
    return log_beta, log_rest


def _me():
    return lax.axis_index("x"), lax.axis_index("y"), lax.axis_index("c")


def _slot(p):
    return 4 * p[0] + 2 * p[1] + p[2]


def _peer(me, k):
    flips = ((k >> 2) & 1, (k >> 1) & 1, k & 1)
    return tuple(1 - a if f else a for a, f in zip(me, flips))


def _gather_weights(shards):
    n = len(shards)

    def body(*refs):
        ins, outs, stage = refs[:n], refs[n:2 * n], refs[2 * n:3 * n]
        send_sems, recv_sems, local_sems = refs[3 * n:]
        x, y, c = _me()
        me, sibling = (x, y, c), (x, y, 1 - c)
        chips = [(1 - x, y), (x, 1 - y), (1 - x, 1 - y)]

        def copy(a, k, block, to, src=None):
            dst = outs[a].at[_slot(block)]
            return pltpu.make_async_remote_copy(
                src_ref=dst if src is None else src, dst_ref=dst,
                send_sem=send_sems.at[7 * a + k], recv_sem=recv_sems.at[7 * a + k],
                device_id=to, device_id_type=MESH_ID)

        started = []
        for a in range(n):
            stage[a][...] = ins[a][...].astype(BF16)
            mine = pltpu.make_async_copy(stage[a], outs[a].at[_slot(me)], local_sems.at[a])
            mine.start()
            started.append(mine)
        sends = []
        for a in range(n):
            sends.append(copy(a, 0, me, sibling, src=stage[a]))
            sends += [copy(a, 1 + j, me, (*chip, c), src=stage[a]) for j, chip in enumerate(chips)]
        for cp in sends:
            cp.start()
        for a in range(n):
            for j, chip in enumerate(chips):
                copy(a, 1 + j, (*chip, c), me).wait_recv()
                passed = copy(a, 4 + j, (*chip, c), sibling)
                passed.start()
                sends.append(passed)
        for a in range(n):
            copy(a, 0, sibling, me).wait_recv()
            for j, chip in enumerate(chips):
                copy(a, 4 + j, (*chip, 1 - c), me).wait_recv()
        for cp in sends:
            cp.wait_send()
        for mine in started:
            mine.wait()

    return pl.pallas_call(
        body, name="gather_weights",
        out_shape=[SDS((N_DEV,) + s.shape, BF16) for s in shards],
        in_specs=[pl.BlockSpec(memory_space=pltpu.VMEM)] * n,
        out_specs=[pl.BlockSpec(memory_space=pl.ANY)] * n,
        scratch_shapes=[pltpu.VMEM(s.shape, BF16) for s in shards] + [
            pltpu.SemaphoreType.DMA((7 * n,)), pltpu.SemaphoreType.DMA((7 * n,)),
            pltpu.SemaphoreType.DMA((n,))],
        compiler_params=pltpu.CompilerParams(vmem_limit_bytes=VMEM_LIMIT),
    )(*shards)


def _scatter_partials(parts):
    n = len(parts)

    def body(*refs):
        ins, outs = refs[:n], refs[n:2 * n]
        send_sems, recv_sems, local_sems = refs[2 * n:]
        me = _me()
        mine = []
        for a in range(n):
            cp = pltpu.make_async_copy(ins[a].at[_slot(me)], outs[a].at[_slot(me)], local_sems.at[a])
            cp.start()
            mine.append(cp)
        sends = []
        for a in range(n):
            for k in range(1, N_DEV):
                peer = _peer(me, k)
                cp = pltpu.make_async_remote_copy(
                    src_ref=ins[a].at[_slot(peer)], dst_ref=outs[a].at[_slot(me)],
                    send_sem=send_sems.at[7 * a + k - 1], recv_sem=recv_sems.at[7 * a + k - 1],
                    device_id=peer, device_id_type=MESH_ID)
                cp.start()
                sends.append(cp)
        for a in range(n):
            for k in range(1, N_DEV):
                peer = _peer(me, k)
                landed = outs[a].at[_slot(peer)]
                pltpu.make_async_remote_copy(
                    src_ref=landed, dst_ref=landed,
                    send_sem=send_sems.at[7 * a + k - 1], recv_sem=recv_sems.at[7 * a + k - 1],
                    device_id=peer, device_id_type=MESH_ID).wait_recv()
        for cp in sends:
            cp.wait_send()
        for cp in mine:
            cp.wait()

    return pl.pallas_call(
        body, name="scatter_partials",
        out_shape=[SDS(p.shape, F32) for p in parts],
        in_specs=[pl.BlockSpec(memory_space=pl.ANY)] * n,
        out_specs=[pl.BlockSpec(memory_space=pl.ANY)] * n,
        scratch_shapes=[pltpu.SemaphoreType.DMA((7 * n,)), pltpu.SemaphoreType.DMA((7 * n,)),
                        pltpu.SemaphoreType.DMA((n,))],
        compiler_params=pltpu.CompilerParams(vmem_limit_bytes=VMEM_LIMIT),
    )(*parts)


def _allreduce_small(packed, groups, chunk):
    rows = packed.shape[0]
    gc = groups * chunk

    def body(p_ref, sum_ref, db_ref, loss_ref, gath, send_sems, recv_sems):
        me = _me()
        gath[_slot(me)] = p_ref[...]
        sends = []
        for k in range(1, N_DEV):
            peer = _peer(me, k)
            cp = pltpu.make_async_remote_copy(
                src_ref=p_ref, dst_ref=gath.at[_slot(me)],
                send_sem=send_sems.at[k - 1], recv_sem=recv_sems.at[k - 1],
                device_id=peer, device_id_type=MESH_ID)
            cp.start()
            sends.append(cp)
        for k in range(1, N_DEV):
            peer = _peer(me, k)
            landed = gath.at[_slot(peer)]
            pltpu.make_async_remote_copy(
                src_ref=landed, dst_ref=landed, send_sem=send_sems.at[k - 1], recv_sem=recv_sems.at[k - 1],
                device_id=peer, device_id_type=MESH_ID).wait_recv()
        for cp in sends:
            cp.wait_send()

        row, col = _iotas(chunk)
        tril = col <= row
        for g in range(groups):
            rs = slice(g * chunk, (g + 1) * chunk)
            tot = gath[0, rs, :]
            for d in range(1, N_DEV):
                tot = tot + gath[d, rs, :]
            sum_ref[rs, :] = jnp.where(tril, tot, 0.0)
        for g in range(groups):
            rs = slice(gc + g * chunk, gc + (g + 1) * chunk)
            tot = gath[0, rs, :]
            for d in range(1, N_DEV):
                tot = tot + gath[d, rs, :]
            sum_ref[rs, :] = tot
            db_ref[g:g + 1, :] = jnp.sum(tot.T, axis=0, keepdims=True)
        rs = slice(2 * gc, rows)
        tot = gath[0, rs, :]
        for d in range(1, N_DEV):
            tot = tot + gath[d, rs, :]
        sum_ref[rs, :] = tot
        loss_ref[...] = jnp.full((SUBLANE, LANE), jnp.sum(tot[rows - 2 * gc - SUBLANE:, :]), F32)

    return pl.pallas_call(
        body, name="allreduce_small",
        out_shape=[SDS((rows, LANE), F32), SDS((groups, chunk), F32), SDS((SUBLANE, LANE), F32)],
        in_specs=[pl.BlockSpec(memory_space=pltpu.VMEM)],
        out_specs=[pl.BlockSpec(memory_space=pltpu.VMEM)] * 3,
        scratch_shapes=[pltpu.VMEM((N_DEV, rows, LANE), F32),
                        pltpu.SemaphoreType.DMA((7,)), pltpu.SemaphoreType.DMA((7,))],
        compiler_params=pltpu.CompilerParams(vmem_limit_bytes=VMEM_LIMIT),
    )(packed)


def _in_proj(x2d, norm_in, wg_in):
    n, d = x2d.shape
    nsh, _, esh = wg_in.shape
    tm = _tile(n, 1024)

    def body(x_ref, g_ref, w_ref, proj_ref, h_ref):
        @pl.when(pl.program_id(1) == 0)
        def _():
            x = x_ref[...]
            h_ref[...] = (x * _rms_scale(x) * g_ref[...]).astype(BF16)

        proj_ref[...] = _dot(h_ref[...], w_ref[0])

    return pl.pallas_call(
        body, name="in_proj", grid=(n // tm, nsh),
        in_specs=[pl.BlockSpec((tm, d), lambda i, j: (i, 0)),
                  pl.BlockSpec((1, d), lambda i, j: (0, 0)),
                  pl.BlockSpec((1, d, esh), lambda i, j: (j, 0, 0))],
        out_specs=[pl.BlockSpec((tm, esh), lambda i, j: (i, j)),
                   pl.BlockSpec((tm, d), lambda i, j: (i, 0))],
        out_shape=[SDS((n, nsh * esh), F32), SDS((n, d), BF16)],
        compiler_params=_params(("parallel", "arbitrary")),
    )(x2d, norm_in, wg_in)


def _branch_a_fwd(proj, norm_v, w_s, b_col):
    n = proj.shape[0]
    d = norm_v.shape[1]
    groups, chunk, _ = w_s.shape
    tr = _tile(n, 4 * chunk)

    def body(u_ref, v_ref, z_ref, gv_ref, ws_ref, b_ref, ya_ref, vn_s, pre_s):
        row, col = _iotas(chunk)
        tril = col <= row
        vg, _ = _gelu(v_ref[...])
        vn_s[...] = (vg * _rms_scale(vg) * gv_ref[...]).astype(BF16)
        ug, _ = _gelu(u_ref[...])
        sz, _ = _silu(z_ref[...])
        pre_s[...] = ug * sz
        for g in range(groups):
            wm = jnp.where(tril, ws_ref[g], 0.0).astype(BF16)
            cs = slice(g * chunk, (g + 1) * chunk)
            for c in range(tr // chunk):
                rs = slice(c * chunk, (c + 1) * chunk)
                mixed = _dot(wm, vn_s[rs, cs]) + b_ref[g]
                ya_ref[rs, cs] = (pre_s[rs, cs] * mixed).astype(BF16)

    seg = lambda k: pl.BlockSpec((tr, d), lambda i: (i, k))
    return pl.pallas_call(
        body, name="branch_a_fwd", grid=(n // tr,),
        in_specs=[seg(0), seg(1), seg(2),
                  pl.BlockSpec((1, d), lambda i: (0, 0)),
                  pl.BlockSpec((groups, chunk, chunk), lambda i: (0, 0, 0)),
                  pl.BlockSpec((groups, chunk, 1), lambda i: (0, 0, 0))],
        out_specs=pl.BlockSpec((tr, d), lambda i: (i, 0)),
        out_shape=SDS((n, d), BF16),
        scratch_shapes=[pltpu.VMEM((tr, d), BF16), pltpu.VMEM((tr, d), F32)],
        compiler_params=_params(("parallel",)),
    )(proj, proj, proj, norm_v, w_s, b_col)


def _sb_fwd(proj, batch, seq, d, hd):
    heads = d // hd
    t = _tile(seq, SB_TILE)
    scale = hd ** -0.5
    nblk = seq // t
    nh = SB_HEADS
    wide = nh * hd

    def body(q_ref, k_ref, v_ref, zb_ref, yb_ref, o_ref, tot_ref, qs, ks, vs, later):
        qs[...] = q_ref[...].astype(BF16)
        ks[...] = k_ref[...].astype(BF16)
        vs[...] = v_ref[...].astype(BF16)
        row, col = _iotas(t)
        later[...] = (row > col).astype(BF16)

        def qblock(i, carry):
            r0 = pl.multiple_of(i * t, t)

            def tile(j, state, valid):
                c0 = pl.multiple_of(j * t, t)
                cols = [slice(hh * hd, (hh + 1) * hd) for hh in range(nh)]
                scores = [_sb_scores(qs[pl.ds(r0, t), cs], ks[pl.ds(c0, t), cs], scale, valid) for cs in cols]
                sums = [_tri_sum(log_rest, later[...]) for _, log_rest in scores]
                ws = []
                for hh in range(nh):
                    w = jnp.exp(scores[hh][0] + sums[hh] + state[hh][1])
                    if valid is not None:
                        w = jnp.where(valid, w, 0.0)
                    ws.append(w.astype(BF16))
                pv = [_dot(ws[hh], vs[pl.ds(c0, t), cols[hh]]) for hh in range(nh)]
                return tuple((state[hh][0] + pv[hh], state[hh][1] + jnp.sum(scores[hh][1], axis=-1, keepdims=True))
                             for hh in range(nh))

            causal = col < row
            zero = (jnp.zeros((t, hd), F32), jnp.zeros((t, 1), F32))
            state = tile(i, (zero,) * nh, causal)
            state = lax.fori_loop(0, i, lambda jj, st: tile(i - 1 - jj, st, None), state)
            for hh in range(nh):
                acc, run = state[hh]
                cs = slice(hh * hd, (hh + 1) * hd)
                o_ref[pl.ds(r0, t), cs] = acc
                tot_ref[hh, pl.ds(r0, t), :] = run
                sz, _ = _silu(zb_ref[pl.ds(r0, t), cs])
                yb_ref[pl.ds(r0, t), cs] = (acc * sz).astype(BF16)
            return carry

        lax.fori_loop(0, nblk, qblock, 0)

    col0 = d // wide
    seg = lambda k: pl.BlockSpec((seq, wide), lambda b, h: (b, k * col0 + h))
    return pl.pallas_call(
        body, name="sb_fwd", grid=(batch, heads // nh),
        in_specs=[seg(3), seg(4), seg(5), seg(6)],
        out_specs=[pl.BlockSpec((seq, wide), lambda b, h: (b, h))] * 2 + [
            pl.BlockSpec((nh, seq, 1), lambda b, h: (b * (heads // nh) + h, 0, 0))],
        out_shape=[SDS((batch * seq, d), BF16), SDS((batch * seq, d), F32), SDS((batch * heads, seq, 1), F32)],
        scratch_shapes=[pltpu.VMEM((seq, wide), BF16)] * 3 + [pltpu.VMEM((t, t), BF16)],
        compiler_params=_params(("parallel", "parallel")),
    )(proj, proj, proj, proj)


def _tail(x2d, tgt, ya, yb, proj, w_oa, w_ob, w_out, norm_final):
    n, d = x2d.shape
    e = proj.shape[1]
    tm = _tile(n, 256)

    def body(x_ref, t_ref, ya_ref, yb_ref, ga_ref, gb_ref, woa_ref, wob_ref, wout_ref, gf_ref,
             dproj_ref, dx2_ref, dya_ref, dyb_ref, mrg_ref, dpa_ref, dpb_ref, loss_ref, dgf_ref, dgb_s):
        i, kk = pl.program_id(0), pl.program_id(1)

        @pl.when(jnp.logical_and(i == 0, kk == 0))
        def _():
            loss_ref[...] = jnp.zeros_like(loss_ref)
            dgf_ref[...] = jnp.zeros_like(dgf_ref)

        @pl.when(kk == 0)
        def _():
            pa = _dot(ya_ref[...], woa_ref[...])
            pb = _dot(yb_ref[...], wob_ref[...])
            sa = _sigmoid(ga_ref[...])
            sb = _sigmoid(gb_ref[...])
            merged = (sa * pa + sb * pb).astype(BF16)
            mrg_ref[...] = merged
            x2 = x_ref[...] + _dot(merged, wout_ref[...])
            r2 = _rms_scale(x2)
            xh = x2 * r2
            gf = gf_ref[...]
            diff = xh * gf - t_ref[...]
            loss_ref[...] += jnp.sum(diff * diff, axis=0, keepdims=True) * (0.5 / d)
            dy = diff * (1.0 / d)
            dgf_ref[...] += jnp.sum(dy * xh, axis=0, keepdims=True)
            dxh = dy * gf
            dx2 = r2 * (dxh - xh * jnp.mean(dxh * xh, axis=-1, keepdims=True))
            dx2_ref[...] = dx2
            dm = _dot_nt(dx2.astype(BF16), wout_ref[...])
            dpa = (dm * sa).astype(BF16)
            dpb = (dm * sb).astype(BF16)
            dpa_ref[...] = dpa
            dpb_ref[...] = dpb
            dproj_ref[...] = (dm * pa * (sa * (1.0 - sa))).astype(BF16)
            dgb_s[...] = (dm * pb * (sb * (1.0 - sb))).astype(BF16)
            dya_ref[...] = _dot_nt(dpa, woa_ref[...])
            dyb_ref[...] = _dot_nt(dpb, wob_ref[...])

        @pl.when(kk == 1)
        def _():
            dproj_ref[...] = dgb_s[...]

    rows = lambda k=0: pl.BlockSpec((tm, d), lambda i, kk: (i, k))
    full = pl.BlockSpec((d, d), lambda i, kk: (0, 0))
    vec = pl.BlockSpec((1, d), lambda i, kk: (0, 0))
    return pl.pallas_call(
        body, name="tail", grid=(n // tm, 2),
        in_specs=[rows(), rows(), rows(), rows(), rows(7), rows(8), full, full, full, vec],
        out_specs=[pl.BlockSpec((tm, d), lambda i, kk: (i, 7 + kk)),
                   rows(), rows(), rows(), rows(), rows(), rows(), vec, vec],
        out_shape=[SDS((n, e), BF16), SDS((n, d), F32), SDS((n, d), F32), SDS((n, d), F32),
                   SDS((n, d), BF16), SDS((n, d), BF16), SDS((n, d), BF16),
                   SDS((1, d), F32), SDS((1, d), F32)],
        scratch_shapes=[pltpu.VMEM((tm, d), BF16)],
        compiler_params=_params(("arbitrary", "arbitrary")),
    )(x2d, tgt, ya, yb, proj, proj, w_oa, w_ob, w_out, norm_final)


def _tn_matmul(a, b, name):
    n, p = a.shape
    q = b.shape[1]
    tk = _tile(n, 512)

    def body(a_ref, b_ref, o_ref):
        @pl.when(pl.program_id(0) == 0)
        def _():
            o_ref[...] = jnp.zeros_like(o_ref)

        o_ref[...] += _dot_tn(a_ref[...], b_ref[...].astype(BF16))

    return pl.pallas_call(
        body, name=name, grid=(n // tk,),
        in_specs=[pl.BlockSpec((tk, p), lambda k: (k, 0)), pl.BlockSpec((tk, q), lambda k: (k, 0))],
        out_specs=pl.BlockSpec((p, q), lambda k: (0, 0)),
        out_shape=SDS((p, q), F32),
        compiler_params=_params(("arbitrary",)),
    )(a, b)


def _sb_bwd(proj, o, dyb, tot, dproj, batch, seq, d, hd):
    heads = d // hd
    t = _tile(seq, SB_TILE)
    scale = hd ** -0.5
    nblk = seq // t
    nh = SB_HEADS
    wide = nh * hd

    def compute(q_ref, k_ref, v_ref, zb_ref, o_ref, dyb_ref, tot_ref, qs, ks, vs, dos, res, upto, before):
        qs[...] = q_ref[...].astype(BF16)
        ks[...] = k_ref[...].astype(BF16)
        vs[...] = v_ref[...].astype(BF16)
        sz, dsz = _silu(zb_ref[...])
        dyb_v = dyb_ref[...]
        dos[...] = (dyb_v * sz).astype(BF16)
        res[3] = dyb_v * o_ref[...] * dsz
        res[1] = jnp.zeros((seq, wide), F32)
        res[2] = jnp.zeros((seq, wide), F32)
        row, col = _iotas(t)
        upto[...] = (row <= col).astype(BF16)
        before[...] = (row < col).astype(BF16)

        def qblock(i, carry):
            r0 = pl.multiple_of(i * t, t)

            def tile(j, state, valid):
                c0 = pl.multiple_of(j * t, t)
                hs = range(nh)
                cols = [slice(hh * hd, (hh + 1) * hd) for hh in hs]
                q_i = [qs[pl.ds(r0, t), cs] for cs in cols]
                k_j = [ks[pl.ds(c0, t), cs] for cs in cols]
                do_i = [dos[pl.ds(r0, t), cs] for cs in cols]
                scores = [_sb_scores(q_i[hh], k_j[hh], scale, valid) for hh in hs]
                dw = [_dot_nt(do_i[hh], vs[pl.ds(c0, t), cols[hh]]) for hh in hs]
                sums = [_tri_sum(scores[hh][1], upto[...]) for hh in hs]
                ws, gs = [], []
                for hh in hs:
                    left = tot_ref[hh, pl.ds(r0, t), :] - state[hh][1]
                    w = jnp.exp(scores[hh][0] + (left - sums[hh]))
                    if valid is not None:
                        w = jnp.where(valid, w, 0.0)
                    ws.append(w.astype(BF16))
                    gs.append(dw[hh] * w)
                gsums = [_tri_sum(gs[hh], before[...]) for hh in hs]
                dzs = []
                for hh in hs:
                    beta = jnp.exp(scores[hh][0])
                    dz = (gs[hh] * (1.0 - beta) - (gsums[hh] + state[hh][2]) * beta) * scale
                    if valid is not None:
                        dz = jnp.where(valid, dz, 0.0)
                    dzs.append(dz.astype(BF16))
                for hh in hs:
                    res[2, pl.ds(c0, t), cols[hh]] += _dot_tn(ws[hh], do_i[hh])
                for hh in hs:
                    res[1, pl.ds(c0, t), cols[hh]] += _dot_tn(dzs[hh], q_i[hh])
                dqs = [_dot(dzs[hh], k_j[hh]) for hh in hs]
                return tuple((state[hh][0] + dqs[hh],
                              state[hh][1] + jnp.sum(scores[hh][1], axis=-1, keepdims=True),
                              state[hh][2] + jnp.sum(gs[hh], axis=-1, keepdims=True)) for hh in hs)

            zero = jnp.zeros((t, 1), F32)
            state = lax.fori_loop(0, i, lambda j, st: tile(j, st, None),
                                  ((jnp.zeros((t, hd), F32), zero, zero),) * nh)
            state = tile(i, state, col < row)
            for hh in range(nh):
                res[0, pl.ds(r0, t), hh * hd:(hh + 1) * hd] = state[hh][0]
            return carry

        lax.fori_loop(0, nblk, qblock, 0)

    def body(q_ref, k_ref, v_ref, zb_ref, o_ref, dyb_ref, tot_ref, dproj_in, out_ref,
             qs, ks, vs, dos, res, upto, before):
        del dproj_in
        kk = pl.program_id(2)

        @pl.when(kk == 0)
        def _():
            compute(q_ref, k_ref, v_ref, zb_ref, o_ref, dyb_ref, tot_ref, qs, ks, vs, dos, res, upto, before)

        out_ref[...] = res[kk].astype(BF16)

    col0 = d // wide
    seg = lambda k: pl.BlockSpec((seq, wide), lambda b, h, kk: (b, k * col0 + h))
    head = pl.BlockSpec((seq, wide), lambda b, h, kk: (b, h))
    return pl.pallas_call(
        body, name="sb_bwd", grid=(batch, heads // nh, 4),
        in_specs=[seg(3), seg(4), seg(5), seg(6), head, head,
                  pl.BlockSpec((nh, seq, 1), lambda b, h, kk: (b * (heads // nh) + h, 0, 0)),
                  pl.BlockSpec(memory_space=pl.ANY)],
        out_specs=pl.BlockSpec((seq, wide), lambda b, h, kk: (b, (3 + kk) * col0 + h)),
        out_shape=SDS(dproj.shape, dproj.dtype),
        input_output_aliases={7: 0},
        scratch_shapes=[pltpu.VMEM((seq, wide), BF16)] * 4 + [pltpu.VMEM((4, seq, wide), F32)] + [
            pltpu.VMEM((t, t), BF16)] * 2,
        compiler_params=_params(("arbitrary", "arbitrary", "arbitrary")),
    )(proj, proj, proj, proj, o, dyb, tot, dproj)


def _branch_a_bwd(proj, dya, norm_v, w_s, b_col, dproj):
    n = proj.shape[0]
    d = norm_v.shape[1]
    groups, chunk, _ = w_s.shape
    tr = _tile(n, 2 * chunk)

    def body(u_ref, v_ref, z_ref, dya_ref, gv_ref, ws_ref, b_ref, dproj_in,
             out_ref, dws_ref, db_ref, dgv_ref, vn_s, dmix_s, dvn_s):
        del dproj_in

        @pl.when(pl.program_id(0) == 0)
        def _():
            dws_ref[...] = jnp.zeros_like(dws_ref)
            db_ref[...] = jnp.zeros_like(db_ref)
            dgv_ref[...] = jnp.zeros_like(dgv_ref)

        row, col = _iotas(chunk)
        tril = col <= row
        u, v, z, dya_v = u_ref[...], v_ref[...], z_ref[...], dya_ref[...]
        gv = gv_ref[...]
        vg, dvg_dv = _gelu(v)
        r = _rms_scale(vg)
        vh = vg * r
        vn_s[...] = (vh * gv).astype(BF16)
        ug, dug_du = _gelu(u)
        sz, dsz = _silu(z)
        dmix_s[...] = dya_v * ug * sz
        for g in range(groups):
            wm = jnp.where(tril, ws_ref[g], 0.0).astype(BF16)
            cs = slice(g * chunk, (g + 1) * chunk)
            for c in range(tr // chunk):
                rs = slice(c * chunk, (c + 1) * chunk)
                vn = vn_s[rs, cs]
                mixed = _dot(wm, vn) + b_ref[g]
                dmix = dmix_s[rs, cs]
                dmix16 = dmix.astype(BF16)
                dws_ref[g] += _dot_nt(dmix16, vn)
                db_ref[g] += dmix
                dvn_s[rs, cs] = _dot_tn(wm, dmix16)
                t_u = dya_v[rs, cs] * mixed
                out_ref[rs, g * chunk:(g + 1) * chunk] = (t_u * sz[rs, cs] * dug_du[rs, cs]).astype(BF16)
                out_ref[rs, 2 * d + g * chunk:2 * d + (g + 1) * chunk] = (t_u * ug[rs, cs] * dsz[rs, cs]).astype(BF16)
        dvn = dvn_s[...]
        dgv_ref[...] += jnp.sum(dvn * vh, axis=0, keepdims=True)
        dvh = dvn * gv
        dvg = r * (dvh - vh * jnp.mean(dvh * vh, axis=-1, keepdims=True))
        out_ref[:, d:2 * d] = (dvg * dvg_dv).astype(BF16)

    seg = lambda k: pl.BlockSpec((tr, d), lambda i: (i, k))
    return pl.pallas_call(
        body, name="branch_a_bwd", grid=(n // tr,),
        in_specs=[seg(0), seg(1), seg(2), seg(0),
                  pl.BlockSpec((1, d), lambda i: (0, 0)),
                  pl.BlockSpec((groups, chunk, chunk), lambda i: (0, 0, 0)),
                  pl.BlockSpec((groups, chunk, 1), lambda i: (0, 0, 0)),
                  pl.BlockSpec(memory_space=pl.ANY)],
        out_specs=[pl.BlockSpec((tr, 3 * d), lambda i: (i, 0)),
                   pl.BlockSpec((groups, chunk, chunk), lambda i: (0, 0, 0)),
                   pl.BlockSpec((groups, chunk, chunk), lambda i: (0, 0, 0)),
                   pl.BlockSpec((1, d), lambda i: (0, 0))],
        out_shape=[SDS(dproj.shape, dproj.dtype), SDS((groups, chunk, chunk), F32),
                   SDS((groups, chunk, chunk), F32), SDS((1, d), F32)],
        input_output_aliases={7: 0},
        scratch_shapes=[pltpu.VMEM((tr, d), BF16), pltpu.VMEM((tr, d), F32), pltpu.VMEM((tr, d), F32)],
        compiler_params=_params(("arbitrary",)),
    )(proj, proj, proj, dya, norm_v, w_s, b_col, dproj)


def _dx(dproj, wg_in, x2d, dx2, norm_in):
    n, d = x2d.shape
    nsh, _, esh = wg_in.shape
    tm = _tile(n, 1024)

    def body(dp_ref, w_ref, x_ref, dx2_ref, g_ref, gx_ref, dg_ref, acc):
        i, k = pl.program_id(0), pl.program_id(1)

        @pl.when(jnp.logical_and(i == 0, k == 0))
        def _():
            dg_ref[...] = jnp.zeros_like(dg_ref)

        @pl.when(k == 0)
        def _():
            acc[...] = jnp.zeros_like(acc)

        acc[...] += _dot_nt(dp_ref[...], w_ref[0])

        @pl.when(k == nsh - 1)
        def _():
            dh = acc[...]
            x = x_ref[...]
            r = _rms_scale(x)
            xh = x * r
            dg_ref[...] += jnp.sum(dh * xh, axis=0, keepdims=True)
            dxh = dh * g_ref[...]
            gx_ref[...] = dx2_ref[...] + r * (dxh - xh * jnp.mean(dxh * xh, axis=-1, keepdims=True))

    rows = pl.BlockSpec((tm, d), lambda i, k: (i, 0))
    vec = pl.BlockSpec((1, d), lambda i, k: (0, 0))
    return pl.pallas_call(
        body, name="dx", grid=(n // tm, nsh),
        in_specs=[pl.BlockSpec((tm, esh), lambda i, k: (i, k)),
                  pl.BlockSpec((1, d, esh), lambda i, k: (k, 0, 0)), rows, rows, vec],
        out_specs=[rows, vec],
        out_shape=[SDS((n, d), F32), SDS((1, d), F32)],
        scratch_shapes=[pltpu.VMEM((tm, d), F32)],
        compiler_params=_params(("arbitrary", "arbitrary")),
    )(dproj, wg_in, x2d, dx2, norm_in)


def _dw_in(h, dproj, nsh):
    n, d = h.shape
    esh = dproj.shape[1] // nsh
    tk = _tile(n, 512)

    def body(h_ref, dp_ref, o_ref):
        @pl.when(pl.program_id(1) == 0)
        def _():
            o_ref[...] = jnp.zeros_like(o_ref)

        o_ref[0] += _dot_tn(h_ref[...], dp_ref[...])

    return pl.pallas_call(
        body, name="dw_in", grid=(nsh, n // tk),
        in_specs=[pl.BlockSpec((tk, d), lambda j, k: (k, 0)), pl.BlockSpec((tk, esh), lambda j, k: (k, j))],
        out_specs=pl.BlockSpec((1, d, esh), lambda j, k: (j, 0, 0)),
        out_shape=SDS((nsh, d, esh), F32),
        compiler_params=_params(("parallel", "arbitrary")),
    )(h, dproj)


def _adamw_outputs(g_ref, d_ref, m_ref, v_ref, g, w, m, v):
    delta, m2, v2 = _adamw(w, g, m, v)
    g_ref[...] = g
    d_ref[...] = delta
    m_ref[...] = m2
    v_ref[...] = v2


def _reduce_adamw(slots, w, m, v, name):
    _, r, c = slots.shape
    tr = _tile(r, 128)

    def body(s_ref, w_ref, m_ref, v_ref, g_out, d_out, m_out, v_out):
        g = s_ref[0]
        for k in range(1, N_DEV):
            g = g + s_ref[k]
        _adamw_outputs(g_out, d_out, m_out, v_out, g, w_ref[...], m_ref[...], v_ref[...])

    blk = pl.BlockSpec((tr, c), lambda i: (i, 0))
    return pl.pallas_call(
        body, name=name, grid=(r // tr,),
        in_specs=[pl.BlockSpec((N_DEV, tr, c), lambda i: (0, i, 0)), blk, blk, blk],
        out_specs=[blk] * 4,
        out_shape=[SDS((r, c), F32)] * 4,
        compiler_params=_params(("parallel",)),
    )(slots, w, m, v)


def _adamw_small(g, w, m, v, name):
    def body(g_ref, w_ref, m_ref, v_ref, g_out, d_out, m_out, v_out):
        _adamw_outputs(g_out, d_out, m_out, v_out, g_ref[...], w_ref[...], m_ref[...], v_ref[...])

    return pl.pallas_call(
        body, name=name,
        out_shape=[SDS(g.shape, F32)] * 4,
        in_specs=[pl.BlockSpec(memory_space=pltpu.VMEM)] * 4,
        out_specs=[pl.BlockSpec(memory_space=pltpu.VMEM)] * 4,
    )(g, w, m, v)


def kernel(x, norm_in, w_in, norm_v, w_s, b_s, w_o_gmlp, w_o_sb, w_out, norm_final, loss_target, m_norm_in, m_w_in, m_norm_v, m_w_s, m_b_s, m_w_o_gmlp, m_w_o_sb, m_w_out, m_norm_final, v_norm_in, v_w_in, v_norm_v, v_w_s, v_b_s, v_w_o_gmlp, v_w_o_sb, v_w_out, v_norm_final):
    batch, seq, d = x.shape
    n = batch * seq
    groups, chunk = w_s.shape[1], w_s.shape[2]
    hd = LANE
    x2d = x.reshape(n, d)
    tgt = loss_target.reshape(n, d)
    b_col = b_s[0].reshape(groups, chunk, 1)
    norm_final2 = norm_final.reshape(1, d)

    wg_in, wg_oa, wg_ob, wg_out = _gather_weights([w_in[0], w_o_gmlp[0], w_o_sb[0], w_out[0]])
    rsh = wg_oa.shape[1]
    wf_oa, wf_ob, wf_out = (w.reshape(N_DEV * rsh, d) for w in (wg_oa, wg_ob, wg_out))

    proj, h = _in_proj(x2d, norm_in, wg_in)
    ya = _branch_a_fwd(proj, norm_v, w_s[0], b_col)
    yb, o, sb_tot = _sb_fwd(proj, batch, seq, d, hd)
    dproj, dx2, dya, dyb, merged, dpa, dpb, loss_vec, dgf = _tail(
        x2d, tgt, ya, yb, proj, wf_oa, wf_ob, wf_out, norm_final2)
    gp_oa = _tn_matmul(ya, dpa, "dw_o_gmlp")
    gp_ob = _tn_matmul(yb, dpb, "dw_o_sb")
    gp_out = _tn_matmul(merged, dx2, "dw_out")
    dproj = _sb_bwd(proj, o, dyb, sb_tot, dproj, batch, seq, d, hd)
    dproj, gp_ws, gp_b, gp_nv = _branch_a_bwd(proj, dya, norm_v, w_s[0], b_col, dproj)
    grad_x, gp_nin = _dx(dproj, wg_in, x2d, dx2, norm_in)
    gp_win = _dw_in(h, dproj, N_DEV)

    s_win, s_oa, s_ob, s_out = _scatter_partials(
        [gp_win] + [g.reshape(N_DEV, rsh, d) for g in (gp_oa, gp_ob, gp_out)])

    slab = lambda a: a.reshape(d // LANE, LANE)
    gc = groups * chunk
    packed = jnp.concatenate(
        [gp_ws.reshape(gc, chunk), gp_b.reshape(gc, chunk), slab(gp_nin), slab(gp_nv), slab(dgf), slab(loss_vec)],
        axis=0)
    tot, g_b, loss_slab = _allreduce_small(packed, groups, chunk)
    ns = d // LANE
    g_ws = tot[:gc]
    g_nin, g_nv, g_nf = (tot[2 * gc + k * ns:2 * gc + (k + 1) * ns] for k in range(3))
    loss = loss_slab[0, 0]

    res = {}
    res["w_in"] = _reduce_adamw(s_win, w_in[0], m_w_in[0], v_w_in[0], "adamw_w_in")
    res["w_o_gmlp"] = _reduce_adamw(s_oa, w_o_gmlp[0], m_w_o_gmlp[0], v_w_o_gmlp[0], "adamw_w_o_gmlp")
    res["w_o_sb"] = _reduce_adamw(s_ob, w_o_sb[0], m_w_o_sb[0], v_w_o_sb[0], "adamw_w_o_sb")
    res["w_out"] = _reduce_adamw(s_out, w_out[0], m_w_out[0], v_w_out[0], "adamw_w_out")
    res["norm_in"] = _adamw_small(g_nin, slab(norm_in), slab(m_norm_in), slab(v_norm_in), "adamw_norm_in")
    res["norm_v"] = _adamw_small(g_nv, slab(norm_v), slab(m_norm_v), slab(v_norm_v), "adamw_norm_v")
    res["norm_final"] = _adamw_small(g_nf, slab(norm_final), slab(m_norm_final), slab(v_norm_final), "adamw_norm_final")
    res["w_s"] = _adamw_small(g_ws, w_s.reshape(gc, chunk), m_w_s.reshape(gc, chunk), v_w_s.reshape(gc, chunk), "adamw_w_s")
    res["b_s"] = _adamw_small(g_b, b_s[0], m_b_s[0], v_b_s[0], "adamw_b_s")

    shapes = {"norm_in": norm_in.shape, "w_in": w_in.shape, "norm_v": norm_v.shape, "w_s": w_s.shape,
              "b_s": b_s.shape, "w_o_gmlp": w_o_gmlp.shape, "w_o_sb": w_o_sb.shape, "w_out": w_out.shape,
              "norm_final": norm_final.shape}
    names = list(shapes)
    outs = [loss, grad_x.reshape(batch, seq, d)]
    for kind in range(4):
        outs += [res[name][kind].reshape(shapes[name]) for name in names]
    return tuple(outs)
```

```python
import functools
import math

import jax
import jax.numpy as jnp
from jax import lax
from jax.experimental import pallas as pl
from jax.experimental.pallas import tpu as pltpu

F32 = jnp.float32
BF16 = jnp.bfloat16
SDS = jax.ShapeDtypeStruct
MESH_ID = pl.DeviceIdType.MESH

N_DEV = 8
LANE = 128
SUBLANE = 8
VMEM_LIMIT = 56 * 1024 * 1024
SB_TILE = 256
SB_HEADS = 2
RMS_EPS = 1e-6

ADAM_LR = 0.001
ADAM_B1 = 0.9
ADAM_B2 = 0.999
ADAM_EPS = 1e-08
ADAM_WD = 0.01
ADAM_STEP = 10

NT_DIMS = (((1,), (1,)), ((), ()))
TN_DIMS = (((0,), (0,)), ((), ()))


def _params(semantics=None):
    return pltpu.CompilerParams(dimension_semantics=semantics, vmem_limit_bytes=VMEM_LIMIT)


def _tile(n, preferred):
    t = min(n, preferred)
    assert n % t == 0, (n, t)
    return t


def _sigmoid(x):
    return 1.0 / (1.0 + jnp.exp(-x))


def _silu(x):
    s = _sigmoid(x)
    return x * s, s * (1.0 + x * (1.0 - s))


def _gelu(x):
    k = math.sqrt(2.0 / math.pi)
    x2 = x * x
    t = jnp.tanh(k * (x + 0.044715 * (x * x2)))
    cdf = 0.5 * (1.0 + t)
    return x * cdf, cdf + 0.5 * x * (1.0 - t * t) * (k * (1.0 + 3.0 * 0.044715 * x2))


def _rms_scale(x):
    return lax.rsqrt(jnp.mean(x * x, axis=-1, keepdims=True) + RMS_EPS)


def _iotas(n):
    return (lax.broadcasted_iota(jnp.int32, (n, n), 0), lax.broadcasted_iota(jnp.int32, (n, n), 1))


def _adamw(w, g, m, v):
    m = ADAM_B1 * m + (1.0 - ADAM_B1) * g
    v = ADAM_B2 * v + (1.0 - ADAM_B2) * (g * g)
    m_hat = m / (1.0 - ADAM_B1 ** ADAM_STEP)
    v_hat = v / (1.0 - ADAM_B2 ** ADAM_STEP)
    delta = -ADAM_LR * (m_hat / (jnp.sqrt(v_hat) + ADAM_EPS) + ADAM_WD * w)
    return delta, m, v


def _dot(a, b):
    return jnp.dot(a, b, preferred_element_type=F32)


def _dot_nt(a, b):
    return lax.dot_general(a, b, NT_DIMS, preferred_element_type=F32)


def _dot_tn(a, b):
    return lax.dot_general(a, b, TN_DIMS, preferred_element_type=F32)


def _tri_sum(x, tri):
    t = x.shape[0]
    hi = x.astype(BF16)
    lo = (x - hi.astype(F32)).astype(BF16)
    s = _dot(jnp.concatenate([hi, lo], axis=0), tri)
    return s[:t] + s[t:]


def _sb_scores(q_i, k_j, scale, valid):
    z = _dot_nt(q_i, k_j) * scale
    log_beta = jnp.minimum(z, 0.0) - jnp.log(1.0 + jnp.exp(-jnp.abs(z)))
    log_rest = log_beta - z
    if valid is not None:
        log_rest = jnp.where(valid, log_rest, 0.0)
    return log_beta, log_rest


def _me():
    return lax.axis_index("x"), lax.axis_index("y"), lax.axis_index("c")


def _slot(p):
    return 4 * p[0] + 2 * p[1] + p[2]


def _peer(me, k):
    flips = ((k >> 2) & 1, (k >> 1) & 1, k & 1)
    return tuple(1 - a if f else a for a, f in zip(me, flips))


def _gather_weights(shards):
    n = len(shards)

    def body(*refs):
        ins, outs, stage = refs[:n], refs[n:2 * n], refs[2 * n:3 * n]
        send_sems, recv_sems, local_sems = refs[3 * n:]
        x, y, c = _me()
        me, sibling = (x, y, c), (x, y, 1 - c)
        chips = [(1 - x, y), (x, 1 - y), (1 - x, 1 - y)]

        def copy(a, k, block, to, src=None):
            dst = outs[a].at[_slot(block)]
            return pltpu.make_async_remote_copy(
                src_ref=dst if src is None else src, dst_ref=dst,
                send_sem=send_sems.at[7 * a + k], recv_sem=recv_sems.at[7 * a + k],
                device_id=to, device_id_type=MESH_ID)

        started = []
        for a in range(n):
            stage[a][...] = ins[a][...].astype(BF16)
            mine = pltpu.make_async_copy(stage[a], outs[a].at[_slot(me)], local_sems.at[a])
            mine.start()
            started.append(mine)
        sends = []
        for a in range(n):
            sends.append(copy(a, 0, me, sibling, src=stage[a]))
            sends += [copy(a, 1 + j, me, (*chip, c), src=stage[a]) for j, chip in enumerate(chips)]
        for cp in sends:
            cp.start()
        for a in range(n):
            for j, chip in enumerate(chips):
                copy(a, 1 + j, (*chip, c), me).wait_recv()
                passed = copy(a, 4 + j, (*chip, c), sibling)
                passed.start()
                sends.append(passed)
        for a in range(n):
            copy(a, 0, sibling, me).wait_recv()
            for j, chip in enumerate(chips):
                copy(a, 4 + j, (*chip, 1 - c), me).wait_recv()
        for cp in sends:
            cp.wait_send()
        for mine in started:
            mine.wait()

    return pl.pallas_call(
        body, name="gather_weights",
        out_shape=[SDS((N_DEV,) + s.shape, BF16) for s in shards],
        in_specs=[pl.BlockSpec(memory_space=pltpu.VMEM)] * n,
        out_specs=[pl.BlockSpec(memory_space=pl.ANY)] * n,
        scratch_shapes=[pltpu.VMEM(s.shape, BF16) for s in shards] + [
            pltpu.SemaphoreType.DMA((7 * n,)), pltpu.SemaphoreType.DMA((7 * n,)),
            pltpu.SemaphoreType.DMA((n,))],
        compiler_params=pltpu.CompilerParams(vmem_limit_bytes=VMEM_LIMIT),
    )(*shards)


def _dw_in_exchange(h, dproj, my_slot, stacks, packed):
    n, d = h.shape
    esh = dproj.shape[1] // N_DEV
    tk = _tile(n, 512)
    nk = n // tk
    ns = len(stacks)
    last_j = N_DEV - 1

    def body(me_ref, h_ref, dp_ref, *refs):
        del me_ref
        st_in, pk_in = refs[:ns], refs[ns]
        win_out, st_out, pk_out = refs[ns + 1], refs[ns + 2:2 * ns + 2], refs[2 * ns + 2]
        acc, sendbuf, win_send, win_recv, send_sems, recv_sems, local_sems = refs[2 * ns + 3:]
        j, k = pl.program_id(0), pl.program_id(1)
        me = _me()
        mine = _slot(me)

        def ready_copies():
            local = [pltpu.make_async_copy(st_in[a].at[mine], st_out[a].at[mine], local_sems.at[a])
                     for a in range(ns)]
            local.append(pltpu.make_async_copy(pk_in, pk_out.at[mine], local_sems.at[ns]))
            remote = []
            for kk in range(1, N_DEV):
                peer = _peer(me, kk)
                for a in range(ns):
                    remote.append(pltpu.make_async_remote_copy(
                        src_ref=st_in[a].at[_slot(peer)], dst_ref=st_out[a].at[mine],
                        send_sem=send_sems.at[(ns + 1) * (kk - 1) + a],
                        recv_sem=recv_sems.at[(ns + 1) * (kk - 1) + a],
                        device_id=peer, device_id_type=MESH_ID))
                remote.append(pltpu.make_async_remote_copy(
                    src_ref=pk_in, dst_ref=pk_out.at[mine],
                    send_sem=send_sems.at[(ns + 1) * (kk - 1) + ns],
                    recv_sem=recv_sems.at[(ns + 1) * (kk - 1) + ns],
                    device_id=peer, device_id_type=MESH_ID))
            return local, remote

        def shard_copy(jj):
            owner = (mine + 1 + jj) % N_DEV
            return pltpu.make_async_remote_copy(
                src_ref=sendbuf.at[jj % 2], dst_ref=win_out.at[mine],
                send_sem=win_send.at[jj % 2], recv_sem=win_recv.at[mine],
                device_id=(owner // 4, (owner // 2) % 2, owner % 2), device_id_type=MESH_ID)

        def own_copy():
            return pltpu.make_async_copy(sendbuf.at[last_j % 2], win_out.at[mine], local_sems.at[ns + 1])

        @pl.when(jnp.logical_and(j == 0, k == 0))
        def _():
            local, remote = ready_copies()
            for cp in local + remote:
                cp.start()

        @pl.when(k == 0)
        def _():
            acc[...] = jnp.zeros_like(acc)

        acc[...] += _dot_tn(h_ref[...], dp_ref[...])

        @pl.when(k == nk - 1)
        def _():
            @pl.when(j >= 2)
            def _():
                shard_copy(j - 2).wait_send()

            sendbuf[j % 2] = acc[...].astype(BF16)

            @pl.when(j < last_j)
            def _():
                shard_copy(j).start()

            @pl.when(j == last_j)
            def _():
                own_copy().start()
                shard_copy(last_j - 1).wait_send()
                own_copy().wait()
                for src in range(N_DEV):
                    @pl.when(src != mine)
                    def _():
                        landed = win_out.at[src]
                        pltpu.make_async_remote_copy(
                            src_ref=landed, dst_ref=landed, send_sem=win_send.at[0], recv_sem=win_recv.at[src],
                            device_id=me, device_id_type=MESH_ID).wait_recv()
                local, remote = ready_copies()
                for cp in remote:
                    cp.wait_send()
                idx = 0
                for kk in range(1, N_DEV):
                    peer = _slot(_peer(me, kk))
                    for a in range(ns + 1):
                        landed = pk_out.at[peer] if a == ns else st_out[a].at[peer]
                        pltpu.make_async_remote_copy(
                            src_ref=landed, dst_ref=landed, send_sem=send_sems.at[idx], recv_sem=recv_sems.at[idx],
                            device_id=me, device_id_type=MESH_ID).wait_recv()
                        idx += 1
                for cp in local:
                    cp.wait()

    any_spec = pl.BlockSpec(memory_space=pl.ANY)
    n_ready = 7 * (ns + 1)
    grid_spec = pltpu.PrefetchScalarGridSpec(
        num_scalar_prefetch=1, grid=(N_DEV, nk),
        in_specs=[pl.BlockSpec((tk, d), lambda j, k, me: (k, 0)),
                  pl.BlockSpec((tk, esh), lambda j, k, me: (k, (me[0] + 1 + j) % N_DEV))] + [any_spec] * (ns + 1),
        out_specs=[any_spec] * (ns + 2),
        scratch_shapes=[pltpu.VMEM((d, esh), F32), pltpu.VMEM((2, d, esh), BF16),
                        pltpu.SemaphoreType.DMA((2,)), pltpu.SemaphoreType.DMA((N_DEV,)),
                        pltpu.SemaphoreType.DMA((n_ready,)), pltpu.SemaphoreType.DMA((n_ready,)),
                        pltpu.SemaphoreType.DMA((ns + 2,))])
    return pl.pallas_call(
        body, name="dw_in_exchange", grid_spec=grid_spec,
        out_shape=[SDS((N_DEV, d, esh), BF16)] + [SDS(s.shape, s.dtype) for s in stacks] + [
            SDS((N_DEV,) + packed.shape, packed.dtype)],
        compiler_params=_params(("arbitrary", "arbitrary")),
    )(my_slot, h, dproj, *stacks, packed)


def _finish_small(packs, groups, chunk):
    rows = packs.shape[1]
    gc = groups * chunk

    def body(p_ref, sum_ref, loss_ref):
        row, col = _iotas(chunk)
        tril = col <= row
        for g in range(groups):
            rs = slice(g * chunk, (g + 1) * chunk)
            tot = p_ref[0, rs, :]
            for dev in range(1, N_DEV):
                tot = tot + p_ref[dev, rs, :]
            sum_ref[rs, :] = jnp.where(tril, tot, 0.0)
        rs = slice(gc, rows)
        tot = p_ref[0, rs, :]
        for dev in range(1, N_DEV):
            tot = tot + p_ref[dev, rs, :]
        sum_ref[rs, :] = tot
        loss_ref[...] = jnp.full((SUBLANE, LANE), jnp.sum(tot[rows - gc - SUBLANE:, :]), F32)

    return pl.pallas_call(
        body, name="finish_small",
        out_shape=[SDS((rows, LANE), F32), SDS((SUBLANE, LANE), F32)],
        in_specs=[pl.BlockSpec(memory_space=pltpu.VMEM)],
        out_specs=[pl.BlockSpec(memory_space=pltpu.VMEM)] * 2,
        compiler_params=pltpu.CompilerParams(vmem_limit_bytes=VMEM_LIMIT),
    )(packs)


def _in_proj(x2d, norm_in, wg_in):
    n, d = x2d.shape
    nsh, _, esh = wg_in.shape
    tm = _tile(n, 1024)

    def body(x_ref, g_ref, w_ref, proj_ref, h_ref):
        @pl.when(pl.program_id(1) == 0)
        def _():
            x = x_ref[...]
            h_ref[...] = (x * _rms_scale(x) * g_ref[...]).astype(BF16)

        proj_ref[...] = _dot(h_ref[...], w_ref[0])

    return pl.pallas_call(
        body, name="in_proj", grid=(n // tm, nsh),
        in_specs=[pl.BlockSpec((tm, d), lambda i, j: (i, 0)),
                  pl.BlockSpec((1, d), lambda i, j: (0, 0)),
                  pl.BlockSpec((1, d, esh), lambda i, j: (j, 0, 0))],
        out_specs=[pl.BlockSpec((tm, esh), lambda i, j: (i, j)),
                   pl.BlockSpec((tm, d), lambda i, j: (i, 0))],
        out_shape=[SDS((n, nsh * esh), F32), SDS((n, d), BF16)],
        compiler_params=_params(("parallel", "arbitrary")),
    )(x2d, norm_in, wg_in)


def _branch_a_fwd(proj, norm_v, w_s, b_col):
    n = proj.shape[0]
    d = norm_v.shape[1]
    groups, chunk, _ = w_s.shape
    tr = _tile(n, 4 * chunk)

    def body(u_ref, v_ref, z_ref, gv_ref, ws_ref, b_ref, ya_ref, vn_s, pre_s):
        row, col = _iotas(chunk)
        tril = col <= row
        vg, _ = _gelu(v_ref[...])
        vn_s[...] = (vg * _rms_scale(vg) * gv_ref[...]).astype(BF16)
        ug, _ = _gelu(u_ref[...])
        sz, _ = _silu(z_ref[...])
        pre_s[...] = ug * sz
        for g in range(groups):
            wm = jnp.where(tril, ws_ref[g], 0.0).astype(BF16)
            cs = slice(g * chunk, (g + 1) * chunk)
            for c in range(tr // chunk):
                rs = slice(c * chunk, (c + 1) * chunk)
                mixed = _dot(wm, vn_s[rs, cs]) + b_ref[g]
                ya_ref[rs, cs] = (pre_s[rs, cs] * mixed).astype(BF16)

    seg = lambda k: pl.BlockSpec((tr, d), lambda i: (i, k))
    return pl.pallas_call(
        body, name="branch_a_fwd", grid=(n // tr,),
        in_specs=[seg(0), seg(1), seg(2),
                  pl.BlockSpec((1, d), lambda i: (0, 0)),
                  pl.BlockSpec((groups, chunk, chunk), lambda i: (0, 0, 0)),
                  pl.BlockSpec((groups, chunk, 1), lambda i: (0, 0, 0))],
        out_specs=pl.BlockSpec((tr, d), lambda i: (i, 0)),
        out_shape=SDS((n, d), BF16),
        scratch_shapes=[pltpu.VMEM((tr, d), BF16), pltpu.VMEM((tr, d), F32)],
        compiler_params=_params(("parallel",)),
    )(proj, proj, proj, norm_v, w_s, b_col)


def _sb_fwd(proj, batch, seq, d, hd):
    heads = d // hd
    t = _tile(seq, SB_TILE)
    scale = hd ** -0.5
    nblk = seq // t
    nh = SB_HEADS
    wide = nh * hd

    def body(q_ref, k_ref, v_ref, zb_ref, yb_ref, o_ref, tot_ref, qs, ks, vs, later):
        qs[...] = q_ref[...].astype(BF16)
        ks[...] = k_ref[...].astype(BF16)
        vs[...] = v_ref[...].astype(BF16)
        row, col = _iotas(t)
        later[...] = (row > col).astype(BF16)

        def qblock(i, carry):
            r0 = pl.multiple_of(i * t, t)

            def tile(j, state, valid):
                c0 = pl.multiple_of(j * t, t)
                cols = [slice(hh * hd, (hh + 1) * hd) for hh in range(nh)]
                scores = [_sb_scores(qs[pl.ds(r0, t), cs], ks[pl.ds(c0, t), cs], scale, valid) for cs in cols]
                sums = [_tri_sum(log_rest, later[...]) for _, log_rest in scores]
                ws = []
                for hh in range(nh):
                    w = jnp.exp(scores[hh][0] + sums[hh] + state[hh][1])
                    if valid is not None:
                        w = jnp.where(valid, w, 0.0)
                    ws.append(w.astype(BF16))
                pv = [_dot(ws[hh], vs[pl.ds(c0, t), cols[hh]]) for hh in range(nh)]
                return tuple((state[hh][0] + pv[hh], state[hh][1] + jnp.sum(scores[hh][1], axis=-1, keepdims=True))
                             for hh in range(nh))

            causal = col < row
            zero = (jnp.zeros((t, hd), F32), jnp.zeros((t, 1), F32))
            state = tile(i, (zero,) * nh, causal)
            state = lax.fori_loop(0, i, lambda jj, st: tile(i - 1 - jj, st, None), state)
            for hh in range(nh):
                acc, run = state[hh]
                cs = slice(hh * hd, (hh + 1) * hd)
                o_ref[pl.ds(r0, t), cs] = acc
                tot_ref[hh, pl.ds(r0, t), :] = run
                sz, _ = _silu(zb_ref[pl.ds(r0, t), cs])
                yb_ref[pl.ds(r0, t), cs] = (acc * sz).astype(BF16)
            return carry

        lax.fori_loop(0, nblk, qblock, 0)

    col0 = d // wide
    seg = lambda k: pl.BlockSpec((seq, wide), lambda b, h: (b, k * col0 + h))
    return pl.pallas_call(
        body, name="sb_fwd", grid=(batch, heads // nh),
        in_specs=[seg(3), seg(4), seg(5), seg(6)],
        out_specs=[pl.BlockSpec((seq, wide), lambda b, h: (b, h))] * 2 + [
            pl.BlockSpec((nh, seq, 1), lambda b, h: (b * (heads // nh) + h, 0, 0))],
        out_shape=[SDS((batch * seq, d), BF16), SDS((batch * seq, d), F32), SDS((batch * heads, seq, 1), F32)],
        scratch_shapes=[pltpu.VMEM((seq, wide), BF16)] * 3 + [pltpu.VMEM((t, t), BF16)],
        compiler_params=_params(("parallel", "parallel")),
    )(proj, proj, proj, proj)


def _tail(x2d, tgt, ya, yb, proj, w_oa, w_ob, w_out, norm_final):
    n, d = x2d.shape
    e = proj.shape[1]
    tm = _tile(n, 256)

    def body(x_ref, t_ref, ya_ref, yb_ref, ga_ref, gb_ref, woa_ref, wob_ref, wout_ref, gf_ref,
             dproj_ref, dx2_ref, dya_ref, dyb_ref, mrg_ref, dpa_ref, dpb_ref, loss_ref, dgf_ref, dgb_s):
        i, kk = pl.program_id(0), pl.program_id(1)

        @pl.when(jnp.logical_and(i == 0, kk == 0))
        def _():
            loss_ref[...] = jnp.zeros_like(loss_ref)
            dgf_ref[...] = jnp.zeros_like(dgf_ref)

        @pl.when(kk == 0)
        def _():
            pa = _dot(ya_ref[...], woa_ref[...])
            pb = _dot(yb_ref[...], wob_ref[...])
            sa = _sigmoid(ga_ref[...])
            sb = _sigmoid(gb_ref[...])
            merged = (sa * pa + sb * pb).astype(BF16)
            mrg_ref[...] = merged
            x2 = x_ref[...] + _dot(merged, wout_ref[...])
            r2 = _rms_scale(x2)
            xh = x2 * r2
            gf = gf_ref[...]
            diff = xh * gf - t_ref[...]
            loss_ref[...] += jnp.sum(diff * diff, axis=0, keepdims=True) * (0.5 / d)
            dy = diff * (1.0 / d)
            dgf_ref[...] += jnp.sum(dy * xh, axis=0, keepdims=True)
            dxh = dy * gf
            dx2 = r2 * (dxh - xh * jnp.mean(dxh * xh, axis=-1, keepdims=True))
            dx2_ref[...] = dx2
            dm = _dot_nt(dx2.astype(BF16), wout_ref[...])
            dpa = (dm * sa).astype(BF16)
            dpb = (dm * sb).astype(BF16)
            dpa_ref[...] = dpa
            dpb_ref[...] = dpb
            dproj_ref[...] = (dm * pa * (sa * (1.0 - sa))).astype(BF16)
            dgb_s[...] = (dm * pb * (sb * (1.0 - sb))).astype(BF16)
            dya_ref[...] = _dot_nt(dpa, woa_ref[...])
            dyb_ref[...] = _dot_nt(dpb, wob_ref[...])

        @pl.when(kk == 1)
        def _():
            dproj_ref[...] = dgb_s[...]

    rows = lambda k=0: pl.BlockSpec((tm, d), lambda i, kk: (i, k))
    full = pl.BlockSpec((d, d), lambda i, kk: (0, 0))
    vec = pl.BlockSpec((1, d), lambda i, kk: (0, 0))
    return pl.pallas_call(
        body, name="tail", grid=(n // tm, 2),
        in_specs=[rows(), rows(), rows(), rows(), rows(7), rows(8), full, full, full, vec],
        out_specs=[pl.BlockSpec((tm, d), lambda i, kk: (i, 7 + kk)),
                   rows(), rows(), rows(), rows(), rows(), rows(), vec, vec],
        out_shape=[SDS((n, e), BF16), SDS((n, d), F32), SDS((n, d), F32), SDS((n, d), F32),
                   SDS((n, d), BF16), SDS((n, d), BF16), SDS((n, d), BF16),
                   SDS((1, d), F32), SDS((1, d), F32)],
        scratch_shapes=[pltpu.VMEM((tm, d), BF16)],
        compiler_params=_params(("arbitrary", "arbitrary")),
    )(x2d, tgt, ya, yb, proj, proj, w_oa, w_ob, w_out, norm_final)


def _tn_matmul(a, b, name):
    n, p = a.shape
    q = b.shape[1]
    tk = _tile(n, 512)
    nk = n // tk

    def body(a_ref, b_ref, o_ref, acc):
        k = pl.program_id(0)

        @pl.when(k == 0)
        def _():
            acc[...] = jnp.zeros_like(acc)

        acc[...] += _dot_tn(a_ref[...], b_ref[...].astype(BF16))

        @pl.when(k == nk - 1)
        def _():
            o_ref[...] = acc[...].astype(BF16)

    return pl.pallas_call(
        body, name=name, grid=(nk,),
        in_specs=[pl.BlockSpec((tk, p), lambda k: (k, 0)), pl.BlockSpec((tk, q), lambda k: (k, 0))],
        out_specs=pl.BlockSpec((p, q), lambda k: (0, 0)),
        out_shape=SDS((p, q), BF16),
        scratch_shapes=[pltpu.VMEM((p, q), F32)],
        compiler_params=_params(("arbitrary",)),
    )(a, b)


def _sb_bwd(proj, o, dyb, tot, dproj, batch, seq, d, hd):
    heads = d // hd
    t = _tile(seq, SB_TILE)
    scale = hd ** -0.5
    nblk = seq // t
    nh = SB_HEADS
    wide = nh * hd

    def compute(q_ref, k_ref, v_ref, zb_ref, o_ref, dyb_ref, tot_ref, qs, ks, vs, dos, res, upto, before):
        qs[...] = q_ref[...].astype(BF16)
        ks[...] = k_ref[...].astype(BF16)
        vs[...] = v_ref[...].astype(BF16)
        sz, dsz = _silu(zb_ref[...])
        dyb_v = dyb_ref[...]
        dos[...] = (dyb_v * sz).astype(BF16)
        res[3] = dyb_v * o_ref[...] * dsz
        res[1] = jnp.zeros((seq, wide), F32)
        res[2] = jnp.zeros((seq, wide), F32)
        row, col = _iotas(t)
        upto[...] = (row <= col).astype(BF16)
        before[...] = (row < col).astype(BF16)

        def qblock(i, carry):
            r0 = pl.multiple_of(i * t, t)

            def tile(j, state, valid):
                c0 = pl.multiple_of(j * t, t)
                hs = range(nh)
                cols = [slice(hh * hd, (hh + 1) * hd) for hh in hs]
                q_i = [qs[pl.ds(r0, t), cs] for cs in cols]
                k_j = [ks[pl.ds(c0, t), cs] for cs in cols]
                do_i = [dos[pl.ds(r0, t), cs] for cs in cols]
                scores = [_sb_scores(q_i[hh], k_j[hh], scale, valid) for hh in hs]
                dw = [_dot_nt(do_i[hh], vs[pl.ds(c0, t), cols[hh]]) for hh in hs]
                sums = [_tri_sum(scores[hh][1], upto[...]) for hh in hs]
                ws, gs = [], []
                for hh in hs:
                    left = tot_ref[hh, pl.ds(r0, t), :] - state[hh][1]
                    w = jnp.exp(scores[hh][0] + (left - sums[hh]))
                    if valid is not None:
                        w = jnp.where(valid, w, 0.0)
                    ws.append(w.astype(BF16))
                    gs.append(dw[hh] * w)
                gsums = [_tri_sum(gs[hh], before[...]) for hh in hs]
                dzs = []
                for hh in hs:
                    beta = jnp.exp(scores[hh][0])
                    dz = (gs[hh] * (1.0 - beta) - (gsums[hh] + state[hh][2]) * beta) * scale
                    if valid is not None:
                        dz = jnp.where(valid, dz, 0.0)
                    dzs.append(dz.astype(BF16))
                for hh in hs:
                    res[2, pl.ds(c0, t), cols[hh]] += _dot_tn(ws[hh], do_i[hh])
                for hh in hs:
                    res[1, pl.ds(c0, t), cols[hh]] += _dot_tn(dzs[hh], q_i[hh])
                dqs = [_dot(dzs[hh], k_j[hh]) for hh in hs]
                return tuple((state[hh][0] + dqs[hh],
                              state[hh][1] + jnp.sum(scores[hh][1], axis=-1, keepdims=True),
                              state[hh][2] + jnp.sum(gs[hh], axis=-1, keepdims=True)) for hh in hs)

            zero = jnp.zeros((t, 1), F32)
            state = lax.fori_loop(0, i, lambda j, st: tile(j, st, None),
                                  ((jnp.zeros((t, hd), F32), zero, zero),) * nh)
            state = tile(i, state, col < row)
            for hh in range(nh):
                res[0, pl.ds(r0, t), hh * hd:(hh + 1) * hd] = state[hh][0]
            return carry

        lax.fori_loop(0, nblk, qblock, 0)

    def body(q_ref, k_ref, v_ref, zb_ref, o_ref, dyb_ref, tot_ref, dproj_in, out_ref,
             qs, ks, vs, dos, res, upto, before):
        del dproj_in
        kk = pl.program_id(2)

        @pl.when(kk == 0)
        def _():
            compute(q_ref, k_ref, v_ref, zb_ref, o_ref, dyb_ref, tot_ref, qs, ks, vs, dos, res, upto, before)

        out_ref[...] = res[kk].astype(BF16)

    col0 = d // wide
    seg = lambda k: pl.BlockSpec((seq, wide), lambda b, h, kk: (b, k * col0 + h))
    head = pl.BlockSpec((seq, wide), lambda b, h, kk: (b, h))
    return pl.pallas_call(
        body, name="sb_bwd", grid=(batch, heads // nh, 4),
        in_specs=[seg(3), seg(4), seg(5), seg(6), head, head,
                  pl.BlockSpec((nh, seq, 1), lambda b, h, kk: (b * (heads // nh) + h, 0, 0)),
                  pl.BlockSpec(memory_space=pl.ANY)],
        out_specs=pl.BlockSpec((seq, wide), lambda b, h, kk: (b, (3 + kk) * col0 + h)),
        out_shape=SDS(dproj.shape, dproj.dtype),
        input_output_aliases={7: 0},
        scratch_shapes=[pltpu.VMEM((seq, wide), BF16)] * 4 + [pltpu.VMEM((4, seq, wide), F32)] + [
            pltpu.VMEM((t, t), BF16)] * 2,
        compiler_params=_params(("arbitrary", "arbitrary", "arbitrary")),
    )(proj, proj, proj, proj, o, dyb, tot, dproj)


def _branch_a_bwd(proj, dya, norm_v, w_s, b_col, dproj):
    n = proj.shape[0]
    d = norm_v.shape[1]
    groups, chunk, _ = w_s.shape
    tr = _tile(n, 2 * chunk)

    def body(u_ref, v_ref, z_ref, dya_ref, gv_ref, ws_ref, b_ref, dproj_in,
             out_ref, dws_ref, dbias_ref, dgv_ref, vn_s, dmix_s, dvn_s, db_ref):
        del dproj_in

        @pl.when(pl.program_id(0) == 0)
        def _():
            dws_ref[...] = jnp.zeros_like(dws_ref)
            db_ref[...] = jnp.zeros_like(db_ref)
            dgv_ref[...] = jnp.zeros_like(dgv_ref)

        row, col = _iotas(chunk)
        tril = col <= row
        u, v, z, dya_v = u_ref[...], v_ref[...], z_ref[...], dya_ref[...]
        gv = gv_ref[...]
        vg, dvg_dv = _gelu(v)
        r = _rms_scale(vg)
        vh = vg * r
        vn_s[...] = (vh * gv).astype(BF16)
        ug, dug_du = _gelu(u)
        sz, dsz = _silu(z)
        dmix_s[...] = dya_v * ug * sz
        for g in range(groups):
            wm = jnp.where(tril, ws_ref[g], 0.0).astype(BF16)
            cs = slice(g * chunk, (g + 1) * chunk)
            for c in range(tr // chunk):
                rs = slice(c * chunk, (c + 1) * chunk)
                vn = vn_s[rs, cs]
                mixed = _dot(wm, vn) + b_ref[g]
                dmix = dmix_s[rs, cs]
                dmix16 = dmix.astype(BF16)
                dws_ref[g] += _dot_nt(dmix16, vn)
                db_ref[g] += dmix
                dvn_s[rs, cs] = _dot_tn(wm, dmix16)
                t_u = dya_v[rs, cs] * mixed
                out_ref[rs, g * chunk:(g + 1) * chunk] = (t_u * sz[rs, cs] * dug_du[rs, cs]).astype(BF16)
                out_ref[rs, 2 * d + g * chunk:2 * d + (g + 1) * chunk] = (t_u * ug[rs, cs] * dsz[rs, cs]).astype(BF16)
        dvn = dvn_s[...]
        dgv_ref[...] += jnp.sum(dvn * vh, axis=0, keepdims=True)
        dvh = dvn * gv
        dvg = r * (dvh - vh * jnp.mean(dvh * vh, axis=-1, keepdims=True))
        out_ref[:, d:2 * d] = (dvg * dvg_dv).astype(BF16)

        @pl.when(pl.program_id(0) == n // tr - 1)
        def _():
            for g in range(groups):
                dbias_ref[g:g + 1, :] = jnp.sum(db_ref[g].T, axis=0, keepdims=True)

    seg = lambda k: pl.BlockSpec((tr, d), lambda i: (i, k))
    return pl.pallas_call(
        body, name="branch_a_bwd", grid=(n // tr,),
        in_specs=[seg(0), seg(1), seg(2), seg(0),
                  pl.BlockSpec((1, d), lambda i: (0, 0)),
                  pl.BlockSpec((groups, chunk, chunk), lambda i: (0, 0, 0)),
                  pl.BlockSpec((groups, chunk, 1), lambda i: (0, 0, 0)),
                  pl.BlockSpec(memory_space=pl.ANY)],
        out_specs=[pl.BlockSpec((tr, 3 * d), lambda i: (i, 0)),
                   pl.BlockSpec((groups, chunk, chunk), lambda i: (0, 0, 0)),
                   pl.BlockSpec((groups, chunk), lambda i: (0, 0)),
                   pl.BlockSpec((1, d), lambda i: (0, 0))],
        out_shape=[SDS(dproj.shape, dproj.dtype), SDS((groups, chunk, chunk), F32),
                   SDS((groups, chunk), F32), SDS((1, d), F32)],
        input_output_aliases={7: 0},
        scratch_shapes=[pltpu.VMEM((tr, d), BF16), pltpu.VMEM((tr, d), F32), pltpu.VMEM((tr, d), F32),
                        pltpu.VMEM((groups, chunk, chunk), F32)],
        compiler_params=_params(("arbitrary",)),
    )(proj, proj, proj, dya, norm_v, w_s, b_col, dproj)


def _dx(dproj, wg_in, x2d, dx2, norm_in):
    n, d = x2d.shape
    nsh, _, esh = wg_in.shape
    tm = _tile(n, 1024)

    def body(dp_ref, w_ref, x_ref, dx2_ref, g_ref, gx_ref, dg_ref, acc):
        i, k = pl.program_id(0), pl.program_id(1)

        @pl.when(jnp.logical_and(i == 0, k == 0))
        def _():
            dg_ref[...] = jnp.zeros_like(dg_ref)

        @pl.when(k == 0)
        def _():
            acc[...] = jnp.zeros_like(acc)

        acc[...] += _dot_nt(dp_ref[...], w_ref[0])

        @pl.when(k == nsh - 1)
        def _():
            dh = acc[...]
            x = x_ref[...]
            r = _rms_scale(x)
            xh = x * r
            dg_ref[...] += jnp.sum(dh * xh, axis=0, keepdims=True)
            dxh = dh * g_ref[...]
            gx_ref[...] = dx2_ref[...] + r * (dxh - xh * jnp.mean(dxh * xh, axis=-1, keepdims=True))

    rows = pl.BlockSpec((tm, d), lambda i, k: (i, 0))
    vec = pl.BlockSpec((1, d), lambda i, k: (0, 0))
    return pl.pallas_call(
        body, name="dx", grid=(n // tm, nsh),
        in_specs=[pl.BlockSpec((tm, esh), lambda i, k: (i, k)),
                  pl.BlockSpec((1, d, esh), lambda i, k: (k, 0, 0)), rows, rows, vec],
        out_specs=[rows, vec],
        out_shape=[SDS((n, d), F32), SDS((1, d), F32)],
        scratch_shapes=[pltpu.VMEM((tm, d), F32)],
        compiler_params=_params(("arbitrary", "arbitrary")),
    )(dproj, wg_in, x2d, dx2, norm_in)


def _adamw_outputs(g_ref, d_ref, m_ref, v_ref, g, w, m, v):
    delta, m2, v2 = _adamw(w, g, m, v)
    g_ref[...] = g
    d_ref[...] = delta
    m_ref[...] = m2
    v_ref[...] = v2


def _reduce_adamw(slots, w, m, v, name):
    _, r, c = slots.shape
    tr = _tile(r, 128)

    def body(s_ref, w_ref, m_ref, v_ref, g_out, d_out, m_out, v_out):
        g = s_ref[0].astype(F32)
        for k in range(1, N_DEV):
            g = g + s_ref[k].astype(F32)
        _adamw_outputs(g_out, d_out, m_out, v_out, g, w_ref[...], m_ref[...], v_ref[...])

    blk = pl.BlockSpec((tr, c), lambda i: (i, 0))
    return pl.pallas_call(
        body, name=name, grid=(r // tr,),
        in_specs=[pl.BlockSpec((N_DEV, tr, c), lambda i: (0, i, 0)), blk, blk, blk],
        out_specs=[blk] * 4,
        out_shape=[SDS((r, c), F32)] * 4,
        compiler_params=_params(("parallel",)),
    )(slots, w, m, v)


def _adamw_small(g, w, m, v, name):
    def body(g_ref, w_ref, m_ref, v_ref, g_out, d_out, m_out, v_out):
        _adamw_outputs(g_out, d_out, m_out, v_out, g_ref[...], w_ref[...], m_ref[...], v_ref[...])

    return pl.pallas_call(
        body, name=name,
        out_shape=[SDS(g.shape, F32)] * 4,
        in_specs=[pl.BlockSpec(memory_space=pltpu.VMEM)] * 4,
        out_specs=[pl.BlockSpec(memory_space=pltpu.VMEM)] * 4,
    )(g, w, m, v)


def kernel(x, norm_in, w_in, norm_v, w_s, b_s, w_o_gmlp, w_o_sb, w_out, norm_final, loss_target, m_norm_in, m_w_in, m_norm_v, m_w_s, m_b_s, m_w_o_gmlp, m_w_o_sb, m_w_out, m_norm_final, v_norm_in, v_w_in, v_norm_v, v_w_s, v_b_s, v_w_o_gmlp, v_w_o_sb, v_w_out, v_norm_final):
    batch, seq, d = x.shape
    n = batch * seq
    groups, chunk = w_s.shape[1], w_s.shape[2]
    hd = LANE
    x2d = x.reshape(n, d)
    tgt = loss_target.reshape(n, d)
    b_col = b_s[0].reshape(groups, chunk, 1)
    norm_final2 = norm_final.reshape(1, d)

    wg_in, wg_oa, wg_ob, wg_out = _gather_weights([w_in[0], w_o_gmlp[0], w_o_sb[0], w_out[0]])
    rsh = wg_oa.shape[1]
    wf_oa, wf_ob, wf_out = (w.reshape(N_DEV * rsh, d) for w in (wg_oa, wg_ob, wg_out))

    proj, h = _in_proj(x2d, norm_in, wg_in)
    ya = _branch_a_fwd(proj, norm_v, w_s[0], b_col)
    yb, o, sb_tot = _sb_fwd(proj, batch, seq, d, hd)
    dproj, dx2, dya, dyb, merged, dpa, dpb, loss_vec, dgf = _tail(
        x2d, tgt, ya, yb, proj, wf_oa, wf_ob, wf_out, norm_final2)
    gp_oa = _tn_matmul(ya, dpa, "dw_o_gmlp")
    gp_ob = _tn_matmul(yb, dpb, "dw_o_sb")
    gp_out = _tn_matmul(merged, dx2, "dw_out")
    dproj = _sb_bwd(proj, o, dyb, sb_tot, dproj, batch, seq, d, hd)
    dproj, gp_ws, gp_b, gp_nv = _branch_a_bwd(proj, dya, norm_v, w_s[0], b_col, dproj)
    grad_x, gp_nin = _dx(dproj, wg_in, x2d, dx2, norm_in)

    slab = lambda a: a.reshape(d // LANE, LANE)
    gc = groups * chunk
    packed = jnp.concatenate(
        [gp_ws.reshape(gc, chunk), gp_b, slab(gp_nin), slab(gp_nv), slab(dgf), slab(loss_vec)], axis=0)
    my_slot = _slot(_me()).astype(jnp.int32).reshape(1)
    s_win, s_oa, s_ob, s_out, packs = _dw_in_exchange(
        h, dproj, my_slot, [g.reshape(N_DEV, rsh, d) for g in (gp_oa, gp_ob, gp_out)], packed)
    tot, loss_slab = _finish_small(packs, groups, chunk)
    ns = d // LANE
    g_ws = tot[:gc]
    g_b = tot[gc:gc + groups]
    g_nin, g_nv, g_nf = (tot[gc + groups + k * ns:gc + groups + (k + 1) * ns] for k in range(3))
    loss = loss_slab[0, 0]

    res = {}
    res["w_in"] = _reduce_adamw(s_win, w_in[0], m_w_in[0], v_w_in[0], "adamw_w_in")
    res["w_o_gmlp"] = _reduce_adamw(s_oa, w_o_gmlp[0], m_w_o_gmlp[0], v_w_o_gmlp[0], "adamw_w_o_gmlp")
    res["w_o_sb"] = _reduce_adamw(s_ob, w_o_sb[0], m_w_o_sb[0], v_w_o_sb[0], "adamw_w_o_sb")
    res["w_out"] = _reduce_adamw(s_out, w_out[0], m_w_out[0], v_w_out[0], "adamw_w_out")
    res["norm_in"] = _adamw_small(g_nin, slab(norm_in), slab(m_norm_in), slab(v_norm_in), "adamw_norm_in")
    res["norm_v"] = _adamw_small(g_nv, slab(norm_v), slab(m_norm_v), slab(v_norm_v), "adamw_norm_v")
    res["norm_final"] = _adamw_small(g_nf, slab(norm_final), slab(m_norm_final), slab(v_norm_final), "adamw_norm_final")
    res["w_s"] = _adamw_small(g_ws, w_s.reshape(gc, chunk), m_w_s.reshape(gc, chunk), v_w_s.reshape(gc, chunk), "adamw_w_s")
    res["b_s"] = _adamw_small(g_b, b_s[0], m_b_s[0], v_b_s[0], "adamw_b_s")

    shapes = {"norm_in": norm_in.shape, "w_in": w_in.shape, "norm_v": norm_v.shape, "w_s": w_s.shape,
              "b_s": b_s.shape, "w_o_gmlp": w_o_gmlp.shape, "w_o_sb": w_o_sb.shape, "w_out": w_out.shape,
              "norm_final": norm_final.shape}
    names = list(shapes)
    outs = [loss, grad_x.reshape(batch, seq, d)]
    for kind in range(4):
        outs += [res[name][kind].reshape(shapes[name]) for name in names]
    return tuple(outs)
```

```python
import functools
import math

import jax
import jax.numpy as jnp
from jax import lax
from jax.experimental import pallas as pl
from jax.experimental.pallas import tpu as pltpu

F32 = jnp.float32
BF16 = jnp.bfloat16
SDS = jax.ShapeDtypeStruct
MESH_ID = pl.DeviceIdType.MESH

N_DEV = 8
LANE = 128
SUBLANE = 8
VMEM_LIMIT = 56 * 1024 * 1024
SB_TILE = 256
SB_HEADS = 2
SB_STRIP = 64
RMS_EPS = 1e-6

ADAM_LR = 0.001
ADAM_B1 = 0.9
ADAM_B2 = 0.999
ADAM_EPS = 1e-08
ADAM_WD = 0.01
ADAM_STEP = 10

NT_DIMS = (((1,), (1,)), ((), ()))
TN_DIMS = (((0,), (0,)), ((), ()))


def _params(semantics=None):
    return pltpu.CompilerParams(dimension_semantics=semantics, vmem_limit_bytes=VMEM_LIMIT)


def _tile(n, preferred):
    t = min(n, preferred)
    assert n % t == 0, (n, t)
    return t


def _sigmoid(x):
    return 1.0 / (1.0 + jnp.exp(-x))


def _silu(x):
    s = _sigmoid(x)
    return x * s, s * (1.0 + x * (1.0 - s))


def _gelu(x):
    k = math.sqrt(2.0 / math.pi)
    x2 = x * x
    t = jnp.tanh(k * (x + 0.044715 * (x * x2)))
    cdf = 0.5 * (1.0 + t)
    return x * cdf, cdf + 0.5 * x * (1.0 - t * t) * (k * (1.0 + 3.0 * 0.044715 * x2))


def _rms_scale(x):
    return lax.rsqrt(jnp.mean(x * x, axis=-1, keepdims=True) + RMS_EPS)


def _iotas(n):
    return (lax.broadcasted_iota(jnp.int32, (n, n), 0), lax.broadcasted_iota(jnp.int32, (n, n), 1))


def _adamw(w, g, m, v):
    m = ADAM_B1 * m + (1.0 - ADAM_B1) * g
    v = ADAM_B2 * v + (1.0 - ADAM_B2) * (g * g)
    m_hat = m / (1.0 - ADAM_B1 ** ADAM_STEP)
    v_hat = v / (1.0 - ADAM_B2 ** ADAM_STEP)
    delta = -ADAM_LR * (m_hat / (jnp.sqrt(v_hat) + ADAM_EPS) + ADAM_WD * w)
    return delta, m, v


def _dot(a, b):
    return jnp.dot(a, b, preferred_element_type=F32)


def _dot_nt(a, b):
    return lax.dot_general(a, b, NT_DIMS, preferred_element_type=F32)


def _dot_tn(a, b):
    return lax.dot_general(a, b, TN_DIMS, preferred_element_type=F32)


def _sb_logs(raw, scale, valid):
    z = (raw * scale).astype(BF16)
    log_beta = jnp.minimum(z, 0) - jnp.log(1 + jnp.exp(-jnp.abs(z)))
    log_rest = log_beta - z
    if valid is not None:
        log_rest = jnp.where(valid, log_rest, 0)
    return log_beta, log_rest


def _me():
    return lax.axis_index("x"), lax.axis_index("y"), lax.axis_index("c")


def _slot(p):
    return 4 * p[0] + 2 * p[1] + p[2]


def _peer(me, k):
    flips = ((k >> 2) & 1, (k >> 1) & 1, k & 1)
    return tuple(1 - a if f else a for a, f in zip(me, flips))


def _gather_weights(shards):
    n = len(shards)

    def body(*refs):
        ins, outs, stage = refs[:n], refs[n:2 * n], refs[2 * n:3 * n]
        send_sems, recv_sems, local_sems = refs[3 * n:]
        x, y, c = _me()
        me, sibling = (x, y, c), (x, y, 1 - c)
        chips = [(1 - x, y), (x, 1 - y), (1 - x, 1 - y)]

        def copy(a, k, block, to, src=None):
            dst = outs[a].at[_slot(block)]
            return pltpu.make_async_remote_copy(
                src_ref=dst if src is None else src, dst_ref=dst,
                send_sem=send_sems.at[7 * a + k], recv_sem=recv_sems.at[7 * a + k],
                device_id=to, device_id_type=MESH_ID)

        started = []
        for a in range(n):
            stage[a][...] = ins[a][...].astype(BF16)
            mine = pltpu.make_async_copy(stage[a], outs[a].at[_slot(me)], local_sems.at[a])
            mine.start()
            started.append(mine)
        sends = []
        for a in range(n):
            sends.append(copy(a, 0, me, sibling, src=stage[a]))
            sends += [copy(a, 1 + j, me, (*chip, c), src=stage[a]) for j, chip in enumerate(chips)]
        for cp in sends:
            cp.start()
        for a in range(n):
            for j, chip in enumerate(chips):
                copy(a, 1 + j, (*chip, c), me).wait_recv()
                passed = copy(a, 4 + j, (*chip, c), sibling)
                passed.start()
                sends.append(passed)
        for a in range(n):
            copy(a, 0, sibling, me).wait_recv()
            for j, chip in enumerate(chips):
                copy(a, 4 + j, (*chip, 1 - c), me).wait_recv()
        for cp in sends:
            cp.wait_send()
        for mine in started:
            mine.wait()

    return pl.pallas_call(
        body, name="gather_weights",
        out_shape=[SDS((N_DEV,) + s.shape, BF16) for s in shards],
        in_specs=[pl.BlockSpec(memory_space=pltpu.VMEM)] * n,
        out_specs=[pl.BlockSpec(memory_space=pl.ANY)] * n,
        scratch_shapes=[pltpu.VMEM(s.shape, BF16) for s in shards] + [
            pltpu.SemaphoreType.DMA((7 * n,)), pltpu.SemaphoreType.DMA((7 * n,)),
            pltpu.SemaphoreType.DMA((n,))],
        compiler_params=pltpu.CompilerParams(vmem_limit_bytes=VMEM_LIMIT),
    )(*shards)


def _dw_in_exchange(h, dproj, my_slot, stacks, packed):
    n, d = h.shape
    esh = dproj.shape[1] // N_DEV
    tk = _tile(n, 512)
    nk = n // tk
    ns = len(stacks)
    last_j = N_DEV - 1

    def body(me_ref, h_ref, dp_ref, *refs):
        del me_ref
        st_in, pk_in = refs[:ns], refs[ns]
        win_out, st_out, pk_out = refs[ns + 1], refs[ns + 2:2 * ns + 2], refs[2 * ns + 2]
        acc, sendbuf, win_send, win_recv, send_sems, recv_sems, local_sems = refs[2 * ns + 3:]
        j, k = pl.program_id(0), pl.program_id(1)
        me = _me()
        mine = _slot(me)

        def ready_copies():
            local = [pltpu.make_async_copy(st_in[a].at[mine], st_out[a].at[mine], local_sems.at[a])
                     for a in range(ns)]
            local.append(pltpu.make_async_copy(pk_in, pk_out.at[mine], local_sems.at[ns]))
            remote = []
            for kk in range(1, N_DEV):
                peer = _peer(me, kk)
                for a in range(ns):
                    remote.append(pltpu.make_async_remote_copy(
                        src_ref=st_in[a].at[_slot(peer)], dst_ref=st_out[a].at[mine],
                        send_sem=send_sems.at[(ns + 1) * (kk - 1) + a],
                        recv_sem=recv_sems.at[(ns + 1) * (kk - 1) + a],
                        device_id=peer, device_id_type=MESH_ID))
                remote.append(pltpu.make_async_remote_copy(
                    src_ref=pk_in, dst_ref=pk_out.at[mine],
                    send_sem=send_sems.at[(ns + 1) * (kk - 1) + ns],
                    recv_sem=recv_sems.at[(ns + 1) * (kk - 1) + ns],
                    device_id=peer, device_id_type=MESH_ID))
            return local, remote

        def shard_copy(jj):
            owner = (mine + 1 + jj) % N_DEV
            return pltpu.make_async_remote_copy(
                src_ref=sendbuf.at[jj % 2], dst_ref=win_out.at[mine],
                send_sem=win_send.at[jj % 2], recv_sem=win_recv.at[mine],
                device_id=(owner // 4, (owner // 2) % 2, owner % 2), device_id_type=MESH_ID)

        def own_copy():
            return pltpu.make_async_copy(sendbuf.at[last_j % 2], win_out.at[mine], local_sems.at[ns + 1])

        @pl.when(jnp.logical_and(j == 0, k == 0))
        def _():
            local, remote = ready_copies()
            for cp in local + remote:
                cp.start()

        @pl.when(k == 0)
        def _():
            acc[...] = jnp.zeros_like(acc)

        acc[...] += _dot_tn(h_ref[...], dp_ref[...])

        @pl.when(k == nk - 1)
        def _():
            @pl.when(j >= 2)
            def _():
                shard_copy(j - 2).wait_send()

            sendbuf[j % 2] = acc[...].astype(BF16)

            @pl.when(j < last_j)
            def _():
                shard_copy(j).start()

            @pl.when(j == last_j)
            def _():
                own_copy().start()
                shard_copy(last_j - 1).wait_send()
                own_copy().wait()
                for src in range(N_DEV):
                    @pl.when(src != mine)
                    def _():
                        landed = win_out.at[src]
                        pltpu.make_async_remote_copy(
                            src_ref=landed, dst_ref=landed, send_sem=win_send.at[0], recv_sem=win_recv.at[src],
                            device_id=me, device_id_type=MESH_ID).wait_recv()
                local, remote = ready_copies()
                for cp in remote:
                    cp.wait_send()
                idx = 0
                for kk in range(1, N_DEV):
                    peer = _slot(_peer(me, kk))
                    for a in range(ns + 1):
                        landed = pk_out.at[peer] if a == ns else st_out[a].at[peer]
                        pltpu.make_async_remote_copy(
                            src_ref=landed, dst_ref=landed, send_sem=send_sems.at[idx], recv_sem=recv_sems.at[idx],
                            device_id=me, device_id_type=MESH_ID).wait_recv()
                        idx += 1
                for cp in local:
                    cp.wait()

    any_spec = pl.BlockSpec(memory_space=pl.ANY)
    n_ready = 7 * (ns + 1)
    grid_spec = pltpu.PrefetchScalarGridSpec(
        num_scalar_prefetch=1, grid=(N_DEV, nk),
        in_specs=[pl.BlockSpec((tk, d), lambda j, k, me: (k, 0)),
                  pl.BlockSpec((tk, esh), lambda j, k, me: (k, (me[0] + 1 + j) % N_DEV))] + [any_spec] * (ns + 1),
        out_specs=[any_spec] * (ns + 2),
        scratch_shapes=[pltpu.VMEM((d, esh), F32), pltpu.VMEM((2, d, esh), BF16),
                        pltpu.SemaphoreType.DMA((2,)), pltpu.SemaphoreType.DMA((N_DEV,)),
                        pltpu.SemaphoreType.DMA((n_ready,)), pltpu.SemaphoreType.DMA((n_ready,)),
                        pltpu.SemaphoreType.DMA((ns + 2,))])
    return pl.pallas_call(
        body, name="dw_in_exchange", grid_spec=grid_spec,
        out_shape=[SDS((N_DEV, d, esh), BF16)] + [SDS(s.shape, s.dtype) for s in stacks] + [
            SDS((N_DEV,) + packed.shape, packed.dtype)],
        compiler_params=_params(("arbitrary", "arbitrary")),
    )(my_slot, h, dproj, *stacks, packed)


def _finish_small(packs, groups, chunk):
    rows = packs.shape[1]
    gc = groups * chunk

    def body(p_ref, sum_ref, loss_ref):
        row, col = _iotas(chunk)
        tril = col <= row
        for g in range(groups):
            rs = slice(g * chunk, (g + 1) * chunk)
            tot = p_ref[0, rs, :]
            for dev in range(1, N_DEV):
                tot = tot + p_ref[dev, rs, :]
            sum_ref[rs, :] = jnp.where(tril, tot, 0.0)
        rs = slice(gc, rows)
        tot = p_ref[0, rs, :]
        for dev in range(1, N_DEV):
            tot = tot + p_ref[dev, rs, :]
        sum_ref[rs, :] = tot
        loss_ref[...] = jnp.full((SUBLANE, LANE), jnp.sum(tot[rows - gc - SUBLANE:, :]), F32)

    return pl.pallas_call(
        body, name="finish_small",
        out_shape=[SDS((rows, LANE), F32), SDS((SUBLANE, LANE), F32)],
        in_specs=[pl.BlockSpec(memory_space=pltpu.VMEM)],
        out_specs=[pl.BlockSpec(memory_space=pltpu.VMEM)] * 2,
        compiler_params=pltpu.CompilerParams(vmem_limit_bytes=VMEM_LIMIT),
    )(packs)


def _in_proj(x2d, norm_in, wg_in):
    n, d = x2d.shape
    nsh, _, esh = wg_in.shape
    tm = _tile(n, 1024)

    def body(x_ref, g_ref, w_ref, proj_ref, h_ref):
        @pl.when(pl.program_id(1) == 0)
        def _():
            x = x_ref[...]
            h_ref[...] = (x * _rms_scale(x) * g_ref[...]).astype(BF16)

        proj_ref[...] = _dot(h_ref[...], w_ref[0])

    return pl.pallas_call(
        body, name="in_proj", grid=(n // tm, nsh),
        in_specs=[pl.BlockSpec((tm, d), lambda i, j: (i, 0)),
                  pl.BlockSpec((1, d), lambda i, j: (0, 0)),
                  pl.BlockSpec((1, d, esh), lambda i, j: (j, 0, 0))],
        out_specs=[pl.BlockSpec((tm, esh), lambda i, j: (i, j)),
                   pl.BlockSpec((tm, d), lambda i, j: (i, 0))],
        out_shape=[SDS((n, nsh * esh), F32), SDS((n, d), BF16)],
        compiler_params=_params(("parallel", "arbitrary")),
    )(x2d, norm_in, wg_in)


def _branch_a_fwd(proj, norm_v, w_s, b_col):
    n = proj.shape[0]
    d = norm_v.shape[1]
    groups, chunk, _ = w_s.shape
    tr = _tile(n, 4 * chunk)

    def body(u_ref, v_ref, z_ref, gv_ref, ws_ref, b_ref, ya_ref, vn_s, pre_s):
        row, col = _iotas(chunk)
        tril = col <= row
        vg, _ = _gelu(v_ref[...])
        vn_s[...] = (vg * _rms_scale(vg) * gv_ref[...]).astype(BF16)
        ug, _ = _gelu(u_ref[...])
        sz, _ = _silu(z_ref[...])
        pre_s[...] = ug * sz
        for g in range(groups):
            wm = jnp.where(tril, ws_ref[g], 0.0).astype(BF16)
            cs = slice(g * chunk, (g + 1) * chunk)
            for c in range(tr // chunk):
                rs = slice(c * chunk, (c + 1) * chunk)
                mixed = _dot(wm, vn_s[rs, cs]) + b_ref[g]
                ya_ref[rs, cs] = (pre_s[rs, cs] * mixed).astype(BF16)

    seg = lambda k: pl.BlockSpec((tr, d), lambda i: (i, k))
    return pl.pallas_call(
        body, name="branch_a_fwd", grid=(n // tr,),
        in_specs=[seg(0), seg(1), seg(2),
                  pl.BlockSpec((1, d), lambda i: (0, 0)),
                  pl.BlockSpec((groups, chunk, chunk), lambda i: (0, 0, 0)),
                  pl.BlockSpec((groups, chunk, 1), lambda i: (0, 0, 0))],
        out_specs=pl.BlockSpec((tr, d), lambda i: (i, 0)),
        out_shape=SDS((n, d), BF16),
        scratch_shapes=[pltpu.VMEM((tr, d), BF16), pltpu.VMEM((tr, d), F32)],
        compiler_params=_params(("parallel",)),
    )(proj, proj, proj, norm_v, w_s, b_col)


def _sb_fwd(proj, batch, seq, d, hd):
    heads = d // hd
    t = _tile(seq, SB_TILE)
    scale = hd ** -0.5
    nblk = seq // t
    nh = SB_HEADS
    wide = nh * hd

    def body(q_ref, k_ref, v_ref, zb_ref, yb_ref, o_ref, tot_ref, qs, ks, vs, later):
        qs[...] = q_ref[...].astype(BF16)
        ks[...] = k_ref[...].astype(BF16)
        vs[...] = v_ref[...].astype(BF16)
        row, col = _iotas(t)
        later[...] = (row > col).astype(BF16)

        def qblock(i, carry):
            r0 = pl.multiple_of(i * t, t)

            def tile(j, state, valid):
                c0 = pl.multiple_of(j * t, t)
                cols = [slice(hh * hd, (hh + 1) * hd) for hh in range(nh)]
                logs = [_sb_logs(_dot_nt(qs[pl.ds(r0, t), cs], ks[pl.ds(c0, t), cs]), scale, valid) for cs in cols]
                scans = [_dot(logs[hh][1], later[...]) for hh in range(nh)]
                ws = []
                for hh in range(nh):
                    w = jnp.exp(logs[hh][0].astype(F32) + scans[hh] + state[hh][1])
                    if valid is not None:
                        w = jnp.where(valid, w, 0.0)
                    ws.append(w.astype(BF16))
                pv = [_dot(ws[hh], vs[pl.ds(c0, t), cols[hh]]) for hh in range(nh)]
                totals = [scans[hh][:, 0:1] + logs[hh][1][:, 0:1].astype(F32) for hh in range(nh)]
                return tuple((state[hh][0] + pv[hh], state[hh][1] + totals[hh]) for hh in range(nh))

            causal = col < row
            zero = (jnp.zeros((t, hd), F32), jnp.zeros((t, 1), F32))
            state = tile(i, (zero,) * nh, causal)
            state = lax.fori_loop(0, i, lambda jj, st: tile(i - 1 - jj, st, None), state)
            for hh in range(nh):
                acc, run = state[hh]
                cs = slice(hh * hd, (hh + 1) * hd)
                o_ref[pl.ds(r0, t), cs] = acc
                tot_ref[hh, pl.ds(r0, t), :] = run
                sz, _ = _silu(zb_ref[pl.ds(r0, t), cs])
                yb_ref[pl.ds(r0, t), cs] = (acc * sz).astype(BF16)
            return carry

        lax.fori_loop(0, nblk, qblock, 0)

    col0 = d // wide
    seg = lambda k: pl.BlockSpec((seq, wide), lambda b, h: (b, k * col0 + h))
    return pl.pallas_call(
        body, name="sb_fwd", grid=(batch, heads // nh),
        in_specs=[seg(3), seg(4), seg(5), seg(6)],
        out_specs=[pl.BlockSpec((seq, wide), lambda b, h: (b, h))] * 2 + [
            pl.BlockSpec((nh, seq, 1), lambda b, h: (b * (heads // nh) + h, 0, 0))],
        out_shape=[SDS((batch * seq, d), BF16), SDS((batch * seq, d), F32), SDS((batch * heads, seq, 1), F32)],
        scratch_shapes=[pltpu.VMEM((seq, wide), BF16)] * 3 + [pltpu.VMEM((t, t), BF16)],
        compiler_params=_params(("parallel", "parallel")),
    )(proj, proj, proj, proj)


def _tail(x2d, tgt, ya, yb, proj, w_oa, w_ob, w_out, norm_final):
    n, d = x2d.shape
    e = proj.shape[1]
    tm = _tile(n, 256)

    def body(x_ref, t_ref, ya_ref, yb_ref, ga_ref, gb_ref, woa_ref, wob_ref, wout_ref, gf_ref,
             dproj_ref, dx2_ref, dya_ref, dyb_ref, mrg_ref, dpa_ref, dpb_ref, loss_ref, dgf_ref, dgb_s):
        i, kk = pl.program_id(0), pl.program_id(1)

        @pl.when(jnp.logical_and(i == 0, kk == 0))
        def _():
            loss_ref[...] = jnp.zeros_like(loss_ref)
            dgf_ref[...] = jnp.zeros_like(dgf_ref)

        @pl.when(kk == 0)
        def _():
            pa = _dot(ya_ref[...], woa_ref[...])
            pb = _dot(yb_ref[...], wob_ref[...])
            sa = _sigmoid(ga_ref[...])
            sb = _sigmoid(gb_ref[...])
            merged = (sa * pa + sb * pb).astype(BF16)
            mrg_ref[...] = merged
            x2 = x_ref[...] + _dot(merged, wout_ref[...])
            r2 = _rms_scale(x2)
            xh = x2 * r2
            gf = gf_ref[...]
            diff = xh * gf - t_ref[...]
            loss_ref[...] += jnp.sum(diff * diff, axis=0, keepdims=True) * (0.5 / d)
            dy = diff * (1.0 / d)
            dgf_ref[...] += jnp.sum(dy * xh, axis=0, keepdims=True)
            dxh = dy * gf
            dx2 = r2 * (dxh - xh * jnp.mean(dxh * xh, axis=-1, keepdims=True))
            dx2_ref[...] = dx2
            dm = _dot_nt(dx2.astype(BF16), wout_ref[...])
            dpa = (dm * sa).astype(BF16)
            dpb = (dm * sb).astype(BF16)
            dpa_ref[...] = dpa
            dpb_ref[...] = dpb
            dproj_ref[...] = (dm * pa * (sa * (1.0 - sa))).astype(BF16)
            dgb_s[...] = (dm * pb * (sb * (1.0 - sb))).astype(BF16)
            dya_ref[...] = _dot_nt(dpa, woa_ref[...])
            dyb_ref[...] = _dot_nt(dpb, wob_ref[...])

        @pl.when(kk == 1)
        def _():
            dproj_ref[...] = dgb_s[...]

    rows = lambda k=0: pl.BlockSpec((tm, d), lambda i, kk: (i, k))
    full = pl.BlockSpec((d, d), lambda i, kk: (0, 0))
    vec = pl.BlockSpec((1, d), lambda i, kk: (0, 0))
    return pl.pallas_call(
        body, name="tail", grid=(n // tm, 2),
        in_specs=[rows(), rows(), rows(), rows(), rows(7), rows(8), full, full, full, vec],
        out_specs=[pl.BlockSpec((tm, d), lambda i, kk: (i, 7 + kk)),
                   rows(), rows(), rows(), rows(), rows(), rows(), vec, vec],
        out_shape=[SDS((n, e), BF16), SDS((n, d), F32), SDS((n, d), F32), SDS((n, d), F32),
                   SDS((n, d), BF16), SDS((n, d), BF16), SDS((n, d), BF16),
                   SDS((1, d), F32), SDS((1, d), F32)],
        scratch_shapes=[pltpu.VMEM((tm, d), BF16)],
        compiler_params=_params(("arbitrary", "arbitrary")),
    )(x2d, tgt, ya, yb, proj, proj, w_oa, w_ob, w_out, norm_final)


def _tn_matmul(a, b, name):
    n, p = a.shape
    q = b.shape[1]
    tk = _tile(n, 512)
    nk = n // tk

    def body(a_ref, b_ref, o_ref, acc):
        k = pl.program_id(0)

        @pl.when(k == 0)
        def _():
            acc[...] = jnp.zeros_like(acc)

        acc[...] += _dot_tn(a_ref[...], b_ref[...].astype(BF16))

        @pl.when(k == nk - 1)
        def _():
            o_ref[...] = acc[...].astype(BF16)

    return pl.pallas_call(
        body, name=name, grid=(nk,),
        in_specs=[pl.BlockSpec((tk, p), lambda k: (k, 0)), pl.BlockSpec((tk, q), lambda k: (k, 0))],
        out_specs=pl.BlockSpec((p, q), lambda k: (0, 0)),
        out_shape=SDS((p, q), BF16),
        scratch_shapes=[pltpu.VMEM((p, q), F32)],
        compiler_params=_params(("arbitrary",)),
    )(a, b)


def _sb_bwd(proj, o, dyb, tot, dproj, batch, seq, d, hd):
    heads = d // hd
    t = _tile(seq, SB_TILE)
    scale = hd ** -0.5
    nblk = seq // t
    nh = SB_HEADS
    wide = nh * hd

    def compute(q_ref, k_ref, v_ref, zb_ref, o_ref, dyb_ref, tot_ref, qs, ks, vs, dos, res, upto, before):
        qs[...] = q_ref[...].astype(BF16)
        ks[...] = k_ref[...].astype(BF16)
        vs[...] = v_ref[...].astype(BF16)
        sz, dsz = _silu(zb_ref[...])
        dyb_v = dyb_ref[...]
        dos[...] = (dyb_v * sz).astype(BF16)
        res[3] = dyb_v * o_ref[...] * dsz
        res[1] = jnp.zeros((seq, wide), F32)
        res[2] = jnp.zeros((seq, wide), F32)
        row, col = _iotas(t)
        upto[...] = (row <= col).astype(BF16)
        before[...] = (row < col).astype(BF16)

        def qblock(i, carry):
            r0 = pl.multiple_of(i * t, t)

            def tile(j, state, valid):
                c0 = pl.multiple_of(j * t, t)
                hs = range(nh)
                cols = [slice(hh * hd, (hh + 1) * hd) for hh in hs]
                q_i = [qs[pl.ds(r0, t), cs] for cs in cols]
                k_j = [ks[pl.ds(c0, t), cs] for cs in cols]
                do_i = [dos[pl.ds(r0, t), cs] for cs in cols]
                logs = [_sb_logs(_dot_nt(q_i[hh], k_j[hh]), scale, valid) for hh in hs]
                dw = [_dot_nt(do_i[hh], vs[pl.ds(c0, t), cols[hh]]) for hh in hs]
                scans = [_dot(logs[hh][1], upto[...]) for hh in hs]
                ws, gs = [], []
                for hh in hs:
                    left = tot_ref[hh, pl.ds(r0, t), :] - state[hh][1]
                    w = jnp.exp(logs[hh][0].astype(F32) + (left - scans[hh]))
                    if valid is not None:
                        w = jnp.where(valid, w, 0.0)
                    ws.append(w.astype(BF16))
                    gs.append((dw[hh] * w).astype(BF16))
                gscans = [_dot(gs[hh], before[...]) for hh in hs]
                dzs = []
                for hh in hs:
                    beta = jnp.exp(logs[hh][0]).astype(F32)
                    g = gs[hh].astype(F32)
                    dz = (g - (g + gscans[hh] + state[hh][2]) * beta) * scale
                    if valid is not None:
                        dz = jnp.where(valid, dz, 0.0)
                    dzs.append(dz.astype(BF16))
                for hh in hs:
                    res[2, pl.ds(c0, t), cols[hh]] += _dot_tn(ws[hh], do_i[hh])
                for hh in hs:
                    res[1, pl.ds(c0, t), cols[hh]] += _dot_tn(dzs[hh], q_i[hh])
                dqs = [_dot(dzs[hh], k_j[hh]) for hh in hs]
                last = slice(t - 1, t)
                return tuple((state[hh][0] + dqs[hh],
                              state[hh][1] + scans[hh][:, last],
                              state[hh][2] + gscans[hh][:, last] + gs[hh][:, last].astype(F32)) for hh in hs)

            zero = jnp.zeros((t, 1), F32)
            state = lax.fori_loop(0, i, lambda j, st: tile(j, st, None),
                                  ((jnp.zeros((t, hd), F32), zero, zero),) * nh)
            state = tile(i, state, col < row)
            for hh in range(nh):
                res[0, pl.ds(r0, t), hh * hd:(hh + 1) * hd] = state[hh][0]
            return carry

        lax.fori_loop(0, nblk, qblock, 0)

    def body(q_ref, k_ref, v_ref, zb_ref, o_ref, dyb_ref, tot_ref, dproj_in, out_ref,
             qs, ks, vs, dos, res, upto, before):
        del dproj_in
        kk = pl.program_id(2)

        @pl.when(kk == 0)
        def _():
            compute(q_ref, k_ref, v_ref, zb_ref, o_ref, dyb_ref, tot_ref, qs, ks, vs, dos, res, upto, before)

        out_ref[...] = res[kk].astype(BF16)

    col0 = d // wide
    seg = lambda k: pl.BlockSpec((seq, wide), lambda b, h, kk: (b, k * col0 + h))
    head = pl.BlockSpec((seq, wide), lambda b, h, kk: (b, h))
    return pl.pallas_call(
        body, name="sb_bwd", grid=(batch, heads // nh, 4),
        in_specs=[seg(3), seg(4), seg(5), seg(6), head, head,
                  pl.BlockSpec((nh, seq, 1), lambda b, h, kk: (b * (heads // nh) + h, 0, 0)),
                  pl.BlockSpec(memory_space=pl.ANY)],
        out_specs=pl.BlockSpec((seq, wide), lambda b, h, kk: (b, (3 + kk) * col0 + h)),
        out_shape=SDS(dproj.shape, dproj.dtype),
        input_output_aliases={7: 0},
        scratch_shapes=[pltpu.VMEM((seq, wide), BF16)] * 4 + [pltpu.VMEM((4, seq, wide), F32)] + [
            pltpu.VMEM((t, t), BF16)] * 2,
        compiler_params=_params(("arbitrary", "arbitrary", "arbitrary")),
    )(proj, proj, proj, proj, o, dyb, tot, dproj)


def _branch_a_bwd(proj, dya, norm_v, w_s, b_col, dproj):
    n = proj.shape[0]
    d = norm_v.shape[1]
    groups, chunk, _ = w_s.shape
    tr = _tile(n, 2 * chunk)

    def body(u_ref, v_ref, z_ref, dya_ref, gv_ref, ws_ref, b_ref, dproj_in,
             out_ref, dws_ref, dbias_ref, dgv_ref, vn_s, dmix_s, dvn_s, db_ref):
        del dproj_in

        @pl.when(pl.program_id(0) == 0)
        def _():
            dws_ref[...] = jnp.zeros_like(dws_ref)
            db_ref[...] = jnp.zeros_like(db_ref)
            dgv_ref[...] = jnp.zeros_like(dgv_ref)

        row, col = _iotas(chunk)
        tril = col <= row
        u, v, z, dya_v = u_ref[...], v_ref[...], z_ref[...], dya_ref[...]
        gv = gv_ref[...]
        vg, dvg_dv = _gelu(v)
        r = _rms_scale(vg)
        vh = vg * r
        vn_s[...] = (vh * gv).astype(BF16)
        ug, dug_du = _gelu(u)
        sz, dsz = _silu(z)
        dmix_s[...] = dya_v * ug * sz
        for g in range(groups):
            wm = jnp.where(tril, ws_ref[g], 0.0).astype(BF16)
            cs = slice(g * chunk, (g + 1) * chunk)
            for c in range(tr // chunk):
                rs = slice(c * chunk, (c + 1) * chunk)
                vn = vn_s[rs, cs]
                mixed = _dot(wm, vn) + b_ref[g]
                dmix = dmix_s[rs, cs]
                dmix16 = dmix.astype(BF16)
                dws_ref[g] += _dot_nt(dmix16, vn)
                db_ref[g] += dmix
                dvn_s[rs, cs] = _dot_tn(wm, dmix16)
                t_u = dya_v[rs, cs] * mixed
                out_ref[rs, g * chunk:(g + 1) * chunk] = (t_u * sz[rs, cs] * dug_du[rs, cs]).astype(BF16)
                out_ref[rs, 2 * d + g * chunk:2 * d + (g + 1) * chunk] = (t_u * ug[rs, cs] * dsz[rs, cs]).astype(BF16)
        dvn = dvn_s[...]
        dgv_ref[...] += jnp.sum(dvn * vh, axis=0, keepdims=True)
        dvh = dvn * gv
        dvg = r * (dvh - vh * jnp.mean(dvh * vh, axis=-1, keepdims=True))
        out_ref[:, d:2 * d] = (dvg * dvg_dv).astype(BF16)

        @pl.when(pl.program_id(0) == n // tr - 1)
        def _():
            for g in range(groups):
                dbias_ref[g:g + 1, :] = jnp.sum(db_ref[g].T, axis=0, keepdims=True)

    seg = lambda k: pl.BlockSpec((tr, d), lambda i: (i, k))
    return pl.pallas_call(
        body, name="branch_a_bwd", grid=(n // tr,),
        in_specs=[seg(0), seg(1), seg(2), seg(0),
                  pl.BlockSpec((1, d), lambda i: (0, 0)),
                  pl.BlockSpec((groups, chunk, chunk), lambda i: (0, 0, 0)),
                  pl.BlockSpec((groups, chunk, 1), lambda i: (0, 0, 0)),
                  pl.BlockSpec(memory_space=pl.ANY)],
        out_specs=[pl.BlockSpec((tr, 3 * d), lambda i: (i, 0)),
                   pl.BlockSpec((groups, chunk, chunk), lambda i: (0, 0, 0)),
                   pl.BlockSpec((groups, chunk), lambda i: (0, 0)),
                   pl.BlockSpec((1, d), lambda i: (0, 0))],
        out_shape=[SDS(dproj.shape, dproj.dtype), SDS((groups, chunk, chunk), F32),
                   SDS((groups, chunk), F32), SDS((1, d), F32)],
        input_output_aliases={7: 0},
        scratch_shapes=[pltpu.VMEM((tr, d), BF16), pltpu.VMEM((tr, d), F32), pltpu.VMEM((tr, d), F32),
                        pltpu.VMEM((groups, chunk, chunk), F32)],
        compiler_params=_params(("arbitrary",)),
    )(proj, proj, proj, dya, norm_v, w_s, b_col, dproj)


def _dx(dproj, wg_in, x2d, dx2, norm_in):
    n, d = x2d.shape
    nsh, _, esh = wg_in.shape
    tm = _tile(n, 1024)

    def body(dp_ref, w_ref, x_ref, dx2_ref, g_ref, gx_ref, dg_ref, acc):
        i, k = pl.program_id(0), pl.program_id(1)

        @pl.when(jnp.logical_and(i == 0, k == 0))
        def _():
            dg_ref[...] = jnp.zeros_like(dg_ref)

        @pl.when(k == 0)
        def _():
            acc[...] = jnp.zeros_like(acc)

        acc[...] += _dot_nt(dp_ref[...], w_ref[0])

        @pl.when(k == nsh - 1)
        def _():
            dh = acc[...]
            x = x_ref[...]
            r = _rms_scale(x)
            xh = x * r
            dg_ref[...] += jnp.sum(dh * xh, axis=0, keepdims=True)
            dxh = dh * g_ref[...]
            gx_ref[...] = dx2_ref[...] + r * (dxh - xh * jnp.mean(dxh * xh, axis=-1, keepdims=True))

    rows = pl.BlockSpec((tm, d), lambda i, k: (i, 0))
    vec = pl.BlockSpec((1, d), lambda i, k: (0, 0))
    return pl.pallas_call(
        body, name="dx", grid=(n // tm, nsh),
        in_specs=[pl.BlockSpec((tm, esh), lambda i, k: (i, k)),
                  pl.BlockSpec((1, d, esh), lambda i, k: (k, 0, 0)), rows, rows, vec],
        out_specs=[rows, vec],
        out_shape=[SDS((n, d), F32), SDS((1, d), F32)],
        scratch_shapes=[pltpu.VMEM((tm, d), F32)],
        compiler_params=_params(("arbitrary", "arbitrary")),
    )(dproj, wg_in, x2d, dx2, norm_in)


def _adamw_outputs(g_ref, d_ref, m_ref, v_ref, g, w, m, v):
    delta, m2, v2 = _adamw(w, g, m, v)
    g_ref[...] = g
    d_ref[...] = delta
    m_ref[...] = m2
    v_ref[...] = v2


def _reduce_adamw(slots, w, m, v, name):
    _, r, c = slots.shape
    tr = _tile(r, 128)

    def body(s_ref, w_ref, m_ref, v_ref, g_out, d_out, m_out, v_out):
        g = s_ref[0].astype(F32)
        for k in range(1, N_DEV):
            g = g + s_ref[k].astype(F32)
        _adamw_outputs(g_out, d_out, m_out, v_out, g, w_ref[...], m_ref[...], v_ref[...])

    blk = pl.BlockSpec((tr, c), lambda i: (i, 0))
    return pl.pallas_call(
        body, name=name, grid=(r // tr,),
        in_specs=[pl.BlockSpec((N_DEV, tr, c), lambda i: (0, i, 0)), blk, blk, blk],
        out_specs=[blk] * 4,
        out_shape=[SDS((r, c), F32)] * 4,
        compiler_params=_params(("parallel",)),
    )(slots, w, m, v)


def _adamw_small(g, w, m, v, name):
    def body(g_ref, w_ref, m_ref, v_ref, g_out, d_out, m_out, v_out):
        _adamw_outputs(g_out, d_out, m_out, v_out, g_ref[...], w_ref[...], m_ref[...], v_ref[...])

    return pl.pallas_call(
        body, name=name,
        out_shape=[SDS(g.shape, F32)] * 4,
        in_specs=[pl.BlockSpec(memory_space=pltpu.VMEM)] * 4,
        out_specs=[pl.BlockSpec(memory_space=pltpu.VMEM)] * 4,
    )(g, w, m, v)


def kernel(x, norm_in, w_in, norm_v, w_s, b_s, w_o_gmlp, w_o_sb, w_out, norm_final, loss_target, m_norm_in, m_w_in, m_norm_v, m_w_s, m_b_s, m_w_o_gmlp, m_w_o_sb, m_w_out, m_norm_final, v_norm_in, v_w_in, v_norm_v, v_w_s, v_b_s, v_w_o_gmlp, v_w_o_sb, v_w_out, v_norm_final):
    batch, seq, d = x.shape
    n = batch * seq
    groups, chunk = w_s.shape[1], w_s.shape[2]
    hd = LANE
    x2d = x.reshape(n, d)
    tgt = loss_target.reshape(n, d)
    b_col = b_s[0].reshape(groups, chunk, 1)
    norm_final2 = norm_final.reshape(1, d)

    wg_in, wg_oa, wg_ob, wg_out = _gather_weights([w_in[0], w_o_gmlp[0], w_o_sb[0], w_out[0]])
    rsh = wg_oa.shape[1]
    wf_oa, wf_ob, wf_out = (w.reshape(N_DEV * rsh, d) for w in (wg_oa, wg_ob, wg_out))

    proj, h = _in_proj(x2d, norm_in, wg_in)
    ya = _branch_a_fwd(proj, norm_v, w_s[0], b_col)
    yb, o, sb_tot = _sb_fwd(proj, batch, seq, d, hd)
    dproj, dx2, dya, dyb, merged, dpa, dpb, loss_vec, dgf = _tail(
        x2d, tgt, ya, yb, proj, wf_oa, wf_ob, wf_out, norm_final2)
    gp_oa = _tn_matmul(ya, dpa, "dw_o_gmlp")
    gp_ob = _tn_matmul(yb, dpb, "dw_o_sb")
    gp_out = _tn_matmul(merged, dx2, "dw_out")
    dproj = _sb_bwd(proj, o, dyb, sb_tot, dproj, batch, seq, d, hd)
    dproj, gp_ws, gp_b, gp_nv = _branch_a_bwd(proj, dya, norm_v, w_s[0], b_col, dproj)
    grad_x, gp_nin = _dx(dproj, wg_in, x2d, dx2, norm_in)

    slab = lambda a: a.reshape(d // LANE, LANE)
    gc = groups * chunk
    packed = jnp.concatenate(
        [gp_ws.reshape(gc, chunk), gp_b, slab(gp_nin), slab(gp_nv), slab(dgf), slab(loss_vec)], axis=0)
    my_slot = _slot(_me()).astype(jnp.int32).reshape(1)
    s_win, s_oa, s_ob, s_out, packs = _dw_in_exchange(
        h, dproj, my_slot, [g.reshape(N_DEV, rsh, d) for g in (gp_oa, gp_ob, gp_out)], packed)
    tot, loss_slab = _finish_small(packs, groups, chunk)
    ns = d // LANE
    g_ws = tot[:gc]
    g_b = tot[gc:gc + groups]
    g_nin, g_nv, g_nf = (tot[gc + groups + k * ns:gc + groups + (k + 1) * ns] for k in range(3))
    loss = loss_slab[0, 0]

    res = {}
    res["w_in"] = _reduce_adamw(s_win, w_in[0], m_w_in[0], v_w_in[0], "adamw_w_in")
    res["w_o_gmlp"] = _reduce_adamw(s_oa, w_o_gmlp[0], m_w_o_gmlp[0], v_w_o_gmlp[0], "adamw_w_o_gmlp")
    res["w_o_sb"] = _reduce_adamw(s_ob, w_o_sb[0], m_w_o_sb[0], v_w_o_sb[0], "adamw_w_o_sb")
    res["w_out"] = _reduce_adamw(s_out, w_out[0], m_w_out[0], v_w_out[0], "adamw_w_out")
    res["norm_in"] = _adamw_small(g_nin, slab(norm_in), slab(m_norm_in), slab(v_norm_in), "adamw_norm_in")
    res["norm_v"] = _adamw_small(g_nv, slab(norm_v), slab(m_norm_v), slab(v_norm_v), "adamw_norm_v")
    res["norm_final"] = _adamw_small(g_nf, slab(norm_final), slab(m_norm_final), slab(v_norm_final), "adamw_norm_final")
    res["w_s"] = _adamw_small(g_ws, w_s.reshape(gc, chunk), m_w_s.reshape(gc, chunk), v_w_s.reshape(gc, chunk), "adamw_w_s")
    res["b_s"] = _adamw_small(g_b, b_s[0], m_b_s[0], v_b_s[0], "adamw_b_s")

    shapes = {"norm_in": norm_in.shape, "w_in": w_in.shape, "norm_v": norm_v.shape, "w_s": w_s.shape,
              "b_s": b_s.shape, "w_o_gmlp": w_o_gmlp.shape, "w_o_sb": w_o_sb.shape, "w_out": w_out.shape,
              "norm_final": norm_final.shape}
    names = list(shapes)
    outs = [loss, grad_x.reshape(batch, seq, d)]
    for kind in range(4):
        outs += [res[name][kind].reshape(shapes[name]) for name in names]
    return tuple(outs)
```

```python
import functools
import math

import jax
import jax.numpy as jnp
from jax import lax
from jax.experimental import pallas as pl
from jax.experimental.pallas import tpu as pltpu

F32 = jnp.float32
BF16 = jnp.bfloat16
SDS = jax.ShapeDtypeStruct
MESH_ID = pl.DeviceIdType.MESH

N_DEV = 8
LANE = 128
SUBLANE = 8
VMEM_LIMIT = 56 * 1024 * 1024
SB_TILE = 512
SB_TILE_BWD = 256
SB_SCAN = 256
SB_HEADS = 2
MASKED_LOG = -1e30
RMS_EPS = 1e-6

ADAM_LR = 0.001
ADAM_B1 = 0.9
ADAM_B2 = 0.999
ADAM_EPS = 1e-08
ADAM_WD = 0.01
ADAM_STEP = 10

NT_DIMS = (((1,), (1,)), ((), ()))
TN_DIMS = (((0,), (0,)), ((), ()))


def _params(semantics=None):
    return pltpu.CompilerParams(dimension_semantics=semantics, vmem_limit_bytes=VMEM_LIMIT)


def _tile(n, preferred):
    t = min(n, preferred)
    assert n % t == 0, (n, t)
    return t


def _sigmoid(x):
    return 1.0 / (1.0 + jnp.exp(-x))


def _silu(x):
    s = _sigmoid(x)
    return x * s, s * (1.0 + x * (1.0 - s))


def _gelu(x):
    k = math.sqrt(2.0 / math.pi)
    x2 = x * x
    t = jnp.tanh(k * (x + 0.044715 * (x * x2)))
    cdf = 0.5 * (1.0 + t)
    return x * cdf, cdf + 0.5 * x * (1.0 - t * t) * (k * (1.0 + 3.0 * 0.044715 * x2))


def _rms_scale(x):
    return lax.rsqrt(jnp.mean(x * x, axis=-1, keepdims=True) + RMS_EPS)


def _iotas(n):
    return (lax.broadcasted_iota(jnp.int32, (n, n), 0), lax.broadcasted_iota(jnp.int32, (n, n), 1))


def _adamw(w, g, m, v):
    m = ADAM_B1 * m + (1.0 - ADAM_B1) * g
    v = ADAM_B2 * v + (1.0 - ADAM_B2) * (g * g)
    m_hat = m / (1.0 - ADAM_B1 ** ADAM_STEP)
    v_hat = v / (1.0 - ADAM_B2 ** ADAM_STEP)
    delta = -ADAM_LR * (m_hat / (jnp.sqrt(v_hat) + ADAM_EPS) + ADAM_WD * w)
    return delta, m, v


def _dot(a, b):
    return jnp.dot(a, b, preferred_element_type=F32)


def _dot_nt(a, b):
    return lax.dot_general(a, b, NT_DIMS, preferred_element_type=F32)


def _dot_tn(a, b):
    return lax.dot_general(a, b, TN_DIMS, preferred_element_type=F32)


def _sb_logs(raw, scale, valid):
    z = (raw * scale).astype(BF16)
    log_beta = jnp.minimum(z, 0) - jnp.log(1 + jnp.exp(-jnp.abs(z)))
    log_rest = log_beta - z
    if valid is not None:
        log_beta = jnp.where(valid, log_beta, MASKED_LOG)
        log_rest = jnp.where(valid, log_rest, 0)
    return log_beta, log_rest


def _me():
    return lax.axis_index("x"), lax.axis_index("y"), lax.axis_index("c")


def _slot(p):
    return 4 * p[0] + 2 * p[1] + p[2]


def _peer(me, k):
    flips = ((k >> 2) & 1, (k >> 1) & 1, k & 1)
    return tuple(1 - a if f else a for a, f in zip(me, flips))


def _gather_weights(shards):
    n = len(shards)

    def body(*refs):
        ins, outs, stage = refs[:n], refs[n:2 * n], refs[2 * n:3 * n]
        send_sems, recv_sems, local_sems = refs[3 * n:]
        x, y, c = _me()
        me, sibling = (x, y, c), (x, y, 1 - c)
        chips = [(1 - x, y), (x, 1 - y), (1 - x, 1 - y)]

        def copy(a, k, block, to, src=None):
            dst = outs[a].at[_slot(block)]
            return pltpu.make_async_remote_copy(
                src_ref=dst if src is None else src, dst_ref=dst,
                send_sem=send_sems.at[7 * a + k], recv_sem=recv_sems.at[7 * a + k],
                device_id=to, device_id_type=MESH_ID)

        started = []
        for a in range(n):
            stage[a][...] = ins[a][...].astype(BF16)
            mine = pltpu.make_async_copy(stage[a], outs[a].at[_slot(me)], local_sems.at[a])
            mine.start()
            started.append(mine)
        sends = []
        for a in range(n):
            sends.append(copy(a, 0, me, sibling, src=stage[a]))
            sends += [copy(a, 1 + j, me, (*chip, c), src=stage[a]) for j, chip in enumerate(chips)]
        for cp in sends:
            cp.start()
        for a in range(n):
            for j, chip in enumerate(chips):
                copy(a, 1 + j, (*chip, c), me).wait_recv()
                passed = copy(a, 4 + j, (*chip, c), sibling)
                passed.start()
                sends.append(passed)
        for a in range(n):
            copy(a, 0, sibling, me).wait_recv()
            for j, chip in enumerate(chips):
                copy(a, 4 + j, (*chip, 1 - c), me).wait_recv()
        for cp in sends:
            cp.wait_send()
        for mine in started:
            mine.wait()

    return pl.pallas_call(
        body, name="gather_weights",
        out_shape=[SDS((N_DEV,) + s.shape, BF16) for s in shards],
        in_specs=[pl.BlockSpec(memory_space=pltpu.VMEM)] * n,
        out_specs=[pl.BlockSpec(memory_space=pl.ANY)] * n,
        scratch_shapes=[pltpu.VMEM(s.shape, BF16) for s in shards] + [
            pltpu.SemaphoreType.DMA((7 * n,)), pltpu.SemaphoreType.DMA((7 * n,)),
            pltpu.SemaphoreType.DMA((n,))],
        compiler_params=pltpu.CompilerParams(vmem_limit_bytes=VMEM_LIMIT),
    )(*shards)


def _dw_in_exchange(h, dproj, my_slot, stacks, packed):
    n, d = h.shape
    esh = dproj.shape[1] // N_DEV
    tk = _tile(n, 512)
    nk = n // tk
    ns = len(stacks)
    last_j = N_DEV - 1

    def body(me_ref, h_ref, dp_ref, *refs):
        del me_ref
        st_in, pk_in = refs[:ns], refs[ns]
        win_out, st_out, pk_out = refs[ns + 1], refs[ns + 2:2 * ns + 2], refs[2 * ns + 2]
        acc, sendbuf, win_send, win_recv, send_sems, recv_sems, local_sems = refs[2 * ns + 3:]
        j, k = pl.program_id(0), pl.program_id(1)
        me = _me()
        mine = _slot(me)

        def ready_copies():
            local = [pltpu.make_async_copy(st_in[a].at[mine], st_out[a].at[mine], local_sems.at[a])
                     for a in range(ns)]
            local.append(pltpu.make_async_copy(pk_in, pk_out.at[mine], local_sems.at[ns]))
            remote = []
            for kk in range(1, N_DEV):
                peer = _peer(me, kk)
                for a in range(ns):
                    remote.append(pltpu.make_async_remote_copy(
                        src_ref=st_in[a].at[_slot(peer)], dst_ref=st_out[a].at[mine],
                        send_sem=send_sems.at[(ns + 1) * (kk - 1) + a],
                        recv_sem=recv_sems.at[(ns + 1) * (kk - 1) + a],
                        device_id=peer, device_id_type=MESH_ID))
                remote.append(pltpu.make_async_remote_copy(
                    src_ref=pk_in, dst_ref=pk_out.at[mine],
                    send_sem=send_sems.at[(ns + 1) * (kk - 1) + ns],
                    recv_sem=recv_sems.at[(ns + 1) * (kk - 1) + ns],
                    device_id=peer, device_id_type=MESH_ID))
            return local, remote

        def shard_copy(jj):
            owner = (mine + 1 + jj) % N_DEV
            return pltpu.make_async_remote_copy(
                src_ref=sendbuf.at[jj % 2], dst_ref=win_out.at[mine],
                send_sem=win_send.at[jj % 2], recv_sem=win_recv.at[mine],
                device_id=(owner // 4, (owner // 2) % 2, owner % 2), device_id_type=MESH_ID)

        def own_copy():
            return pltpu.make_async_copy(sendbuf.at[last_j % 2], win_out.at[mine], local_sems.at[ns + 1])

        @pl.when(jnp.logical_and(j == 0, k == 0))
        def _():
            local, remote = ready_copies()
            for cp in local + remote:
                cp.start()

        @pl.when(k == 0)
        def _():
            acc[...] = jnp.zeros_like(acc)

        acc[...] += _dot_tn(h_ref[...], dp_ref[...])

        @pl.when(k == nk - 1)
        def _():
            @pl.when(j >= 2)
            def _():
                shard_copy(j - 2).wait_send()

            sendbuf[j % 2] = acc[...].astype(BF16)

            @pl.when(j < last_j)
            def _():
                shard_copy(j).start()

            @pl.when(j == last_j)
            def _():
                own_copy().start()
                shard_copy(last_j - 1).wait_send()
                own_copy().wait()
                for src in range(N_DEV):
                    @pl.when(src != mine)
                    def _():
                        landed = win_out.at[src]
                        pltpu.make_async_remote_copy(
                            src_ref=landed, dst_ref=landed, send_sem=win_send.at[0], recv_sem=win_recv.at[src],
                            device_id=me, device_id_type=MESH_ID).wait_recv()
                local, remote = ready_copies()
                for cp in remote:
                    cp.wait_send()
                idx = 0
                for kk in range(1, N_DEV):
                    peer = _slot(_peer(me, kk))
                    for a in range(ns + 1):
                        landed = pk_out.at[peer] if a == ns else st_out[a].at[peer]
                        pltpu.make_async_remote_copy(
                            src_ref=landed, dst_ref=landed, send_sem=send_sems.at[idx], recv_sem=recv_sems.at[idx],
                            device_id=me, device_id_type=MESH_ID).wait_recv()
                        idx += 1
                for cp in local:
                    cp.wait()

    any_spec = pl.BlockSpec(memory_space=pl.ANY)
    n_ready = 7 * (ns + 1)
    grid_spec = pltpu.PrefetchScalarGridSpec(
        num_scalar_prefetch=1, grid=(N_DEV, nk),
        in_specs=[pl.BlockSpec((tk, d), lambda j, k, me: (k, 0)),
                  pl.BlockSpec((tk, esh), lambda j, k, me: (k, (me[0] + 1 + j) % N_DEV))] + [any_spec] * (ns + 1),
        out_specs=[any_spec] * (ns + 2),
        scratch_shapes=[pltpu.VMEM((d, esh), F32), pltpu.VMEM((2, d, esh), BF16),
                        pltpu.SemaphoreType.DMA((2,)), pltpu.SemaphoreType.DMA((N_DEV,)),
                        pltpu.SemaphoreType.DMA((n_ready,)), pltpu.SemaphoreType.DMA((n_ready,)),
                        pltpu.SemaphoreType.DMA((ns + 2,))])
    return pl.pallas_call(
        body, name="dw_in_exchange", grid_spec=grid_spec,
        out_shape=[SDS((N_DEV, d, esh), BF16)] + [SDS(s.shape, s.dtype) for s in stacks] + [
            SDS((N_DEV,) + packed.shape, packed.dtype)],
        compiler_params=_params(("arbitrary", "arbitrary")),
    )(my_slot, h, dproj, *stacks, packed)


def _finish_small(packs, groups, chunk):
    rows = packs.shape[1]
    gc = groups * chunk

    def body(p_ref, sum_ref, loss_ref):
        row, col = _iotas(chunk)
        tril = col <= row
        for g in range(groups):
            rs = slice(g * chunk, (g + 1) * chunk)
            tot = p_ref[0, rs, :]
            for dev in range(1, N_DEV):
                tot = tot + p_ref[dev, rs, :]
            sum_ref[rs, :] = jnp.where(tril, tot, 0.0)
        rs = slice(gc, rows)
        tot = p_ref[0, rs, :]
        for dev in range(1, N_DEV):
            tot = tot + p_ref[dev, rs, :]
        sum_ref[rs, :] = tot
        loss_ref[...] = jnp.full((SUBLANE, LANE), jnp.sum(tot[rows - gc - SUBLANE:, :]), F32)

    return pl.pallas_call(
        body, name="finish_small",
        out_shape=[SDS((rows, LANE), F32), SDS((SUBLANE, LANE), F32)],
        in_specs=[pl.BlockSpec(memory_space=pltpu.VMEM)],
        out_specs=[pl.BlockSpec(memory_space=pltpu.VMEM)] * 2,
        compiler_params=pltpu.CompilerParams(vmem_limit_bytes=VMEM_LIMIT),
    )(packs)


def _in_proj(x2d, norm_in, wg_in):
    n, d = x2d.shape
    nsh, _, esh = wg_in.shape
    tm = _tile(n, 1024)

    def body(x_ref, g_ref, w_ref, proj_ref, h_ref):
        @pl.when(pl.program_id(1) == 0)
        def _():
            x = x_ref[...]
            h_ref[...] = (x * _rms_scale(x) * g_ref[...]).astype(BF16)

        proj_ref[...] = _dot(h_ref[...], w_ref[0])

    return pl.pallas_call(
        body, name="in_proj", grid=(n // tm, nsh),
        in_specs=[pl.BlockSpec((tm, d), lambda i, j: (i, 0)),
                  pl.BlockSpec((1, d), lambda i, j: (0, 0)),
                  pl.BlockSpec((1, d, esh), lambda i, j: (j, 0, 0))],
        out_specs=[pl.BlockSpec((tm, esh), lambda i, j: (i, j)),
                   pl.BlockSpec((tm, d), lambda i, j: (i, 0))],
        out_shape=[SDS((n, nsh * esh), F32), SDS((n, d), BF16)],
        compiler_params=_params(("parallel", "arbitrary")),
    )(x2d, norm_in, wg_in)


def _branch_a_fwd(proj, norm_v, w_s, b_col):
    n = proj.shape[0]
    d = norm_v.shape[1]
    groups, chunk, _ = w_s.shape
    tr = _tile(n, 4 * chunk)

    def body(u_ref, v_ref, z_ref, gv_ref, ws_ref, b_ref, ya_ref, vn_s, pre_s):
        row, col = _iotas(chunk)
        tril = col <= row
        vg, _ = _gelu(v_ref[...])
        vn_s[...] = (vg * _rms_scale(vg) * gv_ref[...]).astype(BF16)
        ug, _ = _gelu(u_ref[...])
        sz, _ = _silu(z_ref[...])
        pre_s[...] = ug * sz
        for g in range(groups):
            wm = jnp.where(tril, ws_ref[g], 0.0).astype(BF16)
            cs = slice(g * chunk, (g + 1) * chunk)
            for c in range(tr // chunk):
                rs = slice(c * chunk, (c + 1) * chunk)
                mixed = _dot(wm, vn_s[rs, cs]) + b_ref[g]
                ya_ref[rs, cs] = (pre_s[rs, cs] * mixed).astype(BF16)

    seg = lambda k: pl.BlockSpec((tr, d), lambda i: (i, k))
    return pl.pallas_call(
        body, name="branch_a_fwd", grid=(n // tr,),
        in_specs=[seg(0), seg(1), seg(2),
                  pl.BlockSpec((1, d), lambda i: (0, 0)),
                  pl.BlockSpec((groups, chunk, chunk), lambda i: (0, 0, 0)),
                  pl.BlockSpec((groups, chunk, 1), lambda i: (0, 0, 0))],
        out_specs=pl.BlockSpec((tr, d), lambda i: (i, 0)),
        out_shape=SDS((n, d), BF16),
        scratch_shapes=[pltpu.VMEM((tr, d), BF16), pltpu.VMEM((tr, d), F32)],
        compiler_params=_params(("parallel",)),
    )(proj, proj, proj, norm_v, w_s, b_col)


def _sb_fwd(proj, batch, seq, d, hd):
    heads = d // hd
    t = _tile(seq, SB_TILE)
    sw = _tile(t, SB_SCAN)
    nb = t // sw
    scale = hd ** -0.5
    nblk = seq // t
    nh = SB_HEADS
    wide = nh * hd
    cols = [slice(hh * hd, (hh + 1) * hd) for hh in range(nh)]

    def body(q_ref, k_ref, v_ref, zb_ref, yb_ref, o_ref, tot_ref, qs, kts, vs, later, acc):
        qs[...] = q_ref[...].astype(BF16)
        vs[...] = v_ref[...].astype(BF16)
        for jb in range(nblk):
            kts[jb] = k_ref[jb * t:(jb + 1) * t, :].T.astype(BF16)
        row, col = _iotas(t)
        later[...] = (row[:sw, :sw] > col[:sw, :sw]).astype(BF16)

        def qblock(i, carry):
            r0 = pl.multiple_of(i * t, t)

            def tile(j, runs, valid):
                c0 = pl.multiple_of(j * t, t)
                logs = [_sb_logs(_dot(qs[pl.ds(r0, t), cs], kts[j, cs, :]), scale, valid) for cs in cols]
                scans = [_dot(jnp.concatenate([logs[hh][1][:, b * sw:(b + 1) * sw] for b in range(nb)], axis=0),
                              later[...]) for hh in range(nh)]
                new_runs = []
                for hh in range(nh):
                    after = runs[hh]
                    blocks = [None] * nb
                    for b in reversed(range(nb)):
                        ks_ = slice(b * sw, (b + 1) * sw)
                        inside = scans[hh][b * t:(b + 1) * t]
                        blocks[b] = jnp.exp(logs[hh][0][:, ks_].astype(F32) + inside + after).astype(BF16)
                        after = after + inside[:, 0:1] + logs[hh][1][:, b * sw:b * sw + 1].astype(F32)
                    new_runs.append(after)
                    pv = _dot(jnp.concatenate(blocks, axis=1), vs[pl.ds(c0, t), cols[hh]])
                    if valid is None:
                        acc[:, cols[hh]] += pv
                    else:
                        acc[:, cols[hh]] = pv
                return tuple(new_runs)

            runs = tile(i, (jnp.zeros((t, 1), F32),) * nh, col < row)
            runs = lax.fori_loop(0, i, lambda jj, rs: tile(i - 1 - jj, rs, None), runs)
            for hh in range(nh):
                out = acc[:, cols[hh]]
                o_ref[pl.ds(r0, t), cols[hh]] = out
                tot_ref[hh, pl.ds(r0, t), :] = runs[hh]
                sz, _ = _silu(zb_ref[pl.ds(r0, t), cols[hh]])
                yb_ref[pl.ds(r0, t), cols[hh]] = (out * sz).astype(BF16)
            return carry

        lax.fori_loop(0, nblk, qblock, 0)

    col0 = d // wide
    seg = lambda k: pl.BlockSpec((seq, wide), lambda b, h: (b, k * col0 + h))
    return pl.pallas_call(
        body, name="sb_fwd", grid=(batch, heads // nh),
        in_specs=[seg(3), seg(4), seg(5), seg(6)],
        out_specs=[pl.BlockSpec((seq, wide), lambda b, h: (b, h))] * 2 + [
            pl.BlockSpec((nh, seq, 1), lambda b, h: (b * (heads // nh) + h, 0, 0))],
        out_shape=[SDS((batch * seq, d), BF16), SDS((batch * seq, d), F32), SDS((batch * heads, seq, 1), F32)],
        scratch_shapes=[pltpu.VMEM((seq, wide), BF16), pltpu.VMEM((nblk, wide, t), BF16),
                        pltpu.VMEM((seq, wide), BF16), pltpu.VMEM((sw, sw), BF16), pltpu.VMEM((t, wide), F32)],
        compiler_params=_params(("parallel", "parallel")),
    )(proj, proj, proj, proj)


def _tail(x2d, tgt, ya, yb, proj, w_oa, w_ob, w_out, norm_final):
    n, d = x2d.shape
    e = proj.shape[1]
    tm = _tile(n, 256)

    def body(x_ref, t_ref, ya_ref, yb_ref, ga_ref, gb_ref, woa_ref, wob_ref, wout_ref, gf_ref,
             dproj_ref, dx2_ref, dya_ref, dyb_ref, mrg_ref, dpa_ref, dpb_ref, loss_ref, dgf_ref, dgb_s):
        i, kk = pl.program_id(0), pl.program_id(1)

        @pl.when(jnp.logical_and(i == 0, kk == 0))
        def _():
            loss_ref[...] = jnp.zeros_like(loss_ref)
            dgf_ref[...] = jnp.zeros_like(dgf_ref)

        @pl.when(kk == 0)
        def _():
            pa = _dot(ya_ref[...], woa_ref[...])
            pb = _dot(yb_ref[...], wob_ref[...])
            sa = _sigmoid(ga_ref[...])
            sb = _sigmoid(gb_ref[...])
            merged = (sa * pa + sb * pb).astype(BF16)
            mrg_ref[...] = merged
            x2 = x_ref[...] + _dot(merged, wout_ref[...])
            r2 = _rms_scale(x2)
            xh = x2 * r2
            gf = gf_ref[...]
            diff = xh * gf - t_ref[...]
            loss_ref[...] += jnp.sum(diff * diff, axis=0, keepdims=True) * (0.5 / d)
            dy = diff * (1.0 / d)
            dgf_ref[...] += jnp.sum(dy * xh, axis=0, keepdims=True)
            dxh = dy * gf
            dx2 = r2 * (dxh - xh * jnp.mean(dxh * xh, axis=-1, keepdims=True))
            dx2_ref[...] = dx2
            dm = _dot_nt(dx2.astype(BF16), wout_ref[...])
            dpa = (dm * sa).astype(BF16)
            dpb = (dm * sb).astype(BF16)
            dpa_ref[...] = dpa
            dpb_ref[...] = dpb
            dproj_ref[...] = (dm * pa * (sa * (1.0 - sa))).astype(BF16)
            dgb_s[...] = (dm * pb * (sb * (1.0 - sb))).astype(BF16)
            dya_ref[...] = _dot_nt(dpa, woa_ref[...])
            dyb_ref[...] = _dot_nt(dpb, wob_ref[...])

        @pl.when(kk == 1)
        def _():
            dproj_ref[...] = dgb_s[...]

    rows = lambda k=0: pl.BlockSpec((tm, d), lambda i, kk: (i, k))
    full = pl.BlockSpec((d, d), lambda i, kk: (0, 0))
    vec = pl.BlockSpec((1, d), lambda i, kk: (0, 0))
    return pl.pallas_call(
        body, name="tail", grid=(n // tm, 2),
        in_specs=[rows(), rows(), rows(), rows(), rows(7), rows(8), full, full, full, vec],
        out_specs=[pl.BlockSpec((tm, d), lambda i, kk: (i, 7 + kk)),
                   rows(), rows(), rows(), rows(), rows(), rows(), vec, vec],
        out_shape=[SDS((n, e), BF16), SDS((n, d), F32), SDS((n, d), F32), SDS((n, d), F32),
                   SDS((n, d), BF16), SDS((n, d), BF16), SDS((n, d), BF16),
                   SDS((1, d), F32), SDS((1, d), F32)],
        scratch_shapes=[pltpu.VMEM((tm, d), BF16)],
        compiler_params=_params(("arbitrary", "arbitrary")),
    )(x2d, tgt, ya, yb, proj, proj, w_oa, w_ob, w_out, norm_final)


def _tn_matmul(a, b, name):
    n, p = a.shape
    q = b.shape[1]
    tk = _tile(n, 512)
    nk = n // tk

    def body(a_ref, b_ref, o_ref, acc):
        k = pl.program_id(0)

        @pl.when(k == 0)
        def _():
            acc[...] = jnp.zeros_like(acc)

        acc[...] += _dot_tn(a_ref[...], b_ref[...].astype(BF16))

        @pl.when(k == nk - 1)
        def _():
            o_ref[...] = acc[...].astype(BF16)

    return pl.pallas_call(
        body, name=name, grid=(nk,),
        in_specs=[pl.BlockSpec((tk, p), lambda k: (k, 0)), pl.BlockSpec((tk, q), lambda k: (k, 0))],
        out_specs=pl.BlockSpec((p, q), lambda k: (0, 0)),
        out_shape=SDS((p, q), BF16),
        scratch_shapes=[pltpu.VMEM((p, q), F32)],
        compiler_params=_params(("arbitrary",)),
    )(a, b)


def _sb_bwd(proj, o, dyb, tot, dproj, batch, seq, d, hd):
    heads = d // hd
    t = _tile(seq, SB_TILE_BWD)
    scale = hd ** -0.5
    nblk = seq // t
    nh = SB_HEADS
    wide = nh * hd
    hs = range(nh)
    cols = [slice(hh * hd, (hh + 1) * hd) for hh in hs]

    def compute(q_ref, k_ref, v_ref, zb_ref, o_ref, dyb_ref, tot_ref, qs, ks, kts, vts, dos, res, upto, before, dq):
        qs[...] = q_ref[...].astype(BF16)
        ks[...] = k_ref[...].astype(BF16)
        for jb in range(nblk):
            kts[jb] = k_ref[jb * t:(jb + 1) * t, :].T.astype(BF16)
            vts[jb] = v_ref[jb * t:(jb + 1) * t, :].T.astype(BF16)
        sz, dsz = _silu(zb_ref[...])
        dyb_v = dyb_ref[...]
        dos[...] = (dyb_v * sz).astype(BF16)
        res[3] = dyb_v * o_ref[...] * dsz
        res[1] = jnp.zeros((seq, wide), F32)
        res[2] = jnp.zeros((seq, wide), F32)
        row, col = _iotas(t)
        upto[...] = (row <= col).astype(BF16)
        before[...] = (row < col).astype(BF16)

        def qblock(i, carry):
            r0 = pl.multiple_of(i * t, t)

            def tile(j, sums, valid):
                c0 = pl.multiple_of(j * t, t)
                q_i = [qs[pl.ds(r0, t), cs] for cs in cols]
                do_i = [dos[pl.ds(r0, t), cs] for cs in cols]
                logs = [_sb_logs(_dot(q_i[hh], kts[j, cols[hh], :]), scale, valid) for hh in hs]
                dw = [_dot(do_i[hh], vts[j, cols[hh], :]) for hh in hs]
                scans = [_dot(logs[hh][1], upto[...]) for hh in hs]
                ws, gs = [], []
                for hh in hs:
                    left = tot_ref[hh, pl.ds(r0, t), :] - sums[hh][0]
                    w = jnp.exp(logs[hh][0].astype(F32) + (left - scans[hh]))
                    ws.append(w.astype(BF16))
                    gs.append((dw[hh] * w).astype(BF16))
                gscans = [_dot(gs[hh], before[...]) for hh in hs]
                dzs = []
                for hh in hs:
                    beta = jnp.exp(logs[hh][0]).astype(F32)
                    g = gs[hh].astype(F32)
                    dzs.append(((g - (g + gscans[hh] + sums[hh][1]) * beta) * scale).astype(BF16))
                for hh in hs:
                    res[2, pl.ds(c0, t), cols[hh]] += _dot_tn(ws[hh], do_i[hh])
                for hh in hs:
                    res[1, pl.ds(c0, t), cols[hh]] += _dot_tn(dzs[hh], q_i[hh])
                for hh in hs:
                    dq[:, cols[hh]] += _dot(dzs[hh], ks[pl.ds(c0, t), cols[hh]])
                last = slice(t - 1, t)
                return tuple((sums[hh][0] + scans[hh][:, last],
                              sums[hh][1] + gscans[hh][:, last] + gs[hh][:, last].astype(F32)) for hh in hs)

            zero = jnp.zeros((t, 1), F32)
            dq[...] = jnp.zeros_like(dq)
            sums = lax.fori_loop(0, i, lambda j, sm: tile(j, sm, None), ((zero, zero),) * nh)
            tile(i, sums, col < row)
            res[0, pl.ds(r0, t), :] = dq[...]
            return carry

        lax.fori_loop(0, nblk, qblock, 0)

    def body(q_ref, k_ref, v_ref, zb_ref, o_ref, dyb_ref, tot_ref, dproj_in, out_ref,
             qs, ks, kts, vts, dos, res, upto, before, dq):
        del dproj_in
        kk = pl.program_id(2)

        @pl.when(kk == 0)
        def _():
            compute(q_ref, k_ref, v_ref, zb_ref, o_ref, dyb_ref, tot_ref, qs, ks, kts, vts, dos, res, upto, before, dq)

        out_ref[...] = res[kk].astype(BF16)

    col0 = d // wide
    seg = lambda k: pl.BlockSpec((seq, wide), lambda b, h, kk: (b, k * col0 + h))
    head = pl.BlockSpec((seq, wide), lambda b, h, kk: (b, h))
    return pl.pallas_call(
        body, name="sb_bwd", grid=(batch, heads // nh, 4),
        in_specs=[seg(3), seg(4), seg(5), seg(6), head, head,
                  pl.BlockSpec((nh, seq, 1), lambda b, h, kk: (b * (heads // nh) + h, 0, 0)),
                  pl.BlockSpec(memory_space=pl.ANY)],
        out_specs=pl.BlockSpec((seq, wide), lambda b, h, kk: (b, (3 + kk) * col0 + h)),
        out_shape=SDS(dproj.shape, dproj.dtype),
        input_output_aliases={7: 0},
        scratch_shapes=[pltpu.VMEM((seq, wide), BF16)] * 2 + [pltpu.VMEM((nblk, wide, t), BF16)] * 2 + [
            pltpu.VMEM((seq, wide), BF16), pltpu.VMEM((4, seq, wide), F32),
            pltpu.VMEM((t, t), BF16), pltpu.VMEM((t, t), BF16), pltpu.VMEM((t, wide), F32)],
        compiler_params=_params(("arbitrary", "arbitrary", "arbitrary")),
    )(proj, proj, proj, proj, o, dyb, tot, dproj)


def _branch_a_bwd(proj, dya, norm_v, w_s, b_col, dproj):
    n = proj.shape[0]
    d = norm_v.shape[1]
    groups, chunk, _ = w_s.shape
    tr = _tile(n, 2 * chunk)

    def body(u_ref, v_ref, z_ref, dya_ref, gv_ref, ws_ref, b_ref, dproj_in,
             out_ref, dws_ref, dbias_ref, dgv_ref, vn_s, dmix_s, dvn_s, db_ref):
        del dproj_in

        @pl.when(pl.program_id(0) == 0)
        def _():
            dws_ref[...] = jnp.zeros_like(dws_ref)
            db_ref[...] = jnp.zeros_like(db_ref)
            dgv_ref[...] = jnp.zeros_like(dgv_ref)

        row, col = _iotas(chunk)
        tril = col <= row
        u, v, z, dya_v = u_ref[...], v_ref[...], z_ref[...], dya_ref[...]
        gv = gv_ref[...]
        vg, dvg_dv = _gelu(v)
        r = _rms_scale(vg)
        vh = vg * r
        vn_s[...] = (vh * gv).astype(BF16)
        ug, dug_du = _gelu(u)
        sz, dsz = _silu(z)
        dmix_s[...] = dya_v * ug * sz
        for g in range(groups):
            wm = jnp.where(tril, ws_ref[g], 0.0).astype(BF16)
            cs = slice(g * chunk, (g + 1) * chunk)
            for c in range(tr // chunk):
                rs = slice(c * chunk, (c + 1) * chunk)
                vn = vn_s[rs, cs]
                mixed = _dot(wm, vn) + b_ref[g]
                dmix = dmix_s[rs, cs]
                dmix16 = dmix.astype(BF16)
                dws_ref[g] += _dot_nt(dmix16, vn)
                db_ref[g] += dmix
                dvn_s[rs, cs] = _dot_tn(wm, dmix16)
                t_u = dya_v[rs, cs] * mixed
                out_ref[rs, g * chunk:(g + 1) * chunk] = (t_u * sz[rs, cs] * dug_du[rs, cs]).astype(BF16)
                out_ref[rs, 2 * d + g * chunk:2 * d + (g + 1) * chunk] = (t_u * ug[rs, cs] * dsz[rs, cs]).astype(BF16)
        dvn = dvn_s[...]
        dgv_ref[...] += jnp.sum(dvn * vh, axis=0, keepdims=True)
        dvh = dvn * gv
        dvg = r * (dvh - vh * jnp.mean(dvh * vh, axis=-1, keepdims=True))
        out_ref[:, d:2 * d] = (dvg * dvg_dv).astype(BF16)

        @pl.when(pl.program_id(0) == n // tr - 1)
        def _():
            for g in range(groups):
                dbias_ref[g:g + 1, :] = jnp.sum(db_ref[g].T, axis=0, keepdims=True)

    seg = lambda k: pl.BlockSpec((tr, d), lambda i: (i, k))
    return pl.pallas_call(
        body, name="branch_a_bwd", grid=(n // tr,),
        in_specs=[seg(0), seg(1), seg(2), seg(0),
                  pl.BlockSpec((1, d), lambda i: (0, 0)),
                  pl.BlockSpec((groups, chunk, chunk), lambda i: (0, 0, 0)),
                  pl.BlockSpec((groups, chunk, 1), lambda i: (0, 0, 0)),
                  pl.BlockSpec(memory_space=pl.ANY)],
        out_specs=[pl.BlockSpec((tr, 3 * d), lambda i: (i, 0)),
                   pl.BlockSpec((groups, chunk, chunk), lambda i: (0, 0, 0)),
                   pl.BlockSpec((groups, chunk), lambda i: (0, 0)),
                   pl.BlockSpec((1, d), lambda i: (0, 0))],
        out_shape=[SDS(dproj.shape, dproj.dtype), SDS((groups, chunk, chunk), F32),
                   SDS((groups, chunk), F32), SDS((1, d), F32)],
        input_output_aliases={7: 0},
        scratch_shapes=[pltpu.VMEM((tr, d), BF16), pltpu.VMEM((tr, d), F32), pltpu.VMEM((tr, d), F32),
                        pltpu.VMEM((groups, chunk, chunk), F32)],
        compiler_params=_params(("arbitrary",)),
    )(proj, proj, proj, dya, norm_v, w_s, b_col, dproj)


def _dx(dproj, wg_in, x2d, dx2, norm_in):
    n, d = x2d.shape
    nsh, _, esh = wg_in.shape
    tm = _tile(n, 1024)

    def body(dp_ref, w_ref, x_ref, dx2_ref, g_ref, gx_ref, dg_ref, acc):
        i, k = pl.program_id(0), pl.program_id(1)

        @pl.when(jnp.logical_and(i == 0, k == 0))
        def _():
            dg_ref[...] = jnp.zeros_like(dg_ref)

        @pl.when(k == 0)
        def _():
            acc[...] = jnp.zeros_like(acc)

        acc[...] += _dot_nt(dp_ref[...], w_ref[0])

        @pl.when(k == nsh - 1)
        def _():
            dh = acc[...]
            x = x_ref[...]
            r = _rms_scale(x)
            xh = x * r
            dg_ref[...] += jnp.sum(dh * xh, axis=0, keepdims=True)
            dxh = dh * g_ref[...]
            gx_ref[...] = dx2_ref[...] + r * (dxh - xh * jnp.mean(dxh * xh, axis=-1, keepdims=True))

    rows = pl.BlockSpec((tm, d), lambda i, k: (i, 0))
    vec = pl.BlockSpec((1, d), lambda i, k: (0, 0))
    return pl.pallas_call(
        body, name="dx", grid=(n // tm, nsh),
        in_specs=[pl.BlockSpec((tm, esh), lambda i, k: (i, k)),
                  pl.BlockSpec((1, d, esh), lambda i, k: (k, 0, 0)), rows, rows, vec],
        out_specs=[rows, vec],
        out_shape=[SDS((n, d), F32), SDS((1, d), F32)],
        scratch_shapes=[pltpu.VMEM((tm, d), F32)],
        compiler_params=_params(("arbitrary", "arbitrary")),
    )(dproj, wg_in, x2d, dx2, norm_in)


def _adamw_outputs(g_ref, d_ref, m_ref, v_ref, g, w, m, v):
    delta, m2, v2 = _adamw(w, g, m, v)
    g_ref[...] = g
    d_ref[...] = delta
    m_ref[...] = m2
    v_ref[...] = v2


def _reduce_adamw(slots, w, m, v, name):
    _, r, c = slots.shape
    tr = _tile(r, 128)

    def body(s_ref, w_ref, m_ref, v_ref, g_out, d_out, m_out, v_out):
        g = s_ref[0].astype(F32)
        for k in range(1, N_DEV):
            g = g + s_ref[k].astype(F32)
        _adamw_outputs(g_out, d_out, m_out, v_out, g, w_ref[...], m_ref[...], v_ref[...])

    blk = pl.BlockSpec((tr, c), lambda i: (i, 0))
    return pl.pallas_call(
        body, name=name, grid=(r // tr,),
        in_specs=[pl.BlockSpec((N_DEV, tr, c), lambda i: (0, i, 0)), blk, blk, blk],
        out_specs=[blk] * 4,
        out_shape=[SDS((r, c), F32)] * 4,
        compiler_params=_params(("parallel",)),
    )(slots, w, m, v)


def _adamw_small(g, w, m, v, name):
    def body(g_ref, w_ref, m_ref, v_ref, g_out, d_out, m_out, v_out):
        _adamw_outputs(g_out, d_out, m_out, v_out, g_ref[...], w_ref[...], m_ref[...], v_ref[...])

    return pl.pallas_call(
        body, name=name,
        out_shape=[SDS(g.shape, F32)] * 4,
        in_specs=[pl.BlockSpec(memory_space=pltpu.VMEM)] * 4,
        out_specs=[pl.BlockSpec(memory_space=pltpu.VMEM)] * 4,
    )(g, w, m, v)


def kernel(x, norm_in, w_in, norm_v, w_s, b_s, w_o_gmlp, w_o_sb, w_out, norm_final, loss_target, m_norm_in, m_w_in, m_norm_v, m_w_s, m_b_s, m_w_o_gmlp, m_w_o_sb, m_w_out, m_norm_final, v_norm_in, v_w_in, v_norm_v, v_w_s, v_b_s, v_w_o_gmlp, v_w_o_sb, v_w_out, v_norm_final):
    batch, seq, d = x.shape
    n = batch * seq
    groups, chunk = w_s.shape[1], w_s.shape[2]
    hd = LANE
    x2d = x.reshape(n, d)
    tgt = loss_target.reshape(n, d)
    b_col = b_s[0].reshape(groups, chunk, 1)
    norm_final2 = norm_final.reshape(1, d)

    wg_in, wg_oa, wg_ob, wg_out = _gather_weights([w_in[0], w_o_gmlp[0], w_o_sb[0], w_out[0]])
    rsh = wg_oa.shape[1]
    wf_oa, wf_ob, wf_out = (w.reshape(N_DEV * rsh, d) for w in (wg_oa, wg_ob, wg_out))

    proj, h = _in_proj(x2d, norm_in, wg_in)
    ya = _branch_a_fwd(proj, norm_v, w_s[0], b_col)
    yb, o, sb_tot = _sb_fwd(proj, batch, seq, d, hd)
    dproj, dx2, dya, dyb, merged, dpa, dpb, loss_vec, dgf = _tail(
        x2d, tgt, ya, yb, proj, wf_oa, wf_ob, wf_out, norm_final2)
    gp_oa = _tn_matmul(ya, dpa, "dw_o_gmlp")
    gp_ob = _tn_matmul(yb, dpb, "dw_o_sb")
    gp_out = _tn_matmul(merged, dx2, "dw_out")
    dproj = _sb_bwd(proj, o, dyb, sb_tot, dproj, batch, seq, d, hd)
    dproj, gp_ws, gp_b, gp_nv = _branch_a_bwd(proj, dya, norm_v, w_s[0], b_col, dproj)
    grad_x, gp_nin = _dx(dproj, wg_in, x2d, dx2, norm_in)

    slab = lambda a: a.reshape(d // LANE, LANE)
    gc = groups * chunk
    packed = jnp.concatenate(
        [gp_ws.reshape(gc, chunk), gp_b, slab(gp_nin), slab(gp_nv), slab(dgf), slab(loss_vec)], axis=0)
    my_slot = _slot(_me()).astype(jnp.int32).reshape(1)
    s_win, s_oa, s_ob, s_out, packs = _dw_in_exchange(
        h, dproj, my_slot, [g.reshape(N_DEV, rsh, d) for g in (gp_oa, gp_ob, gp_out)], packed)
    tot, loss_slab = _finish_small(packs, groups, chunk)
    ns = d // LANE
    g_ws = tot[:gc]
    g_b = tot[gc:gc + groups]
    g_nin, g_nv, g_nf = (tot[gc + groups + k * ns:gc + groups + (k + 1) * ns] for k in range(3))
    loss = loss_slab[0, 0]

    res = {}
    res["w_in"] = _reduce_adamw(s_win, w_in[0], m_w_in[0], v_w_in[0], "adamw_w_in")
    res["w_o_gmlp"] = _reduce_adamw(s_oa, w_o_gmlp[0], m_w_o_gmlp[0], v_w_o_gmlp[0], "adamw_w_o_gmlp")
    res["w_o_sb"] = _reduce_adamw(s_ob, w_o_sb[0], m_w_o_sb[0], v_w_o_sb[0], "adamw_w_o_sb")
    res["w_out"] = _reduce_adamw(s_out, w_out[0], m_w_out[0], v_w_out[0], "adamw_w_out")
    res["norm_in"] = _adamw_small(g_nin, slab(norm_in), slab(m_norm_in), slab(v_norm_in), "adamw_norm_in")
    res["norm_v"] = _adamw_small(g_nv, slab(norm_v), slab(m_norm_v), slab(v_norm_v), "adamw_norm_v")
    res["norm_final"] = _adamw_small(g_nf, slab(norm_final), slab(m_norm_final), slab(v_norm_final), "adamw_norm_final")
    res["w_s"] = _adamw_small(g_ws, w_s.reshape(gc, chunk), m_w_s.reshape(gc, chunk), v_w_s.reshape(gc, chunk), "adamw_w_s")
    res["b_s"] = _adamw_small(g_b, b_s[0], m_b_s[0], v_b_s[0], "adamw_b_s")

    shapes = {"norm_in": norm_in.shape, "w_in": w_in.shape, "norm_v": norm_v.shape, "w_s": w_s.shape,
              "b_s": b_s.shape, "w_o_gmlp": w_o_gmlp.shape, "w_o_sb": w_o_sb.shape, "w_out": w_out.shape,
              "norm_final": norm_final.shape}
    names = list(shapes)
    outs = [loss, grad_x.reshape(batch, seq, d)]
    for kind in range(4):
        outs += [res[name][kind].reshape(shapes[name]) for name in names]
    return tuple(outs)
```

```python
import functools
import math

import jax
import jax.numpy as jnp
from jax import lax
from jax.experimental import pallas as pl
from jax.experimental.pallas import tpu as pltpu

F32 = jnp.float32
BF16 = jnp.bfloat16
SDS = jax.ShapeDtypeStruct
MESH_ID = pl.DeviceIdType.MESH

N_DEV = 8
LANE = 128
SUBLANE = 8
VMEM_LIMIT = 56 * 1024 * 1024
SB_TILE = 512
SB_TILE_BWD = 512
SB_SCAN = 256
SB_HEADS = 2
MASKED_LOG = -1e30
RMS_EPS = 1e-6

ADAM_LR = 0.001
ADAM_B1 = 0.9
ADAM_B2 = 0.999
ADAM_EPS = 1e-08
ADAM_WD = 0.01
ADAM_STEP = 10

NT_DIMS = (((1,), (1,)), ((), ()))
TN_DIMS = (((0,), (0,)), ((), ()))


def _params(semantics=None):
    return pltpu.CompilerParams(dimension_semantics=semantics, vmem_limit_bytes=VMEM_LIMIT)


def _tile(n, preferred):
    t = min(n, preferred)
    assert n % t == 0, (n, t)
    return t


def _sigmoid(x):
    return 1.0 / (1.0 + jnp.exp(-x))


def _silu(x):
    s = _sigmoid(x)
    return x * s, s * (1.0 + x * (1.0 - s))


def _gelu(x):
    k = math.sqrt(2.0 / math.pi)
    x2 = x * x
    t = jnp.tanh(k * (x + 0.044715 * (x * x2)))
    cdf = 0.5 * (1.0 + t)
    return x * cdf, cdf + 0.5 * x * (1.0 - t * t) * (k * (1.0 + 3.0 * 0.044715 * x2))


def _rms_scale(x):
    return lax.rsqrt(jnp.mean(x * x, axis=-1, keepdims=True) + RMS_EPS)


def _iotas(n):
    return (lax.broadcasted_iota(jnp.int32, (n, n), 0), lax.broadcasted_iota(jnp.int32, (n, n), 1))


def _adamw(w, g, m, v):
    m = ADAM_B1 * m + (1.0 - ADAM_B1) * g
    v = ADAM_B2 * v + (1.0 - ADAM_B2) * (g * g)
    m_hat = m / (1.0 - ADAM_B1 ** ADAM_STEP)
    v_hat = v / (1.0 - ADAM_B2 ** ADAM_STEP)
    delta = -ADAM_LR * (m_hat / (jnp.sqrt(v_hat) + ADAM_EPS) + ADAM_WD * w)
    return delta, m, v


def _dot(a, b):
    return jnp.dot(a, b, preferred_element_type=F32)


def _dot_nt(a, b):
    return lax.dot_general(a, b, NT_DIMS, preferred_element_type=F32)


def _dot_tn(a, b):
    return lax.dot_general(a, b, TN_DIMS, preferred_element_type=F32)


def _sb_logs(raw, scale, valid):
    z = (raw * scale).astype(BF16)
    log_beta = jnp.minimum(z, 0) - jnp.log(1 + jnp.exp(-jnp.abs(z)))
    log_rest = log_beta - z
    if valid is not None:
        log_beta = jnp.where(valid, log_beta, MASKED_LOG)
        log_rest = jnp.where(valid, log_rest, 0)
    return log_beta, log_rest


def _me():
    return lax.axis_index("x"), lax.axis_index("y"), lax.axis_index("c")


def _slot(p):
    return 4 * p[0] + 2 * p[1] + p[2]


def _peer(me, k):
    flips = ((k >> 2) & 1, (k >> 1) & 1, k & 1)
    return tuple(1 - a if f else a for a, f in zip(me, flips))


def _gather_weights(shards):
    n = len(shards)

    def body(*refs):
        ins, outs, stage = refs[:n], refs[n:2 * n], refs[2 * n:3 * n]
        send_sems, recv_sems, local_sems = refs[3 * n:]
        x, y, c = _me()
        me, sibling = (x, y, c), (x, y, 1 - c)
        chips = [(1 - x, y), (x, 1 - y), (1 - x, 1 - y)]

        def copy(a, k, block, to, src=None):
            dst = outs[a].at[_slot(block)]
            return pltpu.make_async_remote_copy(
                src_ref=dst if src is None else src, dst_ref=dst,
                send_sem=send_sems.at[7 * a + k], recv_sem=recv_sems.at[7 * a + k],
                device_id=to, device_id_type=MESH_ID)

        started = []
        for a in range(n):
            stage[a][...] = ins[a][...].astype(BF16)
            mine = pltpu.make_async_copy(stage[a], outs[a].at[_slot(me)], local_sems.at[a])
            mine.start()
            started.append(mine)
        sends = []
        for a in range(n):
            sends.append(copy(a, 0, me, sibling, src=stage[a]))
            sends += [copy(a, 1 + j, me, (*chip, c), src=stage[a]) for j, chip in enumerate(chips)]
        for cp in sends:
            cp.start()
        for a in range(n):
            for j, chip in enumerate(chips):
                copy(a, 1 + j, (*chip, c), me).wait_recv()
                passed = copy(a, 4 + j, (*chip, c), sibling)
                passed.start()
                sends.append(passed)
        for a in range(n):
            copy(a, 0, sibling, me).wait_recv()
            for j, chip in enumerate(chips):
                copy(a, 4 + j, (*chip, 1 - c), me).wait_recv()
        for cp in sends:
            cp.wait_send()
        for mine in started:
            mine.wait()

    return pl.pallas_call(
        body, name="gather_weights",
        out_shape=[SDS((N_DEV,) + s.shape, BF16) for s in shards],
        in_specs=[pl.BlockSpec(memory_space=pltpu.VMEM)] * n,
        out_specs=[pl.BlockSpec(memory_space=pl.ANY)] * n,
        scratch_shapes=[pltpu.VMEM(s.shape, BF16) for s in shards] + [
            pltpu.SemaphoreType.DMA((7 * n,)), pltpu.SemaphoreType.DMA((7 * n,)),
            pltpu.SemaphoreType.DMA((n,))],
        compiler_params=pltpu.CompilerParams(vmem_limit_bytes=VMEM_LIMIT),
    )(*shards)


def _dw_in_exchange(h, dproj, my_slot, stacks, packed):
    n, d = h.shape
    esh = dproj.shape[1] // N_DEV
    tk = _tile(n, 512)
    nk = n // tk
    ns = len(stacks)
    last_j = N_DEV - 1

    def body(me_ref, h_ref, dp_ref, *refs):
        del me_ref
        st_in, pk_in = refs[:ns], refs[ns]
        win_out, st_out, pk_out = refs[ns + 1], refs[ns + 2:2 * ns + 2], refs[2 * ns + 2]
        acc, sendbuf, win_send, win_recv, send_sems, recv_sems, local_sems = refs[2 * ns + 3:]
        j, k = pl.program_id(0), pl.program_id(1)
        me = _me()
        mine = _slot(me)

        def ready_copies():
            local = [pltpu.make_async_copy(st_in[a].at[mine], st_out[a].at[mine], local_sems.at[a])
                     for a in range(ns)]
            local.append(pltpu.make_async_copy(pk_in, pk_out.at[mine], local_sems.at[ns]))
            remote = []
            for kk in range(1, N_DEV):
                peer = _peer(me, kk)
                for a in range(ns):
                    remote.append(pltpu.make_async_remote_copy(
                        src_ref=st_in[a].at[_slot(peer)], dst_ref=st_out[a].at[mine],
                        send_sem=send_sems.at[(ns + 1) * (kk - 1) + a],
                        recv_sem=recv_sems.at[(ns + 1) * (kk - 1) + a],
                        device_id=peer, device_id_type=MESH_ID))
                remote.append(pltpu.make_async_remote_copy(
                    src_ref=pk_in, dst_ref=pk_out.at[mine],
                    send_sem=send_sems.at[(ns + 1) * (kk - 1) + ns],
                    recv_sem=recv_sems.at[(ns + 1) * (kk - 1) + ns],
                    device_id=peer, device_id_type=MESH_ID))
            return local, remote

        def shard_copy(jj):
            owner = (mine + 1 + jj) % N_DEV
            return pltpu.make_async_remote_copy(
                src_ref=sendbuf.at[jj % 2], dst_ref=win_out.at[mine],
                send_sem=win_send.at[jj % 2], recv_sem=win_recv.at[mine],
                device_id=(owner // 4, (owner // 2) % 2, owner % 2), device_id_type=MESH_ID)

        def own_copy():
            return pltpu.make_async_copy(sendbuf.at[last_j % 2], win_out.at[mine], local_sems.at[ns + 1])

        @pl.when(jnp.logical_and(j == 0, k == 0))
        def _():
            local, remote = ready_copies()
            for cp in local + remote:
                cp.start()

        @pl.when(k == 0)
        def _():
            acc[...] = jnp.zeros_like(acc)

        acc[...] += _dot_tn(h_ref[...], dp_ref[...])

        @pl.when(k == nk - 1)
        def _():
            @pl.when(j >= 2)
            def _():
                shard_copy(j - 2).wait_send()

            sendbuf[j % 2] = acc[...].astype(BF16)

            @pl.when(j < last_j)
            def _():
                shard_copy(j).start()

            @pl.when(j == last_j)
            def _():
                own_copy().start()
                shard_copy(last_j - 1).wait_send()
                own_copy().wait()
                for src in range(N_DEV):
                    @pl.when(src != mine)
                    def _():
                        landed = win_out.at[src]
                        pltpu.make_async_remote_copy(
                            src_ref=landed, dst_ref=landed, send_sem=win_send.at[0], recv_sem=win_recv.at[src],
                            device_id=me, device_id_type=MESH_ID).wait_recv()
                local, remote = ready_copies()
                for cp in remote:
                    cp.wait_send()
                idx = 0
                for kk in range(1, N_DEV):
                    peer = _slot(_peer(me, kk))
                    for a in range(ns + 1):
                        landed = pk_out.at[peer] if a == ns else st_out[a].at[peer]
                        pltpu.make_async_remote_copy(
                            src_ref=landed, dst_ref=landed, send_sem=send_sems.at[idx], recv_sem=recv_sems.at[idx],
                            device_id=me, device_id_type=MESH_ID).wait_recv()
                        idx += 1
                for cp in local:
                    cp.wait()

    any_spec = pl.BlockSpec(memory_space=pl.ANY)
    n_ready = 7 * (ns + 1)
    grid_spec = pltpu.PrefetchScalarGridSpec(
        num_scalar_prefetch=1, grid=(N_DEV, nk),
        in_specs=[pl.BlockSpec((tk, d), lambda j, k, me: (k, 0)),
                  pl.BlockSpec((tk, esh), lambda j, k, me: (k, (me[0] + 1 + j) % N_DEV))] + [any_spec] * (ns + 1),
        out_specs=[any_spec] * (ns + 2),
        scratch_shapes=[pltpu.VMEM((d, esh), F32), pltpu.VMEM((2, d, esh), BF16),
                        pltpu.SemaphoreType.DMA((2,)), pltpu.SemaphoreType.DMA((N_DEV,)),
                        pltpu.SemaphoreType.DMA((n_ready,)), pltpu.SemaphoreType.DMA((n_ready,)),
                        pltpu.SemaphoreType.DMA((ns + 2,))])
    return pl.pallas_call(
        body, name="dw_in_exchange", grid_spec=grid_spec,
        out_shape=[SDS((N_DEV, d, esh), BF16)] + [SDS(s.shape, s.dtype) for s in stacks] + [
            SDS((N_DEV,) + packed.shape, packed.dtype)],
        compiler_params=_params(("arbitrary", "arbitrary")),
    )(my_slot, h, dproj, *stacks, packed)


def _finish_small(packs, groups, chunk):
    rows = packs.shape[1]
    gc = groups * chunk

    def body(p_ref, sum_ref, loss_ref):
        row, col = _iotas(chunk)
        tril = col <= row
        for g in range(groups):
            rs = slice(g * chunk, (g + 1) * chunk)
            tot = p_ref[0, rs, :]
            for dev in range(1, N_DEV):
                tot = tot + p_ref[dev, rs, :]
            sum_ref[rs, :] = jnp.where(tril, tot, 0.0)
        rs = slice(gc, rows)
        tot = p_ref[0, rs, :]
        for dev in range(1, N_DEV):
            tot = tot + p_ref[dev, rs, :]
        sum_ref[rs, :] = tot
        loss_ref[...] = jnp.full((SUBLANE, LANE), jnp.sum(tot[rows - gc - SUBLANE:, :]), F32)

    return pl.pallas_call(
        body, name="finish_small",
        out_shape=[SDS((rows, LANE), F32), SDS((SUBLANE, LANE), F32)],
        in_specs=[pl.BlockSpec(memory_space=pltpu.VMEM)],
        out_specs=[pl.BlockSpec(memory_space=pltpu.VMEM)] * 2,
        compiler_params=pltpu.CompilerParams(vmem_limit_bytes=VMEM_LIMIT),
    )(packs)


def _in_proj(x2d, norm_in, wg_in):
    n, d = x2d.shape
    nsh, _, esh = wg_in.shape
    tm = _tile(n, 1024)

    def body(x_ref, g_ref, w_ref, proj_ref, h_ref):
        @pl.when(pl.program_id(1) == 0)
        def _():
            x = x_ref[...]
            h_ref[...] = (x * _rms_scale(x) * g_ref[...]).astype(BF16)

        proj_ref[...] = _dot(h_ref[...], w_ref[0]).astype(BF16)

    return pl.pallas_call(
        body, name="in_proj", grid=(n // tm, nsh),
        in_specs=[pl.BlockSpec((tm, d), lambda i, j: (i, 0)),
                  pl.BlockSpec((1, d), lambda i, j: (0, 0)),
                  pl.BlockSpec((1, d, esh), lambda i, j: (j, 0, 0))],
        out_specs=[pl.BlockSpec((tm, esh), lambda i, j: (i, j)),
                   pl.BlockSpec((tm, d), lambda i, j: (i, 0))],
        out_shape=[SDS((n, nsh * esh), BF16), SDS((n, d), BF16)],
        compiler_params=_params(("parallel", "arbitrary")),
    )(x2d, norm_in, wg_in)


def _branch_a_fwd(proj, norm_v, w_s, b_col):
    n = proj.shape[0]
    d = norm_v.shape[1]
    groups, chunk, _ = w_s.shape
    tr = _tile(n, 4 * chunk)

    def body(u_ref, v_ref, z_ref, gv_ref, ws_ref, b_ref, ya_ref, vn_s, pre_s):
        row, col = _iotas(chunk)
        tril = col <= row
        vg, _ = _gelu(v_ref[...].astype(F32))
        vn_s[...] = (vg * _rms_scale(vg) * gv_ref[...]).astype(BF16)
        ug, _ = _gelu(u_ref[...].astype(F32))
        sz, _ = _silu(z_ref[...].astype(F32))
        pre_s[...] = ug * sz
        for g in range(groups):
            wm = jnp.where(tril, ws_ref[g], 0.0).astype(BF16)
            cs = slice(g * chunk, (g + 1) * chunk)
            for c in range(tr // chunk):
                rs = slice(c * chunk, (c + 1) * chunk)
                mixed = _dot(wm, vn_s[rs, cs]) + b_ref[g]
                ya_ref[rs, cs] = (pre_s[rs, cs] * mixed).astype(BF16)

    seg = lambda k: pl.BlockSpec((tr, d), lambda i: (i, k))
    return pl.pallas_call(
        body, name="branch_a_fwd", grid=(n // tr,),
        in_specs=[seg(0), seg(1), seg(2),
                  pl.BlockSpec((1, d), lambda i: (0, 0)),
                  pl.BlockSpec((groups, chunk, chunk), lambda i: (0, 0, 0)),
                  pl.BlockSpec((groups, chunk, 1), lambda i: (0, 0, 0))],
        out_specs=pl.BlockSpec((tr, d), lambda i: (i, 0)),
        out_shape=SDS((n, d), BF16),
        scratch_shapes=[pltpu.VMEM((tr, d), BF16), pltpu.VMEM((tr, d), F32)],
        compiler_params=_params(("parallel",)),
    )(proj, proj, proj, norm_v, w_s, b_col)


def _sb_fwd(proj, batch, seq, d, hd):
    heads = d // hd
    t = _tile(seq, SB_TILE)
    sw = _tile(t, SB_SCAN)
    nb = t // sw
    scale = hd ** -0.5
    nblk = seq // t
    nh = SB_HEADS
    wide = nh * hd
    cols = [slice(hh * hd, (hh + 1) * hd) for hh in range(nh)]

    def body(qs, k_ref, vs, zb_ref, yb_ref, o_ref, tot_ref, kts, later, acc):
        for jb in range(nblk):
            kts[jb] = k_ref[jb * t:(jb + 1) * t, :].astype(F32).T.astype(BF16)
        row, col = _iotas(t)
        later[...] = (row[:sw, :sw] > col[:sw, :sw]).astype(BF16)

        def qblock(i, carry):
            r0 = pl.multiple_of(i * t, t)

            def tile(j, runs, valid):
                c0 = pl.multiple_of(j * t, t)
                logs = [_sb_logs(_dot(qs[pl.ds(r0, t), cs], kts[j, cs, :]), scale, valid) for cs in cols]
                scans = [_dot(jnp.concatenate([logs[hh][1][:, b * sw:(b + 1) * sw] for b in range(nb)], axis=0),
                              later[...]) for hh in range(nh)]
                new_runs = []
                for hh in range(nh):
                    after = runs[hh]
                    blocks = [None] * nb
                    for b in reversed(range(nb)):
                        ks_ = slice(b * sw, (b + 1) * sw)
                        inside = scans[hh][b * t:(b + 1) * t]
                        blocks[b] = jnp.exp(logs[hh][0][:, ks_].astype(F32) + inside + after).astype(BF16)
                        after = after + inside[:, 0:1] + logs[hh][1][:, b * sw:b * sw + 1].astype(F32)
                    new_runs.append(after)
                    pv = _dot(jnp.concatenate(blocks, axis=1), vs[pl.ds(c0, t), cols[hh]])
                    if valid is None:
                        acc[:, cols[hh]] += pv
                    else:
                        acc[:, cols[hh]] = pv
                return tuple(new_runs)

            runs = tile(i, (jnp.zeros((t, 1), F32),) * nh, col < row)
            runs = lax.fori_loop(0, i, lambda jj, rs: tile(i - 1 - jj, rs, None), runs)
            for hh in range(nh):
                out = acc[:, cols[hh]]
                o_ref[pl.ds(r0, t), cols[hh]] = out.astype(BF16)
                tot_ref[hh, pl.ds(r0, t), :] = runs[hh]
                sz, _ = _silu(zb_ref[pl.ds(r0, t), cols[hh]].astype(F32))
                yb_ref[pl.ds(r0, t), cols[hh]] = (out * sz).astype(BF16)
            return carry

        lax.fori_loop(0, nblk, qblock, 0)

    col0 = d // wide
    seg = lambda k: pl.BlockSpec((seq, wide), lambda b, h: (b, k * col0 + h))
    return pl.pallas_call(
        body, name="sb_fwd", grid=(batch, heads // nh),
        in_specs=[seg(3), seg(4), seg(5), seg(6)],
        out_specs=[pl.BlockSpec((seq, wide), lambda b, h: (b, h))] * 2 + [
            pl.BlockSpec((nh, seq, 1), lambda b, h: (b * (heads // nh) + h, 0, 0))],
        out_shape=[SDS((batch * seq, d), BF16), SDS((batch * seq, d), BF16), SDS((batch * heads, seq, 1), F32)],
        scratch_shapes=[pltpu.VMEM((nblk, wide, t), BF16), pltpu.VMEM((sw, sw), BF16), pltpu.VMEM((t, wide), F32)],
        compiler_params=_params(("parallel", "parallel")),
    )(proj, proj, proj, proj)


def _tail(x2d, tgt, ya, yb, proj, w_oa, w_ob, w_out, norm_final):
    n, d = x2d.shape
    e = proj.shape[1]
    tm = _tile(n, 256)

    def body(x_ref, t_ref, ya_ref, yb_ref, ga_ref, gb_ref, woa_ref, wob_ref, wout_ref, gf_ref,
             dproj_ref, dx2_ref, dya_ref, dyb_ref, mrg_ref, dpa_ref, dpb_ref, loss_ref, dgf_ref, dgb_s):
        i, kk = pl.program_id(0), pl.program_id(1)

        @pl.when(jnp.logical_and(i == 0, kk == 0))
        def _():
            loss_ref[...] = jnp.zeros_like(loss_ref)
            dgf_ref[...] = jnp.zeros_like(dgf_ref)

        @pl.when(kk == 0)
        def _():
            pa = _dot(ya_ref[...], woa_ref[...])
            pb = _dot(yb_ref[...], wob_ref[...])
            sa = _sigmoid(ga_ref[...].astype(F32))
            sb = _sigmoid(gb_ref[...].astype(F32))
            merged = (sa * pa + sb * pb).astype(BF16)
            mrg_ref[...] = merged
            x2 = x_ref[...] + _dot(merged, wout_ref[...])
            r2 = _rms_scale(x2)
            xh = x2 * r2
            gf = gf_ref[...]
            diff = xh * gf - t_ref[...]
            loss_ref[...] += jnp.sum(diff * diff, axis=0, keepdims=True) * (0.5 / d)
            dy = diff * (1.0 / d)
            dgf_ref[...] += jnp.sum(dy * xh, axis=0, keepdims=True)
            dxh = dy * gf
            dx2 = r2 * (dxh - xh * jnp.mean(dxh * xh, axis=-1, keepdims=True))
            dx2_ref[...] = dx2
            dm = _dot_nt(dx2.astype(BF16), wout_ref[...])
            dpa = (dm * sa).astype(BF16)
            dpb = (dm * sb).astype(BF16)
            dpa_ref[...] = dpa
            dpb_ref[...] = dpb
            dproj_ref[...] = (dm * pa * (sa * (1.0 - sa))).astype(BF16)
            dgb_s[...] = (dm * pb * (sb * (1.0 - sb))).astype(BF16)
            dya_ref[...] = _dot_nt(dpa, woa_ref[...]).astype(BF16)
            dyb_ref[...] = _dot_nt(dpb, wob_ref[...]).astype(BF16)

        @pl.when(kk == 1)
        def _():
            dproj_ref[...] = dgb_s[...]

    rows = lambda k=0: pl.BlockSpec((tm, d), lambda i, kk: (i, k))
    full = pl.BlockSpec((d, d), lambda i, kk: (0, 0))
    vec = pl.BlockSpec((1, d), lambda i, kk: (0, 0))
    return pl.pallas_call(
        body, name="tail", grid=(n // tm, 2),
        in_specs=[rows(), rows(), rows(), rows(), rows(7), rows(8), full, full, full, vec],
        out_specs=[pl.BlockSpec((tm, d), lambda i, kk: (i, 7 + kk)),
                   rows(), rows(), rows(), rows(), rows(), rows(), vec, vec],
        out_shape=[SDS((n, e), BF16), SDS((n, d), F32), SDS((n, d), BF16), SDS((n, d), BF16),
                   SDS((n, d), BF16), SDS((n, d), BF16), SDS((n, d), BF16),
                   SDS((1, d), F32), SDS((1, d), F32)],
        scratch_shapes=[pltpu.VMEM((tm, d), BF16)],
        compiler_params=_params(("arbitrary", "arbitrary")),
    )(x2d, tgt, ya, yb, proj, proj, w_oa, w_ob, w_out, norm_final)


def _tn_matmul(a, b, name):
    n, p = a.shape
    q = b.shape[1]
    tk = _tile(n, 512)
    nk = n // tk

    def body(a_ref, b_ref, o_ref, acc):
        k = pl.program_id(0)

        @pl.when(k == 0)
        def _():
            acc[...] = jnp.zeros_like(acc)

        acc[...] += _dot_tn(a_ref[...], b_ref[...].astype(BF16))

        @pl.when(k == nk - 1)
        def _():
            o_ref[...] = acc[...].astype(BF16)

    return pl.pallas_call(
        body, name=name, grid=(nk,),
        in_specs=[pl.BlockSpec((tk, p), lambda k: (k, 0)), pl.BlockSpec((tk, q), lambda k: (k, 0))],
        out_specs=pl.BlockSpec((p, q), lambda k: (0, 0)),
        out_shape=SDS((p, q), BF16),
        scratch_shapes=[pltpu.VMEM((p, q), F32)],
        compiler_params=_params(("arbitrary",)),
    )(a, b)


def _sb_bwd(proj, o, dyb, tot, dproj, batch, seq, d, hd):
    heads = d // hd
    t = _tile(seq, SB_TILE_BWD)
    sw = _tile(t, SB_SCAN)
    nb = t // sw
    scale = hd ** -0.5
    nblk = seq // t
    nh = SB_HEADS
    wide = nh * hd
    hs = range(nh)
    cols = [slice(hh * hd, (hh + 1) * hd) for hh in hs]
    blocks = [slice(b * sw, (b + 1) * sw) for b in range(nb)]
    last = slice(sw - 1, sw)

    def compute(qs, ks, v_ref, zb_ref, dyb_ref, tot_ref, kts, vts, dos, res, upto, before, dq):
        for jb in range(nblk):
            rows = slice(jb * t, (jb + 1) * t)
            kts[jb] = ks[rows, :].astype(F32).T.astype(BF16)
            vts[jb] = v_ref[rows, :].astype(F32).T.astype(BF16)
        sz, _ = _silu(zb_ref[...].astype(F32))
        dos[...] = (dyb_ref[...].astype(F32) * sz).astype(BF16)
        res[1] = jnp.zeros((seq, wide), F32)
        res[2] = jnp.zeros((seq, wide), F32)
        row, col = _iotas(t)
        upto[...] = (row[:sw, :sw] <= col[:sw, :sw]).astype(BF16)
        before[...] = (row[:sw, :sw] < col[:sw, :sw]).astype(BF16)

        def qblock(i, carry):
            r0 = pl.multiple_of(i * t, t)

            def tile(j, sums, valid):
                c0 = pl.multiple_of(j * t, t)
                q_i = [qs[pl.ds(r0, t), cs] for cs in cols]
                do_i = [dos[pl.ds(r0, t), cs] for cs in cols]
                logs = [_sb_logs(_dot(q_i[hh], kts[j, cols[hh], :]), scale, valid) for hh in hs]
                dw = [_dot(do_i[hh], vts[j, cols[hh], :]) for hh in hs]
                scans = [_dot(jnp.concatenate([logs[hh][1][:, ks_] for ks_ in blocks], axis=0), upto[...]) for hh in hs]
                ws, gs, new_runs = [], [], []
                for hh in hs:
                    left = tot_ref[hh, pl.ds(r0, t), :] - sums[hh][0]
                    w_b, g_b = [], []
                    for b, ks_ in enumerate(blocks):
                        inside = scans[hh][b * t:(b + 1) * t]
                        w = jnp.exp(logs[hh][0][:, ks_].astype(F32) + (left - inside))
                        w_b.append(w.astype(BF16))
                        g_b.append((dw[hh][:, ks_] * w).astype(BF16))
                        left = left - inside[:, last]
                    ws.append(jnp.concatenate(w_b, axis=1))
                    gs.append(g_b)
                    new_runs.append(tot_ref[hh, pl.ds(r0, t), :] - left)
                gscans = [_dot(jnp.concatenate(gs[hh], axis=0), before[...]) for hh in hs]
                dzs, new_gruns = [], []
                for hh in hs:
                    g_before = sums[hh][1]
                    dz_b = []
                    for b, ks_ in enumerate(blocks):
                        inside = gscans[hh][b * t:(b + 1) * t]
                        beta = jnp.exp(logs[hh][0][:, ks_]).astype(F32)
                        g = gs[hh][b].astype(F32)
                        dz_b.append(((g - (g + inside + g_before) * beta) * scale).astype(BF16))
                        g_before = g_before + inside[:, last] + g[:, last]
                    dzs.append(jnp.concatenate(dz_b, axis=1))
                    new_gruns.append(g_before)
                for hh in hs:
                    res[2, pl.ds(c0, t), cols[hh]] += _dot_tn(ws[hh], do_i[hh])
                for hh in hs:
                    res[1, pl.ds(c0, t), cols[hh]] += _dot_tn(dzs[hh], q_i[hh])
                for hh in hs:
                    dq[:, cols[hh]] += _dot(dzs[hh], ks[pl.ds(c0, t), cols[hh]])
                return tuple((new_runs[hh], new_gruns[hh]) for hh in hs)

            zero = jnp.zeros((t, 1), F32)
            dq[...] = jnp.zeros_like(dq)
            sums = lax.fori_loop(0, i, lambda j, sm: tile(j, sm, None), ((zero, zero),) * nh)
            tile(i, sums, col < row)
            res[0, pl.ds(r0, t), :] = dq[...]
            return carry

        lax.fori_loop(0, nblk, qblock, 0)

    def body(qs, ks, v_ref, zb_ref, o_ref, dyb_ref, tot_ref, dproj_in, out_ref,
             kts, vts, dos, res, upto, before, dq):
        del dproj_in
        kk = pl.program_id(2)

        @pl.when(kk == 0)
        def _():
            compute(qs, ks, v_ref, zb_ref, dyb_ref, tot_ref, kts, vts, dos, res, upto, before, dq)

        @pl.when(kk < 3)
        def _():
            out_ref[...] = res[kk].astype(BF16)

        @pl.when(kk == 3)
        def _():
            _, dsz = _silu(zb_ref[...].astype(F32))
            out_ref[...] = (dyb_ref[...].astype(F32) * o_ref[...].astype(F32) * dsz).astype(BF16)

    col0 = d // wide
    seg = lambda k: pl.BlockSpec((seq, wide), lambda b, h, kk: (b, k * col0 + h))
    head = pl.BlockSpec((seq, wide), lambda b, h, kk: (b, h))
    return pl.pallas_call(
        body, name="sb_bwd", grid=(batch, heads // nh, 4),
        in_specs=[seg(3), seg(4), seg(5), seg(6), head, head,
                  pl.BlockSpec((nh, seq, 1), lambda b, h, kk: (b * (heads // nh) + h, 0, 0)),
                  pl.BlockSpec(memory_space=pl.ANY)],
        out_specs=pl.BlockSpec((seq, wide), lambda b, h, kk: (b, (3 + kk) * col0 + h)),
        out_shape=SDS(dproj.shape, dproj.dtype),
        input_output_aliases={7: 0},
        scratch_shapes=[pltpu.VMEM((nblk, wide, t), BF16)] * 2 + [
            pltpu.VMEM((seq, wide), BF16), pltpu.VMEM((3, seq, wide), F32),
            pltpu.VMEM((sw, sw), BF16), pltpu.VMEM((sw, sw), BF16), pltpu.VMEM((t, wide), F32)],
        compiler_params=_params(("arbitrary", "arbitrary", "arbitrary")),
    )(proj, proj, proj, proj, o, dyb, tot, dproj)


def _branch_a_bwd(proj, dya, norm_v, w_s, b_col, dproj):
    n = proj.shape[0]
    d = norm_v.shape[1]
    groups, chunk, _ = w_s.shape
    tr = _tile(n, 2 * chunk)

    def body(u_ref, v_ref, z_ref, dya_ref, gv_ref, ws_ref, b_ref, dproj_in,
             out_ref, dws_ref, dbias_ref, dgv_ref, vn_s, dmix_s, dvn_s, db_ref):
        del dproj_in

        @pl.when(pl.program_id(0) == 0)
        def _():
            dws_ref[...] = jnp.zeros_like(dws_ref)
            db_ref[...] = jnp.zeros_like(db_ref)
            dgv_ref[...] = jnp.zeros_like(dgv_ref)

        row, col = _iotas(chunk)
        tril = col <= row
        u, v, z, dya_v = (r[...].astype(F32) for r in (u_ref, v_ref, z_ref, dya_ref))
        gv = gv_ref[...]
        vg, dvg_dv = _gelu(v)
        r = _rms_scale(vg)
        vh = vg * r
        vn_s[...] = (vh * gv).astype(BF16)
        ug, dug_du = _gelu(u)
        sz, dsz = _silu(z)
        dmix_s[...] = dya_v * ug * sz
        for g in range(groups):
            wm = jnp.where(tril, ws_ref[g], 0.0).astype(BF16)
            cs = slice(g * chunk, (g + 1) * chunk)
            for c in range(tr // chunk):
                rs = slice(c * chunk, (c + 1) * chunk)
                vn = vn_s[rs, cs]
                mixed = _dot(wm, vn) + b_ref[g]
                dmix = dmix_s[rs, cs]
                dmix16 = dmix.astype(BF16)
                dws_ref[g] += _dot_nt(dmix16, vn)
                db_ref[g] += dmix
                dvn_s[rs, cs] = _dot_tn(wm, dmix16)
                t_u = dya_v[rs, cs] * mixed
                out_ref[rs, g * chunk:(g + 1) * chunk] = (t_u * sz[rs, cs] * dug_du[rs, cs]).astype(BF16)
                out_ref[rs, 2 * d + g * chunk:2 * d + (g + 1) * chunk] = (t_u * ug[rs, cs] * dsz[rs, cs]).astype(BF16)
        dvn = dvn_s[...]
        dgv_ref[...] += jnp.sum(dvn * vh, axis=0, keepdims=True)
        dvh = dvn * gv
        dvg = r * (dvh - vh * jnp.mean(dvh * vh, axis=-1, keepdims=True))
        out_ref[:, d:2 * d] = (dvg * dvg_dv).astype(BF16)

        @pl.when(pl.program_id(0) == n // tr - 1)
        def _():
            for g in range(groups):
                dbias_ref[g:g + 1, :] = jnp.sum(db_ref[g].T, axis=0, keepdims=True)

    seg = lambda k: pl.BlockSpec((tr, d), lambda i: (i, k))
    return pl.pallas_call(
        body, name="branch_a_bwd", grid=(n // tr,),
        in_specs=[seg(0), seg(1), seg(2), seg(0),
                  pl.BlockSpec((1, d), lambda i: (0, 0)),
                  pl.BlockSpec((groups, chunk, chunk), lambda i: (0, 0, 0)),
                  pl.BlockSpec((groups, chunk, 1), lambda i: (0, 0, 0)),
                  pl.BlockSpec(memory_space=pl.ANY)],
        out_specs=[pl.BlockSpec((tr, 3 * d), lambda i: (i, 0)),
                   pl.BlockSpec((groups, chunk, chunk), lambda i: (0, 0, 0)),
                   pl.BlockSpec((groups, chunk), lambda i: (0, 0)),
                   pl.BlockSpec((1, d), lambda i: (0, 0))],
        out_shape=[SDS(dproj.shape, dproj.dtype), SDS((groups, chunk, chunk), F32),
                   SDS((groups, chunk), F32), SDS((1, d), F32)],
        input_output_aliases={7: 0},
        scratch_shapes=[pltpu.VMEM((tr, d), BF16), pltpu.VMEM((tr, d), F32), pltpu.VMEM((tr, d), F32),
                        pltpu.VMEM((groups, chunk, chunk), F32)],
        compiler_params=_params(("arbitrary",)),
    )(proj, proj, proj, dya, norm_v, w_s, b_col, dproj)


def _dx(dproj, wg_in, x2d, dx2, norm_in):
    n, d = x2d.shape
    nsh, _, esh = wg_in.shape
    tm = _tile(n, 1024)

    def body(dp_ref, w_ref, x_ref, dx2_ref, g_ref, gx_ref, dg_ref, acc):
        i, k = pl.program_id(0), pl.program_id(1)

        @pl.when(jnp.logical_and(i == 0, k == 0))
        def _():
            dg_ref[...] = jnp.zeros_like(dg_ref)

        @pl.when(k == 0)
        def _():
            acc[...] = jnp.zeros_like(acc)

        acc[...] += _dot_nt(dp_ref[...], w_ref[0])

        @pl.when(k == nsh - 1)
        def _():
            dh = acc[...]
            x = x_ref[...]
            r = _rms_scale(x)
            xh = x * r
            dg_ref[...] += jnp.sum(dh * xh, axis=0, keepdims=True)
            dxh = dh * g_ref[...]
            gx_ref[...] = dx2_ref[...] + r * (dxh - xh * jnp.mean(dxh * xh, axis=-1, keepdims=True))

    rows = pl.BlockSpec((tm, d), lambda i, k: (i, 0))
    vec = pl.BlockSpec((1, d), lambda i, k: (0, 0))
    return pl.pallas_call(
        body, name="dx", grid=(n // tm, nsh),
        in_specs=[pl.BlockSpec((tm, esh), lambda i, k: (i, k)),
                  pl.BlockSpec((1, d, esh), lambda i, k: (k, 0, 0)), rows, rows, vec],
        out_specs=[rows, vec],
        out_shape=[SDS((n, d), F32), SDS((1, d), F32)],
        scratch_shapes=[pltpu.VMEM((tm, d), F32)],
        compiler_params=_params(("arbitrary", "arbitrary")),
    )(dproj, wg_in, x2d, dx2, norm_in)


def _adamw_outputs(g_ref, d_ref, m_ref, v_ref, g, w, m, v):
    delta, m2, v2 = _adamw(w, g, m, v)
    g_ref[...] = g
    d_ref[...] = delta
    m_ref[...] = m2
    v_ref[...] = v2


def _reduce_adamw(slots, w, m, v, name):
    _, r, c = slots.shape
    tr = _tile(r, 128)

    def body(s_ref, w_ref, m_ref, v_ref, g_out, d_out, m_out, v_out):
        g = s_ref[0].astype(F32)
        for k in range(1, N_DEV):
            g = g + s_ref[k].astype(F32)
        _adamw_outputs(g_out, d_out, m_out, v_out, g, w_ref[...], m_ref[...], v_ref[...])

    blk = pl.BlockSpec((tr, c), lambda i: (i, 0))
    return pl.pallas_call(
        body, name=name, grid=(r // tr,),
        in_specs=[pl.BlockSpec((N_DEV, tr, c), lambda i: (0, i, 0)), blk, blk, blk],
        out_specs=[blk] * 4,
        out_shape=[SDS((r, c), F32)] * 4,
        compiler_params=_params(("parallel",)),
    )(slots, w, m, v)


def _adamw_small(g, w, m, v, name):
    def body(g_ref, w_ref, m_ref, v_ref, g_out, d_out, m_out, v_out):
        _adamw_outputs(g_out, d_out, m_out, v_out, g_ref[...], w_ref[...], m_ref[...], v_ref[...])

    return pl.pallas_call(
        body, name=name,
        out_shape=[SDS(g.shape, F32)] * 4,
        in_specs=[pl.BlockSpec(memory_space=pltpu.VMEM)] * 4,
        out_specs=[pl.BlockSpec(memory_space=pltpu.VMEM)] * 4,
    )(g, w, m, v)


def kernel(x, norm_in, w_in, norm_v, w_s, b_s, w_o_gmlp, w_o_sb, w_out, norm_final, loss_target, m_norm_in, m_w_in, m_norm_v, m_w_s, m_b_s, m_w_o_gmlp, m_w_o_sb, m_w_out, m_norm_final, v_norm_in, v_w_in, v_norm_v, v_w_s, v_b_s, v_w_o_gmlp, v_w_o_sb, v_w_out, v_norm_final):
    batch, seq, d = x.shape
    n = batch * seq
    groups, chunk = w_s.shape[1], w_s.shape[2]
    hd = LANE
    x2d = x.reshape(n, d)
    tgt = loss_target.reshape(n, d)
    b_col = b_s[0].reshape(groups, chunk, 1)
    norm_final2 = norm_final.reshape(1, d)

    wg_in, wg_oa, wg_ob, wg_out = _gather_weights([w_in[0], w_o_gmlp[0], w_o_sb[0], w_out[0]])
    rsh = wg_oa.shape[1]
    wf_oa, wf_ob, wf_out = (w.reshape(N_DEV * rsh, d) for w in (wg_oa, wg_ob, wg_out))

    proj, h = _in_proj(x2d, norm_in, wg_in)
    ya = _branch_a_fwd(proj, norm_v, w_s[0], b_col)
    yb, o, sb_tot = _sb_fwd(proj, batch, seq, d, hd)
    dproj, dx2, dya, dyb, merged, dpa, dpb, loss_vec, dgf = _tail(
        x2d, tgt, ya, yb, proj, wf_oa, wf_ob, wf_out, norm_final2)
    gp_oa = _tn_matmul(ya, dpa, "dw_o_gmlp")
    gp_ob = _tn_matmul(yb, dpb, "dw_o_sb")
    gp_out = _tn_matmul(merged, dx2, "dw_out")
    dproj = _sb_bwd(proj, o, dyb, sb_tot, dproj, batch, seq, d, hd)
    dproj, gp_ws, gp_b, gp_nv = _branch_a_bwd(proj, dya, norm_v, w_s[0], b_col, dproj)
    grad_x, gp_nin = _dx(dproj, wg_in, x2d, dx2, norm_in)

    slab = lambda a: a.reshape(d // LANE, LANE)
    gc = groups * chunk
    packed = jnp.concatenate(
        [gp_ws.reshape(gc, chunk), gp_b, slab(gp_nin), slab(gp_nv), slab(dgf), slab(loss_vec)], axis=0)
    my_slot = _slot(_me()).astype(jnp.int32).reshape(1)
    s_win, s_oa, s_ob, s_out, packs = _dw_in_exchange(
        h, dproj, my_slot, [g.reshape(N_DEV, rsh, d) for g in (gp_oa, gp_ob, gp_out)], packed)
    tot, loss_slab = _finish_small(packs, groups, chunk)
    ns = d // LANE
    g_ws = tot[:gc]
    g_b = tot[gc:gc + groups]
    g_nin, g_nv, g_nf = (tot[gc + groups + k * ns:gc + groups + (k + 1) * ns] for k in range(3))
    loss = loss_slab[0, 0]

    res = {}
    res["w_in"] = _reduce_adamw(s_win, w_in[0], m_w_in[0], v_w_in[0], "adamw_w_in")
    res["w_o_gmlp"] = _reduce_adamw(s_oa, w_o_gmlp[0], m_w_o_gmlp[0], v_w_o_gmlp[0], "adamw_w_o_gmlp")
    res["w_o_sb"] = _reduce_adamw(s_ob, w_o_sb[0], m_w_o_sb[0], v_w_o_sb[0], "adamw_w_o_sb")
    res["w_out"] = _reduce_adamw(s_out, w_out[0], m_w_out[0], v_w_out[0], "adamw_w_out")
    res["norm_in"] = _adamw_small(g_nin, slab(norm_in), slab(m_norm_in), slab(v_norm_in), "adamw_norm_in")
    res["norm_v"] = _adamw_small(g_nv, slab(norm_v), slab(m_norm_v), slab(v_norm_v), "adamw_norm_v")
    res["norm_final"] = _adamw_small(g_nf, slab(norm_final), slab(m_norm_final), slab(v_norm_final), "adamw_norm_final")
    res["w_s"] = _adamw_small(g_ws, w_s.reshape(gc, chunk), m_w_s.reshape(gc, chunk), v_w_s.reshape(gc, chunk), "adamw_w_s")
    res["b_s"] = _adamw_small(g_b, b_s[0], m_b_s[0], v_b_s[0], "adamw_b_s")

    shapes = {"norm_in": norm_in.shape, "w_in": w_in.shape, "norm_v": norm_v.shape, "w_s": w_s.shape,
              "b_s": b_s.shape, "w_o_gmlp": w_o_gmlp.shape, "w_o_sb": w_o_sb.shape, "w_out": w_out.shape,
              "norm_final": norm_final.shape}
    names = list(shapes)
    outs = [loss, grad_x.reshape(batch, seq, d)]
    for kind in range(4):
        outs += [res[name][kind].reshape(shapes[name]) for name in names]
    return tuple(outs)
```

```python
import functools
import math

import jax
import jax.numpy as jnp
from jax import lax
from jax.experimental import pallas as pl
from jax.experimental.pallas import tpu as pltpu

F32 = jnp.float32
BF16 = jnp.bfloat16
SDS = jax.ShapeDtypeStruct
MESH_ID = pl.DeviceIdType.MESH

N_DEV = 8
LANE = 128
SUBLANE = 8
VMEM_LIMIT = 56 * 1024 * 1024
SB_TILE = 512
SB_TILE_BWD = 512
SB_SCAN = 256
SB_HEADS = 2
MASKED_LOG = -1e30
RMS_EPS = 1e-6

ADAM_LR = 0.001
ADAM_B1 = 0.9
ADAM_B2 = 0.999
ADAM_EPS = 1e-08
ADAM_WD = 0.01
ADAM_STEP = 10

NT_DIMS = (((1,), (1,)), ((), ()))
TN_DIMS = (((0,), (0,)), ((), ()))


def _params(semantics=None):
    return pltpu.CompilerParams(dimension_semantics=semantics, vmem_limit_bytes=VMEM_LIMIT)


def _tile(n, preferred):
    t = min(n, preferred)
    assert n % t == 0, (n, t)
    return t


def _sigmoid(x):
    return 1.0 / (1.0 + jnp.exp(-x))


def _silu(x):
    s = _sigmoid(x)
    return x * s, s * (1.0 + x * (1.0 - s))


def _gelu(x):
    k = math.sqrt(2.0 / math.pi)
    x2 = x * x
    t = jnp.tanh(k * (x + 0.044715 * (x * x2)))
    cdf = 0.5 * (1.0 + t)
    return x * cdf, cdf + 0.5 * x * (1.0 - t * t) * (k * (1.0 + 3.0 * 0.044715 * x2))


def _rms_scale(x):
    return lax.rsqrt(jnp.mean(x * x, axis=-1, keepdims=True) + RMS_EPS)


def _iotas(n):
    return (lax.broadcasted_iota(jnp.int32, (n, n), 0), lax.broadcasted_iota(jnp.int32, (n, n), 1))


def _adamw(w, g, m, v):
    m = ADAM_B1 * m + (1.0 - ADAM_B1) * g
    v = ADAM_B2 * v + (1.0 - ADAM_B2) * (g * g)
    m_hat = m / (1.0 - ADAM_B1 ** ADAM_STEP)
    v_hat = v / (1.0 - ADAM_B2 ** ADAM_STEP)
    delta = -ADAM_LR * (m_hat / (jnp.sqrt(v_hat) + ADAM_EPS) + ADAM_WD * w)
    return delta, m, v


def _dot(a, b):
    return jnp.dot(a, b, preferred_element_type=F32)


def _dot_nt(a, b):
    return lax.dot_general(a, b, NT_DIMS, preferred_element_type=F32)


def _dot_tn(a, b):
    return lax.dot_general(a, b, TN_DIMS, preferred_element_type=F32)


def _sb_logs(raw, scale, valid):
    z = (raw * scale).astype(BF16)
    log_beta = jnp.minimum(z, 0) - jnp.log(1 + jnp.exp(-jnp.abs(z)))
    log_rest = log_beta - z
    if valid is not None:
        log_beta = jnp.where(valid, log_beta, MASKED_LOG)
        log_rest = jnp.where(valid, log_rest, 0)
    return log_beta, log_rest


def _me():
    return lax.axis_index("x"), lax.axis_index("y"), lax.axis_index("c")


def _slot(p):
    return 4 * p[0] + 2 * p[1] + p[2]


def _peer(me, k):
    flips = ((k >> 2) & 1, (k >> 1) & 1, k & 1)
    return tuple(1 - a if f else a for a, f in zip(me, flips))


def _gather_weights(shards):
    n = len(shards)

    def body(*refs):
        ins, outs, stage = refs[:n], refs[n:2 * n], refs[2 * n:3 * n]
        send_sems, recv_sems, local_sems = refs[3 * n:]
        x, y, c = _me()
        me, sibling = (x, y, c), (x, y, 1 - c)
        chips = [(1 - x, y), (x, 1 - y), (1 - x, 1 - y)]

        def copy(a, k, block, to, src=None):
            dst = outs[a].at[_slot(block)]
            return pltpu.make_async_remote_copy(
                src_ref=dst if src is None else src, dst_ref=dst,
                send_sem=send_sems.at[7 * a + k], recv_sem=recv_sems.at[7 * a + k],
                device_id=to, device_id_type=MESH_ID)

        started = []
        for a in range(n):
            stage[a][...] = ins[a][...].astype(BF16)
            mine = pltpu.make_async_copy(stage[a], outs[a].at[_slot(me)], local_sems.at[a])
            mine.start()
            started.append(mine)
        sends = []
        for a in range(n):
            sends.append(copy(a, 0, me, sibling, src=stage[a]))
            sends += [copy(a, 1 + j, me, (*chip, c), src=stage[a]) for j, chip in enumerate(chips)]
        for cp in sends:
            cp.start()
        for a in range(n):
            for j, chip in enumerate(chips):
                copy(a, 1 + j, (*chip, c), me).wait_recv()
                passed = copy(a, 4 + j, (*chip, c), sibling)
                passed.start()
                sends.append(passed)
        for a in range(n):
            copy(a, 0, sibling, me).wait_recv()
            for j, chip in enumerate(chips):
                copy(a, 4 + j, (*chip, 1 - c), me).wait_recv()
        for cp in sends:
            cp.wait_send()
        for mine in started:
            mine.wait()

    return pl.pallas_call(
        body, name="gather_weights",
        out_shape=[SDS((N_DEV,) + s.shape, BF16) for s in shards],
        in_specs=[pl.BlockSpec(memory_space=pltpu.VMEM)] * n,
        out_specs=[pl.BlockSpec(memory_space=pl.ANY)] * n,
        scratch_shapes=[pltpu.VMEM(s.shape, BF16) for s in shards] + [
            pltpu.SemaphoreType.DMA((7 * n,)), pltpu.SemaphoreType.DMA((7 * n,)),
            pltpu.SemaphoreType.DMA((n,))],
        compiler_params=pltpu.CompilerParams(vmem_limit_bytes=VMEM_LIMIT),
    )(*shards)


def _dw_in_exchange(h, dproj, my_slot, stacks, packed):
    n, d = h.shape
    esh = dproj.shape[1] // N_DEV
    tk = _tile(n, 512)
    nk = n // tk
    ns = len(stacks)
    last_j = N_DEV - 1

    def body(me_ref, h_ref, dp_ref, *refs):
        del me_ref
        st_in, pk_in = refs[:ns], refs[ns]
        win_out, st_out, pk_out = refs[ns + 1], refs[ns + 2:2 * ns + 2], refs[2 * ns + 2]
        acc, sendbuf, win_send, win_recv, send_sems, recv_sems, local_sems = refs[2 * ns + 3:]
        j, k = pl.program_id(0), pl.program_id(1)
        me = _me()
        mine = _slot(me)

        def ready_copies():
            local = [pltpu.make_async_copy(st_in[a].at[mine], st_out[a].at[mine], local_sems.at[a])
                     for a in range(ns)]
            local.append(pltpu.make_async_copy(pk_in, pk_out.at[mine], local_sems.at[ns]))
            remote = []
            for kk in range(1, N_DEV):
                peer = _peer(me, kk)
                for a in range(ns):
                    remote.append(pltpu.make_async_remote_copy(
                        src_ref=st_in[a].at[_slot(peer)], dst_ref=st_out[a].at[mine],
                        send_sem=send_sems.at[(ns + 1) * (kk - 1) + a],
                        recv_sem=recv_sems.at[(ns + 1) * (kk - 1) + a],
                        device_id=peer, device_id_type=MESH_ID))
                remote.append(pltpu.make_async_remote_copy(
                    src_ref=pk_in, dst_ref=pk_out.at[mine],
                    send_sem=send_sems.at[(ns + 1) * (kk - 1) + ns],
                    recv_sem=recv_sems.at[(ns + 1) * (kk - 1) + ns],
                    device_id=peer, device_id_type=MESH_ID))
            return local, remote

        def shard_copy(jj):
            owner = (mine + 1 + jj) % N_DEV
            return pltpu.make_async_remote_copy(
                src_ref=sendbuf.at[jj % 2], dst_ref=win_out.at[mine],
                send_sem=win_send.at[jj % 2], recv_sem=win_recv.at[mine],
                device_id=(owner // 4, (owner // 2) % 2, owner % 2), device_id_type=MESH_ID)

        def own_copy():
            return pltpu.make_async_copy(sendbuf.at[last_j % 2], win_out.at[mine], local_sems.at[ns + 1])

        @pl.when(jnp.logical_and(j == 0, k == 0))
        def _():
            local, remote = ready_copies()
            for cp in local + remote:
                cp.start()

        @pl.when(k == 0)
        def _():
            acc[...] = jnp.zeros_like(acc)

        acc[...] += _dot_tn(h_ref[...], dp_ref[...])

        @pl.when(k == nk - 1)
        def _():
            @pl.when(j >= 2)
            def _():
                shard_copy(j - 2).wait_send()

            sendbuf[j % 2] = acc[...].astype(BF16)

            @pl.when(j < last_j)
            def _():
                shard_copy(j).start()

            @pl.when(j == last_j)
            def _():
                own_copy().start()
                shard_copy(last_j - 1).wait_send()
                own_copy().wait()
                for src in range(N_DEV):
                    @pl.when(src != mine)
                    def _():
                        landed = win_out.at[src]
                        pltpu.make_async_remote_copy(
                            src_ref=landed, dst_ref=landed, send_sem=win_send.at[0], recv_sem=win_recv.at[src],
                            device_id=me, device_id_type=MESH_ID).wait_recv()
                local, remote = ready_copies()
                for cp in remote:
                    cp.wait_send()
                idx = 0
                for kk in range(1, N_DEV):
                    peer = _slot(_peer(me, kk))
                    for a in range(ns + 1):
                        landed = pk_out.at[peer] if a == ns else st_out[a].at[peer]
                        pltpu.make_async_remote_copy(
                            src_ref=landed, dst_ref=landed, send_sem=send_sems.at[idx], recv_sem=recv_sems.at[idx],
                            device_id=me, device_id_type=MESH_ID).wait_recv()
                        idx += 1
                for cp in local:
                    cp.wait()

    any_spec = pl.BlockSpec(memory_space=pl.ANY)
    n_ready = 7 * (ns + 1)
    grid_spec = pltpu.PrefetchScalarGridSpec(
        num_scalar_prefetch=1, grid=(N_DEV, nk),
        in_specs=[pl.BlockSpec((tk, d), lambda j, k, me: (k, 0)),
                  pl.BlockSpec((tk, esh), lambda j, k, me: (k, (me[0] + 1 + j) % N_DEV))] + [any_spec] * (ns + 1),
        out_specs=[any_spec] * (ns + 2),
        scratch_shapes=[pltpu.VMEM((d, esh), F32), pltpu.VMEM((2, d, esh), BF16),
                        pltpu.SemaphoreType.DMA((2,)), pltpu.SemaphoreType.DMA((N_DEV,)),
                        pltpu.SemaphoreType.DMA((n_ready,)), pltpu.SemaphoreType.DMA((n_ready,)),
                        pltpu.SemaphoreType.DMA((ns + 2,))])
    return pl.pallas_call(
        body, name="dw_in_exchange", grid_spec=grid_spec,
        out_shape=[SDS((N_DEV, d, esh), BF16)] + [SDS(s.shape, s.dtype) for s in stacks] + [
            SDS((N_DEV,) + packed.shape, packed.dtype)],
        compiler_params=_params(("arbitrary", "arbitrary")),
    )(my_slot, h, dproj, *stacks, packed)


def _finish_small(packs, groups, chunk):
    rows = packs.shape[1]
    gc = groups * chunk

    def body(p_ref, sum_ref, loss_ref):
        row, col = _iotas(chunk)
        tril = col <= row
        for g in range(groups):
            rs = slice(g * chunk, (g + 1) * chunk)
            tot = p_ref[0, rs, :]
            for dev in range(1, N_DEV):
                tot = tot + p_ref[dev, rs, :]
            sum_ref[rs, :] = jnp.where(tril, tot, 0.0)
        rs = slice(gc, rows)
        tot = p_ref[0, rs, :]
        for dev in range(1, N_DEV):
            tot = tot + p_ref[dev, rs, :]
        sum_ref[rs, :] = tot
        loss_ref[...] = jnp.full((SUBLANE, LANE), jnp.sum(tot[rows - gc - SUBLANE:, :]), F32)

    return pl.pallas_call(
        body, name="finish_small",
        out_shape=[SDS((rows, LANE), F32), SDS((SUBLANE, LANE), F32)],
        in_specs=[pl.BlockSpec(memory_space=pltpu.VMEM)],
        out_specs=[pl.BlockSpec(memory_space=pltpu.VMEM)] * 2,
        compiler_params=pltpu.CompilerParams(vmem_limit_bytes=VMEM_LIMIT),
    )(packs)


def _in_proj(x2d, norm_in, wg_in):
    n, d = x2d.shape
    nsh, _, esh = wg_in.shape
    tm = _tile(n, 1024)

    def body(x_ref, g_ref, w_ref, proj_ref, h_ref):
        @pl.when(pl.program_id(1) == 0)
        def _():
            x = x_ref[...]
            h_ref[...] = (x * _rms_scale(x) * g_ref[...]).astype(BF16)

        proj_ref[...] = _dot(h_ref[...], w_ref[0]).astype(BF16)

    return pl.pallas_call(
        body, name="in_proj", grid=(n // tm, nsh),
        in_specs=[pl.BlockSpec((tm, d), lambda i, j: (i, 0)),
                  pl.BlockSpec((1, d), lambda i, j: (0, 0)),
                  pl.BlockSpec((1, d, esh), lambda i, j: (j, 0, 0))],
        out_specs=[pl.BlockSpec((tm, esh), lambda i, j: (i, j)),
                   pl.BlockSpec((tm, d), lambda i, j: (i, 0))],
        out_shape=[SDS((n, nsh * esh), BF16), SDS((n, d), BF16)],
        compiler_params=_params(("parallel", "arbitrary")),
    )(x2d, norm_in, wg_in)


def _branch_a_fwd(proj, norm_v, w_s, b_col):
    n = proj.shape[0]
    d = norm_v.shape[1]
    groups, chunk, _ = w_s.shape
    tr = _tile(n, 4 * chunk)

    def body(u_ref, v_ref, z_ref, gv_ref, ws_ref, b_ref, ya_ref, vn_s, pre_s):
        row, col = _iotas(chunk)
        tril = col <= row
        vg, _ = _gelu(v_ref[...].astype(F32))
        vn_s[...] = (vg * _rms_scale(vg) * gv_ref[...]).astype(BF16)
        ug, _ = _gelu(u_ref[...].astype(F32))
        sz, _ = _silu(z_ref[...].astype(F32))
        pre_s[...] = ug * sz
        for g in range(groups):
            wm = jnp.where(tril, ws_ref[g], 0.0).astype(BF16)
            cs = slice(g * chunk, (g + 1) * chunk)
            for c in range(tr // chunk):
                rs = slice(c * chunk, (c + 1) * chunk)
                mixed = _dot(wm, vn_s[rs, cs]) + b_ref[g]
                ya_ref[rs, cs] = (pre_s[rs, cs] * mixed).astype(BF16)

    seg = lambda k: pl.BlockSpec((tr, d), lambda i: (i, k))
    return pl.pallas_call(
        body, name="branch_a_fwd", grid=(n // tr,),
        in_specs=[seg(0), seg(1), seg(2),
                  pl.BlockSpec((1, d), lambda i: (0, 0)),
                  pl.BlockSpec((groups, chunk, chunk), lambda i: (0, 0, 0)),
                  pl.BlockSpec((groups, chunk, 1), lambda i: (0, 0, 0))],
        out_specs=pl.BlockSpec((tr, d), lambda i: (i, 0)),
        out_shape=SDS((n, d), BF16),
        scratch_shapes=[pltpu.VMEM((tr, d), BF16), pltpu.VMEM((tr, d), F32)],
        compiler_params=_params(("parallel",)),
    )(proj, proj, proj, norm_v, w_s, b_col)


def _sb_fwd(proj, batch, seq, d, hd):
    heads = d // hd
    t = _tile(seq, SB_TILE)
    sw = _tile(t, SB_SCAN)
    nb = t // sw
    scale = hd ** -0.5
    nblk = seq // t
    nh = SB_HEADS
    wide = nh * hd
    cols = [slice(hh * hd, (hh + 1) * hd) for hh in range(nh)]

    def body(qs, k_ref, vs, zb_ref, yb_ref, o_ref, tot_ref, kts, later, acc):
        for jb in range(nblk):
            kts[jb] = k_ref[jb * t:(jb + 1) * t, :].astype(F32).T.astype(BF16)
        row, col = _iotas(t)
        later[...] = (row[:sw, :sw] > col[:sw, :sw]).astype(BF16)

        def qblock(i, carry):
            r0 = pl.multiple_of(i * t, t)

            def tile(j, runs, valid):
                c0 = pl.multiple_of(j * t, t)
                logs = [_sb_logs(_dot(qs[pl.ds(r0, t), cs], kts[j, cs, :]), scale, valid) for cs in cols]
                scans = [_dot(jnp.concatenate([logs[hh][1][:, b * sw:(b + 1) * sw] for b in range(nb)], axis=0),
                              later[...]) for hh in range(nh)]
                new_runs = []
                for hh in range(nh):
                    after = runs[hh]
                    blocks = [None] * nb
                    for b in reversed(range(nb)):
                        ks_ = slice(b * sw, (b + 1) * sw)
                        inside = scans[hh][b * t:(b + 1) * t]
                        blocks[b] = jnp.exp(logs[hh][0][:, ks_].astype(F32) + inside + after).astype(BF16)
                        after = after + inside[:, 0:1] + logs[hh][1][:, b * sw:b * sw + 1].astype(F32)
                    new_runs.append(after)
                    pv = _dot(jnp.concatenate(blocks, axis=1), vs[pl.ds(c0, t), cols[hh]])
                    if valid is None:
                        acc[:, cols[hh]] += pv
                    else:
                        acc[:, cols[hh]] = pv
                return tuple(new_runs)

            runs = tile(i, (jnp.zeros((t, 1), F32),) * nh, col < row)
            runs = lax.fori_loop(0, i, lambda jj, rs: tile(i - 1 - jj, rs, None), runs)
            for hh in range(nh):
                out = acc[:, cols[hh]]
                o_ref[pl.ds(r0, t), cols[hh]] = out.astype(BF16)
                tot_ref[hh, pl.ds(r0, t), :] = runs[hh]
                sz, _ = _silu(zb_ref[pl.ds(r0, t), cols[hh]].astype(F32))
                yb_ref[pl.ds(r0, t), cols[hh]] = (out * sz).astype(BF16)
            return carry

        lax.fori_loop(0, nblk, qblock, 0)

    col0 = d // wide
    seg = lambda k: pl.BlockSpec((seq, wide), lambda b, h: (b, k * col0 + h))
    return pl.pallas_call(
        body, name="sb_fwd", grid=(batch, heads // nh),
        in_specs=[seg(3), seg(4), seg(5), seg(6)],
        out_specs=[pl.BlockSpec((seq, wide), lambda b, h: (b, h))] * 2 + [
            pl.BlockSpec((nh, seq, 1), lambda b, h: (b * (heads // nh) + h, 0, 0))],
        out_shape=[SDS((batch * seq, d), BF16), SDS((batch * seq, d), BF16), SDS((batch * heads, seq, 1), F32)],
        scratch_shapes=[pltpu.VMEM((nblk, wide, t), BF16), pltpu.VMEM((sw, sw), BF16), pltpu.VMEM((t, wide), F32)],
        compiler_params=_params(("parallel", "parallel")),
    )(proj, proj, proj, proj)


def _tail(x2d, tgt, ya, yb, proj, w_oa, w_ob, w_out, norm_final):
    n, d = x2d.shape
    e = proj.shape[1]
    tm = _tile(n, 256)

    steps = n // tm

    def body(x_ref, t_ref, ya_ref, yb_ref, ga_ref, gb_ref, woa_ref, wob_ref, wout_ref, gf_ref,
             dproj_ref, dx2_ref, dya_ref, dyb_ref, mrg_ref, dpa_ref, dpb_ref, loss_ref, dgf_ref, dg_s, dg_sems):
        i = pl.program_id(0)

        def gate_copy(step):
            rows_ = pl.ds(pl.multiple_of(step * tm, tm), tm)
            return pltpu.make_async_copy(dg_s.at[step % 2], dproj_ref.at[rows_, pl.ds(7 * d, 2 * d)],
                                         dg_sems.at[step % 2])

        @pl.when(i == 0)
        def _():
            loss_ref[...] = jnp.zeros_like(loss_ref)
            dgf_ref[...] = jnp.zeros_like(dgf_ref)

        @pl.when(i >= 2)
        def _():
            gate_copy(i - 2).wait()

        pa = _dot(ya_ref[...], woa_ref[...])
        pb = _dot(yb_ref[...], wob_ref[...])
        sa = _sigmoid(ga_ref[...].astype(F32))
        sb = _sigmoid(gb_ref[...].astype(F32))
        merged = (sa * pa + sb * pb).astype(BF16)
        mrg_ref[...] = merged
        x2 = x_ref[...] + _dot(merged, wout_ref[...])
        r2 = _rms_scale(x2)
        xh = x2 * r2
        gf = gf_ref[...]
        diff = xh * gf - t_ref[...]
        loss_ref[...] += jnp.sum(diff * diff, axis=0, keepdims=True) * (0.5 / d)
        dy = diff * (1.0 / d)
        dgf_ref[...] += jnp.sum(dy * xh, axis=0, keepdims=True)
        dxh = dy * gf
        dx2 = r2 * (dxh - xh * jnp.mean(dxh * xh, axis=-1, keepdims=True))
        dx2_ref[...] = dx2
        dm = _dot_nt(dx2.astype(BF16), wout_ref[...])
        dpa = (dm * sa).astype(BF16)
        dpb = (dm * sb).astype(BF16)
        dpa_ref[...] = dpa
        dpb_ref[...] = dpb
        dg_s[i % 2, :, 0:d] = (dm * pa * (sa * (1.0 - sa))).astype(BF16)
        dg_s[i % 2, :, d:2 * d] = (dm * pb * (sb * (1.0 - sb))).astype(BF16)
        gate_copy(i).start()
        dya_ref[...] = _dot_nt(dpa, woa_ref[...]).astype(BF16)
        dyb_ref[...] = _dot_nt(dpb, wob_ref[...]).astype(BF16)

        @pl.when(i == steps - 1)
        def _():
            if steps >= 2:
                gate_copy(i - 1).wait()
            gate_copy(i).wait()

    rows = lambda k=0: pl.BlockSpec((tm, d), lambda i: (i, k))
    full = pl.BlockSpec((d, d), lambda i: (0, 0))
    vec = pl.BlockSpec((1, d), lambda i: (0, 0))
    return pl.pallas_call(
        body, name="tail", grid=(steps,),
        in_specs=[rows(), rows(), rows(), rows(), rows(7), rows(8), full, full, full, vec],
        out_specs=[pl.BlockSpec(memory_space=pl.ANY),
                   rows(), rows(), rows(), rows(), rows(), rows(), vec, vec],
        out_shape=[SDS((n, e), BF16), SDS((n, d), F32), SDS((n, d), BF16), SDS((n, d), BF16),
                   SDS((n, d), BF16), SDS((n, d), BF16), SDS((n, d), BF16),
                   SDS((1, d), F32), SDS((1, d), F32)],
        scratch_shapes=[pltpu.VMEM((2, tm, 2 * d), BF16), pltpu.SemaphoreType.DMA((2,))],
        compiler_params=_params(("arbitrary",)),
    )(x2d, tgt, ya, yb, proj, proj, w_oa, w_ob, w_out, norm_final)


def _tn_matmul(a, b, name):
    n, p = a.shape
    q = b.shape[1]
    tk = _tile(n, 512)
    nk = n // tk

    def body(a_ref, b_ref, o_ref, acc):
        k = pl.program_id(0)

        @pl.when(k == 0)
        def _():
            acc[...] = jnp.zeros_like(acc)

        acc[...] += _dot_tn(a_ref[...], b_ref[...].astype(BF16))

        @pl.when(k == nk - 1)
        def _():
            o_ref[...] = acc[...].astype(BF16)

    return pl.pallas_call(
        body, name=name, grid=(nk,),
        in_specs=[pl.BlockSpec((tk, p), lambda k: (k, 0)), pl.BlockSpec((tk, q), lambda k: (k, 0))],
        out_specs=pl.BlockSpec((p, q), lambda k: (0, 0)),
        out_shape=SDS((p, q), BF16),
        scratch_shapes=[pltpu.VMEM((p, q), F32)],
        compiler_params=_params(("arbitrary",)),
    )(a, b)


def _sb_bwd(proj, o, dyb, tot, dproj, batch, seq, d, hd):
    heads = d // hd
    t = _tile(seq, SB_TILE_BWD)
    sw = _tile(t, SB_SCAN)
    nb = t // sw
    scale = hd ** -0.5
    nblk = seq // t
    nh = SB_HEADS
    wide = nh * hd
    hs = range(nh)
    cols = [slice(hh * hd, (hh + 1) * hd) for hh in hs]
    blocks = [slice(b * sw, (b + 1) * sw) for b in range(nb)]
    last = slice(sw - 1, sw)

    def compute(qs, ks, v_ref, zb_ref, dyb_ref, tot_ref, kts, vts, dos, res, upto, before, dq):
        for jb in range(nblk):
            rows = slice(jb * t, (jb + 1) * t)
            kts[jb] = ks[rows, :].astype(F32).T.astype(BF16)
            vts[jb] = v_ref[rows, :].astype(F32).T.astype(BF16)
        sz, _ = _silu(zb_ref[...].astype(F32))
        dos[...] = (dyb_ref[...].astype(F32) * sz).astype(BF16)
        res[1] = jnp.zeros((seq, wide), F32)
        res[2] = jnp.zeros((seq, wide), F32)
        row, col = _iotas(t)
        upto[...] = (row[:sw, :sw] <= col[:sw, :sw]).astype(BF16)
        before[...] = (row[:sw, :sw] < col[:sw, :sw]).astype(BF16)

        def qblock(i, carry):
            r0 = pl.multiple_of(i * t, t)

            def tile(j, sums, valid):
                c0 = pl.multiple_of(j * t, t)
                q_i = [qs[pl.ds(r0, t), cs] for cs in cols]
                do_i = [dos[pl.ds(r0, t), cs] for cs in cols]
                logs = [_sb_logs(_dot(q_i[hh], kts[j, cols[hh], :]), scale, valid) for hh in hs]
                dw = [_dot(do_i[hh], vts[j, cols[hh], :]) for hh in hs]
                scans = [_dot(jnp.concatenate([logs[hh][1][:, ks_] for ks_ in blocks], axis=0), upto[...]) for hh in hs]
                ws, gs, new_runs = [], [], []
                for hh in hs:
                    left = tot_ref[hh, pl.ds(r0, t), :] - sums[hh][0]
                    w_b, g_b = [], []
                    for b, ks_ in enumerate(blocks):
                        inside = scans[hh][b * t:(b + 1) * t]
                        w = jnp.exp(logs[hh][0][:, ks_].astype(F32) + (left - inside))
                        w_b.append(w.astype(BF16))
                        g_b.append((dw[hh][:, ks_] * w).astype(BF16))
                        left = left - inside[:, last]
                    ws.append(jnp.concatenate(w_b, axis=1))
                    gs.append(g_b)
                    new_runs.append(tot_ref[hh, pl.ds(r0, t), :] - left)
                gscans = [_dot(jnp.concatenate(gs[hh], axis=0), before[...]) for hh in hs]
                dzs, new_gruns = [], []
                for hh in hs:
                    g_before = sums[hh][1]
                    dz_b = []
                    for b, ks_ in enumerate(blocks):
                        inside = gscans[hh][b * t:(b + 1) * t]
                        beta = jnp.exp(logs[hh][0][:, ks_]).astype(F32)
                        g = gs[hh][b].astype(F32)
                        dz_b.append(((g - (g + inside + g_before) * beta) * scale).astype(BF16))
                        g_before = g_before + inside[:, last] + g[:, last]
                    dzs.append(jnp.concatenate(dz_b, axis=1))
                    new_gruns.append(g_before)
                for hh in hs:
                    res[2, pl.ds(c0, t), cols[hh]] += _dot_tn(ws[hh], do_i[hh])
                for hh in hs:
                    res[1, pl.ds(c0, t), cols[hh]] += _dot_tn(dzs[hh], q_i[hh])
                for hh in hs:
                    dq[:, cols[hh]] += _dot(dzs[hh], ks[pl.ds(c0, t), cols[hh]])
                return tuple((new_runs[hh], new_gruns[hh]) for hh in hs)

            zero = jnp.zeros((t, 1), F32)
            dq[...] = jnp.zeros_like(dq)
            sums = lax.fori_loop(0, i, lambda j, sm: tile(j, sm, None), ((zero, zero),) * nh)
            tile(i, sums, col < row)
            res[0, pl.ds(r0, t), :] = dq[...]
            return carry

        lax.fori_loop(0, nblk, qblock, 0)

    pairs = heads // nh

    def body(qs, ks, v_ref, zb_ref, o_ref, dyb_ref, tot_ref, dproj_in, out_ref,
             kts, vts, dos, res, upto, before, dq, stage, stage_sems):
        del dproj_in
        step = pl.program_id(0) * pairs + pl.program_id(1)

        def out_copies(s):
            rows_ = pl.ds(pl.multiple_of((s // pairs) * seq, seq), seq)
            return [pltpu.make_async_copy(
                stage.at[k], out_ref.at[rows_, pl.ds(pl.multiple_of((3 + k) * d + (s % pairs) * wide, wide), wide)],
                stage_sems.at[k]) for k in range(4)]

        compute(qs, ks, v_ref, zb_ref, dyb_ref, tot_ref, kts, vts, dos, res, upto, before, dq)

        @pl.when(step > 0)
        def _():
            for cp in out_copies(step - 1):
                cp.wait()

        for k in range(3):
            stage[k] = res[k].astype(BF16)
        _, dsz = _silu(zb_ref[...].astype(F32))
        stage[3] = (dyb_ref[...].astype(F32) * o_ref[...].astype(F32) * dsz).astype(BF16)
        for cp in out_copies(step):
            cp.start()

        @pl.when(step == batch * pairs - 1)
        def _():
            for cp in out_copies(step):
                cp.wait()

    col0 = d // wide
    seg = lambda k: pl.BlockSpec((seq, wide), lambda b, h: (b, k * col0 + h))
    head = pl.BlockSpec((seq, wide), lambda b, h: (b, h))
    return pl.pallas_call(
        body, name="sb_bwd", grid=(batch, pairs),
        in_specs=[seg(3), seg(4), seg(5), seg(6), head, head,
                  pl.BlockSpec((nh, seq, 1), lambda b, h: (b * pairs + h, 0, 0)),
                  pl.BlockSpec(memory_space=pl.ANY)],
        out_specs=pl.BlockSpec(memory_space=pl.ANY),
        out_shape=SDS(dproj.shape, dproj.dtype),
        input_output_aliases={7: 0},
        scratch_shapes=[pltpu.VMEM((nblk, wide, t), BF16)] * 2 + [
            pltpu.VMEM((seq, wide), BF16), pltpu.VMEM((3, seq, wide), F32),
            pltpu.VMEM((sw, sw), BF16), pltpu.VMEM((sw, sw), BF16), pltpu.VMEM((t, wide), F32),
            pltpu.VMEM((4, seq, wide), BF16), pltpu.SemaphoreType.DMA((4,))],
        compiler_params=_params(("arbitrary", "arbitrary")),
    )(proj, proj, proj, proj, o, dyb, tot, dproj)


def _branch_a_bwd(proj, dya, norm_v, w_s, b_col, dproj):
    n = proj.shape[0]
    d = norm_v.shape[1]
    groups, chunk, _ = w_s.shape
    tr = _tile(n, 2 * chunk)

    def body(u_ref, v_ref, z_ref, dya_ref, gv_ref, ws_ref, b_ref, dproj_in,
             out_ref, dws_ref, dbias_ref, dgv_ref, vn_s, dmix_s, dvn_s, db_ref):
        del dproj_in

        @pl.when(pl.program_id(0) == 0)
        def _():
            dws_ref[...] = jnp.zeros_like(dws_ref)
            db_ref[...] = jnp.zeros_like(db_ref)
            dgv_ref[...] = jnp.zeros_like(dgv_ref)

        row, col = _iotas(chunk)
        tril = col <= row
        u, v, z, dya_v = (r[...].astype(F32) for r in (u_ref, v_ref, z_ref, dya_ref))
        gv = gv_ref[...]
        vg, dvg_dv = _gelu(v)
        r = _rms_scale(vg)
        vh = vg * r
        vn_s[...] = (vh * gv).astype(BF16)
        ug, dug_du = _gelu(u)
        sz, dsz = _silu(z)
        dmix_s[...] = dya_v * ug * sz
        for g in range(groups):
            wm = jnp.where(tril, ws_ref[g], 0.0).astype(BF16)
            cs = slice(g * chunk, (g + 1) * chunk)
            for c in range(tr // chunk):
                rs = slice(c * chunk, (c + 1) * chunk)
                vn = vn_s[rs, cs]
                mixed = _dot(wm, vn) + b_ref[g]
                dmix = dmix_s[rs, cs]
                dmix16 = dmix.astype(BF16)
                dws_ref[g] += _dot_nt(dmix16, vn)
                db_ref[g] += dmix
                dvn_s[rs, cs] = _dot_tn(wm, dmix16)
                t_u = dya_v[rs, cs] * mixed
                out_ref[rs, g * chunk:(g + 1) * chunk] = (t_u * sz[rs, cs] * dug_du[rs, cs]).astype(BF16)
                out_ref[rs, 2 * d + g * chunk:2 * d + (g + 1) * chunk] = (t_u * ug[rs, cs] * dsz[rs, cs]).astype(BF16)
        dvn = dvn_s[...]
        dgv_ref[...] += jnp.sum(dvn * vh, axis=0, keepdims=True)
        dvh = dvn * gv
        dvg = r * (dvh - vh * jnp.mean(dvh * vh, axis=-1, keepdims=True))
        out_ref[:, d:2 * d] = (dvg * dvg_dv).astype(BF16)

        @pl.when(pl.program_id(0) == n // tr - 1)
        def _():
            for g in range(groups):
                dbias_ref[g:g + 1, :] = jnp.sum(db_ref[g].T, axis=0, keepdims=True)

    seg = lambda k: pl.BlockSpec((tr, d), lambda i: (i, k))
    return pl.pallas_call(
        body, name="branch_a_bwd", grid=(n // tr,),
        in_specs=[seg(0), seg(1), seg(2), seg(0),
                  pl.BlockSpec((1, d), lambda i: (0, 0)),
                  pl.BlockSpec((groups, chunk, chunk), lambda i: (0, 0, 0)),
                  pl.BlockSpec((groups, chunk, 1), lambda i: (0, 0, 0)),
                  pl.BlockSpec(memory_space=pl.ANY)],
        out_specs=[pl.BlockSpec((tr, 3 * d), lambda i: (i, 0)),
                   pl.BlockSpec((groups, chunk, chunk), lambda i: (0, 0, 0)),
                   pl.BlockSpec((groups, chunk), lambda i: (0, 0)),
                   pl.BlockSpec((1, d), lambda i: (0, 0))],
        out_shape=[SDS(dproj.shape, dproj.dtype), SDS((groups, chunk, chunk), F32),
                   SDS((groups, chunk), F32), SDS((1, d), F32)],
        input_output_aliases={7: 0},
        scratch_shapes=[pltpu.VMEM((tr, d), BF16), pltpu.VMEM((tr, d), F32), pltpu.VMEM((tr, d), F32),
                        pltpu.VMEM((groups, chunk, chunk), F32)],
        compiler_params=_params(("arbitrary",)),
    )(proj, proj, proj, dya, norm_v, w_s, b_col, dproj)


def _dx(dproj, wg_in, x2d, dx2, norm_in):
    n, d = x2d.shape
    nsh, _, esh = wg_in.shape
    tm = _tile(n, 1024)

    def body(dp_ref, w_ref, x_ref, dx2_ref, g_ref, gx_ref, dg_ref, acc):
        i, k = pl.program_id(0), pl.program_id(1)

        @pl.when(jnp.logical_and(i == 0, k == 0))
        def _():
            dg_ref[...] = jnp.zeros_like(dg_ref)

        @pl.when(k == 0)
        def _():
            acc[...] = jnp.zeros_like(acc)

        acc[...] += _dot_nt(dp_ref[...], w_ref[0])

        @pl.when(k == nsh - 1)
        def _():
            dh = acc[...]
            x = x_ref[...]
            r = _rms_scale(x)
            xh = x * r
            dg_ref[...] += jnp.sum(dh * xh, axis=0, keepdims=True)
            dxh = dh * g_ref[...]
            gx_ref[...] = dx2_ref[...] + r * (dxh - xh * jnp.mean(dxh * xh, axis=-1, keepdims=True))

    rows = pl.BlockSpec((tm, d), lambda i, k: (i, 0))
    vec = pl.BlockSpec((1, d), lambda i, k: (0, 0))
    return pl.pallas_call(
        body, name="dx", grid=(n // tm, nsh),
        in_specs=[pl.BlockSpec((tm, esh), lambda i, k: (i, k)),
                  pl.BlockSpec((1, d, esh), lambda i, k: (k, 0, 0)), rows, rows, vec],
        out_specs=[rows, vec],
        out_shape=[SDS((n, d), F32), SDS((1, d), F32)],
        scratch_shapes=[pltpu.VMEM((tm, d), F32)],
        compiler_params=_params(("arbitrary", "arbitrary")),
    )(dproj, wg_in, x2d, dx2, norm_in)


def _adamw_outputs(g_ref, d_ref, m_ref, v_ref, g, w, m, v):
    delta, m2, v2 = _adamw(w, g, m, v)
    g_ref[...] = g
    d_ref[...] = delta
    m_ref[...] = m2
    v_ref[...] = v2


def _reduce_adamw(slots, w, m, v, name):
    _, r, c = slots.shape
    tr = _tile(r, 128)

    def body(s_ref, w_ref, m_ref, v_ref, g_out, d_out, m_out, v_out):
        g = s_ref[0].astype(F32)
        for k in range(1, N_DEV):
            g = g + s_ref[k].astype(F32)
        _adamw_outputs(g_out, d_out, m_out, v_out, g, w_ref[...], m_ref[...], v_ref[...])

    blk = pl.BlockSpec((tr, c), lambda i: (i, 0))
    return pl.pallas_call(
        body, name=name, grid=(r // tr,),
        in_specs=[pl.BlockSpec((N_DEV, tr, c), lambda i: (0, i, 0)), blk, blk, blk],
        out_specs=[blk] * 4,
        out_shape=[SDS((r, c), F32)] * 4,
        compiler_params=_params(("parallel",)),
    )(slots, w, m, v)


def _adamw_small(g, w, m, v, name):
    def body(g_ref, w_ref, m_ref, v_ref, g_out, d_out, m_out, v_out):
        _adamw_outputs(g_out, d_out, m_out, v_out, g_ref[...], w_ref[...], m_ref[...], v_ref[...])

    return pl.pallas_call(
        body, name=name,
        out_shape=[SDS(g.shape, F32)] * 4,
        in_specs=[pl.BlockSpec(memory_space=pltpu.VMEM)] * 4,
        out_specs=[pl.BlockSpec(memory_space=pltpu.VMEM)] * 4,
    )(g, w, m, v)


def kernel(x, norm_in, w_in, norm_v, w_s, b_s, w_o_gmlp, w_o_sb, w_out, norm_final, loss_target, m_norm_in, m_w_in, m_norm_v, m_w_s, m_b_s, m_w_o_gmlp, m_w_o_sb, m_w_out, m_norm_final, v_norm_in, v_w_in, v_norm_v, v_w_s, v_b_s, v_w_o_gmlp, v_w_o_sb, v_w_out, v_norm_final):
    batch, seq, d = x.shape
    n = batch * seq
    groups, chunk = w_s.shape[1], w_s.shape[2]
    hd = LANE
    x2d = x.reshape(n, d)
    tgt = loss_target.reshape(n, d)
    b_col = b_s[0].reshape(groups, chunk, 1)
    norm_final2 = norm_final.reshape(1, d)

    wg_in, wg_oa, wg_ob, wg_out = _gather_weights([w_in[0], w_o_gmlp[0], w_o_sb[0], w_out[0]])
    rsh = wg_oa.shape[1]
    wf_oa, wf_ob, wf_out = (w.reshape(N_DEV * rsh, d) for w in (wg_oa, wg_ob, wg_out))

    proj, h = _in_proj(x2d, norm_in, wg_in)
    ya = _branch_a_fwd(proj, norm_v, w_s[0], b_col)
    yb, o, sb_tot = _sb_fwd(proj, batch, seq, d, hd)
    dproj, dx2, dya, dyb, merged, dpa, dpb, loss_vec, dgf = _tail(
        x2d, tgt, ya, yb, proj, wf_oa, wf_ob, wf_out, norm_final2)
    gp_oa = _tn_matmul(ya, dpa, "dw_o_gmlp")
    gp_ob = _tn_matmul(yb, dpb, "dw_o_sb")
    gp_out = _tn_matmul(merged, dx2, "dw_out")
    dproj = _sb_bwd(proj, o, dyb, sb_tot, dproj, batch, seq, d, hd)
    dproj, gp_ws, gp_b, gp_nv = _branch_a_bwd(proj, dya, norm_v, w_s[0], b_col, dproj)
    grad_x, gp_nin = _dx(dproj, wg_in, x2d, dx2, norm_in)

    slab = lambda a: a.reshape(d // LANE, LANE)
    gc = groups * chunk
    packed = jnp.concatenate(
        [gp_ws.reshape(gc, chunk), gp_b, slab(gp_nin), slab(gp_nv), slab(dgf), slab(loss_vec)], axis=0)
    my_slot = _slot(_me()).astype(jnp.int32).reshape(1)
    s_win, s_oa, s_ob, s_out, packs = _dw_in_exchange(
        h, dproj, my_slot, [g.reshape(N_DEV, rsh, d) for g in (gp_oa, gp_ob, gp_out)], packed)
    tot, loss_slab = _finish_small(packs, groups, chunk)
    ns = d // LANE
    g_ws = tot[:gc]
    g_b = tot[gc:gc + groups]
    g_nin, g_nv, g_nf = (tot[gc + groups + k * ns:gc + groups + (k + 1) * ns] for k in range(3))
    loss = loss_slab[0, 0]

    res = {}
    res["w_in"] = _reduce_adamw(s_win, w_in[0], m_w_in[0], v_w_in[0], "adamw_w_in")
    res["w_o_gmlp"] = _reduce_adamw(s_oa, w_o_gmlp[0], m_w_o_gmlp[0], v_w_o_gmlp[0], "adamw_w_o_gmlp")
    res["w_o_sb"] = _reduce_adamw(s_ob, w_o_sb[0], m_w_o_sb[0], v_w_o_sb[0], "adamw_w_o_sb")
    res["w_out"] = _reduce_adamw(s_out, w_out[0], m_w_out[0], v_w_out[0], "adamw_w_out")
    res["norm_in"] = _adamw_small(g_nin, slab(norm_in), slab(m_norm_in), slab(v_norm_in), "adamw_norm_in")
    res["norm_v"] = _adamw_small(g_nv, slab(norm_v), slab(m_norm_v), slab(v_norm_v), "adamw_norm_v")
    res["norm_final"] = _adamw_small(g_nf, slab(norm_final), slab(m_norm_final), slab(v_norm_final), "adamw_norm_final")
    res["w_s"] = _adamw_small(g_ws, w_s.reshape(gc, chunk), m_w_s.reshape(gc, chunk), v_w_s.reshape(gc, chunk), "adamw_w_s")
    res["b_s"] = _adamw_small(g_b, b_s[0], m_b_s[0], v_b_s[0], "adamw_b_s")

    shapes = {"norm_in": norm_in.shape, "w_in": w_in.shape, "norm_v": norm_v.shape, "w_s": w_s.shape,
              "b_s": b_s.shape, "w_o_gmlp": w_o_gmlp.shape, "w_o_sb": w_o_sb.shape, "w_out": w_out.shape,
              "norm_final": norm_final.shape}
    names = list(shapes)
    outs = [loss, grad_x.reshape(batch, seq, d)]
    for kind in range(4):
        outs += [res[name][kind].reshape(shapes[name]) for name in names]
    return tuple(outs)
```

```python
import functools
import math

import jax
import jax.numpy as jnp
from jax import lax
from jax.experimental import pallas as pl
from jax.experimental.pallas import tpu as pltpu

F32 = jnp.float32
BF16 = jnp.bfloat16
SDS = jax.ShapeDtypeStruct
MESH_ID = pl.DeviceIdType.MESH

N_DEV = 8
LANE = 128
SUBLANE = 8
VMEM_LIMIT = 56 * 1024 * 1024
SB_TILE = 512
SB_TILE_BWD = 512
SB_SCAN = 256
SB_HEADS = 2
MASKED_LOG = -1e30
RMS_EPS = 1e-6

ADAM_LR = 0.001
ADAM_B1 = 0.9
ADAM_B2 = 0.999
ADAM_EPS = 1e-08
ADAM_WD = 0.01
ADAM_STEP = 10

NT_DIMS = (((1,), (1,)), ((), ()))
TN_DIMS = (((0,), (0,)), ((), ()))


def _params(semantics=None):
    return pltpu.CompilerParams(dimension_semantics=semantics, vmem_limit_bytes=VMEM_LIMIT)


def _tile(n, preferred):
    t = min(n, preferred)
    assert n % t == 0, (n, t)
    return t


def _sigmoid(x):
    return 1.0 / (1.0 + jnp.exp(-x))


def _silu(x):
    s = _sigmoid(x)
    return x * s, s * (1.0 + x * (1.0 - s))


def _gelu(x):
    k = math.sqrt(2.0 / math.pi)
    x2 = x * x
    t = jnp.tanh(k * (x + 0.044715 * (x * x2)))
    cdf = 0.5 * (1.0 + t)
    return x * cdf, cdf + 0.5 * x * (1.0 - t * t) * (k * (1.0 + 3.0 * 0.044715 * x2))


def _rms_scale(x):
    return lax.rsqrt(jnp.mean(x * x, axis=-1, keepdims=True) + RMS_EPS)


def _iotas(n):
    return (lax.broadcasted_iota(jnp.int32, (n, n), 0), lax.broadcasted_iota(jnp.int32, (n, n), 1))


def _adamw(w, g, m, v):
    m = ADAM_B1 * m + (1.0 - ADAM_B1) * g
    v = ADAM_B2 * v + (1.0 - ADAM_B2) * (g * g)
    m_hat = m / (1.0 - ADAM_B1 ** ADAM_STEP)
    v_hat = v / (1.0 - ADAM_B2 ** ADAM_STEP)
    delta = -ADAM_LR * (m_hat / (jnp.sqrt(v_hat) + ADAM_EPS) + ADAM_WD * w)
    return delta, m, v


def _dot(a, b):
    return jnp.dot(a, b, preferred_element_type=F32)


def _dot_nt(a, b):
    return lax.dot_general(a, b, NT_DIMS, preferred_element_type=F32)


def _dot_tn(a, b):
    return lax.dot_general(a, b, TN_DIMS, preferred_element_type=F32)


def _sb_logs(raw, scale, valid):
    z = (raw * scale).astype(BF16)
    log_beta = jnp.minimum(z, 0) - jnp.log(1 + jnp.exp(-jnp.abs(z)))
    log_rest = log_beta - z
    if valid is not None:
        log_beta = jnp.where(valid, log_beta, MASKED_LOG)
        log_rest = jnp.where(valid, log_rest, 0)
    return log_beta, log_rest


def _me():
    return lax.axis_index("x"), lax.axis_index("y"), lax.axis_index("c")


def _slot(p):
    return 4 * p[0] + 2 * p[1] + p[2]


def _peer(me, k):
    flips = ((k >> 2) & 1, (k >> 1) & 1, k & 1)
    return tuple(1 - a if f else a for a, f in zip(me, flips))


def _stack_exchange(me, st_in, st_out, send_sems, recv_sems, local_sems, arrivals=True):
    mine = _slot(me)
    ns = len(st_in)
    local = [pltpu.make_async_copy(st_in[a].at[mine], st_out[a].at[mine], local_sems.at[a]) for a in range(ns)]
    remote, landed = [], []
    for k in range(1, N_DEV):
        peer = _peer(me, k)
        for a in range(ns):
            sems = dict(send_sem=send_sems.at[7 * a + k - 1], recv_sem=recv_sems.at[7 * a + k - 1])
            remote.append(pltpu.make_async_remote_copy(
                src_ref=st_in[a].at[_slot(peer)], dst_ref=st_out[a].at[mine],
                device_id=peer, device_id_type=MESH_ID, **sems))
            if arrivals:
                got = st_out[a].at[_slot(peer)]
                landed.append(pltpu.make_async_remote_copy(
                    src_ref=got, dst_ref=got, device_id=me, device_id_type=MESH_ID, **sems))
    return local, remote, landed


def _gather_weights(shards):
    n = len(shards)

    def body(*refs):
        ins, outs, stage = refs[:n], refs[n:2 * n], refs[2 * n:3 * n]
        send_sems, recv_sems, local_sems = refs[3 * n:]
        x, y, c = _me()
        me, sibling = (x, y, c), (x, y, 1 - c)
        chips = [(1 - x, y), (x, 1 - y), (1 - x, 1 - y)]

        def copy(a, k, block, to, src=None):
            dst = outs[a].at[_slot(block)]
            return pltpu.make_async_remote_copy(
                src_ref=dst if src is None else src, dst_ref=dst,
                send_sem=send_sems.at[7 * a + k], recv_sem=recv_sems.at[7 * a + k],
                device_id=to, device_id_type=MESH_ID)

        started = []
        for a in range(n):
            stage[a][...] = ins[a][...].astype(BF16)
            mine = pltpu.make_async_copy(stage[a], outs[a].at[_slot(me)], local_sems.at[a])
            mine.start()
            started.append(mine)
        sends = []
        for a in range(n):
            sends.append(copy(a, 0, me, sibling, src=stage[a]))
            sends += [copy(a, 1 + j, me, (*chip, c), src=stage[a]) for j, chip in enumerate(chips)]
        for cp in sends:
            cp.start()
        for a in range(n):
            for j, chip in enumerate(chips):
                copy(a, 1 + j, (*chip, c), me).wait_recv()
                passed = copy(a, 4 + j, (*chip, c), sibling)
                passed.start()
                sends.append(passed)
        for a in range(n):
            copy(a, 0, sibling, me).wait_recv()
            for j, chip in enumerate(chips):
                copy(a, 4 + j, (*chip, 1 - c), me).wait_recv()
        for cp in sends:
            cp.wait_send()
        for mine in started:
            mine.wait()

    return pl.pallas_call(
        body, name="gather_weights",
        out_shape=[SDS((N_DEV,) + s.shape, BF16) for s in shards],
        in_specs=[pl.BlockSpec(memory_space=pltpu.VMEM)] * n,
        out_specs=[pl.BlockSpec(memory_space=pl.ANY)] * n,
        scratch_shapes=[pltpu.VMEM(s.shape, BF16) for s in shards] + [
            pltpu.SemaphoreType.DMA((7 * n,)), pltpu.SemaphoreType.DMA((7 * n,)),
            pltpu.SemaphoreType.DMA((n,))],
        compiler_params=pltpu.CompilerParams(vmem_limit_bytes=VMEM_LIMIT),
    )(*shards)


def _dw_in_exchange(h, dproj, my_slot, packed):
    n, d = h.shape
    esh = dproj.shape[1] // N_DEV
    tk = _tile(n, 512)
    nk = n // tk
    last_j = N_DEV - 1

    def body(me_ref, h_ref, dp_ref, pk_in, win_out, pk_out,
             acc, sendbuf, win_send, win_recv, send_sems, recv_sems, local_sems):
        del me_ref
        j, k = pl.program_id(0), pl.program_id(1)
        me = _me()
        mine = _slot(me)

        def pack_copies():
            local = pltpu.make_async_copy(pk_in, pk_out.at[mine], local_sems.at[0])
            remote = [pltpu.make_async_remote_copy(
                src_ref=pk_in, dst_ref=pk_out.at[mine], send_sem=send_sems.at[kk - 1], recv_sem=recv_sems.at[kk - 1],
                device_id=_peer(me, kk), device_id_type=MESH_ID) for kk in range(1, N_DEV)]
            return local, remote

        def shard_copy(jj):
            owner = (mine + 1 + jj) % N_DEV
            return pltpu.make_async_remote_copy(
                src_ref=sendbuf.at[jj % 2], dst_ref=win_out.at[mine],
                send_sem=win_send.at[jj % 2], recv_sem=win_recv.at[mine],
                device_id=(owner // 4, (owner // 2) % 2, owner % 2), device_id_type=MESH_ID)

        def own_copy():
            return pltpu.make_async_copy(sendbuf.at[last_j % 2], win_out.at[mine], local_sems.at[1])

        @pl.when(jnp.logical_and(j == 0, k == 0))
        def _():
            local, remote = pack_copies()
            for cp in [local] + remote:
                cp.start()

        @pl.when(k == 0)
        def _():
            acc[...] = jnp.zeros_like(acc)

        acc[...] += _dot_tn(h_ref[...], dp_ref[...])

        @pl.when(k == nk - 1)
        def _():
            @pl.when(j >= 2)
            def _():
                shard_copy(j - 2).wait_send()

            sendbuf[j % 2] = acc[...].astype(BF16)

            @pl.when(j < last_j)
            def _():
                shard_copy(j).start()

            @pl.when(j == last_j)
            def _():
                own_copy().start()
                shard_copy(last_j - 1).wait_send()
                own_copy().wait()
                for src in range(N_DEV):
                    @pl.when(src != mine)
                    def _():
                        landed = win_out.at[src]
                        pltpu.make_async_remote_copy(
                            src_ref=landed, dst_ref=landed, send_sem=win_send.at[0], recv_sem=win_recv.at[src],
                            device_id=me, device_id_type=MESH_ID).wait_recv()
                local, remote = pack_copies()
                for cp in remote:
                    cp.wait_send()
                for kk in range(1, N_DEV):
                    landed = pk_out.at[_slot(_peer(me, kk))]
                    pltpu.make_async_remote_copy(
                        src_ref=landed, dst_ref=landed, send_sem=send_sems.at[kk - 1], recv_sem=recv_sems.at[kk - 1],
                        device_id=me, device_id_type=MESH_ID).wait_recv()
                local.wait()

    any_spec = pl.BlockSpec(memory_space=pl.ANY)
    grid_spec = pltpu.PrefetchScalarGridSpec(
        num_scalar_prefetch=1, grid=(N_DEV, nk),
        in_specs=[pl.BlockSpec((tk, d), lambda j, k, me: (k, 0)),
                  pl.BlockSpec((tk, esh), lambda j, k, me: (k, (me[0] + 1 + j) % N_DEV)), any_spec],
        out_specs=[any_spec] * 2,
        scratch_shapes=[pltpu.VMEM((d, esh), F32), pltpu.VMEM((2, d, esh), BF16),
                        pltpu.SemaphoreType.DMA((2,)), pltpu.SemaphoreType.DMA((N_DEV,)),
                        pltpu.SemaphoreType.DMA((N_DEV - 1,)), pltpu.SemaphoreType.DMA((N_DEV - 1,)),
                        pltpu.SemaphoreType.DMA((2,))])
    return pl.pallas_call(
        body, name="dw_in_exchange", grid_spec=grid_spec,
        out_shape=[SDS((N_DEV, d, esh), BF16), SDS((N_DEV,) + packed.shape, packed.dtype)],
        compiler_params=_params(("arbitrary", "arbitrary")),
    )(my_slot, h, dproj, packed)


def _finish_small(packs, groups, chunk):
    rows = packs.shape[1]
    gc = groups * chunk

    def body(p_ref, sum_ref, loss_ref):
        row, col = _iotas(chunk)
        tril = col <= row
        for g in range(groups):
            rs = slice(g * chunk, (g + 1) * chunk)
            tot = p_ref[0, rs, :]
            for dev in range(1, N_DEV):
                tot = tot + p_ref[dev, rs, :]
            sum_ref[rs, :] = jnp.where(tril, tot, 0.0)
        rs = slice(gc, rows)
        tot = p_ref[0, rs, :]
        for dev in range(1, N_DEV):
            tot = tot + p_ref[dev, rs, :]
        sum_ref[rs, :] = tot
        loss_ref[...] = jnp.full((SUBLANE, LANE), jnp.sum(tot[rows - gc - SUBLANE:, :]), F32)

    return pl.pallas_call(
        body, name="finish_small",
        out_shape=[SDS((rows, LANE), F32), SDS((SUBLANE, LANE), F32)],
        in_specs=[pl.BlockSpec(memory_space=pltpu.VMEM)],
        out_specs=[pl.BlockSpec(memory_space=pltpu.VMEM)] * 2,
        compiler_params=pltpu.CompilerParams(vmem_limit_bytes=VMEM_LIMIT),
    )(packs)


def _in_proj(x2d, norm_in, wg_in):
    n, d = x2d.shape
    nsh, _, esh = wg_in.shape
    tm = _tile(n, 1024)

    def body(x_ref, g_ref, w_ref, proj_ref, h_ref):
        @pl.when(pl.program_id(1) == 0)
        def _():
            x = x_ref[...]
            h_ref[...] = (x * _rms_scale(x) * g_ref[...]).astype(BF16)

        proj_ref[...] = _dot(h_ref[...], w_ref[0]).astype(BF16)

    return pl.pallas_call(
        body, name="in_proj", grid=(n // tm, nsh),
        in_specs=[pl.BlockSpec((tm, d), lambda i, j: (i, 0)),
                  pl.BlockSpec((1, d), lambda i, j: (0, 0)),
                  pl.BlockSpec((1, d, esh), lambda i, j: (j, 0, 0))],
        out_specs=[pl.BlockSpec((tm, esh), lambda i, j: (i, j)),
                   pl.BlockSpec((tm, d), lambda i, j: (i, 0))],
        out_shape=[SDS((n, nsh * esh), BF16), SDS((n, d), BF16)],
        compiler_params=_params(("parallel", "arbitrary")),
    )(x2d, norm_in, wg_in)


def _branch_a_fwd(proj, norm_v, w_s, b_col):
    n = proj.shape[0]
    d = norm_v.shape[1]
    groups, chunk, _ = w_s.shape
    tr = _tile(n, 4 * chunk)

    def body(u_ref, v_ref, z_ref, gv_ref, ws_ref, b_ref, ya_ref, vn_s, pre_s):
        row, col = _iotas(chunk)
        tril = col <= row
        vg, _ = _gelu(v_ref[...].astype(F32))
        vn_s[...] = (vg * _rms_scale(vg) * gv_ref[...]).astype(BF16)
        ug, _ = _gelu(u_ref[...].astype(F32))
        sz, _ = _silu(z_ref[...].astype(F32))
        pre_s[...] = ug * sz
        for g in range(groups):
            wm = jnp.where(tril, ws_ref[g], 0.0).astype(BF16)
            cs = slice(g * chunk, (g + 1) * chunk)
            for c in range(tr // chunk):
                rs = slice(c * chunk, (c + 1) * chunk)
                mixed = _dot(wm, vn_s[rs, cs]) + b_ref[g]
                ya_ref[rs, cs] = (pre_s[rs, cs] * mixed).astype(BF16)

    seg = lambda k: pl.BlockSpec((tr, d), lambda i: (i, k))
    return pl.pallas_call(
        body, name="branch_a_fwd", grid=(n // tr,),
        in_specs=[seg(0), seg(1), seg(2),
                  pl.BlockSpec((1, d), lambda i: (0, 0)),
                  pl.BlockSpec((groups, chunk, chunk), lambda i: (0, 0, 0)),
                  pl.BlockSpec((groups, chunk, 1), lambda i: (0, 0, 0))],
        out_specs=pl.BlockSpec((tr, d), lambda i: (i, 0)),
        out_shape=SDS((n, d), BF16),
        scratch_shapes=[pltpu.VMEM((tr, d), BF16), pltpu.VMEM((tr, d), F32)],
        compiler_params=_params(("parallel",)),
    )(proj, proj, proj, norm_v, w_s, b_col)


def _sb_fwd(proj, batch, seq, d, hd):
    heads = d // hd
    t = _tile(seq, SB_TILE)
    sw = _tile(t, SB_SCAN)
    nb = t // sw
    scale = hd ** -0.5
    nblk = seq // t
    nh = SB_HEADS
    wide = nh * hd
    cols = [slice(hh * hd, (hh + 1) * hd) for hh in range(nh)]

    def body(qs, k_ref, vs, zb_ref, yb_ref, o_ref, tot_ref, kts, later, acc):
        for jb in range(nblk):
            kts[jb] = k_ref[jb * t:(jb + 1) * t, :].astype(F32).T.astype(BF16)
        row, col = _iotas(t)
        later[...] = (row[:sw, :sw] > col[:sw, :sw]).astype(BF16)

        def qblock(i, carry):
            r0 = pl.multiple_of(i * t, t)

            def tile(j, runs, valid):
                c0 = pl.multiple_of(j * t, t)
                logs = [_sb_logs(_dot(qs[pl.ds(r0, t), cs], kts[j, cs, :]), scale, valid) for cs in cols]
                scans = [_dot(jnp.concatenate([logs[hh][1][:, b * sw:(b + 1) * sw] for b in range(nb)], axis=0),
                              later[...]) for hh in range(nh)]
                new_runs = []
                for hh in range(nh):
                    after = runs[hh]
                    blocks = [None] * nb
                    for b in reversed(range(nb)):
                        ks_ = slice(b * sw, (b + 1) * sw)
                        inside = scans[hh][b * t:(b + 1) * t]
                        blocks[b] = jnp.exp(logs[hh][0][:, ks_].astype(F32) + inside + after).astype(BF16)
                        after = after + inside[:, 0:1] + logs[hh][1][:, b * sw:b * sw + 1].astype(F32)
                    new_runs.append(after)
                    pv = _dot(jnp.concatenate(blocks, axis=1), vs[pl.ds(c0, t), cols[hh]])
                    if valid is None:
                        acc[:, cols[hh]] += pv
                    else:
                        acc[:, cols[hh]] = pv
                return tuple(new_runs)

            runs = tile(i, (jnp.zeros((t, 1), F32),) * nh, col < row)
            runs = lax.fori_loop(0, i, lambda jj, rs: tile(i - 1 - jj, rs, None), runs)
            for hh in range(nh):
                out = acc[:, cols[hh]]
                o_ref[pl.ds(r0, t), cols[hh]] = out.astype(BF16)
                tot_ref[hh, pl.ds(r0, t), :] = runs[hh]
                sz, _ = _silu(zb_ref[pl.ds(r0, t), cols[hh]].astype(F32))
                yb_ref[pl.ds(r0, t), cols[hh]] = (out * sz).astype(BF16)
            return carry

        lax.fori_loop(0, nblk, qblock, 0)

    col0 = d // wide
    seg = lambda k: pl.BlockSpec((seq, wide), lambda b, h: (b, k * col0 + h))
    return pl.pallas_call(
        body, name="sb_fwd", grid=(batch, heads // nh),
        in_specs=[seg(3), seg(4), seg(5), seg(6)],
        out_specs=[pl.BlockSpec((seq, wide), lambda b, h: (b, h))] * 2 + [
            pl.BlockSpec((nh, seq, 1), lambda b, h: (b * (heads // nh) + h, 0, 0))],
        out_shape=[SDS((batch * seq, d), BF16), SDS((batch * seq, d), BF16), SDS((batch * heads, seq, 1), F32)],
        scratch_shapes=[pltpu.VMEM((nblk, wide, t), BF16), pltpu.VMEM((sw, sw), BF16), pltpu.VMEM((t, wide), F32)],
        compiler_params=_params(("parallel", "parallel")),
    )(proj, proj, proj, proj)


def _tail(x2d, tgt, ya, yb, proj, w_oa, w_ob, w_out, norm_final):
    n, d = x2d.shape
    e = proj.shape[1]
    tm = _tile(n, 256)

    steps = n // tm

    def body(x_ref, t_ref, ya_ref, yb_ref, ga_ref, gb_ref, woa_ref, wob_ref, wout_ref, gf_ref,
             dproj_ref, dx2_ref, dya_ref, dyb_ref, mrg_ref, dpa_ref, dpb_ref, loss_ref, dgf_ref, dg_s, dg_sems):
        i = pl.program_id(0)

        def gate_copy(step):
            rows_ = pl.ds(pl.multiple_of(step * tm, tm), tm)
            return pltpu.make_async_copy(dg_s.at[step % 2], dproj_ref.at[rows_, pl.ds(7 * d, 2 * d)],
                                         dg_sems.at[step % 2])

        @pl.when(i == 0)
        def _():
            loss_ref[...] = jnp.zeros_like(loss_ref)
            dgf_ref[...] = jnp.zeros_like(dgf_ref)

        @pl.when(i >= 2)
        def _():
            gate_copy(i - 2).wait()

        pa = _dot(ya_ref[...], woa_ref[...])
        pb = _dot(yb_ref[...], wob_ref[...])
        sa = _sigmoid(ga_ref[...].astype(F32))
        sb = _sigmoid(gb_ref[...].astype(F32))
        merged = (sa * pa + sb * pb).astype(BF16)
        mrg_ref[...] = merged
        x2 = x_ref[...] + _dot(merged, wout_ref[...])
        r2 = _rms_scale(x2)
        xh = x2 * r2
        gf = gf_ref[...]
        diff = xh * gf - t_ref[...]
        loss_ref[...] += jnp.sum(diff * diff, axis=0, keepdims=True) * (0.5 / d)
        dy = diff * (1.0 / d)
        dgf_ref[...] += jnp.sum(dy * xh, axis=0, keepdims=True)
        dxh = dy * gf
        dx2 = r2 * (dxh - xh * jnp.mean(dxh * xh, axis=-1, keepdims=True))
        dx2_ref[...] = dx2
        dm = _dot_nt(dx2.astype(BF16), wout_ref[...])
        dpa = (dm * sa).astype(BF16)
        dpb = (dm * sb).astype(BF16)
        dpa_ref[...] = dpa
        dpb_ref[...] = dpb
        dg_s[i % 2, :, 0:d] = (dm * pa * (sa * (1.0 - sa))).astype(BF16)
        dg_s[i % 2, :, d:2 * d] = (dm * pb * (sb * (1.0 - sb))).astype(BF16)
        gate_copy(i).start()
        dya_ref[...] = _dot_nt(dpa, woa_ref[...]).astype(BF16)
        dyb_ref[...] = _dot_nt(dpb, wob_ref[...]).astype(BF16)

        @pl.when(i == steps - 1)
        def _():
            if steps >= 2:
                gate_copy(i - 1).wait()
            gate_copy(i).wait()

    rows = lambda k=0: pl.BlockSpec((tm, d), lambda i: (i, k))
    full = pl.BlockSpec((d, d), lambda i: (0, 0))
    vec = pl.BlockSpec((1, d), lambda i: (0, 0))
    return pl.pallas_call(
        body, name="tail", grid=(steps,),
        in_specs=[rows(), rows(), rows(), rows(), rows(7), rows(8), full, full, full, vec],
        out_specs=[pl.BlockSpec(memory_space=pl.ANY),
                   rows(), rows(), rows(), rows(), rows(), rows(), vec, vec],
        out_shape=[SDS((n, e), BF16), SDS((n, d), F32), SDS((n, d), BF16), SDS((n, d), BF16),
                   SDS((n, d), BF16), SDS((n, d), BF16), SDS((n, d), BF16),
                   SDS((1, d), F32), SDS((1, d), F32)],
        scratch_shapes=[pltpu.VMEM((2, tm, 2 * d), BF16), pltpu.SemaphoreType.DMA((2,))],
        compiler_params=_params(("arbitrary",)),
    )(x2d, tgt, ya, yb, proj, proj, w_oa, w_ob, w_out, norm_final)


def _tn_matmul(a, b, name):
    n, p = a.shape
    q = b.shape[1]
    tk = _tile(n, 512)
    nk = n // tk

    def body(a_ref, b_ref, o_ref, acc):
        k = pl.program_id(0)

        @pl.when(k == 0)
        def _():
            acc[...] = jnp.zeros_like(acc)

        acc[...] += _dot_tn(a_ref[...], b_ref[...].astype(BF16))

        @pl.when(k == nk - 1)
        def _():
            o_ref[...] = acc[...].astype(BF16)

    return pl.pallas_call(
        body, name=name, grid=(nk,),
        in_specs=[pl.BlockSpec((tk, p), lambda k: (k, 0)), pl.BlockSpec((tk, q), lambda k: (k, 0))],
        out_specs=pl.BlockSpec((p, q), lambda k: (0, 0)),
        out_shape=SDS((p, q), BF16),
        scratch_shapes=[pltpu.VMEM((p, q), F32)],
        compiler_params=_params(("arbitrary",)),
    )(a, b)


def _sb_bwd(proj, o, dyb, tot, dproj, stacks, batch, seq, d, hd):
    heads = d // hd
    t = _tile(seq, SB_TILE_BWD)
    sw = _tile(t, SB_SCAN)
    nb = t // sw
    scale = hd ** -0.5
    nblk = seq // t
    nh = SB_HEADS
    wide = nh * hd
    hs = range(nh)
    cols = [slice(hh * hd, (hh + 1) * hd) for hh in hs]
    blocks = [slice(b * sw, (b + 1) * sw) for b in range(nb)]
    last = slice(sw - 1, sw)

    def compute(qs, ks, v_ref, zb_ref, dyb_ref, tot_ref, kts, vts, dos, res, upto, before, dq):
        for jb in range(nblk):
            rows = slice(jb * t, (jb + 1) * t)
            kts[jb] = ks[rows, :].astype(F32).T.astype(BF16)
            vts[jb] = v_ref[rows, :].astype(F32).T.astype(BF16)
        sz, _ = _silu(zb_ref[...].astype(F32))
        dos[...] = (dyb_ref[...].astype(F32) * sz).astype(BF16)
        res[1] = jnp.zeros((seq, wide), F32)
        res[2] = jnp.zeros((seq, wide), F32)
        row, col = _iotas(t)
        upto[...] = (row[:sw, :sw] <= col[:sw, :sw]).astype(BF16)
        before[...] = (row[:sw, :sw] < col[:sw, :sw]).astype(BF16)

        def qblock(i, carry):
            r0 = pl.multiple_of(i * t, t)

            def tile(j, sums, valid):
                c0 = pl.multiple_of(j * t, t)
                q_i = [qs[pl.ds(r0, t), cs] for cs in cols]
                do_i = [dos[pl.ds(r0, t), cs] for cs in cols]
                logs = [_sb_logs(_dot(q_i[hh], kts[j, cols[hh], :]), scale, valid) for hh in hs]
                dw = [_dot(do_i[hh], vts[j, cols[hh], :]) for hh in hs]
                scans = [_dot(jnp.concatenate([logs[hh][1][:, ks_] for ks_ in blocks], axis=0), upto[...]) for hh in hs]
                ws, gs, new_runs = [], [], []
                for hh in hs:
                    left = tot_ref[hh, pl.ds(r0, t), :] - sums[hh][0]
                    w_b, g_b = [], []
                    for b, ks_ in enumerate(blocks):
                        inside = scans[hh][b * t:(b + 1) * t]
                        w = jnp.exp(logs[hh][0][:, ks_].astype(F32) + (left - inside))
                        w_b.append(w.astype(BF16))
                        g_b.append((dw[hh][:, ks_] * w).astype(BF16))
                        left = left - inside[:, last]
                    ws.append(jnp.concatenate(w_b, axis=1))
                    gs.append(g_b)
                    new_runs.append(tot_ref[hh, pl.ds(r0, t), :] - left)
                gscans = [_dot(jnp.concatenate(gs[hh], axis=0), before[...]) for hh in hs]
                dzs, new_gruns = [], []
                for hh in hs:
                    g_before = sums[hh][1]
                    dz_b = []
                    for b, ks_ in enumerate(blocks):
                        inside = gscans[hh][b * t:(b + 1) * t]
                        beta = jnp.exp(logs[hh][0][:, ks_]).astype(F32)
                        g = gs[hh][b].astype(F32)
                        dz_b.append(((g - (g + inside + g_before) * beta) * scale).astype(BF16))
                        g_before = g_before + inside[:, last] + g[:, last]
                    dzs.append(jnp.concatenate(dz_b, axis=1))
                    new_gruns.append(g_before)
                for hh in hs:
                    res[2, pl.ds(c0, t), cols[hh]] += _dot_tn(ws[hh], do_i[hh])
                for hh in hs:
                    res[1, pl.ds(c0, t), cols[hh]] += _dot_tn(dzs[hh], q_i[hh])
                for hh in hs:
                    dq[:, cols[hh]] += _dot(dzs[hh], ks[pl.ds(c0, t), cols[hh]])
                return tuple((new_runs[hh], new_gruns[hh]) for hh in hs)

            zero = jnp.zeros((t, 1), F32)
            dq[...] = jnp.zeros_like(dq)
            sums = lax.fori_loop(0, i, lambda j, sm: tile(j, sm, None), ((zero, zero),) * nh)
            tile(i, sums, col < row)
            res[0, pl.ds(r0, t), :] = dq[...]
            return carry

        lax.fori_loop(0, nblk, qblock, 0)

    pairs = heads // nh

    ns = len(stacks)

    def body(qs, ks, v_ref, zb_ref, o_ref, dyb_ref, tot_ref, dproj_in, *refs):
        del dproj_in
        st_in, out_ref, st_out = refs[:ns], refs[ns], refs[ns + 1:2 * ns + 1]
        (kts, vts, dos, res, upto, before, dq, stage, stage_sems,
         send_sems, recv_sems, local_sems) = refs[2 * ns + 1:]
        step = pl.program_id(0) * pairs + pl.program_id(1)
        exchange = functools.partial(_stack_exchange, _me(), st_in, st_out, send_sems, recv_sems, local_sems)

        @pl.when(step == 0)
        def _():
            local, remote, _ = exchange(arrivals=False)
            for cp in local + remote:
                cp.start()

        def out_copies(s):
            rows_ = pl.ds(pl.multiple_of((s // pairs) * seq, seq), seq)
            return [pltpu.make_async_copy(
                stage.at[k], out_ref.at[rows_, pl.ds(pl.multiple_of((3 + k) * d + (s % pairs) * wide, wide), wide)],
                stage_sems.at[k]) for k in range(4)]

        compute(qs, ks, v_ref, zb_ref, dyb_ref, tot_ref, kts, vts, dos, res, upto, before, dq)

        @pl.when(step > 0)
        def _():
            for cp in out_copies(step - 1):
                cp.wait()

        for k in range(3):
            stage[k] = res[k].astype(BF16)
        _, dsz = _silu(zb_ref[...].astype(F32))
        stage[3] = (dyb_ref[...].astype(F32) * o_ref[...].astype(F32) * dsz).astype(BF16)
        for cp in out_copies(step):
            cp.start()

        @pl.when(step == batch * pairs - 1)
        def _():
            for cp in out_copies(step):
                cp.wait()
            local, remote, landed = exchange()
            for cp in remote:
                cp.wait_send()
            for cp in landed:
                cp.wait_recv()
            for cp in local:
                cp.wait()

    col0 = d // wide
    seg = lambda k: pl.BlockSpec((seq, wide), lambda b, h: (b, k * col0 + h))
    head = pl.BlockSpec((seq, wide), lambda b, h: (b, h))
    any_spec = pl.BlockSpec(memory_space=pl.ANY)
    return pl.pallas_call(
        body, name="sb_bwd", grid=(batch, pairs),
        in_specs=[seg(3), seg(4), seg(5), seg(6), head, head,
                  pl.BlockSpec((nh, seq, 1), lambda b, h: (b * pairs + h, 0, 0)), any_spec] + [any_spec] * ns,
        out_specs=[any_spec] * (ns + 1),
        out_shape=[SDS(dproj.shape, dproj.dtype)] + [SDS(s.shape, s.dtype) for s in stacks],
        input_output_aliases={7: 0},
        scratch_shapes=[pltpu.VMEM((nblk, wide, t), BF16)] * 2 + [
            pltpu.VMEM((seq, wide), BF16), pltpu.VMEM((3, seq, wide), F32),
            pltpu.VMEM((sw, sw), BF16), pltpu.VMEM((sw, sw), BF16), pltpu.VMEM((t, wide), F32),
            pltpu.VMEM((4, seq, wide), BF16), pltpu.SemaphoreType.DMA((4,)),
            pltpu.SemaphoreType.DMA((7 * ns,)), pltpu.SemaphoreType.DMA((7 * ns,)),
            pltpu.SemaphoreType.DMA((ns,))],
        compiler_params=_params(("arbitrary", "arbitrary")),
    )(proj, proj, proj, proj, o, dyb, tot, dproj, *stacks)


def _branch_a_bwd(proj, dya, norm_v, w_s, b_col, dproj):
    n = proj.shape[0]
    d = norm_v.shape[1]
    groups, chunk, _ = w_s.shape
    tr = _tile(n, 2 * chunk)

    def body(u_ref, v_ref, z_ref, dya_ref, gv_ref, ws_ref, b_ref, dproj_in,
             out_ref, dws_ref, dbias_ref, dgv_ref, vn_s, dmix_s, dvn_s, db_ref):
        del dproj_in

        @pl.when(pl.program_id(0) == 0)
        def _():
            dws_ref[...] = jnp.zeros_like(dws_ref)
            db_ref[...] = jnp.zeros_like(db_ref)
            dgv_ref[...] = jnp.zeros_like(dgv_ref)

        row, col = _iotas(chunk)
        tril = col <= row
        u, v, z, dya_v = (r[...].astype(F32) for r in (u_ref, v_ref, z_ref, dya_ref))
        gv = gv_ref[...]
        vg, dvg_dv = _gelu(v)
        r = _rms_scale(vg)
        vh = vg * r
        vn_s[...] = (vh * gv).astype(BF16)
        ug, dug_du = _gelu(u)
        sz, dsz = _silu(z)
        dmix_s[...] = dya_v * ug * sz
        for g in range(groups):
            wm = jnp.where(tril, ws_ref[g], 0.0).astype(BF16)
            cs = slice(g * chunk, (g + 1) * chunk)
            for c in range(tr // chunk):
                rs = slice(c * chunk, (c + 1) * chunk)
                vn = vn_s[rs, cs]
                mixed = _dot(wm, vn) + b_ref[g]
                dmix = dmix_s[rs, cs]
                dmix16 = dmix.astype(BF16)
                dws_ref[g] += _dot_nt(dmix16, vn)
                db_ref[g] += dmix
                dvn_s[rs, cs] = _dot_tn(wm, dmix16)
                t_u = dya_v[rs, cs] * mixed
                out_ref[rs, g * chunk:(g + 1) * chunk] = (t_u * sz[rs, cs] * dug_du[rs, cs]).astype(BF16)
                out_ref[rs, 2 * d + g * chunk:2 * d + (g + 1) * chunk] = (t_u * ug[rs, cs] * dsz[rs, cs]).astype(BF16)
        dvn = dvn_s[...]
        dgv_ref[...] += jnp.sum(dvn * vh, axis=0, keepdims=True)
        dvh = dvn * gv
        dvg = r * (dvh - vh * jnp.mean(dvh * vh, axis=-1, keepdims=True))
        out_ref[:, d:2 * d] = (dvg * dvg_dv).astype(BF16)

        @pl.when(pl.program_id(0) == n // tr - 1)
        def _():
            for g in range(groups):
                dbias_ref[g:g + 1, :] = jnp.sum(db_ref[g].T, axis=0, keepdims=True)

    seg = lambda k: pl.BlockSpec((tr, d), lambda i: (i, k))
    return pl.pallas_call(
        body, name="branch_a_bwd", grid=(n // tr,),
        in_specs=[seg(0), seg(1), seg(2), seg(0),
                  pl.BlockSpec((1, d), lambda i: (0, 0)),
                  pl.BlockSpec((groups, chunk, chunk), lambda i: (0, 0, 0)),
                  pl.BlockSpec((groups, chunk, 1), lambda i: (0, 0, 0)),
                  pl.BlockSpec(memory_space=pl.ANY)],
        out_specs=[pl.BlockSpec((tr, 3 * d), lambda i: (i, 0)),
                   pl.BlockSpec((groups, chunk, chunk), lambda i: (0, 0, 0)),
                   pl.BlockSpec((groups, chunk), lambda i: (0, 0)),
                   pl.BlockSpec((1, d), lambda i: (0, 0))],
        out_shape=[SDS(dproj.shape, dproj.dtype), SDS((groups, chunk, chunk), F32),
                   SDS((groups, chunk), F32), SDS((1, d), F32)],
        input_output_aliases={7: 0},
        scratch_shapes=[pltpu.VMEM((tr, d), BF16), pltpu.VMEM((tr, d), F32), pltpu.VMEM((tr, d), F32),
                        pltpu.VMEM((groups, chunk, chunk), F32)],
        compiler_params=_params(("arbitrary",)),
    )(proj, proj, proj, dya, norm_v, w_s, b_col, dproj)


def _dx(dproj, wg_in, x2d, dx2, norm_in):
    n, d = x2d.shape
    nsh, _, esh = wg_in.shape
    tm = _tile(n, 1024)

    def body(dp_ref, w_ref, x_ref, dx2_ref, g_ref, gx_ref, dg_ref, acc):
        i, k = pl.program_id(0), pl.program_id(1)

        @pl.when(jnp.logical_and(i == 0, k == 0))
        def _():
            dg_ref[...] = jnp.zeros_like(dg_ref)

        @pl.when(k == 0)
        def _():
            acc[...] = jnp.zeros_like(acc)

        acc[...] += _dot_nt(dp_ref[...], w_ref[0])

        @pl.when(k == nsh - 1)
        def _():
            dh = acc[...]
            x = x_ref[...]
            r = _rms_scale(x)
            xh = x * r
            dg_ref[...] += jnp.sum(dh * xh, axis=0, keepdims=True)
            dxh = dh * g_ref[...]
            gx_ref[...] = dx2_ref[...] + r * (dxh - xh * jnp.mean(dxh * xh, axis=-1, keepdims=True))

    rows = pl.BlockSpec((tm, d), lambda i, k: (i, 0))
    vec = pl.BlockSpec((1, d), lambda i, k: (0, 0))
    return pl.pallas_call(
        body, name="dx", grid=(n // tm, nsh),
        in_specs=[pl.BlockSpec((tm, esh), lambda i, k: (i, k)),
                  pl.BlockSpec((1, d, esh), lambda i, k: (k, 0, 0)), rows, rows, vec],
        out_specs=[rows, vec],
        out_shape=[SDS((n, d), F32), SDS((1, d), F32)],
        scratch_shapes=[pltpu.VMEM((tm, d), F32)],
        compiler_params=_params(("arbitrary", "arbitrary")),
    )(dproj, wg_in, x2d, dx2, norm_in)


def _adamw_outputs(g_ref, d_ref, m_ref, v_ref, g, w, m, v):
    delta, m2, v2 = _adamw(w, g, m, v)
    g_ref[...] = g
    d_ref[...] = delta
    m_ref[...] = m2
    v_ref[...] = v2


def _reduce_adamw(slots, w, m, v, name):
    _, r, c = slots.shape
    tr = _tile(r, 128)

    def body(s_ref, w_ref, m_ref, v_ref, g_out, d_out, m_out, v_out):
        g = s_ref[0].astype(F32)
        for k in range(1, N_DEV):
            g = g + s_ref[k].astype(F32)
        _adamw_outputs(g_out, d_out, m_out, v_out, g, w_ref[...], m_ref[...], v_ref[...])

    blk = pl.BlockSpec((tr, c), lambda i: (i, 0))
    return pl.pallas_call(
        body, name=name, grid=(r // tr,),
        in_specs=[pl.BlockSpec((N_DEV, tr, c), lambda i: (0, i, 0)), blk, blk, blk],
        out_specs=[blk] * 4,
        out_shape=[SDS((r, c), F32)] * 4,
        compiler_params=_params(("parallel",)),
    )(slots, w, m, v)


def _adamw_small(g, w, m, v, name):
    def body(g_ref, w_ref, m_ref, v_ref, g_out, d_out, m_out, v_out):
        _adamw_outputs(g_out, d_out, m_out, v_out, g_ref[...], w_ref[...], m_ref[...], v_ref[...])

    return pl.pallas_call(
        body, name=name,
        out_shape=[SDS(g.shape, F32)] * 4,
        in_specs=[pl.BlockSpec(memory_space=pltpu.VMEM)] * 4,
        out_specs=[pl.BlockSpec(memory_space=pltpu.VMEM)] * 4,
    )(g, w, m, v)


def kernel(x, norm_in, w_in, norm_v, w_s, b_s, w_o_gmlp, w_o_sb, w_out, norm_final, loss_target, m_norm_in, m_w_in, m_norm_v, m_w_s, m_b_s, m_w_o_gmlp, m_w_o_sb, m_w_out, m_norm_final, v_norm_in, v_w_in, v_norm_v, v_w_s, v_b_s, v_w_o_gmlp, v_w_o_sb, v_w_out, v_norm_final):
    batch, seq, d = x.shape
    n = batch * seq
    groups, chunk = w_s.shape[1], w_s.shape[2]
    hd = LANE
    x2d = x.reshape(n, d)
    tgt = loss_target.reshape(n, d)
    b_col = b_s[0].reshape(groups, chunk, 1)
    norm_final2 = norm_final.reshape(1, d)

    wg_in, wg_oa, wg_ob, wg_out = _gather_weights([w_in[0], w_o_gmlp[0], w_o_sb[0], w_out[0]])
    rsh = wg_oa.shape[1]
    wf_oa, wf_ob, wf_out = (w.reshape(N_DEV * rsh, d) for w in (wg_oa, wg_ob, wg_out))

    proj, h = _in_proj(x2d, norm_in, wg_in)
    ya = _branch_a_fwd(proj, norm_v, w_s[0], b_col)
    yb, o, sb_tot = _sb_fwd(proj, batch, seq, d, hd)
    dproj, dx2, dya, dyb, merged, dpa, dpb, loss_vec, dgf = _tail(
        x2d, tgt, ya, yb, proj, wf_oa, wf_ob, wf_out, norm_final2)
    gp_oa = _tn_matmul(ya, dpa, "dw_o_gmlp")
    gp_ob = _tn_matmul(yb, dpb, "dw_o_sb")
    gp_out = _tn_matmul(merged, dx2, "dw_out")
    dproj, s_oa, s_ob, s_out = _sb_bwd(
        proj, o, dyb, sb_tot, dproj, [g.reshape(N_DEV, rsh, d) for g in (gp_oa, gp_ob, gp_out)], batch, seq, d, hd)
    dproj, gp_ws, gp_b, gp_nv = _branch_a_bwd(proj, dya, norm_v, w_s[0], b_col, dproj)
    grad_x, gp_nin = _dx(dproj, wg_in, x2d, dx2, norm_in)

    slab = lambda a: a.reshape(d // LANE, LANE)
    gc = groups * chunk
    packed = jnp.concatenate(
        [gp_ws.reshape(gc, chunk), gp_b, slab(gp_nin), slab(gp_nv), slab(dgf), slab(loss_vec)], axis=0)
    my_slot = _slot(_me()).astype(jnp.int32).reshape(1)
    s_win, packs = _dw_in_exchange(h, dproj, my_slot, packed)
    tot, loss_slab = _finish_small(packs, groups, chunk)
    ns = d // LANE
    g_ws = tot[:gc]
    g_b = tot[gc:gc + groups]
    g_nin, g_nv, g_nf = (tot[gc + groups + k * ns:gc + groups + (k + 1) * ns] for k in range(3))
    loss = loss_slab[0, 0]

    res = {}
    res["w_in"] = _reduce_adamw(s_win, w_in[0], m_w_in[0], v_w_in[0], "adamw_w_in")
    res["w_o_gmlp"] = _reduce_adamw(s_oa, w_o_gmlp[0], m_w_o_gmlp[0], v_w_o_gmlp[0], "adamw_w_o_gmlp")
    res["w_o_sb"] = _reduce_adamw(s_ob, w_o_sb[0], m_w_o_sb[0], v_w_o_sb[0], "adamw_w_o_sb")
    res["w_out"] = _reduce_adamw(s_out, w_out[0], m_w_out[0], v_w_out[0], "adamw_w_out")
    res["norm_in"] = _adamw_small(g_nin, slab(norm_in), slab(m_norm_in), slab(v_norm_in), "adamw_norm_in")
    res["norm_v"] = _adamw_small(g_nv, slab(norm_v), slab(m_norm_v), slab(v_norm_v), "adamw_norm_v")
    res["norm_final"] = _adamw_small(g_nf, slab(norm_final), slab(m_norm_final), slab(v_norm_final), "adamw_norm_final")
    res["w_s"] = _adamw_small(g_ws, w_s.reshape(gc, chunk), m_w_s.reshape(gc, chunk), v_w_s.reshape(gc, chunk), "adamw_w_s")
    res["b_s"] = _adamw_small(g_b, b_s[0], m_b_s[0], v_b_s[0], "adamw_b_s")

    shapes = {"norm_in": norm_in.shape, "w_in": w_in.shape, "norm_v": norm_v.shape, "w_s": w_s.shape,
              "b_s": b_s.shape, "w_o_gmlp": w_o_gmlp.shape, "w_o_sb": w_o_sb.shape, "w_out": w_out.shape,
              "norm_final": norm_final.shape}
    names = list(shapes)
    outs = [loss, grad_x.reshape(batch, seq, d)]
    for kind in range(4):
        outs += [res[name][kind].reshape(shapes[name]) for name in names]
    return tuple(outs)
```

```python
import functools
import math

import jax
import jax.numpy as jnp
from jax import lax
from jax.experimental import pallas as pl
from jax.experimental.pallas import tpu as pltpu

F32 = jnp.float32
BF16 = jnp.bfloat16
SDS = jax.ShapeDtypeStruct
MESH_ID = pl.DeviceIdType.MESH

N_DEV = 8
LANE = 128
SUBLANE = 8
VMEM_LIMIT = 56 * 1024 * 1024
SB_TILE = 512
SB_TILE_BWD = 512
SB_SCAN = 256
SB_HEADS = 2
MASKED_LOG = -1e30
RMS_EPS = 1e-6

ADAM_LR = 0.001
ADAM_B1 = 0.9
ADAM_B2 = 0.999
ADAM_EPS = 1e-08
ADAM_WD = 0.01
ADAM_STEP = 10

NT_DIMS = (((1,), (1,)), ((), ()))
TN_DIMS = (((0,), (0,)), ((), ()))


def _params(semantics=None):
    return pltpu.CompilerParams(dimension_semantics=semantics, vmem_limit_bytes=VMEM_LIMIT)


def _tile(n, preferred):
    t = min(n, preferred)
    assert n % t == 0, (n, t)
    return t


def _sigmoid(x):
    return 1.0 / (1.0 + jnp.exp(-x))


def _silu(x):
    s = _sigmoid(x)
    return x * s, s * (1.0 + x * (1.0 - s))


def _gelu(x):
    k = math.sqrt(2.0 / math.pi)
    x2 = x * x
    t = jnp.tanh(k * (x + 0.044715 * (x * x2)))
    cdf = 0.5 * (1.0 + t)
    return x * cdf, cdf + 0.5 * x * (1.0 - t * t) * (k * (1.0 + 3.0 * 0.044715 * x2))


def _rms_scale(x):
    return lax.rsqrt(jnp.mean(x * x, axis=-1, keepdims=True) + RMS_EPS)


def _iotas(n):
    return (lax.broadcasted_iota(jnp.int32, (n, n), 0), lax.broadcasted_iota(jnp.int32, (n, n), 1))


def _adamw(w, g, m, v):
    m = ADAM_B1 * m + (1.0 - ADAM_B1) * g
    v = ADAM_B2 * v + (1.0 - ADAM_B2) * (g * g)
    m_hat = m / (1.0 - ADAM_B1 ** ADAM_STEP)
    v_hat = v / (1.0 - ADAM_B2 ** ADAM_STEP)
    delta = -ADAM_LR * (m_hat / (jnp.sqrt(v_hat) + ADAM_EPS) + ADAM_WD * w)
    return delta, m, v


def _dot(a, b):
    return jnp.dot(a, b, preferred_element_type=F32)


def _dot_nt(a, b):
    return lax.dot_general(a, b, NT_DIMS, preferred_element_type=F32)


def _dot_tn(a, b):
    return lax.dot_general(a, b, TN_DIMS, preferred_element_type=F32)


def _sb_logs(raw, scale, valid):
    z = (raw * scale).astype(BF16)
    log_beta = jnp.minimum(z, 0) - jnp.log(1 + jnp.exp(-jnp.abs(z)))
    log_rest = log_beta - z
    if valid is not None:
        log_beta = jnp.where(valid, log_beta, MASKED_LOG)
        log_rest = jnp.where(valid, log_rest, 0)
    return log_beta, log_rest


def _me():
    return lax.axis_index("x"), lax.axis_index("y"), lax.axis_index("c")


def _slot(p):
    return 4 * p[0] + 2 * p[1] + p[2]


def _peer(me, k):
    flips = ((k >> 2) & 1, (k >> 1) & 1, k & 1)
    return tuple(1 - a if f else a for a, f in zip(me, flips))


def _stack_exchange(me, st_in, st_out, n_whole, send_sems, recv_sems, local_sems, arrivals=True):
    mine = _slot(me)
    ns = len(st_in)
    part = lambda a, dev: st_in[a] if a >= ns - n_whole else st_in[a].at[_slot(dev)]
    local = [pltpu.make_async_copy(part(a, me), st_out[a].at[mine], local_sems.at[a]) for a in range(ns)]
    remote, landed = [], []
    for k in range(1, N_DEV):
        peer = _peer(me, k)
        for a in range(ns):
            sems = dict(send_sem=send_sems.at[7 * a + k - 1], recv_sem=recv_sems.at[7 * a + k - 1])
            remote.append(pltpu.make_async_remote_copy(
                src_ref=part(a, peer), dst_ref=st_out[a].at[mine],
                device_id=peer, device_id_type=MESH_ID, **sems))
            if arrivals:
                got = st_out[a].at[_slot(peer)]
                landed.append(pltpu.make_async_remote_copy(
                    src_ref=got, dst_ref=got, device_id=me, device_id_type=MESH_ID, **sems))
    return local, remote, landed


def _gather_weights(shards):
    n = len(shards)

    def body(*refs):
        ins, outs, stage = refs[:n], refs[n:2 * n], refs[2 * n:3 * n]
        send_sems, recv_sems, local_sems = refs[3 * n:]
        x, y, c = _me()
        me, sibling = (x, y, c), (x, y, 1 - c)
        chips = [(1 - x, y), (x, 1 - y), (1 - x, 1 - y)]

        def copy(a, k, block, to, src=None):
            dst = outs[a].at[_slot(block)]
            return pltpu.make_async_remote_copy(
                src_ref=dst if src is None else src, dst_ref=dst,
                send_sem=send_sems.at[7 * a + k], recv_sem=recv_sems.at[7 * a + k],
                device_id=to, device_id_type=MESH_ID)

        started = []
        for a in range(n):
            stage[a][...] = ins[a][...].astype(BF16)
            mine = pltpu.make_async_copy(stage[a], outs[a].at[_slot(me)], local_sems.at[a])
            mine.start()
            started.append(mine)
        sends = []
        for a in range(n):
            sends.append(copy(a, 0, me, sibling, src=stage[a]))
            sends += [copy(a, 1 + j, me, (*chip, c), src=stage[a]) for j, chip in enumerate(chips)]
        for cp in sends:
            cp.start()
        for a in range(n):
            for j, chip in enumerate(chips):
                copy(a, 1 + j, (*chip, c), me).wait_recv()
                passed = copy(a, 4 + j, (*chip, c), sibling)
                passed.start()
                sends.append(passed)
        for a in range(n):
            copy(a, 0, sibling, me).wait_recv()
            for j, chip in enumerate(chips):
                copy(a, 4 + j, (*chip, 1 - c), me).wait_recv()
        for cp in sends:
            cp.wait_send()
        for mine in started:
            mine.wait()

    return pl.pallas_call(
        body, name="gather_weights",
        out_shape=[SDS((N_DEV,) + s.shape, BF16) for s in shards],
        in_specs=[pl.BlockSpec(memory_space=pltpu.VMEM)] * n,
        out_specs=[pl.BlockSpec(memory_space=pl.ANY)] * n,
        scratch_shapes=[pltpu.VMEM(s.shape, BF16) for s in shards] + [
            pltpu.SemaphoreType.DMA((7 * n,)), pltpu.SemaphoreType.DMA((7 * n,)),
            pltpu.SemaphoreType.DMA((n,))],
        compiler_params=pltpu.CompilerParams(vmem_limit_bytes=VMEM_LIMIT),
    )(*shards)


def _dw_in_exchange(h, dproj, my_slot, packed):
    n, d = h.shape
    esh = dproj.shape[1] // N_DEV
    tk = _tile(n, 512)
    nk = n // tk
    last_j = N_DEV - 1

    def body(me_ref, h_ref, dp_ref, pk_in, win_out, pk_out,
             acc, sendbuf, win_send, win_recv, send_sems, recv_sems, local_sems):
        del me_ref
        j, k = pl.program_id(0), pl.program_id(1)
        me = _me()
        mine = _slot(me)

        def pack_copies():
            local = pltpu.make_async_copy(pk_in, pk_out.at[mine], local_sems.at[0])
            remote = [pltpu.make_async_remote_copy(
                src_ref=pk_in, dst_ref=pk_out.at[mine], send_sem=send_sems.at[kk - 1], recv_sem=recv_sems.at[kk - 1],
                device_id=_peer(me, kk), device_id_type=MESH_ID) for kk in range(1, N_DEV)]
            return local, remote

        def shard_copy(jj):
            owner = (mine + 1 + jj) % N_DEV
            return pltpu.make_async_remote_copy(
                src_ref=sendbuf.at[jj % 2], dst_ref=win_out.at[mine],
                send_sem=win_send.at[jj % 2], recv_sem=win_recv.at[mine],
                device_id=(owner // 4, (owner // 2) % 2, owner % 2), device_id_type=MESH_ID)

        def own_copy():
            return pltpu.make_async_copy(sendbuf.at[last_j % 2], win_out.at[mine], local_sems.at[1])

        @pl.when(jnp.logical_and(j == 0, k == 0))
        def _():
            local, remote = pack_copies()
            for cp in [local] + remote:
                cp.start()

        @pl.when(k == 0)
        def _():
            acc[...] = jnp.zeros_like(acc)

        acc[...] += _dot_tn(h_ref[...], dp_ref[...])

        @pl.when(k == nk - 1)
        def _():
            @pl.when(j >= 2)
            def _():
                shard_copy(j - 2).wait_send()

            sendbuf[j % 2] = acc[...].astype(BF16)

            @pl.when(j < last_j)
            def _():
                shard_copy(j).start()

            @pl.when(j == last_j)
            def _():
                own_copy().start()
                shard_copy(last_j - 1).wait_send()
                own_copy().wait()
                for src in range(N_DEV):
                    @pl.when(src != mine)
                    def _():
                        landed = win_out.at[src]
                        pltpu.make_async_remote_copy(
                            src_ref=landed, dst_ref=landed, send_sem=win_send.at[0], recv_sem=win_recv.at[src],
                            device_id=me, device_id_type=MESH_ID).wait_recv()
                local, remote = pack_copies()
                for cp in remote:
                    cp.wait_send()
                for kk in range(1, N_DEV):
                    landed = pk_out.at[_slot(_peer(me, kk))]
                    pltpu.make_async_remote_copy(
                        src_ref=landed, dst_ref=landed, send_sem=send_sems.at[kk - 1], recv_sem=recv_sems.at[kk - 1],
                        device_id=me, device_id_type=MESH_ID).wait_recv()
                local.wait()

    any_spec = pl.BlockSpec(memory_space=pl.ANY)
    grid_spec = pltpu.PrefetchScalarGridSpec(
        num_scalar_prefetch=1, grid=(N_DEV, nk),
        in_specs=[pl.BlockSpec((tk, d), lambda j, k, me: (k, 0)),
                  pl.BlockSpec((tk, esh), lambda j, k, me: (k, (me[0] + 1 + j) % N_DEV)), any_spec],
        out_specs=[any_spec] * 2,
        scratch_shapes=[pltpu.VMEM((d, esh), F32), pltpu.VMEM((2, d, esh), BF16),
                        pltpu.SemaphoreType.DMA((2,)), pltpu.SemaphoreType.DMA((N_DEV,)),
                        pltpu.SemaphoreType.DMA((N_DEV - 1,)), pltpu.SemaphoreType.DMA((N_DEV - 1,)),
                        pltpu.SemaphoreType.DMA((2,))])
    return pl.pallas_call(
        body, name="dw_in_exchange", grid_spec=grid_spec,
        out_shape=[SDS((N_DEV, d, esh), BF16), SDS((N_DEV,) + packed.shape, packed.dtype)],
        compiler_params=_params(("arbitrary", "arbitrary")),
    )(my_slot, h, dproj, packed)


def _finish_small(packs, late_packs, groups, chunk):
    rows = packs.shape[1]
    late = late_packs.shape[1]
    gc = groups * chunk

    def body(p_ref, l_ref, sum_ref, loss_ref):
        row, col = _iotas(chunk)
        tril = col <= row
        for g in range(groups):
            rs = slice(g * chunk, (g + 1) * chunk)
            tot = p_ref[0, rs, :]
            for dev in range(1, N_DEV):
                tot = tot + p_ref[dev, rs, :]
            sum_ref[rs, :] = jnp.where(tril, tot, 0.0)
        rs = slice(gc, rows)
        tot = p_ref[0, rs, :]
        for dev in range(1, N_DEV):
            tot = tot + p_ref[dev, rs, :]
        sum_ref[rs, :] = tot
        loss_ref[...] = jnp.full((SUBLANE, LANE), jnp.sum(tot[rows - gc - SUBLANE:, :]), F32)
        tot = l_ref[0]
        for dev in range(1, N_DEV):
            tot = tot + l_ref[dev]
        sum_ref[rows:rows + late, :] = tot

    return pl.pallas_call(
        body, name="finish_small",
        out_shape=[SDS((rows + late, LANE), F32), SDS((SUBLANE, LANE), F32)],
        in_specs=[pl.BlockSpec(memory_space=pltpu.VMEM)] * 2,
        out_specs=[pl.BlockSpec(memory_space=pltpu.VMEM)] * 2,
        compiler_params=pltpu.CompilerParams(vmem_limit_bytes=VMEM_LIMIT),
    )(packs, late_packs)


def _in_proj(x2d, norm_in, wg_in):
    n, d = x2d.shape
    nsh, _, esh = wg_in.shape
    tm = _tile(n, 1024)

    def body(x_ref, g_ref, w_ref, proj_ref, h_ref):
        @pl.when(pl.program_id(1) == 0)
        def _():
            x = x_ref[...]
            h_ref[...] = (x * _rms_scale(x) * g_ref[...]).astype(BF16)

        proj_ref[...] = _dot(h_ref[...], w_ref[0]).astype(BF16)

    return pl.pallas_call(
        body, name="in_proj", grid=(n // tm, nsh),
        in_specs=[pl.BlockSpec((tm, d), lambda i, j: (i, 0)),
                  pl.BlockSpec((1, d), lambda i, j: (0, 0)),
                  pl.BlockSpec((1, d, esh), lambda i, j: (j, 0, 0))],
        out_specs=[pl.BlockSpec((tm, esh), lambda i, j: (i, j)),
                   pl.BlockSpec((tm, d), lambda i, j: (i, 0))],
        out_shape=[SDS((n, nsh * esh), BF16), SDS((n, d), BF16)],
        compiler_params=_params(("parallel", "arbitrary")),
    )(x2d, norm_in, wg_in)


def _branch_a_fwd(proj, norm_v, w_s, b_col):
    n = proj.shape[0]
    d = norm_v.shape[1]
    groups, chunk, _ = w_s.shape
    tr = _tile(n, 4 * chunk)

    def body(u_ref, v_ref, z_ref, gv_ref, ws_ref, b_ref, ya_ref, vn_s, pre_s):
        row, col = _iotas(chunk)
        tril = col <= row
        vg, _ = _gelu(v_ref[...].astype(F32))
        vn_s[...] = (vg * _rms_scale(vg) * gv_ref[...]).astype(BF16)
        ug, _ = _gelu(u_ref[...].astype(F32))
        sz, _ = _silu(z_ref[...].astype(F32))
        pre_s[...] = ug * sz
        for g in range(groups):
            wm = jnp.where(tril, ws_ref[g], 0.0).astype(BF16)
            cs = slice(g * chunk, (g + 1) * chunk)
            for c in range(tr // chunk):
                rs = slice(c * chunk, (c + 1) * chunk)
                mixed = _dot(wm, vn_s[rs, cs]) + b_ref[g]
                ya_ref[rs, cs] = (pre_s[rs, cs] * mixed).astype(BF16)

    seg = lambda k: pl.BlockSpec((tr, d), lambda i: (i, k))
    return pl.pallas_call(
        body, name="branch_a_fwd", grid=(n // tr,),
        in_specs=[seg(0), seg(1), seg(2),
                  pl.BlockSpec((1, d), lambda i: (0, 0)),
                  pl.BlockSpec((groups, chunk, chunk), lambda i: (0, 0, 0)),
                  pl.BlockSpec((groups, chunk, 1), lambda i: (0, 0, 0))],
        out_specs=pl.BlockSpec((tr, d), lambda i: (i, 0)),
        out_shape=SDS((n, d), BF16),
        scratch_shapes=[pltpu.VMEM((tr, d), BF16), pltpu.VMEM((tr, d), F32)],
        compiler_params=_params(("parallel",)),
    )(proj, proj, proj, norm_v, w_s, b_col)


def _sb_fwd(proj, batch, seq, d, hd):
    heads = d // hd
    t = _tile(seq, SB_TILE)
    sw = _tile(t, SB_SCAN)
    nb = t // sw
    scale = hd ** -0.5
    nblk = seq // t
    nh = SB_HEADS
    wide = nh * hd
    cols = [slice(hh * hd, (hh + 1) * hd) for hh in range(nh)]

    def body(qs, k_ref, vs, zb_ref, yb_ref, o_ref, tot_ref, kts, later, acc):
        for jb in range(nblk):
            kts[jb] = k_ref[jb * t:(jb + 1) * t, :].astype(F32).T.astype(BF16)
        row, col = _iotas(t)
        later[...] = (row[:sw, :sw] > col[:sw, :sw]).astype(BF16)

        def qblock(i, carry):
            r0 = pl.multiple_of(i * t, t)

            def tile(j, runs, valid):
                c0 = pl.multiple_of(j * t, t)
                logs = [_sb_logs(_dot(qs[pl.ds(r0, t), cs], kts[j, cs, :]), scale, valid) for cs in cols]
                scans = [_dot(jnp.concatenate([logs[hh][1][:, b * sw:(b + 1) * sw] for b in range(nb)], axis=0),
                              later[...]) for hh in range(nh)]
                new_runs = []
                for hh in range(nh):
                    after = runs[hh]
                    blocks = [None] * nb
                    for b in reversed(range(nb)):
                        ks_ = slice(b * sw, (b + 1) * sw)
                        inside = scans[hh][b * t:(b + 1) * t]
                        blocks[b] = jnp.exp(logs[hh][0][:, ks_].astype(F32) + inside + after).astype(BF16)
                        after = after + inside[:, 0:1] + logs[hh][1][:, b * sw:b * sw + 1].astype(F32)
                    new_runs.append(after)
                    pv = _dot(jnp.concatenate(blocks, axis=1), vs[pl.ds(c0, t), cols[hh]])
                    if valid is None:
                        acc[:, cols[hh]] += pv
                    else:
                        acc[:, cols[hh]] = pv
                return tuple(new_runs)

            runs = tile(i, (jnp.zeros((t, 1), F32),) * nh, col < row)
            runs = lax.fori_loop(0, i, lambda jj, rs: tile(i - 1 - jj, rs, None), runs)
            for hh in range(nh):
                out = acc[:, cols[hh]]
                o_ref[pl.ds(r0, t), cols[hh]] = out.astype(BF16)
                tot_ref[hh, pl.ds(r0, t), :] = runs[hh]
                sz, _ = _silu(zb_ref[pl.ds(r0, t), cols[hh]].astype(F32))
                yb_ref[pl.ds(r0, t), cols[hh]] = (out * sz).astype(BF16)
            return carry

        lax.fori_loop(0, nblk, qblock, 0)

    col0 = d // wide
    seg = lambda k: pl.BlockSpec((seq, wide), lambda b, h: (b, k * col0 + h))
    return pl.pallas_call(
        body, name="sb_fwd", grid=(batch, heads // nh),
        in_specs=[seg(3), seg(4), seg(5), seg(6)],
        out_specs=[pl.BlockSpec((seq, wide), lambda b, h: (b, h))] * 2 + [
            pl.BlockSpec((nh, seq, 1), lambda b, h: (b * (heads // nh) + h, 0, 0))],
        out_shape=[SDS((batch * seq, d), BF16), SDS((batch * seq, d), BF16), SDS((batch * heads, seq, 1), F32)],
        scratch_shapes=[pltpu.VMEM((nblk, wide, t), BF16), pltpu.VMEM((sw, sw), BF16), pltpu.VMEM((t, wide), F32)],
        compiler_params=_params(("parallel", "parallel")),
    )(proj, proj, proj, proj)


def _tail(x2d, tgt, ya, yb, proj, w_oa, w_ob, w_out, norm_final):
    n, d = x2d.shape
    e = proj.shape[1]
    tm = _tile(n, 256)

    steps = n // tm

    def body(x_ref, t_ref, ya_ref, yb_ref, ga_ref, gb_ref, woa_ref, wob_ref, wout_ref, gf_ref,
             dproj_ref, dx2_ref, dya_ref, dyb_ref, mrg_ref, dpa_ref, dpb_ref, loss_ref, dgf_ref, dg_s, dg_sems):
        i = pl.program_id(0)

        def gate_copy(step):
            rows_ = pl.ds(pl.multiple_of(step * tm, tm), tm)
            return pltpu.make_async_copy(dg_s.at[step % 2], dproj_ref.at[rows_, pl.ds(7 * d, 2 * d)],
                                         dg_sems.at[step % 2])

        @pl.when(i == 0)
        def _():
            loss_ref[...] = jnp.zeros_like(loss_ref)
            dgf_ref[...] = jnp.zeros_like(dgf_ref)

        @pl.when(i >= 2)
        def _():
            gate_copy(i - 2).wait()

        pa = _dot(ya_ref[...], woa_ref[...])
        pb = _dot(yb_ref[...], wob_ref[...])
        sa = _sigmoid(ga_ref[...].astype(F32))
        sb = _sigmoid(gb_ref[...].astype(F32))
        merged = (sa * pa + sb * pb).astype(BF16)
        mrg_ref[...] = merged
        x2 = x_ref[...] + _dot(merged, wout_ref[...])
        r2 = _rms_scale(x2)
        xh = x2 * r2
        gf = gf_ref[...]
        diff = xh * gf - t_ref[...]
        loss_ref[...] += jnp.sum(diff * diff, axis=0, keepdims=True) * (0.5 / d)
        dy = diff * (1.0 / d)
        dgf_ref[...] += jnp.sum(dy * xh, axis=0, keepdims=True)
        dxh = dy * gf
        dx2 = r2 * (dxh - xh * jnp.mean(dxh * xh, axis=-1, keepdims=True))
        dx2_ref[...] = dx2
        dm = _dot_nt(dx2.astype(BF16), wout_ref[...])
        dpa = (dm * sa).astype(BF16)
        dpb = (dm * sb).astype(BF16)
        dpa_ref[...] = dpa
        dpb_ref[...] = dpb
        dg_s[i % 2, :, 0:d] = (dm * pa * (sa * (1.0 - sa))).astype(BF16)
        dg_s[i % 2, :, d:2 * d] = (dm * pb * (sb * (1.0 - sb))).astype(BF16)
        gate_copy(i).start()
        dya_ref[...] = _dot_nt(dpa, woa_ref[...]).astype(BF16)
        dyb_ref[...] = _dot_nt(dpb, wob_ref[...]).astype(BF16)

        @pl.when(i == steps - 1)
        def _():
            if steps >= 2:
                gate_copy(i - 1).wait()
            gate_copy(i).wait()

    rows = lambda k=0: pl.BlockSpec((tm, d), lambda i: (i, k))
    full = pl.BlockSpec((d, d), lambda i: (0, 0))
    vec = pl.BlockSpec((1, d), lambda i: (0, 0))
    return pl.pallas_call(
        body, name="tail", grid=(steps,),
        in_specs=[rows(), rows(), rows(), rows(), rows(7), rows(8), full, full, full, vec],
        out_specs=[pl.BlockSpec(memory_space=pl.ANY),
                   rows(), rows(), rows(), rows(), rows(), rows(), vec, vec],
        out_shape=[SDS((n, e), BF16), SDS((n, d), F32), SDS((n, d), BF16), SDS((n, d), BF16),
                   SDS((n, d), BF16), SDS((n, d), BF16), SDS((n, d), BF16),
                   SDS((1, d), F32), SDS((1, d), F32)],
        scratch_shapes=[pltpu.VMEM((2, tm, 2 * d), BF16), pltpu.SemaphoreType.DMA((2,))],
        compiler_params=_params(("arbitrary",)),
    )(x2d, tgt, ya, yb, proj, proj, w_oa, w_ob, w_out, norm_final)


def _tn_matmul(a, b, name):
    n, p = a.shape
    q = b.shape[1]
    tk = _tile(n, 512)
    nk = n // tk

    def body(a_ref, b_ref, o_ref, acc):
        k = pl.program_id(0)

        @pl.when(k == 0)
        def _():
            acc[...] = jnp.zeros_like(acc)

        acc[...] += _dot_tn(a_ref[...], b_ref[...].astype(BF16))

        @pl.when(k == nk - 1)
        def _():
            o_ref[...] = acc[...].astype(BF16)

    return pl.pallas_call(
        body, name=name, grid=(nk,),
        in_specs=[pl.BlockSpec((tk, p), lambda k: (k, 0)), pl.BlockSpec((tk, q), lambda k: (k, 0))],
        out_specs=pl.BlockSpec((p, q), lambda k: (0, 0)),
        out_shape=SDS((p, q), BF16),
        scratch_shapes=[pltpu.VMEM((p, q), F32)],
        compiler_params=_params(("arbitrary",)),
    )(a, b)


def _sb_bwd(proj, o, dyb, tot, dproj, stacks, batch, seq, d, hd):
    heads = d // hd
    t = _tile(seq, SB_TILE_BWD)
    sw = _tile(t, SB_SCAN)
    nb = t // sw
    scale = hd ** -0.5
    nblk = seq // t
    nh = SB_HEADS
    wide = nh * hd
    hs = range(nh)
    cols = [slice(hh * hd, (hh + 1) * hd) for hh in hs]
    blocks = [slice(b * sw, (b + 1) * sw) for b in range(nb)]
    last = slice(sw - 1, sw)

    def compute(qs, ks, v_ref, zb_ref, dyb_ref, tot_ref, kts, vts, dos, res, upto, before, dq):
        for jb in range(nblk):
            rows = slice(jb * t, (jb + 1) * t)
            kts[jb] = ks[rows, :].astype(F32).T.astype(BF16)
            vts[jb] = v_ref[rows, :].astype(F32).T.astype(BF16)
        sz, _ = _silu(zb_ref[...].astype(F32))
        dos[...] = (dyb_ref[...].astype(F32) * sz).astype(BF16)
        res[1] = jnp.zeros((seq, wide), F32)
        res[2] = jnp.zeros((seq, wide), F32)
        row, col = _iotas(t)
        upto[...] = (row[:sw, :sw] <= col[:sw, :sw]).astype(BF16)
        before[...] = (row[:sw, :sw] < col[:sw, :sw]).astype(BF16)

        def qblock(i, carry):
            r0 = pl.multiple_of(i * t, t)

            def tile(j, sums, valid):
                c0 = pl.multiple_of(j * t, t)
                q_i = [qs[pl.ds(r0, t), cs] for cs in cols]
                do_i = [dos[pl.ds(r0, t), cs] for cs in cols]
                logs = [_sb_logs(_dot(q_i[hh], kts[j, cols[hh], :]), scale, valid) for hh in hs]
                dw = [_dot(do_i[hh], vts[j, cols[hh], :]) for hh in hs]
                scans = [_dot(jnp.concatenate([logs[hh][1][:, ks_] for ks_ in blocks], axis=0), upto[...]) for hh in hs]
                ws, gs, new_runs = [], [], []
                for hh in hs:
                    left = tot_ref[hh, pl.ds(r0, t), :] - sums[hh][0]
                    w_b, g_b = [], []
                    for b, ks_ in enumerate(blocks):
                        inside = scans[hh][b * t:(b + 1) * t]
                        w = jnp.exp(logs[hh][0][:, ks_].astype(F32) + (left - inside))
                        w_b.append(w.astype(BF16))
                        g_b.append((dw[hh][:, ks_] * w).astype(BF16))
                        left = left - inside[:, last]
                    ws.append(jnp.concatenate(w_b, axis=1))
                    gs.append(g_b)
                    new_runs.append(tot_ref[hh, pl.ds(r0, t), :] - left)
                gscans = [_dot(jnp.concatenate(gs[hh], axis=0), before[...]) for hh in hs]
                dzs, new_gruns = [], []
                for hh in hs:
                    g_before = sums[hh][1]
                    dz_b = []
                    for b, ks_ in enumerate(blocks):
                        inside = gscans[hh][b * t:(b + 1) * t]
                        beta = jnp.exp(logs[hh][0][:, ks_]).astype(F32)
                        g = gs[hh][b].astype(F32)
                        dz_b.append(((g - (g + inside + g_before) * beta) * scale).astype(BF16))
                        g_before = g_before + inside[:, last] + g[:, last]
                    dzs.append(jnp.concatenate(dz_b, axis=1))
                    new_gruns.append(g_before)
                for hh in hs:
                    res[2, pl.ds(c0, t), cols[hh]] += _dot_tn(ws[hh], do_i[hh])
                for hh in hs:
                    res[1, pl.ds(c0, t), cols[hh]] += _dot_tn(dzs[hh], q_i[hh])
                for hh in hs:
                    dq[:, cols[hh]] += _dot(dzs[hh], ks[pl.ds(c0, t), cols[hh]])
                return tuple((new_runs[hh], new_gruns[hh]) for hh in hs)

            zero = jnp.zeros((t, 1), F32)
            dq[...] = jnp.zeros_like(dq)
            sums = lax.fori_loop(0, i, lambda j, sm: tile(j, sm, None), ((zero, zero),) * nh)
            tile(i, sums, col < row)
            res[0, pl.ds(r0, t), :] = dq[...]
            return carry

        lax.fori_loop(0, nblk, qblock, 0)

    pairs = heads // nh

    ns = len(stacks)

    def body(qs, ks, v_ref, zb_ref, o_ref, dyb_ref, tot_ref, dproj_in, *refs):
        del dproj_in
        st_in, out_ref, st_out = refs[:ns], refs[ns], refs[ns + 1:2 * ns + 1]
        (kts, vts, dos, res, upto, before, dq, stage, stage_sems,
         send_sems, recv_sems, local_sems) = refs[2 * ns + 1:]
        step = pl.program_id(0) * pairs + pl.program_id(1)
        exchange = functools.partial(_stack_exchange, _me(), st_in, st_out, 1, send_sems, recv_sems, local_sems)

        @pl.when(step == 0)
        def _():
            local, remote, _ = exchange(arrivals=False)
            for cp in local + remote:
                cp.start()

        def out_copies(s):
            rows_ = pl.ds(pl.multiple_of((s // pairs) * seq, seq), seq)
            return [pltpu.make_async_copy(
                stage.at[k], out_ref.at[rows_, pl.ds(pl.multiple_of((3 + k) * d + (s % pairs) * wide, wide), wide)],
                stage_sems.at[k]) for k in range(4)]

        compute(qs, ks, v_ref, zb_ref, dyb_ref, tot_ref, kts, vts, dos, res, upto, before, dq)

        @pl.when(step > 0)
        def _():
            for cp in out_copies(step - 1):
                cp.wait()

        for k in range(3):
            stage[k] = res[k].astype(BF16)
        _, dsz = _silu(zb_ref[...].astype(F32))
        stage[3] = (dyb_ref[...].astype(F32) * o_ref[...].astype(F32) * dsz).astype(BF16)
        for cp in out_copies(step):
            cp.start()

        @pl.when(step == batch * pairs - 1)
        def _():
            for cp in out_copies(step):
                cp.wait()
            local, remote, landed = exchange()
            for cp in remote:
                cp.wait_send()
            for cp in landed:
                cp.wait_recv()
            for cp in local:
                cp.wait()

    col0 = d // wide
    seg = lambda k: pl.BlockSpec((seq, wide), lambda b, h: (b, k * col0 + h))
    head = pl.BlockSpec((seq, wide), lambda b, h: (b, h))
    any_spec = pl.BlockSpec(memory_space=pl.ANY)
    return pl.pallas_call(
        body, name="sb_bwd", grid=(batch, pairs),
        in_specs=[seg(3), seg(4), seg(5), seg(6), head, head,
                  pl.BlockSpec((nh, seq, 1), lambda b, h: (b * pairs + h, 0, 0)), any_spec] + [any_spec] * ns,
        out_specs=[any_spec] * (ns + 1),
        out_shape=[SDS(dproj.shape, dproj.dtype)] + [SDS(s.shape, s.dtype) for s in stacks[:-1]] + [
            SDS((N_DEV,) + stacks[-1].shape, stacks[-1].dtype)],
        input_output_aliases={7: 0},
        scratch_shapes=[pltpu.VMEM((nblk, wide, t), BF16)] * 2 + [
            pltpu.VMEM((seq, wide), BF16), pltpu.VMEM((3, seq, wide), F32),
            pltpu.VMEM((sw, sw), BF16), pltpu.VMEM((sw, sw), BF16), pltpu.VMEM((t, wide), F32),
            pltpu.VMEM((4, seq, wide), BF16), pltpu.SemaphoreType.DMA((4,)),
            pltpu.SemaphoreType.DMA((7 * ns,)), pltpu.SemaphoreType.DMA((7 * ns,)),
            pltpu.SemaphoreType.DMA((ns,))],
        compiler_params=_params(("arbitrary", "arbitrary")),
    )(proj, proj, proj, proj, o, dyb, tot, dproj, *stacks)


def _branch_a_bwd(proj, dya, norm_v, w_s, b_col, dproj):
    n = proj.shape[0]
    d = norm_v.shape[1]
    groups, chunk, _ = w_s.shape
    tr = _tile(n, 2 * chunk)

    def body(u_ref, v_ref, z_ref, dya_ref, gv_ref, ws_ref, b_ref, dproj_in,
             out_ref, dws_ref, dbias_ref, dgv_ref, vn_s, dmix_s, dvn_s, db_ref):
        del dproj_in

        @pl.when(pl.program_id(0) == 0)
        def _():
            dws_ref[...] = jnp.zeros_like(dws_ref)
            db_ref[...] = jnp.zeros_like(db_ref)
            dgv_ref[...] = jnp.zeros_like(dgv_ref)

        row, col = _iotas(chunk)
        tril = col <= row
        u, v, z, dya_v = (r[...].astype(F32) for r in (u_ref, v_ref, z_ref, dya_ref))
        gv = gv_ref[...]
        vg, dvg_dv = _gelu(v)
        r = _rms_scale(vg)
        vh = vg * r
        vn_s[...] = (vh * gv).astype(BF16)
        ug, dug_du = _gelu(u)
        sz, dsz = _silu(z)
        dmix_s[...] = dya_v * ug * sz
        for g in range(groups):
            wm = jnp.where(tril, ws_ref[g], 0.0).astype(BF16)
            cs = slice(g * chunk, (g + 1) * chunk)
            for c in range(tr // chunk):
                rs = slice(c * chunk, (c + 1) * chunk)
                vn = vn_s[rs, cs]
                mixed = _dot(wm, vn) + b_ref[g]
                dmix = dmix_s[rs, cs]
                dmix16 = dmix.astype(BF16)
                dws_ref[g] += _dot_nt(dmix16, vn)
                db_ref[g] += dmix
                dvn_s[rs, cs] = _dot_tn(wm, dmix16)
                t_u = dya_v[rs, cs] * mixed
                out_ref[rs, g * chunk:(g + 1) * chunk] = (t_u * sz[rs, cs] * dug_du[rs, cs]).astype(BF16)
                out_ref[rs, 2 * d + g * chunk:2 * d + (g + 1) * chunk] = (t_u * ug[rs, cs] * dsz[rs, cs]).astype(BF16)
        dvn = dvn_s[...]
        dgv_ref[...] += jnp.sum(dvn * vh, axis=0, keepdims=True)
        dvh = dvn * gv
        dvg = r * (dvh - vh * jnp.mean(dvh * vh, axis=-1, keepdims=True))
        out_ref[:, d:2 * d] = (dvg * dvg_dv).astype(BF16)

        @pl.when(pl.program_id(0) == n // tr - 1)
        def _():
            for g in range(groups):
                dbias_ref[g:g + 1, :] = jnp.sum(db_ref[g].T, axis=0, keepdims=True)

    seg = lambda k: pl.BlockSpec((tr, d), lambda i: (i, k))
    return pl.pallas_call(
        body, name="branch_a_bwd", grid=(n // tr,),
        in_specs=[seg(0), seg(1), seg(2), seg(0),
                  pl.BlockSpec((1, d), lambda i: (0, 0)),
                  pl.BlockSpec((groups, chunk, chunk), lambda i: (0, 0, 0)),
                  pl.BlockSpec((groups, chunk, 1), lambda i: (0, 0, 0)),
                  pl.BlockSpec(memory_space=pl.ANY)],
        out_specs=[pl.BlockSpec((tr, 3 * d), lambda i: (i, 0)),
                   pl.BlockSpec((groups, chunk, chunk), lambda i: (0, 0, 0)),
                   pl.BlockSpec((groups, chunk), lambda i: (0, 0)),
                   pl.BlockSpec((1, d), lambda i: (0, 0))],
        out_shape=[SDS(dproj.shape, dproj.dtype), SDS((groups, chunk, chunk), F32),
                   SDS((groups, chunk), F32), SDS((1, d), F32)],
        input_output_aliases={7: 0},
        scratch_shapes=[pltpu.VMEM((tr, d), BF16), pltpu.VMEM((tr, d), F32), pltpu.VMEM((tr, d), F32),
                        pltpu.VMEM((groups, chunk, chunk), F32)],
        compiler_params=_params(("arbitrary",)),
    )(proj, proj, proj, dya, norm_v, w_s, b_col, dproj)


def _dx(dproj, wg_in, x2d, dx2, norm_in):
    n, d = x2d.shape
    nsh, _, esh = wg_in.shape
    tm = _tile(n, 1024)

    def body(dp_ref, w_ref, x_ref, dx2_ref, g_ref, gx_ref, dg_ref, acc):
        i, k = pl.program_id(0), pl.program_id(1)

        @pl.when(jnp.logical_and(i == 0, k == 0))
        def _():
            dg_ref[...] = jnp.zeros_like(dg_ref)

        @pl.when(k == 0)
        def _():
            acc[...] = jnp.zeros_like(acc)

        acc[...] += _dot_nt(dp_ref[...], w_ref[0])

        @pl.when(k == nsh - 1)
        def _():
            dh = acc[...]
            x = x_ref[...]
            r = _rms_scale(x)
            xh = x * r
            dg_ref[...] += jnp.sum(dh * xh, axis=0, keepdims=True)
            dxh = dh * g_ref[...]
            gx_ref[...] = dx2_ref[...] + r * (dxh - xh * jnp.mean(dxh * xh, axis=-1, keepdims=True))

    rows = pl.BlockSpec((tm, d), lambda i, k: (i, 0))
    vec = pl.BlockSpec((1, d), lambda i, k: (0, 0))
    return pl.pallas_call(
        body, name="dx", grid=(n // tm, nsh),
        in_specs=[pl.BlockSpec((tm, esh), lambda i, k: (i, k)),
                  pl.BlockSpec((1, d, esh), lambda i, k: (k, 0, 0)), rows, rows, vec],
        out_specs=[rows, vec],
        out_shape=[SDS((n, d), F32), SDS((1, d), F32)],
        scratch_shapes=[pltpu.VMEM((tm, d), F32)],
        compiler_params=_params(("arbitrary", "arbitrary")),
    )(dproj, wg_in, x2d, dx2, norm_in)


def _adamw_outputs(g_ref, d_ref, m_ref, v_ref, g, w, m, v):
    delta, m2, v2 = _adamw(w, g, m, v)
    g_ref[...] = g
    d_ref[...] = delta
    m_ref[...] = m2
    v_ref[...] = v2


def _reduce_adamw(slots, w, m, v, name):
    _, r, c = slots.shape
    tr = _tile(r, 128)

    def body(s_ref, w_ref, m_ref, v_ref, g_out, d_out, m_out, v_out):
        g = s_ref[0].astype(F32)
        for k in range(1, N_DEV):
            g = g + s_ref[k].astype(F32)
        _adamw_outputs(g_out, d_out, m_out, v_out, g, w_ref[...], m_ref[...], v_ref[...])

    blk = pl.BlockSpec((tr, c), lambda i: (i, 0))
    return pl.pallas_call(
        body, name=name, grid=(r // tr,),
        in_specs=[pl.BlockSpec((N_DEV, tr, c), lambda i: (0, i, 0)), blk, blk, blk],
        out_specs=[blk] * 4,
        out_shape=[SDS((r, c), F32)] * 4,
        compiler_params=_params(("parallel",)),
    )(slots, w, m, v)


def _adamw_small(g, w, m, v, name):
    def body(g_ref, w_ref, m_ref, v_ref, g_out, d_out, m_out, v_out):
        _adamw_outputs(g_out, d_out, m_out, v_out, g_ref[...], w_ref[...], m_ref[...], v_ref[...])

    return pl.pallas_call(
        body, name=name,
        out_shape=[SDS(g.shape, F32)] * 4,
        in_specs=[pl.BlockSpec(memory_space=pltpu.VMEM)] * 4,
        out_specs=[pl.BlockSpec(memory_space=pltpu.VMEM)] * 4,
    )(g, w, m, v)


def kernel(x, norm_in, w_in, norm_v, w_s, b_s, w_o_gmlp, w_o_sb, w_out, norm_final, loss_target, m_norm_in, m_w_in, m_norm_v, m_w_s, m_b_s, m_w_o_gmlp, m_w_o_sb, m_w_out, m_norm_final, v_norm_in, v_w_in, v_norm_v, v_w_s, v_b_s, v_w_o_gmlp, v_w_o_sb, v_w_out, v_norm_final):
    batch, seq, d = x.shape
    n = batch * seq
    groups, chunk = w_s.shape[1], w_s.shape[2]
    hd = LANE
    x2d = x.reshape(n, d)
    tgt = loss_target.reshape(n, d)
    b_col = b_s[0].reshape(groups, chunk, 1)
    norm_final2 = norm_final.reshape(1, d)

    wg_in, wg_oa, wg_ob, wg_out = _gather_weights([w_in[0], w_o_gmlp[0], w_o_sb[0], w_out[0]])
    rsh = wg_oa.shape[1]
    wf_oa, wf_ob, wf_out = (w.reshape(N_DEV * rsh, d) for w in (wg_oa, wg_ob, wg_out))

    proj, h = _in_proj(x2d, norm_in, wg_in)
    ya = _branch_a_fwd(proj, norm_v, w_s[0], b_col)
    yb, o, sb_tot = _sb_fwd(proj, batch, seq, d, hd)
    dproj, dx2, dya, dyb, merged, dpa, dpb, loss_vec, dgf = _tail(
        x2d, tgt, ya, yb, proj, wf_oa, wf_ob, wf_out, norm_final2)
    gp_oa = _tn_matmul(ya, dpa, "dw_o_gmlp")
    gp_ob = _tn_matmul(yb, dpb, "dw_o_sb")
    gp_out = _tn_matmul(merged, dx2, "dw_out")
    dproj, gp_ws, gp_b, gp_nv = _branch_a_bwd(proj, dya, norm_v, w_s[0], b_col, dproj)

    slab = lambda a: a.reshape(d // LANE, LANE)
    gc = groups * chunk
    packed = jnp.concatenate([gp_ws.reshape(gc, chunk), gp_b, slab(gp_nv), slab(dgf), slab(loss_vec)], axis=0)
    dproj, s_oa, s_ob, s_out, packs = _sb_bwd(
        proj, o, dyb, sb_tot, dproj, [g.reshape(N_DEV, rsh, d) for g in (gp_oa, gp_ob, gp_out)] + [packed],
        batch, seq, d, hd)
    grad_x, gp_nin = _dx(dproj, wg_in, x2d, dx2, norm_in)
    my_slot = _slot(_me()).astype(jnp.int32).reshape(1)
    s_win, late_packs = _dw_in_exchange(h, dproj, my_slot, slab(gp_nin))
    tot, loss_slab = _finish_small(packs, late_packs, groups, chunk)
    ns = d // LANE
    g_ws = tot[:gc]
    g_b = tot[gc:gc + groups]
    g_nv, g_nf, _, g_nin = (tot[gc + groups + k * ns:gc + groups + (k + 1) * ns] for k in range(4))
    loss = loss_slab[0, 0]

    res = {}
    res["w_in"] = _reduce_adamw(s_win, w_in[0], m_w_in[0], v_w_in[0], "adamw_w_in")
    res["w_o_gmlp"] = _reduce_adamw(s_oa, w_o_gmlp[0], m_w_o_gmlp[0], v_w_o_gmlp[0], "adamw_w_o_gmlp")
    res["w_o_sb"] = _reduce_adamw(s_ob, w_o_sb[0], m_w_o_sb[0], v_w_o_sb[0], "adamw_w_o_sb")
    res["w_out"] = _reduce_adamw(s_out, w_out[0], m_w_out[0], v_w_out[0], "adamw_w_out")
    res["norm_in"] = _adamw_small(g_nin, slab(norm_in), slab(m_norm_in), slab(v_norm_in), "adamw_norm_in")
    res["norm_v"] = _adamw_small(g_nv, slab(norm_v), slab(m_norm_v), slab(v_norm_v), "adamw_norm_v")
    res["norm_final"] = _adamw_small(g_nf, slab(norm_final), slab(m_norm_final), slab(v_norm_final), "adamw_norm_final")
    res["w_s"] = _adamw_small(g_ws, w_s.reshape(gc, chunk), m_w_s.reshape(gc, chunk), v_w_s.reshape(gc, chunk), "adamw_w_s")
    res["b_s"] = _adamw_small(g_b, b_s[0], m_b_s[0], v_b_s[0], "adamw_b_s")

    shapes = {"norm_in": norm_in.shape, "w_in": w_in.shape, "norm_v": norm_v.shape, "w_s": w_s.shape,
              "b_s": b_s.shape, "w_o_gmlp": w_o_gmlp.shape, "w_o_sb": w_o_sb.shape, "w_out": w_out.shape,
              "norm_final": norm_final.shape}
    names = list(shapes)
    outs = [loss, grad_x.reshape(batch, seq, d)]
    for kind in range(4):
        outs += [res[name][kind].reshape(shapes[name]) for name in names]
    return tuple(outs)
```

```python
import functools
import math

import jax
import jax.numpy as jnp
from jax import lax
from jax.experimental import pallas as pl
from jax.experimental.pallas import tpu as pltpu

F32 = jnp.float32
BF16 = jnp.bfloat16
SDS = jax.ShapeDtypeStruct
MESH_ID = pl.DeviceIdType.MESH

N_DEV = 8
LANE = 128
SUBLANE = 8
VMEM_LIMIT = 56 * 1024 * 1024
SB_TILE = 512
SB_TILE_BWD = 512
SB_SCAN = 256
SB_HEADS = 2
MASKED_LOG = -1e30
RMS_EPS = 1e-6

ADAM_LR = 0.001
ADAM_B1 = 0.9
ADAM_B2 = 0.999
ADAM_EPS = 1e-08
ADAM_WD = 0.01
ADAM_STEP = 10

NT_DIMS = (((1,), (1,)), ((), ()))
TN_DIMS = (((0,), (0,)), ((), ()))


def _params(semantics=None):
    return pltpu.CompilerParams(dimension_semantics=semantics, vmem_limit_bytes=VMEM_LIMIT)


def _tile(n, preferred):
    t = min(n, preferred)
    assert n % t == 0, (n, t)
    return t


def _sigmoid(x):
    return 1.0 / (1.0 + jnp.exp(-x))


def _silu(x):
    s = _sigmoid(x)
    return x * s, s * (1.0 + x * (1.0 - s))


def _gelu(x):
    k = math.sqrt(2.0 / math.pi)
    x2 = x * x
    t = jnp.tanh(k * (x + 0.044715 * (x * x2)))
    cdf = 0.5 * (1.0 + t)
    return x * cdf, cdf + 0.5 * x * (1.0 - t * t) * (k * (1.0 + 3.0 * 0.044715 * x2))


def _rms_scale(x):
    return lax.rsqrt(jnp.mean(x * x, axis=-1, keepdims=True) + RMS_EPS)


def _iotas(n):
    return (lax.broadcasted_iota(jnp.int32, (n, n), 0), lax.broadcasted_iota(jnp.int32, (n, n), 1))


def _adamw(w, g, m, v):
    m = ADAM_B1 * m + (1.0 - ADAM_B1) * g
    v = ADAM_B2 * v + (1.0 - ADAM_B2) * (g * g)
    m_hat = m / (1.0 - ADAM_B1 ** ADAM_STEP)
    v_hat = v / (1.0 - ADAM_B2 ** ADAM_STEP)
    delta = -ADAM_LR * (m_hat / (jnp.sqrt(v_hat) + ADAM_EPS) + ADAM_WD * w)
    return delta, m, v


def _dot(a, b):
    return jnp.dot(a, b, preferred_element_type=F32)


def _dot_nt(a, b):
    return lax.dot_general(a, b, NT_DIMS, preferred_element_type=F32)


def _dot_tn(a, b):
    return lax.dot_general(a, b, TN_DIMS, preferred_element_type=F32)


def _sb_logs(raw, scale, valid):
    z = (raw * scale).astype(BF16)
    log_beta = jnp.minimum(z, 0) - jnp.log(1 + jnp.exp(-jnp.abs(z)))
    log_rest = log_beta - z
    if valid is not None:
        log_beta = jnp.where(valid, log_beta, MASKED_LOG)
        log_rest = jnp.where(valid, log_rest, 0)
    return log_beta, log_rest


def _me():
    return lax.axis_index("x"), lax.axis_index("y"), lax.axis_index("c")


def _slot(p):
    return 4 * p[0] + 2 * p[1] + p[2]


def _peer(me, k):
    flips = ((k >> 2) & 1, (k >> 1) & 1, k & 1)
    return tuple(1 - a if f else a for a, f in zip(me, flips))


def _stack_exchange(me, st_in, st_out, n_whole, send_sems, recv_sems, local_sems, arrivals=True):
    mine = _slot(me)
    ns = len(st_in)
    part = lambda a, dev: st_in[a] if a >= ns - n_whole else st_in[a].at[_slot(dev)]
    local = [pltpu.make_async_copy(part(a, me), st_out[a].at[mine], local_sems.at[a]) for a in range(ns)]
    remote, landed = [], []
    for k in range(1, N_DEV):
        peer = _peer(me, k)
        for a in range(ns):
            sems = dict(send_sem=send_sems.at[7 * a + k - 1], recv_sem=recv_sems.at[7 * a + k - 1])
            remote.append(pltpu.make_async_remote_copy(
                src_ref=part(a, peer), dst_ref=st_out[a].at[mine],
                device_id=peer, device_id_type=MESH_ID, **sems))
            if arrivals:
                got = st_out[a].at[_slot(peer)]
                landed.append(pltpu.make_async_remote_copy(
                    src_ref=got, dst_ref=got, device_id=me, device_id_type=MESH_ID, **sems))
    return local, remote, landed


def _gather_weights(shards):
    n = len(shards)

    def body(*refs):
        ins, outs, stage = refs[:n], refs[n:2 * n], refs[2 * n:3 * n]
        send_sems, recv_sems, local_sems = refs[3 * n:]
        x, y, c = _me()
        me, sibling = (x, y, c), (x, y, 1 - c)
        chips = [(1 - x, y), (x, 1 - y), (1 - x, 1 - y)]

        def copy(a, k, block, to, src=None):
            dst = outs[a].at[_slot(block)]
            return pltpu.make_async_remote_copy(
                src_ref=dst if src is None else src, dst_ref=dst,
                send_sem=send_sems.at[7 * a + k], recv_sem=recv_sems.at[7 * a + k],
                device_id=to, device_id_type=MESH_ID)

        started = []
        for a in range(n):
            stage[a][...] = ins[a][...].astype(BF16)
            mine = pltpu.make_async_copy(stage[a], outs[a].at[_slot(me)], local_sems.at[a])
            mine.start()
            started.append(mine)
        sends = []
        for a in range(n):
            sends.append(copy(a, 0, me, sibling, src=stage[a]))
            sends += [copy(a, 1 + j, me, (*chip, c), src=stage[a]) for j, chip in enumerate(chips)]
        for cp in sends:
            cp.start()
        for a in range(n):
            for j, chip in enumerate(chips):
                copy(a, 1 + j, (*chip, c), me).wait_recv()
                passed = copy(a, 4 + j, (*chip, c), sibling)
                passed.start()
                sends.append(passed)
        for a in range(n):
            copy(a, 0, sibling, me).wait_recv()
            for j, chip in enumerate(chips):
                copy(a, 4 + j, (*chip, 1 - c), me).wait_recv()
        for cp in sends:
            cp.wait_send()
        for mine in started:
            mine.wait()

    return pl.pallas_call(
        body, name="gather_weights",
        out_shape=[SDS((N_DEV,) + s.shape, BF16) for s in shards],
        in_specs=[pl.BlockSpec(memory_space=pltpu.VMEM)] * n,
        out_specs=[pl.BlockSpec(memory_space=pl.ANY)] * n,
        scratch_shapes=[pltpu.VMEM(s.shape, BF16) for s in shards] + [
            pltpu.SemaphoreType.DMA((7 * n,)), pltpu.SemaphoreType.DMA((7 * n,)),
            pltpu.SemaphoreType.DMA((n,))],
        compiler_params=pltpu.CompilerParams(vmem_limit_bytes=VMEM_LIMIT),
    )(*shards)


def _gather_in_proj(x2d, norm_in, w_in_sh, wo_shards, my_slot):
    n, d = x2d.shape
    esh = w_in_sh.shape[1]
    pw = 2 * esh
    n_chip = N_DEV // 2
    tm = _tile(n, 1024)
    n_i = n // tm
    mid = n_i // 2
    no = len(wo_shards)
    flip_at = lambda st: jnp.where(st == 1, 2, jnp.where(st == 2, 1, jnp.where(st == 3, 3, 0)))

    def body(me_ref, x_ref, g_ref, win_ref, *refs):
        del me_ref
        wo_in = refs[:no]
        proj_ref, h_ref, wg_ref = refs[no:no + 3]
        wo_out = refs[no + 3:2 * no + 3]
        wv, stage = refs[2 * no + 3:2 * no + 5]
        wo_stage = refs[2 * no + 5:3 * no + 5]
        send_sems, recv_sems, pair_sems, own_sems, wo_send, wo_recv, wo_local = refs[3 * no + 5:]
        st, i = pl.program_id(0), pl.program_id(1)
        x, y, c = _me()
        me, sibling = (x, y, c), (x, y, 1 - c)
        chips = [(1 - x, y), (x, 1 - y), (1 - x, 1 - y)]
        chip_id = lambda p: 2 * p[0] + p[1]

        def window(chip, core):
            return wv.at[chip_id(chip), :, pl.ds(pl.multiple_of(core * esh, LANE), esh)]

        def copy(k, block, to, src=None):
            dst = window(block[:2], block[2])
            return pltpu.make_async_remote_copy(
                src_ref=dst if src is None else src, dst_ref=dst,
                send_sem=send_sems.at[k], recv_sem=recv_sems.at[k], device_id=to, device_id_type=MESH_ID)

        def wo_copy(a, k, block, to, src=None):
            dst = wo_out[a].at[_slot(block)]
            return pltpu.make_async_remote_copy(
                src_ref=dst if src is None else src, dst_ref=dst,
                send_sem=wo_send.at[7 * a + k], recv_sem=wo_recv.at[7 * a + k], device_id=to, device_id_type=MESH_ID)

        def own_copy():
            return pltpu.make_async_copy(stage, window((x, y), c), own_sems.at[0])

        def wo_own_copy(a):
            return pltpu.make_async_copy(wo_stage[a], wo_out[a].at[_slot(me)], wo_local.at[a])

        def pair_copy(step):
            chip = jnp.bitwise_xor(chip_id((x, y)), flip_at(step))
            return pltpu.make_async_copy(wv.at[chip], wg_ref.at[:, pl.ds(pl.multiple_of(chip * pw, LANE), pw)],
                                         pair_sems.at[step])

        first = jnp.logical_and(st == 0, i == 0)

        @pl.when(first)
        def _():
            stage[...] = win_ref[...].astype(BF16)
            own_copy().start()
            copy(0, me, sibling, src=stage).start()
            for j, chip in enumerate(chips):
                copy(1 + j, me, (*chip, c), src=stage).start()
            for a in range(no):
                wo_stage[a][...] = wo_in[a][...].astype(BF16)
                wo_own_copy(a).start()
                wo_copy(a, 0, me, sibling, src=wo_stage[a]).start()
                for j, chip in enumerate(chips):
                    wo_copy(a, 1 + j, me, (*chip, c), src=wo_stage[a]).start()
            own_copy().wait()
            copy(0, sibling, me).wait_recv()
            pair_copy(0).start()

        for s_ in range(n_chip - 1):
            @pl.when(jnp.logical_and(st == s_, i == mid))
            def _():
                copy(1 + s_, (*chips[s_], c), me).wait_recv()
                copy(4 + s_, (*chips[s_], c), sibling).start()
                if s_ == 0:
                    for a in range(no):
                        for j, chip in enumerate(chips):
                            wo_copy(a, 1 + j, (*chip, c), me).wait_recv()
                            wo_copy(a, 4 + j, (*chip, c), sibling).start()

        for s_ in range(1, n_chip):
            @pl.when(jnp.logical_and(st == s_, i == 0))
            def _():
                copy(3 + s_, (*chips[s_ - 1], 1 - c), me).wait_recv()
                pair_copy(s_).start()

        xv = x_ref[...]
        h = (xv * _rms_scale(xv) * g_ref[...]).astype(BF16)

        @pl.when(st == 0)
        def _():
            h_ref[...] = h

        chip_now = jnp.bitwise_xor(chip_id((x, y)), flip_at(st))
        proj_ref[...] = _dot(h, wv[chip_now]).astype(BF16)

        @pl.when(jnp.logical_and(st == n_chip - 1, i == n_i - 1))
        def _():
            copy(0, me, sibling, src=stage).wait_send()
            for j, chip in enumerate(chips):
                copy(1 + j, me, (*chip, c), src=stage).wait_send()
                copy(4 + j, (*chip, c), sibling).wait_send()
            for s_ in range(n_chip):
                pair_copy(s_).wait()
            for a in range(no):
                wo_copy(a, 0, me, sibling, src=wo_stage[a]).wait_send()
                wo_copy(a, 0, sibling, me).wait_recv()
                for j, chip in enumerate(chips):
                    wo_copy(a, 1 + j, me, (*chip, c), src=wo_stage[a]).wait_send()
                    wo_copy(a, 4 + j, (*chip, c), sibling).wait_send()
                    wo_copy(a, 4 + j, (*chip, 1 - c), me).wait_recv()
                wo_own_copy(a).wait()

    any_spec = pl.BlockSpec(memory_space=pl.ANY)
    vmem = pl.BlockSpec(memory_space=pltpu.VMEM)
    grid_spec = pltpu.PrefetchScalarGridSpec(
        num_scalar_prefetch=1, grid=(n_chip, n_i),
        in_specs=[pl.BlockSpec((tm, d), lambda st, i, me: (i, 0)),
                  pl.BlockSpec((1, d), lambda st, i, me: (0, 0)), vmem] + [vmem] * no,
        out_specs=[pl.BlockSpec((tm, pw), lambda st, i, me: (i, jnp.bitwise_xor(me[0] // 2, flip_at(st)))),
                   pl.BlockSpec((tm, d), lambda st, i, me: (jnp.where(st == 0, i, n_i - 1), 0)),
                   any_spec] + [any_spec] * no,
        scratch_shapes=[pltpu.VMEM((n_chip, d, pw), BF16), pltpu.VMEM((d, esh), BF16)] + [
            pltpu.VMEM(s.shape, BF16) for s in wo_shards] + [
            pltpu.SemaphoreType.DMA((7,)), pltpu.SemaphoreType.DMA((7,)),
            pltpu.SemaphoreType.DMA((n_chip,)), pltpu.SemaphoreType.DMA((1,)),
            pltpu.SemaphoreType.DMA((7 * no,)), pltpu.SemaphoreType.DMA((7 * no,)),
            pltpu.SemaphoreType.DMA((no,))])
    return pl.pallas_call(
        body, name="gather_in_proj", grid_spec=grid_spec,
        out_shape=[SDS((n, n_chip * pw), BF16), SDS((n, d), BF16), SDS((d, n_chip * pw), BF16)] + [
            SDS((N_DEV,) + s.shape, BF16) for s in wo_shards],
        compiler_params=pltpu.CompilerParams(dimension_semantics=("arbitrary", "arbitrary"),
                                             vmem_limit_bytes=VMEM_LIMIT),
    )(my_slot, x2d, norm_in, w_in_sh, *wo_shards)


def _dw_in_exchange(h, dproj, my_slot, packed):
    n, d = h.shape
    esh = dproj.shape[1] // N_DEV
    tk = _tile(n, 512)
    nk = n // tk
    last_j = N_DEV - 1

    def body(me_ref, h_ref, dp_ref, pk_in, win_out, pk_out,
             acc, sendbuf, win_send, win_recv, send_sems, recv_sems, local_sems):
        del me_ref
        j, k = pl.program_id(0), pl.program_id(1)
        me = _me()
        mine = _slot(me)

        def pack_copies():
            local = pltpu.make_async_copy(pk_in, pk_out.at[mine], local_sems.at[0])
            remote = [pltpu.make_async_remote_copy(
                src_ref=pk_in, dst_ref=pk_out.at[mine], send_sem=send_sems.at[kk - 1], recv_sem=recv_sems.at[kk - 1],
                device_id=_peer(me, kk), device_id_type=MESH_ID) for kk in range(1, N_DEV)]
            return local, remote

        def shard_copy(jj):
            owner = (mine + 1 + jj) % N_DEV
            return pltpu.make_async_remote_copy(
                src_ref=sendbuf.at[jj % 2], dst_ref=win_out.at[mine],
                send_sem=win_send.at[jj % 2], recv_sem=win_recv.at[mine],
                device_id=(owner // 4, (owner // 2) % 2, owner % 2), device_id_type=MESH_ID)

        def own_copy():
            return pltpu.make_async_copy(sendbuf.at[last_j % 2], win_out.at[mine], local_sems.at[1])

        @pl.when(jnp.logical_and(j == 0, k == 0))
        def _():
            local, remote = pack_copies()
            for cp in [local] + remote:
                cp.start()

        @pl.when(k == 0)
        def _():
            acc[...] = jnp.zeros_like(acc)

        acc[...] += _dot_tn(h_ref[...], dp_ref[...])

        @pl.when(k == nk - 1)
        def _():
            @pl.when(j >= 2)
            def _():
                shard_copy(j - 2).wait_send()

            sendbuf[j % 2] = acc[...].astype(BF16)

            @pl.when(j < last_j)
            def _():
                shard_copy(j).start()

            @pl.when(j == last_j)
            def _():
                own_copy().start()
                shard_copy(last_j - 1).wait_send()
                own_copy().wait()
                for src in range(N_DEV):
                    @pl.when(src != mine)
                    def _():
                        landed = win_out.at[src]
                        pltpu.make_async_remote_copy(
                            src_ref=landed, dst_ref=landed, send_sem=win_send.at[0], recv_sem=win_recv.at[src],
                            device_id=me, device_id_type=MESH_ID).wait_recv()
                local, remote = pack_copies()
                for cp in remote:
                    cp.wait_send()
                for kk in range(1, N_DEV):
                    landed = pk_out.at[_slot(_peer(me, kk))]
                    pltpu.make_async_remote_copy(
                        src_ref=landed, dst_ref=landed, send_sem=send_sems.at[kk - 1], recv_sem=recv_sems.at[kk - 1],
                        device_id=me, device_id_type=MESH_ID).wait_recv()
                local.wait()

    any_spec = pl.BlockSpec(memory_space=pl.ANY)
    grid_spec = pltpu.PrefetchScalarGridSpec(
        num_scalar_prefetch=1, grid=(N_DEV, nk),
        in_specs=[pl.BlockSpec((tk, d), lambda j, k, me: (k, 0)),
                  pl.BlockSpec((tk, esh), lambda j, k, me: (k, (me[0] + 1 + j) % N_DEV)), any_spec],
        out_specs=[any_spec] * 2,
        scratch_shapes=[pltpu.VMEM((d, esh), F32), pltpu.VMEM((2, d, esh), BF16),
                        pltpu.SemaphoreType.DMA((2,)), pltpu.SemaphoreType.DMA((N_DEV,)),
                        pltpu.SemaphoreType.DMA((N_DEV - 1,)), pltpu.SemaphoreType.DMA((N_DEV - 1,)),
                        pltpu.SemaphoreType.DMA((2,))])
    return pl.pallas_call(
        body, name="dw_in_exchange", grid_spec=grid_spec,
        out_shape=[SDS((N_DEV, d, esh), BF16), SDS((N_DEV,) + packed.shape, packed.dtype)],
        compiler_params=_params(("arbitrary", "arbitrary")),
    )(my_slot, h, dproj, packed)


def _finish_small(packs, late_packs, groups, chunk):
    rows = packs.shape[1]
    late = late_packs.shape[1]
    gc = groups * chunk

    def body(p_ref, l_ref, sum_ref, loss_ref):
        row, col = _iotas(chunk)
        tril = col <= row
        for g in range(groups):
            rs = slice(g * chunk, (g + 1) * chunk)
            tot = p_ref[0, rs, :]
            for dev in range(1, N_DEV):
                tot = tot + p_ref[dev, rs, :]
            sum_ref[rs, :] = jnp.where(tril, tot, 0.0)
        rs = slice(gc, rows)
        tot = p_ref[0, rs, :]
        for dev in range(1, N_DEV):
            tot = tot + p_ref[dev, rs, :]
        sum_ref[rs, :] = tot
        loss_ref[...] = jnp.full((SUBLANE, LANE), jnp.sum(tot[rows - gc - SUBLANE:, :]), F32)
        tot = l_ref[0]
        for dev in range(1, N_DEV):
            tot = tot + l_ref[dev]
        sum_ref[rows:rows + late, :] = tot

    return pl.pallas_call(
        body, name="finish_small",
        out_shape=[SDS((rows + late, LANE), F32), SDS((SUBLANE, LANE), F32)],
        in_specs=[pl.BlockSpec(memory_space=pltpu.VMEM)] * 2,
        out_specs=[pl.BlockSpec(memory_space=pltpu.VMEM)] * 2,
        compiler_params=pltpu.CompilerParams(vmem_limit_bytes=VMEM_LIMIT),
    )(packs, late_packs)


def _in_proj(x2d, norm_in, wg_in):
    n, d = x2d.shape
    nsh, _, esh = wg_in.shape
    tm = _tile(n, 1024)

    def body(x_ref, g_ref, w_ref, proj_ref, h_ref):
        @pl.when(pl.program_id(1) == 0)
        def _():
            x = x_ref[...]
            h_ref[...] = (x * _rms_scale(x) * g_ref[...]).astype(BF16)

        proj_ref[...] = _dot(h_ref[...], w_ref[0]).astype(BF16)

    return pl.pallas_call(
        body, name="in_proj", grid=(n // tm, nsh),
        in_specs=[pl.BlockSpec((tm, d), lambda i, j: (i, 0)),
                  pl.BlockSpec((1, d), lambda i, j: (0, 0)),
                  pl.BlockSpec((1, d, esh), lambda i, j: (j, 0, 0))],
        out_specs=[pl.BlockSpec((tm, esh), lambda i, j: (i, j)),
                   pl.BlockSpec((tm, d), lambda i, j: (i, 0))],
        out_shape=[SDS((n, nsh * esh), BF16), SDS((n, d), BF16)],
        compiler_params=_params(("parallel", "arbitrary")),
    )(x2d, norm_in, wg_in)


def _branch_a_fwd(proj, norm_v, w_s, b_col):
    n = proj.shape[0]
    d = norm_v.shape[1]
    groups, chunk, _ = w_s.shape
    tr = _tile(n, 4 * chunk)

    def body(u_ref, v_ref, z_ref, gv_ref, ws_ref, b_ref, ya_ref, vn_s, pre_s):
        row, col = _iotas(chunk)
        tril = col <= row
        vg, _ = _gelu(v_ref[...].astype(F32))
        vn_s[...] = (vg * _rms_scale(vg) * gv_ref[...]).astype(BF16)
        ug, _ = _gelu(u_ref[...].astype(F32))
        sz, _ = _silu(z_ref[...].astype(F32))
        pre_s[...] = ug * sz
        for g in range(groups):
            wm = jnp.where(tril, ws_ref[g], 0.0).astype(BF16)
            cs = slice(g * chunk, (g + 1) * chunk)
            for c in range(tr // chunk):
                rs = slice(c * chunk, (c + 1) * chunk)
                mixed = _dot(wm, vn_s[rs, cs]) + b_ref[g]
                ya_ref[rs, cs] = (pre_s[rs, cs] * mixed).astype(BF16)

    seg = lambda k: pl.BlockSpec((tr, d), lambda i: (i, k))
    return pl.pallas_call(
        body, name="branch_a_fwd", grid=(n // tr,),
        in_specs=[seg(0), seg(1), seg(2),
                  pl.BlockSpec((1, d), lambda i: (0, 0)),
                  pl.BlockSpec((groups, chunk, chunk), lambda i: (0, 0, 0)),
                  pl.BlockSpec((groups, chunk, 1), lambda i: (0, 0, 0))],
        out_specs=pl.BlockSpec((tr, d), lambda i: (i, 0)),
        out_shape=SDS((n, d), BF16),
        scratch_shapes=[pltpu.VMEM((tr, d), BF16), pltpu.VMEM((tr, d), F32)],
        compiler_params=_params(("parallel",)),
    )(proj, proj, proj, norm_v, w_s, b_col)


def _sb_fwd(proj, batch, seq, d, hd):
    heads = d // hd
    t = _tile(seq, SB_TILE)
    sw = _tile(t, SB_SCAN)
    nb = t // sw
    scale = hd ** -0.5
    nblk = seq // t
    nh = SB_HEADS
    wide = nh * hd
    cols = [slice(hh * hd, (hh + 1) * hd) for hh in range(nh)]

    def body(qs, k_ref, vs, zb_ref, yb_ref, o_ref, tot_ref, kts, later, acc):
        for jb in range(nblk):
            kts[jb] = k_ref[jb * t:(jb + 1) * t, :].astype(F32).T.astype(BF16)
        row, col = _iotas(t)
        later[...] = (row[:sw, :sw] > col[:sw, :sw]).astype(BF16)

        def qblock(i, carry):
            r0 = pl.multiple_of(i * t, t)

            def tile(j, runs, valid):
                c0 = pl.multiple_of(j * t, t)
                logs = [_sb_logs(_dot(qs[pl.ds(r0, t), cs], kts[j, cs, :]), scale, valid) for cs in cols]
                scans = [_dot(jnp.concatenate([logs[hh][1][:, b * sw:(b + 1) * sw] for b in range(nb)], axis=0),
                              later[...]) for hh in range(nh)]
                new_runs = []
                for hh in range(nh):
                    after = runs[hh]
                    blocks = [None] * nb
                    for b in reversed(range(nb)):
                        ks_ = slice(b * sw, (b + 1) * sw)
                        inside = scans[hh][b * t:(b + 1) * t]
                        blocks[b] = jnp.exp(logs[hh][0][:, ks_].astype(F32) + inside + after).astype(BF16)
                        after = after + inside[:, 0:1] + logs[hh][1][:, b * sw:b * sw + 1].astype(F32)
                    new_runs.append(after)
                    pv = _dot(jnp.concatenate(blocks, axis=1), vs[pl.ds(c0, t), cols[hh]])
                    if valid is None:
                        acc[:, cols[hh]] += pv
                    else:
                        acc[:, cols[hh]] = pv
                return tuple(new_runs)

            runs = tile(i, (jnp.zeros((t, 1), F32),) * nh, col < row)
            runs = lax.fori_loop(0, i, lambda jj, rs: tile(i - 1 - jj, rs, None), runs)
            for hh in range(nh):
                out = acc[:, cols[hh]]
                o_ref[pl.ds(r0, t), cols[hh]] = out.astype(BF16)
                tot_ref[hh, pl.ds(r0, t), :] = runs[hh]
                sz, _ = _silu(zb_ref[pl.ds(r0, t), cols[hh]].astype(F32))
                yb_ref[pl.ds(r0, t), cols[hh]] = (out * sz).astype(BF16)
            return carry

        lax.fori_loop(0, nblk, qblock, 0)

    col0 = d // wide
    seg = lambda k: pl.BlockSpec((seq, wide), lambda b, h: (b, k * col0 + h))
    return pl.pallas_call(
        body, name="sb_fwd", grid=(batch, heads // nh),
        in_specs=[seg(3), seg(4), seg(5), seg(6)],
        out_specs=[pl.BlockSpec((seq, wide), lambda b, h: (b, h))] * 2 + [
            pl.BlockSpec((nh, seq, 1), lambda b, h: (b * (heads // nh) + h, 0, 0))],
        out_shape=[SDS((batch * seq, d), BF16), SDS((batch * seq, d), BF16), SDS((batch * heads, seq, 1), F32)],
        scratch_shapes=[pltpu.VMEM((nblk, wide, t), BF16), pltpu.VMEM((sw, sw), BF16), pltpu.VMEM((t, wide), F32)],
        compiler_params=_params(("parallel", "parallel")),
    )(proj, proj, proj, proj)


def _tail(x2d, tgt, ya, yb, proj, w_oa, w_ob, w_out, norm_final):
    n, d = x2d.shape
    e = proj.shape[1]
    tm = _tile(n, 256)

    steps = n // tm

    def body(x_ref, t_ref, ya_ref, yb_ref, ga_ref, gb_ref, woa_ref, wob_ref, wout_ref, gf_ref,
             dproj_ref, dx2_ref, dya_ref, dyb_ref, mrg_ref, dpa_ref, dpb_ref, loss_ref, dgf_ref, dg_s, dg_sems):
        i = pl.program_id(0)

        def gate_copy(step):
            rows_ = pl.ds(pl.multiple_of(step * tm, tm), tm)
            return pltpu.make_async_copy(dg_s.at[step % 2], dproj_ref.at[rows_, pl.ds(7 * d, 2 * d)],
                                         dg_sems.at[step % 2])

        @pl.when(i == 0)
        def _():
            loss_ref[...] = jnp.zeros_like(loss_ref)
            dgf_ref[...] = jnp.zeros_like(dgf_ref)

        @pl.when(i >= 2)
        def _():
            gate_copy(i - 2).wait()

        pa = _dot(ya_ref[...], woa_ref[...])
        pb = _dot(yb_ref[...], wob_ref[...])
        sa = _sigmoid(ga_ref[...].astype(F32))
        sb = _sigmoid(gb_ref[...].astype(F32))
        merged = (sa * pa + sb * pb).astype(BF16)
        mrg_ref[...] = merged
        x2 = x_ref[...] + _dot(merged, wout_ref[...])
        r2 = _rms_scale(x2)
        xh = x2 * r2
        gf = gf_ref[...]
        diff = xh * gf - t_ref[...]
        loss_ref[...] += jnp.sum(diff * diff, axis=0, keepdims=True) * (0.5 / d)
        dy = diff * (1.0 / d)
        dgf_ref[...] += jnp.sum(dy * xh, axis=0, keepdims=True)
        dxh = dy * gf
        dx2 = r2 * (dxh - xh * jnp.mean(dxh * xh, axis=-1, keepdims=True))
        dx2_ref[...] = dx2
        dm = _dot_nt(dx2.astype(BF16), wout_ref[...])
        dpa = (dm * sa).astype(BF16)
        dpb = (dm * sb).astype(BF16)
        dpa_ref[...] = dpa
        dpb_ref[...] = dpb
        dg_s[i % 2, :, 0:d] = (dm * pa * (sa * (1.0 - sa))).astype(BF16)
        dg_s[i % 2, :, d:2 * d] = (dm * pb * (sb * (1.0 - sb))).astype(BF16)
        gate_copy(i).start()
        dya_ref[...] = _dot_nt(dpa, woa_ref[...]).astype(BF16)
        dyb_ref[...] = _dot_nt(dpb, wob_ref[...]).astype(BF16)

        @pl.when(i == steps - 1)
        def _():
            if steps >= 2:
                gate_copy(i - 1).wait()
            gate_copy(i).wait()

    rows = lambda k=0: pl.BlockSpec((tm, d), lambda i: (i, k))
    full = pl.BlockSpec((d, d), lambda i: (0, 0))
    vec = pl.BlockSpec((1, d), lambda i: (0, 0))
    return pl.pallas_call(
        body, name="tail", grid=(steps,),
        in_specs=[rows(), rows(), rows(), rows(), rows(7), rows(8), full, full, full, vec],
        out_specs=[pl.BlockSpec(memory_space=pl.ANY),
                   rows(), rows(), rows(), rows(), rows(), rows(), vec, vec],
        out_shape=[SDS((n, e), BF16), SDS((n, d), F32), SDS((n, d), BF16), SDS((n, d), BF16),
                   SDS((n, d), BF16), SDS((n, d), BF16), SDS((n, d), BF16),
                   SDS((1, d), F32), SDS((1, d), F32)],
        scratch_shapes=[pltpu.VMEM((2, tm, 2 * d), BF16), pltpu.SemaphoreType.DMA((2,))],
        compiler_params=_params(("arbitrary",)),
    )(x2d, tgt, ya, yb, proj, proj, w_oa, w_ob, w_out, norm_final)


def _tn_matmul(a, b, name):
    n, p = a.shape
    q = b.shape[1]
    tk = _tile(n, 512)
    nk = n // tk

    def body(a_ref, b_ref, o_ref, acc):
        k = pl.program_id(0)

        @pl.when(k == 0)
        def _():
            acc[...] = jnp.zeros_like(acc)

        acc[...] += _dot_tn(a_ref[...], b_ref[...].astype(BF16))

        @pl.when(k == nk - 1)
        def _():
            o_ref[...] = acc[...].astype(BF16)

    return pl.pallas_call(
        body, name=name, grid=(nk,),
        in_specs=[pl.BlockSpec((tk, p), lambda k: (k, 0)), pl.BlockSpec((tk, q), lambda k: (k, 0))],
        out_specs=pl.BlockSpec((p, q), lambda k: (0, 0)),
        out_shape=SDS((p, q), BF16),
        scratch_shapes=[pltpu.VMEM((p, q), F32)],
        compiler_params=_params(("arbitrary",)),
    )(a, b)


def _sb_bwd(proj, o, dyb, tot, dproj, stacks, batch, seq, d, hd):
    heads = d // hd
    t = _tile(seq, SB_TILE_BWD)
    sw = _tile(t, SB_SCAN)
    nb = t // sw
    scale = hd ** -0.5
    nblk = seq // t
    nh = SB_HEADS
    wide = nh * hd
    hs = range(nh)
    cols = [slice(hh * hd, (hh + 1) * hd) for hh in hs]
    blocks = [slice(b * sw, (b + 1) * sw) for b in range(nb)]
    last = slice(sw - 1, sw)

    def compute(qs, ks, v_ref, zb_ref, dyb_ref, tot_ref, kts, vts, dos, res, upto, before, dq):
        for jb in range(nblk):
            rows = slice(jb * t, (jb + 1) * t)
            kts[jb] = ks[rows, :].astype(F32).T.astype(BF16)
            vts[jb] = v_ref[rows, :].astype(F32).T.astype(BF16)
        sz, _ = _silu(zb_ref[...].astype(F32))
        dos[...] = (dyb_ref[...].astype(F32) * sz).astype(BF16)
        res[1] = jnp.zeros((seq, wide), F32)
        res[2] = jnp.zeros((seq, wide), F32)
        row, col = _iotas(t)
        upto[...] = (row[:sw, :sw] <= col[:sw, :sw]).astype(BF16)
        before[...] = (row[:sw, :sw] < col[:sw, :sw]).astype(BF16)

        def qblock(i, carry):
            r0 = pl.multiple_of(i * t, t)

            def tile(j, sums, valid):
                c0 = pl.multiple_of(j * t, t)
                q_i = [qs[pl.ds(r0, t), cs] for cs in cols]
                do_i = [dos[pl.ds(r0, t), cs] for cs in cols]
                logs = [_sb_logs(_dot(q_i[hh], kts[j, cols[hh], :]), scale, valid) for hh in hs]
                dw = [_dot(do_i[hh], vts[j, cols[hh], :]) for hh in hs]
                scans = [_dot(jnp.concatenate([logs[hh][1][:, ks_] for ks_ in blocks], axis=0), upto[...]) for hh in hs]
                ws, gs, new_runs = [], [], []
                for hh in hs:
                    left = tot_ref[hh, pl.ds(r0, t), :] - sums[hh][0]
                    w_b, g_b = [], []
                    for b, ks_ in enumerate(blocks):
                        inside = scans[hh][b * t:(b + 1) * t]
                        w = jnp.exp(logs[hh][0][:, ks_].astype(F32) + (left - inside))
                        w_b.append(w.astype(BF16))
                        g_b.append((dw[hh][:, ks_] * w).astype(BF16))
                        left = left - inside[:, last]
                    ws.append(jnp.concatenate(w_b, axis=1))
                    gs.append(g_b)
                    new_runs.append(tot_ref[hh, pl.ds(r0, t), :] - left)
                gscans = [_dot(jnp.concatenate(gs[hh], axis=0), before[...]) for hh in hs]
                dzs, new_gruns = [], []
                for hh in hs:
                    g_before = sums[hh][1]
                    dz_b = []
                    for b, ks_ in enumerate(blocks):
                        inside = gscans[hh][b * t:(b + 1) * t]
                        beta = jnp.exp(logs[hh][0][:, ks_]).astype(F32)
                        g = gs[hh][b].astype(F32)
                        dz_b.append(((g - (g + inside + g_before) * beta) * scale).astype(BF16))
                        g_before = g_before + inside[:, last] + g[:, last]
                    dzs.append(jnp.concatenate(dz_b, axis=1))
                    new_gruns.append(g_before)
                for hh in hs:
                    res[2, pl.ds(c0, t), cols[hh]] += _dot_tn(ws[hh], do_i[hh])
                for hh in hs:
                    res[1, pl.ds(c0, t), cols[hh]] += _dot_tn(dzs[hh], q_i[hh])
                for hh in hs:
                    dq[:, cols[hh]] += _dot(dzs[hh], ks[pl.ds(c0, t), cols[hh]])
                return tuple((new_runs[hh], new_gruns[hh]) for hh in hs)

            zero = jnp.zeros((t, 1), F32)
            dq[...] = jnp.zeros_like(dq)
            sums = lax.fori_loop(0, i, lambda j, sm: tile(j, sm, None), ((zero, zero),) * nh)
            tile(i, sums, col < row)
            res[0, pl.ds(r0, t), :] = dq[...]
            return carry

        lax.fori_loop(0, nblk, qblock, 0)

    pairs = heads // nh

    ns = len(stacks)

    def body(qs, ks, v_ref, zb_ref, o_ref, dyb_ref, tot_ref, dproj_in, *refs):
        del dproj_in
        st_in, out_ref, st_out = refs[:ns], refs[ns], refs[ns + 1:2 * ns + 1]
        (kts, vts, dos, res, upto, before, dq, stage, stage_sems,
         send_sems, recv_sems, local_sems) = refs[2 * ns + 1:]
        step = pl.program_id(0) * pairs + pl.program_id(1)
        exchange = functools.partial(_stack_exchange, _me(), st_in, st_out, 1, send_sems, recv_sems, local_sems)

        @pl.when(step == 0)
        def _():
            local, remote, _ = exchange(arrivals=False)
            for cp in local + remote:
                cp.start()

        def out_copies(s):
            rows_ = pl.ds(pl.multiple_of((s // pairs) * seq, seq), seq)
            return [pltpu.make_async_copy(
                stage.at[k], out_ref.at[rows_, pl.ds(pl.multiple_of((3 + k) * d + (s % pairs) * wide, wide), wide)],
                stage_sems.at[k]) for k in range(4)]

        compute(qs, ks, v_ref, zb_ref, dyb_ref, tot_ref, kts, vts, dos, res, upto, before, dq)

        @pl.when(step > 0)
        def _():
            for cp in out_copies(step - 1):
                cp.wait()

        for k in range(3):
            stage[k] = res[k].astype(BF16)
        _, dsz = _silu(zb_ref[...].astype(F32))
        stage[3] = (dyb_ref[...].astype(F32) * o_ref[...].astype(F32) * dsz).astype(BF16)
        for cp in out_copies(step):
            cp.start()

        @pl.when(step == batch * pairs - 1)
        def _():
            for cp in out_copies(step):
                cp.wait()
            local, remote, landed = exchange()
            for cp in remote:
                cp.wait_send()
            for cp in landed:
                cp.wait_recv()
            for cp in local:
                cp.wait()

    col0 = d // wide
    seg = lambda k: pl.BlockSpec((seq, wide), lambda b, h: (b, k * col0 + h))
    head = pl.BlockSpec((seq, wide), lambda b, h: (b, h))
    any_spec = pl.BlockSpec(memory_space=pl.ANY)
    return pl.pallas_call(
        body, name="sb_bwd", grid=(batch, pairs),
        in_specs=[seg(3), seg(4), seg(5), seg(6), head, head,
                  pl.BlockSpec((nh, seq, 1), lambda b, h: (b * pairs + h, 0, 0)), any_spec] + [any_spec] * ns,
        out_specs=[any_spec] * (ns + 1),
        out_shape=[SDS(dproj.shape, dproj.dtype)] + [SDS(s.shape, s.dtype) for s in stacks[:-1]] + [
            SDS((N_DEV,) + stacks[-1].shape, stacks[-1].dtype)],
        input_output_aliases={7: 0},
        scratch_shapes=[pltpu.VMEM((nblk, wide, t), BF16)] * 2 + [
            pltpu.VMEM((seq, wide), BF16), pltpu.VMEM((3, seq, wide), F32),
            pltpu.VMEM((sw, sw), BF16), pltpu.VMEM((sw, sw), BF16), pltpu.VMEM((t, wide), F32),
            pltpu.VMEM((4, seq, wide), BF16), pltpu.SemaphoreType.DMA((4,)),
            pltpu.SemaphoreType.DMA((7 * ns,)), pltpu.SemaphoreType.DMA((7 * ns,)),
            pltpu.SemaphoreType.DMA((ns,))],
        compiler_params=_params(("arbitrary", "arbitrary")),
    )(proj, proj, proj, proj, o, dyb, tot, dproj, *stacks)


def _branch_a_bwd(proj, dya, norm_v, w_s, b_col, dproj):
    n = proj.shape[0]
    d = norm_v.shape[1]
    groups, chunk, _ = w_s.shape
    tr = _tile(n, 2 * chunk)

    def body(u_ref, v_ref, z_ref, dya_ref, gv_ref, ws_ref, b_ref, dproj_in,
             out_ref, dws_ref, dbias_ref, dgv_ref, vn_s, dmix_s, dvn_s, db_ref):
        del dproj_in

        @pl.when(pl.program_id(0) == 0)
        def _():
            dws_ref[...] = jnp.zeros_like(dws_ref)
            db_ref[...] = jnp.zeros_like(db_ref)
            dgv_ref[...] = jnp.zeros_like(dgv_ref)

        row, col = _iotas(chunk)
        tril = col <= row
        u, v, z, dya_v = (r[...].astype(F32) for r in (u_ref, v_ref, z_ref, dya_ref))
        gv = gv_ref[...]
        vg, dvg_dv = _gelu(v)
        r = _rms_scale(vg)
        vh = vg * r
        vn_s[...] = (vh * gv).astype(BF16)
        ug, dug_du = _gelu(u)
        sz, dsz = _silu(z)
        dmix_s[...] = dya_v * ug * sz
        for g in range(groups):
            wm = jnp.where(tril, ws_ref[g], 0.0).astype(BF16)
            cs = slice(g * chunk, (g + 1) * chunk)
            for c in range(tr // chunk):
                rs = slice(c * chunk, (c + 1) * chunk)
                vn = vn_s[rs, cs]
                mixed = _dot(wm, vn) + b_ref[g]
                dmix = dmix_s[rs, cs]
                dmix16 = dmix.astype(BF16)
                dws_ref[g] += _dot_nt(dmix16, vn)
                db_ref[g] += dmix
                dvn_s[rs, cs] = _dot_tn(wm, dmix16)
                t_u = dya_v[rs, cs] * mixed
                out_ref[rs, g * chunk:(g + 1) * chunk] = (t_u * sz[rs, cs] * dug_du[rs, cs]).astype(BF16)
                out_ref[rs, 2 * d + g * chunk:2 * d + (g + 1) * chunk] = (t_u * ug[rs, cs] * dsz[rs, cs]).astype(BF16)
        dvn = dvn_s[...]
        dgv_ref[...] += jnp.sum(dvn * vh, axis=0, keepdims=True)
        dvh = dvn * gv
        dvg = r * (dvh - vh * jnp.mean(dvh * vh, axis=-1, keepdims=True))
        out_ref[:, d:2 * d] = (dvg * dvg_dv).astype(BF16)

        @pl.when(pl.program_id(0) == n // tr - 1)
        def _():
            for g in range(groups):
                dbias_ref[g:g + 1, :] = jnp.sum(db_ref[g].T, axis=0, keepdims=True)

    seg = lambda k: pl.BlockSpec((tr, d), lambda i: (i, k))
    return pl.pallas_call(
        body, name="branch_a_bwd", grid=(n // tr,),
        in_specs=[seg(0), seg(1), seg(2), seg(0),
                  pl.BlockSpec((1, d), lambda i: (0, 0)),
                  pl.BlockSpec((groups, chunk, chunk), lambda i: (0, 0, 0)),
                  pl.BlockSpec((groups, chunk, 1), lambda i: (0, 0, 0)),
                  pl.BlockSpec(memory_space=pl.ANY)],
        out_specs=[pl.BlockSpec((tr, 3 * d), lambda i: (i, 0)),
                   pl.BlockSpec((groups, chunk, chunk), lambda i: (0, 0, 0)),
                   pl.BlockSpec((groups, chunk), lambda i: (0, 0)),
                   pl.BlockSpec((1, d), lambda i: (0, 0))],
        out_shape=[SDS(dproj.shape, dproj.dtype), SDS((groups, chunk, chunk), F32),
                   SDS((groups, chunk), F32), SDS((1, d), F32)],
        input_output_aliases={7: 0},
        scratch_shapes=[pltpu.VMEM((tr, d), BF16), pltpu.VMEM((tr, d), F32), pltpu.VMEM((tr, d), F32),
                        pltpu.VMEM((groups, chunk, chunk), F32)],
        compiler_params=_params(("arbitrary",)),
    )(proj, proj, proj, dya, norm_v, w_s, b_col, dproj)


def _dx(dproj, wg_in, x2d, dx2, norm_in):
    n, d = x2d.shape
    nsh = N_DEV // 2
    esh = wg_in.shape[1] // nsh
    tm = _tile(n, 1024)

    def body(dp_ref, w_ref, x_ref, dx2_ref, g_ref, gx_ref, dg_ref, acc):
        i, k = pl.program_id(0), pl.program_id(1)

        @pl.when(jnp.logical_and(i == 0, k == 0))
        def _():
            dg_ref[...] = jnp.zeros_like(dg_ref)

        @pl.when(k == 0)
        def _():
            acc[...] = jnp.zeros_like(acc)

        acc[...] += _dot_nt(dp_ref[...], w_ref[...])

        @pl.when(k == nsh - 1)
        def _():
            dh = acc[...]
            x = x_ref[...]
            r = _rms_scale(x)
            xh = x * r
            dg_ref[...] += jnp.sum(dh * xh, axis=0, keepdims=True)
            dxh = dh * g_ref[...]
            gx_ref[...] = dx2_ref[...] + r * (dxh - xh * jnp.mean(dxh * xh, axis=-1, keepdims=True))

    rows = pl.BlockSpec((tm, d), lambda i, k: (i, 0))
    vec = pl.BlockSpec((1, d), lambda i, k: (0, 0))
    return pl.pallas_call(
        body, name="dx", grid=(n // tm, nsh),
        in_specs=[pl.BlockSpec((tm, esh), lambda i, k: (i, k)),
                  pl.BlockSpec((d, esh), lambda i, k: (0, k)), rows, rows, vec],
        out_specs=[rows, vec],
        out_shape=[SDS((n, d), F32), SDS((1, d), F32)],
        scratch_shapes=[pltpu.VMEM((tm, d), F32)],
        compiler_params=_params(("arbitrary", "arbitrary")),
    )(dproj, wg_in, x2d, dx2, norm_in)


def _adamw_outputs(g_ref, d_ref, m_ref, v_ref, g, w, m, v):
    delta, m2, v2 = _adamw(w, g, m, v)
    g_ref[...] = g
    d_ref[...] = delta
    m_ref[...] = m2
    v_ref[...] = v2


def _reduce_adamw(slots, w, m, v, name):
    _, r, c = slots.shape
    tr = _tile(r, 128)

    def body(s_ref, w_ref, m_ref, v_ref, g_out, d_out, m_out, v_out):
        g = s_ref[0].astype(F32)
        for k in range(1, N_DEV):
            g = g + s_ref[k].astype(F32)
        _adamw_outputs(g_out, d_out, m_out, v_out, g, w_ref[...], m_ref[...], v_ref[...])

    blk = pl.BlockSpec((tr, c), lambda i: (i, 0))
    return pl.pallas_call(
        body, name=name, grid=(r // tr,),
        in_specs=[pl.BlockSpec((N_DEV, tr, c), lambda i: (0, i, 0)), blk, blk, blk],
        out_specs=[blk] * 4,
        out_shape=[SDS((r, c), F32)] * 4,
        compiler_params=_params(("parallel",)),
    )(slots, w, m, v)


def _adamw_small(g, w, m, v, name):
    def body(g_ref, w_ref, m_ref, v_ref, g_out, d_out, m_out, v_out):
        _adamw_outputs(g_out, d_out, m_out, v_out, g_ref[...], w_ref[...], m_ref[...], v_ref[...])

    return pl.pallas_call(
        body, name=name,
        out_shape=[SDS(g.shape, F32)] * 4,
        in_specs=[pl.BlockSpec(memory_space=pltpu.VMEM)] * 4,
        out_specs=[pl.BlockSpec(memory_space=pltpu.VMEM)] * 4,
    )(g, w, m, v)


def kernel(x, norm_in, w_in, norm_v, w_s, b_s, w_o_gmlp, w_o_sb, w_out, norm_final, loss_target, m_norm_in, m_w_in, m_norm_v, m_w_s, m_b_s, m_w_o_gmlp, m_w_o_sb, m_w_out, m_norm_final, v_norm_in, v_w_in, v_norm_v, v_w_s, v_b_s, v_w_o_gmlp, v_w_o_sb, v_w_out, v_norm_final):
    batch, seq, d = x.shape
    n = batch * seq
    groups, chunk = w_s.shape[1], w_s.shape[2]
    hd = LANE
    x2d = x.reshape(n, d)
    tgt = loss_target.reshape(n, d)
    b_col = b_s[0].reshape(groups, chunk, 1)
    norm_final2 = norm_final.reshape(1, d)

    my_slot = _slot(_me()).astype(jnp.int32).reshape(1)
    proj, h, wg_in, wg_oa, wg_ob, wg_out = _gather_in_proj(
        x2d, norm_in, w_in[0], [w_o_gmlp[0], w_o_sb[0], w_out[0]], my_slot)
    rsh = wg_oa.shape[1]
    wf_oa, wf_ob, wf_out = (w.reshape(N_DEV * rsh, d) for w in (wg_oa, wg_ob, wg_out))
    ya = _branch_a_fwd(proj, norm_v, w_s[0], b_col)
    yb, o, sb_tot = _sb_fwd(proj, batch, seq, d, hd)
    dproj, dx2, dya, dyb, merged, dpa, dpb, loss_vec, dgf = _tail(
        x2d, tgt, ya, yb, proj, wf_oa, wf_ob, wf_out, norm_final2)
    gp_oa = _tn_matmul(ya, dpa, "dw_o_gmlp")
    gp_ob = _tn_matmul(yb, dpb, "dw_o_sb")
    gp_out = _tn_matmul(merged, dx2, "dw_out")
    dproj, gp_ws, gp_b, gp_nv = _branch_a_bwd(proj, dya, norm_v, w_s[0], b_col, dproj)

    slab = lambda a: a.reshape(d // LANE, LANE)
    gc = groups * chunk
    packed = jnp.concatenate([gp_ws.reshape(gc, chunk), gp_b, slab(gp_nv), slab(dgf), slab(loss_vec)], axis=0)
    dproj, s_oa, s_ob, s_out, packs = _sb_bwd(
        proj, o, dyb, sb_tot, dproj, [g.reshape(N_DEV, rsh, d) for g in (gp_oa, gp_ob, gp_out)] + [packed],
        batch, seq, d, hd)
    grad_x, gp_nin = _dx(dproj, wg_in, x2d, dx2, norm_in)
    my_slot = _slot(_me()).astype(jnp.int32).reshape(1)
    s_win, late_packs = _dw_in_exchange(h, dproj, my_slot, slab(gp_nin))
    tot, loss_slab = _finish_small(packs, late_packs, groups, chunk)
    ns = d // LANE
    g_ws = tot[:gc]
    g_b = tot[gc:gc + groups]
    g_nv, g_nf, _, g_nin = (tot[gc + groups + k * ns:gc + groups + (k + 1) * ns] for k in range(4))
    loss = loss_slab[0, 0]

    res = {}
    res["w_in"] = _reduce_adamw(s_win, w_in[0], m_w_in[0], v_w_in[0], "adamw_w_in")
    res["w_o_gmlp"] = _reduce_adamw(s_oa, w_o_gmlp[0], m_w_o_gmlp[0], v_w_o_gmlp[0], "adamw_w_o_gmlp")
    res["w_o_sb"] = _reduce_adamw(s_ob, w_o_sb[0], m_w_o_sb[0], v_w_o_sb[0], "adamw_w_o_sb")
    res["w_out"] = _reduce_adamw(s_out, w_out[0], m_w_out[0], v_w_out[0], "adamw_w_out")
    res["norm_in"] = _adamw_small(g_nin, slab(norm_in), slab(m_norm_in), slab(v_norm_in), "adamw_norm_in")
    res["norm_v"] = _adamw_small(g_nv, slab(norm_v), slab(m_norm_v), slab(v_norm_v), "adamw_norm_v")
    res["norm_final"] = _adamw_small(g_nf, slab(norm_final), slab(m_norm_final), slab(v_norm_final), "adamw_norm_final")
    res["w_s"] = _adamw_small(g_ws, w_s.reshape(gc, chunk), m_w_s.reshape(gc, chunk), v_w_s.reshape(gc, chunk), "adamw_w_s")
    res["b_s"] = _adamw_small(g_b, b_s[0], m_b_s[0], v_b_s[0], "adamw_b_s")

    shapes = {"norm_in": norm_in.shape, "w_in": w_in.shape, "norm_v": norm_v.shape, "w_s": w_s.shape,
              "b_s": b_s.shape, "w_o_gmlp": w_o_gmlp.shape, "w_o_sb": w_o_sb.shape, "w_out": w_out.shape,
              "norm_final": norm_final.shape}
    names = list(shapes)
    outs = [loss, grad_x.reshape(batch, seq, d)]
    for kind in range(4):
        outs += [res[name][kind].reshape(shapes[name]) for name in names]
    return tuple(outs)
```

```python
import functools
import math

import jax
import jax.numpy as jnp
from jax import lax
from jax.experimental import pallas as pl
from jax.experimental.pallas import tpu as pltpu

F32 = jnp.float32
BF16 = jnp.bfloat16
SDS = jax.ShapeDtypeStruct
MESH_ID = pl.DeviceIdType.MESH

N_DEV = 8
LANE = 128
SUBLANE = 8
VMEM_LIMIT = 56 * 1024 * 1024
SB_TILE = 512
SB_TILE_BWD = 512
SB_SCAN = 256
SB_HEADS = 2
MASKED_LOG = -1e30
RMS_EPS = 1e-6

ADAM_LR = 0.001
ADAM_B1 = 0.9
ADAM_B2 = 0.999
ADAM_EPS = 1e-08
ADAM_WD = 0.01
ADAM_STEP = 10

NT_DIMS = (((1,), (1,)), ((), ()))
TN_DIMS = (((0,), (0,)), ((), ()))


def _params(semantics=None):
    return pltpu.CompilerParams(dimension_semantics=semantics, vmem_limit_bytes=VMEM_LIMIT)


def _tile(n, preferred):
    t = min(n, preferred)
    assert n % t == 0, (n, t)
    return t


def _sigmoid(x):
    return 1.0 / (1.0 + jnp.exp(-x))


def _silu(x):
    s = _sigmoid(x)
    return x * s, s * (1.0 + x * (1.0 - s))


def _gelu(x):
    k = math.sqrt(2.0 / math.pi)
    x2 = x * x
    t = jnp.tanh(k * (x + 0.044715 * (x * x2)))
    cdf = 0.5 * (1.0 + t)
    return x * cdf, cdf + 0.5 * x * (1.0 - t * t) * (k * (1.0 + 3.0 * 0.044715 * x2))


def _rms_scale(x):
    return lax.rsqrt(jnp.mean(x * x, axis=-1, keepdims=True) + RMS_EPS)


def _iotas(n):
    return (lax.broadcasted_iota(jnp.int32, (n, n), 0), lax.broadcasted_iota(jnp.int32, (n, n), 1))


def _adamw(w, g, m, v):
    m = ADAM_B1 * m + (1.0 - ADAM_B1) * g
    v = ADAM_B2 * v + (1.0 - ADAM_B2) * (g * g)
    m_hat = m / (1.0 - ADAM_B1 ** ADAM_STEP)
    v_hat = v / (1.0 - ADAM_B2 ** ADAM_STEP)
    delta = -ADAM_LR * (m_hat / (jnp.sqrt(v_hat) + ADAM_EPS) + ADAM_WD * w)
    return delta, m, v


def _dot(a, b):
    return jnp.dot(a, b, preferred_element_type=F32)


def _dot_nt(a, b):
    return lax.dot_general(a, b, NT_DIMS, preferred_element_type=F32)


def _dot_tn(a, b):
    return lax.dot_general(a, b, TN_DIMS, preferred_element_type=F32)


def _sb_logs(raw, scale, valid):
    z = (raw * scale).astype(BF16)
    log_beta = jnp.minimum(z, 0) - jnp.log(1 + jnp.exp(-jnp.abs(z)))
    log_rest = log_beta - z
    if valid is not None:
        log_beta = jnp.where(valid, log_beta, MASKED_LOG)
        log_rest = jnp.where(valid, log_rest, 0)
    return log_beta, log_rest


def _me():
    return lax.axis_index("x"), lax.axis_index("y"), lax.axis_index("c")


def _slot(p):
    return 4 * p[0] + 2 * p[1] + p[2]


def _peer(me, k):
    flips = ((k >> 2) & 1, (k >> 1) & 1, k & 1)
    return tuple(1 - a if f else a for a, f in zip(me, flips))


def _stack_exchange(me, st_in, st_out, n_whole, send_sems, recv_sems, local_sems, arrivals=True):
    mine = _slot(me)
    ns = len(st_in)
    part = lambda a, dev: st_in[a] if a >= ns - n_whole else st_in[a].at[_slot(dev)]
    local = [pltpu.make_async_copy(part(a, me), st_out[a].at[mine], local_sems.at[a]) for a in range(ns)]
    remote, landed = [], []
    for k in range(1, N_DEV):
        peer = _peer(me, k)
        for a in range(ns):
            sems = dict(send_sem=send_sems.at[7 * a + k - 1], recv_sem=recv_sems.at[7 * a + k - 1])
            remote.append(pltpu.make_async_remote_copy(
                src_ref=part(a, peer), dst_ref=st_out[a].at[mine],
                device_id=peer, device_id_type=MESH_ID, **sems))
            if arrivals:
                got = st_out[a].at[_slot(peer)]
                landed.append(pltpu.make_async_remote_copy(
                    src_ref=got, dst_ref=got, device_id=me, device_id_type=MESH_ID, **sems))
    return local, remote, landed


def _gather_in_proj(x2d, norm_in, w_in_sh, wo_shards, my_slot):
    n, d = x2d.shape
    esh = w_in_sh.shape[1]
    pw = 2 * esh
    n_chip = N_DEV // 2
    tm = _tile(n, 1024)
    n_i = n // tm
    mid = n_i // 2
    no = len(wo_shards)
    flip_at = lambda st: jnp.where(st == 1, 2, jnp.where(st == 2, 1, jnp.where(st == 3, 3, 0)))

    def body(me_ref, x_ref, g_ref, win_ref, *refs):
        del me_ref
        wo_in = refs[:no]
        proj_ref, h_ref, wg_ref = refs[no:no + 3]
        wo_out = refs[no + 3:2 * no + 3]
        wv, stage = refs[2 * no + 3:2 * no + 5]
        wo_stage = refs[2 * no + 5:3 * no + 5]
        send_sems, recv_sems, pair_sems, own_sems, wo_send, wo_recv, wo_local = refs[3 * no + 5:]
        st, i = pl.program_id(0), pl.program_id(1)
        x, y, c = _me()
        me, sibling = (x, y, c), (x, y, 1 - c)
        chips = [(1 - x, y), (x, 1 - y), (1 - x, 1 - y)]
        chip_id = lambda p: 2 * p[0] + p[1]

        def window(chip, core):
            return wv.at[chip_id(chip), :, pl.ds(pl.multiple_of(core * esh, LANE), esh)]

        def copy(k, block, to, src=None):
            dst = window(block[:2], block[2])
            return pltpu.make_async_remote_copy(
                src_ref=dst if src is None else src, dst_ref=dst,
                send_sem=send_sems.at[k], recv_sem=recv_sems.at[k], device_id=to, device_id_type=MESH_ID)

        def wo_copy(a, k, block, to, src=None):
            dst = wo_out[a].at[_slot(block)]
            return pltpu.make_async_remote_copy(
                src_ref=dst if src is None else src, dst_ref=dst,
                send_sem=wo_send.at[7 * a + k], recv_sem=wo_recv.at[7 * a + k], device_id=to, device_id_type=MESH_ID)

        def own_copy():
            return pltpu.make_async_copy(stage, window((x, y), c), own_sems.at[0])

        def wo_own_copy(a):
            return pltpu.make_async_copy(wo_stage[a], wo_out[a].at[_slot(me)], wo_local.at[a])

        def pair_copy(step):
            chip = jnp.bitwise_xor(chip_id((x, y)), flip_at(step))
            return pltpu.make_async_copy(wv.at[chip], wg_ref.at[:, pl.ds(pl.multiple_of(chip * pw, LANE), pw)],
                                         pair_sems.at[step])

        first = jnp.logical_and(st == 0, i == 0)

        @pl.when(first)
        def _():
            stage[...] = win_ref[...].astype(BF16)
            own_copy().start()
            copy(0, me, sibling, src=stage).start()
            for j in range(2):
                copy(1 + j, me, (*chips[j], c), src=stage).start()
            own_copy().wait()
            copy(0, sibling, me).wait_recv()
            pair_copy(0).start()

        for s_ in range(n_chip - 1):
            @pl.when(jnp.logical_and(st == s_, i == mid))
            def _():
                copy(1 + s_, (*chips[s_], c), me).wait_recv()
                copy(4 + s_, (*chips[s_], c), sibling).start()
                if s_ == 0:
                    copy(3, me, (*chips[2], c), src=stage).start()
                if s_ == 1:
                    for a in range(no):
                        wo_stage[a][...] = wo_in[a][...].astype(BF16)
                        wo_own_copy(a).start()
                        wo_copy(a, 0, me, sibling, src=wo_stage[a]).start()
                        for j, chip in enumerate(chips):
                            wo_copy(a, 1 + j, me, (*chip, c), src=wo_stage[a]).start()
                if s_ == 2:
                    for a in range(no):
                        for j, chip in enumerate(chips):
                            wo_copy(a, 1 + j, (*chip, c), me).wait_recv()
                            wo_copy(a, 4 + j, (*chip, c), sibling).start()

        for s_ in range(1, n_chip):
            @pl.when(jnp.logical_and(st == s_, i == 0))
            def _():
                copy(3 + s_, (*chips[s_ - 1], 1 - c), me).wait_recv()
                pair_copy(s_).start()

        xv = x_ref[...]
        h = (xv * _rms_scale(xv) * g_ref[...]).astype(BF16)

        @pl.when(st == 0)
        def _():
            h_ref[...] = h

        chip_now = jnp.bitwise_xor(chip_id((x, y)), flip_at(st))
        proj_ref[...] = _dot(h, wv[chip_now]).astype(BF16)

        @pl.when(jnp.logical_and(st == n_chip - 1, i == n_i - 1))
        def _():
            copy(0, me, sibling, src=stage).wait_send()
            for j, chip in enumerate(chips):
                copy(1 + j, me, (*chip, c), src=stage).wait_send()
                copy(4 + j, (*chip, c), sibling).wait_send()
            for s_ in range(n_chip):
                pair_copy(s_).wait()
            for a in range(no):
                wo_copy(a, 0, me, sibling, src=wo_stage[a]).wait_send()
                wo_copy(a, 0, sibling, me).wait_recv()
                for j, chip in enumerate(chips):
                    wo_copy(a, 1 + j, me, (*chip, c), src=wo_stage[a]).wait_send()
                    wo_copy(a, 4 + j, (*chip, c), sibling).wait_send()
                    wo_copy(a, 4 + j, (*chip, 1 - c), me).wait_recv()
                wo_own_copy(a).wait()

    any_spec = pl.BlockSpec(memory_space=pl.ANY)
    vmem = pl.BlockSpec(memory_space=pltpu.VMEM)
    grid_spec = pltpu.PrefetchScalarGridSpec(
        num_scalar_prefetch=1, grid=(n_chip, n_i),
        in_specs=[pl.BlockSpec((tm, d), lambda st, i, me: (i, 0)),
                  pl.BlockSpec((1, d), lambda st, i, me: (0, 0)), vmem] + [vmem] * no,
        out_specs=[pl.BlockSpec((tm, pw), lambda st, i, me: (i, jnp.bitwise_xor(me[0] // 2, flip_at(st)))),
                   pl.BlockSpec((tm, d), lambda st, i, me: (jnp.where(st == 0, i, n_i - 1), 0)),
                   any_spec] + [any_spec] * no,
        scratch_shapes=[pltpu.VMEM((n_chip, d, pw), BF16), pltpu.VMEM((d, esh), BF16)] + [
            pltpu.VMEM(s.shape, BF16) for s in wo_shards] + [
            pltpu.SemaphoreType.DMA((7,)), pltpu.SemaphoreType.DMA((7,)),
            pltpu.SemaphoreType.DMA((n_chip,)), pltpu.SemaphoreType.DMA((1,)),
            pltpu.SemaphoreType.DMA((7 * no,)), pltpu.SemaphoreType.DMA((7 * no,)),
            pltpu.SemaphoreType.DMA((no,))])
    return pl.pallas_call(
        body, name="gather_in_proj", grid_spec=grid_spec,
        out_shape=[SDS((n, n_chip * pw), BF16), SDS((n, d), BF16), SDS((d, n_chip * pw), BF16)] + [
            SDS((N_DEV,) + s.shape, BF16) for s in wo_shards],
        compiler_params=pltpu.CompilerParams(dimension_semantics=("arbitrary", "arbitrary"),
                                             vmem_limit_bytes=VMEM_LIMIT),
    )(my_slot, x2d, norm_in, w_in_sh, *wo_shards)


EXCHANGE_ORDER = ((4, 2, 5, 3, 6, 7, 1, 0), (2, 4, 3, 5, 7, 6, 1, 0))


def _owner_at(mine, j):
    k = 0
    for step in range(N_DEV - 1):
        k = jnp.where(j == step, jnp.where(mine % 2 == 0, EXCHANGE_ORDER[0][step], EXCHANGE_ORDER[1][step]), k)
    return jnp.bitwise_xor(mine, k)


def _dw_in_exchange(h, dproj, my_slot, packed):
    n, d = h.shape
    esh = dproj.shape[1] // N_DEV
    tk = _tile(n, 512)
    nk = n // tk
    last_j = N_DEV - 1
    depth = 4

    def body(me_ref, h_ref, dp_ref, pk_in, win_out, pk_out,
             acc, sendbuf, win_send, win_recv, send_sems, recv_sems, local_sems):
        del me_ref
        j, k = pl.program_id(0), pl.program_id(1)
        me = _me()
        mine = _slot(me)

        def pack_copies():
            local = pltpu.make_async_copy(pk_in, pk_out.at[mine], local_sems.at[0])
            remote = [pltpu.make_async_remote_copy(
                src_ref=pk_in, dst_ref=pk_out.at[mine], send_sem=send_sems.at[kk - 1], recv_sem=recv_sems.at[kk - 1],
                device_id=_peer(me, kk), device_id_type=MESH_ID) for kk in range(1, N_DEV)]
            return local, remote

        def shard_copy(jj):
            owner = _owner_at(mine, jj)
            return pltpu.make_async_remote_copy(
                src_ref=sendbuf.at[jj % depth], dst_ref=win_out.at[mine],
                send_sem=win_send.at[jj % depth], recv_sem=win_recv.at[mine],
                device_id=(owner // 4, (owner // 2) % 2, owner % 2), device_id_type=MESH_ID)

        def own_copy():
            return pltpu.make_async_copy(sendbuf.at[last_j % depth], win_out.at[mine], local_sems.at[1])

        @pl.when(jnp.logical_and(j == 0, k == 0))
        def _():
            local, remote = pack_copies()
            for cp in [local] + remote:
                cp.start()

        @pl.when(k == 0)
        def _():
            acc[...] = jnp.zeros_like(acc)

        acc[...] += _dot_tn(h_ref[...], dp_ref[...])

        @pl.when(k == nk - 1)
        def _():
            @pl.when(j >= depth)
            def _():
                shard_copy(j - depth).wait_send()

            sendbuf[j % depth] = acc[...].astype(BF16)

            @pl.when(j < last_j)
            def _():
                shard_copy(j).start()

            @pl.when(j == last_j)
            def _():
                own_copy().start()
                for jj in range(last_j - depth + 1, last_j):
                    shard_copy(jj).wait_send()
                own_copy().wait()
                for src in range(N_DEV):
                    @pl.when(src != mine)
                    def _():
                        landed = win_out.at[src]
                        pltpu.make_async_remote_copy(
                            src_ref=landed, dst_ref=landed, send_sem=win_send.at[0], recv_sem=win_recv.at[src],
                            device_id=me, device_id_type=MESH_ID).wait_recv()
                local, remote = pack_copies()
                for cp in remote:
                    cp.wait_send()
                for kk in range(1, N_DEV):
                    landed = pk_out.at[_slot(_peer(me, kk))]
                    pltpu.make_async_remote_copy(
                        src_ref=landed, dst_ref=landed, send_sem=send_sems.at[kk - 1], recv_sem=recv_sems.at[kk - 1],
                        device_id=me, device_id_type=MESH_ID).wait_recv()
                local.wait()

    any_spec = pl.BlockSpec(memory_space=pl.ANY)
    grid_spec = pltpu.PrefetchScalarGridSpec(
        num_scalar_prefetch=1, grid=(N_DEV, nk),
        in_specs=[pl.BlockSpec((tk, d), lambda j, k, me: (k, 0)),
                  pl.BlockSpec((tk, esh), lambda j, k, me: (k, _owner_at(me[0], j))), any_spec],
        out_specs=[any_spec] * 2,
        scratch_shapes=[pltpu.VMEM((d, esh), F32), pltpu.VMEM((depth, d, esh), BF16),
                        pltpu.SemaphoreType.DMA((depth,)), pltpu.SemaphoreType.DMA((N_DEV,)),
                        pltpu.SemaphoreType.DMA((N_DEV - 1,)), pltpu.SemaphoreType.DMA((N_DEV - 1,)),
                        pltpu.SemaphoreType.DMA((2,))])
    return pl.pallas_call(
        body, name="dw_in_exchange", grid_spec=grid_spec,
        out_shape=[SDS((N_DEV, d, esh), BF16), SDS((N_DEV,) + packed.shape, packed.dtype)],
        compiler_params=_params(("arbitrary", "arbitrary")),
    )(my_slot, h, dproj, packed)


def _finish_small(packs, late_packs, groups, chunk):
    rows = packs.shape[1]
    late = late_packs.shape[1]
    gc = groups * chunk

    def body(p_ref, l_ref, sum_ref, loss_ref):
        row, col = _iotas(chunk)
        tril = col <= row
        for g in range(groups):
            rs = slice(g * chunk, (g + 1) * chunk)
            tot = p_ref[0, rs, :]
            for dev in range(1, N_DEV):
                tot = tot + p_ref[dev, rs, :]
            sum_ref[rs, :] = jnp.where(tril, tot, 0.0)
        rs = slice(gc, rows)
        tot = p_ref[0, rs, :]
        for dev in range(1, N_DEV):
            tot = tot + p_ref[dev, rs, :]
        sum_ref[rs, :] = tot
        loss_ref[...] = jnp.full((SUBLANE, LANE), jnp.sum(tot[rows - gc - SUBLANE:, :]), F32)
        tot = l_ref[0]
        for dev in range(1, N_DEV):
            tot = tot + l_ref[dev]
        sum_ref[rows:rows + late, :] = tot

    return pl.pallas_call(
        body, name="finish_small",
        out_shape=[SDS((rows + late, LANE), F32), SDS((SUBLANE, LANE), F32)],
        in_specs=[pl.BlockSpec(memory_space=pltpu.VMEM)] * 2,
        out_specs=[pl.BlockSpec(memory_space=pltpu.VMEM)] * 2,
        compiler_params=pltpu.CompilerParams(vmem_limit_bytes=VMEM_LIMIT),
    )(packs, late_packs)


def _branch_a_fwd(proj, norm_v, w_s, b_col):
    n = proj.shape[0]
    d = norm_v.shape[1]
    groups, chunk, _ = w_s.shape
    tr = _tile(n, 4 * chunk)

    def body(u_ref, v_ref, z_ref, gv_ref, ws_ref, b_ref, ya_ref, vn_s, pre_s):
        row, col = _iotas(chunk)
        tril = col <= row
        vg, _ = _gelu(v_ref[...].astype(F32))
        vn_s[...] = (vg * _rms_scale(vg) * gv_ref[...]).astype(BF16)
        ug, _ = _gelu(u_ref[...].astype(F32))
        sz, _ = _silu(z_ref[...].astype(F32))
        pre_s[...] = ug * sz
        for g in range(groups):
            wm = jnp.where(tril, ws_ref[g], 0.0).astype(BF16)
            cs = slice(g * chunk, (g + 1) * chunk)
            for c in range(tr // chunk):
                rs = slice(c * chunk, (c + 1) * chunk)
                mixed = _dot(wm, vn_s[rs, cs]) + b_ref[g]
                ya_ref[rs, cs] = (pre_s[rs, cs] * mixed).astype(BF16)

    seg = lambda k: pl.BlockSpec((tr, d), lambda i: (i, k))
    return pl.pallas_call(
        body, name="branch_a_fwd", grid=(n // tr,),
        in_specs=[seg(0), seg(1), seg(2),
                  pl.BlockSpec((1, d), lambda i: (0, 0)),
                  pl.BlockSpec((groups, chunk, chunk), lambda i: (0, 0, 0)),
                  pl.BlockSpec((groups, chunk, 1), lambda i: (0, 0, 0))],
        out_specs=pl.BlockSpec((tr, d), lambda i: (i, 0)),
        out_shape=SDS((n, d), BF16),
        scratch_shapes=[pltpu.VMEM((tr, d), BF16), pltpu.VMEM((tr, d), F32)],
        compiler_params=_params(("parallel",)),
    )(proj, proj, proj, norm_v, w_s, b_col)


def _sb_fwd(proj, batch, seq, d, hd):
    heads = d // hd
    t = _tile(seq, SB_TILE)
    sw = _tile(t, SB_SCAN)
    nb = t // sw
    scale = hd ** -0.5
    nblk = seq // t
    nh = SB_HEADS
    wide = nh * hd
    cols = [slice(hh * hd, (hh + 1) * hd) for hh in range(nh)]

    def body(qs, k_ref, vs, zb_ref, yb_ref, o_ref, tot_ref, kts, later, acc):
        for jb in range(nblk):
            kts[jb] = k_ref[jb * t:(jb + 1) * t, :].astype(F32).T.astype(BF16)
        row, col = _iotas(t)
        later[...] = (row[:sw, :sw] > col[:sw, :sw]).astype(BF16)

        def qblock(i, carry):
            r0 = pl.multiple_of(i * t, t)

            def tile(j, runs, valid):
                c0 = pl.multiple_of(j * t, t)
                logs = [_sb_logs(_dot(qs[pl.ds(r0, t), cs], kts[j, cs, :]), scale, valid) for cs in cols]
                scans = [_dot(jnp.concatenate([logs[hh][1][:, b * sw:(b + 1) * sw] for b in range(nb)], axis=0),
                              later[...]) for hh in range(nh)]
                new_runs = []
                for hh in range(nh):
                    after = runs[hh]
                    blocks = [None] * nb
                    for b in reversed(range(nb)):
                        ks_ = slice(b * sw, (b + 1) * sw)
                        inside = scans[hh][b * t:(b + 1) * t]
                        blocks[b] = jnp.exp(logs[hh][0][:, ks_].astype(F32) + inside + after).astype(BF16)
                        after = after + inside[:, 0:1] + logs[hh][1][:, b * sw:b * sw + 1].astype(F32)
                    new_runs.append(after)
                    pv = _dot(jnp.concatenate(blocks, axis=1), vs[pl.ds(c0, t), cols[hh]])
                    if valid is None:
                        acc[:, cols[hh]] += pv
                    else:
                        acc[:, cols[hh]] = pv
                return tuple(new_runs)

            runs = tile(i, (jnp.zeros((t, 1), F32),) * nh, col < row)
            runs = lax.fori_loop(0, i, lambda jj, rs: tile(i - 1 - jj, rs, None), runs)
            for hh in range(nh):
                out = acc[:, cols[hh]]
                o_ref[pl.ds(r0, t), cols[hh]] = out.astype(BF16)
                tot_ref[hh, pl.ds(r0, t), :] = runs[hh]
                sz, _ = _silu(zb_ref[pl.ds(r0, t), cols[hh]].astype(F32))
                yb_ref[pl.ds(r0, t), cols[hh]] = (out * sz).astype(BF16)
            return carry

        lax.fori_loop(0, nblk, qblock, 0)

    col0 = d // wide
    seg = lambda k: pl.BlockSpec((seq, wide), lambda b, h: (b, k * col0 + h))
    return pl.pallas_call(
        body, name="sb_fwd", grid=(batch, heads // nh),
        in_specs=[seg(3), seg(4), seg(5), seg(6)],
        out_specs=[pl.BlockSpec((seq, wide), lambda b, h: (b, h))] * 2 + [
            pl.BlockSpec((nh, seq, 1), lambda b, h: (b * (heads // nh) + h, 0, 0))],
        out_shape=[SDS((batch * seq, d), BF16), SDS((batch * seq, d), BF16), SDS((batch * heads, seq, 1), F32)],
        scratch_shapes=[pltpu.VMEM((nblk, wide, t), BF16), pltpu.VMEM((sw, sw), BF16), pltpu.VMEM((t, wide), F32)],
        compiler_params=_params(("parallel", "parallel")),
    )(proj, proj, proj, proj)


def _tail(x2d, tgt, ya, yb, proj, w_oa, w_ob, w_out, norm_final):
    n, d = x2d.shape
    e = proj.shape[1]
    tm = _tile(n, 256)

    steps = n // tm

    def body(x_ref, t_ref, ya_ref, yb_ref, ga_ref, gb_ref, woa_ref, wob_ref, wout_ref, gf_ref,
             dproj_ref, dx2_ref, dya_ref, dyb_ref, mrg_ref, dpa_ref, dpb_ref, loss_ref, dgf_ref, dg_s, dg_sems):
        i = pl.program_id(0)

        def gate_copy(step):
            rows_ = pl.ds(pl.multiple_of(step * tm, tm), tm)
            return pltpu.make_async_copy(dg_s.at[step % 2], dproj_ref.at[rows_, pl.ds(7 * d, 2 * d)],
                                         dg_sems.at[step % 2])

        @pl.when(i == 0)
        def _():
            loss_ref[...] = jnp.zeros_like(loss_ref)
            dgf_ref[...] = jnp.zeros_like(dgf_ref)

        @pl.when(i >= 2)
        def _():
            gate_copy(i - 2).wait()

        pa = _dot(ya_ref[...], woa_ref[...])
        pb = _dot(yb_ref[...], wob_ref[...])
        sa = _sigmoid(ga_ref[...].astype(F32))
        sb = _sigmoid(gb_ref[...].astype(F32))
        merged = (sa * pa + sb * pb).astype(BF16)
        mrg_ref[...] = merged
        x2 = x_ref[...] + _dot(merged, wout_ref[...])
        r2 = _rms_scale(x2)
        xh = x2 * r2
        gf = gf_ref[...]
        diff = xh * gf - t_ref[...]
        loss_ref[...] += jnp.sum(diff * diff, axis=0, keepdims=True) * (0.5 / d)
        dy = diff * (1.0 / d)
        dgf_ref[...] += jnp.sum(dy * xh, axis=0, keepdims=True)
        dxh = dy * gf
        dx2 = r2 * (dxh - xh * jnp.mean(dxh * xh, axis=-1, keepdims=True))
        dx2_ref[...] = dx2
        dm = _dot_nt(dx2.astype(BF16), wout_ref[...])
        dpa = (dm * sa).astype(BF16)
        dpb = (dm * sb).astype(BF16)
        dpa_ref[...] = dpa
        dpb_ref[...] = dpb
        dg_s[i % 2, :, 0:d] = (dm * pa * (sa * (1.0 - sa))).astype(BF16)
        dg_s[i % 2, :, d:2 * d] = (dm * pb * (sb * (1.0 - sb))).astype(BF16)
        gate_copy(i).start()
        dya_ref[...] = _dot_nt(dpa, woa_ref[...]).astype(BF16)
        dyb_ref[...] = _dot_nt(dpb, wob_ref[...]).astype(BF16)

        @pl.when(i == steps - 1)
        def _():
            if steps >= 2:
                gate_copy(i - 1).wait()
            gate_copy(i).wait()

    rows = lambda k=0: pl.BlockSpec((tm, d), lambda i: (i, k))
    full = pl.BlockSpec((d, d), lambda i: (0, 0))
    vec = pl.BlockSpec((1, d), lambda i: (0, 0))
    return pl.pallas_call(
        body, name="tail", grid=(steps,),
        in_specs=[rows(), rows(), rows(), rows(), rows(7), rows(8), full, full, full, vec],
        out_specs=[pl.BlockSpec(memory_space=pl.ANY),
                   rows(), rows(), rows(), rows(), rows(), rows(), vec, vec],
        out_shape=[SDS((n, e), BF16), SDS((n, d), F32), SDS((n, d), BF16), SDS((n, d), BF16),
                   SDS((n, d), BF16), SDS((n, d), BF16), SDS((n, d), BF16),
                   SDS((1, d), F32), SDS((1, d), F32)],
        scratch_shapes=[pltpu.VMEM((2, tm, 2 * d), BF16), pltpu.SemaphoreType.DMA((2,))],
        compiler_params=_params(("arbitrary",)),
    )(x2d, tgt, ya, yb, proj, proj, w_oa, w_ob, w_out, norm_final)


def _tn_matmul(a, b, name):
    n, p = a.shape
    q = b.shape[1]
    tk = _tile(n, 512)
    nk = n // tk

    def body(a_ref, b_ref, o_ref, acc):
        k = pl.program_id(0)

        @pl.when(k == 0)
        def _():
            acc[...] = jnp.zeros_like(acc)

        acc[...] += _dot_tn(a_ref[...], b_ref[...].astype(BF16))

        @pl.when(k == nk - 1)
        def _():
            o_ref[...] = acc[...].astype(BF16)

    return pl.pallas_call(
        body, name=name, grid=(nk,),
        in_specs=[pl.BlockSpec((tk, p), lambda k: (k, 0)), pl.BlockSpec((tk, q), lambda k: (k, 0))],
        out_specs=pl.BlockSpec((p, q), lambda k: (0, 0)),
        out_shape=SDS((p, q), BF16),
        scratch_shapes=[pltpu.VMEM((p, q), F32)],
        compiler_params=_params(("arbitrary",)),
    )(a, b)


def _sb_bwd(proj, o, dyb, tot, dproj, stacks, batch, seq, d, hd):
    heads = d // hd
    t = _tile(seq, SB_TILE_BWD)
    sw = _tile(t, SB_SCAN)
    nb = t // sw
    scale = hd ** -0.5
    nblk = seq // t
    nh = SB_HEADS
    wide = nh * hd
    hs = range(nh)
    cols = [slice(hh * hd, (hh + 1) * hd) for hh in hs]
    blocks = [slice(b * sw, (b + 1) * sw) for b in range(nb)]
    last = slice(sw - 1, sw)

    def compute(qs, ks, v_ref, zb_ref, dyb_ref, tot_ref, kts, vts, dos, res, upto, before, dq):
        for jb in range(nblk):
            rows = slice(jb * t, (jb + 1) * t)
            kts[jb] = ks[rows, :].astype(F32).T.astype(BF16)
            vts[jb] = v_ref[rows, :].astype(F32).T.astype(BF16)
        sz, _ = _silu(zb_ref[...].astype(F32))
        dos[...] = (dyb_ref[...].astype(F32) * sz).astype(BF16)
        res[1] = jnp.zeros((seq, wide), F32)
        res[2] = jnp.zeros((seq, wide), F32)
        row, col = _iotas(t)
        upto[...] = (row[:sw, :sw] <= col[:sw, :sw]).astype(BF16)
        before[...] = (row[:sw, :sw] < col[:sw, :sw]).astype(BF16)

        def qblock(i, carry):
            r0 = pl.multiple_of(i * t, t)

            def tile(j, sums, valid):
                c0 = pl.multiple_of(j * t, t)
                q_i = [qs[pl.ds(r0, t), cs] for cs in cols]
                do_i = [dos[pl.ds(r0, t), cs] for cs in cols]
                logs = [_sb_logs(_dot(q_i[hh], kts[j, cols[hh], :]), scale, valid) for hh in hs]
                dw = [_dot(do_i[hh], vts[j, cols[hh], :]) for hh in hs]
                scans = [_dot(jnp.concatenate([logs[hh][1][:, ks_] for ks_ in blocks], axis=0), upto[...]) for hh in hs]
                ws, gs, new_runs = [], [], []
                for hh in hs:
                    left = tot_ref[hh, pl.ds(r0, t), :] - sums[hh][0]
                    w_b, g_b = [], []
                    for b, ks_ in enumerate(blocks):
                        inside = scans[hh][b * t:(b + 1) * t]
                        w = jnp.exp(logs[hh][0][:, ks_].astype(F32) + (left - inside))
                        w_b.append(w.astype(BF16))
                        g_b.append((dw[hh][:, ks_] * w).astype(BF16))
                        left = left - inside[:, last]
                    ws.append(jnp.concatenate(w_b, axis=1))
                    gs.append(g_b)
                    new_runs.append(tot_ref[hh, pl.ds(r0, t), :] - left)
                gscans = [_dot(jnp.concatenate(gs[hh], axis=0), before[...]) for hh in hs]
                dzs, new_gruns = [], []
                for hh in hs:
                    g_before = sums[hh][1]
                    dz_b = []
                    for b, ks_ in enumerate(blocks):
                        inside = gscans[hh][b * t:(b + 1) * t]
                        beta = jnp.exp(logs[hh][0][:, ks_]).astype(F32)
                        g = gs[hh][b].astype(F32)
                        dz_b.append(((g - (g + inside + g_before) * beta) * scale).astype(BF16))
                        g_before = g_before + inside[:, last] + g[:, last]
                    dzs.append(jnp.concatenate(dz_b, axis=1))
                    new_gruns.append(g_before)
                for hh in hs:
                    res[2, pl.ds(c0, t), cols[hh]] += _dot_tn(ws[hh], do_i[hh])
                for hh in hs:
                    res[1, pl.ds(c0, t), cols[hh]] += _dot_tn(dzs[hh], q_i[hh])
                for hh in hs:
                    dq[:, cols[hh]] += _dot(dzs[hh], ks[pl.ds(c0, t), cols[hh]])
                return tuple((new_runs[hh], new_gruns[hh]) for hh in hs)

            zero = jnp.zeros((t, 1), F32)
            dq[...] = jnp.zeros_like(dq)
            sums = lax.fori_loop(0, i, lambda j, sm: tile(j, sm, None), ((zero, zero),) * nh)
            tile(i, sums, col < row)
            res[0, pl.ds(r0, t), :] = dq[...]
            return carry

        lax.fori_loop(0, nblk, qblock, 0)

    pairs = heads // nh

    ns = len(stacks)

    def body(qs, ks, v_ref, zb_ref, o_ref, dyb_ref, tot_ref, dproj_in, *refs):
        del dproj_in
        st_in, out_ref, st_out = refs[:ns], refs[ns], refs[ns + 1:2 * ns + 1]
        (kts, vts, dos, res, upto, before, dq, stage, stage_sems,
         send_sems, recv_sems, local_sems) = refs[2 * ns + 1:]
        step = pl.program_id(0) * pairs + pl.program_id(1)
        exchange = functools.partial(_stack_exchange, _me(), st_in, st_out, 1, send_sems, recv_sems, local_sems)

        @pl.when(step == 0)
        def _():
            local, remote, _ = exchange(arrivals=False)
            for cp in local + remote:
                cp.start()

        def out_copies(s):
            rows_ = pl.ds(pl.multiple_of((s // pairs) * seq, seq), seq)
            return [pltpu.make_async_copy(
                stage.at[k], out_ref.at[rows_, pl.ds(pl.multiple_of((3 + k) * d + (s % pairs) * wide, wide), wide)],
                stage_sems.at[k]) for k in range(4)]

        compute(qs, ks, v_ref, zb_ref, dyb_ref, tot_ref, kts, vts, dos, res, upto, before, dq)

        @pl.when(step > 0)
        def _():
            for cp in out_copies(step - 1):
                cp.wait()

        for k in range(3):
            stage[k] = res[k].astype(BF16)
        _, dsz = _silu(zb_ref[...].astype(F32))
        stage[3] = (dyb_ref[...].astype(F32) * o_ref[...].astype(F32) * dsz).astype(BF16)
        for cp in out_copies(step):
            cp.start()

        @pl.when(step == batch * pairs - 1)
        def _():
            for cp in out_copies(step):
                cp.wait()
            local, remote, landed = exchange()
            for cp in remote:
                cp.wait_send()
            for cp in landed:
                cp.wait_recv()
            for cp in local:
                cp.wait()

    col0 = d // wide
    seg = lambda k: pl.BlockSpec((seq, wide), lambda b, h: (b, k * col0 + h))
    head = pl.BlockSpec((seq, wide), lambda b, h: (b, h))
    any_spec = pl.BlockSpec(memory_space=pl.ANY)
    return pl.pallas_call(
        body, name="sb_bwd", grid=(batch, pairs),
        in_specs=[seg(3), seg(4), seg(5), seg(6), head, head,
                  pl.BlockSpec((nh, seq, 1), lambda b, h: (b * pairs + h, 0, 0)), any_spec] + [any_spec] * ns,
        out_specs=[any_spec] * (ns + 1),
        out_shape=[SDS(dproj.shape, dproj.dtype)] + [SDS(s.shape, s.dtype) for s in stacks[:-1]] + [
            SDS((N_DEV,) + stacks[-1].shape, stacks[-1].dtype)],
        input_output_aliases={7: 0},
        scratch_shapes=[pltpu.VMEM((nblk, wide, t), BF16)] * 2 + [
            pltpu.VMEM((seq, wide), BF16), pltpu.VMEM((3, seq, wide), F32),
            pltpu.VMEM((sw, sw), BF16), pltpu.VMEM((sw, sw), BF16), pltpu.VMEM((t, wide), F32),
            pltpu.VMEM((4, seq, wide), BF16), pltpu.SemaphoreType.DMA((4,)),
            pltpu.SemaphoreType.DMA((7 * ns,)), pltpu.SemaphoreType.DMA((7 * ns,)),
            pltpu.SemaphoreType.DMA((ns,))],
        compiler_params=_params(("arbitrary", "arbitrary")),
    )(proj, proj, proj, proj, o, dyb, tot, dproj, *stacks)


def _branch_a_bwd(proj, dya, norm_v, w_s, b_col, dproj):
    n = proj.shape[0]
    d = norm_v.shape[1]
    groups, chunk, _ = w_s.shape
    tr = _tile(n, 2 * chunk)

    def body(u_ref, v_ref, z_ref, dya_ref, gv_ref, ws_ref, b_ref, dproj_in,
             out_ref, dws_ref, dbias_ref, dgv_ref, vn_s, dmix_s, dvn_s, db_ref):
        del dproj_in

        @pl.when(pl.program_id(0) == 0)
        def _():
            dws_ref[...] = jnp.zeros_like(dws_ref)
            db_ref[...] = jnp.zeros_like(db_ref)
            dgv_ref[...] = jnp.zeros_like(dgv_ref)

        row, col = _iotas(chunk)
        tril = col <= row
        u, v, z, dya_v = (r[...].astype(F32) for r in (u_ref, v_ref, z_ref, dya_ref))
        gv = gv_ref[...]
        vg, dvg_dv = _gelu(v)
        r = _rms_scale(vg)
        vh = vg * r
        vn_s[...] = (vh * gv).astype(BF16)
        ug, dug_du = _gelu(u)
        sz, dsz = _silu(z)
        dmix_s[...] = dya_v * ug * sz
        for g in range(groups):
            wm = jnp.where(tril, ws_ref[g], 0.0).astype(BF16)
            cs = slice(g * chunk, (g + 1) * chunk)
            for c in range(tr // chunk):
                rs = slice(c * chunk, (c + 1) * chunk)
                vn = vn_s[rs, cs]
                mixed = _dot(wm, vn) + b_ref[g]
                dmix = dmix_s[rs, cs]
                dmix16 = dmix.astype(BF16)
                dws_ref[g] += _dot_nt(dmix16, vn)
                db_ref[g] += dmix
                dvn_s[rs, cs] = _dot_tn(wm, dmix16)
                t_u = dya_v[rs, cs] * mixed
                out_ref[rs, g * chunk:(g + 1) * chunk] = (t_u * sz[rs, cs] * dug_du[rs, cs]).astype(BF16)
                out_ref[rs, 2 * d + g * chunk:2 * d + (g + 1) * chunk] = (t_u * ug[rs, cs] * dsz[rs, cs]).astype(BF16)
        dvn = dvn_s[...]
        dgv_ref[...] += jnp.sum(dvn * vh, axis=0, keepdims=True)
        dvh = dvn * gv
        dvg = r * (dvh - vh * jnp.mean(dvh * vh, axis=-1, keepdims=True))
        out_ref[:, d:2 * d] = (dvg * dvg_dv).astype(BF16)

        @pl.when(pl.program_id(0) == n // tr - 1)
        def _():
            for g in range(groups):
                dbias_ref[g:g + 1, :] = jnp.sum(db_ref[g].T, axis=0, keepdims=True)

    seg = lambda k: pl.BlockSpec((tr, d), lambda i: (i, k))
    return pl.pallas_call(
        body, name="branch_a_bwd", grid=(n // tr,),
        in_specs=[seg(0), seg(1), seg(2), seg(0),
                  pl.BlockSpec((1, d), lambda i: (0, 0)),
                  pl.BlockSpec((groups, chunk, chunk), lambda i: (0, 0, 0)),
                  pl.BlockSpec((groups, chunk, 1), lambda i: (0, 0, 0)),
                  pl.BlockSpec(memory_space=pl.ANY)],
        out_specs=[pl.BlockSpec((tr, 3 * d), lambda i: (i, 0)),
                   pl.BlockSpec((groups, chunk, chunk), lambda i: (0, 0, 0)),
                   pl.BlockSpec((groups, chunk), lambda i: (0, 0)),
                   pl.BlockSpec((1, d), lambda i: (0, 0))],
        out_shape=[SDS(dproj.shape, dproj.dtype), SDS((groups, chunk, chunk), F32),
                   SDS((groups, chunk), F32), SDS((1, d), F32)],
        input_output_aliases={7: 0},
        scratch_shapes=[pltpu.VMEM((tr, d), BF16), pltpu.VMEM((tr, d), F32), pltpu.VMEM((tr, d), F32),
                        pltpu.VMEM((groups, chunk, chunk), F32)],
        compiler_params=_params(("arbitrary",)),
    )(proj, proj, proj, dya, norm_v, w_s, b_col, dproj)


def _dx(dproj, wg_in, x2d, dx2, norm_in):
    n, d = x2d.shape
    nsh = N_DEV // 2
    esh = wg_in.shape[1] // nsh
    tm = _tile(n, 1024)

    def body(dp_ref, w_ref, x_ref, dx2_ref, g_ref, gx_ref, dg_ref, acc):
        i, k = pl.program_id(0), pl.program_id(1)

        @pl.when(jnp.logical_and(i == 0, k == 0))
        def _():
            dg_ref[...] = jnp.zeros_like(dg_ref)

        @pl.when(k == 0)
        def _():
            acc[...] = jnp.zeros_like(acc)

        acc[...] += _dot_nt(dp_ref[...], w_ref[...])

        @pl.when(k == nsh - 1)
        def _():
            dh = acc[...]
            x = x_ref[...]
            r = _rms_scale(x)
            xh = x * r
            dg_ref[...] += jnp.sum(dh * xh, axis=0, keepdims=True)
            dxh = dh * g_ref[...]
            gx_ref[...] = dx2_ref[...] + r * (dxh - xh * jnp.mean(dxh * xh, axis=-1, keepdims=True))

    rows = pl.BlockSpec((tm, d), lambda i, k: (i, 0))
    vec = pl.BlockSpec((1, d), lambda i, k: (0, 0))
    return pl.pallas_call(
        body, name="dx", grid=(n // tm, nsh),
        in_specs=[pl.BlockSpec((tm, esh), lambda i, k: (i, k)),
                  pl.BlockSpec((d, esh), lambda i, k: (0, k)), rows, rows, vec],
        out_specs=[rows, vec],
        out_shape=[SDS((n, d), F32), SDS((1, d), F32)],
        scratch_shapes=[pltpu.VMEM((tm, d), F32)],
        compiler_params=_params(("arbitrary", "arbitrary")),
    )(dproj, wg_in, x2d, dx2, norm_in)


def _adamw_outputs(g_ref, d_ref, m_ref, v_ref, g, w, m, v):
    delta, m2, v2 = _adamw(w, g, m, v)
    g_ref[...] = g
    d_ref[...] = delta
    m_ref[...] = m2
    v_ref[...] = v2


def _reduce_adamw(slots, w, m, v, name):
    _, r, c = slots.shape
    tr = _tile(r, 128)

    def body(s_ref, w_ref, m_ref, v_ref, g_out, d_out, m_out, v_out):
        g = s_ref[0].astype(F32)
        for k in range(1, N_DEV):
            g = g + s_ref[k].astype(F32)
        _adamw_outputs(g_out, d_out, m_out, v_out, g, w_ref[...], m_ref[...], v_ref[...])

    blk = pl.BlockSpec((tr, c), lambda i: (i, 0))
    return pl.pallas_call(
        body, name=name, grid=(r // tr,),
        in_specs=[pl.BlockSpec((N_DEV, tr, c), lambda i: (0, i, 0)), blk, blk, blk],
        out_specs=[blk] * 4,
        out_shape=[SDS((r, c), F32)] * 4,
        compiler_params=_params(("parallel",)),
    )(slots, w, m, v)


def _adamw_small(g, w, m, v, name):
    def body(g_ref, w_ref, m_ref, v_ref, g_out, d_out, m_out, v_out):
        _adamw_outputs(g_out, d_out, m_out, v_out, g_ref[...], w_ref[...], m_ref[...], v_ref[...])

    return pl.pallas_call(
        body, name=name,
        out_shape=[SDS(g.shape, F32)] * 4,
        in_specs=[pl.BlockSpec(memory_space=pltpu.VMEM)] * 4,
        out_specs=[pl.BlockSpec(memory_space=pltpu.VMEM)] * 4,
    )(g, w, m, v)


def kernel(x, norm_in, w_in, norm_v, w_s, b_s, w_o_gmlp, w_o_sb, w_out, norm_final, loss_target, m_norm_in, m_w_in, m_norm_v, m_w_s, m_b_s, m_w_o_gmlp, m_w_o_sb, m_w_out, m_norm_final, v_norm_in, v_w_in, v_norm_v, v_w_s, v_b_s, v_w_o_gmlp, v_w_o_sb, v_w_out, v_norm_final):
    batch, seq, d = x.shape
    n = batch * seq
    groups, chunk = w_s.shape[1], w_s.shape[2]
    hd = LANE
    x2d = x.reshape(n, d)
    tgt = loss_target.reshape(n, d)
    b_col = b_s[0].reshape(groups, chunk, 1)
    norm_final2 = norm_final.reshape(1, d)

    my_slot = _slot(_me()).astype(jnp.int32).reshape(1)
    proj, h, wg_in, wg_oa, wg_ob, wg_out = _gather_in_proj(
        x2d, norm_in, w_in[0], [w_o_gmlp[0], w_o_sb[0], w_out[0]], my_slot)
    rsh = wg_oa.shape[1]
    wf_oa, wf_ob, wf_out = (w.reshape(N_DEV * rsh, d) for w in (wg_oa, wg_ob, wg_out))
    ya = _branch_a_fwd(proj, norm_v, w_s[0], b_col)
    yb, o, sb_tot = _sb_fwd(proj, batch, seq, d, hd)
    dproj, dx2, dya, dyb, merged, dpa, dpb, loss_vec, dgf = _tail(
        x2d, tgt, ya, yb, proj, wf_oa, wf_ob, wf_out, norm_final2)
    gp_oa = _tn_matmul(ya, dpa, "dw_o_gmlp")
    gp_ob = _tn_matmul(yb, dpb, "dw_o_sb")
    gp_out = _tn_matmul(merged, dx2, "dw_out")
    dproj, gp_ws, gp_b, gp_nv = _branch_a_bwd(proj, dya, norm_v, w_s[0], b_col, dproj)

    slab = lambda a: a.reshape(d // LANE, LANE)
    gc = groups * chunk
    packed = jnp.concatenate([gp_ws.reshape(gc, chunk), gp_b, slab(gp_nv), slab(dgf), slab(loss_vec)], axis=0)
    dproj, s_oa, s_ob, s_out, packs = _sb_bwd(
        proj, o, dyb, sb_tot, dproj, [g.reshape(N_DEV, rsh, d) for g in (gp_oa, gp_ob, gp_out)] + [packed],
        batch, seq, d, hd)
    grad_x, gp_nin = _dx(dproj, wg_in, x2d, dx2, norm_in)
    s_win, late_packs = _dw_in_exchange(h, dproj, my_slot, slab(gp_nin))
    tot, loss_slab = _finish_small(packs, late_packs, groups, chunk)
    ns = d // LANE
    g_ws = tot[:gc]
    g_b = tot[gc:gc + groups]
    g_nv, g_nf, _, g_nin = (tot[gc + groups + k * ns:gc + groups + (k + 1) * ns] for k in range(4))
    loss = loss_slab[0, 0]

    res = {}
    res["w_in"] = _reduce_adamw(s_win, w_in[0], m_w_in[0], v_w_in[0], "adamw_w_in")
    res["w_o_gmlp"] = _reduce_adamw(s_oa, w_o_gmlp[0], m_w_o_gmlp[0], v_w_o_gmlp[0], "adamw_w_o_gmlp")
    res["w_o_sb"] = _reduce_adamw(s_ob, w_o_sb[0], m_w_o_sb[0], v_w_o_sb[0], "adamw_w_o_sb")
    res["w_out"] = _reduce_adamw(s_out, w_out[0], m_w_out[0], v_w_out[0], "adamw_w_out")
    res["norm_in"] = _adamw_small(g_nin, slab(norm_in), slab(m_norm_in), slab(v_norm_in), "adamw_norm_in")
    res["norm_v"] = _adamw_small(g_nv, slab(norm_v), slab(m_norm_v), slab(v_norm_v), "adamw_norm_v")
    res["norm_final"] = _adamw_small(g_nf, slab(norm_final), slab(m_norm_final), slab(v_norm_final), "adamw_norm_final")
    res["w_s"] = _adamw_small(g_ws, w_s.reshape(gc, chunk), m_w_s.reshape(gc, chunk), v_w_s.reshape(gc, chunk), "adamw_w_s")
    res["b_s"] = _adamw_small(g_b, b_s[0], m_b_s[0], v_b_s[0], "adamw_b_s")

    shapes = {"norm_in": norm_in.shape, "w_in": w_in.shape, "norm_v": norm_v.shape, "w_s": w_s.shape,
              "b_s": b_s.shape, "w_o_gmlp": w_o_gmlp.shape, "w_o_sb": w_o_sb.shape, "w_out": w_out.shape,
              "norm_final": norm_final.shape}
    names = list(shapes)
    outs = [loss, grad_x.reshape(batch, seq, d)]
    for kind in range(4):
        outs += [res[name][kind].reshape(shapes[name]) for name in names]
    return tuple(outs)
```

```python
import functools
import math

import jax
import jax.numpy as jnp
from jax import lax
from jax.experimental import pallas as pl
from jax.experimental.pallas import tpu as pltpu

F32 = jnp.float32
BF16 = jnp.bfloat16
SDS = jax.ShapeDtypeStruct
MESH_ID = pl.DeviceIdType.MESH

N_DEV = 8
LANE = 128
SUBLANE = 8
VMEM_LIMIT = 56 * 1024 * 1024
SB_TILE = 512
SB_TILE_BWD = 512
SB_SCAN = 256
SB_HEADS = 2
MASKED_LOG = -1e30
RMS_EPS = 1e-6

ADAM_LR = 0.001
ADAM_B1 = 0.9
ADAM_B2 = 0.999
ADAM_EPS = 1e-08
ADAM_WD = 0.01
ADAM_STEP = 10

NT_DIMS = (((1,), (1,)), ((), ()))
TN_DIMS = (((0,), (0,)), ((), ()))


def _params(semantics=None):
    return pltpu.CompilerParams(dimension_semantics=semantics, vmem_limit_bytes=VMEM_LIMIT)


def _tile(n, preferred):
    t = min(n, preferred)
    assert n % t == 0, (n, t)
    return t


def _sigmoid(x):
    return 1.0 / (1.0 + jnp.exp(-x))


def _silu(x):
    s = _sigmoid(x)
    return x * s, s * (1.0 + x * (1.0 - s))


def _gelu(x):
    k = math.sqrt(2.0 / math.pi)
    x2 = x * x
    t = jnp.tanh(k * (x + 0.044715 * (x * x2)))
    cdf = 0.5 * (1.0 + t)
    return x * cdf, cdf + 0.5 * x * (1.0 - t * t) * (k * (1.0 + 3.0 * 0.044715 * x2))


def _rms_scale(x):
    return lax.rsqrt(jnp.mean(x * x, axis=-1, keepdims=True) + RMS_EPS)


def _iotas(n):
    return (lax.broadcasted_iota(jnp.int32, (n, n), 0), lax.broadcasted_iota(jnp.int32, (n, n), 1))


def _adamw(w, g, m, v):
    m = ADAM_B1 * m + (1.0 - ADAM_B1) * g
    v = ADAM_B2 * v + (1.0 - ADAM_B2) * (g * g)
    m_hat = m / (1.0 - ADAM_B1 ** ADAM_STEP)
    v_hat = v / (1.0 - ADAM_B2 ** ADAM_STEP)
    delta = -ADAM_LR * (m_hat / (jnp.sqrt(v_hat) + ADAM_EPS) + ADAM_WD * w)
    return delta, m, v


def _dot(a, b):
    return jnp.dot(a, b, preferred_element_type=F32)


def _dot_nt(a, b):
    return lax.dot_general(a, b, NT_DIMS, preferred_element_type=F32)


def _dot_tn(a, b):
    return lax.dot_general(a, b, TN_DIMS, preferred_element_type=F32)


def _sb_logs(raw, scale, valid):
    z = (raw * scale).astype(BF16)
    log_beta = jnp.minimum(z, 0) - jnp.log(1 + jnp.exp(-jnp.abs(z)))
    log_rest = log_beta - z
    if valid is not None:
        log_beta = jnp.where(valid, log_beta, MASKED_LOG)
        log_rest = jnp.where(valid, log_rest, 0)
    return log_beta, log_rest


def _me():
    return lax.axis_index("x"), lax.axis_index("y"), lax.axis_index("c")


def _slot(p):
    return 4 * p[0] + 2 * p[1] + p[2]


def _peer(me, k):
    flips = ((k >> 2) & 1, (k >> 1) & 1, k & 1)
    return tuple(1 - a if f else a for a, f in zip(me, flips))


def _stack_exchange(me, st_in, st_out, n_whole, send_sems, recv_sems, local_sems, arrivals=True):
    mine = _slot(me)
    ns = len(st_in)
    part = lambda a, dev: st_in[a] if a >= ns - n_whole else st_in[a].at[_slot(dev)]
    local = [pltpu.make_async_copy(part(a, me), st_out[a].at[mine], local_sems.at[a]) for a in range(ns)]
    remote, landed = [], []
    for k in range(1, N_DEV):
        peer = _peer(me, k)
        for a in range(ns):
            sems = dict(send_sem=send_sems.at[7 * a + k - 1], recv_sem=recv_sems.at[7 * a + k - 1])
            remote.append(pltpu.make_async_remote_copy(
                src_ref=part(a, peer), dst_ref=st_out[a].at[mine],
                device_id=peer, device_id_type=MESH_ID, **sems))
            if arrivals:
                got = st_out[a].at[_slot(peer)]
                landed.append(pltpu.make_async_remote_copy(
                    src_ref=got, dst_ref=got, device_id=me, device_id_type=MESH_ID, **sems))
    return local, remote, landed


def _gather_in_proj(x2d, norm_in, w_in_sh, wo_shards, my_slot):
    n, d = x2d.shape
    esh = w_in_sh.shape[1]
    pw = 2 * esh
    n_chip = N_DEV // 2
    tm = _tile(n, 1024)
    n_i = n // tm
    mid = n_i // 2
    no = len(wo_shards)
    flip_at = lambda st: jnp.where(st == 1, 2, jnp.where(st == 2, 1, jnp.where(st == 3, 3, 0)))

    def body(me_ref, x_ref, g_ref, win_ref, *refs):
        del me_ref
        wo_in = refs[:no]
        proj_ref, h_ref, wg_ref = refs[no:no + 3]
        wo_out = refs[no + 3:2 * no + 3]
        wv, stage = refs[2 * no + 3:2 * no + 5]
        wo_stage = refs[2 * no + 5:3 * no + 5]
        send_sems, recv_sems, pair_sems, own_sems, wo_send, wo_recv, wo_local = refs[3 * no + 5:]
        st, i = pl.program_id(0), pl.program_id(1)
        x, y, c = _me()
        me, sibling = (x, y, c), (x, y, 1 - c)
        chips = [(1 - x, y), (x, 1 - y), (1 - x, 1 - y)]
        chip_id = lambda p: 2 * p[0] + p[1]

        def window(chip, core):
            return wv.at[chip_id(chip), :, pl.ds(pl.multiple_of(core * esh, LANE), esh)]

        def copy(k, block, to, src=None):
            dst = window(block[:2], block[2])
            return pltpu.make_async_remote_copy(
                src_ref=dst if src is None else src, dst_ref=dst,
                send_sem=send_sems.at[k], recv_sem=recv_sems.at[k], device_id=to, device_id_type=MESH_ID)

        def wo_copy(a, k, block, to, src=None):
            dst = wo_out[a].at[_slot(block)]
            return pltpu.make_async_remote_copy(
                src_ref=dst if src is None else src, dst_ref=dst,
                send_sem=wo_send.at[7 * a + k], recv_sem=wo_recv.at[7 * a + k], device_id=to, device_id_type=MESH_ID)

        def own_copy():
            return pltpu.make_async_copy(stage, window((x, y), c), own_sems.at[0])

        def wo_own_copy(a):
            return pltpu.make_async_copy(wo_stage[a], wo_out[a].at[_slot(me)], wo_local.at[a])

        def pair_copy(step):
            chip = jnp.bitwise_xor(chip_id((x, y)), flip_at(step))
            return pltpu.make_async_copy(wv.at[chip], wg_ref.at[:, pl.ds(pl.multiple_of(chip * pw, LANE), pw)],
                                         pair_sems.at[step])

        first = jnp.logical_and(st == 0, i == 0)

        @pl.when(first)
        def _():
            stage[...] = win_ref[...].astype(BF16)
            own_copy().start()
            copy(0, me, sibling, src=stage).start()
            for j in range(2):
                copy(1 + j, me, (*chips[j], c), src=stage).start()
            own_copy().wait()
            copy(0, sibling, me).wait_recv()
            pair_copy(0).start()

        for s_ in range(n_chip - 1):
            @pl.when(jnp.logical_and(st == s_, i == mid))
            def _():
                copy(1 + s_, (*chips[s_], c), me).wait_recv()
                copy(4 + s_, (*chips[s_], c), sibling).start()
                if s_ == 0:
                    copy(3, me, (*chips[2], c), src=stage).start()
                if s_ == 1:
                    for a in range(no):
                        wo_stage[a][...] = wo_in[a][...].astype(BF16)
                        wo_own_copy(a).start()
                        wo_copy(a, 0, me, sibling, src=wo_stage[a]).start()
                        for j, chip in enumerate(chips):
                            wo_copy(a, 1 + j, me, (*chip, c), src=wo_stage[a]).start()
                if s_ == 2:
                    for a in range(no):
                        for j, chip in enumerate(chips):
                            wo_copy(a, 1 + j, (*chip, c), me).wait_recv()
                            wo_copy(a, 4 + j, (*chip, c), sibling).start()

        for s_ in range(1, n_chip):
            @pl.when(jnp.logical_and(st == s_, i == 0))
            def _():
                copy(3 + s_, (*chips[s_ - 1], 1 - c), me).wait_recv()
                pair_copy(s_).start()

        xv = x_ref[...]
        h = (xv * _rms_scale(xv) * g_ref[...]).astype(BF16)

        @pl.when(st == 0)
        def _():
            h_ref[...] = h

        chip_now = jnp.bitwise_xor(chip_id((x, y)), flip_at(st))
        proj_ref[...] = _dot(h, wv[chip_now]).astype(BF16)

        @pl.when(jnp.logical_and(st == n_chip - 1, i == n_i - 1))
        def _():
            copy(0, me, sibling, src=stage).wait_send()
            for j, chip in enumerate(chips):
                copy(1 + j, me, (*chip, c), src=stage).wait_send()
                copy(4 + j, (*chip, c), sibling).wait_send()
            for s_ in range(n_chip):
                pair_copy(s_).wait()
            for a in range(no):
                wo_copy(a, 0, me, sibling, src=wo_stage[a]).wait_send()
                wo_copy(a, 0, sibling, me).wait_recv()
                for j, chip in enumerate(chips):
                    wo_copy(a, 1 + j, me, (*chip, c), src=wo_stage[a]).wait_send()
                    wo_copy(a, 4 + j, (*chip, c), sibling).wait_send()
                    wo_copy(a, 4 + j, (*chip, 1 - c), me).wait_recv()
                wo_own_copy(a).wait()

    any_spec = pl.BlockSpec(memory_space=pl.ANY)
    vmem = pl.BlockSpec(memory_space=pltpu.VMEM)
    grid_spec = pltpu.PrefetchScalarGridSpec(
        num_scalar_prefetch=1, grid=(n_chip, n_i),
        in_specs=[pl.BlockSpec((tm, d), lambda st, i, me: (i, 0)),
                  pl.BlockSpec((1, d), lambda st, i, me: (0, 0)), vmem] + [vmem] * no,
        out_specs=[pl.BlockSpec((tm, pw), lambda st, i, me: (i, jnp.bitwise_xor(me[0] // 2, flip_at(st)))),
                   pl.BlockSpec((tm, d), lambda st, i, me: (jnp.where(st == 0, i, n_i - 1), 0)),
                   any_spec] + [any_spec] * no,
        scratch_shapes=[pltpu.VMEM((n_chip, d, pw), BF16), pltpu.VMEM((d, esh), BF16)] + [
            pltpu.VMEM(s.shape, BF16) for s in wo_shards] + [
            pltpu.SemaphoreType.DMA((7,)), pltpu.SemaphoreType.DMA((7,)),
            pltpu.SemaphoreType.DMA((n_chip,)), pltpu.SemaphoreType.DMA((1,)),
            pltpu.SemaphoreType.DMA((7 * no,)), pltpu.SemaphoreType.DMA((7 * no,)),
            pltpu.SemaphoreType.DMA((no,))])
    return pl.pallas_call(
        body, name="gather_in_proj", grid_spec=grid_spec,
        out_shape=[SDS((n, n_chip * pw), BF16), SDS((n, d), BF16), SDS((d, n_chip * pw), BF16)] + [
            SDS((N_DEV,) + s.shape, BF16) for s in wo_shards],
        compiler_params=pltpu.CompilerParams(dimension_semantics=("arbitrary", "arbitrary"),
                                             vmem_limit_bytes=VMEM_LIMIT),
    )(my_slot, x2d, norm_in, w_in_sh, *wo_shards)


EXCHANGE_ORDER = ((4, 2, 5, 3, 6, 7, 1, 0), (2, 4, 3, 5, 7, 6, 1, 0))


def _owner_at(mine, j):
    k = 0
    for step in range(N_DEV - 1):
        k = jnp.where(j == step, jnp.where(mine % 2 == 0, EXCHANGE_ORDER[0][step], EXCHANGE_ORDER[1][step]), k)
    return jnp.bitwise_xor(mine, k)


def _dw_in_exchange(h, dproj, my_slot, packed):
    n, d = h.shape
    esh = dproj.shape[1] // N_DEV
    tk = _tile(n, 512)
    nk = n // tk
    last_j = N_DEV - 1
    depth = 4

    def body(me_ref, h_ref, dp_ref, pk_in, win_out, pk_out,
             acc, sendbuf, win_send, win_recv, send_sems, recv_sems, local_sems):
        del me_ref
        j, k = pl.program_id(0), pl.program_id(1)
        me = _me()
        mine = _slot(me)

        def pack_copies():
            local = pltpu.make_async_copy(pk_in, pk_out.at[mine], local_sems.at[0])
            remote = [pltpu.make_async_remote_copy(
                src_ref=pk_in, dst_ref=pk_out.at[mine], send_sem=send_sems.at[kk - 1], recv_sem=recv_sems.at[kk - 1],
                device_id=_peer(me, kk), device_id_type=MESH_ID) for kk in range(1, N_DEV)]
            return local, remote

        def shard_copy(jj):
            owner = _owner_at(mine, jj)
            return pltpu.make_async_remote_copy(
                src_ref=sendbuf.at[jj % depth], dst_ref=win_out.at[mine],
                send_sem=win_send.at[jj % depth], recv_sem=win_recv.at[mine],
                device_id=(owner // 4, (owner // 2) % 2, owner % 2), device_id_type=MESH_ID)

        def own_copy():
            return pltpu.make_async_copy(sendbuf.at[last_j % depth], win_out.at[mine], local_sems.at[1])

        @pl.when(jnp.logical_and(j == 0, k == 0))
        def _():
            local, remote = pack_copies()
            for cp in [local] + remote:
                cp.start()

        @pl.when(k == 0)
        def _():
            acc[...] = jnp.zeros_like(acc)

        acc[...] += _dot_tn(h_ref[...], dp_ref[...])

        @pl.when(k == nk - 1)
        def _():
            @pl.when(j >= depth)
            def _():
                shard_copy(j - depth).wait_send()

            sendbuf[j % depth] = acc[...].astype(BF16)

            @pl.when(j < last_j)
            def _():
                shard_copy(j).start()

            @pl.when(j == last_j)
            def _():
                own_copy().start()
                for jj in range(last_j - depth + 1, last_j):
                    shard_copy(jj).wait_send()
                own_copy().wait()
                for src in range(N_DEV):
                    @pl.when(src != mine)
                    def _():
                        landed = win_out.at[src]
                        pltpu.make_async_remote_copy(
                            src_ref=landed, dst_ref=landed, send_sem=win_send.at[0], recv_sem=win_recv.at[src],
                            device_id=me, device_id_type=MESH_ID).wait_recv()
                local, remote = pack_copies()
                for cp in remote:
                    cp.wait_send()
                for kk in range(1, N_DEV):
                    landed = pk_out.at[_slot(_peer(me, kk))]
                    pltpu.make_async_remote_copy(
                        src_ref=landed, dst_ref=landed, send_sem=send_sems.at[kk - 1], recv_sem=recv_sems.at[kk - 1],
                        device_id=me, device_id_type=MESH_ID).wait_recv()
                local.wait()

    any_spec = pl.BlockSpec(memory_space=pl.ANY)
    grid_spec = pltpu.PrefetchScalarGridSpec(
        num_scalar_prefetch=1, grid=(N_DEV, nk),
        in_specs=[pl.BlockSpec((tk, d), lambda j, k, me: (k, 0)),
                  pl.BlockSpec((tk, esh), lambda j, k, me: (k, _owner_at(me[0], j))), any_spec],
        out_specs=[any_spec] * 2,
        scratch_shapes=[pltpu.VMEM((d, esh), F32), pltpu.VMEM((depth, d, esh), BF16),
                        pltpu.SemaphoreType.DMA((depth,)), pltpu.SemaphoreType.DMA((N_DEV,)),
                        pltpu.SemaphoreType.DMA((N_DEV - 1,)), pltpu.SemaphoreType.DMA((N_DEV - 1,)),
                        pltpu.SemaphoreType.DMA((2,))])
    return pl.pallas_call(
        body, name="dw_in_exchange", grid_spec=grid_spec,
        out_shape=[SDS((N_DEV, d, esh), BF16), SDS((N_DEV,) + packed.shape, packed.dtype)],
        compiler_params=_params(("arbitrary", "arbitrary")),
    )(my_slot, h, dproj, packed)


def _finish_small(packs, late_packs, groups, chunk):
    rows = packs.shape[1]
    late = late_packs.shape[1]
    gc = groups * chunk

    def body(p_ref, l_ref, sum_ref, loss_ref):
        row, col = _iotas(chunk)
        tril = col <= row
        for g in range(groups):
            rs = slice(g * chunk, (g + 1) * chunk)
            tot = p_ref[0, rs, :]
            for dev in range(1, N_DEV):
                tot = tot + p_ref[dev, rs, :]
            sum_ref[rs, :] = jnp.where(tril, tot, 0.0)
        rs = slice(gc, rows)
        tot = p_ref[0, rs, :]
        for dev in range(1, N_DEV):
            tot = tot + p_ref[dev, rs, :]
        sum_ref[rs, :] = tot
        loss_ref[...] = jnp.full((SUBLANE, LANE), jnp.sum(tot[rows - gc - SUBLANE:, :]), F32)
        tot = l_ref[0]
        for dev in range(1, N_DEV):
            tot = tot + l_ref[dev]
        sum_ref[rows:rows + late, :] = tot

    return pl.pallas_call(
        body, name="finish_small",
        out_shape=[SDS((rows + late, LANE), F32), SDS((SUBLANE, LANE), F32)],
        in_specs=[pl.BlockSpec(memory_space=pltpu.VMEM)] * 2,
        out_specs=[pl.BlockSpec(memory_space=pltpu.VMEM)] * 2,
        compiler_params=pltpu.CompilerParams(vmem_limit_bytes=VMEM_LIMIT),
    )(packs, late_packs)


def _branch_a_fwd(proj, norm_v, w_s, b_col):
    n = proj.shape[0]
    d = norm_v.shape[1]
    groups, chunk, _ = w_s.shape
    tr = _tile(n, 4 * chunk)

    def body(u_ref, v_ref, z_ref, gv_ref, ws_ref, b_ref, ya_ref, vn_s, pre_s):
        row, col = _iotas(chunk)
        tril = col <= row
        vg, _ = _gelu(v_ref[...].astype(F32))
        vn_s[...] = (vg * _rms_scale(vg) * gv_ref[...]).astype(BF16)
        ug, _ = _gelu(u_ref[...].astype(F32))
        sz, _ = _silu(z_ref[...].astype(F32))
        pre_s[...] = ug * sz
        for g in range(groups):
            wm = jnp.where(tril, ws_ref[g], 0.0).astype(BF16)
            cs = slice(g * chunk, (g + 1) * chunk)
            for c in range(tr // chunk):
                rs = slice(c * chunk, (c + 1) * chunk)
                mixed = _dot(wm, vn_s[rs, cs]) + b_ref[g]
                ya_ref[rs, cs] = (pre_s[rs, cs] * mixed).astype(BF16)

    seg = lambda k: pl.BlockSpec((tr, d), lambda i: (i, k))
    return pl.pallas_call(
        body, name="branch_a_fwd", grid=(n // tr,),
        in_specs=[seg(0), seg(1), seg(2),
                  pl.BlockSpec((1, d), lambda i: (0, 0)),
                  pl.BlockSpec((groups, chunk, chunk), lambda i: (0, 0, 0)),
                  pl.BlockSpec((groups, chunk, 1), lambda i: (0, 0, 0))],
        out_specs=pl.BlockSpec((tr, d), lambda i: (i, 0)),
        out_shape=SDS((n, d), BF16),
        scratch_shapes=[pltpu.VMEM((tr, d), BF16), pltpu.VMEM((tr, d), F32)],
        compiler_params=_params(("parallel",)),
    )(proj, proj, proj, norm_v, w_s, b_col)


def _sb_fwd(proj, batch, seq, d, hd):
    heads = d // hd
    t = _tile(seq, SB_TILE)
    sw = _tile(t, SB_SCAN)
    nb = t // sw
    scale = hd ** -0.5
    nblk = seq // t
    nh = SB_HEADS
    wide = nh * hd
    cols = [slice(hh * hd, (hh + 1) * hd) for hh in range(nh)]

    def body(qs, k_ref, vs, zb_ref, yb_ref, o_ref, tot_ref, kts, later, acc):
        for jb in range(nblk):
            kts[jb] = k_ref[jb * t:(jb + 1) * t, :].astype(F32).T.astype(BF16)
        row, col = _iotas(t)
        later[...] = (row[:sw, :sw] > col[:sw, :sw]).astype(BF16)

        def qblock(i, carry):
            r0 = pl.multiple_of(i * t, t)

            def tile(j, runs, valid):
                c0 = pl.multiple_of(j * t, t)
                logs = [_sb_logs(_dot(qs[pl.ds(r0, t), cs], kts[j, cs, :]), scale, valid) for cs in cols]
                scans = [_dot(jnp.concatenate([logs[hh][1][:, b * sw:(b + 1) * sw] for b in range(nb)], axis=0),
                              later[...]) for hh in range(nh)]
                new_runs = []
                for hh in range(nh):
                    after = runs[hh]
                    blocks = [None] * nb
                    for b in reversed(range(nb)):
                        ks_ = slice(b * sw, (b + 1) * sw)
                        inside = scans[hh][b * t:(b + 1) * t]
                        blocks[b] = jnp.exp(logs[hh][0][:, ks_].astype(F32) + inside + after).astype(BF16)
                        after = after + inside[:, 0:1] + logs[hh][1][:, b * sw:b * sw + 1].astype(F32)
                    new_runs.append(after)
                    pv = _dot(jnp.concatenate(blocks, axis=1), vs[pl.ds(c0, t), cols[hh]])
                    if valid is None:
                        acc[:, cols[hh]] += pv
                    else:
                        acc[:, cols[hh]] = pv
                return tuple(new_runs)

            runs = tile(i, (jnp.zeros((t, 1), F32),) * nh, col < row)
            runs = lax.fori_loop(0, i, lambda jj, rs: tile(i - 1 - jj, rs, None), runs)
            for hh in range(nh):
                out = acc[:, cols[hh]]
                o_ref[pl.ds(r0, t), cols[hh]] = out.astype(BF16)
                tot_ref[hh, pl.ds(r0, t), :] = runs[hh]
                sz, _ = _silu(zb_ref[pl.ds(r0, t), cols[hh]].astype(F32))
                yb_ref[pl.ds(r0, t), cols[hh]] = (out * sz).astype(BF16)
            return carry

        lax.fori_loop(0, nblk, qblock, 0)

    col0 = d // wide
    seg = lambda k: pl.BlockSpec((seq, wide), lambda b, h: (b, k * col0 + h))
    return pl.pallas_call(
        body, name="sb_fwd", grid=(batch, heads // nh),
        in_specs=[seg(3), seg(4), seg(5), seg(6)],
        out_specs=[pl.BlockSpec((seq, wide), lambda b, h: (b, h))] * 2 + [
            pl.BlockSpec((nh, seq, 1), lambda b, h: (b * (heads // nh) + h, 0, 0))],
        out_shape=[SDS((batch * seq, d), BF16), SDS((batch * seq, d), BF16), SDS((batch * heads, seq, 1), F32)],
        scratch_shapes=[pltpu.VMEM((nblk, wide, t), BF16), pltpu.VMEM((sw, sw), BF16), pltpu.VMEM((t, wide), F32)],
        compiler_params=_params(("parallel", "parallel")),
    )(proj, proj, proj, proj)


def _tail(x2d, tgt, ya, yb, proj, w_oa, w_ob, w_out, norm_final):
    n, d = x2d.shape
    e = proj.shape[1]
    tm = _tile(n, 256)
    steps = n // tm

    def body(x_ref, t_ref, ya_ref, yb_ref, ga_ref, gb_ref, woa_ref, wob_ref, wout_ref, gf_ref,
             dproj_ref, dx2_ref, dya_ref, dyb_ref, mrg_ref, dpa_ref, dpb_ref, loss_ref, dgf_ref, dg_s, dg_sems):
        i = pl.program_id(0)

        def gate_copy(step):
            rows_ = pl.ds(pl.multiple_of(step * tm, tm), tm)
            return pltpu.make_async_copy(dg_s.at[step % 2], dproj_ref.at[rows_, pl.ds(7 * d, 2 * d)],
                                         dg_sems.at[step % 2])

        @pl.when(i == 0)
        def _():
            loss_ref[...] = jnp.zeros_like(loss_ref)
            dgf_ref[...] = jnp.zeros_like(dgf_ref)

        @pl.when(i >= 2)
        def _():
            gate_copy(i - 2).wait()

        pa = _dot(ya_ref[...], woa_ref[...])
        pb = _dot(yb_ref[...], wob_ref[...])
        sa = _sigmoid(ga_ref[...].astype(F32))
        sb = _sigmoid(gb_ref[...].astype(F32))
        merged = (sa * pa + sb * pb).astype(BF16)
        mrg_ref[...] = merged
        x2 =x_ref[...] + _dot(merged, wout_ref[...])
        r2 = _rms_scale(x2)
        xh = x2 * r2
        gf = gf_ref[...]
        diff = xh * gf - t_ref[...]
        loss_ref[...] += jnp.sum(diff * diff, axis=0, keepdims=True) * (0.5 / d)
        dy = diff * (1.0 / d)
        dgf_ref[...] += jnp.sum(dy * xh, axis=0, keepdims=True)
        dxh = dy * gf
        dx2 = r2 * (dxh - xh * jnp.mean(dxh * xh, axis=-1, keepdims=True))
        dx2_ref[...] = dx2
        dm = _dot_nt(dx2.astype(BF16), wout_ref[...])
        dpa = (dm * sa).astype(BF16)
        dpb = (dm * sb).astype(BF16)
        dpa_ref[...] = dpa
        dpb_ref[...] = dpb
        dg_s[i % 2, :, 0:d] = (dm * pa * (sa * (1.0 - sa))).astype(BF16)
        dg_s[i % 2, :, d:2 * d] = (dm * pb * (sb * (1.0 - sb))).astype(BF16)
        gate_copy(i).start()
        dya_ref[...] = _dot_nt(dpa, woa_ref[...]).astype(BF16)
        dyb_ref[...] = _dot_nt(dpb, wob_ref[...]).astype(BF16)

        @pl.when(i == steps - 1)
        def _():
            if steps >= 2:
                gate_copy(i - 1).wait()
            gate_copy(i).wait()

    rows = lambda k=0: pl.BlockSpec((tm, d), lambda i: (i, k))
    full = pl.BlockSpec((d, d), lambda i: (0, 0))
    vec = pl.BlockSpec((1, d), lambda i: (0, 0))
    return pl.pallas_call(
        body, name="tail", grid=(steps,),
        in_specs=[rows(), rows(), rows(), rows(), rows(7), rows(8), full, full, full, vec],
        out_specs=[pl.BlockSpec(memory_space=pl.ANY),
                   rows(), rows(), rows(), rows(), rows(), rows(), vec, vec],
        out_shape=[SDS((n, e), BF16), SDS((n, d), F32), SDS((n, d), BF16), SDS((n, d), BF16),
                   SDS((n, d), BF16), SDS((n, d), BF16), SDS((n, d), BF16),
                   SDS((1, d), F32), SDS((1, d), F32)],
        scratch_shapes=[pltpu.VMEM((2, tm, 2 * d), BF16), pltpu.SemaphoreType.DMA((2,))],
        compiler_params=_params(("arbitrary",)),
    )(x2d, tgt, ya, yb, proj, proj, w_oa, w_ob, w_out, norm_final)


def _dw_o(pairs):
    n, d = pairs[0][0].shape
    tk = _tile(n, 1024)
    nk = n // tk
    npair = len(pairs)

    def body(*refs):
        a_refs, b_refs = refs[:npair], refs[npair:2 * npair]
        o_ref, acc = refs[2 * npair], refs[2 * npair + 1]
        p, k = pl.program_id(0), pl.program_id(1)

        @pl.when(k == 0)
        def _():
            acc[...] = jnp.zeros_like(acc)

        for q in range(npair):
            @pl.when(p == q)
            def _():
                acc[...] += _dot_tn(a_refs[q][...], b_refs[q][...].astype(BF16))

        @pl.when(k == nk - 1)
        def _():
            o_ref[0] = acc[...].astype(BF16)

    def tiles(q):
        return pl.BlockSpec((tk, d), lambda p, k: (jnp.where(p == q, k, jnp.where(p < q, 0, nk - 1)), 0))

    return pl.pallas_call(
        body, name="dw_o", grid=(npair, nk),
        in_specs=[tiles(q) for q in range(npair)] * 2,
        out_specs=pl.BlockSpec((1, d, d), lambda p, k: (p, 0, 0)),
        out_shape=SDS((npair, d, d), BF16),
        scratch_shapes=[pltpu.VMEM((d, d), F32)],
        compiler_params=_params(("arbitrary", "arbitrary")),
    )(*[a for a, _ in pairs], *[b for _, b in pairs])


def _sb_bwd(proj, o, dyb, tot, dproj, stacks, batch, seq, d, hd):
    heads = d // hd
    t = _tile(seq, SB_TILE_BWD)
    sw = _tile(t, SB_SCAN)
    nb = t // sw
    scale = hd ** -0.5
    nblk = seq // t
    nh = SB_HEADS
    wide = nh * hd
    hs = range(nh)
    cols = [slice(hh * hd, (hh + 1) * hd) for hh in hs]
    blocks = [slice(b * sw, (b + 1) * sw) for b in range(nb)]
    last = slice(sw - 1, sw)

    def compute(qs, ks, v_ref, zb_ref, dyb_ref, tot_ref, kts, vts, dos, res, upto, before, dq):
        for jb in range(nblk):
            rows = slice(jb * t, (jb + 1) * t)
            kts[jb] = ks[rows, :].astype(F32).T.astype(BF16)
            vts[jb] = v_ref[rows, :].astype(F32).T.astype(BF16)
        sz, _ = _silu(zb_ref[...].astype(F32))
        dos[...] = (dyb_ref[...].astype(F32) * sz).astype(BF16)
        res[1] = jnp.zeros((seq, wide), F32)
        res[2] = jnp.zeros((seq, wide), F32)
        row, col = _iotas(t)
        upto[...] = (row[:sw, :sw] <= col[:sw, :sw]).astype(BF16)
        before[...] = (row[:sw, :sw] < col[:sw, :sw]).astype(BF16)

        def qblock(i, carry):
            r0 = pl.multiple_of(i * t, t)

            def tile(j, sums, valid):
                c0 = pl.multiple_of(j * t, t)
                q_i = [qs[pl.ds(r0, t), cs] for cs in cols]
                do_i = [dos[pl.ds(r0, t), cs] for cs in cols]
                logs = [_sb_logs(_dot(q_i[hh], kts[j, cols[hh], :]), scale, valid) for hh in hs]
                dw = [_dot(do_i[hh], vts[j, cols[hh], :]) for hh in hs]
                scans = [_dot(jnp.concatenate([logs[hh][1][:, ks_] for ks_ in blocks], axis=0), upto[...]) for hh in hs]
                ws, gs, new_runs = [], [], []
                for hh in hs:
                    left = tot_ref[hh, pl.ds(r0, t), :] - sums[hh][0]
                    w_b, g_b = [], []
                    for b, ks_ in enumerate(blocks):
                        inside = scans[hh][b * t:(b + 1) * t]
                        w = jnp.exp(logs[hh][0][:, ks_].astype(F32) + (left - inside))
                        w_b.append(w.astype(BF16))
                        g_b.append((dw[hh][:, ks_] * w).astype(BF16))
                        left = left - inside[:, last]
                    ws.append(jnp.concatenate(w_b, axis=1))
                    gs.append(g_b)
                    new_runs.append(tot_ref[hh, pl.ds(r0, t), :] - left)
                gscans = [_dot(jnp.concatenate(gs[hh], axis=0), before[...]) for hh in hs]
                dzs, new_gruns = [], []
                for hh in hs:
                    g_before = sums[hh][1]
                    dz_b = []
                    for b, ks_ in enumerate(blocks):
                        inside = gscans[hh][b * t:(b + 1) * t]
                        beta = jnp.exp(logs[hh][0][:, ks_]).astype(F32)
                        g = gs[hh][b].astype(F32)
                        dz_b.append(((g - (g + inside + g_before) * beta) * scale).astype(BF16))
                        g_before = g_before + inside[:, last] + g[:, last]
                    dzs.append(jnp.concatenate(dz_b, axis=1))
                    new_gruns.append(g_before)
                for hh in hs:
                    res[2, pl.ds(c0, t), cols[hh]] += _dot_tn(ws[hh], do_i[hh])
                for hh in hs:
                    res[1, pl.ds(c0, t), cols[hh]] += _dot_tn(dzs[hh], q_i[hh])
                for hh in hs:
                    dq[:, cols[hh]] += _dot(dzs[hh], ks[pl.ds(c0, t), cols[hh]])
                return tuple((new_runs[hh], new_gruns[hh]) for hh in hs)

            zero = jnp.zeros((t, 1), F32)
            dq[...] = jnp.zeros_like(dq)
            sums = lax.fori_loop(0, i, lambda j, sm: tile(j, sm, None), ((zero, zero),) * nh)
            tile(i, sums, col < row)
            res[0, pl.ds(r0, t), :] = dq[...]
            return carry

        lax.fori_loop(0, nblk, qblock, 0)

    pairs = heads // nh

    ns = len(stacks)

    def body(qs, ks, v_ref, zb_ref, o_ref, dyb_ref, tot_ref, dproj_in, *refs):
        del dproj_in
        st_in, out_ref, st_out = refs[:ns], refs[ns], refs[ns + 1:2 * ns + 1]
        (kts, vts, dos, res, upto, before, dq, stage, stage_sems,
         send_sems, recv_sems, local_sems) = refs[2 * ns + 1:]
        step = pl.program_id(0) * pairs + pl.program_id(1)
        exchange = functools.partial(_stack_exchange, _me(), st_in, st_out, 1, send_sems, recv_sems, local_sems)

        @pl.when(step == 0)
        def _():
            local, remote, _ = exchange(arrivals=False)
            for cp in local + remote:
                cp.start()

        def out_copies(s):
            rows_ = pl.ds(pl.multiple_of((s // pairs) * seq, seq), seq)
            return [pltpu.make_async_copy(
                stage.at[k], out_ref.at[rows_, pl.ds(pl.multiple_of((3 + k) * d + (s % pairs) * wide, wide), wide)],
                stage_sems.at[k]) for k in range(4)]

        compute(qs, ks, v_ref, zb_ref, dyb_ref, tot_ref, kts, vts, dos, res, upto, before, dq)

        @pl.when(step > 0)
        def _():
            for cp in out_copies(step - 1):
                cp.wait()

        for k in range(3):
            stage[k] = res[k].astype(BF16)
        _, dsz = _silu(zb_ref[...].astype(F32))
        stage[3] = (dyb_ref[...].astype(F32) * o_ref[...].astype(F32) * dsz).astype(BF16)
        for cp in out_copies(step):
            cp.start()

        @pl.when(step == batch * pairs - 1)
        def _():
            for cp in out_copies(step):
                cp.wait()
            local, remote, landed = exchange()
            for cp in remote:
                cp.wait_send()
            for cp in landed:
                cp.wait_recv()
            for cp in local:
                cp.wait()

    col0 = d // wide
    seg = lambda k: pl.BlockSpec((seq, wide), lambda b, h: (b, k * col0 + h))
    head = pl.BlockSpec((seq, wide), lambda b, h: (b, h))
    any_spec = pl.BlockSpec(memory_space=pl.ANY)
    return pl.pallas_call(
        body, name="sb_bwd", grid=(batch, pairs),
        in_specs=[seg(3), seg(4), seg(5), seg(6), head, head,
                  pl.BlockSpec((nh, seq, 1), lambda b, h: (b * pairs + h, 0, 0)), any_spec] + [any_spec] * ns,
        out_specs=[any_spec] * (ns + 1),
        out_shape=[SDS(dproj.shape, dproj.dtype)] + [SDS(s.shape, s.dtype) for s in stacks[:-1]] + [
            SDS((N_DEV,) + stacks[-1].shape, stacks[-1].dtype)],
        input_output_aliases={7: 0},
        scratch_shapes=[pltpu.VMEM((nblk, wide, t), BF16)] * 2 + [
            pltpu.VMEM((seq, wide), BF16), pltpu.VMEM((3, seq, wide), F32),
            pltpu.VMEM((sw, sw), BF16), pltpu.VMEM((sw, sw), BF16), pltpu.VMEM((t, wide), F32),
            pltpu.VMEM((4, seq, wide), BF16), pltpu.SemaphoreType.DMA((4,)),
            pltpu.SemaphoreType.DMA((7 * ns,)), pltpu.SemaphoreType.DMA((7 * ns,)),
            pltpu.SemaphoreType.DMA((ns,))],
        compiler_params=_params(("arbitrary", "arbitrary")),
    )(proj, proj, proj, proj, o, dyb, tot, dproj, *stacks)


def _branch_a_bwd(proj, dya, norm_v, w_s, b_col, dproj):
    n = proj.shape[0]
    d = norm_v.shape[1]
    groups, chunk, _ = w_s.shape
    tr = _tile(n, 2 * chunk)

    def body(u_ref, v_ref, z_ref, dya_ref, gv_ref, ws_ref, b_ref, dproj_in,
             out_ref, dws_ref, dbias_ref, dgv_ref, vn_s, dmix_s, dvn_s, db_ref):
        del dproj_in

        @pl.when(pl.program_id(0) == 0)
        def _():
            dws_ref[...] = jnp.zeros_like(dws_ref)
            db_ref[...] = jnp.zeros_like(db_ref)
            dgv_ref[...] = jnp.zeros_like(dgv_ref)

        row, col = _iotas(chunk)
        tril = col <= row
        u, v, z, dya_v = (r[...].astype(F32) for r in (u_ref, v_ref, z_ref, dya_ref))
        gv = gv_ref[...]
        vg, dvg_dv = _gelu(v)
        r = _rms_scale(vg)
        vh = vg * r
        vn_s[...] = (vh * gv).astype(BF16)
        ug, dug_du = _gelu(u)
        sz, dsz = _silu(z)
        dmix_s[...] = dya_v * ug * sz
        for g in range(groups):
            wm = jnp.where(tril, ws_ref[g], 0.0).astype(BF16)
            cs = slice(g * chunk, (g + 1) * chunk)
            for c in range(tr // chunk):
                rs = slice(c * chunk, (c + 1) * chunk)
                vn = vn_s[rs, cs]
                mixed = _dot(wm, vn) + b_ref[g]
                dmix = dmix_s[rs, cs]
                dmix16 = dmix.astype(BF16)
                dws_ref[g] += _dot_nt(dmix16, vn)
                db_ref[g] += dmix
                dvn_s[rs, cs] = _dot_tn(wm, dmix16)
                t_u = dya_v[rs, cs] * mixed
                out_ref[rs, g * chunk:(g + 1) * chunk] = (t_u * sz[rs, cs] * dug_du[rs, cs]).astype(BF16)
                out_ref[rs, 2 * d + g * chunk:2 * d + (g + 1) * chunk] = (t_u * ug[rs, cs] * dsz[rs, cs]).astype(BF16)
        dvn = dvn_s[...]
        dgv_ref[...] += jnp.sum(dvn * vh, axis=0, keepdims=True)
        dvh = dvn * gv
        dvg = r * (dvh - vh * jnp.mean(dvh * vh, axis=-1, keepdims=True))
        out_ref[:, d:2 * d] = (dvg * dvg_dv).astype(BF16)

        @pl.when(pl.program_id(0) == n // tr - 1)
        def _():
            for g in range(groups):
                dbias_ref[g:g + 1, :] = jnp.sum(db_ref[g].T, axis=0, keepdims=True)

    seg = lambda k: pl.BlockSpec((tr, d), lambda i: (i, k))
    return pl.pallas_call(
        body, name="branch_a_bwd", grid=(n // tr,),
        in_specs=[seg(0), seg(1), seg(2), seg(0),
                  pl.BlockSpec((1, d), lambda i: (0, 0)),
                  pl.BlockSpec((groups, chunk, chunk), lambda i: (0, 0, 0)),
                  pl.BlockSpec((groups, chunk, 1), lambda i: (0, 0, 0)),
                  pl.BlockSpec(memory_space=pl.ANY)],
        out_specs=[pl.BlockSpec((tr, 3 * d), lambda i: (i, 0)),
                   pl.BlockSpec((groups, chunk, chunk), lambda i: (0, 0, 0)),
                   pl.BlockSpec((groups, chunk), lambda i: (0, 0)),
                   pl.BlockSpec((1, d), lambda i: (0, 0))],
        out_shape=[SDS(dproj.shape, dproj.dtype), SDS((groups, chunk, chunk), F32),
                   SDS((groups, chunk), F32), SDS((1, d), F32)],
        input_output_aliases={7: 0},
        scratch_shapes=[pltpu.VMEM((tr, d), BF16), pltpu.VMEM((tr, d), F32), pltpu.VMEM((tr, d), F32),
                        pltpu.VMEM((groups, chunk, chunk), F32)],
        compiler_params=_params(("arbitrary",)),
    )(proj, proj, proj, dya, norm_v, w_s, b_col, dproj)


def _dx(dproj, wg_in, x2d, dx2, norm_in):
    n, d = x2d.shape
    nsh = N_DEV // 2
    esh = wg_in.shape[1] // nsh
    tm = _tile(n, 1024)

    def body(dp_ref, w_ref, x_ref, dx2_ref, g_ref, gx_ref, dg_ref, acc):
        i, k = pl.program_id(0), pl.program_id(1)

        @pl.when(jnp.logical_and(i == 0, k == 0))
        def _():
            dg_ref[...] = jnp.zeros_like(dg_ref)

        @pl.when(k == 0)
        def _():
            acc[...] = jnp.zeros_like(acc)

        acc[...] += _dot_nt(dp_ref[...], w_ref[...])

        @pl.when(k == nsh - 1)
        def _():
            dh = acc[...]
            x = x_ref[...]
            r = _rms_scale(x)
            xh = x * r
            dg_ref[...] += jnp.sum(dh * xh, axis=0, keepdims=True)
            dxh = dh * g_ref[...]
            gx_ref[...] = dx2_ref[...] + r * (dxh - xh * jnp.mean(dxh * xh, axis=-1, keepdims=True))

    rows = pl.BlockSpec((tm, d), lambda i, k: (i, 0))
    vec = pl.BlockSpec((1, d), lambda i, k: (0, 0))
    return pl.pallas_call(
        body, name="dx", grid=(n // tm, nsh),
        in_specs=[pl.BlockSpec((tm, esh), lambda i, k: (i, k)),
                  pl.BlockSpec((d, esh), lambda i, k: (0, k)), rows, rows, vec],
        out_specs=[rows, vec],
        out_shape=[SDS((n, d), F32), SDS((1, d), F32)],
        scratch_shapes=[pltpu.VMEM((tm, d), F32)],
        compiler_params=_params(("arbitrary", "arbitrary")),
    )(dproj, wg_in, x2d, dx2, norm_in)


def _adamw_outputs(g_ref, d_ref, m_ref, v_ref, g, w, m, v):
    delta, m2, v2 = _adamw(w, g, m, v)
    g_ref[...] = g
    d_ref[...] = delta
    m_ref[...] = m2
    v_ref[...] = v2


def _reduce_adamw(slots, w, m, v, name):
    _, r, c = slots.shape
    tr = _tile(r, 128)

    def body(s_ref, w_ref, m_ref, v_ref, g_out, d_out, m_out, v_out):
        g = s_ref[0].astype(F32)
        for k in range(1, N_DEV):
            g = g + s_ref[k].astype(F32)
        _adamw_outputs(g_out, d_out, m_out, v_out, g, w_ref[...], m_ref[...], v_ref[...])

    blk = pl.BlockSpec((tr, c), lambda i: (i, 0))
    return pl.pallas_call(
        body, name=name, grid=(r // tr,),
        in_specs=[pl.BlockSpec((N_DEV, tr, c), lambda i: (0, i, 0)), blk, blk, blk],
        out_specs=[blk] * 4,
        out_shape=[SDS((r, c), F32)] * 4,
        compiler_params=_params(("parallel",)),
    )(slots, w, m, v)


def _adamw_small(g, w, m, v, name):
    def body(g_ref, w_ref, m_ref, v_ref, g_out, d_out, m_out, v_out):
        _adamw_outputs(g_out, d_out, m_out, v_out, g_ref[...], w_ref[...], m_ref[...], v_ref[...])

    return pl.pallas_call(
        body, name=name,
        out_shape=[SDS(g.shape, F32)] * 4,
        in_specs=[pl.BlockSpec(memory_space=pltpu.VMEM)] * 4,
        out_specs=[pl.BlockSpec(memory_space=pltpu.VMEM)] * 4,
    )(g, w, m, v)


def kernel(x, norm_in, w_in, norm_v, w_s, b_s, w_o_gmlp, w_o_sb, w_out, norm_final, loss_target, m_norm_in, m_w_in, m_norm_v, m_w_s, m_b_s, m_w_o_gmlp, m_w_o_sb, m_w_out, m_norm_final, v_norm_in, v_w_in, v_norm_v, v_w_s, v_b_s, v_w_o_gmlp, v_w_o_sb, v_w_out, v_norm_final):
    batch, seq, d = x.shape
    n = batch * seq
    groups, chunk = w_s.shape[1], w_s.shape[2]
    hd = LANE
    x2d = x.reshape(n, d)
    tgt = loss_target.reshape(n, d)
    b_col = b_s[0].reshape(groups, chunk, 1)
    norm_final2 = norm_final.reshape(1, d)

    my_slot = _slot(_me()).astype(jnp.int32).reshape(1)
    proj, h, wg_in, wg_oa, wg_ob, wg_out = _gather_in_proj(
        x2d, norm_in, w_in[0], [w_o_gmlp[0], w_o_sb[0], w_out[0]], my_slot)
    rsh = wg_oa.shape[1]
    wf_oa, wf_ob, wf_out = (w.reshape(N_DEV * rsh, d) for w in (wg_oa, wg_ob, wg_out))
    ya = _branch_a_fwd(proj, norm_v, w_s[0], b_col)
    yb, o, sb_tot = _sb_fwd(proj, batch, seq, d, hd)
    dproj, dx2, dya, dyb, merged, dpa, dpb, loss_vec, dgf = _tail(
        x2d, tgt, ya, yb, proj, wf_oa, wf_ob, wf_out, norm_final2)
    gp_wo = _dw_o([(ya, dpa), (yb, dpb), (merged, dx2)])
    dproj, gp_ws, gp_b, gp_nv = _branch_a_bwd(proj, dya, norm_v, w_s[0], b_col, dproj)

    slab = lambda a: a.reshape(d // LANE, LANE)
    gc = groups * chunk
    packed = jnp.concatenate([gp_ws.reshape(gc, chunk), gp_b, slab(gp_nv), slab(dgf), slab(loss_vec)], axis=0)
    dproj, s_oa, s_ob, s_out, packs = _sb_bwd(
        proj, o, dyb, sb_tot, dproj, [gp_wo[k].reshape(N_DEV, rsh, d) for k in range(3)] + [packed],
        batch, seq, d, hd)
    grad_x, gp_nin = _dx(dproj, wg_in, x2d, dx2, norm_in)
    s_win, late_packs = _dw_in_exchange(h, dproj, my_slot, slab(gp_nin))
    tot, loss_slab = _finish_small(packs, late_packs, groups, chunk)
    ns = d // LANE
    g_ws = tot[:gc]
    g_b = tot[gc:gc + groups]
    g_nv, g_nf, _, g_nin = (tot[gc + groups + k * ns:gc + groups + (k + 1) * ns] for k in range(4))
    loss = loss_slab[0, 0]

    res = {}
    res["w_in"] = _reduce_adamw(s_win, w_in[0], m_w_in[0], v_w_in[0], "adamw_w_in")
    res["w_o_gmlp"] = _reduce_adamw(s_oa, w_o_gmlp[0], m_w_o_gmlp[0], v_w_o_gmlp[0], "adamw_w_o_gmlp")
    res["w_o_sb"] = _reduce_adamw(s_ob, w_o_sb[0], m_w_o_sb[0], v_w_o_sb[0], "adamw_w_o_sb")
    res["w_out"] = _reduce_adamw(s_out, w_out[0], m_w_out[0], v_w_out[0], "adamw_w_out")
    res["norm_in"] = _adamw_small(g_nin, slab(norm_in), slab(m_norm_in), slab(v_norm_in), "adamw_norm_in")
    res["norm_v"] = _adamw_small(g_nv, slab(norm_v), slab(m_norm_v), slab(v_norm_v), "adamw_norm_v")
    res["norm_final"] = _adamw_small(g_nf, slab(norm_final), slab(m_norm_final), slab(v_norm_final), "adamw_norm_final")
    res["w_s"] = _adamw_small(g_ws, w_s.reshape(gc, chunk), m_w_s.reshape(gc, chunk), v_w_s.reshape(gc, chunk), "adamw_w_s")
    res["b_s"] = _adamw_small(g_b, b_s[0], m_b_s[0], v_b_s[0], "adamw_b_s")

    shapes = {"norm_in": norm_in.shape, "w_in": w_in.shape, "norm_v": norm_v.shape, "w_s": w_s.shape,
              "b_s": b_s.shape, "w_o_gmlp": w_o_gmlp.shape, "w_o_sb": w_o_sb.shape, "w_out": w_out.shape,
              "norm_final": norm_final.shape}
    names = list(shapes)
    outs = [loss, grad_x.reshape(batch, seq, d)]
    for kind in range(4):
        outs += [res[name][kind].reshape(shapes[name]) for name in names]
    return tuple(outs)
```

```python
import functools
import math

import jax
import jax.numpy as jnp
from jax import lax
from jax.experimental import pallas as pl
from jax.experimental.pallas import tpu as pltpu

F32 = jnp.float32
BF16 = jnp.bfloat16
SDS = jax.ShapeDtypeStruct
MESH_ID = pl.DeviceIdType.MESH

N_DEV = 8
LANE = 128
SUBLANE = 8
VMEM_LIMIT = 56 * 1024 * 1024
SB_TILE = 512
SB_TILE_BWD = 512
SB_SCAN = 256
SB_HEADS = 2
MASKED_LOG = -1e30
RMS_EPS = 1e-6

ADAM_LR = 0.001
ADAM_B1 = 0.9
ADAM_B2 = 0.999
ADAM_EPS = 1e-08
ADAM_WD = 0.01
ADAM_STEP = 10

NT_DIMS = (((1,), (1,)), ((), ()))
TN_DIMS = (((0,), (0,)), ((), ()))


def _params(semantics=None):
    return pltpu.CompilerParams(dimension_semantics=semantics, vmem_limit_bytes=VMEM_LIMIT)


def _tile(n, preferred):
    t = min(n, preferred)
    assert n % t == 0, (n, t)
    return t


def _sigmoid(x):
    return 1.0 / (1.0 + jnp.exp(-x))


def _silu(x):
    s = _sigmoid(x)
    return x * s, s * (1.0 + x * (1.0 - s))


def _gelu(x):
    k = math.sqrt(2.0 / math.pi)
    x2 = x * x
    t = jnp.tanh(k * (x + 0.044715 * (x * x2)))
    cdf = 0.5 * (1.0 + t)
    return x * cdf, cdf + 0.5 * x * (1.0 - t * t) * (k * (1.0 + 3.0 * 0.044715 * x2))


def _rms_scale(x):
    return lax.rsqrt(jnp.mean(x * x, axis=-1, keepdims=True) + RMS_EPS)


def _iotas(n):
    return (lax.broadcasted_iota(jnp.int32, (n, n), 0), lax.broadcasted_iota(jnp.int32, (n, n), 1))


def _adamw(w, g, m, v):
    m = ADAM_B1 * m + (1.0 - ADAM_B1) * g
    v = ADAM_B2 * v + (1.0 - ADAM_B2) * (g * g)
    m_hat = m / (1.0 - ADAM_B1 ** ADAM_STEP)
    v_hat = v / (1.0 - ADAM_B2 ** ADAM_STEP)
    delta = -ADAM_LR * (m_hat / (jnp.sqrt(v_hat) + ADAM_EPS) + ADAM_WD * w)
    return delta, m, v


def _dot(a, b):
    return jnp.dot(a, b, preferred_element_type=F32)


def _dot_nt(a, b):
    return lax.dot_general(a, b, NT_DIMS, preferred_element_type=F32)


def _dot_tn(a, b):
    return lax.dot_general(a, b, TN_DIMS, preferred_element_type=F32)


def _sb_logs(raw, scale, valid):
    z = (raw * scale).astype(BF16)
    log_beta = jnp.minimum(z, 0) - jnp.log(1 + jnp.exp(-jnp.abs(z)))
    log_rest = log_beta - z
    if valid is not None:
        log_beta = jnp.where(valid, log_beta, MASKED_LOG)
        log_rest = jnp.where(valid, log_rest, 0)
    return log_beta, log_rest


def _me():
    return lax.axis_index("x"), lax.axis_index("y"), lax.axis_index("c")


def _slot(p):
    return 4 * p[0] + 2 * p[1] + p[2]


def _peer(me, k):
    flips = ((k >> 2) & 1, (k >> 1) & 1, k & 1)
    return tuple(1 - a if f else a for a, f in zip(me, flips))


def _stack_exchange(me, st_in, st_out, n_whole, send_sems, recv_sems, local_sems, arrivals=True):
    mine = _slot(me)
    ns = len(st_in)
    part = lambda a, dev: st_in[a] if a >= ns - n_whole else st_in[a].at[_slot(dev)]
    local = [pltpu.make_async_copy(part(a, me), st_out[a].at[mine], local_sems.at[a]) for a in range(ns)]
    remote, landed = [], []
    for k in range(1, N_DEV):
        peer = _peer(me, k)
        for a in range(ns):
            sems = dict(send_sem=send_sems.at[7 * a + k - 1], recv_sem=recv_sems.at[7 * a + k - 1])
            remote.append(pltpu.make_async_remote_copy(
                src_ref=part(a, peer), dst_ref=st_out[a].at[mine],
                device_id=peer, device_id_type=MESH_ID, **sems))
            if arrivals:
                got = st_out[a].at[_slot(peer)]
                landed.append(pltpu.make_async_remote_copy(
                    src_ref=got, dst_ref=got, device_id=me, device_id_type=MESH_ID, **sems))
    return local, remote, landed


def _gather_in_proj(x2d, norm_in, w_in_sh, wo_shards, my_slot):
    n, d = x2d.shape
    esh = w_in_sh.shape[1]
    pw = 2 * esh
    n_chip = N_DEV // 2
    tm = _tile(n, 1024)
    n_i = n // tm
    mid = n_i // 2
    no = len(wo_shards)
    flip_at = lambda st: jnp.where(st == 1, 2, jnp.where(st == 2, 1, jnp.where(st == 3, 3, 0)))

    def body(me_ref, x_ref, g_ref, win_ref, *refs):
        del me_ref
        wo_in = refs[:no]
        proj_ref, h_ref, wg_ref = refs[no:no + 3]
        wo_out = refs[no + 3:2 * no + 3]
        wv, stage = refs[2 * no + 3:2 * no + 5]
        wo_stage = refs[2 * no + 5:3 * no + 5]
        send_sems, recv_sems, pair_sems, own_sems, wo_send, wo_recv, wo_local = refs[3 * no + 5:]
        st, i = pl.program_id(0), pl.program_id(1)
        x, y, c = _me()
        me, sibling = (x, y, c), (x, y, 1 - c)
        chips = [(1 - x, y), (x, 1 - y), (1 - x, 1 - y)]
        chip_id = lambda p: 2 * p[0] + p[1]

        def window(chip, core):
            return wv.at[chip_id(chip), :, pl.ds(pl.multiple_of(core * esh, LANE), esh)]

        def copy(k, block, to, src=None):
            dst = window(block[:2], block[2])
            return pltpu.make_async_remote_copy(
                src_ref=dst if src is None else src, dst_ref=dst,
                send_sem=send_sems.at[k], recv_sem=recv_sems.at[k], device_id=to, device_id_type=MESH_ID)

        def wo_copy(a, k, block, to, src=None):
            dst = wo_out[a].at[_slot(block)]
            return pltpu.make_async_remote_copy(
                src_ref=dst if src is None else src, dst_ref=dst,
                send_sem=wo_send.at[7 * a + k], recv_sem=wo_recv.at[7 * a + k], device_id=to, device_id_type=MESH_ID)

        def own_copy():
            return pltpu.make_async_copy(stage, window((x, y), c), own_sems.at[0])

        def wo_own_copy(a):
            return pltpu.make_async_copy(wo_stage[a], wo_out[a].at[_slot(me)], wo_local.at[a])

        def pair_copy(step):
            chip = jnp.bitwise_xor(chip_id((x, y)), flip_at(step))
            return pltpu.make_async_copy(wv.at[chip], wg_ref.at[:, pl.ds(pl.multiple_of(chip * pw, LANE), pw)],
                                         pair_sems.at[step])

        first = jnp.logical_and(st == 0, i == 0)

        @pl.when(first)
        def _():
            stage[...] = win_ref[...].astype(BF16)
            own_copy().start()
            copy(0, me, sibling, src=stage).start()
            for j in range(2):
                copy(1 + j, me, (*chips[j], c), src=stage).start()
            own_copy().wait()
            copy(0, sibling, me).wait_recv()
            pair_copy(0).start()

        for s_ in range(n_chip - 1):
            @pl.when(jnp.logical_and(st == s_, i == mid))
            def _():
                copy(1 + s_, (*chips[s_], c), me).wait_recv()
                copy(4 + s_, (*chips[s_], c), sibling).start()
                if s_ == 0:
                    copy(3, me, (*chips[2], c), src=stage).start()
                if s_ == 1:
                    for a in range(no):
                        wo_stage[a][...] = wo_in[a][...].astype(BF16)
                        wo_own_copy(a).start()
                        wo_copy(a, 0, me, sibling, src=wo_stage[a]).start()
                        for j, chip in enumerate(chips):
                            wo_copy(a, 1 + j, me, (*chip, c), src=wo_stage[a]).start()
                if s_ == 2:
                    for a in range(no):
                        for j, chip in enumerate(chips):
                            wo_copy(a, 1 + j, (*chip, c), me).wait_recv()
                            wo_copy(a, 4 + j, (*chip, c), sibling).start()

        for s_ in range(1, n_chip):
            @pl.when(jnp.logical_and(st == s_, i == 0))
            def _():
                copy(3 + s_, (*chips[s_ - 1], 1 - c), me).wait_recv()
                pair_copy(s_).start()

        xv = x_ref[...]
        h = (xv * _rms_scale(xv) * g_ref[...]).astype(BF16)

        @pl.when(st == 0)
        def _():
            h_ref[...] = h

        chip_now = jnp.bitwise_xor(chip_id((x, y)), flip_at(st))
        proj_ref[...] = _dot(h, wv[chip_now]).astype(BF16)

        @pl.when(jnp.logical_and(st == n_chip - 1, i == n_i - 1))
        def _():
            copy(0, me, sibling, src=stage).wait_send()
            for j, chip in enumerate(chips):
                copy(1 + j, me, (*chip, c), src=stage).wait_send()
                copy(4 + j, (*chip, c), sibling).wait_send()
            for s_ in range(n_chip):
                pair_copy(s_).wait()
            for a in range(no):
                wo_copy(a, 0, me, sibling, src=wo_stage[a]).wait_send()
                wo_copy(a, 0, sibling, me).wait_recv()
                for j, chip in enumerate(chips):
                    wo_copy(a, 1 + j, me, (*chip, c), src=wo_stage[a]).wait_send()
                    wo_copy(a, 4 + j, (*chip, c), sibling).wait_send()
                    wo_copy(a, 4 + j, (*chip, 1 - c), me).wait_recv()
                wo_own_copy(a).wait()

    any_spec = pl.BlockSpec(memory_space=pl.ANY)
    vmem = pl.BlockSpec(memory_space=pltpu.VMEM)
    grid_spec = pltpu.PrefetchScalarGridSpec(
        num_scalar_prefetch=1, grid=(n_chip, n_i),
        in_specs=[pl.BlockSpec((tm, d), lambda st, i, me: (i, 0)),
                  pl.BlockSpec((1, d), lambda st, i, me: (0, 0)), vmem] + [vmem] * no,
        out_specs=[pl.BlockSpec((tm, pw), lambda st, i, me: (i, jnp.bitwise_xor(me[0] // 2, flip_at(st)))),
                   pl.BlockSpec((tm, d), lambda st, i, me: (jnp.where(st == 0, i, n_i - 1), 0)),
                   any_spec] + [any_spec] * no,
        scratch_shapes=[pltpu.VMEM((n_chip, d, pw), BF16), pltpu.VMEM((d, esh), BF16)] + [
            pltpu.VMEM(s.shape, BF16) for s in wo_shards] + [
            pltpu.SemaphoreType.DMA((7,)), pltpu.SemaphoreType.DMA((7,)),
            pltpu.SemaphoreType.DMA((n_chip,)), pltpu.SemaphoreType.DMA((1,)),
            pltpu.SemaphoreType.DMA((7 * no,)), pltpu.SemaphoreType.DMA((7 * no,)),
            pltpu.SemaphoreType.DMA((no,))])
    return pl.pallas_call(
        body, name="gather_in_proj", grid_spec=grid_spec,
        out_shape=[SDS((n, n_chip * pw), BF16), SDS((n, d), BF16), SDS((d, n_chip * pw), BF16)] + [
            SDS((N_DEV,) + s.shape, BF16) for s in wo_shards],
        compiler_params=pltpu.CompilerParams(dimension_semantics=("arbitrary", "arbitrary"),
                                             vmem_limit_bytes=VMEM_LIMIT),
    )(my_slot, x2d, norm_in, w_in_sh, *wo_shards)


EXCHANGE_ORDER = ((4, 2, 5, 3, 6, 7, 1, 0), (2, 4, 3, 5, 7, 6, 1, 0))


def _owner_at(mine, j):
    k = 0
    for step in range(N_DEV - 1):
        k = jnp.where(j == step, jnp.where(mine % 2 == 0, EXCHANGE_ORDER[0][step], EXCHANGE_ORDER[1][step]), k)
    return jnp.bitwise_xor(mine, k)


def _dw_in_exchange(h, dproj, my_slot, packed):
    n, d = h.shape
    esh = dproj.shape[1] // N_DEV
    tk = _tile(n, 512)
    nk = n // tk
    last_j = N_DEV - 1
    depth = 4

    def body(me_ref, h_ref, dp_ref, pk_in, win_out, pk_out,
             acc, sendbuf, win_send, win_recv, send_sems, recv_sems, local_sems):
        del me_ref
        j, k = pl.program_id(0), pl.program_id(1)
        me = _me()
        mine = _slot(me)

        def pack_copies():
            local = pltpu.make_async_copy(pk_in, pk_out.at[mine], local_sems.at[0])
            remote = [pltpu.make_async_remote_copy(
                src_ref=pk_in, dst_ref=pk_out.at[mine], send_sem=send_sems.at[kk - 1], recv_sem=recv_sems.at[kk - 1],
                device_id=_peer(me, kk), device_id_type=MESH_ID) for kk in range(1, N_DEV)]
            return local, remote

        def shard_copy(jj):
            owner = _owner_at(mine, jj)
            return pltpu.make_async_remote_copy(
                src_ref=sendbuf.at[jj % depth], dst_ref=win_out.at[mine],
                send_sem=win_send.at[jj % depth], recv_sem=win_recv.at[mine],
                device_id=(owner // 4, (owner // 2) % 2, owner % 2), device_id_type=MESH_ID)

        def own_copy():
            return pltpu.make_async_copy(sendbuf.at[last_j % depth], win_out.at[mine], local_sems.at[1])

        @pl.when(jnp.logical_and(j == 0, k == 0))
        def _():
            local, remote = pack_copies()
            for cp in [local] + remote:
                cp.start()

        @pl.when(k == 0)
        def _():
            acc[...] = jnp.zeros_like(acc)

        acc[...] += _dot_tn(h_ref[...], dp_ref[...])

        @pl.when(k == nk - 1)
        def _():
            @pl.when(j >= depth)
            def _():
                shard_copy(j - depth).wait_send()

            sendbuf[j % depth] = acc[...].astype(BF16)

            @pl.when(j < last_j)
            def _():
                shard_copy(j).start()

            @pl.when(j == last_j)
            def _():
                own_copy().start()
                for jj in range(last_j - depth + 1, last_j):
                    shard_copy(jj).wait_send()
                own_copy().wait()
                for src in range(N_DEV):
                    @pl.when(src != mine)
                    def _():
                        landed = win_out.at[src]
                        pltpu.make_async_remote_copy(
                            src_ref=landed, dst_ref=landed, send_sem=win_send.at[0], recv_sem=win_recv.at[src],
                            device_id=me, device_id_type=MESH_ID).wait_recv()
                local, remote = pack_copies()
                for cp in remote:
                    cp.wait_send()
                for kk in range(1, N_DEV):
                    landed = pk_out.at[_slot(_peer(me, kk))]
                    pltpu.make_async_remote_copy(
                        src_ref=landed, dst_ref=landed, send_sem=send_sems.at[kk - 1], recv_sem=recv_sems.at[kk - 1],
                        device_id=me, device_id_type=MESH_ID).wait_recv()
                local.wait()

    any_spec = pl.BlockSpec(memory_space=pl.ANY)
    grid_spec = pltpu.PrefetchScalarGridSpec(
        num_scalar_prefetch=1, grid=(N_DEV, nk),
        in_specs=[pl.BlockSpec((tk, d), lambda j, k, me: (k, 0)),
                  pl.BlockSpec((tk, esh), lambda j, k, me: (k, _owner_at(me[0], j))), any_spec],
        out_specs=[any_spec] * 2,
        scratch_shapes=[pltpu.VMEM((d, esh), F32), pltpu.VMEM((depth, d, esh), BF16),
                        pltpu.SemaphoreType.DMA((depth,)), pltpu.SemaphoreType.DMA((N_DEV,)),
                        pltpu.SemaphoreType.DMA((N_DEV - 1,)), pltpu.SemaphoreType.DMA((N_DEV - 1,)),
                        pltpu.SemaphoreType.DMA((2,))])
    return pl.pallas_call(
        body, name="dw_in_exchange", grid_spec=grid_spec,
        out_shape=[SDS((N_DEV, d, esh), BF16), SDS((N_DEV,) + packed.shape, packed.dtype)],
        compiler_params=_params(("arbitrary", "arbitrary")),
    )(my_slot, h, dproj, packed)


def _finish_small(packs, late_packs, groups, chunk):
    rows = packs.shape[1]
    late = late_packs.shape[1]
    gc = groups * chunk

    def body(p_ref, l_ref, sum_ref, loss_ref):
        row, col = _iotas(chunk)
        tril = col <= row
        for g in range(groups):
            rs = slice(g * chunk, (g + 1) * chunk)
            tot = p_ref[0, rs, :]
            for dev in range(1, N_DEV):
                tot = tot + p_ref[dev, rs, :]
            sum_ref[rs, :] = jnp.where(tril, tot, 0.0)
        rs = slice(gc, rows)
        tot = p_ref[0, rs, :]
        for dev in range(1, N_DEV):
            tot = tot + p_ref[dev, rs, :]
        sum_ref[rs, :] = tot
        loss_ref[...] = jnp.full((SUBLANE, LANE), jnp.sum(tot[rows - gc - SUBLANE:, :]), F32)
        tot = l_ref[0]
        for dev in range(1, N_DEV):
            tot = tot + l_ref[dev]
        sum_ref[rows:rows + late, :] = tot

    return pl.pallas_call(
        body, name="finish_small",
        out_shape=[SDS((rows + late, LANE), F32), SDS((SUBLANE, LANE), F32)],
        in_specs=[pl.BlockSpec(memory_space=pltpu.VMEM)] * 2,
        out_specs=[pl.BlockSpec(memory_space=pltpu.VMEM)] * 2,
        compiler_params=pltpu.CompilerParams(vmem_limit_bytes=VMEM_LIMIT),
    )(packs, late_packs)


def _branch_a_fwd(proj, norm_v, w_s, b_col):
    n = proj.shape[0]
    d = norm_v.shape[1]
    groups, chunk, _ = w_s.shape
    tr = _tile(n, 4 * chunk)

    def body(u_ref, v_ref, z_ref, gv_ref, ws_ref, b_ref, ya_ref, vn_s, pre_s):
        row, col = _iotas(chunk)
        tril = col <= row
        vg, _ = _gelu(v_ref[...].astype(F32))
        vn_s[...] = (vg * _rms_scale(vg) * gv_ref[...]).astype(BF16)
        ug, _ = _gelu(u_ref[...].astype(F32))
        sz, _ = _silu(z_ref[...].astype(F32))
        pre_s[...] = ug * sz
        for g in range(groups):
            wm = jnp.where(tril, ws_ref[g], 0.0).astype(BF16)
            cs = slice(g * chunk, (g + 1) * chunk)
            for c in range(tr // chunk):
                rs = slice(c * chunk, (c + 1) * chunk)
                mixed = _dot(wm, vn_s[rs, cs]) + b_ref[g]
                ya_ref[rs, cs] = (pre_s[rs, cs] * mixed).astype(BF16)

    seg = lambda k: pl.BlockSpec((tr, d), lambda i: (i, k))
    return pl.pallas_call(
        body, name="branch_a_fwd", grid=(n // tr,),
        in_specs=[seg(0), seg(1), seg(2),
                  pl.BlockSpec((1, d), lambda i: (0, 0)),
                  pl.BlockSpec((groups, chunk, chunk), lambda i: (0, 0, 0)),
                  pl.BlockSpec((groups, chunk, 1), lambda i: (0, 0, 0))],
        out_specs=pl.BlockSpec((tr, d), lambda i: (i, 0)),
        out_shape=SDS((n, d), BF16),
        scratch_shapes=[pltpu.VMEM((tr, d), BF16), pltpu.VMEM((tr, d), F32)],
        compiler_params=_params(("parallel",)),
    )(proj, proj, proj, norm_v, w_s, b_col)


def _sb_fwd(proj, batch, seq, d, hd):
    heads = d // hd
    t = _tile(seq, SB_TILE)
    sw = _tile(t, SB_SCAN)
    nb = t // sw
    scale = hd ** -0.5
    nblk = seq // t
    nh = SB_HEADS
    wide = nh * hd
    cols = [slice(hh * hd, (hh + 1) * hd) for hh in range(nh)]

    def body(qs, k_ref, vs, zb_ref, yb_ref, o_ref, tot_ref, kts, later, acc):
        for jb in range(nblk):
            kts[jb] = k_ref[jb * t:(jb + 1) * t, :].astype(F32).T.astype(BF16)
        row, col = _iotas(t)
        later[...] = (row[:sw, :sw] > col[:sw, :sw]).astype(BF16)

        def qblock(i, carry):
            r0 = pl.multiple_of(i * t, t)

            def tile(j, runs):
                c0 = pl.multiple_of(j * t, t)
                logs = [_sb_logs(_dot(qs[pl.ds(r0, t), cs], kts[j, cs, :]), scale, None) for cs in cols]
                scans = [_dot(jnp.concatenate([logs[hh][1][:, b * sw:(b + 1) * sw] for b in range(nb)], axis=0),
                              later[...]) for hh in range(nh)]
                new_runs = []
                for hh in range(nh):
                    after = runs[hh]
                    blocks = [None] * nb
                    for b in reversed(range(nb)):
                        ks_ = slice(b * sw, (b + 1) * sw)
                        inside = scans[hh][b * t:(b + 1) * t]
                        blocks[b] = jnp.exp(logs[hh][0][:, ks_].astype(F32) + inside + after).astype(BF16)
                        after = after + inside[:, 0:1] + logs[hh][1][:, b * sw:b * sw + 1].astype(F32)
                    new_runs.append(after)
                    acc[:, cols[hh]] += _dot(jnp.concatenate(blocks, axis=1), vs[pl.ds(c0, t), cols[hh]])
                return tuple(new_runs)

            def diagonal_tile():
                starts = [b * sw for b in range(nb)]
                logs = [[_sb_logs(_dot(qs[pl.ds(r0 + s, t - s), cs], kts[i, cs, s:s + sw]), scale,
                                  col[:t - s, :sw] < row[:t - s, :sw]) for s in starts] for cs in cols]
                scans = [_dot(jnp.concatenate([lr for _, lr in logs[hh]], axis=0), later[...]) for hh in range(nh)]
                new_runs = []
                offs = [sum(t - s for s in starts[:b]) for b in range(nb)]
                for hh in range(nh):
                    after = jnp.zeros((t, 1), F32)
                    ws = [None] * nb
                    for b in reversed(range(nb)):
                        s = starts[b]
                        lb, lr = logs[hh][b]
                        inside = scans[hh][offs[b]:offs[b] + t - s]
                        ws[b] = jnp.exp(lb.astype(F32) + inside + after[s:]).astype(BF16)
                        total = inside[:, 0:1] + lr[:, 0:1].astype(F32)
                        after = after + total if s == 0 else jnp.concatenate([after[:s], after[s:] + total], axis=0)
                    new_runs.append(after)
                    acc[:, cols[hh]] = _dot(ws[0], vs[pl.ds(r0, sw), cols[hh]])
                    for b in range(1, nb):
                        acc[starts[b]:, cols[hh]] += _dot(ws[b], vs[pl.ds(r0 + starts[b], sw), cols[hh]])
                return tuple(new_runs)

            runs = diagonal_tile()
            runs = lax.fori_loop(0, i, lambda jj, rs: tile(i - 1 - jj, rs), runs)
            for hh in range(nh):
                out = acc[:, cols[hh]]
                o_ref[pl.ds(r0, t), cols[hh]] = out.astype(BF16)
                tot_ref[hh, pl.ds(r0, t), :] = runs[hh]
                sz, _ = _silu(zb_ref[pl.ds(r0, t), cols[hh]].astype(F32))
                yb_ref[pl.ds(r0, t), cols[hh]] = (out * sz).astype(BF16)
            return carry

        lax.fori_loop(0, nblk, qblock, 0)

    col0 = d // wide
    seg = lambda k: pl.BlockSpec((seq, wide), lambda b, h: (b, k * col0 + h))
    return pl.pallas_call(
        body, name="sb_fwd", grid=(batch, heads // nh),
        in_specs=[seg(3), seg(4), seg(5), seg(6)],
        out_specs=[pl.BlockSpec((seq, wide), lambda b, h: (b, h))] * 2 + [
            pl.BlockSpec((nh, seq, 1), lambda b, h: (b * (heads // nh) + h, 0, 0))],
        out_shape=[SDS((batch * seq, d), BF16), SDS((batch * seq, d), BF16), SDS((batch * heads, seq, 1), F32)],
        scratch_shapes=[pltpu.VMEM((nblk, wide, t), BF16), pltpu.VMEM((sw, sw), BF16), pltpu.VMEM((t, wide), F32)],
        compiler_params=_params(("parallel", "parallel")),
    )(proj, proj, proj, proj)


def _tail(x2d, tgt, ya, yb, proj, w_oa, w_ob, w_out, norm_final):
    n, d = x2d.shape
    e = proj.shape[1]
    tm = _tile(n, 256)
    steps = n // tm

    def body(x_ref, t_ref, ya_ref, yb_ref, ga_ref, gb_ref, woa_ref, wob_ref, wout_ref, gf_ref,
             dproj_ref, dx2_ref, dya_ref, dyb_ref, mrg_ref, dpa_ref, dpb_ref, loss_ref, dgf_ref, dg_s, dg_sems):
        i = pl.program_id(0)

        def gate_copy(step):
            rows_ = pl.ds(pl.multiple_of(step * tm, tm), tm)
            return pltpu.make_async_copy(dg_s.at[step % 2], dproj_ref.at[rows_, pl.ds(7 * d, 2 * d)],
                                         dg_sems.at[step % 2])

        @pl.when(i == 0)
        def _():
            loss_ref[...] = jnp.zeros_like(loss_ref)
            dgf_ref[...] = jnp.zeros_like(dgf_ref)

        @pl.when(i >= 2)
        def _():
            gate_copy(i - 2).wait()

        pa = _dot(ya_ref[...], woa_ref[...])
        pb = _dot(yb_ref[...], wob_ref[...])
        sa = _sigmoid(ga_ref[...].astype(F32))
        sb = _sigmoid(gb_ref[...].astype(F32))
        merged = (sa * pa + sb * pb).astype(BF16)
        mrg_ref[...] = merged
        x2 =x_ref[...] + _dot(merged, wout_ref[...])
        r2 = _rms_scale(x2)
        xh = x2 * r2
        gf = gf_ref[...]
        diff = xh * gf - t_ref[...]
        loss_ref[...] += jnp.sum(diff * diff, axis=0, keepdims=True) * (0.5 / d)
        dy = diff * (1.0 / d)
        dgf_ref[...] += jnp.sum(dy * xh, axis=0, keepdims=True)
        dxh = dy * gf
        dx2 = r2 * (dxh - xh * jnp.mean(dxh * xh, axis=-1, keepdims=True))
        dx2_ref[...] = dx2
        dm = _dot_nt(dx2.astype(BF16), wout_ref[...])
        dpa = (dm * sa).astype(BF16)
        dpb = (dm * sb).astype(BF16)
        dpa_ref[...] = dpa
        dpb_ref[...] = dpb
        dg_s[i % 2, :, 0:d] = (dm * pa * (sa * (1.0 - sa))).astype(BF16)
        dg_s[i % 2, :, d:2 * d] = (dm * pb * (sb * (1.0 - sb))).astype(BF16)
        gate_copy(i).start()
        dya_ref[...] = _dot_nt(dpa, woa_ref[...]).astype(BF16)
        dyb_ref[...] = _dot_nt(dpb, wob_ref[...]).astype(BF16)

        @pl.when(i == steps - 1)
        def _():
            if steps >= 2:
                gate_copy(i - 1).wait()
            gate_copy(i).wait()

    rows = lambda k=0: pl.BlockSpec((tm, d), lambda i: (i, k))
    full = pl.BlockSpec((d, d), lambda i: (0, 0))
    vec = pl.BlockSpec((1, d), lambda i: (0, 0))
    return pl.pallas_call(
        body, name="tail", grid=(steps,),
        in_specs=[rows(), rows(), rows(), rows(), rows(7), rows(8), full, full, full, vec],
        out_specs=[pl.BlockSpec(memory_space=pl.ANY),
                   rows(), rows(), rows(), rows(), rows(), rows(), vec, vec],
        out_shape=[SDS((n, e), BF16), SDS((n, d), F32), SDS((n, d), BF16), SDS((n, d), BF16),
                   SDS((n, d), BF16), SDS((n, d), BF16), SDS((n, d), BF16),
                   SDS((1, d), F32), SDS((1, d), F32)],
        scratch_shapes=[pltpu.VMEM((2, tm, 2 * d), BF16), pltpu.SemaphoreType.DMA((2,))],
        compiler_params=_params(("arbitrary",)),
    )(x2d, tgt, ya, yb, proj, proj, w_oa, w_ob, w_out, norm_final)


def _dw_o(pairs):
    n, d = pairs[0][0].shape
    tk = _tile(n, 1024)
    nk = n // tk
    npair = len(pairs)

    def body(*refs):
        a_refs, b_refs = refs[:npair], refs[npair:2 * npair]
        o_ref, acc = refs[2 * npair], refs[2 * npair + 1]
        p, k = pl.program_id(0), pl.program_id(1)

        @pl.when(k == 0)
        def _():
            acc[...] = jnp.zeros_like(acc)

        for q in range(npair):
            @pl.when(p == q)
            def _():
                acc[...] += _dot_tn(a_refs[q][...], b_refs[q][...].astype(BF16))

        @pl.when(k == nk - 1)
        def _():
            o_ref[0] = acc[...].astype(BF16)

    def tiles(q):
        return pl.BlockSpec((tk, d), lambda p, k: (jnp.where(p == q, k, jnp.where(p < q, 0, nk - 1)), 0))

    return pl.pallas_call(
        body, name="dw_o", grid=(npair, nk),
        in_specs=[tiles(q) for q in range(npair)] * 2,
        out_specs=pl.BlockSpec((1, d, d), lambda p, k: (p, 0, 0)),
        out_shape=SDS((npair, d, d), BF16),
        scratch_shapes=[pltpu.VMEM((d, d), F32)],
        compiler_params=_params(("arbitrary", "arbitrary")),
    )(*[a for a, _ in pairs], *[b for _, b in pairs])


def _sb_bwd(proj, o, dyb, tot, dproj, stacks, batch, seq, d, hd):
    heads = d // hd
    t = _tile(seq, SB_TILE_BWD)
    sw = _tile(t, SB_SCAN)
    nb = t // sw
    scale = hd ** -0.5
    nblk = seq // t
    nh = SB_HEADS
    wide = nh * hd
    hs = range(nh)
    cols = [slice(hh * hd, (hh + 1) * hd) for hh in hs]
    blocks = [slice(b * sw, (b + 1) * sw) for b in range(nb)]
    last = slice(sw - 1, sw)

    def compute(qs, ks, v_ref, zb_ref, dyb_ref, tot_ref, kts, vts, dos, res, upto, before, dq):
        for jb in range(nblk):
            rows = slice(jb * t, (jb + 1) * t)
            kts[jb] = ks[rows, :].astype(F32).T.astype(BF16)
            vts[jb] = v_ref[rows, :].astype(F32).T.astype(BF16)
        sz, _ = _silu(zb_ref[...].astype(F32))
        dos[...] = (dyb_ref[...].astype(F32) * sz).astype(BF16)
        res[1] = jnp.zeros((seq, wide), F32)
        res[2] = jnp.zeros((seq, wide), F32)
        row, col = _iotas(t)
        upto[...] = (row[:sw, :sw] <= col[:sw, :sw]).astype(BF16)
        before[...] = (row[:sw, :sw] < col[:sw, :sw]).astype(BF16)

        def qblock(i, carry):
            r0 = pl.multiple_of(i * t, t)

            def tile(j, sums):
                c0 = pl.multiple_of(j * t, t)
                q_i = [qs[pl.ds(r0, t), cs] for cs in cols]
                do_i = [dos[pl.ds(r0, t), cs] for cs in cols]
                logs = [_sb_logs(_dot(q_i[hh], kts[j, cols[hh], :]), scale, None) for hh in hs]
                dw = [_dot(do_i[hh], vts[j, cols[hh], :]) for hh in hs]
                scans = [_dot(jnp.concatenate([logs[hh][1][:, ks_] for ks_ in blocks], axis=0), upto[...]) for hh in hs]
                ws, gs, new_runs = [], [], []
                for hh in hs:
                    left = tot_ref[hh, pl.ds(r0, t), :] - sums[hh][0]
                    w_b, g_b = [], []
                    for b, ks_ in enumerate(blocks):
                        inside = scans[hh][b * t:(b + 1) * t]
                        w = jnp.exp(logs[hh][0][:, ks_].astype(F32) + (left - inside))
                        w_b.append(w.astype(BF16))
                        g_b.append((dw[hh][:, ks_] * w).astype(BF16))
                        left = left - inside[:, last]
                    ws.append(jnp.concatenate(w_b, axis=1))
                    gs.append(g_b)
                    new_runs.append(tot_ref[hh, pl.ds(r0, t), :] - left)
                gscans = [_dot(jnp.concatenate(gs[hh], axis=0), before[...]) for hh in hs]
                dzs, new_gruns = [], []
                for hh in hs:
                    g_before = sums[hh][1]
                    dz_b = []
                    for b, ks_ in enumerate(blocks):
                        inside = gscans[hh][b * t:(b + 1) * t]
                        beta = jnp.exp(logs[hh][0][:, ks_]).astype(F32)
                        g = gs[hh][b].astype(F32)
                        dz_b.append(((g - (g + inside + g_before) * beta) * scale).astype(BF16))
                        g_before = g_before + inside[:, last] + g[:, last]
                    dzs.append(jnp.concatenate(dz_b, axis=1))
                    new_gruns.append(g_before)
                for hh in hs:
                    res[2, pl.ds(c0, t), cols[hh]] += _dot_tn(ws[hh], do_i[hh])
                for hh in hs:
                    res[1, pl.ds(c0, t), cols[hh]] += _dot_tn(dzs[hh], q_i[hh])
                for hh in hs:
                    dq[:, cols[hh]] += _dot(dzs[hh], ks[pl.ds(c0, t), cols[hh]])
                return tuple((new_runs[hh], new_gruns[hh]) for hh in hs)

            def diagonal_tile(sums):
                starts = [b * sw for b in range(nb)]
                offs = [sum(t - s for s in starts[:b]) for b in range(nb)]
                q_b = [[qs[pl.ds(r0 + s, t - s), cs] for s in starts] for cs in cols]
                do_b = [[dos[pl.ds(r0 + s, t - s), cs] for s in starts] for cs in cols]
                logs = [[_sb_logs(_dot(q_b[hh][b], kts[i, cols[hh], s:s + sw]), scale,
                                  col[:t - s, :sw] < row[:t - s, :sw]) for b, s in enumerate(starts)] for hh in hs]
                dw = [[_dot(do_b[hh][b], vts[i, cols[hh], s:s + sw]) for b, s in enumerate(starts)] for hh in hs]
                scans = [_dot(jnp.concatenate([lr for _, lr in logs[hh]], axis=0), upto[...]) for hh in hs]
                ws, gs = [], []
                for hh in hs:
                    left = tot_ref[hh, pl.ds(r0, t), :] - sums[hh][0]
                    w_b, g_b = [], []
                    for b, s in enumerate(starts):
                        inside = scans[hh][offs[b]:offs[b] + t - s]
                        w = jnp.exp(logs[hh][b][0].astype(F32) + (left[s:] - inside))
                        w_b.append(w.astype(BF16))
                        g_b.append((dw[hh][b] * w).astype(BF16))
                        total = inside[:, last]
                        left = left - total if s == 0 else jnp.concatenate([left[:s], left[s:] - total], axis=0)
                    ws.append(w_b)
                    gs.append(g_b)
                gscans = [_dot(jnp.concatenate(gs[hh], axis=0), before[...]) for hh in hs]
                dzs = []
                for hh in hs:
                    g_before = sums[hh][1]
                    dz_b = []
                    for b, s in enumerate(starts):
                        inside = gscans[hh][offs[b]:offs[b] + t - s]
                        beta = jnp.exp(logs[hh][b][0]).astype(F32)
                        g = gs[hh][b].astype(F32)
                        dz_b.append(((g - (g + inside + g_before[s:]) * beta) * scale).astype(BF16))
                        total = inside[:, last] + g[:, last]
                        g_before = g_before + total if s == 0 else jnp.concatenate(
                            [g_before[:s], g_before[s:] + total], axis=0)
                    dzs.append(dz_b)
                for hh in hs:
                    for b, s in enumerate(starts):
                        res[2, pl.ds(r0 + s, sw), cols[hh]] += _dot_tn(ws[hh][b], do_b[hh][b])
                for hh in hs:
                    for b, s in enumerate(starts):
                        res[1, pl.ds(r0 + s, sw), cols[hh]] += _dot_tn(dzs[hh][b], q_b[hh][b])
                for hh in hs:
                    for b, s in enumerate(starts):
                        dq[s:, cols[hh]] += _dot(dzs[hh][b], ks[pl.ds(r0 + s, sw), cols[hh]])

            zero = jnp.zeros((t, 1), F32)
            dq[...] = jnp.zeros_like(dq)
            sums = lax.fori_loop(0, i, tile, ((zero, zero),) * nh)
            diagonal_tile(sums)
            res[0, pl.ds(r0, t), :] = dq[...]
            return carry

        lax.fori_loop(0, nblk, qblock, 0)

    pairs = heads // nh

    ns = len(stacks)

    def body(qs, ks, v_ref, zb_ref, o_ref, dyb_ref, tot_ref, dproj_in, *refs):
        del dproj_in
        st_in, out_ref, st_out = refs[:ns], refs[ns], refs[ns + 1:2 * ns + 1]
        (kts, vts, dos, res, upto, before, dq, stage, stage_sems,
         send_sems, recv_sems, local_sems) = refs[2 * ns + 1:]
        step = pl.program_id(0) * pairs + pl.program_id(1)
        exchange = functools.partial(_stack_exchange, _me(), st_in, st_out, 1, send_sems, recv_sems, local_sems)

        @pl.when(step == 0)
        def _():
            local, remote, _ = exchange(arrivals=False)
            for cp in local + remote:
                cp.start()

        def out_copies(s):
            rows_ = pl.ds(pl.multiple_of((s // pairs) * seq, seq), seq)
            return [pltpu.make_async_copy(
                stage.at[k], out_ref.at[rows_, pl.ds(pl.multiple_of((3 + k) * d + (s % pairs) * wide, wide), wide)],
                stage_sems.at[k]) for k in range(4)]

        compute(qs, ks, v_ref, zb_ref, dyb_ref, tot_ref, kts, vts, dos, res, upto, before, dq)

        @pl.when(step > 0)
        def _():
            for cp in out_copies(step - 1):
                cp.wait()

        for k in range(3):
            stage[k] = res[k].astype(BF16)
        _, dsz = _silu(zb_ref[...].astype(F32))
        stage[3] = (dyb_ref[...].astype(F32) * o_ref[...].astype(F32) * dsz).astype(BF16)
        for cp in out_copies(step):
            cp.start()

        @pl.when(step == batch * pairs - 1)
        def _():
            for cp in out_copies(step):
                cp.wait()
            local, remote, landed = exchange()
            for cp in remote:
                cp.wait_send()
            for cp in landed:
                cp.wait_recv()
            for cp in local:
                cp.wait()

    col0 = d // wide
    seg = lambda k: pl.BlockSpec((seq, wide), lambda b, h: (b, k * col0 + h))
    head = pl.BlockSpec((seq, wide), lambda b, h: (b, h))
    any_spec = pl.BlockSpec(memory_space=pl.ANY)
    return pl.pallas_call(
        body, name="sb_bwd", grid=(batch, pairs),
        in_specs=[seg(3), seg(4), seg(5), seg(6), head, head,
                  pl.BlockSpec((nh, seq, 1), lambda b, h: (b * pairs + h, 0, 0)), any_spec] + [any_spec] * ns,
        out_specs=[any_spec] * (ns + 1),
        out_shape=[SDS(dproj.shape, dproj.dtype)] + [SDS(s.shape, s.dtype) for s in stacks[:-1]] + [
            SDS((N_DEV,) + stacks[-1].shape, stacks[-1].dtype)],
        input_output_aliases={7: 0},
        scratch_shapes=[pltpu.VMEM((nblk, wide, t), BF16)] * 2 + [
            pltpu.VMEM((seq, wide), BF16), pltpu.VMEM((3, seq, wide), F32),
            pltpu.VMEM((sw, sw), BF16), pltpu.VMEM((sw, sw), BF16), pltpu.VMEM((t, wide), F32),
            pltpu.VMEM((4, seq, wide), BF16), pltpu.SemaphoreType.DMA((4,)),
            pltpu.SemaphoreType.DMA((7 * ns,)), pltpu.SemaphoreType.DMA((7 * ns,)),
            pltpu.SemaphoreType.DMA((ns,))],
        compiler_params=_params(("arbitrary", "arbitrary")),
    )(proj, proj, proj, proj, o, dyb, tot, dproj, *stacks)


def _branch_a_bwd(proj, dya, norm_v, w_s, b_col, dproj):
    n = proj.shape[0]
    d = norm_v.shape[1]
    groups, chunk, _ = w_s.shape
    tr = _tile(n, 2 * chunk)

    def body(u_ref, v_ref, z_ref, dya_ref, gv_ref, ws_ref, b_ref, dproj_in,
             out_ref, dws_ref, dbias_ref, dgv_ref, vn_s, dmix_s, dvn_s, db_ref):
        del dproj_in

        @pl.when(pl.program_id(0) == 0)
        def _():
            dws_ref[...] = jnp.zeros_like(dws_ref)
            db_ref[...] = jnp.zeros_like(db_ref)
            dgv_ref[...] = jnp.zeros_like(dgv_ref)

        row, col = _iotas(chunk)
        tril = col <= row
        u, v, z, dya_v = (r[...].astype(F32) for r in (u_ref, v_ref, z_ref, dya_ref))
        gv = gv_ref[...]
        vg, dvg_dv = _gelu(v)
        r = _rms_scale(vg)
        vh = vg * r
        vn_s[...] = (vh * gv).astype(BF16)
        ug, dug_du = _gelu(u)
        sz, dsz = _silu(z)
        dmix_s[...] = dya_v * ug * sz
        for g in range(groups):
            wm = jnp.where(tril, ws_ref[g], 0.0).astype(BF16)
            cs = slice(g * chunk, (g + 1) * chunk)
            for c in range(tr // chunk):
                rs = slice(c * chunk, (c + 1) * chunk)
                vn = vn_s[rs, cs]
                mixed = _dot(wm, vn) + b_ref[g]
                dmix = dmix_s[rs, cs]
                dmix16 = dmix.astype(BF16)
                dws_ref[g] += _dot_nt(dmix16, vn)
                db_ref[g] += dmix
                dvn_s[rs, cs] = _dot_tn(wm, dmix16)
                t_u = dya_v[rs, cs] * mixed
                out_ref[rs, g * chunk:(g + 1) * chunk] = (t_u * sz[rs, cs] * dug_du[rs, cs]).astype(BF16)
                out_ref[rs, 2 * d + g * chunk:2 * d + (g + 1) * chunk] = (t_u * ug[rs, cs] * dsz[rs, cs]).astype(BF16)
        dvn = dvn_s[...]
        dgv_ref[...] += jnp.sum(dvn * vh, axis=0, keepdims=True)
        dvh = dvn * gv
        dvg = r * (dvh - vh * jnp.mean(dvh * vh, axis=-1, keepdims=True))
        out_ref[:, d:2 * d] = (dvg * dvg_dv).astype(BF16)

        @pl.when(pl.program_id(0) == n // tr - 1)
        def _():
            for g in range(groups):
                dbias_ref[g:g + 1, :] = jnp.sum(db_ref[g].T, axis=0, keepdims=True)

    seg = lambda k: pl.BlockSpec((tr, d), lambda i: (i, k))
    return pl.pallas_call(
        body, name="branch_a_bwd", grid=(n // tr,),
        in_specs=[seg(0), seg(1), seg(2), seg(0),
                  pl.BlockSpec((1, d), lambda i: (0, 0)),
                  pl.BlockSpec((groups, chunk, chunk), lambda i: (0, 0, 0)),
                  pl.BlockSpec((groups, chunk, 1), lambda i: (0, 0, 0)),
                  pl.BlockSpec(memory_space=pl.ANY)],
        out_specs=[pl.BlockSpec((tr, 3 * d), lambda i: (i, 0)),
                   pl.BlockSpec((groups, chunk, chunk), lambda i: (0, 0, 0)),
                   pl.BlockSpec((groups, chunk), lambda i: (0, 0)),
                   pl.BlockSpec((1, d), lambda i: (0, 0))],
        out_shape=[SDS(dproj.shape, dproj.dtype), SDS((groups, chunk, chunk), F32),
                   SDS((groups, chunk), F32), SDS((1, d), F32)],
        input_output_aliases={7: 0},
        scratch_shapes=[pltpu.VMEM((tr, d), BF16), pltpu.VMEM((tr, d), F32), pltpu.VMEM((tr, d), F32),
                        pltpu.VMEM((groups, chunk, chunk), F32)],
        compiler_params=_params(("arbitrary",)),
    )(proj, proj, proj, dya, norm_v, w_s, b_col, dproj)


def _dx(dproj, wg_in, x2d, dx2, norm_in):
    n, d = x2d.shape
    nsh = N_DEV // 2
    esh = wg_in.shape[1] // nsh
    tm = _tile(n, 1024)

    def body(dp_ref, w_ref, x_ref, dx2_ref, g_ref, gx_ref, dg_ref, acc):
        i, k = pl.program_id(0), pl.program_id(1)

        @pl.when(jnp.logical_and(i == 0, k == 0))
        def _():
            dg_ref[...] = jnp.zeros_like(dg_ref)

        @pl.when(k == 0)
        def _():
            acc[...] = jnp.zeros_like(acc)

        acc[...] += _dot_nt(dp_ref[...], w_ref[...])

        @pl.when(k == nsh - 1)
        def _():
            dh = acc[...]
            x = x_ref[...]
            r = _rms_scale(x)
            xh = x * r
            dg_ref[...] += jnp.sum(dh * xh, axis=0, keepdims=True)
            dxh = dh * g_ref[...]
            gx_ref[...] = dx2_ref[...] + r * (dxh - xh * jnp.mean(dxh * xh, axis=-1, keepdims=True))

    rows = pl.BlockSpec((tm, d), lambda i, k: (i, 0))
    vec = pl.BlockSpec((1, d), lambda i, k: (0, 0))
    return pl.pallas_call(
        body, name="dx", grid=(n // tm, nsh),
        in_specs=[pl.BlockSpec((tm, esh), lambda i, k: (i, k)),
                  pl.BlockSpec((d, esh), lambda i, k: (0, k)), rows, rows, vec],
        out_specs=[rows, vec],
        out_shape=[SDS((n, d), F32), SDS((1, d), F32)],
        scratch_shapes=[pltpu.VMEM((tm, d), F32)],
        compiler_params=_params(("arbitrary", "arbitrary")),
    )(dproj, wg_in, x2d, dx2, norm_in)


def _adamw_outputs(g_ref, d_ref, m_ref, v_ref, g, w, m, v):
    delta, m2, v2 = _adamw(w, g, m, v)
    g_ref[...] = g
    d_ref[...] = delta
    m_ref[...] = m2
    v_ref[...] = v2


def _reduce_adamw(slots, w, m, v, name):
    _, r, c = slots.shape
    tr = _tile(r, 128)

    def body(s_ref, w_ref, m_ref, v_ref, g_out, d_out, m_out, v_out):
        g = s_ref[0].astype(F32)
        for k in range(1, N_DEV):
            g = g + s_ref[k].astype(F32)
        _adamw_outputs(g_out, d_out, m_out, v_out, g, w_ref[...], m_ref[...], v_ref[...])

    blk = pl.BlockSpec((tr, c), lambda i: (i, 0))
    return pl.pallas_call(
        body, name=name, grid=(r // tr,),
        in_specs=[pl.BlockSpec((N_DEV, tr, c), lambda i: (0, i, 0)), blk, blk, blk],
        out_specs=[blk] * 4,
        out_shape=[SDS((r, c), F32)] * 4,
        compiler_params=_params(("parallel",)),
    )(slots, w, m, v)


def _adamw_small(g, w, m, v, name):
    def body(g_ref, w_ref, m_ref, v_ref, g_out, d_out, m_out, v_out):
        _adamw_outputs(g_out, d_out, m_out, v_out, g_ref[...], w_ref[...], m_ref[...], v_ref[...])

    return pl.pallas_call(
        body, name=name,
        out_shape=[SDS(g.shape, F32)] * 4,
        in_specs=[pl.BlockSpec(memory_space=pltpu.VMEM)] * 4,
        out_specs=[pl.BlockSpec(memory_space=pltpu.VMEM)] * 4,
    )(g, w, m, v)


def kernel(x, norm_in, w_in, norm_v, w_s, b_s, w_o_gmlp, w_o_sb, w_out, norm_final, loss_target, m_norm_in, m_w_in, m_norm_v, m_w_s, m_b_s, m_w_o_gmlp, m_w_o_sb, m_w_out, m_norm_final, v_norm_in, v_w_in, v_norm_v, v_w_s, v_b_s, v_w_o_gmlp, v_w_o_sb, v_w_out, v_norm_final):
    batch, seq, d = x.shape
    n = batch * seq
    groups, chunk = w_s.shape[1], w_s.shape[2]
    hd = LANE
    x2d = x.reshape(n, d)
    tgt = loss_target.reshape(n, d)
    b_col = b_s[0].reshape(groups, chunk, 1)
    norm_final2 = norm_final.reshape(1, d)

    my_slot = _slot(_me()).astype(jnp.int32).reshape(1)
    proj, h, wg_in, wg_oa, wg_ob, wg_out = _gather_in_proj(
        x2d, norm_in, w_in[0], [w_o_gmlp[0], w_o_sb[0], w_out[0]], my_slot)
    rsh = wg_oa.shape[1]
    wf_oa, wf_ob, wf_out = (w.reshape(N_DEV * rsh, d) for w in (wg_oa, wg_ob, wg_out))
    ya = _branch_a_fwd(proj, norm_v, w_s[0], b_col)
    yb, o, sb_tot = _sb_fwd(proj, batch, seq, d, hd)
    dproj, dx2, dya, dyb, merged, dpa, dpb, loss_vec, dgf = _tail(
        x2d, tgt, ya, yb, proj, wf_oa, wf_ob, wf_out, norm_final2)
    gp_wo = _dw_o([(ya, dpa), (yb, dpb), (merged, dx2)])
    dproj, gp_ws, gp_b, gp_nv = _branch_a_bwd(proj, dya, norm_v, w_s[0], b_col, dproj)

    slab = lambda a: a.reshape(d // LANE, LANE)
    gc = groups * chunk
    packed = jnp.concatenate([gp_ws.reshape(gc, chunk), gp_b, slab(gp_nv), slab(dgf), slab(loss_vec)], axis=0)
    dproj, s_oa, s_ob, s_out, packs = _sb_bwd(
        proj, o, dyb, sb_tot, dproj, [gp_wo[k].reshape(N_DEV, rsh, d) for k in range(3)] + [packed],
        batch, seq, d, hd)
    grad_x, gp_nin = _dx(dproj, wg_in, x2d, dx2, norm_in)
    s_win, late_packs = _dw_in_exchange(h, dproj, my_slot, slab(gp_nin))
    tot, loss_slab = _finish_small(packs, late_packs, groups, chunk)
    ns = d // LANE
    g_ws = tot[:gc]
    g_b = tot[gc:gc + groups]
    g_nv, g_nf, _, g_nin = (tot[gc + groups + k * ns:gc + groups + (k + 1) * ns] for k in range(4))
    loss = loss_slab[0, 0]

    res = {}
    res["w_in"] = _reduce_adamw(s_win, w_in[0], m_w_in[0], v_w_in[0], "adamw_w_in")
    res["w_o_gmlp"] = _reduce_adamw(s_oa, w_o_gmlp[0], m_w_o_gmlp[0], v_w_o_gmlp[0], "adamw_w_o_gmlp")
    res["w_o_sb"] = _reduce_adamw(s_ob, w_o_sb[0], m_w_o_sb[0], v_w_o_sb[0], "adamw_w_o_sb")
    res["w_out"] = _reduce_adamw(s_out, w_out[0], m_w_out[0], v_w_out[0], "adamw_w_out")
    res["norm_in"] = _adamw_small(g_nin, slab(norm_in), slab(m_norm_in), slab(v_norm_in), "adamw_norm_in")
    res["norm_v"] = _adamw_small(g_nv, slab(norm_v), slab(m_norm_v), slab(v_norm_v), "adamw_norm_v")
    res["norm_final"] = _adamw_small(g_nf, slab(norm_final), slab(m_norm_final), slab(v_norm_final), "adamw_norm_final")
    res["w_s"] = _adamw_small(g_ws, w_s.reshape(gc, chunk), m_w_s.reshape(gc, chunk), v_w_s.reshape(gc, chunk), "adamw_w_s")
    res["b_s"] = _adamw_small(g_b, b_s[0], m_b_s[0], v_b_s[0], "adamw_b_s")

    shapes = {"norm_in": norm_in.shape, "w_in": w_in.shape, "norm_v": norm_v.shape, "w_s": w_s.shape,
              "b_s": b_s.shape, "w_o_gmlp": w_o_gmlp.shape, "w_o_sb": w_o_sb.shape, "w_out": w_out.shape,
              "norm_final": norm_final.shape}
    names = list(shapes)
    outs = [loss, grad_x.reshape(batch, seq, d)]
    for kind in range(4):
        outs += [res[name][kind].reshape(shapes[name]) for name in names]
    return tuple(outs)
```

```python
import functools
import math

import jax
import jax.numpy as jnp
from jax import lax
from jax.experimental import pallas as pl
from jax.experimental.pallas import tpu as pltpu

F32 = jnp.float32
BF16 = jnp.bfloat16
SDS = jax.ShapeDtypeStruct
MESH_ID = pl.DeviceIdType.MESH

N_DEV = 8
LANE = 128
SUBLANE = 8
VMEM_LIMIT = 56 * 1024 * 1024
SB_TILE = 512
SB_TILE_BWD = 512
SB_SCAN = 256
SB_HEADS = 2
MASKED_LOG = -1e30
RMS_EPS = 1e-6

ADAM_LR = 0.001
ADAM_B1 = 0.9
ADAM_B2 = 0.999
ADAM_EPS = 1e-08
ADAM_WD = 0.01
ADAM_STEP = 10

NT_DIMS = (((1,), (1,)), ((), ()))
TN_DIMS = (((0,), (0,)), ((), ()))


def _params(semantics=None):
    return pltpu.CompilerParams(dimension_semantics=semantics, vmem_limit_bytes=VMEM_LIMIT)


def _tile(n, preferred):
    t = min(n, preferred)
    assert n % t == 0, (n, t)
    return t


def _sigmoid(x):
    return 1.0 / (1.0 + jnp.exp(-x))


def _silu(x):
    s = _sigmoid(x)
    return x * s, s * (1.0 + x * (1.0 - s))


def _gelu(x):
    k = math.sqrt(2.0 / math.pi)
    x2 = x * x
    t = jnp.tanh(k * (x + 0.044715 * (x * x2)))
    cdf = 0.5 * (1.0 + t)
    return x * cdf, cdf + 0.5 * x * (1.0 - t * t) * (k * (1.0 + 3.0 * 0.044715 * x2))


def _rms_scale(x):
    return lax.rsqrt(jnp.mean(x * x, axis=-1, keepdims=True) + RMS_EPS)


def _iotas(n):
    return (lax.broadcasted_iota(jnp.int32, (n, n), 0), lax.broadcasted_iota(jnp.int32, (n, n), 1))


def _adamw(w, g, m, v):
    m = ADAM_B1 * m + (1.0 - ADAM_B1) * g
    v = ADAM_B2 * v + (1.0 - ADAM_B2) * (g * g)
    m_hat = m / (1.0 - ADAM_B1 ** ADAM_STEP)
    v_hat = v / (1.0 - ADAM_B2 ** ADAM_STEP)
    delta = -ADAM_LR * (m_hat / (jnp.sqrt(v_hat) + ADAM_EPS) + ADAM_WD * w)
    return delta, m, v


def _dot(a, b):
    return jnp.dot(a, b, preferred_element_type=F32)


def _dot_nt(a, b):
    return lax.dot_general(a, b, NT_DIMS, preferred_element_type=F32)


def _dot_tn(a, b):
    return lax.dot_general(a, b, TN_DIMS, preferred_element_type=F32)


def _sb_logs(raw, scale, valid):
    z = (raw * scale).astype(BF16)
    log_beta = jnp.minimum(z, 0) - jnp.log(1 + jnp.exp(-jnp.abs(z)))
    log_rest = log_beta - z
    if valid is not None:
        log_beta = jnp.where(valid, log_beta, MASKED_LOG)
        log_rest = jnp.where(valid, log_rest, 0)
    return log_beta, log_rest


def _me():
    return lax.axis_index("x"), lax.axis_index("y"), lax.axis_index("c")


def _slot(p):
    return 4 * p[0] + 2 * p[1] + p[2]


def _peer(me, k):
    flips = ((k >> 2) & 1, (k >> 1) & 1, k & 1)
    return tuple(1 - a if f else a for a, f in zip(me, flips))


def _stack_exchange(me, st_in, st_out, n_whole, send_sems, recv_sems, local_sems, arrivals=True):
    mine = _slot(me)
    ns = len(st_in)
    part = lambda a, dev: st_in[a] if a >= ns - n_whole else st_in[a].at[_slot(dev)]
    local = [pltpu.make_async_copy(part(a, me), st_out[a].at[mine], local_sems.at[a]) for a in range(ns)]
    remote, landed = [], []
    for k in range(1, N_DEV):
        peer = _peer(me, k)
        for a in range(ns):
            sems = dict(send_sem=send_sems.at[7 * a + k - 1], recv_sem=recv_sems.at[7 * a + k - 1])
            remote.append(pltpu.make_async_remote_copy(
                src_ref=part(a, peer), dst_ref=st_out[a].at[mine],
                device_id=peer, device_id_type=MESH_ID, **sems))
            if arrivals:
                got = st_out[a].at[_slot(peer)]
                landed.append(pltpu.make_async_remote_copy(
                    src_ref=got, dst_ref=got, device_id=me, device_id_type=MESH_ID, **sems))
    return local, remote, landed


def _gather_in_proj(x2d, norm_in, w_in_sh, wo_shards, my_slot):
    n, d = x2d.shape
    esh = w_in_sh.shape[1]
    pw = 2 * esh
    n_chip = N_DEV // 2
    tm = _tile(n, 1024)
    n_i = n // tm
    mid = n_i // 2
    no = len(wo_shards)
    flip_at = lambda st: jnp.where(st == 1, 2, jnp.where(st == 2, 1, jnp.where(st == 3, 3, 0)))

    def body(me_ref, x_ref, g_ref, win_ref, *refs):
        del me_ref
        wo_in = refs[:no]
        proj_ref, h_ref, wg_ref = refs[no:no + 3]
        wo_out = refs[no + 3:2 * no + 3]
        wv, stage = refs[2 * no + 3:2 * no + 5]
        wo_stage = refs[2 * no + 5:3 * no + 5]
        send_sems, recv_sems, pair_sems, own_sems, wo_send, wo_recv, wo_local = refs[3 * no + 5:]
        st, i = pl.program_id(0), pl.program_id(1)
        x, y, c = _me()
        me, sibling = (x, y, c), (x, y, 1 - c)
        chips = [(1 - x, y), (x, 1 - y), (1 - x, 1 - y)]
        chip_id = lambda p: 2 * p[0] + p[1]

        def window(chip, core):
            return wv.at[chip_id(chip), :, pl.ds(pl.multiple_of(core * esh, LANE), esh)]

        def copy(k, block, to, src=None):
            dst = window(block[:2], block[2])
            return pltpu.make_async_remote_copy(
                src_ref=dst if src is None else src, dst_ref=dst,
                send_sem=send_sems.at[k], recv_sem=recv_sems.at[k], device_id=to, device_id_type=MESH_ID)

        def wo_copy(a, k, block, to, src=None):
            dst = wo_out[a].at[_slot(block)]
            return pltpu.make_async_remote_copy(
                src_ref=dst if src is None else src, dst_ref=dst,
                send_sem=wo_send.at[7 * a + k], recv_sem=wo_recv.at[7 * a + k], device_id=to, device_id_type=MESH_ID)

        def own_copy():
            return pltpu.make_async_copy(stage, window((x, y), c), own_sems.at[0])

        def wo_own_copy(a):
            return pltpu.make_async_copy(wo_stage[a], wo_out[a].at[_slot(me)], wo_local.at[a])

        def pair_copy(step):
            chip = jnp.bitwise_xor(chip_id((x, y)), flip_at(step))
            return pltpu.make_async_copy(wv.at[chip], wg_ref.at[:, pl.ds(pl.multiple_of(chip * pw, LANE), pw)],
                                         pair_sems.at[step])

        first = jnp.logical_and(st == 0, i == 0)

        @pl.when(first)
        def _():
            stage[...] = win_ref[...].astype(BF16)
            own_copy().start()
            copy(0, me, sibling, src=stage).start()
            for j in range(2):
                copy(1 + j, me, (*chips[j], c), src=stage).start()
            own_copy().wait()
            copy(0, sibling, me).wait_recv()
            pair_copy(0).start()

        for s_ in range(n_chip - 1):
            @pl.when(jnp.logical_and(st == s_, i == mid))
            def _():
                copy(1 + s_, (*chips[s_], c), me).wait_recv()
                copy(4 + s_, (*chips[s_], c), sibling).start()
                if s_ == 0:
                    copy(3, me, (*chips[2], c), src=stage).start()
                if s_ == 1:
                    for a in range(no):
                        wo_stage[a][...] = wo_in[a][...].astype(BF16)
                        wo_own_copy(a).start()
                        wo_copy(a, 0, me, sibling, src=wo_stage[a]).start()
                        for j, chip in enumerate(chips):
                            wo_copy(a, 1 + j, me, (*chip, c), src=wo_stage[a]).start()
                if s_ == 2:
                    for a in range(no):
                        for j, chip in enumerate(chips):
                            wo_copy(a, 1 + j, (*chip, c), me).wait_recv()
                            wo_copy(a, 4 + j, (*chip, c), sibling).start()

        for s_ in range(1, n_chip):
            @pl.when(jnp.logical_and(st == s_, i == 0))
            def _():
                copy(3 + s_, (*chips[s_ - 1], 1 - c), me).wait_recv()
                pair_copy(s_).start()

        xv = x_ref[...]
        h = (xv * _rms_scale(xv) * g_ref[...]).astype(BF16)

        @pl.when(st == 0)
        def _():
            h_ref[...] = h

        chip_now = jnp.bitwise_xor(chip_id((x, y)), flip_at(st))
        proj_ref[...] = _dot(h, wv[chip_now]).astype(BF16)

        @pl.when(jnp.logical_and(st == n_chip - 1, i == n_i - 1))
        def _():
            copy(0, me, sibling, src=stage).wait_send()
            for j, chip in enumerate(chips):
                copy(1 + j, me, (*chip, c), src=stage).wait_send()
                copy(4 + j, (*chip, c), sibling).wait_send()
            for s_ in range(n_chip):
                pair_copy(s_).wait()
            for a in range(no):
                wo_copy(a, 0, me, sibling, src=wo_stage[a]).wait_send()
                wo_copy(a, 0, sibling, me).wait_recv()
                for j, chip in enumerate(chips):
                    wo_copy(a, 1 + j, me, (*chip, c), src=wo_stage[a]).wait_send()
                    wo_copy(a, 4 + j, (*chip, c), sibling).wait_send()
                    wo_copy(a, 4 + j, (*chip, 1 - c), me).wait_recv()
                wo_own_copy(a).wait()

    any_spec = pl.BlockSpec(memory_space=pl.ANY)
    vmem = pl.BlockSpec(memory_space=pltpu.VMEM)
    grid_spec = pltpu.PrefetchScalarGridSpec(
        num_scalar_prefetch=1, grid=(n_chip, n_i),
        in_specs=[pl.BlockSpec((tm, d), lambda st, i, me: (i, 0)),
                  pl.BlockSpec((1, d), lambda st, i, me: (0, 0)), vmem] + [vmem] * no,
        out_specs=[pl.BlockSpec((tm, pw), lambda st, i, me: (i, jnp.bitwise_xor(me[0] // 2, flip_at(st)))),
                   pl.BlockSpec((tm, d), lambda st, i, me: (jnp.where(st == 0, i, n_i - 1), 0)),
                   any_spec] + [any_spec] * no,
        scratch_shapes=[pltpu.VMEM((n_chip, d, pw), BF16), pltpu.VMEM((d, esh), BF16)] + [
            pltpu.VMEM(s.shape, BF16) for s in wo_shards] + [
            pltpu.SemaphoreType.DMA((7,)), pltpu.SemaphoreType.DMA((7,)),
            pltpu.SemaphoreType.DMA((n_chip,)), pltpu.SemaphoreType.DMA((1,)),
            pltpu.SemaphoreType.DMA((7 * no,)), pltpu.SemaphoreType.DMA((7 * no,)),
            pltpu.SemaphoreType.DMA((no,))])
    return pl.pallas_call(
        body, name="gather_in_proj", grid_spec=grid_spec,
        out_shape=[SDS((n, n_chip * pw), BF16), SDS((n, d), BF16), SDS((d, n_chip * pw), BF16)] + [
            SDS((N_DEV,) + s.shape, BF16) for s in wo_shards],
        compiler_params=pltpu.CompilerParams(dimension_semantics=("arbitrary", "arbitrary"),
                                             vmem_limit_bytes=VMEM_LIMIT),
    )(my_slot, x2d, norm_in, w_in_sh, *wo_shards)


EXCHANGE_ORDER = ((4, 2, 5, 3, 6, 7, 1, 0), (2, 4, 3, 5, 7, 6, 1, 0))


def _owner_at(mine, j):
    k = 0
    for step in range(N_DEV - 1):
        k = jnp.where(j == step, jnp.where(mine % 2 == 0, EXCHANGE_ORDER[0][step], EXCHANGE_ORDER[1][step]), k)
    return jnp.bitwise_xor(mine, k)


def _dw_in_exchange(h, dproj, my_slot, packed):
    n, d = h.shape
    esh = dproj.shape[1] // N_DEV
    tk = _tile(n, 512)
    nk = n // tk
    last_j = N_DEV - 1
    depth = 4

    def body(me_ref, h_ref, dp_ref, pk_in, win_out, pk_out,
             acc, sendbuf, win_send, win_recv, send_sems, recv_sems, local_sems):
        del me_ref
        j, k = pl.program_id(0), pl.program_id(1)
        me = _me()
        mine = _slot(me)

        def pack_copies():
            local = pltpu.make_async_copy(pk_in, pk_out.at[mine], local_sems.at[0])
            remote = [pltpu.make_async_remote_copy(
                src_ref=pk_in, dst_ref=pk_out.at[mine], send_sem=send_sems.at[kk - 1], recv_sem=recv_sems.at[kk - 1],
                device_id=_peer(me, kk), device_id_type=MESH_ID) for kk in range(1, N_DEV)]
            return local, remote

        def shard_copy(jj):
            owner = _owner_at(mine, jj)
            return pltpu.make_async_remote_copy(
                src_ref=sendbuf.at[jj % depth], dst_ref=win_out.at[mine],
                send_sem=win_send.at[jj % depth], recv_sem=win_recv.at[mine],
                device_id=(owner // 4, (owner // 2) % 2, owner % 2), device_id_type=MESH_ID)

        def own_copy():
            return pltpu.make_async_copy(sendbuf.at[last_j % depth], win_out.at[mine], local_sems.at[1])

        @pl.when(jnp.logical_and(j == 0, k == 0))
        def _():
            local, remote = pack_copies()
            for cp in [local] + remote:
                cp.start()

        @pl.when(k == 0)
        def _():
            acc[...] = jnp.zeros_like(acc)

        acc[...] += _dot_tn(h_ref[...], dp_ref[...])

        @pl.when(k == nk - 1)
        def _():
            @pl.when(j >= depth)
            def _():
                shard_copy(j - depth).wait_send()

            sendbuf[j % depth] = acc[...].astype(BF16)

            @pl.when(j < last_j)
            def _():
                shard_copy(j).start()

            @pl.when(j == last_j)
            def _():
                own_copy().start()
                for jj in range(last_j - depth + 1, last_j):
                    shard_copy(jj).wait_send()
                own_copy().wait()
                for src in range(N_DEV):
                    @pl.when(src != mine)
                    def _():
                        landed = win_out.at[src]
                        pltpu.make_async_remote_copy(
                            src_ref=landed, dst_ref=landed, send_sem=win_send.at[0], recv_sem=win_recv.at[src],
                            device_id=me, device_id_type=MESH_ID).wait_recv()
                local, remote = pack_copies()
                for cp in remote:
                    cp.wait_send()
                for kk in range(1, N_DEV):
                    landed = pk_out.at[_slot(_peer(me, kk))]
                    pltpu.make_async_remote_copy(
                        src_ref=landed, dst_ref=landed, send_sem=send_sems.at[kk - 1], recv_sem=recv_sems.at[kk - 1],
                        device_id=me, device_id_type=MESH_ID).wait_recv()
                local.wait()

    any_spec = pl.BlockSpec(memory_space=pl.ANY)
    grid_spec = pltpu.PrefetchScalarGridSpec(
        num_scalar_prefetch=1, grid=(N_DEV, nk),
        in_specs=[pl.BlockSpec((tk, d), lambda j, k, me: (k, 0)),
                  pl.BlockSpec((tk, esh), lambda j, k, me: (k, _owner_at(me[0], j))), any_spec],
        out_specs=[any_spec] * 2,
        scratch_shapes=[pltpu.VMEM((d, esh), F32), pltpu.VMEM((depth, d, esh), BF16),
                        pltpu.SemaphoreType.DMA((depth,)), pltpu.SemaphoreType.DMA((N_DEV,)),
                        pltpu.SemaphoreType.DMA((N_DEV - 1,)), pltpu.SemaphoreType.DMA((N_DEV - 1,)),
                        pltpu.SemaphoreType.DMA((2,))])
    return pl.pallas_call(
        body, name="dw_in_exchange", grid_spec=grid_spec,
        out_shape=[SDS((N_DEV, d, esh), BF16), SDS((N_DEV,) + packed.shape, packed.dtype)],
        compiler_params=_params(("arbitrary", "arbitrary")),
    )(my_slot, h, dproj, packed)


def _finish_small(packs, late_packs, groups, chunk):
    rows = packs.shape[1]
    late = late_packs.shape[1]
    gc = groups * chunk

    def body(p_ref, l_ref, sum_ref, loss_ref):
        row, col = _iotas(chunk)
        tril = col <= row
        for g in range(groups):
            rs = slice(g * chunk, (g + 1) * chunk)
            tot = p_ref[0, rs, :]
            for dev in range(1, N_DEV):
                tot = tot + p_ref[dev, rs, :]
            sum_ref[rs, :] = jnp.where(tril, tot, 0.0)
        rs = slice(gc, rows)
        tot = p_ref[0, rs, :]
        for dev in range(1, N_DEV):
            tot = tot + p_ref[dev, rs, :]
        sum_ref[rs, :] = tot
        loss_ref[...] = jnp.full((SUBLANE, LANE), jnp.sum(tot[rows - gc - SUBLANE:, :]), F32)
        tot = l_ref[0]
        for dev in range(1, N_DEV):
            tot = tot + l_ref[dev]
        sum_ref[rows:rows + late, :] = tot

    return pl.pallas_call(
        body, name="finish_small",
        out_shape=[SDS((rows + late, LANE), F32), SDS((SUBLANE, LANE), F32)],
        in_specs=[pl.BlockSpec(memory_space=pltpu.VMEM)] * 2,
        out_specs=[pl.BlockSpec(memory_space=pltpu.VMEM)] * 2,
        compiler_params=pltpu.CompilerParams(vmem_limit_bytes=VMEM_LIMIT),
    )(packs, late_packs)


def _branch_a_fwd(proj, norm_v, w_s, b_col):
    n = proj.shape[0]
    d = norm_v.shape[1]
    groups, chunk, _ = w_s.shape
    tr = _tile(n, 4 * chunk)

    def body(u_ref, v_ref, z_ref, gv_ref, ws_ref, b_ref, ya_ref, vn_s, pre_s):
        row, col = _iotas(chunk)
        tril = col <= row
        vg = _gelu(v_ref[...])[0].astype(F32)
        vn_s[...] = (vg * _rms_scale(vg) * gv_ref[...]).astype(BF16)
        pre_s[...] = _gelu(u_ref[...])[0] * _silu(z_ref[...])[0]
        for g in range(groups):
            wm = jnp.where(tril, ws_ref[g], 0.0).astype(BF16)
            cs = slice(g * chunk, (g + 1) * chunk)
            for c in range(tr // chunk):
                rs = slice(c * chunk, (c + 1) * chunk)
                mixed = _dot(wm, vn_s[rs, cs]) + b_ref[g]
                ya_ref[rs, cs] = (pre_s[rs, cs].astype(F32) * mixed).astype(BF16)

    seg = lambda k: pl.BlockSpec((tr, d), lambda i: (i, k))
    return pl.pallas_call(
        body, name="branch_a_fwd", grid=(n // tr,),
        in_specs=[seg(0), seg(1), seg(2),
                  pl.BlockSpec((1, d), lambda i: (0, 0)),
                  pl.BlockSpec((groups, chunk, chunk), lambda i: (0, 0, 0)),
                  pl.BlockSpec((groups, chunk, 1), lambda i: (0, 0, 0))],
        out_specs=pl.BlockSpec((tr, d), lambda i: (i, 0)),
        out_shape=SDS((n, d), BF16),
        scratch_shapes=[pltpu.VMEM((tr, d), BF16), pltpu.VMEM((tr, d), BF16)],
        compiler_params=_params(("parallel",)),
    )(proj, proj, proj, norm_v, w_s, b_col)


def _sb_fwd(proj, batch, seq, d, hd):
    heads = d // hd
    t = _tile(seq, SB_TILE)
    sw = _tile(t, SB_SCAN)
    nb = t // sw
    scale = hd ** -0.5
    nblk = seq // t
    nh = SB_HEADS
    wide = nh * hd
    cols = [slice(hh * hd, (hh + 1) * hd) for hh in range(nh)]

    def body(qs, k_ref, vs, zb_ref, yb_ref, o_ref, tot_ref, kts, later, acc):
        for jb in range(nblk):
            kts[jb] = k_ref[jb * t:(jb + 1) * t, :].astype(F32).T.astype(BF16)
        row, col = _iotas(t)
        later[...] = (row[:sw, :sw] > col[:sw, :sw]).astype(BF16)

        def qblock(i, carry):
            r0 = pl.multiple_of(i * t, t)

            def tile(j, runs):
                c0 = pl.multiple_of(j * t, t)
                logs = [_sb_logs(_dot(qs[pl.ds(r0, t), cs], kts[j, cs, :]), scale, None) for cs in cols]
                scans = [_dot(jnp.concatenate([logs[hh][1][:, b * sw:(b + 1) * sw] for b in range(nb)], axis=0),
                              later[...]) for hh in range(nh)]
                new_runs = []
                for hh in range(nh):
                    after = runs[hh]
                    blocks = [None] * nb
                    for b in reversed(range(nb)):
                        ks_ = slice(b * sw, (b + 1) * sw)
                        inside = scans[hh][b * t:(b + 1) * t]
                        blocks[b] = jnp.exp(logs[hh][0][:, ks_].astype(F32) + inside + after).astype(BF16)
                        after = after + inside[:, 0:1] + logs[hh][1][:, b * sw:b * sw + 1].astype(F32)
                    new_runs.append(after)
                    acc[:, cols[hh]] += _dot(jnp.concatenate(blocks, axis=1), vs[pl.ds(c0, t), cols[hh]])
                return tuple(new_runs)

            def diagonal_tile():
                starts = [b * sw for b in range(nb)]
                logs = [[_sb_logs(_dot(qs[pl.ds(r0 + s, t - s), cs], kts[i, cs, s:s + sw]), scale,
                                  col[:t - s, :sw] < row[:t - s, :sw]) for s in starts] for cs in cols]
                scans = [_dot(jnp.concatenate([lr for _, lr in logs[hh]], axis=0), later[...]) for hh in range(nh)]
                new_runs = []
                offs = [sum(t - s for s in starts[:b]) for b in range(nb)]
                for hh in range(nh):
                    after = jnp.zeros((t, 1), F32)
                    ws = [None] * nb
                    for b in reversed(range(nb)):
                        s = starts[b]
                        lb, lr = logs[hh][b]
                        inside = scans[hh][offs[b]:offs[b] + t - s]
                        ws[b] = jnp.exp(lb.astype(F32) + inside + after[s:]).astype(BF16)
                        total = inside[:, 0:1] + lr[:, 0:1].astype(F32)
                        after = after + total if s == 0 else jnp.concatenate([after[:s], after[s:] + total], axis=0)
                    new_runs.append(after)
                    acc[:, cols[hh]] = _dot(ws[0], vs[pl.ds(r0, sw), cols[hh]])
                    for b in range(1, nb):
                        acc[starts[b]:, cols[hh]] += _dot(ws[b], vs[pl.ds(r0 + starts[b], sw), cols[hh]])
                return tuple(new_runs)

            runs = diagonal_tile()
            runs = lax.fori_loop(0, i, lambda jj, rs: tile(i - 1 - jj, rs), runs)
            for hh in range(nh):
                out = acc[:, cols[hh]]
                o_ref[pl.ds(r0, t), cols[hh]] = out.astype(BF16)
                tot_ref[hh, pl.ds(r0, t), :] = runs[hh]
                sz, _ = _silu(zb_ref[pl.ds(r0, t), cols[hh]].astype(F32))
                yb_ref[pl.ds(r0, t), cols[hh]] = (out * sz).astype(BF16)
            return carry

        lax.fori_loop(0, nblk, qblock, 0)

    col0 = d // wide
    seg = lambda k: pl.BlockSpec((seq, wide), lambda b, h: (b, k * col0 + h))
    return pl.pallas_call(
        body, name="sb_fwd", grid=(batch, heads // nh),
        in_specs=[seg(3), seg(4), seg(5), seg(6)],
        out_specs=[pl.BlockSpec((seq, wide), lambda b, h: (b, h))] * 2 + [
            pl.BlockSpec((nh, seq, 1), lambda b, h: (b * (heads // nh) + h, 0, 0))],
        out_shape=[SDS((batch * seq, d), BF16), SDS((batch * seq, d), BF16), SDS((batch * heads, seq, 1), F32)],
        scratch_shapes=[pltpu.VMEM((nblk, wide, t), BF16), pltpu.VMEM((sw, sw), BF16), pltpu.VMEM((t, wide), F32)],
        compiler_params=_params(("parallel", "parallel")),
    )(proj, proj, proj, proj)


def _tail(x2d, tgt, ya, yb, proj, w_oa, w_ob, w_out, norm_final):
    n, d = x2d.shape
    e = proj.shape[1]
    tm = _tile(n, 256)
    steps = n // tm

    def body(x_ref, t_ref, ya_ref, yb_ref, ga_ref, gb_ref, woa_ref, wob_ref, wout_ref, gf_ref,
             dproj_ref, dx2_ref, dya_ref, dyb_ref, mrg_ref, dpa_ref, dpb_ref, loss_ref, dgf_ref, dg_s, dg_sems):
        i = pl.program_id(0)

        def gate_copy(step):
            rows_ = pl.ds(pl.multiple_of(step * tm, tm), tm)
            return pltpu.make_async_copy(dg_s.at[step % 2], dproj_ref.at[rows_, pl.ds(7 * d, 2 * d)],
                                         dg_sems.at[step % 2])

        @pl.when(i == 0)
        def _():
            loss_ref[...] = jnp.zeros_like(loss_ref)
            dgf_ref[...] = jnp.zeros_like(dgf_ref)

        @pl.when(i >= 2)
        def _():
            gate_copy(i - 2).wait()

        pa = _dot(ya_ref[...], woa_ref[...])
        pb = _dot(yb_ref[...], wob_ref[...])
        sa = _sigmoid(ga_ref[...].astype(F32))
        sb = _sigmoid(gb_ref[...].astype(F32))
        merged = (sa * pa + sb * pb).astype(BF16)
        mrg_ref[...] = merged
        x2 =x_ref[...] + _dot(merged, wout_ref[...])
        r2 = _rms_scale(x2)
        xh = x2 * r2
        gf = gf_ref[...]
        diff = xh * gf - t_ref[...]
        loss_ref[...] += jnp.sum(diff * diff, axis=0, keepdims=True) * (0.5 / d)
        dy = diff * (1.0 / d)
        dgf_ref[...] += jnp.sum(dy * xh, axis=0, keepdims=True)
        dxh = dy * gf
        dx2 = r2 * (dxh - xh * jnp.mean(dxh * xh, axis=-1, keepdims=True))
        dx2_ref[...] = dx2
        dm = _dot_nt(dx2.astype(BF16), wout_ref[...])
        dpa = (dm * sa).astype(BF16)
        dpb = (dm * sb).astype(BF16)
        dpa_ref[...] = dpa
        dpb_ref[...] = dpb
        dg_s[i % 2, :, 0:d] = (dm * pa * (sa * (1.0 - sa))).astype(BF16)
        dg_s[i % 2, :, d:2 * d] = (dm * pb * (sb * (1.0 - sb))).astype(BF16)
        gate_copy(i).start()
        dya_ref[...] = _dot_nt(dpa, woa_ref[...]).astype(BF16)
        dyb_ref[...] = _dot_nt(dpb, wob_ref[...]).astype(BF16)

        @pl.when(i == steps - 1)
        def _():
            if steps >= 2:
                gate_copy(i - 1).wait()
            gate_copy(i).wait()

    rows = lambda k=0: pl.BlockSpec((tm, d), lambda i: (i, k))
    full = pl.BlockSpec((d, d), lambda i: (0, 0))
    vec = pl.BlockSpec((1, d), lambda i: (0, 0))
    return pl.pallas_call(
        body, name="tail", grid=(steps,),
        in_specs=[rows(), rows(), rows(), rows(), rows(7), rows(8), full, full, full, vec],
        out_specs=[pl.BlockSpec(memory_space=pl.ANY),
                   rows(), rows(), rows(), rows(), rows(), rows(), vec, vec],
        out_shape=[SDS((n, e), BF16), SDS((n, d), F32), SDS((n, d), BF16), SDS((n, d), BF16),
                   SDS((n, d), BF16), SDS((n, d), BF16), SDS((n, d), BF16),
                   SDS((1, d), F32), SDS((1, d), F32)],
        scratch_shapes=[pltpu.VMEM((2, tm, 2 * d), BF16), pltpu.SemaphoreType.DMA((2,))],
        compiler_params=_params(("arbitrary",)),
    )(x2d, tgt, ya, yb, proj, proj, w_oa, w_ob, w_out, norm_final)


def _dw_o(pairs):
    n, d = pairs[0][0].shape
    tk = _tile(n, 1024)
    nk = n // tk
    npair = len(pairs)

    def body(*refs):
        a_refs, b_refs = refs[:npair], refs[npair:2 * npair]
        o_ref, acc = refs[2 * npair], refs[2 * npair + 1]
        p, k = pl.program_id(0), pl.program_id(1)

        @pl.when(k == 0)
        def _():
            acc[...] = jnp.zeros_like(acc)

        for q in range(npair):
            @pl.when(p == q)
            def _():
                acc[...] += _dot_tn(a_refs[q][...], b_refs[q][...].astype(BF16))

        @pl.when(k == nk - 1)
        def _():
            o_ref[0] = acc[...].astype(BF16)

    def tiles(q):
        return pl.BlockSpec((tk, d), lambda p, k: (jnp.where(p == q, k, jnp.where(p < q, 0, nk - 1)), 0))

    return pl.pallas_call(
        body, name="dw_o", grid=(npair, nk),
        in_specs=[tiles(q) for q in range(npair)] * 2,
        out_specs=pl.BlockSpec((1, d, d), lambda p, k: (p, 0, 0)),
        out_shape=SDS((npair, d, d), BF16),
        scratch_shapes=[pltpu.VMEM((d, d), F32)],
        compiler_params=_params(("arbitrary", "arbitrary")),
    )(*[a for a, _ in pairs], *[b for _, b in pairs])


def _sb_bwd(proj, o, dyb, tot, dproj, stacks, batch, seq, d, hd):
    heads = d // hd
    t = _tile(seq, SB_TILE_BWD)
    sw = _tile(t, SB_SCAN)
    nb = t // sw
    scale = hd ** -0.5
    nblk = seq // t
    nh = SB_HEADS
    wide = nh * hd
    hs = range(nh)
    cols = [slice(hh * hd, (hh + 1) * hd) for hh in hs]
    blocks = [slice(b * sw, (b + 1) * sw) for b in range(nb)]
    last = slice(sw - 1, sw)

    def compute(qs, ks, v_ref, zb_ref, dyb_ref, tot_ref, kts, vts, dos, res, upto, before, dq):
        for jb in range(nblk):
            rows = slice(jb * t, (jb + 1) * t)
            kts[jb] = ks[rows, :].astype(F32).T.astype(BF16)
            vts[jb] = v_ref[rows, :].astype(F32).T.astype(BF16)
        sz, _ = _silu(zb_ref[...].astype(F32))
        dos[...] = (dyb_ref[...].astype(F32) * sz).astype(BF16)
        res[1] = jnp.zeros((seq, wide), F32)
        res[2] = jnp.zeros((seq, wide), F32)
        row, col = _iotas(t)
        upto[...] = (row[:sw, :sw] <= col[:sw, :sw]).astype(BF16)
        before[...] = (row[:sw, :sw] < col[:sw, :sw]).astype(BF16)

        def qblock(i, carry):
            r0 = pl.multiple_of(i * t, t)

            def tile(j, sums):
                c0 = pl.multiple_of(j * t, t)
                q_i = [qs[pl.ds(r0, t), cs] for cs in cols]
                do_i = [dos[pl.ds(r0, t), cs] for cs in cols]
                logs = [_sb_logs(_dot(q_i[hh], kts[j, cols[hh], :]), scale, None) for hh in hs]
                dw = [_dot(do_i[hh], vts[j, cols[hh], :]) for hh in hs]
                scans = [_dot(jnp.concatenate([logs[hh][1][:, ks_] for ks_ in blocks], axis=0), upto[...]) for hh in hs]
                ws, gs, new_runs = [], [], []
                for hh in hs:
                    left = tot_ref[hh, pl.ds(r0, t), :] - sums[hh][0]
                    w_b, g_b = [], []
                    for b, ks_ in enumerate(blocks):
                        inside = scans[hh][b * t:(b + 1) * t]
                        w = jnp.exp(logs[hh][0][:, ks_].astype(F32) + (left - inside))
                        w_b.append(w.astype(BF16))
                        g_b.append((dw[hh][:, ks_] * w).astype(BF16))
                        left = left - inside[:, last]
                    ws.append(jnp.concatenate(w_b, axis=1))
                    gs.append(g_b)
                    new_runs.append(tot_ref[hh, pl.ds(r0, t), :] - left)
                gscans = [_dot(jnp.concatenate(gs[hh], axis=0), before[...]) for hh in hs]
                dzs, new_gruns = [], []
                for hh in hs:
                    g_before = sums[hh][1]
                    dz_b = []
                    for b, ks_ in enumerate(blocks):
                        inside = gscans[hh][b * t:(b + 1) * t]
                        beta = jnp.exp(logs[hh][0][:, ks_]).astype(F32)
                        g = gs[hh][b].astype(F32)
                        dz_b.append(((g - (g + inside + g_before) * beta) * scale).astype(BF16))
                        g_before = g_before + inside[:, last] + g[:, last]
                    dzs.append(jnp.concatenate(dz_b, axis=1))
                    new_gruns.append(g_before)
                for hh in hs:
                    res[2, pl.ds(c0, t), cols[hh]] += _dot_tn(ws[hh], do_i[hh])
                for hh in hs:
                    res[1, pl.ds(c0, t), cols[hh]] += _dot_tn(dzs[hh], q_i[hh])
                for hh in hs:
                    dq[:, cols[hh]] += _dot(dzs[hh], ks[pl.ds(c0, t), cols[hh]])
                return tuple((new_runs[hh], new_gruns[hh]) for hh in hs)

            def diagonal_tile(sums):
                starts = [b * sw for b in range(nb)]
                offs = [sum(t - s for s in starts[:b]) for b in range(nb)]
                q_b = [[qs[pl.ds(r0 + s, t - s), cs] for s in starts] for cs in cols]
                do_b = [[dos[pl.ds(r0 + s, t - s), cs] for s in starts] for cs in cols]
                logs = [[_sb_logs(_dot(q_b[hh][b], kts[i, cols[hh], s:s + sw]), scale,
                                  col[:t - s, :sw] < row[:t - s, :sw]) for b, s in enumerate(starts)] for hh in hs]
                dw = [[_dot(do_b[hh][b], vts[i, cols[hh], s:s + sw]) for b, s in enumerate(starts)] for hh in hs]
                scans = [_dot(jnp.concatenate([lr for _, lr in logs[hh]], axis=0), upto[...]) for hh in hs]
                ws, gs = [], []
                for hh in hs:
                    left = tot_ref[hh, pl.ds(r0, t), :] - sums[hh][0]
                    w_b, g_b = [], []
                    for b, s in enumerate(starts):
                        inside = scans[hh][offs[b]:offs[b] + t - s]
                        w = jnp.exp(logs[hh][b][0].astype(F32) + (left[s:] - inside))
                        w_b.append(w.astype(BF16))
                        g_b.append((dw[hh][b] * w).astype(BF16))
                        total = inside[:, last]
                        left = left - total if s == 0 else jnp.concatenate([left[:s], left[s:] - total], axis=0)
                    ws.append(w_b)
                    gs.append(g_b)
                gscans = [_dot(jnp.concatenate(gs[hh], axis=0), before[...]) for hh in hs]
                dzs = []
                for hh in hs:
                    g_before = sums[hh][1]
                    dz_b = []
                    for b, s in enumerate(starts):
                        inside = gscans[hh][offs[b]:offs[b] + t - s]
                        beta = jnp.exp(logs[hh][b][0]).astype(F32)
                        g = gs[hh][b].astype(F32)
                        dz_b.append(((g - (g + inside + g_before[s:]) * beta) * scale).astype(BF16))
                        total = inside[:, last] + g[:, last]
                        g_before = g_before + total if s == 0 else jnp.concatenate(
                            [g_before[:s], g_before[s:] + total], axis=0)
                    dzs.append(dz_b)
                for hh in hs:
                    for b, s in enumerate(starts):
                        res[2, pl.ds(r0 + s, sw), cols[hh]] += _dot_tn(ws[hh][b], do_b[hh][b])
                for hh in hs:
                    for b, s in enumerate(starts):
                        res[1, pl.ds(r0 + s, sw), cols[hh]] += _dot_tn(dzs[hh][b], q_b[hh][b])
                for hh in hs:
                    for b, s in enumerate(starts):
                        dq[s:, cols[hh]] += _dot(dzs[hh][b], ks[pl.ds(r0 + s, sw), cols[hh]])

            zero = jnp.zeros((t, 1), F32)
            dq[...] = jnp.zeros_like(dq)
            sums = lax.fori_loop(0, i, tile, ((zero, zero),) * nh)
            diagonal_tile(sums)
            res[0, pl.ds(r0, t), :] = dq[...]
            return carry

        lax.fori_loop(0, nblk, qblock, 0)

    pairs = heads // nh

    ns = len(stacks)

    def body(qs, ks, v_ref, zb_ref, o_ref, dyb_ref, tot_ref, dproj_in, *refs):
        del dproj_in
        st_in, out_ref, st_out = refs[:ns], refs[ns], refs[ns + 1:2 * ns + 1]
        (kts, vts, dos, res, upto, before, dq, stage, stage_sems,
         send_sems, recv_sems, local_sems) = refs[2 * ns + 1:]
        step = pl.program_id(0) * pairs + pl.program_id(1)
        exchange = functools.partial(_stack_exchange, _me(), st_in, st_out, 1, send_sems, recv_sems, local_sems)

        @pl.when(step == 0)
        def _():
            local, remote, _ = exchange(arrivals=False)
            for cp in local + remote:
                cp.start()

        def out_copies(s):
            rows_ = pl.ds(pl.multiple_of((s // pairs) * seq, seq), seq)
            return [pltpu.make_async_copy(
                stage.at[k], out_ref.at[rows_, pl.ds(pl.multiple_of((3 + k) * d + (s % pairs) * wide, wide), wide)],
                stage_sems.at[k]) for k in range(4)]

        compute(qs, ks, v_ref, zb_ref, dyb_ref, tot_ref, kts, vts, dos, res, upto, before, dq)

        @pl.when(step > 0)
        def _():
            for cp in out_copies(step - 1):
                cp.wait()

        for k in range(3):
            stage[k] = res[k].astype(BF16)
        _, dsz = _silu(zb_ref[...].astype(F32))
        stage[3] = (dyb_ref[...].astype(F32) * o_ref[...].astype(F32) * dsz).astype(BF16)
        for cp in out_copies(step):
            cp.start()

        @pl.when(step == batch * pairs - 1)
        def _():
            for cp in out_copies(step):
                cp.wait()
            local, remote, landed = exchange()
            for cp in remote:
                cp.wait_send()
            for cp in landed:
                cp.wait_recv()
            for cp in local:
                cp.wait()

    col0 = d // wide
    seg = lambda k: pl.BlockSpec((seq, wide), lambda b, h: (b, k * col0 + h))
    head = pl.BlockSpec((seq, wide), lambda b, h: (b, h))
    any_spec = pl.BlockSpec(memory_space=pl.ANY)
    return pl.pallas_call(
        body, name="sb_bwd", grid=(batch, pairs),
        in_specs=[seg(3), seg(4), seg(5), seg(6), head, head,
                  pl.BlockSpec((nh, seq, 1), lambda b, h: (b * pairs + h, 0, 0)), any_spec] + [any_spec] * ns,
        out_specs=[any_spec] * (ns + 1),
        out_shape=[SDS(dproj.shape, dproj.dtype)] + [SDS(s.shape, s.dtype) for s in stacks[:-1]] + [
            SDS((N_DEV,) + stacks[-1].shape, stacks[-1].dtype)],
        input_output_aliases={7: 0},
        scratch_shapes=[pltpu.VMEM((nblk, wide, t), BF16)] * 2 + [
            pltpu.VMEM((seq, wide), BF16), pltpu.VMEM((3, seq, wide), F32),
            pltpu.VMEM((sw, sw), BF16), pltpu.VMEM((sw, sw), BF16), pltpu.VMEM((t, wide), F32),
            pltpu.VMEM((4, seq, wide), BF16), pltpu.SemaphoreType.DMA((4,)),
            pltpu.SemaphoreType.DMA((7 * ns,)), pltpu.SemaphoreType.DMA((7 * ns,)),
            pltpu.SemaphoreType.DMA((ns,))],
        compiler_params=_params(("arbitrary", "arbitrary")),
    )(proj, proj, proj, proj, o, dyb, tot, dproj, *stacks)


def _branch_a_bwd(proj, dya, norm_v, w_s, b_col, dproj):
    n = proj.shape[0]
    d = norm_v.shape[1]
    groups, chunk, _ = w_s.shape
    tr = _tile(n, 2 * chunk)

    def body(u_ref, v_ref, z_ref, dya_ref, gv_ref, ws_ref, b_ref, dproj_in,
             out_ref, dws_ref, dbias_ref, dgv_ref, vn_s, dmix_s, dvn_s, db_ref):
        del dproj_in

        @pl.when(pl.program_id(0) == 0)
        def _():
            dws_ref[...] = jnp.zeros_like(dws_ref)
            db_ref[...] = jnp.zeros_like(db_ref)
            dgv_ref[...] = jnp.zeros_like(dgv_ref)

        row, col = _iotas(chunk)
        tril = col <= row
        gv = gv_ref[...]
        vg16, dvg_dv = _gelu(v_ref[...])
        vg = vg16.astype(F32)
        r = _rms_scale(vg)
        vh = vg * r
        vn_s[...] = (vh * gv).astype(BF16)
        ug, dug_du = _gelu(u_ref[...])
        sz, dsz = _silu(z_ref[...])
        dya_v = dya_ref[...]
        dmix_s[...] = dya_v * ug * sz
        du_scale = sz * dug_du
        dz_scale = ug * dsz
        for g in range(groups):
            wm = jnp.where(tril, ws_ref[g], 0.0).astype(BF16)
            cs = slice(g * chunk, (g + 1) * chunk)
            for c in range(tr // chunk):
                rs = slice(c * chunk, (c + 1) * chunk)
                vn = vn_s[rs, cs]
                mixed = _dot(wm, vn) + b_ref[g]
                dmix16 = dmix_s[rs, cs]
                dws_ref[g] += _dot_nt(dmix16, vn)
                db_ref[g] += dmix16.astype(F32)
                dvn_s[rs, cs] = _dot_tn(wm, dmix16)
                t_u = dya_v[rs, cs] * mixed.astype(BF16)
                out_ref[rs, g * chunk:(g + 1) * chunk] = t_u * du_scale[rs, cs]
                out_ref[rs, 2 * d + g * chunk:2 * d + (g + 1) * chunk] = t_u * dz_scale[rs, cs]
        dvn = dvn_s[...]
        dgv_ref[...] += jnp.sum(dvn * vh, axis=0, keepdims=True)
        dvh = dvn * gv
        dvg = r * (dvh - vh * jnp.mean(dvh * vh, axis=-1, keepdims=True))
        out_ref[:, d:2 * d] = (dvg * dvg_dv.astype(F32)).astype(BF16)

        @pl.when(pl.program_id(0) == n // tr - 1)
        def _():
            for g in range(groups):
                dbias_ref[g:g + 1, :] = jnp.sum(db_ref[g].T, axis=0, keepdims=True)

    seg = lambda k: pl.BlockSpec((tr, d), lambda i: (i, k))
    return pl.pallas_call(
        body, name="branch_a_bwd", grid=(n // tr,),
        in_specs=[seg(0), seg(1), seg(2), seg(0),
                  pl.BlockSpec((1, d), lambda i: (0, 0)),
                  pl.BlockSpec((groups, chunk, chunk), lambda i: (0, 0, 0)),
                  pl.BlockSpec((groups, chunk, 1), lambda i: (0, 0, 0)),
                  pl.BlockSpec(memory_space=pl.ANY)],
        out_specs=[pl.BlockSpec((tr, 3 * d), lambda i: (i, 0)),
                   pl.BlockSpec((groups, chunk, chunk), lambda i: (0, 0, 0)),
                   pl.BlockSpec((groups, chunk), lambda i: (0, 0)),
                   pl.BlockSpec((1, d), lambda i: (0, 0))],
        out_shape=[SDS(dproj.shape, dproj.dtype), SDS((groups, chunk, chunk), F32),
                   SDS((groups, chunk), F32), SDS((1, d), F32)],
        input_output_aliases={7: 0},
        scratch_shapes=[pltpu.VMEM((tr, d), BF16), pltpu.VMEM((tr, d), BF16), pltpu.VMEM((tr, d), F32),
                        pltpu.VMEM((groups, chunk, chunk), F32)],
        compiler_params=_params(("arbitrary",)),
    )(proj, proj, proj, dya, norm_v, w_s, b_col, dproj)


def _dx(dproj, wg_in, x2d, dx2, norm_in):
    n, d = x2d.shape
    nsh = N_DEV // 2
    esh = wg_in.shape[1] // nsh
    tm = _tile(n, 1024)

    def body(dp_ref, w_ref, x_ref, dx2_ref, g_ref, gx_ref, dg_ref, acc):
        i, k = pl.program_id(0), pl.program_id(1)

        @pl.when(jnp.logical_and(i == 0, k == 0))
        def _():
            dg_ref[...] = jnp.zeros_like(dg_ref)

        @pl.when(k == 0)
        def _():
            acc[...] = jnp.zeros_like(acc)

        acc[...] += _dot_nt(dp_ref[...], w_ref[...])

        @pl.when(k == nsh - 1)
        def _():
            dh = acc[...]
            x = x_ref[...]
            r = _rms_scale(x)
            xh = x * r
            dg_ref[...] += jnp.sum(dh * xh, axis=0, keepdims=True)
            dxh = dh * g_ref[...]
            gx_ref[...] = dx2_ref[...] + r * (dxh - xh * jnp.mean(dxh * xh, axis=-1, keepdims=True))

    rows = pl.BlockSpec((tm, d), lambda i, k: (i, 0))
    vec = pl.BlockSpec((1, d), lambda i, k: (0, 0))
    return pl.pallas_call(
        body, name="dx", grid=(n // tm, nsh),
        in_specs=[pl.BlockSpec((tm, esh), lambda i, k: (i, k)),
                  pl.BlockSpec((d, esh), lambda i, k: (0, k)), rows, rows, vec],
        out_specs=[rows, vec],
        out_shape=[SDS((n, d), F32), SDS((1, d), F32)],
        scratch_shapes=[pltpu.VMEM((tm, d), F32)],
        compiler_params=_params(("arbitrary", "arbitrary")),
    )(dproj, wg_in, x2d, dx2, norm_in)


def _adamw_outputs(g_ref, d_ref, m_ref, v_ref, g, w, m, v):
    delta, m2, v2 = _adamw(w, g, m, v)
    g_ref[...] = g
    d_ref[...] = delta
    m_ref[...] = m2
    v_ref[...] = v2


def _reduce_adamw(slots, w, m, v, name):
    _, r, c = slots.shape
    tr = _tile(r, 128)

    def body(s_ref, w_ref, m_ref, v_ref, g_out, d_out, m_out, v_out):
        g = s_ref[0].astype(F32)
        for k in range(1, N_DEV):
            g = g + s_ref[k].astype(F32)
        _adamw_outputs(g_out, d_out, m_out, v_out, g, w_ref[...], m_ref[...], v_ref[...])

    blk = pl.BlockSpec((tr, c), lambda i: (i, 0))
    return pl.pallas_call(
        body, name=name, grid=(r // tr,),
        in_specs=[pl.BlockSpec((N_DEV, tr, c), lambda i: (0, i, 0)), blk, blk, blk],
        out_specs=[blk] * 4,
        out_shape=[SDS((r, c), F32)] * 4,
        compiler_params=_params(("parallel",)),
    )(slots, w, m, v)


def _adamw_small(g, w, m, v, name):
    def body(g_ref, w_ref, m_ref, v_ref, g_out, d_out, m_out, v_out):
        _adamw_outputs(g_out, d_out, m_out, v_out, g_ref[...], w_ref[...], m_ref[...], v_ref[...])

    return pl.pallas_call(
        body, name=name,
        out_shape=[SDS(g.shape, F32)] * 4,
        in_specs=[pl.BlockSpec(memory_space=pltpu.VMEM)] * 4,
        out_specs=[pl.BlockSpec(memory_space=pltpu.VMEM)] * 4,
    )(g, w, m, v)


def kernel(x, norm_in, w_in, norm_v, w_s, b_s, w_o_gmlp, w_o_sb, w_out, norm_final, loss_target, m_norm_in, m_w_in, m_norm_v, m_w_s, m_b_s, m_w_o_gmlp, m_w_o_sb, m_w_out, m_norm_final, v_norm_in, v_w_in, v_norm_v, v_w_s, v_b_s, v_w_o_gmlp, v_w_o_sb, v_w_out, v_norm_final):
    batch, seq, d = x.shape
    n = batch * seq
    groups, chunk = w_s.shape[1], w_s.shape[2]
    hd = LANE
    x2d = x.reshape(n, d)
    tgt = loss_target.reshape(n, d)
    b_col = b_s[0].reshape(groups, chunk, 1)
    norm_final2 = norm_final.reshape(1, d)

    my_slot = _slot(_me()).astype(jnp.int32).reshape(1)
    proj, h, wg_in, wg_oa, wg_ob, wg_out = _gather_in_proj(
        x2d, norm_in, w_in[0], [w_o_gmlp[0], w_o_sb[0], w_out[0]], my_slot)
    rsh = wg_oa.shape[1]
    wf_oa, wf_ob, wf_out = (w.reshape(N_DEV * rsh, d) for w in (wg_oa, wg_ob, wg_out))
    ya = _branch_a_fwd(proj, norm_v, w_s[0], b_col)
    yb, o, sb_tot = _sb_fwd(proj, batch, seq, d, hd)
    dproj, dx2, dya, dyb, merged, dpa, dpb, loss_vec, dgf = _tail(
        x2d, tgt, ya, yb, proj, wf_oa, wf_ob, wf_out, norm_final2)
    gp_wo = _dw_o([(ya, dpa), (yb, dpb), (merged, dx2)])
    dproj, gp_ws, gp_b, gp_nv = _branch_a_bwd(proj, dya, norm_v, w_s[0], b_col, dproj)

    slab = lambda a: a.reshape(d // LANE, LANE)
    gc = groups * chunk
    packed = jnp.concatenate([gp_ws.reshape(gc, chunk), gp_b, slab(gp_nv), slab(dgf), slab(loss_vec)], axis=0)
    dproj, s_oa, s_ob, s_out, packs = _sb_bwd(
        proj, o, dyb, sb_tot, dproj, [gp_wo[k].reshape(N_DEV, rsh, d) for k in range(3)] + [packed],
        batch, seq, d, hd)
    grad_x, gp_nin = _dx(dproj, wg_in, x2d, dx2, norm_in)
    s_win, late_packs = _dw_in_exchange(h, dproj, my_slot, slab(gp_nin))
    tot, loss_slab = _finish_small(packs, late_packs, groups, chunk)
    ns = d // LANE
    g_ws = tot[:gc]
    g_b = tot[gc:gc + groups]
    g_nv, g_nf, _, g_nin = (tot[gc + groups + k * ns:gc + groups + (k + 1) * ns] for k in range(4))
    loss = loss_slab[0, 0]

    res = {}
    res["w_in"] = _reduce_adamw(s_win, w_in[0], m_w_in[0], v_w_in[0], "adamw_w_in")
    res["w_o_gmlp"] = _reduce_adamw(s_oa, w_o_gmlp[0], m_w_o_gmlp[0], v_w_o_gmlp[0], "adamw_w_o_gmlp")
    res["w_o_sb"] = _reduce_adamw(s_ob, w_o_sb[0], m_w_o_sb[0], v_w_o_sb[0], "adamw_w_o_sb")
    res["w_out"] = _reduce_adamw(s_out, w_out[0], m_w_out[0], v_w_out[0], "adamw_w_out")
    res["norm_in"] = _adamw_small(g_nin, slab(norm_in), slab(m_norm_in), slab(v_norm_in), "adamw_norm_in")
    res["norm_v"] = _adamw_small(g_nv, slab(norm_v), slab(m_norm_v), slab(v_norm_v), "adamw_norm_v")
    res["norm_final"] = _adamw_small(g_nf, slab(norm_final), slab(m_norm_final), slab(v_norm_final), "adamw_norm_final")
    res["w_s"] = _adamw_small(g_ws, w_s.reshape(gc, chunk), m_w_s.reshape(gc, chunk), v_w_s.reshape(gc, chunk), "adamw_w_s")
    res["b_s"] = _adamw_small(g_b, b_s[0], m_b_s[0], v_b_s[0], "adamw_b_s")

    shapes = {"norm_in": norm_in.shape, "w_in": w_in.shape, "norm_v": norm_v.shape, "w_s": w_s.shape,
              "b_s": b_s.shape, "w_o_gmlp": w_o_gmlp.shape, "w_o_sb": w_o_sb.shape, "w_out": w_out.shape,
              "norm_final": norm_final.shape}
    names = list(shapes)
    outs = [loss, grad_x.reshape(batch, seq, d)]
    for kind in range(4):
        outs += [res[name][kind].reshape(shapes[name]) for name in names]
    return tuple(outs)
```

```python
import functools
import math

import jax
import jax.numpy as jnp
from jax import lax
from jax.experimental import pallas as pl
from jax.experimental.pallas import tpu as pltpu

F32 = jnp.float32
BF16 = jnp.bfloat16
SDS = jax.ShapeDtypeStruct
MESH_ID = pl.DeviceIdType.MESH

N_DEV = 8
LANE = 128
SUBLANE = 8
VMEM_LIMIT = 56 * 1024 * 1024
SB_TILE = 512
SB_TILE_BWD = 512
SB_SCAN = 256
SB_HEADS = 2
MASKED_LOG = -1e30
RMS_EPS = 1e-6

ADAM_LR = 0.001
ADAM_B1 = 0.9
ADAM_B2 = 0.999
ADAM_EPS = 1e-08
ADAM_WD = 0.01
ADAM_STEP = 10

NT_DIMS = (((1,), (1,)), ((), ()))
TN_DIMS = (((0,), (0,)), ((), ()))


def _params(semantics=None):
    return pltpu.CompilerParams(dimension_semantics=semantics, vmem_limit_bytes=VMEM_LIMIT)


def _tile(n, preferred):
    t = min(n, preferred)
    assert n % t == 0, (n, t)
    return t


def _sigmoid(x):
    return 1.0 / (1.0 + jnp.exp(-x))


def _silu(x):
    s = _sigmoid(x)
    return x * s, s * (1.0 + x * (1.0 - s))


def _gelu(x):
    k = math.sqrt(2.0 / math.pi)
    x2 = x * x
    t = jnp.tanh(k * (x + 0.044715 * (x * x2)))
    cdf = 0.5 * (1.0 + t)
    return x * cdf, cdf + 0.5 * x * (1.0 - t * t) * (k * (1.0 + 3.0 * 0.044715 * x2))


def _rms_scale(x):
    return lax.rsqrt(jnp.mean(x * x, axis=-1, keepdims=True) + RMS_EPS)


def _iotas(n):
    return (lax.broadcasted_iota(jnp.int32, (n, n), 0), lax.broadcasted_iota(jnp.int32, (n, n), 1))


def _adamw(w, g, m, v):
    m = ADAM_B1 * m + (1.0 - ADAM_B1) * g
    v = ADAM_B2 * v + (1.0 - ADAM_B2) * (g * g)
    m_hat = m / (1.0 - ADAM_B1 ** ADAM_STEP)
    v_hat = v / (1.0 - ADAM_B2 ** ADAM_STEP)
    delta = -ADAM_LR * (m_hat / (jnp.sqrt(v_hat) + ADAM_EPS) + ADAM_WD * w)
    return delta, m, v


def _dot(a, b):
    return jnp.dot(a, b, preferred_element_type=F32)


def _dot_nt(a, b):
    return lax.dot_general(a, b, NT_DIMS, preferred_element_type=F32)


def _dot_tn(a, b):
    return lax.dot_general(a, b, TN_DIMS, preferred_element_type=F32)


def _sb_logs(raw, scale, valid):
    z = (raw * scale).astype(BF16)
    log_beta = jnp.minimum(z, 0) - jnp.log(1 + jnp.exp(-jnp.abs(z)))
    log_rest = log_beta - z
    if valid is not None:
        log_beta = jnp.where(valid, log_beta, MASKED_LOG)
        log_rest = jnp.where(valid, log_rest, 0)
    return log_beta, log_rest


def _me():
    return lax.axis_index("x"), lax.axis_index("y"), lax.axis_index("c")


def _slot(p):
    return 4 * p[0] + 2 * p[1] + p[2]


def _peer(me, k):
    flips = ((k >> 2) & 1, (k >> 1) & 1, k & 1)
    return tuple(1 - a if f else a for a, f in zip(me, flips))


def _stack_exchange(me, st_in, st_out, n_whole, send_sems, recv_sems, local_sems, arrivals=True):
    mine = _slot(me)
    ns = len(st_in)
    part = lambda a, dev: st_in[a] if a >= ns - n_whole else st_in[a].at[_slot(dev)]
    local = [pltpu.make_async_copy(part(a, me), st_out[a].at[mine], local_sems.at[a]) for a in range(ns)]
    remote, landed = [], []
    for k in range(1, N_DEV):
        peer = _peer(me, k)
        for a in range(ns):
            sems = dict(send_sem=send_sems.at[7 * a + k - 1], recv_sem=recv_sems.at[7 * a + k - 1])
            remote.append(pltpu.make_async_remote_copy(
                src_ref=part(a, peer), dst_ref=st_out[a].at[mine],
                device_id=peer, device_id_type=MESH_ID, **sems))
            if arrivals:
                got = st_out[a].at[_slot(peer)]
                landed.append(pltpu.make_async_remote_copy(
                    src_ref=got, dst_ref=got, device_id=me, device_id_type=MESH_ID, **sems))
    return local, remote, landed


def _gather_in_proj(x2d, norm_in, w_in_sh, wo_shards, my_slot):
    n, d = x2d.shape
    esh = w_in_sh.shape[1]
    pw = 2 * esh
    n_chip = N_DEV // 2
    tm = _tile(n, 1024)
    n_i = n // tm
    mid = n_i // 2
    no = len(wo_shards)
    flip_at = lambda st: jnp.where(st == 1, 2, jnp.where(st == 2, 1, jnp.where(st == 3, 3, 0)))

    def body(me_ref, x_ref, g_ref, win_ref, *refs):
        del me_ref
        wo_in = refs[:no]
        proj_ref, h_ref, wg_ref = refs[no:no + 3]
        wo_out = refs[no + 3:2 * no + 3]
        wv, stage = refs[2 * no + 3:2 * no + 5]
        wo_stage = refs[2 * no + 5:3 * no + 5]
        send_sems, recv_sems, pair_sems, own_sems, wo_send, wo_recv, wo_local = refs[3 * no + 5:]
        st, i = pl.program_id(0), pl.program_id(1)
        x, y, c = _me()
        me, sibling = (x, y, c), (x, y, 1 - c)
        chips = [(1 - x, y), (x, 1 - y), (1 - x, 1 - y)]
        chip_id = lambda p: 2 * p[0] + p[1]

        def window(chip, core):
            return wv.at[chip_id(chip), :, pl.ds(pl.multiple_of(core * esh, LANE), esh)]

        def copy(k, block, to, src=None):
            dst = window(block[:2], block[2])
            return pltpu.make_async_remote_copy(
                src_ref=dst if src is None else src, dst_ref=dst,
                send_sem=send_sems.at[k], recv_sem=recv_sems.at[k], device_id=to, device_id_type=MESH_ID)

        def wo_copy(a, k, block, to, src=None):
            dst = wo_out[a].at[_slot(block)]
            return pltpu.make_async_remote_copy(
                src_ref=dst if src is None else src, dst_ref=dst,
                send_sem=wo_send.at[7 * a + k], recv_sem=wo_recv.at[7 * a + k], device_id=to, device_id_type=MESH_ID)

        def own_copy():
            return pltpu.make_async_copy(stage, window((x, y), c), own_sems.at[0])

        def wo_own_copy(a):
            return pltpu.make_async_copy(wo_stage[a], wo_out[a].at[_slot(me)], wo_local.at[a])

        def pair_copy(step):
            chip = jnp.bitwise_xor(chip_id((x, y)), flip_at(step))
            return pltpu.make_async_copy(wv.at[chip], wg_ref.at[:, pl.ds(pl.multiple_of(chip * pw, LANE), pw)],
                                         pair_sems.at[step])

        first = jnp.logical_and(st == 0, i == 0)

        @pl.when(first)
        def _():
            stage[...] = win_ref[...].astype(BF16)
            own_copy().start()
            copy(0, me, sibling, src=stage).start()
            for j in range(2):
                copy(1 + j, me, (*chips[j], c), src=stage).start()
            own_copy().wait()
            copy(0, sibling, me).wait_recv()
            pair_copy(0).start()

        for s_ in range(n_chip - 1):
            @pl.when(jnp.logical_and(st == s_, i == mid))
            def _():
                copy(1 + s_, (*chips[s_], c), me).wait_recv()
                copy(4 + s_, (*chips[s_], c), sibling).start()
                if s_ == 0:
                    copy(3, me, (*chips[2], c), src=stage).start()
                if s_ == 1:
                    for a in range(no):
                        wo_stage[a][...] = wo_in[a][...].astype(BF16)
                        wo_own_copy(a).start()
                        wo_copy(a, 0, me, sibling, src=wo_stage[a]).start()
                        for j, chip in enumerate(chips):
                            wo_copy(a, 1 + j, me, (*chip, c), src=wo_stage[a]).start()
                if s_ == 2:
                    for a in range(no):
                        for j, chip in enumerate(chips):
                            wo_copy(a, 1 + j, (*chip, c), me).wait_recv()
                            wo_copy(a, 4 + j, (*chip, c), sibling).start()

        for s_ in range(1, n_chip):
            @pl.when(jnp.logical_and(st == s_, i == 0))
            def _():
                copy(3 + s_, (*chips[s_ - 1], 1 - c), me).wait_recv()
                pair_copy(s_).start()

        xv = x_ref[...]
        h = (xv * _rms_scale(xv) * g_ref[...]).astype(BF16)

        @pl.when(st == 0)
        def _():
            h_ref[...] = h

        chip_now = jnp.bitwise_xor(chip_id((x, y)), flip_at(st))
        proj_ref[...] = _dot(h, wv[chip_now]).astype(BF16)

        @pl.when(jnp.logical_and(st == n_chip - 1, i == n_i - 1))
        def _():
            copy(0, me, sibling, src=stage).wait_send()
            for j, chip in enumerate(chips):
                copy(1 + j, me, (*chip, c), src=stage).wait_send()
                copy(4 + j, (*chip, c), sibling).wait_send()
            for s_ in range(n_chip):
                pair_copy(s_).wait()
            for a in range(no):
                wo_copy(a, 0, me, sibling, src=wo_stage[a]).wait_send()
                wo_copy(a, 0, sibling, me).wait_recv()
                for j, chip in enumerate(chips):
                    wo_copy(a, 1 + j, me, (*chip, c), src=wo_stage[a]).wait_send()
                    wo_copy(a, 4 + j, (*chip, c), sibling).wait_send()
                    wo_copy(a, 4 + j, (*chip, 1 - c), me).wait_recv()
                wo_own_copy(a).wait()

    any_spec = pl.BlockSpec(memory_space=pl.ANY)
    vmem = pl.BlockSpec(memory_space=pltpu.VMEM)
    grid_spec = pltpu.PrefetchScalarGridSpec(
        num_scalar_prefetch=1, grid=(n_chip, n_i),
        in_specs=[pl.BlockSpec((tm, d), lambda st, i, me: (i, 0)),
                  pl.BlockSpec((1, d), lambda st, i, me: (0, 0)), vmem] + [vmem] * no,
        out_specs=[pl.BlockSpec((tm, pw), lambda st, i, me: (i, jnp.bitwise_xor(me[0] // 2, flip_at(st)))),
                   pl.BlockSpec((tm, d), lambda st, i, me: (jnp.where(st == 0, i, n_i - 1), 0)),
                   any_spec] + [any_spec] * no,
        scratch_shapes=[pltpu.VMEM((n_chip, d, pw), BF16), pltpu.VMEM((d, esh), BF16)] + [
            pltpu.VMEM(s.shape, BF16) for s in wo_shards] + [
            pltpu.SemaphoreType.DMA((7,)), pltpu.SemaphoreType.DMA((7,)),
            pltpu.SemaphoreType.DMA((n_chip,)), pltpu.SemaphoreType.DMA((1,)),
            pltpu.SemaphoreType.DMA((7 * no,)), pltpu.SemaphoreType.DMA((7 * no,)),
            pltpu.SemaphoreType.DMA((no,))])
    return pl.pallas_call(
        body, name="gather_in_proj", grid_spec=grid_spec,
        out_shape=[SDS((n, n_chip * pw), BF16), SDS((n, d), BF16), SDS((d, n_chip * pw), BF16)] + [
            SDS((N_DEV,) + s.shape, BF16) for s in wo_shards],
        compiler_params=pltpu.CompilerParams(dimension_semantics=("arbitrary", "arbitrary"),
                                             vmem_limit_bytes=VMEM_LIMIT),
    )(my_slot, x2d, norm_in, w_in_sh, *wo_shards)


EXCHANGE_ORDER = ((4, 2, 5, 3, 6, 7, 1, 0), (2, 4, 3, 5, 7, 6, 1, 0))


def _owner_at(mine, j):
    k = 0
    for step in range(N_DEV - 1):
        k = jnp.where(j == step, jnp.where(mine % 2 == 0, EXCHANGE_ORDER[0][step], EXCHANGE_ORDER[1][step]), k)
    return jnp.bitwise_xor(mine, k)


def _dw_in_exchange(h, dproj, my_slot, packed):
    n, d = h.shape
    esh = dproj.shape[1] // N_DEV
    tk = _tile(n, 1024)
    nk = n // tk
    last_j = N_DEV - 1
    depth = 4

    def body(me_ref, h_ref, dp_ref, pk_in, win_out, pk_out,
             acc, sendbuf, win_send, win_recv, send_sems, recv_sems, local_sems):
        del me_ref
        j, k = pl.program_id(0), pl.program_id(1)
        me = _me()
        mine = _slot(me)

        def pack_copies():
            local = pltpu.make_async_copy(pk_in, pk_out.at[mine], local_sems.at[0])
            remote = [pltpu.make_async_remote_copy(
                src_ref=pk_in, dst_ref=pk_out.at[mine], send_sem=send_sems.at[kk - 1], recv_sem=recv_sems.at[kk - 1],
                device_id=_peer(me, kk), device_id_type=MESH_ID) for kk in range(1, N_DEV)]
            return local, remote

        def shard_copy(jj):
            owner = _owner_at(mine, jj)
            return pltpu.make_async_remote_copy(
                src_ref=sendbuf.at[jj % depth], dst_ref=win_out.at[mine],
                send_sem=win_send.at[jj % depth], recv_sem=win_recv.at[mine],
                device_id=(owner // 4, (owner // 2) % 2, owner % 2), device_id_type=MESH_ID)

        def own_copy():
            return pltpu.make_async_copy(sendbuf.at[last_j % depth], win_out.at[mine], local_sems.at[1])

        @pl.when(jnp.logical_and(j == 0, k == 0))
        def _():
            local, remote = pack_copies()
            for cp in [local] + remote:
                cp.start()

        @pl.when(k == 0)
        def _():
            acc[...] = jnp.zeros_like(acc)

        acc[...] += _dot_tn(h_ref[...], dp_ref[...])

        @pl.when(k == nk - 1)
        def _():
            @pl.when(j >= depth)
            def _():
                shard_copy(j - depth).wait_send()

            sendbuf[j % depth] = acc[...].astype(BF16)

            @pl.when(j < last_j)
            def _():
                shard_copy(j).start()

            @pl.when(j == last_j)
            def _():
                own_copy().start()
                for jj in range(last_j - depth + 1, last_j):
                    shard_copy(jj).wait_send()
                own_copy().wait()
                for src in range(N_DEV):
                    @pl.when(src != mine)
                    def _():
                        landed = win_out.at[src]
                        pltpu.make_async_remote_copy(
                            src_ref=landed, dst_ref=landed, send_sem=win_send.at[0], recv_sem=win_recv.at[src],
                            device_id=me, device_id_type=MESH_ID).wait_recv()
                local, remote = pack_copies()
                for cp in remote:
                    cp.wait_send()
                for kk in range(1, N_DEV):
                    landed = pk_out.at[_slot(_peer(me, kk))]
                    pltpu.make_async_remote_copy(
                        src_ref=landed, dst_ref=landed, send_sem=send_sems.at[kk - 1], recv_sem=recv_sems.at[kk - 1],
                        device_id=me, device_id_type=MESH_ID).wait_recv()
                local.wait()

    any_spec = pl.BlockSpec(memory_space=pl.ANY)
    grid_spec = pltpu.PrefetchScalarGridSpec(
        num_scalar_prefetch=1, grid=(N_DEV, nk),
        in_specs=[pl.BlockSpec((tk, d), lambda j, k, me: (k, 0)),
                  pl.BlockSpec((tk, esh), lambda j, k, me: (k, _owner_at(me[0], j))), any_spec],
        out_specs=[any_spec] * 2,
        scratch_shapes=[pltpu.VMEM((d, esh), F32), pltpu.VMEM((depth, d, esh), BF16),
                        pltpu.SemaphoreType.DMA((depth,)), pltpu.SemaphoreType.DMA((N_DEV,)),
                        pltpu.SemaphoreType.DMA((N_DEV - 1,)), pltpu.SemaphoreType.DMA((N_DEV - 1,)),
                        pltpu.SemaphoreType.DMA((2,))])
    return pl.pallas_call(
        body, name="dw_in_exchange", grid_spec=grid_spec,
        out_shape=[SDS((N_DEV, d, esh), BF16), SDS((N_DEV,) + packed.shape, packed.dtype)],
        compiler_params=_params(("arbitrary", "arbitrary")),
    )(my_slot, h, dproj, packed)


def _finish_small(packs, late_packs, groups, chunk):
    rows = packs.shape[1]
    late = late_packs.shape[1]
    gc = groups * chunk

    def body(p_ref, l_ref, sum_ref, loss_ref):
        row, col = _iotas(chunk)
        tril = col <= row
        for g in range(groups):
            rs = slice(g * chunk, (g + 1) * chunk)
            tot = p_ref[0, rs, :]
            for dev in range(1, N_DEV):
                tot = tot + p_ref[dev, rs, :]
            sum_ref[rs, :] = jnp.where(tril, tot, 0.0)
        rs = slice(gc, rows)
        tot = p_ref[0, rs, :]
        for dev in range(1, N_DEV):
            tot = tot + p_ref[dev, rs, :]
        sum_ref[rs, :] = tot
        loss_ref[...] = jnp.full((SUBLANE, LANE), jnp.sum(tot[rows - gc - SUBLANE:, :]), F32)
        tot = l_ref[0]
        for dev in range(1, N_DEV):
            tot = tot + l_ref[dev]
        sum_ref[rows:rows + late, :] = tot

    return pl.pallas_call(
        body, name="finish_small",
        out_shape=[SDS((rows + late, LANE), F32), SDS((SUBLANE, LANE), F32)],
        in_specs=[pl.BlockSpec(memory_space=pltpu.VMEM)] * 2,
        out_specs=[pl.BlockSpec(memory_space=pltpu.VMEM)] * 2,
        compiler_params=pltpu.CompilerParams(vmem_limit_bytes=VMEM_LIMIT),
    )(packs, late_packs)


def _branch_a_fwd(proj, norm_v, w_s, b_col):
    n = proj.shape[0]
    d = norm_v.shape[1]
    groups, chunk, _ = w_s.shape
    tr = _tile(n, 4 * chunk)

    def body(u_ref, v_ref, z_ref, gv_ref, ws_ref, b_ref, ya_ref, vn_s, pre_s):
        row, col = _iotas(chunk)
        tril = col <= row
        vg = _gelu(v_ref[...])[0].astype(F32)
        vn_s[...] = (vg * _rms_scale(vg) * gv_ref[...]).astype(BF16)
        pre_s[...] = _gelu(u_ref[...])[0] * _silu(z_ref[...])[0]
        for g in range(groups):
            wm = jnp.where(tril, ws_ref[g], 0.0).astype(BF16)
            cs = slice(g * chunk, (g + 1) * chunk)
            for c in range(tr // chunk):
                rs = slice(c * chunk, (c + 1) * chunk)
                mixed = _dot(wm, vn_s[rs, cs]) + b_ref[g]
                ya_ref[rs, cs] = (pre_s[rs, cs].astype(F32) * mixed).astype(BF16)

    seg = lambda k: pl.BlockSpec((tr, d), lambda i: (i, k))
    return pl.pallas_call(
        body, name="branch_a_fwd", grid=(n // tr,),
        in_specs=[seg(0), seg(1), seg(2),
                  pl.BlockSpec((1, d), lambda i: (0, 0)),
                  pl.BlockSpec((groups, chunk, chunk), lambda i: (0, 0, 0)),
                  pl.BlockSpec((groups, chunk, 1), lambda i: (0, 0, 0))],
        out_specs=pl.BlockSpec((tr, d), lambda i: (i, 0)),
        out_shape=SDS((n, d), BF16),
        scratch_shapes=[pltpu.VMEM((tr, d), BF16), pltpu.VMEM((tr, d), BF16)],
        compiler_params=_params(("parallel",)),
    )(proj, proj, proj, norm_v, w_s, b_col)


def _sb_fwd(proj, batch, seq, d, hd):
    heads = d // hd
    t = _tile(seq, SB_TILE)
    sw = _tile(t, SB_SCAN)
    nb = t // sw
    scale = hd ** -0.5
    nblk = seq // t
    nh = SB_HEADS
    wide = nh * hd
    cols = [slice(hh * hd, (hh + 1) * hd) for hh in range(nh)]

    def body(qs, k_ref, vs, zb_ref, yb_ref, o_ref, tot_ref, kts, later, acc):
        for jb in range(nblk):
            kts[jb] = k_ref[jb * t:(jb + 1) * t, :].astype(F32).T.astype(BF16)
        row, col = _iotas(t)
        later[...] = (row[:sw, :sw] > col[:sw, :sw]).astype(BF16)

        def qblock(i, carry):
            r0 = pl.multiple_of(i * t, t)

            def tile(j, runs):
                c0 = pl.multiple_of(j * t, t)
                logs = [_sb_logs(_dot(qs[pl.ds(r0, t), cs], kts[j, cs, :]), scale, None) for cs in cols]
                scans = [_dot(jnp.concatenate([logs[hh][1][:, b * sw:(b + 1) * sw] for b in range(nb)], axis=0),
                              later[...]) for hh in range(nh)]
                new_runs = []
                for hh in range(nh):
                    after = runs[hh]
                    blocks = [None] * nb
                    for b in reversed(range(nb)):
                        ks_ = slice(b * sw, (b + 1) * sw)
                        inside = scans[hh][b * t:(b + 1) * t]
                        blocks[b] = jnp.exp(logs[hh][0][:, ks_].astype(F32) + inside + after).astype(BF16)
                        after = after + inside[:, 0:1] + logs[hh][1][:, b * sw:b * sw + 1].astype(F32)
                    new_runs.append(after)
                    acc[:, cols[hh]] += _dot(jnp.concatenate(blocks, axis=1), vs[pl.ds(c0, t), cols[hh]])
                return tuple(new_runs)

            def diagonal_tile():
                starts = [b * sw for b in range(nb)]
                logs = [[_sb_logs(_dot(qs[pl.ds(r0 + s, t - s), cs], kts[i, cs, s:s + sw]), scale,
                                  col[:t - s, :sw] < row[:t - s, :sw]) for s in starts] for cs in cols]
                scans = [_dot(jnp.concatenate([lr for _, lr in logs[hh]], axis=0), later[...]) for hh in range(nh)]
                new_runs = []
                offs = [sum(t - s for s in starts[:b]) for b in range(nb)]
                for hh in range(nh):
                    after = jnp.zeros((t, 1), F32)
                    ws = [None] * nb
                    for b in reversed(range(nb)):
                        s = starts[b]
                        lb, lr = logs[hh][b]
                        inside = scans[hh][offs[b]:offs[b] + t - s]
                        ws[b] = jnp.exp(lb.astype(F32) + inside + after[s:]).astype(BF16)
                        total = inside[:, 0:1] + lr[:, 0:1].astype(F32)
                        after = after + total if s == 0 else jnp.concatenate([after[:s], after[s:] + total], axis=0)
                    new_runs.append(after)
                    acc[:, cols[hh]] = _dot(ws[0], vs[pl.ds(r0, sw), cols[hh]])
                    for b in range(1, nb):
                        acc[starts[b]:, cols[hh]] += _dot(ws[b], vs[pl.ds(r0 + starts[b], sw), cols[hh]])
                return tuple(new_runs)

            runs = diagonal_tile()
            runs = lax.fori_loop(0, i, lambda jj, rs: tile(i - 1 - jj, rs), runs)
            for hh in range(nh):
                out = acc[:, cols[hh]]
                o_ref[pl.ds(r0, t), cols[hh]] = out.astype(BF16)
                tot_ref[hh, pl.ds(r0, t), :] = runs[hh]
                sz, _ = _silu(zb_ref[pl.ds(r0, t), cols[hh]].astype(F32))
                yb_ref[pl.ds(r0, t), cols[hh]] = (out * sz).astype(BF16)
            return carry

        lax.fori_loop(0, nblk, qblock, 0)

    col0 = d // wide
    seg = lambda k: pl.BlockSpec((seq, wide), lambda b, h: (b, k * col0 + h))
    return pl.pallas_call(
        body, name="sb_fwd", grid=(batch, heads // nh),
        in_specs=[seg(3), seg(4), seg(5), seg(6)],
        out_specs=[pl.BlockSpec((seq, wide), lambda b, h: (b, h))] * 2 + [
            pl.BlockSpec((nh, seq, 1), lambda b, h: (b * (heads // nh) + h, 0, 0))],
        out_shape=[SDS((batch * seq, d), BF16), SDS((batch * seq, d), BF16), SDS((batch * heads, seq, 1), F32)],
        scratch_shapes=[pltpu.VMEM((nblk, wide, t), BF16), pltpu.VMEM((sw, sw), BF16), pltpu.VMEM((t, wide), F32)],
        compiler_params=_params(("parallel", "parallel")),
    )(proj, proj, proj, proj)


def _tail(x2d, tgt, ya, yb, proj, w_oa, w_ob, w_out, norm_final):
    n, d = x2d.shape
    e = proj.shape[1]
    tm = _tile(n, 256)
    steps = n // tm

    def body(x_ref, t_ref, ya_ref, yb_ref, ga_ref, gb_ref, woa_ref, wob_ref, wout_ref, gf_ref,
             dproj_ref, dx2_ref, dya_ref, dyb_ref, mrg_ref, dpa_ref, dpb_ref, loss_ref, dgf_ref, dg_s, dg_sems):
        i = pl.program_id(0)

        def gate_copy(step):
            rows_ = pl.ds(pl.multiple_of(step * tm, tm), tm)
            return pltpu.make_async_copy(dg_s.at[step % 2], dproj_ref.at[rows_, pl.ds(7 * d, 2 * d)],
                                         dg_sems.at[step % 2])

        @pl.when(i == 0)
        def _():
            loss_ref[...] = jnp.zeros_like(loss_ref)
            dgf_ref[...] = jnp.zeros_like(dgf_ref)

        @pl.when(i >= 2)
        def _():
            gate_copy(i - 2).wait()

        pa = _dot(ya_ref[...], woa_ref[...])
        pb = _dot(yb_ref[...], wob_ref[...])
        sa = _sigmoid(ga_ref[...].astype(F32))
        sb = _sigmoid(gb_ref[...].astype(F32))
        merged = (sa * pa + sb * pb).astype(BF16)
        mrg_ref[...] = merged
        x2 =x_ref[...] + _dot(merged, wout_ref[...])
        r2 = _rms_scale(x2)
        xh = x2 * r2
        gf = gf_ref[...]
        diff = xh * gf - t_ref[...]
        loss_ref[...] += jnp.sum(diff * diff, axis=0, keepdims=True) * (0.5 / d)
        dy = diff * (1.0 / d)
        dgf_ref[...] += jnp.sum(dy * xh, axis=0, keepdims=True)
        dxh = dy * gf
        dx2 = r2 * (dxh - xh * jnp.mean(dxh * xh, axis=-1, keepdims=True))
        dx2_ref[...] = dx2
        dm = _dot_nt(dx2.astype(BF16), wout_ref[...])
        dpa = (dm * sa).astype(BF16)
        dpb = (dm * sb).astype(BF16)
        dpa_ref[...] = dpa
        dpb_ref[...] = dpb
        dg_s[i % 2, :, 0:d] = (dm * pa * (sa * (1.0 - sa))).astype(BF16)
        dg_s[i % 2, :, d:2 * d] = (dm * pb * (sb * (1.0 - sb))).astype(BF16)
        gate_copy(i).start()
        dya_ref[...] = _dot_nt(dpa, woa_ref[...]).astype(BF16)
        dyb_ref[...] = _dot_nt(dpb, wob_ref[...]).astype(BF16)

        @pl.when(i == steps - 1)
        def _():
            if steps >= 2:
                gate_copy(i - 1).wait()
            gate_copy(i).wait()

    rows = lambda k=0: pl.BlockSpec((tm, d), lambda i: (i, k))
    full = pl.BlockSpec((d, d), lambda i: (0, 0))
    vec = pl.BlockSpec((1, d), lambda i: (0, 0))
    return pl.pallas_call(
        body, name="tail", grid=(steps,),
        in_specs=[rows(), rows(), rows(), rows(), rows(7), rows(8), full, full, full, vec],
        out_specs=[pl.BlockSpec(memory_space=pl.ANY),
                   rows(), rows(), rows(), rows(), rows(), rows(), vec, vec],
        out_shape=[SDS((n, e), BF16), SDS((n, d), F32), SDS((n, d), BF16), SDS((n, d), BF16),
                   SDS((n, d), BF16), SDS((n, d), BF16), SDS((n, d), BF16),
                   SDS((1, d), F32), SDS((1, d), F32)],
        scratch_shapes=[pltpu.VMEM((2, tm, 2 * d), BF16), pltpu.SemaphoreType.DMA((2,))],
        compiler_params=_params(("arbitrary",)),
    )(x2d, tgt, ya, yb, proj, proj, w_oa, w_ob, w_out, norm_final)


def _dw_o(pairs):
    n, d = pairs[0][0].shape
    tk = _tile(n, 1024)
    nk = n // tk
    npair = len(pairs)

    def body(*refs):
        a_refs, b_refs = refs[:npair], refs[npair:2 * npair]
        o_ref, acc = refs[2 * npair], refs[2 * npair + 1]
        p, k = pl.program_id(0), pl.program_id(1)

        @pl.when(k == 0)
        def _():
            acc[...] = jnp.zeros_like(acc)

        for q in range(npair):
            @pl.when(p == q)
            def _():
                acc[...] += _dot_tn(a_refs[q][...], b_refs[q][...].astype(BF16))

        @pl.when(k == nk - 1)
        def _():
            o_ref[0] = acc[...].astype(BF16)

    def tiles(q):
        return pl.BlockSpec((tk, d), lambda p, k: (jnp.where(p == q, k, jnp.where(p < q, 0, nk - 1)), 0))

    return pl.pallas_call(
        body, name="dw_o", grid=(npair, nk),
        in_specs=[tiles(q) for q in range(npair)] * 2,
        out_specs=pl.BlockSpec((1, d, d), lambda p, k: (p, 0, 0)),
        out_shape=SDS((npair, d, d), BF16),
        scratch_shapes=[pltpu.VMEM((d, d), F32)],
        compiler_params=_params(("arbitrary", "arbitrary")),
    )(*[a for a, _ in pairs], *[b for _, b in pairs])


def _sb_bwd(proj, o, dyb, tot, dproj, stacks, batch, seq, d, hd):
    heads = d // hd
    t = _tile(seq, SB_TILE_BWD)
    sw = _tile(t, SB_SCAN)
    nb = t // sw
    scale = hd ** -0.5
    nblk = seq // t
    nh = SB_HEADS
    wide = nh * hd
    hs = range(nh)
    cols = [slice(hh * hd, (hh + 1) * hd) for hh in hs]
    blocks = [slice(b * sw, (b + 1) * sw) for b in range(nb)]
    last = slice(sw - 1, sw)

    def compute(qs, ks, v_ref, zb_ref, dyb_ref, tot_ref, kts, vts, dos, dq_all, dkv_t, qt_s, dot_s, upto, before, dq):
        for jb in range(nblk):
            rows = slice(jb * t, (jb + 1) * t)
            kts[jb] = ks[rows, :].astype(F32).T.astype(BF16)
            vts[jb] = v_ref[rows, :].astype(F32).T.astype(BF16)
        sz, _ = _silu(zb_ref[...].astype(F32))
        dos[...] = (dyb_ref[...].astype(F32) * sz).astype(BF16)
        row, col = _iotas(t)
        upto[...] = (row[:sw, :sw] <= col[:sw, :sw]).astype(BF16)
        before[...] = (row[:sw, :sw] < col[:sw, :sw]).astype(BF16)

        def qblock(i, carry):
            r0 = pl.multiple_of(i * t, t)

            def tile(j, sums):
                c0 = pl.multiple_of(j * t, t)
                q_i = [qs[pl.ds(r0, t), cs] for cs in cols]
                do_i = [dos[pl.ds(r0, t), cs] for cs in cols]
                logs = [_sb_logs(_dot(q_i[hh], kts[j, cols[hh], :]), scale, None) for hh in hs]
                dw = [_dot(do_i[hh], vts[j, cols[hh], :]) for hh in hs]
                scans = [_dot(jnp.concatenate([logs[hh][1][:, ks_] for ks_ in blocks], axis=0), upto[...]) for hh in hs]
                ws, gs, new_runs = [], [], []
                for hh in hs:
                    left = tot_ref[hh, pl.ds(r0, t), :] - sums[hh][0]
                    w_b, g_b = [], []
                    for b, ks_ in enumerate(blocks):
                        inside = scans[hh][b * t:(b + 1) * t]
                        w = jnp.exp(logs[hh][0][:, ks_].astype(F32) + (left - inside))
                        w_b.append(w.astype(BF16))
                        g_b.append((dw[hh][:, ks_] * w).astype(BF16))
                        left = left - inside[:, last]
                    ws.append(jnp.concatenate(w_b, axis=1))
                    gs.append(g_b)
                    new_runs.append(tot_ref[hh, pl.ds(r0, t), :] - left)
                gscans = [_dot(jnp.concatenate(gs[hh], axis=0), before[...]) for hh in hs]
                dzs, new_gruns = [], []
                for hh in hs:
                    g_before = sums[hh][1]
                    dz_b = []
                    for b, ks_ in enumerate(blocks):
                        inside = gscans[hh][b * t:(b + 1) * t]
                        beta = jnp.exp(logs[hh][0][:, ks_]).astype(F32)
                        g = gs[hh][b].astype(F32)
                        dz_b.append(((g - (g + inside + g_before) * beta) * scale).astype(BF16))
                        g_before = g_before + inside[:, last] + g[:, last]
                    dzs.append(jnp.concatenate(dz_b, axis=1))
                    new_gruns.append(g_before)
                for hh in hs:
                    dkv_t[1, j, cols[hh], :] += _dot(dot_s[cols[hh], :], ws[hh])
                for hh in hs:
                    dkv_t[0, j, cols[hh], :] += _dot(qt_s[cols[hh], :], dzs[hh])
                for hh in hs:
                    dq[:, cols[hh]] += _dot(dzs[hh], ks[pl.ds(c0, t), cols[hh]])
                return tuple((new_runs[hh], new_gruns[hh]) for hh in hs)

            def diagonal_tile(sums):
                starts = [b * sw for b in range(nb)]
                offs = [sum(t - s for s in starts[:b]) for b in range(nb)]
                q_b = [[qs[pl.ds(r0 + s, t - s), cs] for s in starts] for cs in cols]
                do_b = [[dos[pl.ds(r0 + s, t - s), cs] for s in starts] for cs in cols]
                logs = [[_sb_logs(_dot(q_b[hh][b], kts[i, cols[hh], s:s + sw]), scale,
                                  col[:t - s, :sw] < row[:t - s, :sw]) for b, s in enumerate(starts)] for hh in hs]
                dw = [[_dot(do_b[hh][b], vts[i, cols[hh], s:s + sw]) for b, s in enumerate(starts)] for hh in hs]
                scans = [_dot(jnp.concatenate([lr for _, lr in logs[hh]], axis=0), upto[...]) for hh in hs]
                ws, gs = [], []
                for hh in hs:
                    left = tot_ref[hh, pl.ds(r0, t), :] - sums[hh][0]
                    w_b, g_b = [], []
                    for b, s in enumerate(starts):
                        inside = scans[hh][offs[b]:offs[b] + t - s]
                        w = jnp.exp(logs[hh][b][0].astype(F32) + (left[s:] - inside))
                        w_b.append(w.astype(BF16))
                        g_b.append((dw[hh][b] * w).astype(BF16))
                        total = inside[:, last]
                        left = left - total if s == 0 else jnp.concatenate([left[:s], left[s:] - total], axis=0)
                    ws.append(w_b)
                    gs.append(g_b)
                gscans = [_dot(jnp.concatenate(gs[hh], axis=0), before[...]) for hh in hs]
                dzs = []
                for hh in hs:
                    g_before = sums[hh][1]
                    dz_b = []
                    for b, s in enumerate(starts):
                        inside = gscans[hh][offs[b]:offs[b] + t - s]
                        beta = jnp.exp(logs[hh][b][0]).astype(F32)
                        g = gs[hh][b].astype(F32)
                        dz_b.append(((g - (g + inside + g_before[s:]) * beta) * scale).astype(BF16))
                        total = inside[:, last] + g[:, last]
                        g_before = g_before + total if s == 0 else jnp.concatenate(
                            [g_before[:s], g_before[s:] + total], axis=0)
                    dzs.append(dz_b)
                for hh in hs:
                    for b, s in enumerate(starts):
                        dkv_t[1, i, cols[hh], s:s + sw] = _dot(dot_s[cols[hh], s:], ws[hh][b])
                for hh in hs:
                    for b, s in enumerate(starts):
                        dkv_t[0, i, cols[hh], s:s + sw] = _dot(qt_s[cols[hh], s:], dzs[hh][b])
                for hh in hs:
                    for b, s in enumerate(starts):
                        dq[s:, cols[hh]] += _dot(dzs[hh][b], ks[pl.ds(r0 + s, sw), cols[hh]])

            qt_s[...] = qs[pl.ds(r0, t), :].astype(F32).T.astype(BF16)
            dot_s[...] = dos[pl.ds(r0, t), :].astype(F32).T.astype(BF16)
            zero = jnp.zeros((t, 1), F32)
            dq[...] = jnp.zeros_like(dq)
            sums = lax.fori_loop(0, i, tile, ((zero, zero),) * nh)
            diagonal_tile(sums)
            dq_all[pl.ds(r0, t), :] = dq[...]
            return carry

        lax.fori_loop(0, nblk, qblock, 0)

    pairs = heads // nh

    ns = len(stacks)

    def body(qs, ks, v_ref, zb_ref, o_ref, dyb_ref, tot_ref, dproj_in, *refs):
        del dproj_in
        st_in, out_ref, st_out = refs[:ns], refs[ns], refs[ns + 1:2 * ns + 1]
        (kts, vts, dos, dq_all, dkv_t, qt_s, dot_s, upto, before, dq, stage, stage_sems,
         send_sems, recv_sems, local_sems) = refs[2 * ns + 1:]
        step = pl.program_id(0) * pairs + pl.program_id(1)
        exchange = functools.partial(_stack_exchange, _me(), st_in, st_out, 1, send_sems, recv_sems, local_sems)

        @pl.when(step == 0)
        def _():
            local, remote, _ = exchange(arrivals=False)
            for cp in local + remote:
                cp.start()

        def out_copies(s):
            rows_ = pl.ds(pl.multiple_of((s // pairs) * seq, seq), seq)
            return [pltpu.make_async_copy(
                stage.at[k], out_ref.at[rows_, pl.ds(pl.multiple_of((3 + k) * d + (s % pairs) * wide, wide), wide)],
                stage_sems.at[k]) for k in range(4)]

        compute(qs, ks, v_ref, zb_ref, dyb_ref, tot_ref, kts, vts, dos, dq_all, dkv_t, qt_s, dot_s, upto, before, dq)

        @pl.when(step > 0)
        def _():
            for cp in out_copies(step - 1):
                cp.wait()

        stage[0] = dq_all[...].astype(BF16)
        for k in range(2):
            for jb in range(nblk):
                stage[1 + k, jb * t:(jb + 1) * t, :] = dkv_t[k, jb].T.astype(BF16)
        _, dsz = _silu(zb_ref[...].astype(F32))
        stage[3] = (dyb_ref[...].astype(F32) * o_ref[...].astype(F32) * dsz).astype(BF16)
        for cp in out_copies(step):
            cp.start()

        @pl.when(step == batch * pairs - 1)
        def _():
            for cp in out_copies(step):
                cp.wait()
            local, remote, landed = exchange()
            for cp in remote:
                cp.wait_send()
            for cp in landed:
                cp.wait_recv()
            for cp in local:
                cp.wait()

    col0 = d // wide
    seg = lambda k: pl.BlockSpec((seq, wide), lambda b, h: (b, k * col0 + h))
    head = pl.BlockSpec((seq, wide), lambda b, h: (b, h))
    any_spec = pl.BlockSpec(memory_space=pl.ANY)
    return pl.pallas_call(
        body, name="sb_bwd", grid=(batch, pairs),
        in_specs=[seg(3), seg(4), seg(5), seg(6), head, head,
                  pl.BlockSpec((nh, seq, 1), lambda b, h: (b * pairs + h, 0, 0)), any_spec] + [any_spec] * ns,
        out_specs=[any_spec] * (ns + 1),
        out_shape=[SDS(dproj.shape, dproj.dtype)] + [SDS(s.shape, s.dtype) for s in stacks[:-1]] + [
            SDS((N_DEV,) + stacks[-1].shape, stacks[-1].dtype)],
        input_output_aliases={7: 0},
        scratch_shapes=[pltpu.VMEM((nblk, wide, t), BF16)] * 2 + [
            pltpu.VMEM((seq, wide), BF16), pltpu.VMEM((seq, wide), F32), pltpu.VMEM((2, nblk, wide, t), F32),
            pltpu.VMEM((wide, t), BF16), pltpu.VMEM((wide, t), BF16),
            pltpu.VMEM((sw, sw), BF16), pltpu.VMEM((sw, sw), BF16), pltpu.VMEM((t, wide), F32),
            pltpu.VMEM((4, seq, wide), BF16), pltpu.SemaphoreType.DMA((4,)),
            pltpu.SemaphoreType.DMA((7 * ns,)), pltpu.SemaphoreType.DMA((7 * ns,)),
            pltpu.SemaphoreType.DMA((ns,))],
        compiler_params=_params(("arbitrary", "arbitrary")),
    )(proj, proj, proj, proj, o, dyb, tot, dproj, *stacks)


def _branch_a_bwd(proj, dya, norm_v, w_s, b_col, dproj):
    n = proj.shape[0]
    d = norm_v.shape[1]
    groups, chunk, _ = w_s.shape
    tr = _tile(n, 2 * chunk)

    def body(u_ref, v_ref, z_ref, dya_ref, gv_ref, ws_ref, b_ref, dproj_in,
             out_ref, dws_ref, dbias_ref, dgv_ref, vn_s, dmix_s, dvn_s, db_ref):
        del dproj_in

        @pl.when(pl.program_id(0) == 0)
        def _():
            dws_ref[...] = jnp.zeros_like(dws_ref)
            db_ref[...] = jnp.zeros_like(db_ref)
            dgv_ref[...] = jnp.zeros_like(dgv_ref)

        row, col = _iotas(chunk)
        tril = col <= row
        gv = gv_ref[...]
        vg16, dvg_dv = _gelu(v_ref[...])
        vg = vg16.astype(F32)
        r = _rms_scale(vg)
        vh = vg * r
        vn_s[...] = (vh * gv).astype(BF16)
        ug, dug_du = _gelu(u_ref[...])
        sz, dsz = _silu(z_ref[...])
        dya_v = dya_ref[...]
        dmix_s[...] = dya_v * ug * sz
        du_scale = sz * dug_du
        dz_scale = ug * dsz
        for g in range(groups):
            wm = jnp.where(tril, ws_ref[g], 0.0).astype(BF16)
            cs = slice(g * chunk, (g + 1) * chunk)
            for c in range(tr // chunk):
                rs = slice(c * chunk, (c + 1) * chunk)
                vn = vn_s[rs, cs]
                mixed = _dot(wm, vn) + b_ref[g]
                dmix16 = dmix_s[rs, cs]
                dws_ref[g] += _dot_nt(dmix16, vn)
                db_ref[g] += dmix16.astype(F32)
                dvn_s[rs, cs] = _dot_tn(wm, dmix16)
                t_u = dya_v[rs, cs] * mixed.astype(BF16)
                out_ref[rs, g * chunk:(g + 1) * chunk] = t_u * du_scale[rs, cs]
                out_ref[rs, 2 * d + g * chunk:2 * d + (g + 1) * chunk] = t_u * dz_scale[rs, cs]
        dvn = dvn_s[...]
        dgv_ref[...] += jnp.sum(dvn * vh, axis=0, keepdims=True)
        dvh = dvn * gv
        dvg = r * (dvh - vh * jnp.mean(dvh * vh, axis=-1, keepdims=True))
        out_ref[:, d:2 * d] = (dvg * dvg_dv.astype(F32)).astype(BF16)

        @pl.when(pl.program_id(0) == n // tr - 1)
        def _():
            for g in range(groups):
                dbias_ref[g:g + 1, :] = jnp.sum(db_ref[g].T, axis=0, keepdims=True)

    seg = lambda k: pl.BlockSpec((tr, d), lambda i: (i, k))
    return pl.pallas_call(
        body, name="branch_a_bwd", grid=(n // tr,),
        in_specs=[seg(0), seg(1), seg(2), seg(0),
                  pl.BlockSpec((1, d), lambda i: (0, 0)),
                  pl.BlockSpec((groups, chunk, chunk), lambda i: (0, 0, 0)),
                  pl.BlockSpec((groups, chunk, 1), lambda i: (0, 0, 0)),
                  pl.BlockSpec(memory_space=pl.ANY)],
        out_specs=[pl.BlockSpec((tr, 3 * d), lambda i: (i, 0)),
                   pl.BlockSpec((groups, chunk, chunk), lambda i: (0, 0, 0)),
                   pl.BlockSpec((groups, chunk), lambda i: (0, 0)),
                   pl.BlockSpec((1, d), lambda i: (0, 0))],
        out_shape=[SDS(dproj.shape, dproj.dtype), SDS((groups, chunk, chunk), F32),
                   SDS((groups, chunk), F32), SDS((1, d), F32)],
        input_output_aliases={7: 0},
        scratch_shapes=[pltpu.VMEM((tr, d), BF16), pltpu.VMEM((tr, d), BF16), pltpu.VMEM((tr, d), F32),
                        pltpu.VMEM((groups, chunk, chunk), F32)],
        compiler_params=_params(("arbitrary",)),
    )(proj, proj, proj, dya, norm_v, w_s, b_col, dproj)


def _dx(dproj, wg_in, x2d, dx2, norm_in):
    n, d = x2d.shape
    nsh = N_DEV // 2
    esh = wg_in.shape[1] // nsh
    tm = _tile(n, 1024)

    def body(dp_ref, w_ref, x_ref, dx2_ref, g_ref, gx_ref, dg_ref, acc):
        i, k = pl.program_id(0), pl.program_id(1)

        @pl.when(jnp.logical_and(i == 0, k == 0))
        def _():
            dg_ref[...] = jnp.zeros_like(dg_ref)

        @pl.when(k == 0)
        def _():
            acc[...] = jnp.zeros_like(acc)

        acc[...] += _dot_nt(dp_ref[...], w_ref[...])

        @pl.when(k == nsh - 1)
        def _():
            dh = acc[...]
            x = x_ref[...]
            r = _rms_scale(x)
            xh = x * r
            dg_ref[...] += jnp.sum(dh * xh, axis=0, keepdims=True)
            dxh = dh * g_ref[...]
            gx_ref[...] = dx2_ref[...] + r * (dxh - xh * jnp.mean(dxh * xh, axis=-1, keepdims=True))

    rows = pl.BlockSpec((tm, d), lambda i, k: (i, 0))
    vec = pl.BlockSpec((1, d), lambda i, k: (0, 0))
    return pl.pallas_call(
        body, name="dx", grid=(n // tm, nsh),
        in_specs=[pl.BlockSpec((tm, esh), lambda i, k: (i, k)),
                  pl.BlockSpec((d, esh), lambda i, k: (0, k)), rows, rows, vec],
        out_specs=[rows, vec],
        out_shape=[SDS((n, d), F32), SDS((1, d), F32)],
        scratch_shapes=[pltpu.VMEM((tm, d), F32)],
        compiler_params=_params(("arbitrary", "arbitrary")),
    )(dproj, wg_in, x2d, dx2, norm_in)


def _adamw_outputs(g_ref, d_ref, m_ref, v_ref, g, w, m, v):
    delta, m2, v2 = _adamw(w, g, m, v)
    g_ref[...] = g
    d_ref[...] = delta
    m_ref[...] = m2
    v_ref[...] = v2


def _reduce_adamw(slots, w, m, v, name):
    _, r, c = slots.shape
    tr = _tile(r, 128)

    def body(s_ref, w_ref, m_ref, v_ref, g_out, d_out, m_out, v_out):
        g = s_ref[0].astype(F32)
        for k in range(1, N_DEV):
            g = g + s_ref[k].astype(F32)
        _adamw_outputs(g_out, d_out, m_out, v_out, g, w_ref[...], m_ref[...], v_ref[...])

    blk = pl.BlockSpec((tr, c), lambda i: (i, 0))
    return pl.pallas_call(
        body, name=name, grid=(r // tr,),
        in_specs=[pl.BlockSpec((N_DEV, tr, c), lambda i: (0, i, 0)), blk, blk, blk],
        out_specs=[blk] * 4,
        out_shape=[SDS((r, c), F32)] * 4,
        compiler_params=_params(("parallel",)),
    )(slots, w, m, v)


def _adamw_small(g, w, m, v, name):
    def body(g_ref, w_ref, m_ref, v_ref, g_out, d_out, m_out, v_out):
        _adamw_outputs(g_out, d_out, m_out, v_out, g_ref[...], w_ref[...], m_ref[...], v_ref[...])

    return pl.pallas_call(
        body, name=name,
        out_shape=[SDS(g.shape, F32)] * 4,
        in_specs=[pl.BlockSpec(memory_space=pltpu.VMEM)] * 4,
        out_specs=[pl.BlockSpec(memory_space=pltpu.VMEM)] * 4,
    )(g, w, m, v)


def kernel(x, norm_in, w_in, norm_v, w_s, b_s, w_o_gmlp, w_o_sb, w_out, norm_final, loss_target, m_norm_in, m_w_in, m_norm_v, m_w_s, m_b_s, m_w_o_gmlp, m_w_o_sb, m_w_out, m_norm_final, v_norm_in, v_w_in, v_norm_v, v_w_s, v_b_s, v_w_o_gmlp, v_w_o_sb, v_w_out, v_norm_final):
    batch, seq, d = x.shape
    n = batch * seq
    groups, chunk = w_s.shape[1], w_s.shape[2]
    hd = LANE
    x2d = x.reshape(n, d)
    tgt = loss_target.reshape(n, d)
    b_col = b_s[0].reshape(groups, chunk, 1)
    norm_final2 = norm_final.reshape(1, d)

    my_slot = _slot(_me()).astype(jnp.int32).reshape(1)
    proj, h, wg_in, wg_oa, wg_ob, wg_out = _gather_in_proj(
        x2d, norm_in, w_in[0], [w_o_gmlp[0], w_o_sb[0], w_out[0]], my_slot)
    rsh = wg_oa.shape[1]
    wf_oa, wf_ob, wf_out = (w.reshape(N_DEV * rsh, d) for w in (wg_oa, wg_ob, wg_out))
    ya = _branch_a_fwd(proj, norm_v, w_s[0], b_col)
    yb, o, sb_tot = _sb_fwd(proj, batch, seq, d, hd)
    dproj, dx2, dya, dyb, merged, dpa, dpb, loss_vec, dgf = _tail(
        x2d, tgt, ya, yb, proj, wf_oa, wf_ob, wf_out, norm_final2)
    gp_wo = _dw_o([(ya, dpa), (yb, dpb), (merged, dx2)])
    dproj, gp_ws, gp_b, gp_nv = _branch_a_bwd(proj, dya, norm_v, w_s[0], b_col, dproj)

    slab = lambda a: a.reshape(d // LANE, LANE)
    gc = groups * chunk
    packed = jnp.concatenate([gp_ws.reshape(gc, chunk), gp_b, slab(gp_nv), slab(dgf), slab(loss_vec)], axis=0)
    dproj, s_oa, s_ob, s_out, packs = _sb_bwd(
        proj, o, dyb, sb_tot, dproj, [gp_wo[k].reshape(N_DEV, rsh, d) for k in range(3)] + [packed],
        batch, seq, d, hd)
    grad_x, gp_nin = _dx(dproj, wg_in, x2d, dx2, norm_in)
    s_win, late_packs = _dw_in_exchange(h, dproj, my_slot, slab(gp_nin))
    tot, loss_slab = _finish_small(packs, late_packs, groups, chunk)
    ns = d // LANE
    g_ws = tot[:gc]
    g_b = tot[gc:gc + groups]
    g_nv, g_nf, _, g_nin = (tot[gc + groups + k * ns:gc + groups + (k + 1) * ns] for k in range(4))
    loss = loss_slab[0, 0]

    res = {}
    res["w_in"] = _reduce_adamw(s_win, w_in[0], m_w_in[0], v_w_in[0], "adamw_w_in")
    res["w_o_gmlp"] = _reduce_adamw(s_oa, w_o_gmlp[0], m_w_o_gmlp[0], v_w_o_gmlp[0], "adamw_w_o_gmlp")
    res["w_o_sb"] = _reduce_adamw(s_ob, w_o_sb[0], m_w_o_sb[0], v_w_o_sb[0], "adamw_w_o_sb")
    res["w_out"] = _reduce_adamw(s_out, w_out[0], m_w_out[0], v_w_out[0], "adamw_w_out")
    res["norm_in"] = _adamw_small(g_nin, slab(norm_in), slab(m_norm_in), slab(v_norm_in), "adamw_norm_in")
    res["norm_v"] = _adamw_small(g_nv, slab(norm_v), slab(m_norm_v), slab(v_norm_v), "adamw_norm_v")
    res["norm_final"] = _adamw_small(g_nf, slab(norm_final), slab(m_norm_final), slab(v_norm_final), "adamw_norm_final")
    res["w_s"] = _adamw_small(g_ws, w_s.reshape(gc, chunk), m_w_s.reshape(gc, chunk), v_w_s.reshape(gc, chunk), "adamw_w_s")
    res["b_s"] = _adamw_small(g_b, b_s[0], m_b_s[0], v_b_s[0], "adamw_b_s")

    shapes = {"norm_in": norm_in.shape, "w_in": w_in.shape, "norm_v": norm_v.shape, "w_s": w_s.shape,
              "b_s": b_s.shape, "w_o_gmlp": w_o_gmlp.shape, "w_o_sb": w_o_sb.shape, "w_out": w_out.shape,
              "norm_final": norm_final.shape}
    names = list(shapes)
    outs = [loss, grad_x.reshape(batch, seq, d)]
    for kind in range(4):
        outs += [res[name][kind].reshape(shapes[name]) for name in names]
    return tuple(outs)
```

```python
import functools
import math

import jax
import jax.numpy as jnp
from jax import lax
from jax.experimental import pallas as pl
from jax.experimental.pallas import tpu as pltpu

F32 = jnp.float32
BF16 = jnp.bfloat16
SDS = jax.ShapeDtypeStruct
MESH_ID = pl.DeviceIdType.MESH

N_DEV = 8
LANE = 128
SUBLANE = 8
VMEM_LIMIT = 56 * 1024 * 1024
SB_TILE = 512
SB_TILE_BWD = 512
SB_SCAN = 256
SB_HEADS = 2
MASKED_LOG = -1e30
RMS_EPS = 1e-6

ADAM_LR = 0.001
ADAM_B1 = 0.9
ADAM_B2 = 0.999
ADAM_EPS = 1e-08
ADAM_WD = 0.01
ADAM_STEP = 10

NT_DIMS = (((1,), (1,)), ((), ()))
TN_DIMS = (((0,), (0,)), ((), ()))


def _params(semantics=None):
    return pltpu.CompilerParams(dimension_semantics=semantics, vmem_limit_bytes=VMEM_LIMIT)


def _tile(n, preferred):
    t = min(n, preferred)
    assert n % t == 0, (n, t)
    return t


def _sigmoid(x):
    return 1.0 / (1.0 + jnp.exp(-x))


def _silu(x):
    s = _sigmoid(x)
    return x * s, s * (1.0 + x * (1.0 - s))


def _gelu(x):
    k = math.sqrt(2.0 / math.pi)
    x2 = x * x
    t = jnp.tanh(k * (x + 0.044715 * (x * x2)))
    cdf = 0.5 * (1.0 + t)
    return x * cdf, cdf + 0.5 * x * (1.0 - t * t) * (k * (1.0 + 3.0 * 0.044715 * x2))


def _rms_scale(x):
    return lax.rsqrt(jnp.mean(x * x, axis=-1, keepdims=True) + RMS_EPS)


def _iotas(n):
    return (lax.broadcasted_iota(jnp.int32, (n, n), 0), lax.broadcasted_iota(jnp.int32, (n, n), 1))


def _adamw(w, g, m, v):
    m = ADAM_B1 * m + (1.0 - ADAM_B1) * g
    v = ADAM_B2 * v + (1.0 - ADAM_B2) * (g * g)
    m_hat = m / (1.0 - ADAM_B1 ** ADAM_STEP)
    v_hat = v / (1.0 - ADAM_B2 ** ADAM_STEP)
    delta = -ADAM_LR * (m_hat / (jnp.sqrt(v_hat) + ADAM_EPS) + ADAM_WD * w)
    return delta, m, v


def _dot(a, b):
    return jnp.dot(a, b, preferred_element_type=F32)


def _dot_nt(a, b):
    return lax.dot_general(a, b, NT_DIMS, preferred_element_type=F32)


def _dot_tn(a, b):
    return lax.dot_general(a, b, TN_DIMS, preferred_element_type=F32)


def _sb_logs(raw, scale, valid):
    z = (raw * scale).astype(BF16)
    log_beta = jnp.minimum(z, 0) - jnp.log(1 + jnp.exp(-jnp.abs(z)))
    log_rest = log_beta - z
    if valid is not None:
        log_beta = jnp.where(valid, log_beta, MASKED_LOG)
        log_rest = jnp.where(valid, log_rest, 0)
    return log_beta, log_rest


def _me():
    return lax.axis_index("x"), lax.axis_index("y"), lax.axis_index("c")


def _slot(p):
    return 4 * p[0] + 2 * p[1] + p[2]


def _peer(me, k):
    flips = ((k >> 2) & 1, (k >> 1) & 1, k & 1)
    return tuple(1 - a if f else a for a, f in zip(me, flips))


def _stack_exchange(me, st_in, st_out, n_whole, send_sems, recv_sems, local_sems, arrivals=True):
    mine = _slot(me)
    ns = len(st_in)
    part = lambda a, dev: st_in[a] if a >= ns - n_whole else st_in[a].at[_slot(dev)]
    local = [pltpu.make_async_copy(part(a, me), st_out[a].at[mine], local_sems.at[a]) for a in range(ns)]
    remote, landed = [], []
    for k in range(1, N_DEV):
        peer = _peer(me, k)
        for a in range(ns):
            sems = dict(send_sem=send_sems.at[7 * a + k - 1], recv_sem=recv_sems.at[7 * a + k - 1])
            remote.append(pltpu.make_async_remote_copy(
                src_ref=part(a, peer), dst_ref=st_out[a].at[mine],
                device_id=peer, device_id_type=MESH_ID, **sems))
            if arrivals:
                got = st_out[a].at[_slot(peer)]
                landed.append(pltpu.make_async_remote_copy(
                    src_ref=got, dst_ref=got, device_id=me, device_id_type=MESH_ID, **sems))
    return local, remote, landed


def _gather_in_proj(x2d, norm_in, w_in_sh, wo_shards, my_slot):
    n, d = x2d.shape
    esh = w_in_sh.shape[1]
    pw = 2 * esh
    n_chip = N_DEV // 2
    tm = _tile(n, 1024)
    n_i = n // tm
    mid = n_i // 2
    no = len(wo_shards)
    flip_at = lambda st: jnp.where(st == 1, 2, jnp.where(st == 2, 1, jnp.where(st == 3, 3, 0)))

    def body(me_ref, x_ref, g_ref, win_ref, *refs):
        del me_ref
        wo_in = refs[:no]
        proj_ref, h_ref, wg_ref = refs[no:no + 3]
        wo_out = refs[no + 3:2 * no + 3]
        wv, stage = refs[2 * no + 3:2 * no + 5]
        wo_stage = refs[2 * no + 5:3 * no + 5]
        send_sems, recv_sems, pair_sems, own_sems, wo_send, wo_recv, wo_local = refs[3 * no + 5:]
        st, i = pl.program_id(0), pl.program_id(1)
        x, y, c = _me()
        me, sibling = (x, y, c), (x, y, 1 - c)
        chips = [(1 - x, y), (x, 1 - y), (1 - x, 1 - y)]
        chip_id = lambda p: 2 * p[0] + p[1]

        def window(chip, core):
            return wv.at[chip_id(chip), :, pl.ds(pl.multiple_of(core * esh, LANE), esh)]

        def copy(k, block, to, src=None):
            dst = window(block[:2], block[2])
            return pltpu.make_async_remote_copy(
                src_ref=dst if src is None else src, dst_ref=dst,
                send_sem=send_sems.at[k], recv_sem=recv_sems.at[k], device_id=to, device_id_type=MESH_ID)

        def wo_copy(a, k, block, to, src=None):
            dst = wo_out[a].at[_slot(block)]
            return pltpu.make_async_remote_copy(
                src_ref=dst if src is None else src, dst_ref=dst,
                send_sem=wo_send.at[7 * a + k], recv_sem=wo_recv.at[7 * a + k], device_id=to, device_id_type=MESH_ID)

        def own_copy():
            return pltpu.make_async_copy(stage, window((x, y), c), own_sems.at[0])

        def wo_own_copy(a):
            return pltpu.make_async_copy(wo_stage[a], wo_out[a].at[_slot(me)], wo_local.at[a])

        def pair_copy(step):
            chip = jnp.bitwise_xor(chip_id((x, y)), flip_at(step))
            return pltpu.make_async_copy(wv.at[chip], wg_ref.at[:, pl.ds(pl.multiple_of(chip * pw, LANE), pw)],
                                         pair_sems.at[step])

        first = jnp.logical_and(st == 0, i == 0)

        @pl.when(first)
        def _():
            stage[...] = win_ref[...].astype(BF16)
            own_copy().start()
            copy(0, me, sibling, src=stage).start()
            for j in range(2):
                copy(1 + j, me, (*chips[j], c), src=stage).start()
            own_copy().wait()
            copy(0, sibling, me).wait_recv()
            pair_copy(0).start()

        for s_ in range(n_chip - 1):
            @pl.when(jnp.logical_and(st == s_, i == mid))
            def _():
                copy(1 + s_, (*chips[s_], c), me).wait_recv()
                copy(4 + s_, (*chips[s_], c), sibling).start()
                if s_ == 0:
                    copy(3, me, (*chips[2], c), src=stage).start()
                if s_ == 1:
                    for a in range(no):
                        wo_stage[a][...] = wo_in[a][...].astype(BF16)
                        wo_own_copy(a).start()
                        wo_copy(a, 0, me, sibling, src=wo_stage[a]).start()
                        for j, chip in enumerate(chips):
                            wo_copy(a, 1 + j, me, (*chip, c), src=wo_stage[a]).start()
                if s_ == 2:
                    for a in range(no):
                        for j, chip in enumerate(chips):
                            wo_copy(a, 1 + j, (*chip, c), me).wait_recv()
                            wo_copy(a, 4 + j, (*chip, c), sibling).start()

        for s_ in range(1, n_chip):
            @pl.when(jnp.logical_and(st == s_, i == 0))
            def _():
                copy(3 + s_, (*chips[s_ - 1], 1 - c), me).wait_recv()
                pair_copy(s_).start()

        xv = x_ref[...]
        h = (xv * _rms_scale(xv) * g_ref[...]).astype(BF16)

        @pl.when(st == 0)
        def _():
            h_ref[...] = h

        chip_now = jnp.bitwise_xor(chip_id((x, y)), flip_at(st))
        proj_ref[...] = _dot(h, wv[chip_now]).astype(BF16)

        @pl.when(jnp.logical_and(st == n_chip - 1, i == n_i - 1))
        def _():
            copy(0, me, sibling, src=stage).wait_send()
            for j, chip in enumerate(chips):
                copy(1 + j, me, (*chip, c), src=stage).wait_send()
                copy(4 + j, (*chip, c), sibling).wait_send()
            for s_ in range(n_chip):
                pair_copy(s_).wait()
            for a in range(no):
                wo_copy(a, 0, me, sibling, src=wo_stage[a]).wait_send()
                wo_copy(a, 0, sibling, me).wait_recv()
                for j, chip in enumerate(chips):
                    wo_copy(a, 1 + j, me, (*chip, c), src=wo_stage[a]).wait_send()
                    wo_copy(a, 4 + j, (*chip, c), sibling).wait_send()
                    wo_copy(a, 4 + j, (*chip, 1 - c), me).wait_recv()
                wo_own_copy(a).wait()

    any_spec = pl.BlockSpec(memory_space=pl.ANY)
    vmem = pl.BlockSpec(memory_space=pltpu.VMEM)
    grid_spec = pltpu.PrefetchScalarGridSpec(
        num_scalar_prefetch=1, grid=(n_chip, n_i),
        in_specs=[pl.BlockSpec((tm, d), lambda st, i, me: (i, 0)),
                  pl.BlockSpec((1, d), lambda st, i, me: (0, 0)), vmem] + [vmem] * no,
        out_specs=[pl.BlockSpec((tm, pw), lambda st, i, me: (i, jnp.bitwise_xor(me[0] // 2, flip_at(st)))),
                   pl.BlockSpec((tm, d), lambda st, i, me: (jnp.where(st == 0, i, n_i - 1), 0)),
                   any_spec] + [any_spec] * no,
        scratch_shapes=[pltpu.VMEM((n_chip, d, pw), BF16), pltpu.VMEM((d, esh), BF16)] + [
            pltpu.VMEM(s.shape, BF16) for s in wo_shards] + [
            pltpu.SemaphoreType.DMA((7,)), pltpu.SemaphoreType.DMA((7,)),
            pltpu.SemaphoreType.DMA((n_chip,)), pltpu.SemaphoreType.DMA((1,)),
            pltpu.SemaphoreType.DMA((7 * no,)), pltpu.SemaphoreType.DMA((7 * no,)),
            pltpu.SemaphoreType.DMA((no,))])
    return pl.pallas_call(
        body, name="gather_in_proj", grid_spec=grid_spec,
        out_shape=[SDS((n, n_chip * pw), BF16), SDS((n, d), BF16), SDS((d, n_chip * pw), BF16)] + [
            SDS((N_DEV,) + s.shape, BF16) for s in wo_shards],
        compiler_params=pltpu.CompilerParams(dimension_semantics=("arbitrary", "arbitrary"),
                                             vmem_limit_bytes=VMEM_LIMIT),
    )(my_slot, x2d, norm_in, w_in_sh, *wo_shards)


EXCHANGE_ORDER = ((4, 2, 5, 3, 6, 7, 1, 0), (2, 4, 3, 5, 7, 6, 1, 0))


def _owner_at(mine, j):
    k = 0
    for step in range(N_DEV - 1):
        k = jnp.where(j == step, jnp.where(mine % 2 == 0, EXCHANGE_ORDER[0][step], EXCHANGE_ORDER[1][step]), k)
    return jnp.bitwise_xor(mine, k)


def _dw_in_exchange(h, dproj, my_slot, packed):
    n, d = h.shape
    esh = dproj.shape[1] // N_DEV
    tk = _tile(n, 1024)
    nk = n // tk
    last_j = N_DEV - 1
    depth = 4

    def body(me_ref, h_ref, dp_ref, pk_in, win_out, pk_out,
             acc, sendbuf, win_send, win_recv, send_sems, recv_sems, local_sems):
        del me_ref
        j, k = pl.program_id(0), pl.program_id(1)
        me = _me()
        mine = _slot(me)

        def pack_copies():
            local = pltpu.make_async_copy(pk_in, pk_out.at[mine], local_sems.at[0])
            remote = [pltpu.make_async_remote_copy(
                src_ref=pk_in, dst_ref=pk_out.at[mine], send_sem=send_sems.at[kk - 1], recv_sem=recv_sems.at[kk - 1],
                device_id=_peer(me, kk), device_id_type=MESH_ID) for kk in range(1, N_DEV)]
            return local, remote

        def shard_copy(jj):
            owner = _owner_at(mine, jj)
            return pltpu.make_async_remote_copy(
                src_ref=sendbuf.at[jj % depth], dst_ref=win_out.at[mine],
                send_sem=win_send.at[jj % depth], recv_sem=win_recv.at[mine],
                device_id=(owner // 4, (owner // 2) % 2, owner % 2), device_id_type=MESH_ID)

        def own_copy():
            return pltpu.make_async_copy(sendbuf.at[last_j % depth], win_out.at[mine], local_sems.at[1])

        @pl.when(jnp.logical_and(j == 0, k == 0))
        def _():
            local, remote = pack_copies()
            for cp in [local] + remote:
                cp.start()

        @pl.when(k == 0)
        def _():
            acc[...] = jnp.zeros_like(acc)

        acc[...] += _dot_tn(dp_ref[...], h_ref[...])

        @pl.when(k == nk - 1)
        def _():
            @pl.when(j >= depth)
            def _():
                shard_copy(j - depth).wait_send()

            sendbuf[j % depth] = acc[...].astype(BF16)

            @pl.when(j < last_j)
            def _():
                shard_copy(j).start()

            @pl.when(j == last_j)
            def _():
                own_copy().start()
                for jj in range(last_j - depth + 1, last_j):
                    shard_copy(jj).wait_send()
                own_copy().wait()
                for src in range(N_DEV):
                    @pl.when(src != mine)
                    def _():
                        landed = win_out.at[src]
                        pltpu.make_async_remote_copy(
                            src_ref=landed, dst_ref=landed, send_sem=win_send.at[0], recv_sem=win_recv.at[src],
                            device_id=me, device_id_type=MESH_ID).wait_recv()
                local, remote = pack_copies()
                for cp in remote:
                    cp.wait_send()
                for kk in range(1, N_DEV):
                    landed = pk_out.at[_slot(_peer(me, kk))]
                    pltpu.make_async_remote_copy(
                        src_ref=landed, dst_ref=landed, send_sem=send_sems.at[kk - 1], recv_sem=recv_sems.at[kk - 1],
                        device_id=me, device_id_type=MESH_ID).wait_recv()
                local.wait()

    any_spec = pl.BlockSpec(memory_space=pl.ANY)
    grid_spec = pltpu.PrefetchScalarGridSpec(
        num_scalar_prefetch=1, grid=(N_DEV, nk),
        in_specs=[pl.BlockSpec((tk, d), lambda j, k, me: (k, 0)),
                  pl.BlockSpec((tk, esh), lambda j, k, me: (k, _owner_at(me[0], j))), any_spec],
        out_specs=[any_spec] * 2,
        scratch_shapes=[pltpu.VMEM((esh, d), F32), pltpu.VMEM((depth, esh, d), BF16),
                        pltpu.SemaphoreType.DMA((depth,)), pltpu.SemaphoreType.DMA((N_DEV,)),
                        pltpu.SemaphoreType.DMA((N_DEV - 1,)), pltpu.SemaphoreType.DMA((N_DEV - 1,)),
                        pltpu.SemaphoreType.DMA((2,))])
    return pl.pallas_call(
        body, name="dw_in_exchange", grid_spec=grid_spec,
        out_shape=[SDS((N_DEV, esh, d), BF16), SDS((N_DEV,) + packed.shape, packed.dtype)],
        compiler_params=_params(("arbitrary", "arbitrary")),
    )(my_slot, h, dproj, packed)


def _finish_small(packs, late_packs, groups, chunk):
    rows = packs.shape[1]
    late = late_packs.shape[1]
    gc = groups * chunk

    def body(p_ref, l_ref, sum_ref, loss_ref):
        row, col = _iotas(chunk)
        tril = col <= row
        for g in range(groups):
            rs = slice(g * chunk, (g + 1) * chunk)
            tot = p_ref[0, rs, :]
            for dev in range(1, N_DEV):
                tot = tot + p_ref[dev, rs, :]
            sum_ref[rs, :] = jnp.where(tril, tot, 0.0)
        rs = slice(gc, rows)
        tot = p_ref[0, rs, :]
        for dev in range(1, N_DEV):
            tot = tot + p_ref[dev, rs, :]
        sum_ref[rs, :] = tot
        loss_ref[...] = jnp.full((SUBLANE, LANE), jnp.sum(tot[rows - gc - SUBLANE:, :]), F32)
        tot = l_ref[0]
        for dev in range(1, N_DEV):
            tot = tot + l_ref[dev]
        sum_ref[rows:rows + late, :] = tot

    return pl.pallas_call(
        body, name="finish_small",
        out_shape=[SDS((rows + late, LANE), F32), SDS((SUBLANE, LANE), F32)],
        in_specs=[pl.BlockSpec(memory_space=pltpu.VMEM)] * 2,
        out_specs=[pl.BlockSpec(memory_space=pltpu.VMEM)] * 2,
        compiler_params=pltpu.CompilerParams(vmem_limit_bytes=VMEM_LIMIT),
    )(packs, late_packs)


def _branch_a_fwd(proj, norm_v, w_s, b_col):
    n = proj.shape[0]
    d = norm_v.shape[1]
    groups, chunk, _ = w_s.shape
    tr = _tile(n, 4 * chunk)

    def body(u_ref, v_ref, z_ref, gv_ref, ws_ref, b_ref, ya_ref, vn_s, pre_s):
        row, col = _iotas(chunk)
        tril = col <= row
        vg = _gelu(v_ref[...])[0].astype(F32)
        vn_s[...] = (vg * _rms_scale(vg) * gv_ref[...]).astype(BF16)
        pre_s[...] = _gelu(u_ref[...])[0] * _silu(z_ref[...])[0]
        for g in range(groups):
            wm = jnp.where(tril, ws_ref[g], 0.0).astype(BF16)
            cs = slice(g * chunk, (g + 1) * chunk)
            for c in range(tr // chunk):
                rs = slice(c * chunk, (c + 1) * chunk)
                mixed = _dot(wm, vn_s[rs, cs]) + b_ref[g]
                ya_ref[rs, cs] = (pre_s[rs, cs].astype(F32) * mixed).astype(BF16)

    seg = lambda k: pl.BlockSpec((tr, d), lambda i: (i, k))
    return pl.pallas_call(
        body, name="branch_a_fwd", grid=(n // tr,),
        in_specs=[seg(0), seg(1), seg(2),
                  pl.BlockSpec((1, d), lambda i: (0, 0)),
                  pl.BlockSpec((groups, chunk, chunk), lambda i: (0, 0, 0)),
                  pl.BlockSpec((groups, chunk, 1), lambda i: (0, 0, 0))],
        out_specs=pl.BlockSpec((tr, d), lambda i: (i, 0)),
        out_shape=SDS((n, d), BF16),
        scratch_shapes=[pltpu.VMEM((tr, d), BF16), pltpu.VMEM((tr, d), BF16)],
        compiler_params=_params(("parallel",)),
    )(proj, proj, proj, norm_v, w_s, b_col)


def _sb_fwd(proj, batch, seq, d, hd):
    heads = d // hd
    t = _tile(seq, SB_TILE)
    sw = _tile(t, SB_SCAN)
    nb = t // sw
    scale = hd ** -0.5
    nblk = seq // t
    nh = SB_HEADS
    wide = nh * hd
    cols = [slice(hh * hd, (hh + 1) * hd) for hh in range(nh)]

    def body(qs, k_ref, vs, zb_ref, yb_ref, o_ref, tot_ref, kts, later, acc):
        for jb in range(nblk):
            kts[jb] = k_ref[jb * t:(jb + 1) * t, :].T
        row, col = _iotas(t)
        later[...] = (row[:sw, :sw] > col[:sw, :sw]).astype(BF16)

        def qblock(i, carry):
            r0 = pl.multiple_of(i * t, t)

            def tile(j, runs):
                c0 = pl.multiple_of(j * t, t)
                logs = [_sb_logs(_dot(qs[pl.ds(r0, t), cs], kts[j, cs, :]), scale, None) for cs in cols]
                scans = [_dot(jnp.concatenate([logs[hh][1][:, b * sw:(b + 1) * sw] for b in range(nb)], axis=0),
                              later[...]) for hh in range(nh)]
                new_runs = []
                for hh in range(nh):
                    after = runs[hh]
                    blocks = [None] * nb
                    for b in reversed(range(nb)):
                        ks_ = slice(b * sw, (b + 1) * sw)
                        inside = scans[hh][b * t:(b + 1) * t]
                        blocks[b] = jnp.exp(logs[hh][0][:, ks_].astype(F32) + inside + after).astype(BF16)
                        after = after + inside[:, 0:1] + logs[hh][1][:, b * sw:b * sw + 1].astype(F32)
                    new_runs.append(after)
                    acc[:, cols[hh]] += _dot(jnp.concatenate(blocks, axis=1), vs[pl.ds(c0, t), cols[hh]])
                return tuple(new_runs)

            def diagonal_tile():
                starts = [b * sw for b in range(nb)]
                logs = [[_sb_logs(_dot(qs[pl.ds(r0 + s, t - s), cs], kts[i, cs, s:s + sw]), scale,
                                  col[:t - s, :sw] < row[:t - s, :sw]) for s in starts] for cs in cols]
                scans = [_dot(jnp.concatenate([lr for _, lr in logs[hh]], axis=0), later[...]) for hh in range(nh)]
                new_runs = []
                offs = [sum(t - s for s in starts[:b]) for b in range(nb)]
                for hh in range(nh):
                    after = jnp.zeros((t, 1), F32)
                    ws = [None] * nb
                    for b in reversed(range(nb)):
                        s = starts[b]
                        lb, lr = logs[hh][b]
                        inside = scans[hh][offs[b]:offs[b] + t - s]
                        ws[b] = jnp.exp(lb.astype(F32) + inside + after[s:]).astype(BF16)
                        total = inside[:, 0:1] + lr[:, 0:1].astype(F32)
                        after = after + total if s == 0 else jnp.concatenate([after[:s], after[s:] + total], axis=0)
                    new_runs.append(after)
                    acc[:, cols[hh]] = _dot(ws[0], vs[pl.ds(r0, sw), cols[hh]])
                    for b in range(1, nb):
                        acc[starts[b]:, cols[hh]] += _dot(ws[b], vs[pl.ds(r0 + starts[b], sw), cols[hh]])
                return tuple(new_runs)

            runs = diagonal_tile()
            runs = lax.fori_loop(0, i, lambda jj, rs: tile(i - 1 - jj, rs), runs)
            for hh in range(nh):
                out = acc[:, cols[hh]]
                o_ref[pl.ds(r0, t), cols[hh]] = out.astype(BF16)
                tot_ref[hh, pl.ds(r0, t), :] = runs[hh]
                sz, _ = _silu(zb_ref[pl.ds(r0, t), cols[hh]].astype(F32))
                yb_ref[pl.ds(r0, t), cols[hh]] = (out * sz).astype(BF16)
            return carry

        lax.fori_loop(0, nblk, qblock, 0)

    col0 = d // wide
    seg = lambda k: pl.BlockSpec((seq, wide), lambda b, h: (b, k * col0 + h))
    return pl.pallas_call(
        body, name="sb_fwd", grid=(batch, heads // nh),
        in_specs=[seg(3), seg(4), seg(5), seg(6)],
        out_specs=[pl.BlockSpec((seq, wide), lambda b, h: (b, h))] * 2 + [
            pl.BlockSpec((nh, seq, 1), lambda b, h: (b * (heads // nh) + h, 0, 0))],
        out_shape=[SDS((batch * seq, d), BF16), SDS((batch * seq, d), BF16), SDS((batch * heads, seq, 1), F32)],
        scratch_shapes=[pltpu.VMEM((nblk, wide, t), BF16), pltpu.VMEM((sw, sw), BF16), pltpu.VMEM((t, wide), F32)],
        compiler_params=_params(("parallel", "parallel")),
    )(proj, proj, proj, proj)


def _tail(x2d, tgt, ya, yb, proj, w_oa, w_ob, w_out, norm_final):
    n, d = x2d.shape
    e = proj.shape[1]
    tm = _tile(n, 256)
    steps = n // tm

    def body(x_ref, t_ref, ya_ref, yb_ref, ga_ref, gb_ref, woa_ref, wob_ref, wout_ref, gf_ref,
             dproj_ref, dx2_ref, dya_ref, dyb_ref, mrg_ref, dpa_ref, dpb_ref, loss_ref, dgf_ref, dg_s, dg_sems):
        i = pl.program_id(0)

        def gate_copy(step):
            rows_ = pl.ds(pl.multiple_of(step * tm, tm), tm)
            return pltpu.make_async_copy(dg_s.at[step % 2], dproj_ref.at[rows_, pl.ds(7 * d, 2 * d)],
                                         dg_sems.at[step % 2])

        @pl.when(i == 0)
        def _():
            loss_ref[...] = jnp.zeros_like(loss_ref)
            dgf_ref[...] = jnp.zeros_like(dgf_ref)

        @pl.when(i >= 2)
        def _():
            gate_copy(i - 2).wait()

        pa = _dot(ya_ref[...], woa_ref[...])
        pb = _dot(yb_ref[...], wob_ref[...])
        sa = _sigmoid(ga_ref[...].astype(F32))
        sb = _sigmoid(gb_ref[...].astype(F32))
        merged = (sa * pa + sb * pb).astype(BF16)
        mrg_ref[...] = merged
        x2 =x_ref[...] + _dot(merged, wout_ref[...])
        r2 = _rms_scale(x2)
        xh = x2 * r2
        gf = gf_ref[...]
        diff = xh * gf - t_ref[...]
        loss_ref[...] += jnp.sum(diff * diff, axis=0, keepdims=True) * (0.5 / d)
        dy = diff * (1.0 / d)
        dgf_ref[...] += jnp.sum(dy * xh, axis=0, keepdims=True)
        dxh = dy * gf
        dx2 = r2 * (dxh - xh * jnp.mean(dxh * xh, axis=-1, keepdims=True))
        dx2_ref[...] = dx2
        dm = _dot_nt(dx2.astype(BF16), wout_ref[...])
        dpa = (dm * sa).astype(BF16)
        dpb = (dm * sb).astype(BF16)
        dpa_ref[...] = dpa
        dpb_ref[...] = dpb
        dg_s[i % 2, :, 0:d] = (dm * pa * (sa * (1.0 - sa))).astype(BF16)
        dg_s[i % 2, :, d:2 * d] = (dm * pb * (sb * (1.0 - sb))).astype(BF16)
        gate_copy(i).start()
        dya_ref[...] = _dot_nt(dpa, woa_ref[...]).astype(BF16)
        dyb_ref[...] = _dot_nt(dpb, wob_ref[...]).astype(BF16)

        @pl.when(i == steps - 1)
        def _():
            if steps >= 2:
                gate_copy(i - 1).wait()
            gate_copy(i).wait()

    rows = lambda k=0: pl.BlockSpec((tm, d), lambda i: (i, k))
    full = pl.BlockSpec((d, d), lambda i: (0, 0))
    vec = pl.BlockSpec((1, d), lambda i: (0, 0))
    return pl.pallas_call(
        body, name="tail", grid=(steps,),
        in_specs=[rows(), rows(), rows(), rows(), rows(7), rows(8), full, full, full, vec],
        out_specs=[pl.BlockSpec(memory_space=pl.ANY),
                   rows(), rows(), rows(), rows(), rows(), rows(), vec, vec],
        out_shape=[SDS((n, e), BF16), SDS((n, d), F32), SDS((n, d), BF16), SDS((n, d), BF16),
                   SDS((n, d), BF16), SDS((n, d), BF16), SDS((n, d), BF16),
                   SDS((1, d), F32), SDS((1, d), F32)],
        scratch_shapes=[pltpu.VMEM((2, tm, 2 * d), BF16), pltpu.SemaphoreType.DMA((2,))],
        compiler_params=_params(("arbitrary",)),
    )(x2d, tgt, ya, yb, proj, proj, w_oa, w_ob, w_out, norm_final)


def _dw_o(pairs):
    n, d = pairs[0][0].shape
    tk = _tile(n, 1024)
    nk = n // tk
    npair = len(pairs)

    def body(*refs):
        a_refs, b_refs = refs[:npair], refs[npair:2 * npair]
        o_ref, acc = refs[2 * npair], refs[2 * npair + 1]
        p, k = pl.program_id(0), pl.program_id(1)

        @pl.when(k == 0)
        def _():
            acc[...] = jnp.zeros_like(acc)

        for q in range(npair):
            @pl.when(p == q)
            def _():
                acc[...] += _dot_tn(a_refs[q][...], b_refs[q][...].astype(BF16))

        @pl.when(k == nk - 1)
        def _():
            o_ref[0] = acc[...].astype(BF16)

    def tiles(q):
        return pl.BlockSpec((tk, d), lambda p, k: (jnp.where(p == q, k, jnp.where(p < q, 0, nk - 1)), 0))

    return pl.pallas_call(
        body, name="dw_o", grid=(npair, nk),
        in_specs=[tiles(q) for q in range(npair)] * 2,
        out_specs=pl.BlockSpec((1, d, d), lambda p, k: (p, 0, 0)),
        out_shape=SDS((npair, d, d), BF16),
        scratch_shapes=[pltpu.VMEM((d, d), F32)],
        compiler_params=_params(("arbitrary", "arbitrary")),
    )(*[a for a, _ in pairs], *[b for _, b in pairs])


def _sb_bwd(proj, o, dyb, tot, dproj, stacks, batch, seq, d, hd):
    heads = d // hd
    t = _tile(seq, SB_TILE_BWD)
    sw = _tile(t, SB_SCAN)
    nb = t // sw
    scale = hd ** -0.5
    nblk = seq // t
    nh = SB_HEADS
    wide = nh * hd
    hs = range(nh)
    cols = [slice(hh * hd, (hh + 1) * hd) for hh in hs]
    blocks = [slice(b * sw, (b + 1) * sw) for b in range(nb)]
    last = slice(sw - 1, sw)

    def compute(qs, ks, v_ref, zb_ref, o_ref, dyb_ref, tot_ref, kts, vts, dos, dzb, dq_all, dkv_t, qt_s, dot_s,
                upto, before, dq):
        for jb in range(nblk):
            rows = slice(jb * t, (jb + 1) * t)
            kts[jb] = ks[rows, :].T
            vts[jb] = v_ref[rows, :].T
        sz, dsz = _silu(zb_ref[...])
        dyb_v = dyb_ref[...]
        dos[...] = dyb_v * sz
        dzb[...] = dyb_v * o_ref[...] * dsz
        row, col = _iotas(t)
        upto[...] = (row[:sw, :sw] <= col[:sw, :sw]).astype(BF16)
        before[...] = (row[:sw, :sw] < col[:sw, :sw]).astype(BF16)

        def qblock(i, carry):
            r0 = pl.multiple_of(i * t, t)

            def tile(j, sums):
                c0 = pl.multiple_of(j * t, t)
                q_i = [qs[pl.ds(r0, t), cs] for cs in cols]
                do_i = [dos[pl.ds(r0, t), cs] for cs in cols]
                logs = [_sb_logs(_dot(q_i[hh], kts[j, cols[hh], :]), scale, None) for hh in hs]
                dw = [_dot(do_i[hh], vts[j, cols[hh], :]) for hh in hs]
                scans = [_dot(jnp.concatenate([logs[hh][1][:, ks_] for ks_ in blocks], axis=0), upto[...]) for hh in hs]
                ws, gs, new_runs = [], [], []
                for hh in hs:
                    left = tot_ref[hh, pl.ds(r0, t), :] - sums[hh][0]
                    w_b, g_b = [], []
                    for b, ks_ in enumerate(blocks):
                        inside = scans[hh][b * t:(b + 1) * t]
                        w = jnp.exp(logs[hh][0][:, ks_].astype(F32) + (left - inside))
                        w_b.append(w.astype(BF16))
                        g_b.append((dw[hh][:, ks_] * w).astype(BF16))
                        left = left - inside[:, last]
                    ws.append(jnp.concatenate(w_b, axis=1))
                    gs.append(g_b)
                    new_runs.append(tot_ref[hh, pl.ds(r0, t), :] - left)
                gscans = [_dot(jnp.concatenate(gs[hh], axis=0), before[...]) for hh in hs]
                dzs, new_gruns = [], []
                for hh in hs:
                    g_before = sums[hh][1]
                    dz_b = []
                    for b, ks_ in enumerate(blocks):
                        inside = gscans[hh][b * t:(b + 1) * t]
                        beta = jnp.exp(logs[hh][0][:, ks_]).astype(F32)
                        g = gs[hh][b].astype(F32)
                        dz_b.append(((g - (g + inside + g_before) * beta) * scale).astype(BF16))
                        g_before = g_before + inside[:, last] + g[:, last]
                    dzs.append(jnp.concatenate(dz_b, axis=1))
                    new_gruns.append(g_before)
                for hh in hs:
                    dkv_t[1, j, cols[hh], :] += _dot(dot_s[cols[hh], :], ws[hh])
                for hh in hs:
                    dkv_t[0, j, cols[hh], :] += _dot(qt_s[cols[hh], :], dzs[hh])
                for hh in hs:
                    dq[:, cols[hh]] += _dot(dzs[hh], ks[pl.ds(c0, t), cols[hh]])
                return tuple((new_runs[hh], new_gruns[hh]) for hh in hs)

            def diagonal_tile(sums):
                starts = [b * sw for b in range(nb)]
                offs = [sum(t - s for s in starts[:b]) for b in range(nb)]
                q_b = [[qs[pl.ds(r0 + s, t - s), cs] for s in starts] for cs in cols]
                do_b = [[dos[pl.ds(r0 + s, t - s), cs] for s in starts] for cs in cols]
                logs = [[_sb_logs(_dot(q_b[hh][b], kts[i, cols[hh], s:s + sw]), scale,
                                  col[:t - s, :sw] < row[:t - s, :sw]) for b, s in enumerate(starts)] for hh in hs]
                dw = [[_dot(do_b[hh][b], vts[i, cols[hh], s:s + sw]) for b, s in enumerate(starts)] for hh in hs]
                scans = [_dot(jnp.concatenate([lr for _, lr in logs[hh]], axis=0), upto[...]) for hh in hs]
                ws, gs = [], []
                for hh in hs:
                    left = tot_ref[hh, pl.ds(r0, t), :] - sums[hh][0]
                    w_b, g_b = [], []
                    for b, s in enumerate(starts):
                        inside = scans[hh][offs[b]:offs[b] + t - s]
                        w = jnp.exp(logs[hh][b][0].astype(F32) + (left[s:] - inside))
                        w_b.append(w.astype(BF16))
                        g_b.append((dw[hh][b] * w).astype(BF16))
                        total = inside[:, last]
                        left = left - total if s == 0 else jnp.concatenate([left[:s], left[s:] - total], axis=0)
                    ws.append(w_b)
                    gs.append(g_b)
                gscans = [_dot(jnp.concatenate(gs[hh], axis=0), before[...]) for hh in hs]
                dzs = []
                for hh in hs:
                    g_before = sums[hh][1]
                    dz_b = []
                    for b, s in enumerate(starts):
                        inside = gscans[hh][offs[b]:offs[b] + t - s]
                        beta = jnp.exp(logs[hh][b][0]).astype(F32)
                        g = gs[hh][b].astype(F32)
                        dz_b.append(((g - (g + inside + g_before[s:]) * beta) * scale).astype(BF16))
                        total = inside[:, last] + g[:, last]
                        g_before = g_before + total if s == 0 else jnp.concatenate(
                            [g_before[:s], g_before[s:] + total], axis=0)
                    dzs.append(dz_b)
                for hh in hs:
                    for b, s in enumerate(starts):
                        dkv_t[1, i, cols[hh], s:s + sw] = _dot(dot_s[cols[hh], s:], ws[hh][b])
                for hh in hs:
                    for b, s in enumerate(starts):
                        dkv_t[0, i, cols[hh], s:s + sw] = _dot(qt_s[cols[hh], s:], dzs[hh][b])
                for hh in hs:
                    for b, s in enumerate(starts):
                        dq[s:, cols[hh]] += _dot(dzs[hh][b], ks[pl.ds(r0 + s, sw), cols[hh]])

            qt_s[...] = qs[pl.ds(r0, t), :].T
            dot_s[...] = dos[pl.ds(r0, t), :].T
            zero = jnp.zeros((t, 1), F32)
            dq[...] = jnp.zeros_like(dq)
            sums = lax.fori_loop(0, i, tile, ((zero, zero),) * nh)
            diagonal_tile(sums)
            dq_all[pl.ds(r0, t), :] = dq[...]
            return carry

        lax.fori_loop(0, nblk, qblock, 0)

    pairs = heads // nh

    ns = len(stacks)

    def body(qs, ks, v_ref, zb_ref, o_ref, dyb_ref, tot_ref, dproj_in, *refs):
        del dproj_in
        st_in, out_ref, st_out = refs[:ns], refs[ns], refs[ns + 1:2 * ns + 1]
        (kts, vts, dos, dzb, dq_all, dkv_t, qt_s, dot_s, upto, before, dq, stage, stage_sems,
         send_sems, recv_sems, local_sems) = refs[2 * ns + 1:]
        step = pl.program_id(0) * pairs + pl.program_id(1)
        exchange = functools.partial(_stack_exchange, _me(), st_in, st_out, 1, send_sems, recv_sems, local_sems)

        @pl.when(step == 0)
        def _():
            local, remote, _ = exchange(arrivals=False)
            for cp in local + remote:
                cp.start()

        def out_copies(s):
            rows_ = pl.ds(pl.multiple_of((s // pairs) * seq, seq), seq)
            return [pltpu.make_async_copy(
                stage.at[k], out_ref.at[rows_, pl.ds(pl.multiple_of((3 + k) * d + (s % pairs) * wide, wide), wide)],
                stage_sems.at[k]) for k in range(4)]

        compute(qs, ks, v_ref, zb_ref, o_ref, dyb_ref, tot_ref, kts, vts, dos, dzb, dq_all, dkv_t, qt_s, dot_s,
                upto, before, dq)

        @pl.when(step > 0)
        def _():
            for cp in out_copies(step - 1):
                cp.wait()

        stage[0] = dq_all[...].astype(BF16)
        for k in range(2):
            for jb in range(nblk):
                stage[1 + k, jb * t:(jb + 1) * t, :] = dkv_t[k, jb].astype(BF16).T
        stage[3] = dzb[...]
        for cp in out_copies(step):
            cp.start()

        @pl.when(step == batch * pairs - 1)
        def _():
            for cp in out_copies(step):
                cp.wait()
            local, remote, landed = exchange()
            for cp in remote:
                cp.wait_send()
            for cp in landed:
                cp.wait_recv()
            for cp in local:
                cp.wait()

    col0 = d // wide
    seg = lambda k: pl.BlockSpec((seq, wide), lambda b, h: (b, k * col0 + h))
    head = pl.BlockSpec((seq, wide), lambda b, h: (b, h))
    any_spec = pl.BlockSpec(memory_space=pl.ANY)
    return pl.pallas_call(
        body, name="sb_bwd", grid=(batch, pairs),
        in_specs=[seg(3), seg(4), seg(5), seg(6), head, head,
                  pl.BlockSpec((nh, seq, 1), lambda b, h: (b * pairs + h, 0, 0)), any_spec] + [any_spec] * ns,
        out_specs=[any_spec] * (ns + 1),
        out_shape=[SDS(dproj.shape, dproj.dtype)] + [SDS(s.shape, s.dtype) for s in stacks[:-1]] + [
            SDS((N_DEV,) + stacks[-1].shape, stacks[-1].dtype)],
        input_output_aliases={7: 0},
        scratch_shapes=[pltpu.VMEM((nblk, wide, t), BF16)] * 2 + [
            pltpu.VMEM((seq, wide), BF16), pltpu.VMEM((seq, wide), BF16),
            pltpu.VMEM((seq, wide), F32), pltpu.VMEM((2, nblk, wide, t), F32),
            pltpu.VMEM((wide, t), BF16), pltpu.VMEM((wide, t), BF16),
            pltpu.VMEM((sw, sw), BF16), pltpu.VMEM((sw, sw), BF16), pltpu.VMEM((t, wide), F32),
            pltpu.VMEM((4, seq, wide), BF16), pltpu.SemaphoreType.DMA((4,)),
            pltpu.SemaphoreType.DMA((7 * ns,)), pltpu.SemaphoreType.DMA((7 * ns,)),
            pltpu.SemaphoreType.DMA((ns,))],
        compiler_params=_params(("arbitrary", "arbitrary")),
    )(proj, proj, proj, proj, o, dyb, tot, dproj, *stacks)


def _branch_a_bwd(proj, dya, norm_v, w_s, b_col, dproj):
    n = proj.shape[0]
    d = norm_v.shape[1]
    groups, chunk, _ = w_s.shape
    tr = _tile(n, 2 * chunk)

    def body(u_ref, v_ref, z_ref, dya_ref, gv_ref, ws_ref, b_ref, dproj_in,
             out_ref, dws_ref, dbias_ref, dgv_ref, vn_s, dmix_s, dvn_s, db_ref):
        del dproj_in

        @pl.when(pl.program_id(0) == 0)
        def _():
            dws_ref[...] = jnp.zeros_like(dws_ref)
            db_ref[...] = jnp.zeros_like(db_ref)
            dgv_ref[...] = jnp.zeros_like(dgv_ref)

        row, col = _iotas(chunk)
        tril = col <= row
        gv = gv_ref[...]
        vg16, dvg_dv = _gelu(v_ref[...])
        vg = vg16.astype(F32)
        r = _rms_scale(vg)
        vh = vg * r
        vn_s[...] = (vh * gv).astype(BF16)
        ug, dug_du = _gelu(u_ref[...])
        sz, dsz = _silu(z_ref[...])
        dya_v = dya_ref[...]
        dmix_s[...] = dya_v * ug * sz
        du_scale = sz * dug_du
        dz_scale = ug * dsz
        for g in range(groups):
            wm = jnp.where(tril, ws_ref[g], 0.0).astype(BF16)
            cs = slice(g * chunk, (g + 1) * chunk)
            for c in range(tr // chunk):
                rs = slice(c * chunk, (c + 1) * chunk)
                vn = vn_s[rs, cs]
                mixed = _dot(wm, vn) + b_ref[g]
                dmix16 = dmix_s[rs, cs]
                dws_ref[g] += _dot_nt(dmix16, vn)
                db_ref[g] += dmix16.astype(F32)
                dvn_s[rs, cs] = _dot_tn(wm, dmix16)
                t_u = dya_v[rs, cs] * mixed.astype(BF16)
                out_ref[rs, g * chunk:(g + 1) * chunk] = t_u * du_scale[rs, cs]
                out_ref[rs, 2 * d + g * chunk:2 * d + (g + 1) * chunk] = t_u * dz_scale[rs, cs]
        dvn = dvn_s[...]
        dgv_ref[...] += jnp.sum(dvn * vh, axis=0, keepdims=True)
        dvh = dvn * gv
        dvg = r * (dvh - vh * jnp.mean(dvh * vh, axis=-1, keepdims=True))
        out_ref[:, d:2 * d] = (dvg * dvg_dv.astype(F32)).astype(BF16)

        @pl.when(pl.program_id(0) == n // tr - 1)
        def _():
            for g in range(groups):
                dbias_ref[g:g + 1, :] = jnp.sum(db_ref[g].T, axis=0, keepdims=True)

    seg = lambda k: pl.BlockSpec((tr, d), lambda i: (i, k))
    return pl.pallas_call(
        body, name="branch_a_bwd", grid=(n // tr,),
        in_specs=[seg(0), seg(1), seg(2), seg(0),
                  pl.BlockSpec((1, d), lambda i: (0, 0)),
                  pl.BlockSpec((groups, chunk, chunk), lambda i: (0, 0, 0)),
                  pl.BlockSpec((groups, chunk, 1), lambda i: (0, 0, 0)),
                  pl.BlockSpec(memory_space=pl.ANY)],
        out_specs=[pl.BlockSpec((tr, 3 * d), lambda i: (i, 0)),
                   pl.BlockSpec((groups, chunk, chunk), lambda i: (0, 0, 0)),
                   pl.BlockSpec((groups, chunk), lambda i: (0, 0)),
                   pl.BlockSpec((1, d), lambda i: (0, 0))],
        out_shape=[SDS(dproj.shape, dproj.dtype), SDS((groups, chunk, chunk), F32),
                   SDS((groups, chunk), F32), SDS((1, d), F32)],
        input_output_aliases={7: 0},
        scratch_shapes=[pltpu.VMEM((tr, d), BF16), pltpu.VMEM((tr, d), BF16), pltpu.VMEM((tr, d), F32),
                        pltpu.VMEM((groups, chunk, chunk), F32)],
        compiler_params=_params(("arbitrary",)),
    )(proj, proj, proj, dya, norm_v, w_s, b_col, dproj)


def _dx(dproj, wg_in, x2d, dx2, norm_in):
    n, d = x2d.shape
    nsh = N_DEV // 2
    esh = wg_in.shape[1] // nsh
    tm = _tile(n, 1024)

    def body(dp_ref, w_ref, x_ref, dx2_ref, g_ref, gx_ref, dg_ref, acc):
        i, k = pl.program_id(0), pl.program_id(1)

        @pl.when(jnp.logical_and(i == 0, k == 0))
        def _():
            dg_ref[...] = jnp.zeros_like(dg_ref)

        @pl.when(k == 0)
        def _():
            acc[...] = jnp.zeros_like(acc)

        acc[...] += _dot_nt(dp_ref[...], w_ref[...])

        @pl.when(k == nsh - 1)
        def _():
            dh = acc[...]
            x = x_ref[...]
            r = _rms_scale(x)
            xh = x * r
            dg_ref[...] += jnp.sum(dh * xh, axis=0, keepdims=True)
            dxh = dh * g_ref[...]
            gx_ref[...] = dx2_ref[...] + r * (dxh - xh * jnp.mean(dxh * xh, axis=-1, keepdims=True))

    rows = pl.BlockSpec((tm, d), lambda i, k: (i, 0))
    vec = pl.BlockSpec((1, d), lambda i, k: (0, 0))
    return pl.pallas_call(
        body, name="dx", grid=(n // tm, nsh),
        in_specs=[pl.BlockSpec((tm, esh), lambda i, k: (i, k)),
                  pl.BlockSpec((d, esh), lambda i, k: (0, k)), rows, rows, vec],
        out_specs=[rows, vec],
        out_shape=[SDS((n, d), F32), SDS((1, d), F32)],
        scratch_shapes=[pltpu.VMEM((tm, d), F32)],
        compiler_params=_params(("arbitrary", "arbitrary")),
    )(dproj, wg_in, x2d, dx2, norm_in)


def _adamw_outputs(g_ref, d_ref, m_ref, v_ref, g, w, m, v):
    delta, m2, v2 = _adamw(w, g, m, v)
    g_ref[...] = g
    d_ref[...] = delta
    m_ref[...] = m2
    v_ref[...] = v2


def _reduce_adamw(slots, w, m, v, name, transposed=False):
    r, c = w.shape
    tr = _tile(r, 128)

    def body(s_ref, w_ref, m_ref, v_ref, g_out, d_out, m_out, v_out):
        g = s_ref[0].astype(F32)
        for k in range(1, N_DEV):
            g = g + s_ref[k].astype(F32)
        if transposed:
            g = g.T
        _adamw_outputs(g_out, d_out, m_out, v_out, g, w_ref[...], m_ref[...], v_ref[...])

    blk = pl.BlockSpec((tr, c), lambda i: (i, 0))
    slot_blk = (pl.BlockSpec((N_DEV, c, tr), lambda i: (0, 0, i)) if transposed
                else pl.BlockSpec((N_DEV, tr, c), lambda i: (0, i, 0)))
    return pl.pallas_call(
        body, name=name, grid=(r // tr,),
        in_specs=[slot_blk, blk, blk, blk],
        out_specs=[blk] * 4,
        out_shape=[SDS((r, c), F32)] * 4,
        compiler_params=_params(("parallel",)),
    )(slots, w, m, v)


def _adamw_small(g, w, m, v, name):
    def body(g_ref, w_ref, m_ref, v_ref, g_out, d_out, m_out, v_out):
        _adamw_outputs(g_out, d_out, m_out, v_out, g_ref[...], w_ref[...], m_ref[...], v_ref[...])

    return pl.pallas_call(
        body, name=name,
        out_shape=[SDS(g.shape, F32)] * 4,
        in_specs=[pl.BlockSpec(memory_space=pltpu.VMEM)] * 4,
        out_specs=[pl.BlockSpec(memory_space=pltpu.VMEM)] * 4,
    )(g, w, m, v)


def kernel(x, norm_in, w_in, norm_v, w_s, b_s, w_o_gmlp, w_o_sb, w_out, norm_final, loss_target, m_norm_in, m_w_in, m_norm_v, m_w_s, m_b_s, m_w_o_gmlp, m_w_o_sb, m_w_out, m_norm_final, v_norm_in, v_w_in, v_norm_v, v_w_s, v_b_s, v_w_o_gmlp, v_w_o_sb, v_w_out, v_norm_final):
    batch, seq, d = x.shape
    n = batch * seq
    groups, chunk = w_s.shape[1], w_s.shape[2]
    hd = LANE
    x2d = x.reshape(n, d)
    tgt = loss_target.reshape(n, d)
    b_col = b_s[0].reshape(groups, chunk, 1)
    norm_final2 = norm_final.reshape(1, d)

    my_slot = _slot(_me()).astype(jnp.int32).reshape(1)
    proj, h, wg_in, wg_oa, wg_ob, wg_out = _gather_in_proj(
        x2d, norm_in, w_in[0], [w_o_gmlp[0], w_o_sb[0], w_out[0]], my_slot)
    rsh = wg_oa.shape[1]
    wf_oa, wf_ob, wf_out = (w.reshape(N_DEV * rsh, d) for w in (wg_oa, wg_ob, wg_out))
    ya = _branch_a_fwd(proj, norm_v, w_s[0], b_col)
    yb, o, sb_tot = _sb_fwd(proj, batch, seq, d, hd)
    dproj, dx2, dya, dyb, merged, dpa, dpb, loss_vec, dgf = _tail(
        x2d, tgt, ya, yb, proj, wf_oa, wf_ob, wf_out, norm_final2)
    gp_wo = _dw_o([(ya, dpa), (yb, dpb), (merged, dx2)])
    dproj, gp_ws, gp_b, gp_nv = _branch_a_bwd(proj, dya, norm_v, w_s[0], b_col, dproj)

    slab = lambda a: a.reshape(d // LANE, LANE)
    gc = groups * chunk
    packed = jnp.concatenate([gp_ws.reshape(gc, chunk), gp_b, slab(gp_nv), slab(dgf), slab(loss_vec)], axis=0)
    dproj, s_oa, s_ob, s_out, packs = _sb_bwd(
        proj, o, dyb, sb_tot, dproj, [gp_wo[k].reshape(N_DEV, rsh, d) for k in range(3)] + [packed],
        batch, seq, d, hd)
    grad_x, gp_nin = _dx(dproj, wg_in, x2d, dx2, norm_in)
    s_win, late_packs = _dw_in_exchange(h, dproj, my_slot, slab(gp_nin))
    tot, loss_slab = _finish_small(packs, late_packs, groups, chunk)
    ns = d // LANE
    g_ws = tot[:gc]
    g_b = tot[gc:gc + groups]
    g_nv, g_nf, _, g_nin = (tot[gc + groups + k * ns:gc + groups + (k + 1) * ns] for k in range(4))
    loss = loss_slab[0, 0]

    res = {}
    res["w_in"] = _reduce_adamw(s_win, w_in[0], m_w_in[0], v_w_in[0], "adamw_w_in", transposed=True)
    res["w_o_gmlp"] = _reduce_adamw(s_oa, w_o_gmlp[0], m_w_o_gmlp[0], v_w_o_gmlp[0], "adamw_w_o_gmlp")
    res["w_o_sb"] = _reduce_adamw(s_ob, w_o_sb[0], m_w_o_sb[0], v_w_o_sb[0], "adamw_w_o_sb")
    res["w_out"] = _reduce_adamw(s_out, w_out[0], m_w_out[0], v_w_out[0], "adamw_w_out")
    res["norm_in"] = _adamw_small(g_nin, slab(norm_in), slab(m_norm_in), slab(v_norm_in), "adamw_norm_in")
    res["norm_v"] = _adamw_small(g_nv, slab(norm_v), slab(m_norm_v), slab(v_norm_v), "adamw_norm_v")
    res["norm_final"] = _adamw_small(g_nf, slab(norm_final), slab(m_norm_final), slab(v_norm_final), "adamw_norm_final")
    res["w_s"] = _adamw_small(g_ws, w_s.reshape(gc, chunk), m_w_s.reshape(gc, chunk), v_w_s.reshape(gc, chunk), "adamw_w_s")
    res["b_s"] = _adamw_small(g_b, b_s[0], m_b_s[0], v_b_s[0], "adamw_b_s")

    shapes = {"norm_in": norm_in.shape, "w_in": w_in.shape, "norm_v": norm_v.shape, "w_s": w_s.shape,
              "b_s": b_s.shape, "w_o_gmlp": w_o_gmlp.shape, "w_o_sb": w_o_sb.shape, "w_out": w_out.shape,
              "norm_final": norm_final.shape}
    names = list(shapes)
    outs = [loss, grad_x.reshape(batch, seq, d)]
    for kind in range(4):
        outs += [res[name][kind].reshape(shapes[name]) for name in names]
    return tuple(outs)
```

```python
import functools
import math

import jax
import jax.numpy as jnp
from jax import lax
from jax.experimental import pallas as pl
from jax.experimental.pallas import tpu as pltpu

F32 = jnp.float32
BF16 = jnp.bfloat16
SDS = jax.ShapeDtypeStruct
MESH_ID = pl.DeviceIdType.MESH

N_DEV = 8
LANE = 128
SUBLANE = 8
VMEM_LIMIT = 56 * 1024 * 1024
SB_TILE = 512
SB_TILE_BWD = 512
SB_SCAN = 256
SB_HEADS = 2
MASKED_LOG = -1e30
RMS_EPS = 1e-6

ADAM_LR = 0.001
ADAM_B1 = 0.9
ADAM_B2 = 0.999
ADAM_EPS = 1e-08
ADAM_WD = 0.01
ADAM_STEP = 10

NT_DIMS = (((1,), (1,)), ((), ()))
TN_DIMS = (((0,), (0,)), ((), ()))


def _params(semantics=None):
    return pltpu.CompilerParams(dimension_semantics=semantics, vmem_limit_bytes=VMEM_LIMIT)


def _tile(n, preferred):
    t = min(n, preferred)
    assert n % t == 0, (n, t)
    return t


def _sigmoid(x):
    return 1.0 / (1.0 + jnp.exp(-x))


def _silu(x):
    s = _sigmoid(x)
    return x * s, s * (1.0 + x * (1.0 - s))


def _gelu(x):
    k = math.sqrt(2.0 / math.pi)
    x2 = x * x
    t = jnp.tanh(k * (x + 0.044715 * (x * x2)))
    cdf = 0.5 * (1.0 + t)
    return x * cdf, cdf + 0.5 * x * (1.0 - t * t) * (k * (1.0 + 3.0 * 0.044715 * x2))


def _rms_scale(x):
    return lax.rsqrt(jnp.mean(x * x, axis=-1, keepdims=True) + RMS_EPS)


def _iotas(n):
    return (lax.broadcasted_iota(jnp.int32, (n, n), 0), lax.broadcasted_iota(jnp.int32, (n, n), 1))


def _adamw(w, g, m, v):
    m = ADAM_B1 * m + (1.0 - ADAM_B1) * g
    v = ADAM_B2 * v + (1.0 - ADAM_B2) * (g * g)
    m_hat = m / (1.0 - ADAM_B1 ** ADAM_STEP)
    v_hat = v / (1.0 - ADAM_B2 ** ADAM_STEP)
    delta = -ADAM_LR * (m_hat / (jnp.sqrt(v_hat) + ADAM_EPS) + ADAM_WD * w)
    return delta, m, v


def _dot(a, b):
    return jnp.dot(a, b, preferred_element_type=F32)


def _dot_nt(a, b):
    return lax.dot_general(a, b, NT_DIMS, preferred_element_type=F32)


def _dot_tn(a, b):
    return lax.dot_general(a, b, TN_DIMS, preferred_element_type=F32)


def _sb_logs(raw, scale, valid):
    z = (raw * scale).astype(BF16)
    log_beta = jnp.minimum(z, 0) - jnp.log(1 + jnp.exp(-jnp.abs(z)))
    log_rest = log_beta - z
    if valid is not None:
        log_beta = jnp.where(valid, log_beta, MASKED_LOG)
        log_rest = jnp.where(valid, log_rest, 0)
    return log_beta, log_rest


def _me():
    return lax.axis_index("x"), lax.axis_index("y"), lax.axis_index("c")


def _slot(p):
    return 4 * p[0] + 2 * p[1] + p[2]


def _peer(me, k):
    flips = ((k >> 2) & 1, (k >> 1) & 1, k & 1)
    return tuple(1 - a if f else a for a, f in zip(me, flips))


def _stack_exchange(me, st_in, st_out, n_whole, send_sems, recv_sems, local_sems, arrivals=True):
    mine = _slot(me)
    ns = len(st_in)
    part = lambda a, dev: st_in[a] if a >= ns - n_whole else st_in[a].at[_slot(dev)]
    local = [pltpu.make_async_copy(part(a, me), st_out[a].at[mine], local_sems.at[a]) for a in range(ns)]
    remote, landed = [], []
    for k in range(1, N_DEV):
        peer = _peer(me, k)
        for a in range(ns):
            sems = dict(send_sem=send_sems.at[7 * a + k - 1], recv_sem=recv_sems.at[7 * a + k - 1])
            remote.append(pltpu.make_async_remote_copy(
                src_ref=part(a, peer), dst_ref=st_out[a].at[mine],
                device_id=peer, device_id_type=MESH_ID, **sems))
            if arrivals:
                got = st_out[a].at[_slot(peer)]
                landed.append(pltpu.make_async_remote_copy(
                    src_ref=got, dst_ref=got, device_id=me, device_id_type=MESH_ID, **sems))
    return local, remote, landed


def _gather_in_proj(x2d, norm_in, w_in_sh, wo_shards, my_slot):
    n, d = x2d.shape
    esh = w_in_sh.shape[1]
    pw = 2 * esh
    n_chip = N_DEV // 2
    tm = _tile(n, 1024)
    n_i = n // tm
    mid = n_i // 2
    no = len(wo_shards)
    flip_at = lambda st: jnp.where(st == 1, 2, jnp.where(st == 2, 1, jnp.where(st == 3, 3, 0)))

    def body(me_ref, x_ref, g_ref, win_ref, *refs):
        del me_ref
        wo_in = refs[:no]
        proj_ref, h_ref, wg_ref = refs[no:no + 3]
        wo_out = refs[no + 3:2 * no + 3]
        wv, stage = refs[2 * no + 3:2 * no + 5]
        wo_stage = refs[2 * no + 5:3 * no + 5]
        send_sems, recv_sems, pair_sems, own_sems, wo_send, wo_recv, wo_local = refs[3 * no + 5:]
        st, i = pl.program_id(0), pl.program_id(1)
        x, y, c = _me()
        me, sibling = (x, y, c), (x, y, 1 - c)
        chips = [(1 - x, y), (x, 1 - y), (1 - x, 1 - y)]
        chip_id = lambda p: 2 * p[0] + p[1]

        def window(chip, core):
            return wv.at[chip_id(chip), :, pl.ds(pl.multiple_of(core * esh, LANE), esh)]

        def copy(k, block, to, src=None):
            dst = window(block[:2], block[2])
            return pltpu.make_async_remote_copy(
                src_ref=dst if src is None else src, dst_ref=dst,
                send_sem=send_sems.at[k], recv_sem=recv_sems.at[k], device_id=to, device_id_type=MESH_ID)

        def wo_copy(a, k, block, to, src=None):
            dst = wo_out[a].at[_slot(block)]
            return pltpu.make_async_remote_copy(
                src_ref=dst if src is None else src, dst_ref=dst,
                send_sem=wo_send.at[7 * a + k], recv_sem=wo_recv.at[7 * a + k], device_id=to, device_id_type=MESH_ID)

        def own_copy():
            return pltpu.make_async_copy(stage, window((x, y), c), own_sems.at[0])

        def wo_own_copy(a):
            return pltpu.make_async_copy(wo_stage[a], wo_out[a].at[_slot(me)], wo_local.at[a])

        def pair_copy(step):
            chip = jnp.bitwise_xor(chip_id((x, y)), flip_at(step))
            return pltpu.make_async_copy(wv.at[chip], wg_ref.at[:, pl.ds(pl.multiple_of(chip * pw, LANE), pw)],
                                         pair_sems.at[step])

        first = jnp.logical_and(st == 0, i == 0)

        @pl.when(first)
        def _():
            stage[...] = win_ref[...].astype(BF16)
            own_copy().start()
            copy(0, me, sibling, src=stage).start()
            for j in range(2):
                copy(1 + j, me, (*chips[j], c), src=stage).start()
            own_copy().wait()
            copy(0, sibling, me).wait_recv()
            pair_copy(0).start()

        for s_ in range(n_chip - 1):
            @pl.when(jnp.logical_and(st == s_, i == mid))
            def _():
                copy(1 + s_, (*chips[s_], c), me).wait_recv()
                copy(4 + s_, (*chips[s_], c), sibling).start()
                if s_ == 0:
                    copy(3, me, (*chips[2], c), src=stage).start()
                if s_ == 1:
                    for a in range(no):
                        wo_stage[a][...] = wo_in[a][...].astype(BF16)
                        wo_own_copy(a).start()
                        wo_copy(a, 0, me, sibling, src=wo_stage[a]).start()
                        for j, chip in enumerate(chips):
                            wo_copy(a, 1 + j, me, (*chip, c), src=wo_stage[a]).start()
                if s_ == 2:
                    for a in range(no):
                        for j, chip in enumerate(chips):
                            wo_copy(a, 1 + j, (*chip, c), me).wait_recv()
                            wo_copy(a, 4 + j, (*chip, c), sibling).start()

        for s_ in range(1, n_chip):
            @pl.when(jnp.logical_and(st == s_, i == 0))
            def _():
                copy(3 + s_, (*chips[s_ - 1], 1 - c), me).wait_recv()
                pair_copy(s_).start()

        xv = x_ref[...]
        h = (xv * _rms_scale(xv) * g_ref[...]).astype(BF16)

        @pl.when(st == 0)
        def _():
            h_ref[...] = h

        chip_now = jnp.bitwise_xor(chip_id((x, y)), flip_at(st))
        proj_ref[...] = _dot(h, wv[chip_now]).astype(BF16)

        @pl.when(jnp.logical_and(st == n_chip - 1, i == n_i - 1))
        def _():
            copy(0, me, sibling, src=stage).wait_send()
            for j, chip in enumerate(chips):
                copy(1 + j, me, (*chip, c), src=stage).wait_send()
                copy(4 + j, (*chip, c), sibling).wait_send()
            for s_ in range(n_chip):
                pair_copy(s_).wait()
            for a in range(no):
                wo_copy(a, 0, me, sibling, src=wo_stage[a]).wait_send()
                wo_copy(a, 0, sibling, me).wait_recv()
                for j, chip in enumerate(chips):
                    wo_copy(a, 1 + j, me, (*chip, c), src=wo_stage[a]).wait_send()
                    wo_copy(a, 4 + j, (*chip, c), sibling).wait_send()
                    wo_copy(a, 4 + j, (*chip, 1 - c), me).wait_recv()
                wo_own_copy(a).wait()

    any_spec = pl.BlockSpec(memory_space=pl.ANY)
    vmem = pl.BlockSpec(memory_space=pltpu.VMEM)
    grid_spec = pltpu.PrefetchScalarGridSpec(
        num_scalar_prefetch=1, grid=(n_chip, n_i),
        in_specs=[pl.BlockSpec((tm, d), lambda st, i, me: (i, 0)),
                  pl.BlockSpec((1, d), lambda st, i, me: (0, 0)), vmem] + [vmem] * no,
        out_specs=[pl.BlockSpec((tm, pw), lambda st, i, me: (i, jnp.bitwise_xor(me[0] // 2, flip_at(st)))),
                   pl.BlockSpec((tm, d), lambda st, i, me: (jnp.where(st == 0, i, n_i - 1), 0)),
                   any_spec] + [any_spec] * no,
        scratch_shapes=[pltpu.VMEM((n_chip, d, pw), BF16), pltpu.VMEM((d, esh), BF16)] + [
            pltpu.VMEM(s.shape, BF16) for s in wo_shards] + [
            pltpu.SemaphoreType.DMA((7,)), pltpu.SemaphoreType.DMA((7,)),
            pltpu.SemaphoreType.DMA((n_chip,)), pltpu.SemaphoreType.DMA((1,)),
            pltpu.SemaphoreType.DMA((7 * no,)), pltpu.SemaphoreType.DMA((7 * no,)),
            pltpu.SemaphoreType.DMA((no,))])
    return pl.pallas_call(
        body, name="gather_in_proj", grid_spec=grid_spec,
        out_shape=[SDS((n, n_chip * pw), BF16), SDS((n, d), BF16), SDS((d, n_chip * pw), BF16)] + [
            SDS((N_DEV,) + s.shape, BF16) for s in wo_shards],
        compiler_params=pltpu.CompilerParams(dimension_semantics=("arbitrary", "arbitrary"),
                                             vmem_limit_bytes=VMEM_LIMIT),
    )(my_slot, x2d, norm_in, w_in_sh, *wo_shards)


EXCHANGE_ORDER = ((6, 7, 4, 2, 5, 3, 1, 0), (7, 6, 2, 4, 3, 5, 1, 0))


def _owner_at(mine, j):
    k = 0
    for step in range(N_DEV - 1):
        k = jnp.where(j == step, jnp.where(mine % 2 == 0, EXCHANGE_ORDER[0][step], EXCHANGE_ORDER[1][step]), k)
    return jnp.bitwise_xor(mine, k)


def _dw_in_exchange(h, dproj, my_slot, packed):
    n, d = h.shape
    esh = dproj.shape[1] // N_DEV
    tk = _tile(n, 1024)
    nk = n // tk
    last_j = N_DEV - 1
    depth = 4

    def body(me_ref, h_ref, dp_ref, pk_in, win_out, pk_out,
             acc, sendbuf, win_send, win_recv, send_sems, recv_sems, local_sems):
        del me_ref
        j, k = pl.program_id(0), pl.program_id(1)
        me = _me()
        mine = _slot(me)

        def pack_copies():
            local = pltpu.make_async_copy(pk_in, pk_out.at[mine], local_sems.at[0])
            remote = [pltpu.make_async_remote_copy(
                src_ref=pk_in, dst_ref=pk_out.at[mine], send_sem=send_sems.at[kk - 1], recv_sem=recv_sems.at[kk - 1],
                device_id=_peer(me, kk), device_id_type=MESH_ID) for kk in range(1, N_DEV)]
            return local, remote

        def shard_copy(jj):
            owner = _owner_at(mine, jj)
            return pltpu.make_async_remote_copy(
                src_ref=sendbuf.at[jj % depth], dst_ref=win_out.at[mine],
                send_sem=win_send.at[jj % depth], recv_sem=win_recv.at[mine],
                device_id=(owner // 4, (owner // 2) % 2, owner % 2), device_id_type=MESH_ID)

        def own_copy():
            return pltpu.make_async_copy(sendbuf.at[last_j % depth], win_out.at[mine], local_sems.at[1])

        @pl.when(jnp.logical_and(j == 0, k == 0))
        def _():
            local, remote = pack_copies()
            for cp in [local] + remote:
                cp.start()

        @pl.when(k == 0)
        def _():
            acc[...] = jnp.zeros_like(acc)

        acc[...] += _dot_tn(dp_ref[...], h_ref[...])

        @pl.when(k == nk - 1)
        def _():
            @pl.when(j >= depth)
            def _():
                shard_copy(j - depth).wait_send()

            sendbuf[j % depth] = acc[...].astype(BF16)

            @pl.when(j < last_j)
            def _():
                shard_copy(j).start()

            @pl.when(j == last_j)
            def _():
                own_copy().start()
                for jj in range(last_j - depth + 1, last_j):
                    shard_copy(jj).wait_send()
                own_copy().wait()
                for src in range(N_DEV):
                    @pl.when(src != mine)
                    def _():
                        landed = win_out.at[src]
                        pltpu.make_async_remote_copy(
                            src_ref=landed, dst_ref=landed, send_sem=win_send.at[0], recv_sem=win_recv.at[src],
                            device_id=me, device_id_type=MESH_ID).wait_recv()
                local, remote = pack_copies()
                for cp in remote:
                    cp.wait_send()
                for kk in range(1, N_DEV):
                    landed = pk_out.at[_slot(_peer(me, kk))]
                    pltpu.make_async_remote_copy(
                        src_ref=landed, dst_ref=landed, send_sem=send_sems.at[kk - 1], recv_sem=recv_sems.at[kk - 1],
                        device_id=me, device_id_type=MESH_ID).wait_recv()
                local.wait()

    any_spec = pl.BlockSpec(memory_space=pl.ANY)
    grid_spec = pltpu.PrefetchScalarGridSpec(
        num_scalar_prefetch=1, grid=(N_DEV, nk),
        in_specs=[pl.BlockSpec((tk, d), lambda j, k, me: (k, 0)),
                  pl.BlockSpec((tk, esh), lambda j, k, me: (k, _owner_at(me[0], j))), any_spec],
        out_specs=[any_spec] * 2,
        scratch_shapes=[pltpu.VMEM((esh, d), F32), pltpu.VMEM((depth, esh, d), BF16),
                        pltpu.SemaphoreType.DMA((depth,)), pltpu.SemaphoreType.DMA((N_DEV,)),
                        pltpu.SemaphoreType.DMA((N_DEV - 1,)), pltpu.SemaphoreType.DMA((N_DEV - 1,)),
                        pltpu.SemaphoreType.DMA((2,))])
    return pl.pallas_call(
        body, name="dw_in_exchange", grid_spec=grid_spec,
        out_shape=[SDS((N_DEV, esh, d), BF16), SDS((N_DEV,) + packed.shape, packed.dtype)],
        compiler_params=_params(("arbitrary", "arbitrary")),
    )(my_slot, h, dproj, packed)


def _finish_small(packs, late_packs, groups, chunk):
    rows = packs.shape[1]
    late = late_packs.shape[1]
    gc = groups * chunk

    def body(p_ref, l_ref, sum_ref, loss_ref):
        row, col = _iotas(chunk)
        tril = col <= row
        for g in range(groups):
            rs = slice(g * chunk, (g + 1) * chunk)
            tot = p_ref[0, rs, :]
            for dev in range(1, N_DEV):
                tot = tot + p_ref[dev, rs, :]
            sum_ref[rs, :] = jnp.where(tril, tot, 0.0)
        rs = slice(gc, rows)
        tot = p_ref[0, rs, :]
        for dev in range(1, N_DEV):
            tot = tot + p_ref[dev, rs, :]
        sum_ref[rs, :] = tot
        loss_ref[...] = jnp.full((SUBLANE, LANE), jnp.sum(tot[rows - gc - SUBLANE:, :]), F32)
        tot = l_ref[0]
        for dev in range(1, N_DEV):
            tot = tot + l_ref[dev]
        sum_ref[rows:rows + late, :] = tot

    return pl.pallas_call(
        body, name="finish_small",
        out_shape=[SDS((rows + late, LANE), F32), SDS((SUBLANE, LANE), F32)],
        in_specs=[pl.BlockSpec(memory_space=pltpu.VMEM)] * 2,
        out_specs=[pl.BlockSpec(memory_space=pltpu.VMEM)] * 2,
        compiler_params=pltpu.CompilerParams(vmem_limit_bytes=VMEM_LIMIT),
    )(packs, late_packs)


def _branch_a_fwd(proj, norm_v, w_s, b_col):
    n = proj.shape[0]
    d = norm_v.shape[1]
    groups, chunk, _ = w_s.shape
    tr = _tile(n, 4 * chunk)

    def body(u_ref, v_ref, z_ref, gv_ref, ws_ref, b_ref, ya_ref, vn_s, pre_s):
        row, col = _iotas(chunk)
        tril = col <= row
        vg = _gelu(v_ref[...])[0].astype(F32)
        vn_s[...] = (vg * _rms_scale(vg) * gv_ref[...]).astype(BF16)
        pre_s[...] = _gelu(u_ref[...])[0] * _silu(z_ref[...])[0]
        for g in range(groups):
            wm = jnp.where(tril, ws_ref[g], 0.0).astype(BF16)
            cs = slice(g * chunk, (g + 1) * chunk)
            for c in range(tr // chunk):
                rs = slice(c * chunk, (c + 1) * chunk)
                mixed = _dot(wm, vn_s[rs, cs]) + b_ref[g]
                ya_ref[rs, cs] = (pre_s[rs, cs].astype(F32) * mixed).astype(BF16)

    seg = lambda k: pl.BlockSpec((tr, d), lambda i: (i, k))
    return pl.pallas_call(
        body, name="branch_a_fwd", grid=(n // tr,),
        in_specs=[seg(0), seg(1), seg(2),
                  pl.BlockSpec((1, d), lambda i: (0, 0)),
                  pl.BlockSpec((groups, chunk, chunk), lambda i: (0, 0, 0)),
                  pl.BlockSpec((groups, chunk, 1), lambda i: (0, 0, 0))],
        out_specs=pl.BlockSpec((tr, d), lambda i: (i, 0)),
        out_shape=SDS((n, d), BF16),
        scratch_shapes=[pltpu.VMEM((tr, d), BF16), pltpu.VMEM((tr, d), BF16)],
        compiler_params=_params(("parallel",)),
    )(proj, proj, proj, norm_v, w_s, b_col)


def _sb_fwd(proj, batch, seq, d, hd):
    heads = d // hd
    t = _tile(seq, SB_TILE)
    sw = _tile(t, SB_SCAN)
    nb = t // sw
    scale = hd ** -0.5
    nblk = seq // t
    nh = SB_HEADS
    wide = nh * hd
    cols = [slice(hh * hd, (hh + 1) * hd) for hh in range(nh)]

    def body(qs, k_ref, vs, zb_ref, yb_ref, o_ref, tot_ref, kts, later, acc):
        for jb in range(nblk):
            kts[jb] = k_ref[jb * t:(jb + 1) * t, :].T
        row, col = _iotas(t)
        later[...] = (row[:sw, :sw] > col[:sw, :sw]).astype(BF16)

        def qblock(i, carry):
            r0 = pl.multiple_of(i * t, t)

            def tile(j, runs):
                c0 = pl.multiple_of(j * t, t)
                logs = [_sb_logs(_dot(qs[pl.ds(r0, t), cs], kts[j, cs, :]), scale, None) for cs in cols]
                scans = [_dot(jnp.concatenate([logs[hh][1][:, b * sw:(b + 1) * sw] for b in range(nb)], axis=0),
                              later[...]) for hh in range(nh)]
                new_runs = []
                for hh in range(nh):
                    after = runs[hh]
                    blocks = [None] * nb
                    for b in reversed(range(nb)):
                        ks_ = slice(b * sw, (b + 1) * sw)
                        inside = scans[hh][b * t:(b + 1) * t]
                        blocks[b] = jnp.exp(logs[hh][0][:, ks_].astype(F32) + inside + after).astype(BF16)
                        after = after + inside[:, 0:1] + logs[hh][1][:, b * sw:b * sw + 1].astype(F32)
                    new_runs.append(after)
                    acc[:, cols[hh]] += _dot(jnp.concatenate(blocks, axis=1), vs[pl.ds(c0, t), cols[hh]])
                return tuple(new_runs)

            def diagonal_tile():
                starts = [b * sw for b in range(nb)]
                logs = [[_sb_logs(_dot(qs[pl.ds(r0 + s, t - s), cs], kts[i, cs, s:s + sw]), scale,
                                  col[:t - s, :sw] < row[:t - s, :sw]) for s in starts] for cs in cols]
                scans = [_dot(jnp.concatenate([lr for _, lr in logs[hh]], axis=0), later[...]) for hh in range(nh)]
                new_runs = []
                offs = [sum(t - s for s in starts[:b]) for b in range(nb)]
                for hh in range(nh):
                    after = jnp.zeros((t, 1), F32)
                    ws = [None] * nb
                    for b in reversed(range(nb)):
                        s = starts[b]
                        lb, lr = logs[hh][b]
                        inside = scans[hh][offs[b]:offs[b] + t - s]
                        ws[b] = jnp.exp(lb.astype(F32) + inside + after[s:]).astype(BF16)
                        total = inside[:, 0:1] + lr[:, 0:1].astype(F32)
                        after = after + total if s == 0 else jnp.concatenate([after[:s], after[s:] + total], axis=0)
                    new_runs.append(after)
                    acc[:, cols[hh]] = _dot(ws[0], vs[pl.ds(r0, sw), cols[hh]])
                    for b in range(1, nb):
                        acc[starts[b]:, cols[hh]] += _dot(ws[b], vs[pl.ds(r0 + starts[b], sw), cols[hh]])
                return tuple(new_runs)

            runs = diagonal_tile()
            runs = lax.fori_loop(0, i, lambda jj, rs: tile(i - 1 - jj, rs), runs)
            for hh in range(nh):
                out = acc[:, cols[hh]]
                o_ref[pl.ds(r0, t), cols[hh]] = out.astype(BF16)
                tot_ref[hh, pl.ds(r0, t), :] = runs[hh]
                sz, _ = _silu(zb_ref[pl.ds(r0, t), cols[hh]].astype(F32))
                yb_ref[pl.ds(r0, t), cols[hh]] = (out * sz).astype(BF16)
            return carry

        lax.fori_loop(0, nblk, qblock, 0)

    col0 = d // wide
    seg = lambda k: pl.BlockSpec((seq, wide), lambda b, h: (b, k * col0 + h))
    return pl.pallas_call(
        body, name="sb_fwd", grid=(batch, heads // nh),
        in_specs=[seg(3), seg(4), seg(5), seg(6)],
        out_specs=[pl.BlockSpec((seq, wide), lambda b, h: (b, h))] * 2 + [
            pl.BlockSpec((nh, seq, 1), lambda b, h: (b * (heads // nh) + h, 0, 0))],
        out_shape=[SDS((batch * seq, d), BF16), SDS((batch * seq, d), BF16), SDS((batch * heads, seq, 1), F32)],
        scratch_shapes=[pltpu.VMEM((nblk, wide, t), BF16), pltpu.VMEM((sw, sw), BF16), pltpu.VMEM((t, wide), F32)],
        compiler_params=_params(("parallel", "parallel")),
    )(proj, proj, proj, proj)


def _tail(x2d, tgt, ya, yb, proj, w_oa, w_ob, w_out, norm_final):
    n, d = x2d.shape
    e = proj.shape[1]
    tm = _tile(n, 256)
    steps = n // tm

    def body(x_ref, t_ref, ya_ref, yb_ref, ga_ref, gb_ref, woa_ref, wob_ref, wout_ref, gf_ref,
             dproj_ref, dx2_ref, dya_ref, dyb_ref, mrg_ref, dpa_ref, dpb_ref, loss_ref, dgf_ref, dg_s, dg_sems):
        i = pl.program_id(0)

        def gate_copy(step):
            rows_ = pl.ds(pl.multiple_of(step * tm, tm), tm)
            return pltpu.make_async_copy(dg_s.at[step % 2], dproj_ref.at[rows_, pl.ds(7 * d, 2 * d)],
                                         dg_sems.at[step % 2])

        @pl.when(i == 0)
        def _():
            loss_ref[...] = jnp.zeros_like(loss_ref)
            dgf_ref[...] = jnp.zeros_like(dgf_ref)

        @pl.when(i >= 2)
        def _():
            gate_copy(i - 2).wait()

        pa = _dot(ya_ref[...], woa_ref[...])
        pb = _dot(yb_ref[...], wob_ref[...])
        sa = _sigmoid(ga_ref[...].astype(F32))
        sb = _sigmoid(gb_ref[...].astype(F32))
        merged = (sa * pa + sb * pb).astype(BF16)
        mrg_ref[...] = merged
        x2 =x_ref[...] + _dot(merged, wout_ref[...])
        r2 = _rms_scale(x2)
        xh = x2 * r2
        gf = gf_ref[...]
        diff = xh * gf - t_ref[...]
        loss_ref[...] += jnp.sum(diff * diff, axis=0, keepdims=True) * (0.5 / d)
        dy = diff * (1.0 / d)
        dgf_ref[...] += jnp.sum(dy * xh, axis=0, keepdims=True)
        dxh = dy * gf
        dx2 = r2 * (dxh - xh * jnp.mean(dxh * xh, axis=-1, keepdims=True))
        dx2_ref[...] = dx2
        dm = _dot_nt(dx2.astype(BF16), wout_ref[...])
        dpa = (dm * sa).astype(BF16)
        dpb = (dm * sb).astype(BF16)
        dpa_ref[...] = dpa
        dpb_ref[...] = dpb
        dg_s[i % 2, :, 0:d] = (dm * pa * (sa * (1.0 - sa))).astype(BF16)
        dg_s[i % 2, :, d:2 * d] = (dm * pb * (sb * (1.0 - sb))).astype(BF16)
        gate_copy(i).start()
        dya_ref[...] = _dot_nt(dpa, woa_ref[...]).astype(BF16)
        dyb_ref[...] = _dot_nt(dpb, wob_ref[...]).astype(BF16)

        @pl.when(i == steps - 1)
        def _():
            if steps >= 2:
                gate_copy(i - 1).wait()
            gate_copy(i).wait()

    rows = lambda k=0: pl.BlockSpec((tm, d), lambda i: (i, k))
    full = pl.BlockSpec((d, d), lambda i: (0, 0))
    vec = pl.BlockSpec((1, d), lambda i: (0, 0))
    return pl.pallas_call(
        body, name="tail", grid=(steps,),
        in_specs=[rows(), rows(), rows(), rows(), rows(7), rows(8), full, full, full, vec],
        out_specs=[pl.BlockSpec(memory_space=pl.ANY),
                   rows(), rows(), rows(), rows(), rows(), rows(), vec, vec],
        out_shape=[SDS((n, e), BF16), SDS((n, d), F32), SDS((n, d), BF16), SDS((n, d), BF16),
                   SDS((n, d), BF16), SDS((n, d), BF16), SDS((n, d), BF16),
                   SDS((1, d), F32), SDS((1, d), F32)],
        scratch_shapes=[pltpu.VMEM((2, tm, 2 * d), BF16), pltpu.SemaphoreType.DMA((2,))],
        compiler_params=_params(("arbitrary",)),
    )(x2d, tgt, ya, yb, proj, proj, w_oa, w_ob, w_out, norm_final)


def _dw_o(pairs):
    n, d = pairs[0][0].shape
    tk = _tile(n, 1024)
    nk = n // tk
    npair = len(pairs)

    def body(*refs):
        a_refs, b_refs = refs[:npair], refs[npair:2 * npair]
        o_ref, acc = refs[2 * npair], refs[2 * npair + 1]
        p, k = pl.program_id(0), pl.program_id(1)

        @pl.when(k == 0)
        def _():
            acc[...] = jnp.zeros_like(acc)

        for q in range(npair):
            @pl.when(p == q)
            def _():
                acc[...] += _dot_tn(a_refs[q][...], b_refs[q][...].astype(BF16))

        @pl.when(k == nk - 1)
        def _():
            o_ref[0] = acc[...].astype(BF16)

    def tiles(q):
        return pl.BlockSpec((tk, d), lambda p, k: (jnp.where(p == q, k, jnp.where(p < q, 0, nk - 1)), 0))

    return pl.pallas_call(
        body, name="dw_o", grid=(npair, nk),
        in_specs=[tiles(q) for q in range(npair)] * 2,
        out_specs=pl.BlockSpec((1, d, d), lambda p, k: (p, 0, 0)),
        out_shape=SDS((npair, d, d), BF16),
        scratch_shapes=[pltpu.VMEM((d, d), F32)],
        compiler_params=_params(("arbitrary", "arbitrary")),
    )(*[a for a, _ in pairs], *[b for _, b in pairs])


def _sb_bwd(proj, o, dyb, tot, dproj, dw_stack, packed, batch, seq, d, hd):
    heads = d // hd
    t = _tile(seq, SB_TILE_BWD)
    sw = _tile(t, SB_SCAN)
    nb = t // sw
    scale = hd ** -0.5
    nblk = seq // t
    nh = SB_HEADS
    wide = nh * hd
    hs = range(nh)
    cols = [slice(hh * hd, (hh + 1) * hd) for hh in hs]
    blocks = [slice(b * sw, (b + 1) * sw) for b in range(nb)]
    last = slice(sw - 1, sw)

    def compute(qs, ks, v_ref, zb_ref, o_ref, dyb_ref, tot_ref, kts, vts, dos, dzb, dq_all, dkv_t, qt_s, dot_s,
                upto, before, dq):
        for jb in range(nblk):
            rows = slice(jb * t, (jb + 1) * t)
            kts[jb] = ks[rows, :].T
            vts[jb] = v_ref[rows, :].T
        sz, dsz = _silu(zb_ref[...])
        dyb_v = dyb_ref[...]
        dos[...] = dyb_v * sz
        dzb[...] = dyb_v * o_ref[...] * dsz
        row, col = _iotas(t)
        upto[...] = (row[:sw, :sw] <= col[:sw, :sw]).astype(BF16)
        before[...] = (row[:sw, :sw] < col[:sw, :sw]).astype(BF16)

        def qblock(i, carry):
            r0 = pl.multiple_of(i * t, t)

            def tile(j, sums):
                c0 = pl.multiple_of(j * t, t)
                q_i = [qs[pl.ds(r0, t), cs] for cs in cols]
                do_i = [dos[pl.ds(r0, t), cs] for cs in cols]
                logs = [_sb_logs(_dot(q_i[hh], kts[j, cols[hh], :]), scale, None) for hh in hs]
                dw = [_dot(do_i[hh], vts[j, cols[hh], :]) for hh in hs]
                scans = [_dot(jnp.concatenate([logs[hh][1][:, ks_] for ks_ in blocks], axis=0), upto[...]) for hh in hs]
                ws, gs, new_runs = [], [], []
                for hh in hs:
                    left = tot_ref[hh, pl.ds(r0, t), :] - sums[hh][0]
                    w_b, g_b = [], []
                    for b, ks_ in enumerate(blocks):
                        inside = scans[hh][b * t:(b + 1) * t]
                        w = jnp.exp(logs[hh][0][:, ks_].astype(F32) + (left - inside))
                        w_b.append(w.astype(BF16))
                        g_b.append((dw[hh][:, ks_] * w).astype(BF16))
                        left = left - inside[:, last]
                    ws.append(jnp.concatenate(w_b, axis=1))
                    gs.append(g_b)
                    new_runs.append(tot_ref[hh, pl.ds(r0, t), :] - left)
                gscans = [_dot(jnp.concatenate(gs[hh], axis=0), before[...]) for hh in hs]
                dzs, new_gruns = [], []
                for hh in hs:
                    g_before = sums[hh][1]
                    dz_b = []
                    for b, ks_ in enumerate(blocks):
                        inside = gscans[hh][b * t:(b + 1) * t]
                        beta = jnp.exp(logs[hh][0][:, ks_]).astype(F32)
                        g = gs[hh][b].astype(F32)
                        dz_b.append(((g - (g + inside + g_before) * beta) * scale).astype(BF16))
                        g_before = g_before + inside[:, last] + g[:, last]
                    dzs.append(jnp.concatenate(dz_b, axis=1))
                    new_gruns.append(g_before)
                for hh in hs:
                    dkv_t[1, j, cols[hh], :] += _dot(dot_s[cols[hh], :], ws[hh])
                for hh in hs:
                    dkv_t[0, j, cols[hh], :] += _dot(qt_s[cols[hh], :], dzs[hh])
                for hh in hs:
                    dq[:, cols[hh]] += _dot(dzs[hh], ks[pl.ds(c0, t), cols[hh]])
                return tuple((new_runs[hh], new_gruns[hh]) for hh in hs)

            def diagonal_tile(sums):
                starts = [b * sw for b in range(nb)]
                offs = [sum(t - s for s in starts[:b]) for b in range(nb)]
                q_b = [[qs[pl.ds(r0 + s, t - s), cs] for s in starts] for cs in cols]
                do_b = [[dos[pl.ds(r0 + s, t - s), cs] for s in starts] for cs in cols]
                logs = [[_sb_logs(_dot(q_b[hh][b], kts[i, cols[hh], s:s + sw]), scale,
                                  col[:t - s, :sw] < row[:t - s, :sw]) for b, s in enumerate(starts)] for hh in hs]
                dw = [[_dot(do_b[hh][b], vts[i, cols[hh], s:s + sw]) for b, s in enumerate(starts)] for hh in hs]
                scans = [_dot(jnp.concatenate([lr for _, lr in logs[hh]], axis=0), upto[...]) for hh in hs]
                ws, gs = [], []
                for hh in hs:
                    left = tot_ref[hh, pl.ds(r0, t), :] - sums[hh][0]
                    w_b, g_b = [], []
                    for b, s in enumerate(starts):
                        inside = scans[hh][offs[b]:offs[b] + t - s]
                        w = jnp.exp(logs[hh][b][0].astype(F32) + (left[s:] - inside))
                        w_b.append(w.astype(BF16))
                        g_b.append((dw[hh][b] * w).astype(BF16))
                        total = inside[:, last]
                        left = left - total if s == 0 else jnp.concatenate([left[:s], left[s:] - total], axis=0)
                    ws.append(w_b)
                    gs.append(g_b)
                gscans = [_dot(jnp.concatenate(gs[hh], axis=0), before[...]) for hh in hs]
                dzs = []
                for hh in hs:
                    g_before = sums[hh][1]
                    dz_b = []
                    for b, s in enumerate(starts):
                        inside = gscans[hh][offs[b]:offs[b] + t - s]
                        beta = jnp.exp(logs[hh][b][0]).astype(F32)
                        g = gs[hh][b].astype(F32)
                        dz_b.append(((g - (g + inside + g_before[s:]) * beta) * scale).astype(BF16))
                        total = inside[:, last] + g[:, last]
                        g_before = g_before + total if s == 0 else jnp.concatenate(
                            [g_before[:s], g_before[s:] + total], axis=0)
                    dzs.append(dz_b)
                for hh in hs:
                    for b, s in enumerate(starts):
                        dkv_t[1, i, cols[hh], s:s + sw] = _dot(dot_s[cols[hh], s:], ws[hh][b])
                for hh in hs:
                    for b, s in enumerate(starts):
                        dkv_t[0, i, cols[hh], s:s + sw] = _dot(qt_s[cols[hh], s:], dzs[hh][b])
                for hh in hs:
                    for b, s in enumerate(starts):
                        dq[s:, cols[hh]] += _dot(dzs[hh][b], ks[pl.ds(r0 + s, sw), cols[hh]])

            qt_s[...] = qs[pl.ds(r0, t), :].T
            dot_s[...] = dos[pl.ds(r0, t), :].T
            zero = jnp.zeros((t, 1), F32)
            dq[...] = jnp.zeros_like(dq)
            sums = lax.fori_loop(0, i, tile, ((zero, zero),) * nh)
            diagonal_tile(sums)
            dq_all[pl.ds(r0, t), :] = dq[...]
            return carry

        lax.fori_loop(0, nblk, qblock, 0)

    pairs = heads // nh

    nst = dw_stack.shape[0]
    ns = nst + 1

    def body(qs, ks, v_ref, zb_ref, o_ref, dyb_ref, tot_ref, dproj_in, dw_ref, pk_ref, out_ref, *refs):
        del dproj_in
        st_in = [dw_ref.at[k] for k in range(nst)] + [pk_ref]
        st_out = refs[:ns]
        (kts, vts, dos, dzb, dq_all, dkv_t, qt_s, dot_s, upto, before, dq, stage, stage_sems,
         send_sems, recv_sems, local_sems) = refs[ns:]
        step = pl.program_id(0) * pairs + pl.program_id(1)
        exchange = functools.partial(_stack_exchange, _me(), st_in, st_out, 1, send_sems, recv_sems, local_sems)

        @pl.when(step == 0)
        def _():
            local, remote, _ = exchange(arrivals=False)
            for cp in local + remote:
                cp.start()

        def out_copies(s):
            rows_ = pl.ds(pl.multiple_of((s // pairs) * seq, seq), seq)
            return [pltpu.make_async_copy(
                stage.at[k], out_ref.at[rows_, pl.ds(pl.multiple_of((3 + k) * d + (s % pairs) * wide, wide), wide)],
                stage_sems.at[k]) for k in range(4)]

        compute(qs, ks, v_ref, zb_ref, o_ref, dyb_ref, tot_ref, kts, vts, dos, dzb, dq_all, dkv_t, qt_s, dot_s,
                upto, before, dq)

        @pl.when(step > 0)
        def _():
            for cp in out_copies(step - 1):
                cp.wait()

        stage[0] = dq_all[...].astype(BF16)
        for k in range(2):
            for jb in range(nblk):
                stage[1 + k, jb * t:(jb + 1) * t, :] = dkv_t[k, jb].astype(BF16).T
        stage[3] = dzb[...]
        for cp in out_copies(step):
            cp.start()

        @pl.when(step == batch * pairs - 1)
        def _():
            for cp in out_copies(step):
                cp.wait()
            local, remote, landed = exchange()
            for cp in remote:
                cp.wait_send()
            for cp in landed:
                cp.wait_recv()
            for cp in local:
                cp.wait()

    col0 = d // wide
    seg = lambda k: pl.BlockSpec((seq, wide), lambda b, h: (b, k * col0 + h))
    head = pl.BlockSpec((seq, wide), lambda b, h: (b, h))
    any_spec = pl.BlockSpec(memory_space=pl.ANY)
    return pl.pallas_call(
        body, name="sb_bwd", grid=(batch, pairs),
        in_specs=[seg(3), seg(4), seg(5), seg(6), head, head,
                  pl.BlockSpec((nh, seq, 1), lambda b, h: (b * pairs + h, 0, 0))] + [any_spec] * 3,
        out_specs=[any_spec] * (ns + 1),
        out_shape=[SDS(dproj.shape, dproj.dtype)] + [SDS(dw_stack.shape[1:], dw_stack.dtype)] * nst + [
            SDS((N_DEV,) + packed.shape, packed.dtype)],
        input_output_aliases={7: 0},
        scratch_shapes=[pltpu.VMEM((nblk, wide, t), BF16)] * 2 + [
            pltpu.VMEM((seq, wide), BF16), pltpu.VMEM((seq, wide), BF16),
            pltpu.VMEM((seq, wide), F32), pltpu.VMEM((2, nblk, wide, t), F32),
            pltpu.VMEM((wide, t), BF16), pltpu.VMEM((wide, t), BF16),
            pltpu.VMEM((sw, sw), BF16), pltpu.VMEM((sw, sw), BF16), pltpu.VMEM((t, wide), F32),
            pltpu.VMEM((4, seq, wide), BF16), pltpu.SemaphoreType.DMA((4,)),
            pltpu.SemaphoreType.DMA((7 * ns,)), pltpu.SemaphoreType.DMA((7 * ns,)),
            pltpu.SemaphoreType.DMA((ns,))],
        compiler_params=_params(("arbitrary", "arbitrary")),
    )(proj, proj, proj, proj, o, dyb, tot, dproj, dw_stack, packed)


def _branch_a_bwd(proj, dya, norm_v, w_s, b_col, dproj):
    n = proj.shape[0]
    d = norm_v.shape[1]
    groups, chunk, _ = w_s.shape
    tr = _tile(n, 2 * chunk)

    def body(u_ref, v_ref, z_ref, dya_ref, gv_ref, ws_ref, b_ref, dproj_in,
             out_ref, dws_ref, dbias_ref, dgv_ref, vn_s, dmix_s, dvn_s, db_ref):
        del dproj_in

        @pl.when(pl.program_id(0) == 0)
        def _():
            dws_ref[...] = jnp.zeros_like(dws_ref)
            db_ref[...] = jnp.zeros_like(db_ref)
            dgv_ref[...] = jnp.zeros_like(dgv_ref)

        row, col = _iotas(chunk)
        tril = col <= row
        gv = gv_ref[...]
        vg16, dvg_dv = _gelu(v_ref[...])
        vg = vg16.astype(F32)
        r = _rms_scale(vg)
        vh = vg * r
        vn_s[...] = (vh * gv).astype(BF16)
        ug, dug_du = _gelu(u_ref[...])
        sz, dsz = _silu(z_ref[...])
        dya_v = dya_ref[...]
        dmix_s[...] = dya_v * ug * sz
        du_scale = sz * dug_du
        dz_scale = ug * dsz
        for g in range(groups):
            wm = jnp.where(tril, ws_ref[g], 0.0).astype(BF16)
            cs = slice(g * chunk, (g + 1) * chunk)
            for c in range(tr // chunk):
                rs = slice(c * chunk, (c + 1) * chunk)
                vn = vn_s[rs, cs]
                mixed = _dot(wm, vn) + b_ref[g]
                dmix16 = dmix_s[rs, cs]
                dws_ref[g] += _dot_nt(dmix16, vn)
                db_ref[g] += dmix16.astype(F32)
                dvn_s[rs, cs] = _dot_tn(wm, dmix16)
                t_u = dya_v[rs, cs] * mixed.astype(BF16)
                out_ref[rs, g * chunk:(g + 1) * chunk] = t_u * du_scale[rs, cs]
                out_ref[rs, 2 * d + g * chunk:2 * d + (g + 1) * chunk] = t_u * dz_scale[rs, cs]
        dvn = dvn_s[...]
        dgv_ref[...] += jnp.sum(dvn * vh, axis=0, keepdims=True)
        dvh = dvn * gv
        dvg = r * (dvh - vh * jnp.mean(dvh * vh, axis=-1, keepdims=True))
        out_ref[:, d:2 * d] = (dvg * dvg_dv.astype(F32)).astype(BF16)

        @pl.when(pl.program_id(0) == n // tr - 1)
        def _():
            for g in range(groups):
                dbias_ref[g:g + 1, :] = jnp.sum(db_ref[g].T, axis=0, keepdims=True)

    seg = lambda k: pl.BlockSpec((tr, d), lambda i: (i, k))
    return pl.pallas_call(
        body, name="branch_a_bwd", grid=(n // tr,),
        in_specs=[seg(0), seg(1), seg(2), seg(0),
                  pl.BlockSpec((1, d), lambda i: (0, 0)),
                  pl.BlockSpec((groups, chunk, chunk), lambda i: (0, 0, 0)),
                  pl.BlockSpec((groups, chunk, 1), lambda i: (0, 0, 0)),
                  pl.BlockSpec(memory_space=pl.ANY)],
        out_specs=[pl.BlockSpec((tr, 3 * d), lambda i: (i, 0)),
                   pl.BlockSpec((groups, chunk, chunk), lambda i: (0, 0, 0)),
                   pl.BlockSpec((groups, chunk), lambda i: (0, 0)),
                   pl.BlockSpec((1, d), lambda i: (0, 0))],
        out_shape=[SDS(dproj.shape, dproj.dtype), SDS((groups, chunk, chunk), F32),
                   SDS((groups, chunk), F32), SDS((1, d), F32)],
        input_output_aliases={7: 0},
        scratch_shapes=[pltpu.VMEM((tr, d), BF16), pltpu.VMEM((tr, d), BF16), pltpu.VMEM((tr, d), F32),
                        pltpu.VMEM((groups, chunk, chunk), F32)],
        compiler_params=_params(("arbitrary",)),
    )(proj, proj, proj, dya, norm_v, w_s, b_col, dproj)


def _dx(dproj, wg_in, x2d, dx2, norm_in):
    n, d = x2d.shape
    nsh = N_DEV // 2
    esh = wg_in.shape[1] // nsh
    tm = _tile(n, 1024)

    def body(dp_ref, w_ref, x_ref, dx2_ref, g_ref, gx_ref, dg_ref, acc):
        i, k = pl.program_id(0), pl.program_id(1)

        @pl.when(jnp.logical_and(i == 0, k == 0))
        def _():
            dg_ref[...] = jnp.zeros_like(dg_ref)

        @pl.when(k == 0)
        def _():
            acc[...] = jnp.zeros_like(acc)

        acc[...] += _dot_nt(dp_ref[...], w_ref[...])

        @pl.when(k == nsh - 1)
        def _():
            dh = acc[...]
            x = x_ref[...]
            r = _rms_scale(x)
            xh = x * r
            dg_ref[...] += jnp.sum(dh * xh, axis=0, keepdims=True)
            dxh = dh * g_ref[...]
            gx_ref[...] = dx2_ref[...] + r * (dxh - xh * jnp.mean(dxh * xh, axis=-1, keepdims=True))

    rows = pl.BlockSpec((tm, d), lambda i, k: (i, 0))
    vec = pl.BlockSpec((1, d), lambda i, k: (0, 0))
    return pl.pallas_call(
        body, name="dx", grid=(n // tm, nsh),
        in_specs=[pl.BlockSpec((tm, esh), lambda i, k: (i, k)),
                  pl.BlockSpec((d, esh), lambda i, k: (0, k)), rows, rows, vec],
        out_specs=[rows, vec],
        out_shape=[SDS((n, d), F32), SDS((1, d), F32)],
        scratch_shapes=[pltpu.VMEM((tm, d), F32)],
        compiler_params=_params(("arbitrary", "arbitrary")),
    )(dproj, wg_in, x2d, dx2, norm_in)


def _adamw_outputs(g_ref, d_ref, m_ref, v_ref, g, w, m, v):
    delta, m2, v2 = _adamw(w, g, m, v)
    g_ref[...] = g
    d_ref[...] = delta
    m_ref[...] = m2
    v_ref[...] = v2


def _reduce_adamw(slots, w, m, v, name, transposed=False):
    r, c = w.shape
    tr = _tile(r, 128)

    def body(s_ref, w_ref, m_ref, v_ref, g_out, d_out, m_out, v_out):
        g = s_ref[0].astype(F32)
        for k in range(1, N_DEV):
            g = g + s_ref[k].astype(F32)
        if transposed:
            g = g.T
        _adamw_outputs(g_out, d_out, m_out, v_out, g, w_ref[...], m_ref[...], v_ref[...])

    blk = pl.BlockSpec((tr, c), lambda i: (i, 0))
    slot_blk = (pl.BlockSpec((N_DEV, c, tr), lambda i: (0, 0, i)) if transposed
                else pl.BlockSpec((N_DEV, tr, c), lambda i: (0, i, 0)))
    return pl.pallas_call(
        body, name=name, grid=(r // tr,),
        in_specs=[slot_blk, blk, blk, blk],
        out_specs=[blk] * 4,
        out_shape=[SDS((r, c), F32)] * 4,
        compiler_params=_params(("parallel",)),
    )(slots, w, m, v)


def _adamw_small(g, w, m, v, name):
    def body(g_ref, w_ref, m_ref, v_ref, g_out, d_out, m_out, v_out):
        _adamw_outputs(g_out, d_out, m_out, v_out, g_ref[...], w_ref[...], m_ref[...], v_ref[...])

    return pl.pallas_call(
        body, name=name,
        out_shape=[SDS(g.shape, F32)] * 4,
        in_specs=[pl.BlockSpec(memory_space=pltpu.VMEM)] * 4,
        out_specs=[pl.BlockSpec(memory_space=pltpu.VMEM)] * 4,
    )(g, w, m, v)


def kernel(x, norm_in, w_in, norm_v, w_s, b_s, w_o_gmlp, w_o_sb, w_out, norm_final, loss_target, m_norm_in, m_w_in, m_norm_v, m_w_s, m_b_s, m_w_o_gmlp, m_w_o_sb, m_w_out, m_norm_final, v_norm_in, v_w_in, v_norm_v, v_w_s, v_b_s, v_w_o_gmlp, v_w_o_sb, v_w_out, v_norm_final):
    batch, seq, d = x.shape
    n = batch * seq
    groups, chunk = w_s.shape[1], w_s.shape[2]
    hd = LANE
    x2d = x.reshape(n, d)
    tgt = loss_target.reshape(n, d)
    b_col = b_s[0].reshape(groups, chunk, 1)
    norm_final2 = norm_final.reshape(1, d)

    my_slot = _slot(_me()).astype(jnp.int32).reshape(1)
    proj, h, wg_in, wg_oa, wg_ob, wg_out = _gather_in_proj(
        x2d, norm_in, w_in[0], [w_o_gmlp[0], w_o_sb[0], w_out[0]], my_slot)
    rsh = wg_oa.shape[1]
    wf_oa, wf_ob, wf_out = (w.reshape(N_DEV * rsh, d) for w in (wg_oa, wg_ob, wg_out))
    ya = _branch_a_fwd(proj, norm_v, w_s[0], b_col)
    yb, o, sb_tot = _sb_fwd(proj, batch, seq, d, hd)
    dproj, dx2, dya, dyb, merged, dpa, dpb, loss_vec, dgf = _tail(
        x2d, tgt, ya, yb, proj, wf_oa, wf_ob, wf_out, norm_final2)
    gp_wo = _dw_o([(ya, dpa), (yb, dpb), (merged, dx2)])
    dproj, gp_ws, gp_b, gp_nv = _branch_a_bwd(proj, dya, norm_v, w_s[0], b_col, dproj)

    slab = lambda a: a.reshape(d // LANE, LANE)
    gc = groups * chunk
    packed = jnp.concatenate([gp_ws.reshape(gc, chunk), gp_b, slab(gp_nv), slab(dgf), slab(loss_vec)], axis=0)
    dproj, s_oa, s_ob, s_out, packs = _sb_bwd(
        proj, o, dyb, sb_tot, dproj, gp_wo.reshape(3, N_DEV, rsh, d), packed, batch, seq, d, hd)
    grad_x, gp_nin = _dx(dproj, wg_in, x2d, dx2, norm_in)
    s_win, late_packs = _dw_in_exchange(h, dproj, my_slot, slab(gp_nin))
    tot, loss_slab = _finish_small(packs, late_packs, groups, chunk)
    ns = d // LANE
    g_ws = tot[:gc]
    g_b = tot[gc:gc + groups]
    g_nv, g_nf, _, g_nin = (tot[gc + groups + k * ns:gc + groups + (k + 1) * ns] for k in range(4))
    loss = loss_slab[0, 0]

    res = {}
    res["w_in"] = _reduce_adamw(s_win, w_in[0], m_w_in[0], v_w_in[0], "adamw_w_in", transposed=True)
    res["w_o_gmlp"] = _reduce_adamw(s_oa, w_o_gmlp[0], m_w_o_gmlp[0], v_w_o_gmlp[0], "adamw_w_o_gmlp")
    res["w_o_sb"] = _reduce_adamw(s_ob, w_o_sb[0], m_w_o_sb[0], v_w_o_sb[0], "adamw_w_o_sb")
    res["w_out"] = _reduce_adamw(s_out, w_out[0], m_w_out[0], v_w_out[0], "adamw_w_out")
    res["norm_in"] = _adamw_small(g_nin, slab(norm_in), slab(m_norm_in), slab(v_norm_in), "adamw_norm_in")
    res["norm_v"] = _adamw_small(g_nv, slab(norm_v), slab(m_norm_v), slab(v_norm_v), "adamw_norm_v")
    res["norm_final"] = _adamw_small(g_nf, slab(norm_final), slab(m_norm_final), slab(v_norm_final), "adamw_norm_final")
    res["w_s"] = _adamw_small(g_ws, w_s.reshape(gc, chunk), m_w_s.reshape(gc, chunk), v_w_s.reshape(gc, chunk), "adamw_w_s")
    res["b_s"] = _adamw_small(g_b, b_s[0], m_b_s[0], v_b_s[0], "adamw_b_s")

    shapes = {"norm_in": norm_in.shape, "w_in": w_in.shape, "norm_v": norm_v.shape, "w_s": w_s.shape,
              "b_s": b_s.shape, "w_o_gmlp": w_o_gmlp.shape, "w_o_sb": w_o_sb.shape, "w_out": w_out.shape,
              "norm_final": norm_final.shape}
    names = list(shapes)
    outs = [loss, grad_x.reshape(batch, seq, d)]
    for kind in range(4):
        outs += [res[name][kind].reshape(shapes[name]) for name in names]
    return tuple(outs)
```

```python
import functools
import math

import jax
import jax.numpy as jnp
from jax import lax
from jax.experimental import pallas as pl
from jax.experimental.pallas import tpu as pltpu

F32 = jnp.float32
BF16 = jnp.bfloat16
SDS = jax.ShapeDtypeStruct
MESH_ID = pl.DeviceIdType.MESH

N_DEV = 8
LANE = 128
SUBLANE = 8
VMEM_LIMIT = 56 * 1024 * 1024
SB_TILE = 512
SB_TILE_BWD = 512
SB_SCAN = 256
SB_HEADS = 2
MASKED_LOG = -1e30
RMS_EPS = 1e-6

ADAM_LR = 0.001
ADAM_B1 = 0.9
ADAM_B2 = 0.999
ADAM_EPS = 1e-08
ADAM_WD = 0.01
ADAM_STEP = 10

NT_DIMS = (((1,), (1,)), ((), ()))
TN_DIMS = (((0,), (0,)), ((), ()))


def _params(semantics=None):
    return pltpu.CompilerParams(dimension_semantics=semantics, vmem_limit_bytes=VMEM_LIMIT)


def _tile(n, preferred):
    t = min(n, preferred)
    assert n % t == 0, (n, t)
    return t


def _sigmoid(x):
    return 1.0 / (1.0 + jnp.exp(-x))


def _silu(x):
    s = _sigmoid(x)
    return x * s, s * (1.0 + x * (1.0 - s))


def _gelu(x):
    k = math.sqrt(2.0 / math.pi)
    x2 = x * x
    t = jnp.tanh(k * (x + 0.044715 * (x * x2)))
    cdf = 0.5 * (1.0 + t)
    return x * cdf, cdf + 0.5 * x * (1.0 - t * t) * (k * (1.0 + 3.0 * 0.044715 * x2))


def _rms_scale(x):
    return lax.rsqrt(jnp.mean(x * x, axis=-1, keepdims=True) + RMS_EPS)


def _iotas(n):
    return (lax.broadcasted_iota(jnp.int32, (n, n), 0), lax.broadcasted_iota(jnp.int32, (n, n), 1))


def _adamw(w, g, m, v):
    m = ADAM_B1 * m + (1.0 - ADAM_B1) * g
    v = ADAM_B2 * v + (1.0 - ADAM_B2) * (g * g)
    m_hat = m / (1.0 - ADAM_B1 ** ADAM_STEP)
    v_hat = v / (1.0 - ADAM_B2 ** ADAM_STEP)
    delta = -ADAM_LR * (m_hat / (jnp.sqrt(v_hat) + ADAM_EPS) + ADAM_WD * w)
    return delta, m, v


def _dot(a, b):
    return jnp.dot(a, b, preferred_element_type=F32)


def _dot_nt(a, b):
    return lax.dot_general(a, b, NT_DIMS, preferred_element_type=F32)


def _dot_tn(a, b):
    return lax.dot_general(a, b, TN_DIMS, preferred_element_type=F32)


def _sb_logs(raw, scale, valid):
    z = (raw * scale).astype(BF16)
    log_beta = jnp.minimum(z, 0) - jnp.log(1 + jnp.exp(-jnp.abs(z)))
    log_rest = log_beta - z
    if valid is not None:
        log_beta = jnp.where(valid, log_beta, MASKED_LOG)
        log_rest = jnp.where(valid, log_rest, 0)
    return log_beta, log_rest


def _me():
    return lax.axis_index("x"), lax.axis_index("y"), lax.axis_index("c")


def _slot(p):
    return 4 * p[0] + 2 * p[1] + p[2]


def _peer(me, k):
    flips = ((k >> 2) & 1, (k >> 1) & 1, k & 1)
    return tuple(1 - a if f else a for a, f in zip(me, flips))


def _stack_exchange(me, st_in, st_out, n_whole, send_sems, recv_sems, local_sems, arrivals=True):
    mine = _slot(me)
    ns = len(st_in)
    part = lambda a, dev: st_in[a] if a >= ns - n_whole else st_in[a].at[_slot(dev)]
    local = [pltpu.make_async_copy(part(a, me), st_out[a].at[mine], local_sems.at[a]) for a in range(ns)]
    remote, landed = [], []
    for k in range(1, N_DEV):
        peer = _peer(me, k)
        for a in range(ns):
            sems = dict(send_sem=send_sems.at[7 * a + k - 1], recv_sem=recv_sems.at[7 * a + k - 1])
            remote.append(pltpu.make_async_remote_copy(
                src_ref=part(a, peer), dst_ref=st_out[a].at[mine],
                device_id=peer, device_id_type=MESH_ID, **sems))
            if arrivals:
                got = st_out[a].at[_slot(peer)]
                landed.append(pltpu.make_async_remote_copy(
                    src_ref=got, dst_ref=got, device_id=me, device_id_type=MESH_ID, **sems))
    return local, remote, landed


def _gather_in_proj(x2d, norm_in, w_in_sh, wo_shards, my_slot):
    n, d = x2d.shape
    esh = w_in_sh.shape[1]
    pw = 2 * esh
    n_chip = N_DEV // 2
    tm = _tile(n, 1024)
    n_i = n // tm
    mid = n_i // 2
    no = len(wo_shards)
    flip_at = lambda st: jnp.where(st == 1, 2, jnp.where(st == 2, 1, jnp.where(st == 3, 3, 0)))

    def body(me_ref, x_ref, g_ref, win_ref, *refs):
        del me_ref
        wo_in = refs[:no]
        proj_ref, h_ref, wg_ref = refs[no:no + 3]
        wo_out = refs[no + 3:2 * no + 3]
        wv, stage = refs[2 * no + 3:2 * no + 5]
        wo_stage = refs[2 * no + 5:3 * no + 5]
        send_sems, recv_sems, pair_sems, own_sems, wo_send, wo_recv, wo_local = refs[3 * no + 5:]
        st, i = pl.program_id(0), pl.program_id(1)
        x, y, c = _me()
        me, sibling = (x, y, c), (x, y, 1 - c)
        chips = [(1 - x, y), (x, 1 - y), (1 - x, 1 - y)]
        chip_id = lambda p: 2 * p[0] + p[1]

        def window(chip, core):
            return wv.at[chip_id(chip), :, pl.ds(pl.multiple_of(core * esh, LANE), esh)]

        def copy(k, block, to, src=None):
            dst = window(block[:2], block[2])
            return pltpu.make_async_remote_copy(
                src_ref=dst if src is None else src, dst_ref=dst,
                send_sem=send_sems.at[k], recv_sem=recv_sems.at[k], device_id=to, device_id_type=MESH_ID)

        def wo_copy(a, k, block, to, src=None):
            dst = wo_out[a].at[_slot(block)]
            return pltpu.make_async_remote_copy(
                src_ref=dst if src is None else src, dst_ref=dst,
                send_sem=wo_send.at[7 * a + k], recv_sem=wo_recv.at[7 * a + k], device_id=to, device_id_type=MESH_ID)

        def own_copy():
            return pltpu.make_async_copy(stage, window((x, y), c), own_sems.at[0])

        def wo_own_copy(a):
            return pltpu.make_async_copy(wo_stage[a], wo_out[a].at[_slot(me)], wo_local.at[a])

        def pair_copy(step):
            chip = jnp.bitwise_xor(chip_id((x, y)), flip_at(step))
            return pltpu.make_async_copy(wv.at[chip], wg_ref.at[:, pl.ds(pl.multiple_of(chip * pw, LANE), pw)],
                                         pair_sems.at[step])

        first = jnp.logical_and(st == 0, i == 0)

        @pl.when(first)
        def _():
            stage[...] = win_ref[...].astype(BF16)
            own_copy().start()
            copy(0, me, sibling, src=stage).start()
            for j in range(2):
                copy(1 + j, me, (*chips[j], c), src=stage).start()
            own_copy().wait()
            copy(0, sibling, me).wait_recv()
            pair_copy(0).start()

        for s_ in range(n_chip - 1):
            @pl.when(jnp.logical_and(st == s_, i == mid))
            def _():
                copy(1 + s_, (*chips[s_], c), me).wait_recv()
                copy(4 + s_, (*chips[s_], c), sibling).start()
                if s_ == 0:
                    copy(3, me, (*chips[2], c), src=stage).start()
                if s_ == 1:
                    for a in range(no):
                        wo_stage[a][...] = wo_in[a][...].astype(BF16)
                        wo_own_copy(a).start()
                        wo_copy(a, 0, me, sibling, src=wo_stage[a]).start()
                        for j, chip in enumerate(chips):
                            wo_copy(a, 1 + j, me, (*chip, c), src=wo_stage[a]).start()
                if s_ == 2:
                    for a in range(no):
                        for j, chip in enumerate(chips):
                            wo_copy(a, 1 + j, (*chip, c), me).wait_recv()
                            wo_copy(a, 4 + j, (*chip, c), sibling).start()

        for s_ in range(1, n_chip):
            @pl.when(jnp.logical_and(st == s_, i == 0))
            def _():
                copy(3 + s_, (*chips[s_ - 1], 1 - c), me).wait_recv()
                pair_copy(s_).start()

        xv = x_ref[...]
        h = (xv * _rms_scale(xv) * g_ref[...]).astype(BF16)

        @pl.when(st == 0)
        def _():
            h_ref[...] = h

        chip_now = jnp.bitwise_xor(chip_id((x, y)), flip_at(st))
        proj_ref[...] = _dot(h, wv[chip_now]).astype(BF16)

        @pl.when(jnp.logical_and(st == n_chip - 1, i == n_i - 1))
        def _():
            copy(0, me, sibling, src=stage).wait_send()
            for j, chip in enumerate(chips):
                copy(1 + j, me, (*chip, c), src=stage).wait_send()
                copy(4 + j, (*chip, c), sibling).wait_send()
            for s_ in range(n_chip):
                pair_copy(s_).wait()
            for a in range(no):
                wo_copy(a, 0, me, sibling, src=wo_stage[a]).wait_send()
                wo_copy(a, 0, sibling, me).wait_recv()
                for j, chip in enumerate(chips):
                    wo_copy(a, 1 + j, me, (*chip, c), src=wo_stage[a]).wait_send()
                    wo_copy(a, 4 + j, (*chip, c), sibling).wait_send()
                    wo_copy(a, 4 + j, (*chip, 1 - c), me).wait_recv()
                wo_own_copy(a).wait()

    any_spec = pl.BlockSpec(memory_space=pl.ANY)
    vmem = pl.BlockSpec(memory_space=pltpu.VMEM)
    grid_spec = pltpu.PrefetchScalarGridSpec(
        num_scalar_prefetch=1, grid=(n_chip, n_i),
        in_specs=[pl.BlockSpec((tm, d), lambda st, i, me: (i, 0)),
                  pl.BlockSpec((1, d), lambda st, i, me: (0, 0)), vmem] + [vmem] * no,
        out_specs=[pl.BlockSpec((tm, pw), lambda st, i, me: (i, jnp.bitwise_xor(me[0] // 2, flip_at(st)))),
                   pl.BlockSpec((tm, d), lambda st, i, me: (jnp.where(st == 0, i, n_i - 1), 0)),
                   any_spec] + [any_spec] * no,
        scratch_shapes=[pltpu.VMEM((n_chip, d, pw), BF16), pltpu.VMEM((d, esh), BF16)] + [
            pltpu.VMEM(s.shape, BF16) for s in wo_shards] + [
            pltpu.SemaphoreType.DMA((7,)), pltpu.SemaphoreType.DMA((7,)),
            pltpu.SemaphoreType.DMA((n_chip,)), pltpu.SemaphoreType.DMA((1,)),
            pltpu.SemaphoreType.DMA((7 * no,)), pltpu.SemaphoreType.DMA((7 * no,)),
            pltpu.SemaphoreType.DMA((no,))])
    return pl.pallas_call(
        body, name="gather_in_proj", grid_spec=grid_spec,
        out_shape=[SDS((n, n_chip * pw), BF16), SDS((n, d), BF16), SDS((d, n_chip * pw), BF16)] + [
            SDS((N_DEV,) + s.shape, BF16) for s in wo_shards],
        compiler_params=pltpu.CompilerParams(dimension_semantics=("arbitrary", "arbitrary"),
                                             vmem_limit_bytes=VMEM_LIMIT),
    )(my_slot, x2d, norm_in, w_in_sh, *wo_shards)


EXCHANGE_ORDER = ((6, 7, 4, 2, 5, 3, 1, 0), (7, 6, 2, 4, 3, 5, 1, 0))


def _owner_at(mine, j):
    k = 0
    for step in range(N_DEV - 1):
        k = jnp.where(j == step, jnp.where(mine % 2 == 0, EXCHANGE_ORDER[0][step], EXCHANGE_ORDER[1][step]), k)
    return jnp.bitwise_xor(mine, k)


def _dw_in_exchange(h, dproj, my_slot, packed):
    n, d = h.shape
    esh = dproj.shape[1] // N_DEV
    tk = _tile(n, 2048)
    nk = n // tk
    last_j = N_DEV - 1
    depth = 4

    def body(me_ref, h_ref, dp_ref, pk_in, win_out, pk_out,
             acc, sendbuf, win_send, win_recv, send_sems, recv_sems, local_sems):
        del me_ref
        j, k = pl.program_id(0), pl.program_id(1)
        me = _me()
        mine = _slot(me)

        def pack_copies():
            local = pltpu.make_async_copy(pk_in, pk_out.at[mine], local_sems.at[0])
            remote = [pltpu.make_async_remote_copy(
                src_ref=pk_in, dst_ref=pk_out.at[mine], send_sem=send_sems.at[kk - 1], recv_sem=recv_sems.at[kk - 1],
                device_id=_peer(me, kk), device_id_type=MESH_ID) for kk in range(1, N_DEV)]
            return local, remote

        def shard_copy(jj):
            owner = _owner_at(mine, jj)
            return pltpu.make_async_remote_copy(
                src_ref=sendbuf.at[jj % depth], dst_ref=win_out.at[mine],
                send_sem=win_send.at[jj % depth], recv_sem=win_recv.at[mine],
                device_id=(owner // 4, (owner // 2) % 2, owner % 2), device_id_type=MESH_ID)

        def own_copy():
            return pltpu.make_async_copy(sendbuf.at[last_j % depth], win_out.at[mine], local_sems.at[1])

        @pl.when(jnp.logical_and(j == 0, k == 0))
        def _():
            local, remote = pack_copies()
            for cp in [local] + remote:
                cp.start()

        @pl.when(k == 0)
        def _():
            acc[...] = jnp.zeros_like(acc)

        acc[...] += _dot_tn(dp_ref[...], h_ref[...])

        @pl.when(k == nk - 1)
        def _():
            @pl.when(j >= depth)
            def _():
                shard_copy(j - depth).wait_send()

            sendbuf[j % depth] = acc[...].astype(BF16)

            @pl.when(j < last_j)
            def _():
                shard_copy(j).start()

            @pl.when(j == last_j)
            def _():
                own_copy().start()
                for jj in range(last_j - depth + 1, last_j):
                    shard_copy(jj).wait_send()
                own_copy().wait()
                for src in range(N_DEV):
                    @pl.when(src != mine)
                    def _():
                        landed = win_out.at[src]
                        pltpu.make_async_remote_copy(
                            src_ref=landed, dst_ref=landed, send_sem=win_send.at[0], recv_sem=win_recv.at[src],
                            device_id=me, device_id_type=MESH_ID).wait_recv()
                local, remote = pack_copies()
                for cp in remote:
                    cp.wait_send()
                for kk in range(1, N_DEV):
                    landed = pk_out.at[_slot(_peer(me, kk))]
                    pltpu.make_async_remote_copy(
                        src_ref=landed, dst_ref=landed, send_sem=send_sems.at[kk - 1], recv_sem=recv_sems.at[kk - 1],
                        device_id=me, device_id_type=MESH_ID).wait_recv()
                local.wait()

    any_spec = pl.BlockSpec(memory_space=pl.ANY)
    grid_spec = pltpu.PrefetchScalarGridSpec(
        num_scalar_prefetch=1, grid=(N_DEV, nk),
        in_specs=[pl.BlockSpec((tk, d), lambda j, k, me: (k, 0)),
                  pl.BlockSpec((tk, esh), lambda j, k, me: (k, _owner_at(me[0], j))), any_spec],
        out_specs=[any_spec] * 2,
        scratch_shapes=[pltpu.VMEM((esh, d), F32), pltpu.VMEM((depth, esh, d), BF16),
                        pltpu.SemaphoreType.DMA((depth,)), pltpu.SemaphoreType.DMA((N_DEV,)),
                        pltpu.SemaphoreType.DMA((N_DEV - 1,)), pltpu.SemaphoreType.DMA((N_DEV - 1,)),
                        pltpu.SemaphoreType.DMA((2,))])
    return pl.pallas_call(
        body, name="dw_in_exchange", grid_spec=grid_spec,
        out_shape=[SDS((N_DEV, esh, d), BF16), SDS((N_DEV,) + packed.shape, packed.dtype)],
        compiler_params=_params(("arbitrary", "arbitrary")),
    )(my_slot, h, dproj, packed)


def _finish_small(packs, late_packs, groups, chunk):
    rows = packs.shape[1]
    late = late_packs.shape[1]
    gc = groups * chunk

    def body(p_ref, l_ref, sum_ref, loss_ref):
        row, col = _iotas(chunk)
        tril = col <= row
        for g in range(groups):
            rs = slice(g * chunk, (g + 1) * chunk)
            tot = p_ref[0, rs, :]
            for dev in range(1, N_DEV):
                tot = tot + p_ref[dev, rs, :]
            sum_ref[rs, :] = jnp.where(tril, tot, 0.0)
        rs = slice(gc, rows)
        tot = p_ref[0, rs, :]
        for dev in range(1, N_DEV):
            tot = tot + p_ref[dev, rs, :]
        sum_ref[rs, :] = tot
        loss_ref[...] = jnp.full((SUBLANE, LANE), jnp.sum(tot[rows - gc - SUBLANE:, :]), F32)
        tot = l_ref[0]
        for dev in range(1, N_DEV):
            tot = tot + l_ref[dev]
        sum_ref[rows:rows + late, :] = tot

    return pl.pallas_call(
        body, name="finish_small",
        out_shape=[SDS((rows + late, LANE), F32), SDS((SUBLANE, LANE), F32)],
        in_specs=[pl.BlockSpec(memory_space=pltpu.VMEM)] * 2,
        out_specs=[pl.BlockSpec(memory_space=pltpu.VMEM)] * 2,
        compiler_params=pltpu.CompilerParams(vmem_limit_bytes=VMEM_LIMIT),
    )(packs, late_packs)


def _branch_a_fwd(proj, norm_v, w_s, b_col):
    n = proj.shape[0]
    d = norm_v.shape[1]
    groups, chunk, _ = w_s.shape
    tr = _tile(n, 4 * chunk)

    def body(u_ref, v_ref, z_ref, gv_ref, ws_ref, b_ref, ya_ref, vn_s, pre_s):
        row, col = _iotas(chunk)
        tril = col <= row
        vg = _gelu(v_ref[...])[0].astype(F32)
        vn_s[...] = (vg * _rms_scale(vg) * gv_ref[...]).astype(BF16)
        pre_s[...] = _gelu(u_ref[...])[0] * _silu(z_ref[...])[0]
        for g in range(groups):
            wm = jnp.where(tril, ws_ref[g], 0.0).astype(BF16)
            cs = slice(g * chunk, (g + 1) * chunk)
            for c in range(tr // chunk):
                rs = slice(c * chunk, (c + 1) * chunk)
                mixed = _dot(wm, vn_s[rs, cs]) + b_ref[g]
                ya_ref[rs, cs] = (pre_s[rs, cs].astype(F32) * mixed).astype(BF16)

    seg = lambda k: pl.BlockSpec((tr, d), lambda i: (i, k))
    return pl.pallas_call(
        body, name="branch_a_fwd", grid=(n // tr,),
        in_specs=[seg(0), seg(1), seg(2),
                  pl.BlockSpec((1, d), lambda i: (0, 0)),
                  pl.BlockSpec((groups, chunk, chunk), lambda i: (0, 0, 0)),
                  pl.BlockSpec((groups, chunk, 1), lambda i: (0, 0, 0))],
        out_specs=pl.BlockSpec((tr, d), lambda i: (i, 0)),
        out_shape=SDS((n, d), BF16),
        scratch_shapes=[pltpu.VMEM((tr, d), BF16), pltpu.VMEM((tr, d), BF16)],
        compiler_params=_params(("parallel",)),
    )(proj, proj, proj, norm_v, w_s, b_col)


def _sb_fwd(proj, batch, seq, d, hd):
    heads = d // hd
    t = _tile(seq, SB_TILE)
    sw = _tile(t, SB_SCAN)
    nb = t // sw
    scale = hd ** -0.5
    nblk = seq // t
    nh = SB_HEADS
    wide = nh * hd
    cols = [slice(hh * hd, (hh + 1) * hd) for hh in range(nh)]

    def body(qs, k_ref, vs, zb_ref, yb_ref, o_ref, tot_ref, kts, later, acc):
        for jb in range(nblk):
            kts[jb] = k_ref[jb * t:(jb + 1) * t, :].T
        row, col = _iotas(t)
        later[...] = (row[:sw, :sw] > col[:sw, :sw]).astype(BF16)

        def qblock(i, carry):
            r0 = pl.multiple_of(i * t, t)

            def tile(j, runs):
                c0 = pl.multiple_of(j * t, t)
                logs = [_sb_logs(_dot(qs[pl.ds(r0, t), cs], kts[j, cs, :]), scale, None) for cs in cols]
                scans = [_dot(jnp.concatenate([logs[hh][1][:, b * sw:(b + 1) * sw] for b in range(nb)], axis=0),
                              later[...]) for hh in range(nh)]
                new_runs = []
                for hh in range(nh):
                    after = runs[hh]
                    blocks = [None] * nb
                    for b in reversed(range(nb)):
                        ks_ = slice(b * sw, (b + 1) * sw)
                        inside = scans[hh][b * t:(b + 1) * t]
                        blocks[b] = jnp.exp(logs[hh][0][:, ks_].astype(F32) + inside + after).astype(BF16)
                        after = after + inside[:, 0:1] + logs[hh][1][:, b * sw:b * sw + 1].astype(F32)
                    new_runs.append(after)
                    acc[:, cols[hh]] += _dot(jnp.concatenate(blocks, axis=1), vs[pl.ds(c0, t), cols[hh]])
                return tuple(new_runs)

            def diagonal_tile():
                starts = [b * sw for b in range(nb)]
                logs = [[_sb_logs(_dot(qs[pl.ds(r0 + s, t - s), cs], kts[i, cs, s:s + sw]), scale,
                                  col[:t - s, :sw] < row[:t - s, :sw]) for s in starts] for cs in cols]
                scans = [_dot(jnp.concatenate([lr for _, lr in logs[hh]], axis=0), later[...]) for hh in range(nh)]
                new_runs = []
                offs = [sum(t - s for s in starts[:b]) for b in range(nb)]
                for hh in range(nh):
                    after = jnp.zeros((t, 1), F32)
                    ws = [None] * nb
                    for b in reversed(range(nb)):
                        s = starts[b]
                        lb, lr = logs[hh][b]
                        inside = scans[hh][offs[b]:offs[b] + t - s]
                        ws[b] = jnp.exp(lb.astype(F32) + inside + after[s:]).astype(BF16)
                        total = inside[:, 0:1] + lr[:, 0:1].astype(F32)
                        after = after + total if s == 0 else jnp.concatenate([after[:s], after[s:] + total], axis=0)
                    new_runs.append(after)
                    acc[:, cols[hh]] = _dot(ws[0], vs[pl.ds(r0, sw), cols[hh]])
                    for b in range(1, nb):
                        acc[starts[b]:, cols[hh]] += _dot(ws[b], vs[pl.ds(r0 + starts[b], sw), cols[hh]])
                return tuple(new_runs)

            runs = diagonal_tile()
            runs = lax.fori_loop(0, i, lambda jj, rs: tile(i - 1 - jj, rs), runs)
            for hh in range(nh):
                out = acc[:, cols[hh]]
                o_ref[pl.ds(r0, t), cols[hh]] = out.astype(BF16)
                tot_ref[hh, pl.ds(r0, t), :] = runs[hh]
                sz, _ = _silu(zb_ref[pl.ds(r0, t), cols[hh]].astype(F32))
                yb_ref[pl.ds(r0, t), cols[hh]] = (out * sz).astype(BF16)
            return carry

        lax.fori_loop(0, nblk, qblock, 0)

    col0 = d // wide
    seg = lambda k: pl.BlockSpec((seq, wide), lambda b, h: (b, k * col0 + h))
    return pl.pallas_call(
        body, name="sb_fwd", grid=(batch, heads // nh),
        in_specs=[seg(3), seg(4), seg(5), seg(6)],
        out_specs=[pl.BlockSpec((seq, wide), lambda b, h: (b, h))] * 2 + [
            pl.BlockSpec((nh, seq, 1), lambda b, h: (b * (heads // nh) + h, 0, 0))],
        out_shape=[SDS((batch * seq, d), BF16), SDS((batch * seq, d), BF16), SDS((batch * heads, seq, 1), F32)],
        scratch_shapes=[pltpu.VMEM((nblk, wide, t), BF16), pltpu.VMEM((sw, sw), BF16), pltpu.VMEM((t, wide), F32)],
        compiler_params=_params(("parallel", "parallel")),
    )(proj, proj, proj, proj)


def _tail(x2d, tgt, ya, yb, proj, w_oa, w_ob, w_out, norm_final):
    n, d = x2d.shape
    e = proj.shape[1]
    tm = _tile(n, 512)
    steps = n // tm

    def body(x_ref, t_ref, ya_ref, yb_ref, ga_ref, gb_ref, woa_ref, wob_ref, wout_ref, gf_ref,
             dproj_ref, dx2_ref, dya_ref, dyb_ref, mrg_ref, dpa_ref, dpb_ref, loss_ref, dgf_ref, dg_s, dg_sems):
        i = pl.program_id(0)

        def gate_copy(step):
            rows_ = pl.ds(pl.multiple_of(step * tm, tm), tm)
            return pltpu.make_async_copy(dg_s.at[step % 2], dproj_ref.at[rows_, pl.ds(7 * d, 2 * d)],
                                         dg_sems.at[step % 2])

        @pl.when(i == 0)
        def _():
            loss_ref[...] = jnp.zeros_like(loss_ref)
            dgf_ref[...] = jnp.zeros_like(dgf_ref)

        @pl.when(i >= 2)
        def _():
            gate_copy(i - 2).wait()

        pa = _dot(ya_ref[...], woa_ref[...])
        pb = _dot(yb_ref[...], wob_ref[...])
        sa = _sigmoid(ga_ref[...].astype(F32))
        sb = _sigmoid(gb_ref[...].astype(F32))
        merged = (sa * pa + sb * pb).astype(BF16)
        mrg_ref[...] = merged
        x2 =x_ref[...] + _dot(merged, wout_ref[...])
        r2 = _rms_scale(x2)
        xh = x2 * r2
        gf = gf_ref[...]
        diff = xh * gf - t_ref[...]
        loss_ref[...] += jnp.sum(diff * diff, axis=0, keepdims=True) * (0.5 / d)
        dy = diff * (1.0 / d)
        dgf_ref[...] += jnp.sum(dy * xh, axis=0, keepdims=True)
        dxh = dy * gf
        dx2 = r2 * (dxh - xh * jnp.mean(dxh * xh, axis=-1, keepdims=True))
        dx2_ref[...] = dx2
        dm = _dot_nt(dx2.astype(BF16), wout_ref[...])
        dpa = (dm * sa).astype(BF16)
        dpb = (dm * sb).astype(BF16)
        dpa_ref[...] = dpa
        dpb_ref[...] = dpb
        dg_s[i % 2, :, 0:d] = (dm * pa * (sa * (1.0 - sa))).astype(BF16)
        dg_s[i % 2, :, d:2 * d] = (dm * pb * (sb * (1.0 - sb))).astype(BF16)
        gate_copy(i).start()
        dya_ref[...] = _dot_nt(dpa, woa_ref[...]).astype(BF16)
        dyb_ref[...] = _dot_nt(dpb, wob_ref[...]).astype(BF16)

        @pl.when(i == steps - 1)
        def _():
            if steps >= 2:
                gate_copy(i - 1).wait()
            gate_copy(i).wait()

    rows = lambda k=0: pl.BlockSpec((tm, d), lambda i: (i, k))
    full = pl.BlockSpec((d, d), lambda i: (0, 0), pipeline_mode=pl.Buffered(1))
    vec = pl.BlockSpec((1, d), lambda i: (0, 0))
    return pl.pallas_call(
        body, name="tail", grid=(steps,),
        in_specs=[rows(), rows(), rows(), rows(), rows(7), rows(8), full, full, full, vec],
        out_specs=[pl.BlockSpec(memory_space=pl.ANY),
                   rows(), rows(), rows(), rows(), rows(), rows(), vec, vec],
        out_shape=[SDS((n, e), BF16), SDS((n, d), F32), SDS((n, d), BF16), SDS((n, d), BF16),
                   SDS((n, d), BF16), SDS((n, d), BF16), SDS((n, d), BF16),
                   SDS((1, d), F32), SDS((1, d), F32)],
        scratch_shapes=[pltpu.VMEM((2, tm, 2 * d), BF16), pltpu.SemaphoreType.DMA((2,))],
        compiler_params=_params(("arbitrary",)),
    )(x2d, tgt, ya, yb, proj, proj, w_oa, w_ob, w_out, norm_final)


def _dw_o(pairs):
    n, d = pairs[0][0].shape
    tk = _tile(n, 1024)
    nk = n // tk
    npair = len(pairs)

    def body(*refs):
        a_refs, b_refs = refs[:npair], refs[npair:2 * npair]
        o_ref, acc = refs[2 * npair], refs[2 * npair + 1]
        p, k = pl.program_id(0), pl.program_id(1)

        @pl.when(k == 0)
        def _():
            acc[...] = jnp.zeros_like(acc)

        for q in range(npair):
            @pl.when(p == q)
            def _():
                acc[...] += _dot_tn(a_refs[q][...], b_refs[q][...].astype(BF16))

        @pl.when(k == nk - 1)
        def _():
            o_ref[0] = acc[...].astype(BF16)

    def tiles(q):
        return pl.BlockSpec((tk, d), lambda p, k: (jnp.where(p == q, k, jnp.where(p < q, 0, nk - 1)), 0))

    return pl.pallas_call(
        body, name="dw_o", grid=(npair, nk),
        in_specs=[tiles(q) for q in range(npair)] * 2,
        out_specs=pl.BlockSpec((1, d, d), lambda p, k: (p, 0, 0)),
        out_shape=SDS((npair, d, d), BF16),
        scratch_shapes=[pltpu.VMEM((d, d), F32)],
        compiler_params=_params(("arbitrary", "arbitrary")),
    )(*[a for a, _ in pairs], *[b for _, b in pairs])


def _sb_bwd(proj, o, dyb, tot, dproj, dw_stack, packed, batch, seq, d, hd):
    heads = d // hd
    t = _tile(seq, SB_TILE_BWD)
    sw = _tile(t, SB_SCAN)
    nb = t // sw
    scale = hd ** -0.5
    nblk = seq // t
    nh = SB_HEADS
    wide = nh * hd
    hs = range(nh)
    cols = [slice(hh * hd, (hh + 1) * hd) for hh in hs]
    blocks = [slice(b * sw, (b + 1) * sw) for b in range(nb)]
    last = slice(sw - 1, sw)

    def compute(qs, ks, v_ref, zb_ref, o_ref, dyb_ref, tot_ref, kts, vts, dos, dzb, dq_all, dkv_t, qt_s, dot_s,
                upto, before, dq):
        for jb in range(nblk):
            rows = slice(jb * t, (jb + 1) * t)
            kts[jb] = ks[rows, :].T
            vts[jb] = v_ref[rows, :].T
        sz, dsz = _silu(zb_ref[...])
        dyb_v = dyb_ref[...]
        dos[...] = dyb_v * sz
        dzb[...] = dyb_v * o_ref[...] * dsz
        row, col = _iotas(t)
        upto[...] = (row[:sw, :sw] <= col[:sw, :sw]).astype(BF16)
        before[...] = (row[:sw, :sw] < col[:sw, :sw]).astype(BF16)

        def qblock(i, carry):
            r0 = pl.multiple_of(i * t, t)

            def tile(j, sums):
                c0 = pl.multiple_of(j * t, t)
                q_i = [qs[pl.ds(r0, t), cs] for cs in cols]
                do_i = [dos[pl.ds(r0, t), cs] for cs in cols]
                logs = [_sb_logs(_dot(q_i[hh], kts[j, cols[hh], :]), scale, None) for hh in hs]
                dw = [_dot(do_i[hh], vts[j, cols[hh], :]) for hh in hs]
                scans = [_dot(jnp.concatenate([logs[hh][1][:, ks_] for ks_ in blocks], axis=0), upto[...]) for hh in hs]
                ws, gs, new_runs = [], [], []
                for hh in hs:
                    left = tot_ref[hh, pl.ds(r0, t), :] - sums[hh][0]
                    w_b, g_b = [], []
                    for b, ks_ in enumerate(blocks):
                        inside = scans[hh][b * t:(b + 1) * t]
                        w = jnp.exp(logs[hh][0][:, ks_].astype(F32) + (left - inside))
                        w_b.append(w.astype(BF16))
                        g_b.append((dw[hh][:, ks_] * w).astype(BF16))
                        left = left - inside[:, last]
                    ws.append(jnp.concatenate(w_b, axis=1))
                    gs.append(g_b)
                    new_runs.append(tot_ref[hh, pl.ds(r0, t), :] - left)
                gscans = [_dot(jnp.concatenate(gs[hh], axis=0), before[...]) for hh in hs]
                dzs, new_gruns = [], []
                for hh in hs:
                    g_before = sums[hh][1]
                    dz_b = []
                    for b, ks_ in enumerate(blocks):
                        inside = gscans[hh][b * t:(b + 1) * t]
                        beta = jnp.exp(logs[hh][0][:, ks_]).astype(F32)
                        g = gs[hh][b].astype(F32)
                        dz_b.append(((g - (g + inside + g_before) * beta) * scale).astype(BF16))
                        g_before = g_before + inside[:, last] + g[:, last]
                    dzs.append(jnp.concatenate(dz_b, axis=1))
                    new_gruns.append(g_before)
                for hh in hs:
                    dkv_t[1, j, cols[hh], :] += _dot(dot_s[cols[hh], :], ws[hh])
                for hh in hs:
                    dkv_t[0, j, cols[hh], :] += _dot(qt_s[cols[hh], :], dzs[hh])
                for hh in hs:
                    dq[:, cols[hh]] += _dot(dzs[hh], ks[pl.ds(c0, t), cols[hh]])
                return tuple((new_runs[hh], new_gruns[hh]) for hh in hs)

            def diagonal_tile(sums):
                starts = [b * sw for b in range(nb)]
                offs = [sum(t - s for s in starts[:b]) for b in range(nb)]
                q_b = [[qs[pl.ds(r0 + s, t - s), cs] for s in starts] for cs in cols]
                do_b = [[dos[pl.ds(r0 + s, t - s), cs] for s in starts] for cs in cols]
                logs = [[_sb_logs(_dot(q_b[hh][b], kts[i, cols[hh], s:s + sw]), scale,
                                  col[:t - s, :sw] < row[:t - s, :sw]) for b, s in enumerate(starts)] for hh in hs]
                dw = [[_dot(do_b[hh][b], vts[i, cols[hh], s:s + sw]) for b, s in enumerate(starts)] for hh in hs]
                scans = [_dot(jnp.concatenate([lr for _, lr in logs[hh]], axis=0), upto[...]) for hh in hs]
                ws, gs = [], []
                for hh in hs:
                    left = tot_ref[hh, pl.ds(r0, t), :] - sums[hh][0]
                    w_b, g_b = [], []
                    for b, s in enumerate(starts):
                        inside = scans[hh][offs[b]:offs[b] + t - s]
                        w = jnp.exp(logs[hh][b][0].astype(F32) + (left[s:] - inside))
                        w_b.append(w.astype(BF16))
                        g_b.append((dw[hh][b] * w).astype(BF16))
                        total = inside[:, last]
                        left = left - total if s == 0 else jnp.concatenate([left[:s], left[s:] - total], axis=0)
                    ws.append(w_b)
                    gs.append(g_b)
                gscans = [_dot(jnp.concatenate(gs[hh], axis=0), before[...]) for hh in hs]
                dzs = []
                for hh in hs:
                    g_before = sums[hh][1]
                    dz_b = []
                    for b, s in enumerate(starts):
                        inside = gscans[hh][offs[b]:offs[b] + t - s]
                        beta = jnp.exp(logs[hh][b][0]).astype(F32)
                        g = gs[hh][b].astype(F32)
                        dz_b.append(((g - (g + inside + g_before[s:]) * beta) * scale).astype(BF16))
                        total = inside[:, last] + g[:, last]
                        g_before = g_before + total if s == 0 else jnp.concatenate(
                            [g_before[:s], g_before[s:] + total], axis=0)
                    dzs.append(dz_b)
                for hh in hs:
                    for b, s in enumerate(starts):
                        dkv_t[1, i, cols[hh], s:s + sw] = _dot(dot_s[cols[hh], s:], ws[hh][b])
                for hh in hs:
                    for b, s in enumerate(starts):
                        dkv_t[0, i, cols[hh], s:s + sw] = _dot(qt_s[cols[hh], s:], dzs[hh][b])
                for hh in hs:
                    for b, s in enumerate(starts):
                        dq[s:, cols[hh]] += _dot(dzs[hh][b], ks[pl.ds(r0 + s, sw), cols[hh]])

            qt_s[...] = qs[pl.ds(r0, t), :].T
            dot_s[...] = dos[pl.ds(r0, t), :].T
            zero = jnp.zeros((t, 1), F32)
            dq[...] = jnp.zeros_like(dq)
            sums = lax.fori_loop(0, i, tile, ((zero, zero),) * nh)
            diagonal_tile(sums)
            dq_all[pl.ds(r0, t), :] = dq[...]
            return carry

        lax.fori_loop(0, nblk, qblock, 0)

    pairs = heads // nh

    nst = dw_stack.shape[0]
    ns = nst + 1

    def body(qs, ks, v_ref, zb_ref, o_ref, dyb_ref, tot_ref, dproj_in, dw_ref, pk_ref, out_ref, *refs):
        del dproj_in
        st_in = [dw_ref.at[k] for k in range(nst)] + [pk_ref]
        st_out = refs[:ns]
        (kts, vts, dos, dzb, dq_all, dkv_t, qt_s, dot_s, upto, before, dq, stage, stage_sems,
         send_sems, recv_sems, local_sems) = refs[ns:]
        step = pl.program_id(0) * pairs + pl.program_id(1)
        exchange = functools.partial(_stack_exchange, _me(), st_in, st_out, 1, send_sems, recv_sems, local_sems)

        @pl.when(step == 0)
        def _():
            local, remote, _ = exchange(arrivals=False)
            for cp in local + remote:
                cp.start()

        def out_copies(s):
            rows_ = pl.ds(pl.multiple_of((s // pairs) * seq, seq), seq)
            return [pltpu.make_async_copy(
                stage.at[k], out_ref.at[rows_, pl.ds(pl.multiple_of((3 + k) * d + (s % pairs) * wide, wide), wide)],
                stage_sems.at[k]) for k in range(4)]

        compute(qs, ks, v_ref, zb_ref, o_ref, dyb_ref, tot_ref, kts, vts, dos, dzb, dq_all, dkv_t, qt_s, dot_s,
                upto, before, dq)

        @pl.when(step > 0)
        def _():
            for cp in out_copies(step - 1):
                cp.wait()

        stage[0] = dq_all[...].astype(BF16)
        for k in range(2):
            for jb in range(nblk):
                stage[1 + k, jb * t:(jb + 1) * t, :] = dkv_t[k, jb].astype(BF16).T
        stage[3] = dzb[...]
        for cp in out_copies(step):
            cp.start()

        @pl.when(step == batch * pairs - 1)
        def _():
            for cp in out_copies(step):
                cp.wait()
            local, remote, landed = exchange()
            for cp in remote:
                cp.wait_send()
            for cp in landed:
                cp.wait_recv()
            for cp in local:
                cp.wait()

    col0 = d // wide
    seg = lambda k: pl.BlockSpec((seq, wide), lambda b, h: (b, k * col0 + h))
    head = pl.BlockSpec((seq, wide), lambda b, h: (b, h))
    any_spec = pl.BlockSpec(memory_space=pl.ANY)
    return pl.pallas_call(
        body, name="sb_bwd", grid=(batch, pairs),
        in_specs=[seg(3), seg(4), seg(5), seg(6), head, head,
                  pl.BlockSpec((nh, seq, 1), lambda b, h: (b * pairs + h, 0, 0))] + [any_spec] * 3,
        out_specs=[any_spec] * (ns + 1),
        out_shape=[SDS(dproj.shape, dproj.dtype)] + [SDS(dw_stack.shape[1:], dw_stack.dtype)] * nst + [
            SDS((N_DEV,) + packed.shape, packed.dtype)],
        input_output_aliases={7: 0},
        scratch_shapes=[pltpu.VMEM((nblk, wide, t), BF16)] * 2 + [
            pltpu.VMEM((seq, wide), BF16), pltpu.VMEM((seq, wide), BF16),
            pltpu.VMEM((seq, wide), F32), pltpu.VMEM((2, nblk, wide, t), F32),
            pltpu.VMEM((wide, t), BF16), pltpu.VMEM((wide, t), BF16),
            pltpu.VMEM((sw, sw), BF16), pltpu.VMEM((sw, sw), BF16), pltpu.VMEM((t, wide), F32),
            pltpu.VMEM((4, seq, wide), BF16), pltpu.SemaphoreType.DMA((4,)),
            pltpu.SemaphoreType.DMA((7 * ns,)), pltpu.SemaphoreType.DMA((7 * ns,)),
            pltpu.SemaphoreType.DMA((ns,))],
        compiler_params=_params(("arbitrary", "arbitrary")),
    )(proj, proj, proj, proj, o, dyb, tot, dproj, dw_stack, packed)


def _branch_a_bwd(proj, dya, norm_v, w_s, b_col, dproj):
    n = proj.shape[0]
    d = norm_v.shape[1]
    groups, chunk, _ = w_s.shape
    tr = _tile(n, 2 * chunk)

    def body(u_ref, v_ref, z_ref, dya_ref, gv_ref, ws_ref, b_ref, dproj_in,
             out_ref, dws_ref, dbias_ref, dgv_ref, vn_s, dmix_s, dvn_s, db_ref):
        del dproj_in

        @pl.when(pl.program_id(0) == 0)
        def _():
            dws_ref[...] = jnp.zeros_like(dws_ref)
            db_ref[...] = jnp.zeros_like(db_ref)
            dgv_ref[...] = jnp.zeros_like(dgv_ref)

        row, col = _iotas(chunk)
        tril = col <= row
        gv = gv_ref[...]
        vg16, dvg_dv = _gelu(v_ref[...])
        vg = vg16.astype(F32)
        r = _rms_scale(vg)
        vh = vg * r
        vn_s[...] = (vh * gv).astype(BF16)
        ug, dug_du = _gelu(u_ref[...])
        sz, dsz = _silu(z_ref[...])
        dya_v = dya_ref[...]
        dmix_s[...] = dya_v * ug * sz
        du_scale = sz * dug_du
        dz_scale = ug * dsz
        for g in range(groups):
            wm = jnp.where(tril, ws_ref[g], 0.0).astype(BF16)
            cs = slice(g * chunk, (g + 1) * chunk)
            for c in range(tr // chunk):
                rs = slice(c * chunk, (c + 1) * chunk)
                vn = vn_s[rs, cs]
                mixed = _dot(wm, vn) + b_ref[g]
                dmix16 = dmix_s[rs, cs]
                dws_ref[g] += _dot_nt(dmix16, vn)
                db_ref[g] += dmix16.astype(F32)
                dvn_s[rs, cs] = _dot_tn(wm, dmix16)
                t_u = dya_v[rs, cs] * mixed.astype(BF16)
                out_ref[rs, g * chunk:(g + 1) * chunk] = t_u * du_scale[rs, cs]
                out_ref[rs, 2 * d + g * chunk:2 * d + (g + 1) * chunk] = t_u * dz_scale[rs, cs]
        dvn = dvn_s[...]
        dgv_ref[...] += jnp.sum(dvn * vh, axis=0, keepdims=True)
        dvh = dvn * gv
        dvg = r * (dvh - vh * jnp.mean(dvh * vh, axis=-1, keepdims=True))
        out_ref[:, d:2 * d] = (dvg * dvg_dv.astype(F32)).astype(BF16)

        @pl.when(pl.program_id(0) == n // tr - 1)
        def _():
            for g in range(groups):
                dbias_ref[g:g + 1, :] = jnp.sum(db_ref[g].T, axis=0, keepdims=True)

    seg = lambda k: pl.BlockSpec((tr, d), lambda i: (i, k))
    return pl.pallas_call(
        body, name="branch_a_bwd", grid=(n // tr,),
        in_specs=[seg(0), seg(1), seg(2), seg(0),
                  pl.BlockSpec((1, d), lambda i: (0, 0)),
                  pl.BlockSpec((groups, chunk, chunk), lambda i: (0, 0, 0)),
                  pl.BlockSpec((groups, chunk, 1), lambda i: (0, 0, 0)),
                  pl.BlockSpec(memory_space=pl.ANY)],
        out_specs=[pl.BlockSpec((tr, 3 * d), lambda i: (i, 0)),
                   pl.BlockSpec((groups, chunk, chunk), lambda i: (0, 0, 0)),
                   pl.BlockSpec((groups, chunk), lambda i: (0, 0)),
                   pl.BlockSpec((1, d), lambda i: (0, 0))],
        out_shape=[SDS(dproj.shape, dproj.dtype), SDS((groups, chunk, chunk), F32),
                   SDS((groups, chunk), F32), SDS((1, d), F32)],
        input_output_aliases={7: 0},
        scratch_shapes=[pltpu.VMEM((tr, d), BF16), pltpu.VMEM((tr, d), BF16), pltpu.VMEM((tr, d), F32),
                        pltpu.VMEM((groups, chunk, chunk), F32)],
        compiler_params=_params(("arbitrary",)),
    )(proj, proj, proj, dya, norm_v, w_s, b_col, dproj)


def _dx(dproj, wg_in, x2d, dx2, norm_in):
    n, d = x2d.shape
    nsh = N_DEV // 2
    esh = wg_in.shape[1] // nsh
    tm = _tile(n, 1024)

    def body(dp_ref, w_ref, x_ref, dx2_ref, g_ref, gx_ref, dg_ref, acc):
        i, k = pl.program_id(0), pl.program_id(1)

        @pl.when(jnp.logical_and(i == 0, k == 0))
        def _():
            dg_ref[...] = jnp.zeros_like(dg_ref)

        @pl.when(k == 0)
        def _():
            acc[...] = jnp.zeros_like(acc)

        acc[...] += _dot_nt(dp_ref[...], w_ref[...])

        @pl.when(k == nsh - 1)
        def _():
            dh = acc[...]
            x = x_ref[...]
            r = _rms_scale(x)
            xh = x * r
            dg_ref[...] += jnp.sum(dh * xh, axis=0, keepdims=True)
            dxh = dh * g_ref[...]
            gx_ref[...] = dx2_ref[...] + r * (dxh - xh * jnp.mean(dxh * xh, axis=-1, keepdims=True))

    rows = pl.BlockSpec((tm, d), lambda i, k: (i, 0))
    vec = pl.BlockSpec((1, d), lambda i, k: (0, 0))
    return pl.pallas_call(
        body, name="dx", grid=(n // tm, nsh),
        in_specs=[pl.BlockSpec((tm, esh), lambda i, k: (i, k)),
                  pl.BlockSpec((d, esh), lambda i, k: (0, k)), rows, rows, vec],
        out_specs=[rows, vec],
        out_shape=[SDS((n, d), F32), SDS((1, d), F32)],
        scratch_shapes=[pltpu.VMEM((tm, d), F32)],
        compiler_params=_params(("arbitrary", "arbitrary")),
    )(dproj, wg_in, x2d, dx2, norm_in)


def _adamw_outputs(g_ref, d_ref, m_ref, v_ref, g, w, m, v):
    delta, m2, v2 = _adamw(w, g, m, v)
    g_ref[...] = g
    d_ref[...] = delta
    m_ref[...] = m2
    v_ref[...] = v2


def _reduce_adamw(slots, w, m, v, name, transposed=False):
    r, c = w.shape
    tr = _tile(r, 128)

    def body(s_ref, w_ref, m_ref, v_ref, g_out, d_out, m_out, v_out):
        g = s_ref[0].astype(F32)
        for k in range(1, N_DEV):
            g = g + s_ref[k].astype(F32)
        if transposed:
            g = g.T
        _adamw_outputs(g_out, d_out, m_out, v_out, g, w_ref[...], m_ref[...], v_ref[...])

    blk = pl.BlockSpec((tr, c), lambda i: (i, 0))
    slot_blk = (pl.BlockSpec((N_DEV, c, tr), lambda i: (0, 0, i)) if transposed
                else pl.BlockSpec((N_DEV, tr, c), lambda i: (0, i, 0)))
    return pl.pallas_call(
        body, name=name, grid=(r // tr,),
        in_specs=[slot_blk, blk, blk, blk],
        out_specs=[blk] * 4,
        out_shape=[SDS((r, c), F32)] * 4,
        compiler_params=_params(("parallel",)),
    )(slots, w, m, v)


def _adamw_small(g, w, m, v, name):
    def body(g_ref, w_ref, m_ref, v_ref, g_out, d_out, m_out, v_out):
        _adamw_outputs(g_out, d_out, m_out, v_out, g_ref[...], w_ref[...], m_ref[...], v_ref[...])

    return pl.pallas_call(
        body, name=name,
        out_shape=[SDS(g.shape, F32)] * 4,
        in_specs=[pl.BlockSpec(memory_space=pltpu.VMEM)] * 4,
        out_specs=[pl.BlockSpec(memory_space=pltpu.VMEM)] * 4,
    )(g, w, m, v)


def kernel(x, norm_in, w_in, norm_v, w_s, b_s, w_o_gmlp, w_o_sb, w_out, norm_final, loss_target, m_norm_in, m_w_in, m_norm_v, m_w_s, m_b_s, m_w_o_gmlp, m_w_o_sb, m_w_out, m_norm_final, v_norm_in, v_w_in, v_norm_v, v_w_s, v_b_s, v_w_o_gmlp, v_w_o_sb, v_w_out, v_norm_final):
    batch, seq, d = x.shape
    n = batch * seq
    groups, chunk = w_s.shape[1], w_s.shape[2]
    hd = LANE
    x2d = x.reshape(n, d)
    tgt = loss_target.reshape(n, d)
    b_col = b_s[0].reshape(groups, chunk, 1)
    norm_final2 = norm_final.reshape(1, d)

    my_slot = _slot(_me()).astype(jnp.int32).reshape(1)
    proj, h, wg_in, wg_oa, wg_ob, wg_out = _gather_in_proj(
        x2d, norm_in, w_in[0], [w_o_gmlp[0], w_o_sb[0], w_out[0]], my_slot)
    rsh = wg_oa.shape[1]
    wf_oa, wf_ob, wf_out = (w.reshape(N_DEV * rsh, d) for w in (wg_oa, wg_ob, wg_out))
    ya = _branch_a_fwd(proj, norm_v, w_s[0], b_col)
    yb, o, sb_tot = _sb_fwd(proj, batch, seq, d, hd)
    dproj, dx2, dya, dyb, merged, dpa, dpb, loss_vec, dgf = _tail(
        x2d, tgt, ya, yb, proj, wf_oa, wf_ob, wf_out, norm_final2)
    gp_wo = _dw_o([(ya, dpa), (yb, dpb), (merged, dx2)])
    dproj, gp_ws, gp_b, gp_nv = _branch_a_bwd(proj, dya, norm_v, w_s[0], b_col, dproj)

    slab = lambda a: a.reshape(d // LANE, LANE)
    gc = groups * chunk
    packed = jnp.concatenate([gp_ws.reshape(gc, chunk), gp_b, slab(gp_nv), slab(dgf), slab(loss_vec)], axis=0)
    dproj, s_oa, s_ob, s_out, packs = _sb_bwd(
        proj, o, dyb, sb_tot, dproj, gp_wo.reshape(3, N_DEV, rsh, d), packed, batch, seq, d, hd)
    grad_x, gp_nin = _dx(dproj, wg_in, x2d, dx2, norm_in)
    s_win, late_packs = _dw_in_exchange(h, dproj, my_slot, slab(gp_nin))
    tot, loss_slab = _finish_small(packs, late_packs, groups, chunk)
    ns = d // LANE
    g_ws = tot[:gc]
    g_b = tot[gc:gc + groups]
    g_nv, g_nf, _, g_nin = (tot[gc + groups + k * ns:gc + groups + (k + 1) * ns] for k in range(4))
    loss = loss_slab[0, 0]

    res = {}
    res["w_in"] = _reduce_adamw(s_win, w_in[0], m_w_in[0], v_w_in[0], "adamw_w_in", transposed=True)
    res["w_o_gmlp"] = _reduce_adamw(s_oa, w_o_gmlp[0], m_w_o_gmlp[0], v_w_o_gmlp[0], "adamw_w_o_gmlp")
    res["w_o_sb"] = _reduce_adamw(s_ob, w_o_sb[0], m_w_o_sb[0], v_w_o_sb[0], "adamw_w_o_sb")
    res["w_out"] = _reduce_adamw(s_out, w_out[0], m_w_out[0], v_w_out[0], "adamw_w_out")
    res["norm_in"] = _adamw_small(g_nin, slab(norm_in), slab(m_norm_in), slab(v_norm_in), "adamw_norm_in")
    res["norm_v"] = _adamw_small(g_nv, slab(norm_v), slab(m_norm_v), slab(v_norm_v), "adamw_norm_v")
    res["norm_final"] = _adamw_small(g_nf, slab(norm_final), slab(m_norm_final), slab(v_norm_final), "adamw_norm_final")
    res["w_s"] = _adamw_small(g_ws, w_s.reshape(gc, chunk), m_w_s.reshape(gc, chunk), v_w_s.reshape(gc, chunk), "adamw_w_s")
    res["b_s"] = _adamw_small(g_b, b_s[0], m_b_s[0], v_b_s[0], "adamw_b_s")

    shapes = {"norm_in": norm_in.shape, "w_in": w_in.shape, "norm_v": norm_v.shape, "w_s": w_s.shape,
              "b_s": b_s.shape, "w_o_gmlp": w_o_gmlp.shape, "w_o_sb": w_o_sb.shape, "w_out": w_out.shape,
              "norm_final": norm_final.shape}
    names = list(shapes)
    outs = [loss, grad_x.reshape(batch, seq, d)]
    for kind in range(4):
        outs += [res[name][kind].reshape(shapes[name]) for name in names]
    return tuple(outs)
```

```python
import functools
import math

import jax
import jax.numpy as jnp
from jax import lax
from jax.experimental import pallas as pl
from jax.experimental.pallas import tpu as pltpu

F32 = jnp.float32
BF16 = jnp.bfloat16
SDS = jax.ShapeDtypeStruct
MESH_ID = pl.DeviceIdType.MESH

N_DEV = 8
LANE = 128
SUBLANE = 8
VMEM_LIMIT = 56 * 1024 * 1024
SB_TILE = 512
SB_TILE_BWD = 512
SB_SCAN = 256
SB_HEADS = 2
MASKED_LOG = -1e30
RMS_EPS = 1e-6

ADAM_LR = 0.001
ADAM_B1 = 0.9
ADAM_B2 = 0.999
ADAM_EPS = 1e-08
ADAM_WD = 0.01
ADAM_STEP = 10

NT_DIMS = (((1,), (1,)), ((), ()))
TN_DIMS = (((0,), (0,)), ((), ()))


def _params(semantics=None):
    return pltpu.CompilerParams(dimension_semantics=semantics, vmem_limit_bytes=VMEM_LIMIT)


def _tile(n, preferred):
    t = min(n, preferred)
    assert n % t == 0, (n, t)
    return t


def _sigmoid(x):
    return 1.0 / (1.0 + jnp.exp(-x))


def _silu(x):
    s = _sigmoid(x)
    return x * s, s * (1.0 + x * (1.0 - s))


def _gelu(x):
    k = math.sqrt(2.0 / math.pi)
    x2 = x * x
    t = jnp.tanh(k * (x + 0.044715 * (x * x2)))
    cdf = 0.5 * (1.0 + t)
    return x * cdf, cdf + 0.5 * x * (1.0 - t * t) * (k * (1.0 + 3.0 * 0.044715 * x2))


def _rms_scale(x):
    return lax.rsqrt(jnp.mean(x * x, axis=-1, keepdims=True) + RMS_EPS)


def _iotas(n):
    return (lax.broadcasted_iota(jnp.int32, (n, n), 0), lax.broadcasted_iota(jnp.int32, (n, n), 1))


def _adamw(w, g, m, v):
    m = ADAM_B1 * m + (1.0 - ADAM_B1) * g
    v = ADAM_B2 * v + (1.0 - ADAM_B2) * (g * g)
    m_hat = m / (1.0 - ADAM_B1 ** ADAM_STEP)
    v_hat = v / (1.0 - ADAM_B2 ** ADAM_STEP)
    delta = -ADAM_LR * (m_hat / (jnp.sqrt(v_hat) + ADAM_EPS) + ADAM_WD * w)
    return delta, m, v


def _dot(a, b):
    return jnp.dot(a, b, preferred_element_type=F32)


def _dot_nt(a, b):
    return lax.dot_general(a, b, NT_DIMS, preferred_element_type=F32)


def _dot_tn(a, b):
    return lax.dot_general(a, b, TN_DIMS, preferred_element_type=F32)


def _sb_logs(raw, scale, valid):
    z = (raw * scale).astype(BF16)
    log_beta = jnp.minimum(z, 0) - jnp.log(1 + jnp.exp(-jnp.abs(z)))
    log_rest = log_beta - z
    if valid is not None:
        log_beta = jnp.where(valid, log_beta, MASKED_LOG)
        log_rest = jnp.where(valid, log_rest, 0)
    return log_beta, log_rest


def _me():
    return lax.axis_index("x"), lax.axis_index("y"), lax.axis_index("c")


def _slot(p):
    return 4 * p[0] + 2 * p[1] + p[2]


def _peer(me, k):
    flips = ((k >> 2) & 1, (k >> 1) & 1, k & 1)
    return tuple(1 - a if f else a for a, f in zip(me, flips))


def _stack_exchange(me, st_in, st_out, n_whole, send_sems, recv_sems, local_sems, arrivals=True):
    mine = _slot(me)
    ns = len(st_in)
    part = lambda a, dev: st_in[a] if a >= ns - n_whole else st_in[a].at[_slot(dev)]
    local = [pltpu.make_async_copy(part(a, me), st_out[a].at[mine], local_sems.at[a]) for a in range(ns)]
    remote, landed = [], []
    for k in range(1, N_DEV):
        peer = _peer(me, k)
        for a in range(ns):
            sems = dict(send_sem=send_sems.at[7 * a + k - 1], recv_sem=recv_sems.at[7 * a + k - 1])
            remote.append(pltpu.make_async_remote_copy(
                src_ref=part(a, peer), dst_ref=st_out[a].at[mine],
                device_id=peer, device_id_type=MESH_ID, **sems))
            if arrivals:
                got = st_out[a].at[_slot(peer)]
                landed.append(pltpu.make_async_remote_copy(
                    src_ref=got, dst_ref=got, device_id=me, device_id_type=MESH_ID, **sems))
    return local, remote, landed


def _gather_in_proj(x2d, norm_in, w_in_sh, wo_shards, my_slot):
    n, d = x2d.shape
    esh = w_in_sh.shape[1]
    pw = 2 * esh
    n_chip = N_DEV // 2
    tm = _tile(n, 1024)
    n_i = n // tm
    mid = n_i // 2
    no = len(wo_shards)
    flip_at = lambda st: jnp.where(st == 1, 2, jnp.where(st == 2, 1, jnp.where(st == 3, 3, 0)))

    def body(me_ref, x_ref, g_ref, win_ref, *refs):
        del me_ref
        wo_in = refs[:no]
        proj_ref, h_ref, wg_ref = refs[no:no + 3]
        wo_out = refs[no + 3:2 * no + 3]
        wv, stage = refs[2 * no + 3:2 * no + 5]
        wo_stage = refs[2 * no + 5:3 * no + 5]
        send_sems, recv_sems, pair_sems, own_sems, wo_send, wo_recv, wo_local = refs[3 * no + 5:]
        st, i = pl.program_id(0), pl.program_id(1)
        x, y, c = _me()
        me, sibling = (x, y, c), (x, y, 1 - c)
        chips = [(1 - x, y), (x, 1 - y), (1 - x, 1 - y)]
        chip_id = lambda p: 2 * p[0] + p[1]

        def window(chip, core):
            return wv.at[chip_id(chip), :, pl.ds(pl.multiple_of(core * esh, LANE), esh)]

        def copy(k, block, to, src=None):
            dst = window(block[:2], block[2])
            return pltpu.make_async_remote_copy(
                src_ref=dst if src is None else src, dst_ref=dst,
                send_sem=send_sems.at[k], recv_sem=recv_sems.at[k], device_id=to, device_id_type=MESH_ID)

        def wo_copy(a, k, block, to, src=None):
            dst = wo_out[a].at[_slot(block)]
            return pltpu.make_async_remote_copy(
                src_ref=dst if src is None else src, dst_ref=dst,
                send_sem=wo_send.at[7 * a + k], recv_sem=wo_recv.at[7 * a + k], device_id=to, device_id_type=MESH_ID)

        def own_copy():
            return pltpu.make_async_copy(stage, window((x, y), c), own_sems.at[0])

        def wo_own_copy(a):
            return pltpu.make_async_copy(wo_stage[a], wo_out[a].at[_slot(me)], wo_local.at[a])

        def pair_copy(step):
            chip = jnp.bitwise_xor(chip_id((x, y)), flip_at(step))
            return pltpu.make_async_copy(wv.at[chip], wg_ref.at[:, pl.ds(pl.multiple_of(chip * pw, LANE), pw)],
                                         pair_sems.at[step])

        first = jnp.logical_and(st == 0, i == 0)

        @pl.when(first)
        def _():
            stage[...] = win_ref[...].astype(BF16)
            own_copy().start()
            copy(0, me, sibling, src=stage).start()
            for j in range(2):
                copy(1 + j, me, (*chips[j], c), src=stage).start()
            own_copy().wait()
            copy(0, sibling, me).wait_recv()
            pair_copy(0).start()

        for s_ in range(n_chip - 1):
            @pl.when(jnp.logical_and(st == s_, i == mid))
            def _():
                copy(1 + s_, (*chips[s_], c), me).wait_recv()
                copy(4 + s_, (*chips[s_], c), sibling).start()
                if s_ == 0:
                    copy(3, me, (*chips[2], c), src=stage).start()
                if s_ == 1:
                    for a in range(no):
                        wo_stage[a][...] = wo_in[a][...].astype(BF16)
                        wo_own_copy(a).start()
                        wo_copy(a, 0, me, sibling, src=wo_stage[a]).start()
                        for j, chip in enumerate(chips):
                            wo_copy(a, 1 + j, me, (*chip, c), src=wo_stage[a]).start()
                if s_ == 2:
                    for a in range(no):
                        for j, chip in enumerate(chips):
                            wo_copy(a, 1 + j, (*chip, c), me).wait_recv()
                            wo_copy(a, 4 + j, (*chip, c), sibling).start()

        for s_ in range(1, n_chip):
            @pl.when(jnp.logical_and(st == s_, i == 0))
            def _():
                copy(3 + s_, (*chips[s_ - 1], 1 - c), me).wait_recv()
                pair_copy(s_).start()

        xv = x_ref[...]
        h = (xv * _rms_scale(xv) * g_ref[...]).astype(BF16)

        @pl.when(st == 0)
        def _():
            h_ref[...] = h

        chip_now = jnp.bitwise_xor(chip_id((x, y)), flip_at(st))
        proj_ref[...] = _dot(h, wv[chip_now]).astype(BF16)

        @pl.when(jnp.logical_and(st == n_chip - 1, i == n_i - 1))
        def _():
            copy(0, me, sibling, src=stage).wait_send()
            for j, chip in enumerate(chips):
                copy(1 + j, me, (*chip, c), src=stage).wait_send()
                copy(4 + j, (*chip, c), sibling).wait_send()
            for s_ in range(n_chip):
                pair_copy(s_).wait()
            for a in range(no):
                wo_copy(a, 0, me, sibling, src=wo_stage[a]).wait_send()
                wo_copy(a, 0, sibling, me).wait_recv()
                for j, chip in enumerate(chips):
                    wo_copy(a, 1 + j, me, (*chip, c), src=wo_stage[a]).wait_send()
                    wo_copy(a, 4 + j, (*chip, c), sibling).wait_send()
                    wo_copy(a, 4 + j, (*chip, 1 - c), me).wait_recv()
                wo_own_copy(a).wait()

    any_spec = pl.BlockSpec(memory_space=pl.ANY)
    vmem = pl.BlockSpec(memory_space=pltpu.VMEM)
    grid_spec = pltpu.PrefetchScalarGridSpec(
        num_scalar_prefetch=1, grid=(n_chip, n_i),
        in_specs=[pl.BlockSpec((tm, d), lambda st, i, me: (i, 0)),
                  pl.BlockSpec((1, d), lambda st, i, me: (0, 0)), vmem] + [vmem] * no,
        out_specs=[pl.BlockSpec((tm, pw), lambda st, i, me: (i, jnp.bitwise_xor(me[0] // 2, flip_at(st)))),
                   pl.BlockSpec((tm, d), lambda st, i, me: (jnp.where(st == 0, i, n_i - 1), 0)),
                   any_spec] + [any_spec] * no,
        scratch_shapes=[pltpu.VMEM((n_chip, d, pw), BF16), pltpu.VMEM((d, esh), BF16)] + [
            pltpu.VMEM(s.shape, BF16) for s in wo_shards] + [
            pltpu.SemaphoreType.DMA((7,)), pltpu.SemaphoreType.DMA((7,)),
            pltpu.SemaphoreType.DMA((n_chip,)), pltpu.SemaphoreType.DMA((1,)),
            pltpu.SemaphoreType.DMA((7 * no,)), pltpu.SemaphoreType.DMA((7 * no,)),
            pltpu.SemaphoreType.DMA((no,))])
    return pl.pallas_call(
        body, name="gather_in_proj", grid_spec=grid_spec,
        out_shape=[SDS((n, n_chip * pw), BF16), SDS((n, d), BF16), SDS((d, n_chip * pw), BF16)] + [
            SDS((N_DEV,) + s.shape, BF16) for s in wo_shards],
        compiler_params=pltpu.CompilerParams(dimension_semantics=("arbitrary", "arbitrary"),
                                             vmem_limit_bytes=VMEM_LIMIT),
    )(my_slot, x2d, norm_in, w_in_sh, *wo_shards)


N_CHIP = N_DEV // 2
CHIP_FLIPS = (3, 2, 1, 0)


def _owner_at(mine, j):
    flip = 0
    for pair, f in enumerate(CHIP_FLIPS):
        flip = jnp.where(j // 2 == pair, f, flip)
    return 2 * jnp.bitwise_xor(mine // 2, flip) + j % 2


def _dw_in_exchange(h, dproj, my_slot, packed):
    n, d = h.shape
    esh = dproj.shape[1] // N_DEV
    tk = _tile(n, 2048)
    nk = n // tk
    last_j = N_DEV - 1

    def body(me_ref, h_ref, dp_ref, pk_in, win_out, pk_out,
             acc, halfbuf, recvbuf, sendbuf, half_send, half_recv, win_send, win_recv,
             send_sems, recv_sems, local_sems):
        del me_ref
        j, k = pl.program_id(0), pl.program_id(1)
        x, y, c = _me()
        me, sibling = (x, y, c), (x, y, 1 - c)
        mine = _slot(me)
        my_chip = mine // 2

        def pack_copies():
            local = pltpu.make_async_copy(pk_in, pk_out.at[mine], local_sems.at[0])
            remote = [pltpu.make_async_remote_copy(
                src_ref=pk_in, dst_ref=pk_out.at[mine], send_sem=send_sems.at[kk - 1], recv_sem=recv_sems.at[kk - 1],
                device_id=_peer(me, kk), device_id_type=MESH_ID) for kk in range(1, N_DEV)]
            return local, remote

        def half_copy(jj):
            slot = (jj // 2) % 2
            return pltpu.make_async_remote_copy(
                src_ref=halfbuf.at[slot], dst_ref=recvbuf.at[slot],
                send_sem=half_send.at[slot], recv_sem=half_recv.at[slot],
                device_id=sibling, device_id_type=MESH_ID)

        def chip_copy(jj):
            slot = (jj // 2) % 2
            owner = _owner_at(mine, jj)
            return pltpu.make_async_remote_copy(
                src_ref=sendbuf.at[slot], dst_ref=win_out.at[my_chip],
                send_sem=win_send.at[slot], recv_sem=win_recv.at[my_chip],
                device_id=(owner // 4, (owner // 2) % 2, owner % 2), device_id_type=MESH_ID)

        def own_copy():
            return pltpu.make_async_copy(sendbuf.at[(last_j // 2) % 2], win_out.at[my_chip], local_sems.at[1])

        @pl.when(jnp.logical_and(j == 0, k == 0))
        def _():
            local, remote = pack_copies()
            for cp in [local] + remote:
                cp.start()

        @pl.when(k == 0)
        def _():
            acc[...] = jnp.zeros_like(acc)

        acc[...] += _dot_tn(dp_ref[...], h_ref[...])

        done = k == nk - 1
        combine = j % 2 == c
        slot = (j // 2) % 2

        @pl.when(jnp.logical_and(done, jnp.logical_not(combine)))
        def _():
            @pl.when(j >= 4)
            def _():
                half_copy(j - 4).wait_send()

            halfbuf[slot] = acc[...].astype(BF16)
            half_copy(j).start()

        @pl.when(jnp.logical_and(done, combine))
        def _():
            half_copy(j).wait_recv()

            @pl.when(j >= 4)
            def _():
                chip_copy(j - 4).wait_send()

            sendbuf[slot] = (acc[...] + recvbuf[slot].astype(F32)).astype(BF16)

            @pl.when(j < last_j - 1)
            def _():
                chip_copy(j).start()

            @pl.when(j >= last_j - 1)
            def _():
                own_copy().start()

        @pl.when(jnp.logical_and(j == last_j, done))
        def _():
            half_copy(5 - c).wait_send()
            half_copy(7 - c).wait_send()
            chip_copy(4 + c).wait_send()
            own_copy().wait()
            for chip in range(N_CHIP):
                @pl.when(chip != my_chip)
                def _():
                    landed = win_out.at[chip]
                    pltpu.make_async_remote_copy(
                        src_ref=landed, dst_ref=landed, send_sem=win_send.at[0], recv_sem=win_recv.at[chip],
                        device_id=me, device_id_type=MESH_ID).wait_recv()
            local, remote = pack_copies()
            for cp in remote:
                cp.wait_send()
            for kk in range(1, N_DEV):
                landed = pk_out.at[_slot(_peer(me, kk))]
                pltpu.make_async_remote_copy(
                    src_ref=landed, dst_ref=landed, send_sem=send_sems.at[kk - 1], recv_sem=recv_sems.at[kk - 1],
                    device_id=me, device_id_type=MESH_ID).wait_recv()
            local.wait()

    any_spec = pl.BlockSpec(memory_space=pl.ANY)
    grid_spec = pltpu.PrefetchScalarGridSpec(
        num_scalar_prefetch=1, grid=(N_DEV, nk),
        in_specs=[pl.BlockSpec((tk, d), lambda j, k, me: (k, 0)),
                  pl.BlockSpec((tk, esh), lambda j, k, me: (k, _owner_at(me[0], j))), any_spec],
        out_specs=[any_spec] * 2,
        scratch_shapes=[pltpu.VMEM((esh, d), F32)] + [pltpu.VMEM((2, esh, d), BF16)] * 3 + [
            pltpu.SemaphoreType.DMA((2,)), pltpu.SemaphoreType.DMA((2,)),
            pltpu.SemaphoreType.DMA((2,)), pltpu.SemaphoreType.DMA((N_CHIP,)),
            pltpu.SemaphoreType.DMA((N_DEV - 1,)), pltpu.SemaphoreType.DMA((N_DEV - 1,)),
            pltpu.SemaphoreType.DMA((2,))])
    return pl.pallas_call(
        body, name="dw_in_exchange", grid_spec=grid_spec,
        out_shape=[SDS((N_CHIP, esh, d), BF16), SDS((N_DEV,) + packed.shape, packed.dtype)],
        compiler_params=_params(("arbitrary", "arbitrary")),
    )(my_slot, h, dproj, packed)


def _finish_small(packs, late_packs, groups, chunk):
    rows = packs.shape[1]
    late = late_packs.shape[1]
    gc = groups * chunk

    def body(p_ref, l_ref, sum_ref, loss_ref):
        row, col = _iotas(chunk)
        tril = col <= row
        for g in range(groups):
            rs = slice(g * chunk, (g + 1) * chunk)
            tot = p_ref[0, rs, :]
            for dev in range(1, N_DEV):
                tot = tot + p_ref[dev, rs, :]
            sum_ref[rs, :] = jnp.where(tril, tot, 0.0)
        rs = slice(gc, rows)
        tot = p_ref[0, rs, :]
        for dev in range(1, N_DEV):
            tot = tot + p_ref[dev, rs, :]
        sum_ref[rs, :] = tot
        loss_ref[...] = jnp.full((SUBLANE, LANE), jnp.sum(tot[rows - gc - SUBLANE:, :]), F32)
        tot = l_ref[0]
        for dev in range(1, N_DEV):
            tot = tot + l_ref[dev]
        sum_ref[rows:rows + late, :] = tot

    return pl.pallas_call(
        body, name="finish_small",
        out_shape=[SDS((rows + late, LANE), F32), SDS((SUBLANE, LANE), F32)],
        in_specs=[pl.BlockSpec(memory_space=pltpu.VMEM)] * 2,
        out_specs=[pl.BlockSpec(memory_space=pltpu.VMEM)] * 2,
        compiler_params=pltpu.CompilerParams(vmem_limit_bytes=VMEM_LIMIT),
    )(packs, late_packs)


def _branch_a_fwd(proj, norm_v, w_s, b_col):
    n = proj.shape[0]
    d = norm_v.shape[1]
    groups, chunk, _ = w_s.shape
    tr = _tile(n, 4 * chunk)

    def body(u_ref, v_ref, z_ref, gv_ref, ws_ref, b_ref, ya_ref, vn_s, pre_s):
        row, col = _iotas(chunk)
        tril = col <= row
        vg = _gelu(v_ref[...])[0].astype(F32)
        vn_s[...] = (vg * _rms_scale(vg) * gv_ref[...]).astype(BF16)
        pre_s[...] = _gelu(u_ref[...])[0] * _silu(z_ref[...])[0]
        for g in range(groups):
            wm = jnp.where(tril, ws_ref[g], 0.0).astype(BF16)
            cs = slice(g * chunk, (g + 1) * chunk)
            for c in range(tr // chunk):
                rs = slice(c * chunk, (c + 1) * chunk)
                mixed = _dot(wm, vn_s[rs, cs]) + b_ref[g]
                ya_ref[rs, cs] = (pre_s[rs, cs].astype(F32) * mixed).astype(BF16)

    seg = lambda k: pl.BlockSpec((tr, d), lambda i: (i, k))
    return pl.pallas_call(
        body, name="branch_a_fwd", grid=(n // tr,),
        in_specs=[seg(0), seg(1), seg(2),
                  pl.BlockSpec((1, d), lambda i: (0, 0)),
                  pl.BlockSpec((groups, chunk, chunk), lambda i: (0, 0, 0)),
                  pl.BlockSpec((groups, chunk, 1), lambda i: (0, 0, 0))],
        out_specs=pl.BlockSpec((tr, d), lambda i: (i, 0)),
        out_shape=SDS((n, d), BF16),
        scratch_shapes=[pltpu.VMEM((tr, d), BF16), pltpu.VMEM((tr, d), BF16)],
        compiler_params=_params(("parallel",)),
    )(proj, proj, proj, norm_v, w_s, b_col)


def _sb_fwd(proj, batch, seq, d, hd):
    heads = d // hd
    t = _tile(seq, SB_TILE)
    sw = _tile(t, SB_SCAN)
    nb = t // sw
    scale = hd ** -0.5
    nblk = seq // t
    nh = SB_HEADS
    wide = nh * hd
    cols = [slice(hh * hd, (hh + 1) * hd) for hh in range(nh)]

    def body(qs, k_ref, vs, zb_ref, yb_ref, o_ref, tot_ref, kts, later, acc):
        for jb in range(nblk):
            kts[jb] = k_ref[jb * t:(jb + 1) * t, :].T
        row, col = _iotas(t)
        later[...] = (row[:sw, :sw] > col[:sw, :sw]).astype(BF16)

        def qblock(i, carry):
            r0 = pl.multiple_of(i * t, t)

            def tile(j, runs):
                c0 = pl.multiple_of(j * t, t)
                logs = [_sb_logs(_dot(qs[pl.ds(r0, t), cs], kts[j, cs, :]), scale, None) for cs in cols]
                scans = [_dot(jnp.concatenate([logs[hh][1][:, b * sw:(b + 1) * sw] for b in range(nb)], axis=0),
                              later[...]) for hh in range(nh)]
                new_runs = []
                for hh in range(nh):
                    after = runs[hh]
                    blocks = [None] * nb
                    for b in reversed(range(nb)):
                        ks_ = slice(b * sw, (b + 1) * sw)
                        inside = scans[hh][b * t:(b + 1) * t]
                        blocks[b] = jnp.exp(logs[hh][0][:, ks_].astype(F32) + inside + after).astype(BF16)
                        after = after + inside[:, 0:1] + logs[hh][1][:, b * sw:b * sw + 1].astype(F32)
                    new_runs.append(after)
                    acc[:, cols[hh]] += _dot(jnp.concatenate(blocks, axis=1), vs[pl.ds(c0, t), cols[hh]])
                return tuple(new_runs)

            def diagonal_tile():
                starts = [b * sw for b in range(nb)]
                logs = [[_sb_logs(_dot(qs[pl.ds(r0 + s, t - s), cs], kts[i, cs, s:s + sw]), scale,
                                  col[:t - s, :sw] < row[:t - s, :sw]) for s in starts] for cs in cols]
                scans = [_dot(jnp.concatenate([lr for _, lr in logs[hh]], axis=0), later[...]) for hh in range(nh)]
                new_runs = []
                offs = [sum(t - s for s in starts[:b]) for b in range(nb)]
                for hh in range(nh):
                    after = jnp.zeros((t, 1), F32)
                    ws = [None] * nb
                    for b in reversed(range(nb)):
                        s = starts[b]
                        lb, lr = logs[hh][b]
                        inside = scans[hh][offs[b]:offs[b] + t - s]
                        ws[b] = jnp.exp(lb.astype(F32) + inside + after[s:]).astype(BF16)
                        total = inside[:, 0:1] + lr[:, 0:1].astype(F32)
                        after = after + total if s == 0 else jnp.concatenate([after[:s], after[s:] + total], axis=0)
                    new_runs.append(after)
                    acc[:, cols[hh]] = _dot(ws[0], vs[pl.ds(r0, sw), cols[hh]])
                    for b in range(1, nb):
                        acc[starts[b]:, cols[hh]] += _dot(ws[b], vs[pl.ds(r0 + starts[b], sw), cols[hh]])
                return tuple(new_runs)

            runs = diagonal_tile()
            runs = lax.fori_loop(0, i, lambda jj, rs: tile(i - 1 - jj, rs), runs)
            for hh in range(nh):
                out = acc[:, cols[hh]]
                o_ref[pl.ds(r0, t), cols[hh]] = out.astype(BF16)
                tot_ref[hh, pl.ds(r0, t), :] = runs[hh]
                sz, _ = _silu(zb_ref[pl.ds(r0, t), cols[hh]].astype(F32))
                yb_ref[pl.ds(r0, t), cols[hh]] = (out * sz).astype(BF16)
            return carry

        lax.fori_loop(0, nblk, qblock, 0)

    col0 = d // wide
    seg = lambda k: pl.BlockSpec((seq, wide), lambda b, h: (b, k * col0 + h))
    return pl.pallas_call(
        body, name="sb_fwd", grid=(batch, heads // nh),
        in_specs=[seg(3), seg(4), seg(5), seg(6)],
        out_specs=[pl.BlockSpec((seq, wide), lambda b, h: (b, h))] * 2 + [
            pl.BlockSpec((nh, seq, 1), lambda b, h: (b * (heads // nh) + h, 0, 0))],
        out_shape=[SDS((batch * seq, d), BF16), SDS((batch * seq, d), BF16), SDS((batch * heads, seq, 1), F32)],
        scratch_shapes=[pltpu.VMEM((nblk, wide, t), BF16), pltpu.VMEM((sw, sw), BF16), pltpu.VMEM((t, wide), F32)],
        compiler_params=_params(("parallel", "parallel")),
    )(proj, proj, proj, proj)


def _tail(x2d, tgt, ya, yb, proj, w_oa, w_ob, w_out, norm_final):
    n, d = x2d.shape
    e = proj.shape[1]
    tm = _tile(n, 512)
    steps = n // tm

    def body(x_ref, t_ref, ya_ref, yb_ref, ga_ref, gb_ref, woa_ref, wob_ref, wout_ref, gf_ref,
             dproj_ref, dx2_ref, dya_ref, dyb_ref, mrg_ref, dpa_ref, dpb_ref, loss_ref, dgf_ref, dg_s, dg_sems):
        i = pl.program_id(0)

        def gate_copy(step):
            rows_ = pl.ds(pl.multiple_of(step * tm, tm), tm)
            return pltpu.make_async_copy(dg_s.at[step % 2], dproj_ref.at[rows_, pl.ds(7 * d, 2 * d)],
                                         dg_sems.at[step % 2])

        @pl.when(i == 0)
        def _():
            loss_ref[...] = jnp.zeros_like(loss_ref)
            dgf_ref[...] = jnp.zeros_like(dgf_ref)

        @pl.when(i >= 2)
        def _():
            gate_copy(i - 2).wait()

        pa = _dot(ya_ref[...], woa_ref[...])
        pb = _dot(yb_ref[...], wob_ref[...])
        sa = _sigmoid(ga_ref[...].astype(F32))
        sb = _sigmoid(gb_ref[...].astype(F32))
        merged = (sa * pa + sb * pb).astype(BF16)
        mrg_ref[...] = merged
        x2 =x_ref[...] + _dot(merged, wout_ref[...])
        r2 = _rms_scale(x2)
        xh = x2 * r2
        gf = gf_ref[...]
        diff = xh * gf - t_ref[...]
        loss_ref[...] += jnp.sum(diff * diff, axis=0, keepdims=True) * (0.5 / d)
        dy = diff * (1.0 / d)
        dgf_ref[...] += jnp.sum(dy * xh, axis=0, keepdims=True)
        dxh = dy * gf
        dx2 = r2 * (dxh - xh * jnp.mean(dxh * xh, axis=-1, keepdims=True))
        dx2_ref[...] = dx2
        dm = _dot_nt(dx2.astype(BF16), wout_ref[...])
        dpa = (dm * sa).astype(BF16)
        dpb = (dm * sb).astype(BF16)
        dpa_ref[...] = dpa
        dpb_ref[...] = dpb
        dg_s[i % 2, :, 0:d] = (dm * pa * (sa * (1.0 - sa))).astype(BF16)
        dg_s[i % 2, :, d:2 * d] = (dm * pb * (sb * (1.0 - sb))).astype(BF16)
        gate_copy(i).start()
        dya_ref[...] = _dot_nt(dpa, woa_ref[...]).astype(BF16)
        dyb_ref[...] = _dot_nt(dpb, wob_ref[...]).astype(BF16)

        @pl.when(i == steps - 1)
        def _():
            if steps >= 2:
                gate_copy(i - 1).wait()
            gate_copy(i).wait()

    rows = lambda k=0: pl.BlockSpec((tm, d), lambda i: (i, k))
    full = pl.BlockSpec((d, d), lambda i: (0, 0), pipeline_mode=pl.Buffered(1))
    vec = pl.BlockSpec((1, d), lambda i: (0, 0))
    return pl.pallas_call(
        body, name="tail", grid=(steps,),
        in_specs=[rows(), rows(), rows(), rows(), rows(7), rows(8), full, full, full, vec],
        out_specs=[pl.BlockSpec(memory_space=pl.ANY),
                   rows(), rows(), rows(), rows(), rows(), rows(), vec, vec],
        out_shape=[SDS((n, e), BF16), SDS((n, d), F32), SDS((n, d), BF16), SDS((n, d), BF16),
                   SDS((n, d), BF16), SDS((n, d), BF16), SDS((n, d), BF16),
                   SDS((1, d), F32), SDS((1, d), F32)],
        scratch_shapes=[pltpu.VMEM((2, tm, 2 * d), BF16), pltpu.SemaphoreType.DMA((2,))],
        compiler_params=_params(("arbitrary",)),
    )(x2d, tgt, ya, yb, proj, proj, w_oa, w_ob, w_out, norm_final)


def _dw_o(pairs):
    n, d = pairs[0][0].shape
    tk = _tile(n, 1024)
    nk = n // tk
    npair = len(pairs)

    def body(*refs):
        a_refs, b_refs = refs[:npair], refs[npair:2 * npair]
        o_ref, acc = refs[2 * npair], refs[2 * npair + 1]
        p, k = pl.program_id(0), pl.program_id(1)

        @pl.when(k == 0)
        def _():
            acc[...] = jnp.zeros_like(acc)

        for q in range(npair):
            @pl.when(p == q)
            def _():
                acc[...] += _dot_tn(a_refs[q][...], b_refs[q][...].astype(BF16))

        @pl.when(k == nk - 1)
        def _():
            o_ref[0] = acc[...].astype(BF16)

    def tiles(q):
        return pl.BlockSpec((tk, d), lambda p, k: (jnp.where(p == q, k, jnp.where(p < q, 0, nk - 1)), 0))

    return pl.pallas_call(
        body, name="dw_o", grid=(npair, nk),
        in_specs=[tiles(q) for q in range(npair)] * 2,
        out_specs=pl.BlockSpec((1, d, d), lambda p, k: (p, 0, 0)),
        out_shape=SDS((npair, d, d), BF16),
        scratch_shapes=[pltpu.VMEM((d, d), F32)],
        compiler_params=_params(("arbitrary", "arbitrary")),
    )(*[a for a, _ in pairs], *[b for _, b in pairs])


def _sb_bwd(proj, o, dyb, tot, dproj, dw_stack, packed, batch, seq, d, hd):
    heads = d // hd
    t = _tile(seq, SB_TILE_BWD)
    sw = _tile(t, SB_SCAN)
    nb = t // sw
    scale = hd ** -0.5
    nblk = seq // t
    nh = SB_HEADS
    wide = nh * hd
    hs = range(nh)
    cols = [slice(hh * hd, (hh + 1) * hd) for hh in hs]
    blocks = [slice(b * sw, (b + 1) * sw) for b in range(nb)]
    last = slice(sw - 1, sw)

    def compute(qs, ks, v_ref, zb_ref, o_ref, dyb_ref, tot_ref, kts, vts, dos, dzb, dq_all, dkv_t, qt_s, dot_s,
                upto, before, dq):
        for jb in range(nblk):
            rows = slice(jb * t, (jb + 1) * t)
            kts[jb] = ks[rows, :].T
            vts[jb] = v_ref[rows, :].T
        sz, dsz = _silu(zb_ref[...])
        dyb_v = dyb_ref[...]
        dos[...] = dyb_v * sz
        dzb[...] = dyb_v * o_ref[...] * dsz
        row, col = _iotas(t)
        upto[...] = (row[:sw, :sw] <= col[:sw, :sw]).astype(BF16)
        before[...] = (row[:sw, :sw] < col[:sw, :sw]).astype(BF16)

        def qblock(i, carry):
            r0 = pl.multiple_of(i * t, t)

            def tile(j, sums):
                c0 = pl.multiple_of(j * t, t)
                q_i = [qs[pl.ds(r0, t), cs] for cs in cols]
                do_i = [dos[pl.ds(r0, t), cs] for cs in cols]
                logs = [_sb_logs(_dot(q_i[hh], kts[j, cols[hh], :]), scale, None) for hh in hs]
                dw = [_dot(do_i[hh], vts[j, cols[hh], :]) for hh in hs]
                scans = [_dot(jnp.concatenate([logs[hh][1][:, ks_] for ks_ in blocks], axis=0), upto[...]) for hh in hs]
                ws, gs, new_runs = [], [], []
                for hh in hs:
                    left = tot_ref[hh, pl.ds(r0, t), :] - sums[hh][0]
                    w_b, g_b = [], []
                    for b, ks_ in enumerate(blocks):
                        inside = scans[hh][b * t:(b + 1) * t]
                        w = jnp.exp(logs[hh][0][:, ks_].astype(F32) + (left - inside))
                        w_b.append(w.astype(BF16))
                        g_b.append((dw[hh][:, ks_] * w).astype(BF16))
                        left = left - inside[:, last]
                    ws.append(jnp.concatenate(w_b, axis=1))
                    gs.append(g_b)
                    new_runs.append(tot_ref[hh, pl.ds(r0, t), :] - left)
                gscans = [_dot(jnp.concatenate(gs[hh], axis=0), before[...]) for hh in hs]
                dzs, new_gruns = [], []
                for hh in hs:
                    g_before = sums[hh][1]
                    dz_b = []
                    for b, ks_ in enumerate(blocks):
                        inside = gscans[hh][b * t:(b + 1) * t]
                        beta = jnp.exp(logs[hh][0][:, ks_]).astype(F32)
                        g = gs[hh][b].astype(F32)
                        dz_b.append(((g - (g + inside + g_before) * beta) * scale).astype(BF16))
                        g_before = g_before + inside[:, last] + g[:, last]
                    dzs.append(jnp.concatenate(dz_b, axis=1))
                    new_gruns.append(g_before)
                for hh in hs:
                    dkv_t[1, j, cols[hh], :] += _dot(dot_s[cols[hh], :], ws[hh])
                for hh in hs:
                    dkv_t[0, j, cols[hh], :] += _dot(qt_s[cols[hh], :], dzs[hh])
                for hh in hs:
                    dq[:, cols[hh]] += _dot(dzs[hh], ks[pl.ds(c0, t), cols[hh]])
                return tuple((new_runs[hh], new_gruns[hh]) for hh in hs)

            def diagonal_tile(sums):
                starts = [b * sw for b in range(nb)]
                offs = [sum(t - s for s in starts[:b]) for b in range(nb)]
                q_b = [[qs[pl.ds(r0 + s, t - s), cs] for s in starts] for cs in cols]
                do_b = [[dos[pl.ds(r0 + s, t - s), cs] for s in starts] for cs in cols]
                logs = [[_sb_logs(_dot(q_b[hh][b], kts[i, cols[hh], s:s + sw]), scale,
                                  col[:t - s, :sw] < row[:t - s, :sw]) for b, s in enumerate(starts)] for hh in hs]
                dw = [[_dot(do_b[hh][b], vts[i, cols[hh], s:s + sw]) for b, s in enumerate(starts)] for hh in hs]
                scans = [_dot(jnp.concatenate([lr for _, lr in logs[hh]], axis=0), upto[...]) for hh in hs]
                ws, gs = [], []
                for hh in hs:
                    left = tot_ref[hh, pl.ds(r0, t), :] - sums[hh][0]
                    w_b, g_b = [], []
                    for b, s in enumerate(starts):
                        inside = scans[hh][offs[b]:offs[b] + t - s]
                        w = jnp.exp(logs[hh][b][0].astype(F32) + (left[s:] - inside))
                        w_b.append(w.astype(BF16))
                        g_b.append((dw[hh][b] * w).astype(BF16))
                        total = inside[:, last]
                        left = left - total if s == 0 else jnp.concatenate([left[:s], left[s:] - total], axis=0)
                    ws.append(w_b)
                    gs.append(g_b)
                gscans = [_dot(jnp.concatenate(gs[hh], axis=0), before[...]) for hh in hs]
                dzs = []
                for hh in hs:
                    g_before = sums[hh][1]
                    dz_b = []
                    for b, s in enumerate(starts):
                        inside = gscans[hh][offs[b]:offs[b] + t - s]
                        beta = jnp.exp(logs[hh][b][0]).astype(F32)
                        g = gs[hh][b].astype(F32)
                        dz_b.append(((g - (g + inside + g_before[s:]) * beta) * scale).astype(BF16))
                        total = inside[:, last] + g[:, last]
                        g_before = g_before + total if s == 0 else jnp.concatenate(
                            [g_before[:s], g_before[s:] + total], axis=0)
                    dzs.append(dz_b)
                for hh in hs:
                    for b, s in enumerate(starts):
                        dkv_t[1, i, cols[hh], s:s + sw] = _dot(dot_s[cols[hh], s:], ws[hh][b])
                for hh in hs:
                    for b, s in enumerate(starts):
                        dkv_t[0, i, cols[hh], s:s + sw] = _dot(qt_s[cols[hh], s:], dzs[hh][b])
                for hh in hs:
                    for b, s in enumerate(starts):
                        dq[s:, cols[hh]] += _dot(dzs[hh][b], ks[pl.ds(r0 + s, sw), cols[hh]])

            qt_s[...] = qs[pl.ds(r0, t), :].T
            dot_s[...] = dos[pl.ds(r0, t), :].T
            zero = jnp.zeros((t, 1), F32)
            dq[...] = jnp.zeros_like(dq)
            sums = lax.fori_loop(0, i, tile, ((zero, zero),) * nh)
            diagonal_tile(sums)
            dq_all[pl.ds(r0, t), :] = dq[...]
            return carry

        lax.fori_loop(0, nblk, qblock, 0)

    pairs = heads // nh

    nst = dw_stack.shape[0]
    ns = nst + 1

    def body(qs, ks, v_ref, zb_ref, o_ref, dyb_ref, tot_ref, dproj_in, dw_ref, pk_ref, out_ref, *refs):
        del dproj_in
        st_in = [dw_ref.at[k] for k in range(nst)] + [pk_ref]
        st_out = refs[:ns]
        (kts, vts, dos, dzb, dq_all, dkv_t, qt_s, dot_s, upto, before, dq, stage, stage_sems,
         send_sems, recv_sems, local_sems) = refs[ns:]
        step = pl.program_id(0) * pairs + pl.program_id(1)
        exchange = functools.partial(_stack_exchange, _me(), st_in, st_out, 1, send_sems, recv_sems, local_sems)

        @pl.when(step == 0)
        def _():
            local, remote, _ = exchange(arrivals=False)
            for cp in local + remote:
                cp.start()

        def out_copies(s):
            rows_ = pl.ds(pl.multiple_of((s // pairs) * seq, seq), seq)
            return [pltpu.make_async_copy(
                stage.at[k], out_ref.at[rows_, pl.ds(pl.multiple_of((3 + k) * d + (s % pairs) * wide, wide), wide)],
                stage_sems.at[k]) for k in range(4)]

        compute(qs, ks, v_ref, zb_ref, o_ref, dyb_ref, tot_ref, kts, vts, dos, dzb, dq_all, dkv_t, qt_s, dot_s,
                upto, before, dq)

        @pl.when(step > 0)
        def _():
            for cp in out_copies(step - 1):
                cp.wait()

        stage[0] = dq_all[...].astype(BF16)
        for k in range(2):
            for jb in range(nblk):
                stage[1 + k, jb * t:(jb + 1) * t, :] = dkv_t[k, jb].astype(BF16).T
        stage[3] = dzb[...]
        for cp in out_copies(step):
            cp.start()

        @pl.when(step == batch * pairs - 1)
        def _():
            for cp in out_copies(step):
                cp.wait()
            local, remote, landed = exchange()
            for cp in remote:
                cp.wait_send()
            for cp in landed:
                cp.wait_recv()
            for cp in local:
                cp.wait()

    col0 = d // wide
    seg = lambda k: pl.BlockSpec((seq, wide), lambda b, h: (b, k * col0 + h))
    head = pl.BlockSpec((seq, wide), lambda b, h: (b, h))
    any_spec = pl.BlockSpec(memory_space=pl.ANY)
    return pl.pallas_call(
        body, name="sb_bwd", grid=(batch, pairs),
        in_specs=[seg(3), seg(4), seg(5), seg(6), head, head,
                  pl.BlockSpec((nh, seq, 1), lambda b, h: (b * pairs + h, 0, 0))] + [any_spec] * 3,
        out_specs=[any_spec] * (ns + 1),
        out_shape=[SDS(dproj.shape, dproj.dtype)] + [SDS(dw_stack.shape[1:], dw_stack.dtype)] * nst + [
            SDS((N_DEV,) + packed.shape, packed.dtype)],
        input_output_aliases={7: 0},
        scratch_shapes=[pltpu.VMEM((nblk, wide, t), BF16)] * 2 + [
            pltpu.VMEM((seq, wide), BF16), pltpu.VMEM((seq, wide), BF16),
            pltpu.VMEM((seq, wide), F32), pltpu.VMEM((2, nblk, wide, t), F32),
            pltpu.VMEM((wide, t), BF16), pltpu.VMEM((wide, t), BF16),
            pltpu.VMEM((sw, sw), BF16), pltpu.VMEM((sw, sw), BF16), pltpu.VMEM((t, wide), F32),
            pltpu.VMEM((4, seq, wide), BF16), pltpu.SemaphoreType.DMA((4,)),
            pltpu.SemaphoreType.DMA((7 * ns,)), pltpu.SemaphoreType.DMA((7 * ns,)),
            pltpu.SemaphoreType.DMA((ns,))],
        compiler_params=_params(("arbitrary", "arbitrary")),
    )(proj, proj, proj, proj, o, dyb, tot, dproj, dw_stack, packed)


def _branch_a_bwd(proj, dya, norm_v, w_s, b_col, dproj):
    n = proj.shape[0]
    d = norm_v.shape[1]
    groups, chunk, _ = w_s.shape
    tr = _tile(n, 2 * chunk)

    def body(u_ref, v_ref, z_ref, dya_ref, gv_ref, ws_ref, b_ref, dproj_in,
             out_ref, dws_ref, dbias_ref, dgv_ref, vn_s, dmix_s, dvn_s, db_ref):
        del dproj_in

        @pl.when(pl.program_id(0) == 0)
        def _():
            dws_ref[...] = jnp.zeros_like(dws_ref)
            db_ref[...] = jnp.zeros_like(db_ref)
            dgv_ref[...] = jnp.zeros_like(dgv_ref)

        row, col = _iotas(chunk)
        tril = col <= row
        gv = gv_ref[...]
        vg16, dvg_dv = _gelu(v_ref[...])
        vg = vg16.astype(F32)
        r = _rms_scale(vg)
        vh = vg * r
        vn_s[...] = (vh * gv).astype(BF16)
        ug, dug_du = _gelu(u_ref[...])
        sz, dsz = _silu(z_ref[...])
        dya_v = dya_ref[...]
        dmix_s[...] = dya_v * ug * sz
        du_scale = sz * dug_du
        dz_scale = ug * dsz
        for g in range(groups):
            wm = jnp.where(tril, ws_ref[g], 0.0).astype(BF16)
            cs = slice(g * chunk, (g + 1) * chunk)
            for c in range(tr // chunk):
                rs = slice(c * chunk, (c + 1) * chunk)
                vn = vn_s[rs, cs]
                mixed = _dot(wm, vn) + b_ref[g]
                dmix16 = dmix_s[rs, cs]
                dws_ref[g] += _dot_nt(dmix16, vn)
                db_ref[g] += dmix16.astype(F32)
                dvn_s[rs, cs] = _dot_tn(wm, dmix16)
                t_u = dya_v[rs, cs] * mixed.astype(BF16)
                out_ref[rs, g * chunk:(g + 1) * chunk] = t_u * du_scale[rs, cs]
                out_ref[rs, 2 * d + g * chunk:2 * d + (g + 1) * chunk] = t_u * dz_scale[rs, cs]
        dvn = dvn_s[...]
        dgv_ref[...] += jnp.sum(dvn * vh, axis=0, keepdims=True)
        dvh = dvn * gv
        dvg = r * (dvh - vh * jnp.mean(dvh * vh, axis=-1, keepdims=True))
        out_ref[:, d:2 * d] = (dvg * dvg_dv.astype(F32)).astype(BF16)

        @pl.when(pl.program_id(0) == n // tr - 1)
        def _():
            for g in range(groups):
                dbias_ref[g:g + 1, :] = jnp.sum(db_ref[g].T, axis=0, keepdims=True)

    seg = lambda k: pl.BlockSpec((tr, d), lambda i: (i, k))
    return pl.pallas_call(
        body, name="branch_a_bwd", grid=(n // tr,),
        in_specs=[seg(0), seg(1), seg(2), seg(0),
                  pl.BlockSpec((1, d), lambda i: (0, 0)),
                  pl.BlockSpec((groups, chunk, chunk), lambda i: (0, 0, 0)),
                  pl.BlockSpec((groups, chunk, 1), lambda i: (0, 0, 0)),
                  pl.BlockSpec(memory_space=pl.ANY)],
        out_specs=[pl.BlockSpec((tr, 3 * d), lambda i: (i, 0)),
                   pl.BlockSpec((groups, chunk, chunk), lambda i: (0, 0, 0)),
                   pl.BlockSpec((groups, chunk), lambda i: (0, 0)),
                   pl.BlockSpec((1, d), lambda i: (0, 0))],
        out_shape=[SDS(dproj.shape, dproj.dtype), SDS((groups, chunk, chunk), F32),
                   SDS((groups, chunk), F32), SDS((1, d), F32)],
        input_output_aliases={7: 0},
        scratch_shapes=[pltpu.VMEM((tr, d), BF16), pltpu.VMEM((tr, d), BF16), pltpu.VMEM((tr, d), F32),
                        pltpu.VMEM((groups, chunk, chunk), F32)],
        compiler_params=_params(("arbitrary",)),
    )(proj, proj, proj, dya, norm_v, w_s, b_col, dproj)


def _dx(dproj, wg_in, x2d, dx2, norm_in):
    n, d = x2d.shape
    nsh = N_DEV // 2
    esh = wg_in.shape[1] // nsh
    tm = _tile(n, 1024)

    def body(dp_ref, w_ref, x_ref, dx2_ref, g_ref, gx_ref, dg_ref, acc):
        i, k = pl.program_id(0), pl.program_id(1)

        @pl.when(jnp.logical_and(i == 0, k == 0))
        def _():
            dg_ref[...] = jnp.zeros_like(dg_ref)

        @pl.when(k == 0)
        def _():
            acc[...] = jnp.zeros_like(acc)

        acc[...] += _dot_nt(dp_ref[...], w_ref[...])

        @pl.when(k == nsh - 1)
        def _():
            dh = acc[...]
            x = x_ref[...]
            r = _rms_scale(x)
            xh = x * r
            dg_ref[...] += jnp.sum(dh * xh, axis=0, keepdims=True)
            dxh = dh * g_ref[...]
            gx_ref[...] = dx2_ref[...] + r * (dxh - xh * jnp.mean(dxh * xh, axis=-1, keepdims=True))

    rows = pl.BlockSpec((tm, d), lambda i, k: (i, 0))
    vec = pl.BlockSpec((1, d), lambda i, k: (0, 0))
    return pl.pallas_call(
        body, name="dx", grid=(n // tm, nsh),
        in_specs=[pl.BlockSpec((tm, esh), lambda i, k: (i, k)),
                  pl.BlockSpec((d, esh), lambda i, k: (0, k)), rows, rows, vec],
        out_specs=[rows, vec],
        out_shape=[SDS((n, d), F32), SDS((1, d), F32)],
        scratch_shapes=[pltpu.VMEM((tm, d), F32)],
        compiler_params=_params(("arbitrary", "arbitrary")),
    )(dproj, wg_in, x2d, dx2, norm_in)


def _adamw_outputs(g_ref, d_ref, m_ref, v_ref, g, w, m, v):
    delta, m2, v2 = _adamw(w, g, m, v)
    g_ref[...] = g
    d_ref[...] = delta
    m_ref[...] = m2
    v_ref[...] = v2


def _reduce_adamw(slots, w, m, v, name, transposed=False):
    r, c = w.shape
    ns = slots.shape[0]
    tr = _tile(r, 128)

    def body(s_ref, w_ref, m_ref, v_ref, g_out, d_out, m_out, v_out):
        g = s_ref[0].astype(F32)
        for k in range(1, ns):
            g = g + s_ref[k].astype(F32)
        if transposed:
            g = g.T
        _adamw_outputs(g_out, d_out, m_out, v_out, g, w_ref[...], m_ref[...], v_ref[...])

    blk = pl.BlockSpec((tr, c), lambda i: (i, 0))
    slot_blk = (pl.BlockSpec((ns, c, tr), lambda i: (0, 0, i)) if transposed
                else pl.BlockSpec((ns, tr, c), lambda i: (0, i, 0)))
    return pl.pallas_call(
        body, name=name, grid=(r // tr,),
        in_specs=[slot_blk, blk, blk, blk],
        out_specs=[blk] * 4,
        out_shape=[SDS((r, c), F32)] * 4,
        compiler_params=_params(("parallel",)),
    )(slots, w, m, v)


def _adamw_small(g, w, m, v, name):
    def body(g_ref, w_ref, m_ref, v_ref, g_out, d_out, m_out, v_out):
        _adamw_outputs(g_out, d_out, m_out, v_out, g_ref[...], w_ref[...], m_ref[...], v_ref[...])

    return pl.pallas_call(
        body, name=name,
        out_shape=[SDS(g.shape, F32)] * 4,
        in_specs=[pl.BlockSpec(memory_space=pltpu.VMEM)] * 4,
        out_specs=[pl.BlockSpec(memory_space=pltpu.VMEM)] * 4,
    )(g, w, m, v)


def kernel(x, norm_in, w_in, norm_v, w_s, b_s, w_o_gmlp, w_o_sb, w_out, norm_final, loss_target, m_norm_in, m_w_in, m_norm_v, m_w_s, m_b_s, m_w_o_gmlp, m_w_o_sb, m_w_out, m_norm_final, v_norm_in, v_w_in, v_norm_v, v_w_s, v_b_s, v_w_o_gmlp, v_w_o_sb, v_w_out, v_norm_final):
    batch, seq, d = x.shape
    n = batch * seq
    groups, chunk = w_s.shape[1], w_s.shape[2]
    hd = LANE
    x2d = x.reshape(n, d)
    tgt = loss_target.reshape(n, d)
    b_col = b_s[0].reshape(groups, chunk, 1)
    norm_final2 = norm_final.reshape(1, d)

    my_slot = _slot(_me()).astype(jnp.int32).reshape(1)
    proj, h, wg_in, wg_oa, wg_ob, wg_out = _gather_in_proj(
        x2d, norm_in, w_in[0], [w_o_gmlp[0], w_o_sb[0], w_out[0]], my_slot)
    rsh = wg_oa.shape[1]
    wf_oa, wf_ob, wf_out = (w.reshape(N_DEV * rsh, d) for w in (wg_oa, wg_ob, wg_out))
    ya = _branch_a_fwd(proj, norm_v, w_s[0], b_col)
    yb, o, sb_tot = _sb_fwd(proj, batch, seq, d, hd)
    dproj, dx2, dya, dyb, merged, dpa, dpb, loss_vec, dgf = _tail(
        x2d, tgt, ya, yb, proj, wf_oa, wf_ob, wf_out, norm_final2)
    gp_wo = _dw_o([(ya, dpa), (yb, dpb), (merged, dx2)])
    dproj, gp_ws, gp_b, gp_nv = _branch_a_bwd(proj, dya, norm_v, w_s[0], b_col, dproj)

    slab = lambda a: a.reshape(d // LANE, LANE)
    gc = groups * chunk
    packed = jnp.concatenate([gp_ws.reshape(gc, chunk), gp_b, slab(gp_nv), slab(dgf), slab(loss_vec)], axis=0)
    dproj, s_oa, s_ob, s_out, packs = _sb_bwd(
        proj, o, dyb, sb_tot, dproj, gp_wo.reshape(3, N_DEV, rsh, d), packed, batch, seq, d, hd)
    grad_x, gp_nin = _dx(dproj, wg_in, x2d, dx2, norm_in)
    s_win, late_packs = _dw_in_exchange(h, dproj, my_slot, slab(gp_nin))
    tot, loss_slab = _finish_small(packs, late_packs, groups, chunk)
    ns = d // LANE
    g_ws = tot[:gc]
    g_b = tot[gc:gc + groups]
    g_nv, g_nf, _, g_nin = (tot[gc + groups + k * ns:gc + groups + (k + 1) * ns] for k in range(4))
    loss = loss_slab[0, 0]

    res = {}
    res["w_in"] = _reduce_adamw(s_win, w_in[0], m_w_in[0], v_w_in[0], "adamw_w_in", transposed=True)
    res["w_o_gmlp"] = _reduce_adamw(s_oa, w_o_gmlp[0], m_w_o_gmlp[0], v_w_o_gmlp[0], "adamw_w_o_gmlp")
    res["w_o_sb"] = _reduce_adamw(s_ob, w_o_sb[0], m_w_o_sb[0], v_w_o_sb[0], "adamw_w_o_sb")
    res["w_out"] = _reduce_adamw(s_out, w_out[0], m_w_out[0], v_w_out[0], "adamw_w_out")
    res["norm_in"] = _adamw_small(g_nin, slab(norm_in), slab(m_norm_in), slab(v_norm_in), "adamw_norm_in")
    res["norm_v"] = _adamw_small(g_nv, slab(norm_v), slab(m_norm_v), slab(v_norm_v), "adamw_norm_v")
    res["norm_final"] = _adamw_small(g_nf, slab(norm_final), slab(m_norm_final), slab(v_norm_final), "adamw_norm_final")
    res["w_s"] = _adamw_small(g_ws, w_s.reshape(gc, chunk), m_w_s.reshape(gc, chunk), v_w_s.reshape(gc, chunk), "adamw_w_s")
    res["b_s"] = _adamw_small(g_b, b_s[0], m_b_s[0], v_b_s[0], "adamw_b_s")

    shapes = {"norm_in": norm_in.shape, "w_in": w_in.shape, "norm_v": norm_v.shape, "w_s": w_s.shape,
              "b_s": b_s.shape, "w_o_gmlp": w_o_gmlp.shape, "w_o_sb": w_o_sb.shape, "w_out": w_out.shape,
              "norm_final": norm_final.shape}
    names = list(shapes)
    outs = [loss, grad_x.reshape(batch, seq, d)]
    for kind in range(4):
        outs += [res[name][kind].reshape(shapes[name]) for name in names]
    return tuple(outs)
```

```python
import functools
import math

import jax
import jax.numpy as jnp
from jax import lax
from jax.experimental import pallas as pl
from jax.experimental.pallas import tpu as pltpu

F32 = jnp.float32
BF16 = jnp.bfloat16
SDS = jax.ShapeDtypeStruct
MESH_ID = pl.DeviceIdType.MESH

N_DEV = 8
LANE = 128
SUBLANE = 8
VMEM_LIMIT = 56 * 1024 * 1024
SB_TILE = 512
SB_TILE_BWD = 512
SB_SCAN = 256
SB_HEADS = 2
MASKED_LOG = -1e30
RMS_EPS = 1e-6

ADAM_LR = 0.001
ADAM_B1 = 0.9
ADAM_B2 = 0.999
ADAM_EPS = 1e-08
ADAM_WD = 0.01
ADAM_STEP = 10

NT_DIMS = (((1,), (1,)), ((), ()))
TN_DIMS = (((0,), (0,)), ((), ()))


def _params(semantics=None):
    return pltpu.CompilerParams(dimension_semantics=semantics, vmem_limit_bytes=VMEM_LIMIT)


def _tile(n, preferred):
    t = min(n, preferred)
    assert n % t == 0, (n, t)
    return t


def _sigmoid(x):
    return 1.0 / (1.0 + jnp.exp(-x))


def _silu(x):
    s = _sigmoid(x)
    return x * s, s * (1.0 + x * (1.0 - s))


def _gelu(x):
    k = math.sqrt(2.0 / math.pi)
    x2 = x * x
    t = jnp.tanh(k * (x + 0.044715 * (x * x2)))
    cdf = 0.5 * (1.0 + t)
    return x * cdf, cdf + 0.5 * x * (1.0 - t * t) * (k * (1.0 + 3.0 * 0.044715 * x2))


def _rms_scale(x):
    return lax.rsqrt(jnp.mean(x * x, axis=-1, keepdims=True) + RMS_EPS)


def _iotas(n):
    return (lax.broadcasted_iota(jnp.int32, (n, n), 0), lax.broadcasted_iota(jnp.int32, (n, n), 1))


def _adamw(w, g, m, v):
    m = ADAM_B1 * m + (1.0 - ADAM_B1) * g
    v = ADAM_B2 * v + (1.0 - ADAM_B2) * (g * g)
    m_hat = m / (1.0 - ADAM_B1 ** ADAM_STEP)
    v_hat = v / (1.0 - ADAM_B2 ** ADAM_STEP)
    delta = -ADAM_LR * (m_hat / (jnp.sqrt(v_hat) + ADAM_EPS) + ADAM_WD * w)
    return delta, m, v


def _dot(a, b):
    return jnp.dot(a, b, preferred_element_type=F32)


def _dot_nt(a, b):
    return lax.dot_general(a, b, NT_DIMS, preferred_element_type=F32)


def _dot_tn(a, b):
    return lax.dot_general(a, b, TN_DIMS, preferred_element_type=F32)


def _sb_logs(raw, scale, valid):
    z = (raw * scale).astype(BF16)
    log_beta = jnp.minimum(z, 0) - jnp.log(1 + jnp.exp(-jnp.abs(z)))
    log_rest = log_beta - z
    if valid is not None:
        log_beta = jnp.where(valid, log_beta, MASKED_LOG)
        log_rest = jnp.where(valid, log_rest, 0)
    return log_beta, log_rest


def _me():
    return lax.axis_index("x"), lax.axis_index("y"), lax.axis_index("c")


def _slot(p):
    return 4 * p[0] + 2 * p[1] + p[2]


def _peer(me, k):
    flips = ((k >> 2) & 1, (k >> 1) & 1, k & 1)
    return tuple(1 - a if f else a for a, f in zip(me, flips))


def _stack_exchange(me, st_in, st_out, n_whole, send_sems, recv_sems, local_sems, arrivals=True):
    mine = _slot(me)
    ns = len(st_in)
    part = lambda a, dev: st_in[a] if a >= ns - n_whole else st_in[a].at[_slot(dev)]
    local = [pltpu.make_async_copy(part(a, me), st_out[a].at[mine], local_sems.at[a]) for a in range(ns)]
    remote, landed = [], []
    for k in range(1, N_DEV):
        peer = _peer(me, k)
        for a in range(ns):
            sems = dict(send_sem=send_sems.at[7 * a + k - 1], recv_sem=recv_sems.at[7 * a + k - 1])
            remote.append(pltpu.make_async_remote_copy(
                src_ref=part(a, peer), dst_ref=st_out[a].at[mine],
                device_id=peer, device_id_type=MESH_ID, **sems))
            if arrivals:
                got = st_out[a].at[_slot(peer)]
                landed.append(pltpu.make_async_remote_copy(
                    src_ref=got, dst_ref=got, device_id=me, device_id_type=MESH_ID, **sems))
    return local, remote, landed


def _gather_in_proj(x2d, norm_in, w_in_sh, wo_shards, my_slot):
    n, d = x2d.shape
    esh = w_in_sh.shape[1]
    pw = 2 * esh
    n_chip = N_DEV // 2
    tm = _tile(n, 1024)
    n_i = n // tm
    mid = n_i // 2
    no = len(wo_shards)
    flip_at = lambda st: jnp.where(st == 1, 2, jnp.where(st == 2, 1, jnp.where(st == 3, 3, 0)))

    def body(me_ref, x_ref, g_ref, win_ref, *refs):
        del me_ref
        wo_in = refs[:no]
        proj_ref, h_ref, wg_ref = refs[no:no + 3]
        wo_out = refs[no + 3:2 * no + 3]
        wv, stage = refs[2 * no + 3:2 * no + 5]
        wo_stage = refs[2 * no + 5:3 * no + 5]
        send_sems, recv_sems, pair_sems, own_sems, wo_send, wo_recv, wo_local = refs[3 * no + 5:]
        st, i = pl.program_id(0), pl.program_id(1)
        x, y, c = _me()
        me, sibling = (x, y, c), (x, y, 1 - c)
        chips = [(1 - x, y), (x, 1 - y), (1 - x, 1 - y)]
        chip_id = lambda p: 2 * p[0] + p[1]

        def window(chip, core):
            return wv.at[chip_id(chip), :, pl.ds(pl.multiple_of(core * esh, LANE), esh)]

        def copy(k, block, to, src=None):
            dst = window(block[:2], block[2])
            return pltpu.make_async_remote_copy(
                src_ref=dst if src is None else src, dst_ref=dst,
                send_sem=send_sems.at[k], recv_sem=recv_sems.at[k], device_id=to, device_id_type=MESH_ID)

        def wo_copy(a, k, block, to, src=None):
            dst = wo_out[a].at[_slot(block)]
            return pltpu.make_async_remote_copy(
                src_ref=dst if src is None else src, dst_ref=dst,
                send_sem=wo_send.at[7 * a + k], recv_sem=wo_recv.at[7 * a + k], device_id=to, device_id_type=MESH_ID)

        def own_copy():
            return pltpu.make_async_copy(stage, window((x, y), c), own_sems.at[0])

        def wo_own_copy(a):
            return pltpu.make_async_copy(wo_stage[a], wo_out[a].at[_slot(me)], wo_local.at[a])

        def pair_copy(step):
            chip = jnp.bitwise_xor(chip_id((x, y)), flip_at(step))
            return pltpu.make_async_copy(wv.at[chip], wg_ref.at[:, pl.ds(pl.multiple_of(chip * pw, LANE), pw)],
                                         pair_sems.at[step])

        first = jnp.logical_and(st == 0, i == 0)

        @pl.when(first)
        def _():
            stage[...] = win_ref[...].astype(BF16)
            own_copy().start()
            copy(0, me, sibling, src=stage).start()
            for j in range(2):
                copy(1 + j, me, (*chips[j], c), src=stage).start()
            own_copy().wait()
            copy(0, sibling, me).wait_recv()
            pair_copy(0).start()

        for s_ in range(n_chip - 1):
            @pl.when(jnp.logical_and(st == s_, i == mid))
            def _():
                copy(1 + s_, (*chips[s_], c), me).wait_recv()
                copy(4 + s_, (*chips[s_], c), sibling).start()
                if s_ == 0:
                    copy(3, me, (*chips[2], c), src=stage).start()
                if s_ == 1:
                    for a in range(no):
                        wo_stage[a][...] = wo_in[a][...].astype(BF16)
                        wo_own_copy(a).start()
                        wo_copy(a, 0, me, sibling, src=wo_stage[a]).start()
                        for j, chip in enumerate(chips):
                            wo_copy(a, 1 + j, me, (*chip, c), src=wo_stage[a]).start()
                if s_ == 2:
                    for a in range(no):
                        for j, chip in enumerate(chips):
                            wo_copy(a, 1 + j, (*chip, c), me).wait_recv()
                            wo_copy(a, 4 + j, (*chip, c), sibling).start()

        for s_ in range(1, n_chip):
            @pl.when(jnp.logical_and(st == s_, i == 0))
            def _():
                copy(3 + s_, (*chips[s_ - 1], 1 - c), me).wait_recv()
                pair_copy(s_).start()

        xv = x_ref[...]
        h = (xv * _rms_scale(xv) * g_ref[...]).astype(BF16)

        @pl.when(st == 0)
        def _():
            h_ref[...] = h

        chip_now = jnp.bitwise_xor(chip_id((x, y)), flip_at(st))
        proj_ref[...] = _dot(h, wv[chip_now]).astype(BF16)

        @pl.when(jnp.logical_and(st == n_chip - 1, i == n_i - 1))
        def _():
            copy(0, me, sibling, src=stage).wait_send()
            for j, chip in enumerate(chips):
                copy(1 + j, me, (*chip, c), src=stage).wait_send()
                copy(4 + j, (*chip, c), sibling).wait_send()
            for s_ in range(n_chip):
                pair_copy(s_).wait()
            for a in range(no):
                wo_copy(a, 0, me, sibling, src=wo_stage[a]).wait_send()
                wo_copy(a, 0, sibling, me).wait_recv()
                for j, chip in enumerate(chips):
                    wo_copy(a, 1 + j, me, (*chip, c), src=wo_stage[a]).wait_send()
                    wo_copy(a, 4 + j, (*chip, c), sibling).wait_send()
                    wo_copy(a, 4 + j, (*chip, 1 - c), me).wait_recv()
                wo_own_copy(a).wait()

    any_spec = pl.BlockSpec(memory_space=pl.ANY)
    vmem = pl.BlockSpec(memory_space=pltpu.VMEM)
    grid_spec = pltpu.PrefetchScalarGridSpec(
        num_scalar_prefetch=1, grid=(n_chip, n_i),
        in_specs=[pl.BlockSpec((tm, d), lambda st, i, me: (i, 0)),
                  pl.BlockSpec((1, d), lambda st, i, me: (0, 0)), vmem] + [vmem] * no,
        out_specs=[pl.BlockSpec((tm, pw), lambda st, i, me: (i, jnp.bitwise_xor(me[0] // 2, flip_at(st)))),
                   pl.BlockSpec((tm, d), lambda st, i, me: (jnp.where(st == 0, i, n_i - 1), 0)),
                   any_spec] + [any_spec] * no,
        scratch_shapes=[pltpu.VMEM((n_chip, d, pw), BF16), pltpu.VMEM((d, esh), BF16)] + [
            pltpu.VMEM(s.shape, BF16) for s in wo_shards] + [
            pltpu.SemaphoreType.DMA((7,)), pltpu.SemaphoreType.DMA((7,)),
            pltpu.SemaphoreType.DMA((n_chip,)), pltpu.SemaphoreType.DMA((1,)),
            pltpu.SemaphoreType.DMA((7 * no,)), pltpu.SemaphoreType.DMA((7 * no,)),
            pltpu.SemaphoreType.DMA((no,))])
    return pl.pallas_call(
        body, name="gather_in_proj", grid_spec=grid_spec,
        out_shape=[SDS((n, n_chip * pw), BF16), SDS((n, d), BF16), SDS((d, n_chip * pw), BF16)] + [
            SDS((N_DEV,) + s.shape, BF16) for s in wo_shards],
        compiler_params=pltpu.CompilerParams(dimension_semantics=("arbitrary", "arbitrary"),
                                             vmem_limit_bytes=VMEM_LIMIT),
    )(my_slot, x2d, norm_in, w_in_sh, *wo_shards)


N_CHIP = N_DEV // 2
CHIP_FLIPS = (3, 2, 1, 0)


def _owner_at(mine, j):
    flip = 0
    for pair, f in enumerate(CHIP_FLIPS):
        flip = jnp.where(j // 2 == pair, f, flip)
    return 2 * jnp.bitwise_xor(mine // 2, flip) + j % 2


def _dw_in_exchange(h, dproj, my_slot, packed):
    n, d = h.shape
    esh = dproj.shape[1] // N_DEV
    tk = _tile(n, 2048)
    nk = n // tk
    last_j = N_DEV - 1

    def body(me_ref, h_ref, dp_ref, pk_in, win_out, pk_out,
             acc, halfbuf, recvbuf, sendbuf, half_send, half_recv, win_send, win_recv,
             send_sems, recv_sems, local_sems):
        del me_ref
        j, k = pl.program_id(0), pl.program_id(1)
        x, y, c = _me()
        me, sibling = (x, y, c), (x, y, 1 - c)
        mine = _slot(me)
        my_chip = mine // 2

        def pack_copies():
            local = pltpu.make_async_copy(pk_in, pk_out.at[mine], local_sems.at[0])
            remote = [pltpu.make_async_remote_copy(
                src_ref=pk_in, dst_ref=pk_out.at[mine], send_sem=send_sems.at[kk - 1], recv_sem=recv_sems.at[kk - 1],
                device_id=_peer(me, kk), device_id_type=MESH_ID) for kk in range(1, N_DEV)]
            return local, remote

        def half_copy(jj):
            slot = (jj // 2) % 2
            return pltpu.make_async_remote_copy(
                src_ref=halfbuf.at[slot], dst_ref=recvbuf.at[slot],
                send_sem=half_send.at[slot], recv_sem=half_recv.at[slot],
                device_id=sibling, device_id_type=MESH_ID)

        def chip_copy(jj):
            slot = (jj // 2) % 2
            owner = _owner_at(mine, jj)
            return pltpu.make_async_remote_copy(
                src_ref=sendbuf.at[slot], dst_ref=win_out.at[my_chip],
                send_sem=win_send.at[slot], recv_sem=win_recv.at[my_chip],
                device_id=(owner // 4, (owner // 2) % 2, owner % 2), device_id_type=MESH_ID)

        def own_copy():
            return pltpu.make_async_copy(sendbuf.at[(last_j // 2) % 2], win_out.at[my_chip], local_sems.at[1])

        @pl.when(jnp.logical_and(j == 0, k == 0))
        def _():
            local, remote = pack_copies()
            for cp in [local] + remote:
                cp.start()

        @pl.when(k == 0)
        def _():
            acc[...] = jnp.zeros_like(acc)

        acc[...] += _dot_tn(dp_ref[...], h_ref[...])

        done = k == nk - 1
        combine = j % 2 == c
        slot = (j // 2) % 2

        @pl.when(jnp.logical_and(done, jnp.logical_not(combine)))
        def _():
            @pl.when(j >= 4)
            def _():
                half_copy(j - 4).wait_send()

            halfbuf[slot] = acc[...].astype(BF16)
            half_copy(j).start()

        @pl.when(jnp.logical_and(done, combine))
        def _():
            half_copy(j).wait_recv()

            @pl.when(j >= 4)
            def _():
                chip_copy(j - 4).wait_send()

            sendbuf[slot] = (acc[...] + recvbuf[slot].astype(F32)).astype(BF16)

            @pl.when(j < last_j - 1)
            def _():
                chip_copy(j).start()

            @pl.when(j >= last_j - 1)
            def _():
                own_copy().start()

        @pl.when(jnp.logical_and(j == last_j, done))
        def _():
            half_copy(5 - c).wait_send()
            half_copy(7 - c).wait_send()
            chip_copy(4 + c).wait_send()
            own_copy().wait()
            for chip in range(N_CHIP):
                @pl.when(chip != my_chip)
                def _():
                    landed = win_out.at[chip]
                    pltpu.make_async_remote_copy(
                        src_ref=landed, dst_ref=landed, send_sem=win_send.at[0], recv_sem=win_recv.at[chip],
                        device_id=me, device_id_type=MESH_ID).wait_recv()
            local, remote = pack_copies()
            for cp in remote:
                cp.wait_send()
            for kk in range(1, N_DEV):
                landed = pk_out.at[_slot(_peer(me, kk))]
                pltpu.make_async_remote_copy(
                    src_ref=landed, dst_ref=landed, send_sem=send_sems.at[kk - 1], recv_sem=recv_sems.at[kk - 1],
                    device_id=me, device_id_type=MESH_ID).wait_recv()
            local.wait()

    any_spec = pl.BlockSpec(memory_space=pl.ANY)
    grid_spec = pltpu.PrefetchScalarGridSpec(
        num_scalar_prefetch=1, grid=(N_DEV, nk),
        in_specs=[pl.BlockSpec((tk, d), lambda j, k, me: (k, 0)),
                  pl.BlockSpec((tk, esh), lambda j, k, me: (k, _owner_at(me[0], j))), any_spec],
        out_specs=[any_spec] * 2,
        scratch_shapes=[pltpu.VMEM((esh, d), F32)] + [pltpu.VMEM((2, esh, d), BF16)] * 3 + [
            pltpu.SemaphoreType.DMA((2,)), pltpu.SemaphoreType.DMA((2,)),
            pltpu.SemaphoreType.DMA((2,)), pltpu.SemaphoreType.DMA((N_CHIP,)),
            pltpu.SemaphoreType.DMA((N_DEV - 1,)), pltpu.SemaphoreType.DMA((N_DEV - 1,)),
            pltpu.SemaphoreType.DMA((2,))])
    return pl.pallas_call(
        body, name="dw_in_exchange", grid_spec=grid_spec,
        out_shape=[SDS((N_CHIP, esh, d), BF16), SDS((N_DEV,) + packed.shape, packed.dtype)],
        compiler_params=_params(("arbitrary", "arbitrary")),
    )(my_slot, h, dproj, packed)


def _finish_small(packs, late_packs, groups, chunk):
    rows = packs.shape[1]
    late = late_packs.shape[1]
    gc = groups * chunk

    def body(p_ref, l_ref, sum_ref, loss_ref):
        row, col = _iotas(chunk)
        tril = col <= row
        for g in range(groups):
            rs = slice(g * chunk, (g + 1) * chunk)
            tot = p_ref[0, rs, :]
            for dev in range(1, N_DEV):
                tot = tot + p_ref[dev, rs, :]
            sum_ref[rs, :] = jnp.where(tril, tot, 0.0)
        rs = slice(gc, rows)
        tot = p_ref[0, rs, :]
        for dev in range(1, N_DEV):
            tot = tot + p_ref[dev, rs, :]
        sum_ref[rs, :] = tot
        loss_ref[...] = jnp.full((SUBLANE, LANE), jnp.sum(tot[rows - gc - SUBLANE:, :]), F32)
        tot = l_ref[0]
        for dev in range(1, N_DEV):
            tot = tot + l_ref[dev]
        sum_ref[rows:rows + late, :] = tot

    return pl.pallas_call(
        body, name="finish_small",
        out_shape=[SDS((rows + late, LANE), F32), SDS((SUBLANE, LANE), F32)],
        in_specs=[pl.BlockSpec(memory_space=pltpu.VMEM)] * 2,
        out_specs=[pl.BlockSpec(memory_space=pltpu.VMEM)] * 2,
        compiler_params=pltpu.CompilerParams(vmem_limit_bytes=VMEM_LIMIT),
    )(packs, late_packs)


def _branch_a_fwd(proj, norm_v, w_s, b_col):
    n = proj.shape[0]
    d = norm_v.shape[1]
    groups, chunk, _ = w_s.shape
    tr = _tile(n, 4 * chunk)

    def body(u_ref, v_ref, z_ref, gv_ref, ws_ref, b_ref, ya_ref, vn_s, pre_s):
        row, col = _iotas(chunk)
        tril = col <= row
        vg = _gelu(v_ref[...])[0].astype(F32)
        vn_s[...] = (vg * _rms_scale(vg) * gv_ref[...]).astype(BF16)
        pre_s[...] = _gelu(u_ref[...])[0] * _silu(z_ref[...])[0]
        for g in range(groups):
            wm = jnp.where(tril, ws_ref[g], 0.0).astype(BF16)
            cs = slice(g * chunk, (g + 1) * chunk)
            for c in range(tr // chunk):
                rs = slice(c * chunk, (c + 1) * chunk)
                mixed = _dot(wm, vn_s[rs, cs]) + b_ref[g]
                ya_ref[rs, cs] = (pre_s[rs, cs].astype(F32) * mixed).astype(BF16)

    seg = lambda k: pl.BlockSpec((tr, d), lambda i: (i, k))
    return pl.pallas_call(
        body, name="branch_a_fwd", grid=(n // tr,),
        in_specs=[seg(0), seg(1), seg(2),
                  pl.BlockSpec((1, d), lambda i: (0, 0)),
                  pl.BlockSpec((groups, chunk, chunk), lambda i: (0, 0, 0)),
                  pl.BlockSpec((groups, chunk, 1), lambda i: (0, 0, 0))],
        out_specs=pl.BlockSpec((tr, d), lambda i: (i, 0)),
        out_shape=SDS((n, d), BF16),
        scratch_shapes=[pltpu.VMEM((tr, d), BF16), pltpu.VMEM((tr, d), BF16)],
        compiler_params=_params(("parallel",)),
    )(proj, proj, proj, norm_v, w_s, b_col)


def _sb_fwd(proj, batch, seq, d, hd):
    heads = d // hd
    t = _tile(seq, SB_TILE)
    sw = _tile(t, SB_SCAN)
    nb = t // sw
    scale = hd ** -0.5
    nblk = seq // t
    nh = SB_HEADS
    wide = nh * hd
    cols = [slice(hh * hd, (hh + 1) * hd) for hh in range(nh)]

    def body(qs, k_ref, vs, zb_ref, yb_ref, o_ref, tot_ref, kts, later, acc):
        for jb in range(nblk):
            kts[jb] = k_ref[jb * t:(jb + 1) * t, :].T
        row, col = _iotas(t)
        later[...] = (row[:sw, :sw] > col[:sw, :sw]).astype(BF16)

        def qblock(i, carry):
            r0 = pl.multiple_of(i * t, t)

            def tile(j, runs):
                c0 = pl.multiple_of(j * t, t)
                logs = [_sb_logs(_dot(qs[pl.ds(r0, t), cs], kts[j, cs, :]), scale, None) for cs in cols]
                scans = [_dot(jnp.concatenate([logs[hh][1][:, b * sw:(b + 1) * sw] for b in range(nb)], axis=0),
                              later[...]) for hh in range(nh)]
                new_runs = []
                for hh in range(nh):
                    after = runs[hh]
                    blocks = [None] * nb
                    for b in reversed(range(nb)):
                        ks_ = slice(b * sw, (b + 1) * sw)
                        inside = scans[hh][b * t:(b + 1) * t]
                        blocks[b] = jnp.exp(logs[hh][0][:, ks_].astype(F32) + inside + after).astype(BF16)
                        after = after + inside[:, 0:1] + logs[hh][1][:, b * sw:b * sw + 1].astype(F32)
                    new_runs.append(after)
                    acc[:, cols[hh]] += _dot(jnp.concatenate(blocks, axis=1), vs[pl.ds(c0, t), cols[hh]])
                return tuple(new_runs)

            def diagonal_tile():
                starts = [b * sw for b in range(nb)]
                logs = [[_sb_logs(_dot(qs[pl.ds(r0 + s, t - s), cs], kts[i, cs, s:s + sw]), scale,
                                  col[:t - s, :sw] < row[:t - s, :sw]) for s in starts] for cs in cols]
                scans = [_dot(jnp.concatenate([lr for _, lr in logs[hh]], axis=0), later[...]) for hh in range(nh)]
                new_runs = []
                offs = [sum(t - s for s in starts[:b]) for b in range(nb)]
                for hh in range(nh):
                    after = jnp.zeros((t, 1), F32)
                    ws = [None] * nb
                    for b in reversed(range(nb)):
                        s = starts[b]
                        lb, lr = logs[hh][b]
                        inside = scans[hh][offs[b]:offs[b] + t - s]
                        ws[b] = jnp.exp(lb.astype(F32) + inside + after[s:]).astype(BF16)
                        total = inside[:, 0:1] + lr[:, 0:1].astype(F32)
                        after = after + total if s == 0 else jnp.concatenate([after[:s], after[s:] + total], axis=0)
                    new_runs.append(after)
                    acc[:, cols[hh]] = _dot(ws[0], vs[pl.ds(r0, sw), cols[hh]])
                    for b in range(1, nb):
                        acc[starts[b]:, cols[hh]] += _dot(ws[b], vs[pl.ds(r0 + starts[b], sw), cols[hh]])
                return tuple(new_runs)

            runs = diagonal_tile()
            runs = lax.fori_loop(0, i, lambda jj, rs: tile(i - 1 - jj, rs), runs)
            for hh in range(nh):
                out = acc[:, cols[hh]]
                o_ref[pl.ds(r0, t), cols[hh]] = out.astype(BF16)
                tot_ref[hh, pl.ds(r0, t), :] = runs[hh]
                sz, _ = _silu(zb_ref[pl.ds(r0, t), cols[hh]].astype(F32))
                yb_ref[pl.ds(r0, t), cols[hh]] = (out * sz).astype(BF16)
            return carry

        lax.fori_loop(0, nblk, qblock, 0)

    col0 = d // wide
    seg = lambda k: pl.BlockSpec((seq, wide), lambda b, h: (b, k * col0 + h))
    return pl.pallas_call(
        body, name="sb_fwd", grid=(batch, heads // nh),
        in_specs=[seg(3), seg(4), seg(5), seg(6)],
        out_specs=[pl.BlockSpec((seq, wide), lambda b, h: (b, h))] * 2 + [
            pl.BlockSpec((nh, seq, 1), lambda b, h: (b * (heads // nh) + h, 0, 0))],
        out_shape=[SDS((batch * seq, d), BF16), SDS((batch * seq, d), BF16), SDS((batch * heads, seq, 1), F32)],
        scratch_shapes=[pltpu.VMEM((nblk, wide, t), BF16), pltpu.VMEM((sw, sw), BF16), pltpu.VMEM((t, wide), F32)],
        compiler_params=_params(("parallel", "parallel")),
    )(proj, proj, proj, proj)


def _tail(x2d, tgt, ya, yb, proj, w_oa, w_ob, w_out, norm_final):
    n, d = x2d.shape
    e = proj.shape[1]
    tm = _tile(n, 512)
    steps = n // tm

    def body(x_ref, t_ref, ya_ref, yb_ref, ga_ref, gb_ref, woa_ref, wob_ref, wout_ref, gf_ref,
             dproj_ref, dx2_ref, dya_ref, dyb_ref, mrg_ref, dpa_ref, dpb_ref, loss_ref, dgf_ref, dg_s, dg_sems):
        i = pl.program_id(0)

        def gate_copy(step):
            rows_ = pl.ds(pl.multiple_of(step * tm, tm), tm)
            return pltpu.make_async_copy(dg_s.at[step % 2], dproj_ref.at[rows_, pl.ds(7 * d, 2 * d)],
                                         dg_sems.at[step % 2])

        @pl.when(i == 0)
        def _():
            loss_ref[...] = jnp.zeros_like(loss_ref)
            dgf_ref[...] = jnp.zeros_like(dgf_ref)

        @pl.when(i >= 2)
        def _():
            gate_copy(i - 2).wait()

        pa = _dot(ya_ref[...], woa_ref[...])
        pb = _dot(yb_ref[...], wob_ref[...])
        sa = _sigmoid(ga_ref[...].astype(F32))
        sb = _sigmoid(gb_ref[...].astype(F32))
        merged = (sa * pa + sb * pb).astype(BF16)
        mrg_ref[...] = merged
        x2 =x_ref[...] + _dot(merged, wout_ref[...])
        r2 = _rms_scale(x2)
        xh = x2 * r2
        gf = gf_ref[...]
        diff = xh * gf - t_ref[...]
        loss_ref[...] += jnp.sum(diff * diff, axis=0, keepdims=True) * (0.5 / d)
        dy = diff * (1.0 / d)
        dgf_ref[...] += jnp.sum(dy * xh, axis=0, keepdims=True)
        dxh = dy * gf
        dx2 = r2 * (dxh - xh * jnp.mean(dxh * xh, axis=-1, keepdims=True))
        dx2_ref[...] = dx2
        dm = _dot_nt(dx2.astype(BF16), wout_ref[...])
        dpa = (dm * sa).astype(BF16)
        dpb = (dm * sb).astype(BF16)
        dpa_ref[...] = dpa
        dpb_ref[...] = dpb
        dg_s[i % 2, :, 0:d] = (dm * pa * (sa * (1.0 - sa))).astype(BF16)
        dg_s[i % 2, :, d:2 * d] = (dm * pb * (sb * (1.0 - sb))).astype(BF16)
        gate_copy(i).start()
        dya_ref[...] = _dot_nt(dpa, woa_ref[...]).astype(BF16)
        dyb_ref[...] = _dot_nt(dpb, wob_ref[...]).astype(BF16)

        @pl.when(i == steps - 1)
        def _():
            if steps >= 2:
                gate_copy(i - 1).wait()
            gate_copy(i).wait()

    rows = lambda k=0: pl.BlockSpec((tm, d), lambda i: (i, k))
    full = pl.BlockSpec((d, d), lambda i: (0, 0), pipeline_mode=pl.Buffered(1))
    vec = pl.BlockSpec((1, d), lambda i: (0, 0))
    return pl.pallas_call(
        body, name="tail", grid=(steps,),
        in_specs=[rows(), rows(), rows(), rows(), rows(7), rows(8), full, full, full, vec],
        out_specs=[pl.BlockSpec(memory_space=pl.ANY),
                   rows(), rows(), rows(), rows(), rows(), rows(), vec, vec],
        out_shape=[SDS((n, e), BF16), SDS((n, d), F32), SDS((n, d), BF16), SDS((n, d), BF16),
                   SDS((n, d), BF16), SDS((n, d), BF16), SDS((n, d), BF16),
                   SDS((1, d), F32), SDS((1, d), F32)],
        scratch_shapes=[pltpu.VMEM((2, tm, 2 * d), BF16), pltpu.SemaphoreType.DMA((2,))],
        compiler_params=_params(("arbitrary",)),
    )(x2d, tgt, ya, yb, proj, proj, w_oa, w_ob, w_out, norm_final)


def _dw_o(pairs):
    n, d = pairs[0][0].shape
    tk = _tile(n, 1024)
    nk = n // tk
    npair = len(pairs)

    def body(*refs):
        a_refs, b_refs = refs[:npair], refs[npair:2 * npair]
        o_ref, acc = refs[2 * npair], refs[2 * npair + 1]
        p, k = pl.program_id(0), pl.program_id(1)

        @pl.when(k == 0)
        def _():
            acc[...] = jnp.zeros_like(acc)

        for q in range(npair):
            @pl.when(p == q)
            def _():
                acc[...] += _dot_tn(a_refs[q][...], b_refs[q][...].astype(BF16))

        @pl.when(k == nk - 1)
        def _():
            o_ref[0] = acc[...].astype(BF16)

    def tiles(q):
        return pl.BlockSpec((tk, d), lambda p, k: (jnp.where(p == q, k, jnp.where(p < q, 0, nk - 1)), 0))

    return pl.pallas_call(
        body, name="dw_o", grid=(npair, nk),
        in_specs=[tiles(q) for q in range(npair)] * 2,
        out_specs=pl.BlockSpec((1, d, d), lambda p, k: (p, 0, 0)),
        out_shape=SDS((npair, d, d), BF16),
        scratch_shapes=[pltpu.VMEM((d, d), F32)],
        compiler_params=_params(("arbitrary", "arbitrary")),
    )(*[a for a, _ in pairs], *[b for _, b in pairs])


def _sb_bwd(proj, o, dyb, tot, dproj, dw_stack, packed, batch, seq, d, hd):
    heads = d // hd
    t = _tile(seq, SB_TILE_BWD)
    sw = _tile(t, SB_SCAN)
    nb = t // sw
    scale = hd ** -0.5
    nblk = seq // t
    nh = SB_HEADS
    wide = nh * hd
    hs = range(nh)
    cols = [slice(hh * hd, (hh + 1) * hd) for hh in hs]
    blocks = [slice(b * sw, (b + 1) * sw) for b in range(nb)]
    last = slice(sw - 1, sw)

    def compute(qs, ks, v_ref, zb_ref, o_ref, dyb_ref, tot_ref, kts, vts, dos, dzb, dq_all, dkv_t, qt_s, dot_s,
                upto, before, dq):
        for jb in range(nblk):
            rows = slice(jb * t, (jb + 1) * t)
            kts[jb] = ks[rows, :].T
            vts[jb] = v_ref[rows, :].T
        sz, dsz = _silu(zb_ref[...])
        dyb_v = dyb_ref[...]
        dos[...] = dyb_v * sz
        dzb[...] = dyb_v * o_ref[...] * dsz
        row, col = _iotas(t)
        upto[...] = (row[:sw, :sw] <= col[:sw, :sw]).astype(BF16)
        before[...] = (row[:sw, :sw] < col[:sw, :sw]).astype(BF16)

        def qblock(i, carry):
            r0 = pl.multiple_of(i * t, t)

            def tile(j, sums):
                c0 = pl.multiple_of(j * t, t)
                q_i = [qs[pl.ds(r0, t), cs] for cs in cols]
                do_i = [dos[pl.ds(r0, t), cs] for cs in cols]
                logs = [_sb_logs(_dot(q_i[hh], kts[j, cols[hh], :]), scale, None) for hh in hs]
                scans = [_dot(jnp.concatenate([logs[hh][1][:, ks_] for ks_ in blocks], axis=0), upto[...]) for hh in hs]
                dw = [_dot(do_i[hh], vts[j, cols[hh], :]) for hh in hs]
                ws, gs, new_runs = [], [], []
                for hh in hs:
                    left = tot_ref[hh, pl.ds(r0, t), :] - sums[hh][0]
                    w_b, g_b = [], []
                    for b, ks_ in enumerate(blocks):
                        inside = scans[hh][b * t:(b + 1) * t]
                        w = jnp.exp(logs[hh][0][:, ks_].astype(F32) + (left - inside))
                        w_b.append(w.astype(BF16))
                        g_b.append((dw[hh][:, ks_] * w).astype(BF16))
                        left = left - inside[:, last]
                    ws.append(jnp.concatenate(w_b, axis=1))
                    gs.append(g_b)
                    new_runs.append(tot_ref[hh, pl.ds(r0, t), :] - left)
                gscans = [_dot(jnp.concatenate(gs[hh], axis=0), before[...]) for hh in hs]
                dzs, new_gruns = [], []
                for hh in hs:
                    g_before = sums[hh][1]
                    dz_b = []
                    for b, ks_ in enumerate(blocks):
                        inside = gscans[hh][b * t:(b + 1) * t]
                        beta = jnp.exp(logs[hh][0][:, ks_]).astype(F32)
                        g = gs[hh][b].astype(F32)
                        dz_b.append((g - (g + inside + g_before) * beta).astype(BF16))
                        g_before = g_before + inside[:, last] + g[:, last]
                    dzs.append(jnp.concatenate(dz_b, axis=1))
                    new_gruns.append(g_before)
                for hh in hs:
                    dkv_t[1, j, cols[hh], :] += _dot(dot_s[cols[hh], :], ws[hh])
                for hh in hs:
                    dkv_t[0, j, cols[hh], :] += _dot(qt_s[cols[hh], :], dzs[hh])
                for hh in hs:
                    dq[:, cols[hh]] += _dot(dzs[hh], ks[pl.ds(c0, t), cols[hh]])
                return tuple((new_runs[hh], new_gruns[hh]) for hh in hs)

            def diagonal_tile(sums):
                starts = [b * sw for b in range(nb)]
                offs = [sum(t - s for s in starts[:b]) for b in range(nb)]
                q_b = [[qs[pl.ds(r0 + s, t - s), cs] for s in starts] for cs in cols]
                do_b = [[dos[pl.ds(r0 + s, t - s), cs] for s in starts] for cs in cols]
                logs = [[_sb_logs(_dot(q_b[hh][b], kts[i, cols[hh], s:s + sw]), scale,
                                  col[:t - s, :sw] < row[:t - s, :sw]) for b, s in enumerate(starts)] for hh in hs]
                dw = [[_dot(do_b[hh][b], vts[i, cols[hh], s:s + sw]) for b, s in enumerate(starts)] for hh in hs]
                scans = [_dot(jnp.concatenate([lr for _, lr in logs[hh]], axis=0), upto[...]) for hh in hs]
                ws, gs = [], []
                for hh in hs:
                    left = tot_ref[hh, pl.ds(r0, t), :] - sums[hh][0]
                    w_b, g_b = [], []
                    for b, s in enumerate(starts):
                        inside = scans[hh][offs[b]:offs[b] + t - s]
                        w = jnp.exp(logs[hh][b][0].astype(F32) + (left[s:] - inside))
                        w_b.append(w.astype(BF16))
                        g_b.append((dw[hh][b] * w).astype(BF16))
                        total = inside[:, last]
                        left = left - total if s == 0 else jnp.concatenate([left[:s], left[s:] - total], axis=0)
                    ws.append(w_b)
                    gs.append(g_b)
                gscans = [_dot(jnp.concatenate(gs[hh], axis=0), before[...]) for hh in hs]
                dzs = []
                for hh in hs:
                    g_before = sums[hh][1]
                    dz_b = []
                    for b, s in enumerate(starts):
                        inside = gscans[hh][offs[b]:offs[b] + t - s]
                        beta = jnp.exp(logs[hh][b][0]).astype(F32)
                        g = gs[hh][b].astype(F32)
                        dz_b.append((g - (g + inside + g_before[s:]) * beta).astype(BF16))
                        total = inside[:, last] + g[:, last]
                        g_before = g_before + total if s == 0 else jnp.concatenate(
                            [g_before[:s], g_before[s:] + total], axis=0)
                    dzs.append(dz_b)
                for hh in hs:
                    for b, s in enumerate(starts):
                        dkv_t[1, i, cols[hh], s:s + sw] = _dot(dot_s[cols[hh], s:], ws[hh][b])
                for hh in hs:
                    for b, s in enumerate(starts):
                        dkv_t[0, i, cols[hh], s:s + sw] = _dot(qt_s[cols[hh], s:], dzs[hh][b])
                for hh in hs:
                    for b, s in enumerate(starts):
                        dq[s:, cols[hh]] += _dot(dzs[hh][b], ks[pl.ds(r0 + s, sw), cols[hh]])

            qt_s[...] = qs[pl.ds(r0, t), :].T
            dot_s[...] = dos[pl.ds(r0, t), :].T
            zero = jnp.zeros((t, 1), F32)
            dq[...] = jnp.zeros_like(dq)
            sums = lax.fori_loop(0, i, tile, ((zero, zero),) * nh)
            diagonal_tile(sums)
            dq_all[pl.ds(r0, t), :] = dq[...]
            return carry

        lax.fori_loop(0, nblk, qblock, 0)

    pairs = heads // nh

    nst = dw_stack.shape[0]
    ns = nst + 1

    def body(qs, ks, v_ref, zb_ref, o_ref, dyb_ref, tot_ref, dproj_in, dw_ref, pk_ref, out_ref, *refs):
        del dproj_in
        st_in = [dw_ref.at[k] for k in range(nst)] + [pk_ref]
        st_out = refs[:ns]
        (kts, vts, dos, dzb, dq_all, dkv_t, qt_s, dot_s, upto, before, dq, stage, stage_sems,
         send_sems, recv_sems, local_sems) = refs[ns:]
        step = pl.program_id(0) * pairs + pl.program_id(1)
        exchange = functools.partial(_stack_exchange, _me(), st_in, st_out, 1, send_sems, recv_sems, local_sems)

        @pl.when(step == 0)
        def _():
            local, remote, _ = exchange(arrivals=False)
            for cp in local + remote:
                cp.start()

        def out_copies(s):
            rows_ = pl.ds(pl.multiple_of((s // pairs) * seq, seq), seq)
            return [pltpu.make_async_copy(
                stage.at[k], out_ref.at[rows_, pl.ds(pl.multiple_of((3 + k) * d + (s % pairs) * wide, wide), wide)],
                stage_sems.at[k]) for k in range(4)]

        compute(qs, ks, v_ref, zb_ref, o_ref, dyb_ref, tot_ref, kts, vts, dos, dzb, dq_all, dkv_t, qt_s, dot_s,
                upto, before, dq)

        @pl.when(step > 0)
        def _():
            for cp in out_copies(step - 1):
                cp.wait()

        stage[0] = (dq_all[...] * scale).astype(BF16)
        for jb in range(nblk):
            stage[1, jb * t:(jb + 1) * t, :] = (dkv_t[0, jb] * scale).astype(BF16).T
            stage[2, jb * t:(jb + 1) * t, :] = dkv_t[1, jb].astype(BF16).T
        stage[3] = dzb[...]
        for cp in out_copies(step):
            cp.start()

        @pl.when(step == batch * pairs - 1)
        def _():
            for cp in out_copies(step):
                cp.wait()
            local, remote, landed = exchange()
            for cp in remote:
                cp.wait_send()
            for cp in landed:
                cp.wait_recv()
            for cp in local:
                cp.wait()

    col0 = d // wide
    seg = lambda k: pl.BlockSpec((seq, wide), lambda b, h: (b, k * col0 + h))
    head = pl.BlockSpec((seq, wide), lambda b, h: (b, h))
    any_spec = pl.BlockSpec(memory_space=pl.ANY)
    return pl.pallas_call(
        body, name="sb_bwd", grid=(batch, pairs),
        in_specs=[seg(3), seg(4), seg(5), seg(6), head, head,
                  pl.BlockSpec((nh, seq, 1), lambda b, h: (b * pairs + h, 0, 0))] + [any_spec] * 3,
        out_specs=[any_spec] * (ns + 1),
        out_shape=[SDS(dproj.shape, dproj.dtype)] + [SDS(dw_stack.shape[1:], dw_stack.dtype)] * nst + [
            SDS((N_DEV,) + packed.shape, packed.dtype)],
        input_output_aliases={7: 0},
        scratch_shapes=[pltpu.VMEM((nblk, wide, t), BF16)] * 2 + [
            pltpu.VMEM((seq, wide), BF16), pltpu.VMEM((seq, wide), BF16),
            pltpu.VMEM((seq, wide), F32), pltpu.VMEM((2, nblk, wide, t), F32),
            pltpu.VMEM((wide, t), BF16), pltpu.VMEM((wide, t), BF16),
            pltpu.VMEM((sw, sw), BF16), pltpu.VMEM((sw, sw), BF16), pltpu.VMEM((t, wide), F32),
            pltpu.VMEM((4, seq, wide), BF16), pltpu.SemaphoreType.DMA((4,)),
            pltpu.SemaphoreType.DMA((7 * ns,)), pltpu.SemaphoreType.DMA((7 * ns,)),
            pltpu.SemaphoreType.DMA((ns,))],
        compiler_params=_params(("arbitrary", "arbitrary")),
    )(proj, proj, proj, proj, o, dyb, tot, dproj, dw_stack, packed)


def _branch_a_bwd(proj, dya, norm_v, w_s, b_col, dproj):
    n = proj.shape[0]
    d = norm_v.shape[1]
    groups, chunk, _ = w_s.shape
    tr = _tile(n, 2 * chunk)

    def body(u_ref, v_ref, z_ref, dya_ref, gv_ref, ws_ref, b_ref, dproj_in,
             out_ref, dws_ref, dbias_ref, dgv_ref, vn_s, dmix_s, dvn_s, db_ref):
        del dproj_in

        @pl.when(pl.program_id(0) == 0)
        def _():
            dws_ref[...] = jnp.zeros_like(dws_ref)
            db_ref[...] = jnp.zeros_like(db_ref)
            dgv_ref[...] = jnp.zeros_like(dgv_ref)

        row, col = _iotas(chunk)
        tril = col <= row
        gv = gv_ref[...]
        vg16, dvg_dv = _gelu(v_ref[...])
        vg = vg16.astype(F32)
        r = _rms_scale(vg)
        vh = vg * r
        vn_s[...] = (vh * gv).astype(BF16)
        ug, dug_du = _gelu(u_ref[...])
        sz, dsz = _silu(z_ref[...])
        dya_v = dya_ref[...]
        dmix_s[...] = dya_v * ug * sz
        du_scale = sz * dug_du
        dz_scale = ug * dsz
        for g in range(groups):
            wm = jnp.where(tril, ws_ref[g], 0.0).astype(BF16)
            cs = slice(g * chunk, (g + 1) * chunk)
            for c in range(tr // chunk):
                rs = slice(c * chunk, (c + 1) * chunk)
                vn = vn_s[rs, cs]
                mixed = _dot(wm, vn) + b_ref[g]
                dmix16 = dmix_s[rs, cs]
                dws_ref[g] += _dot_nt(dmix16, vn)
                db_ref[g] += dmix16.astype(F32)
                dvn_s[rs, cs] = _dot_tn(wm, dmix16)
                t_u = dya_v[rs, cs] * mixed.astype(BF16)
                out_ref[rs, g * chunk:(g + 1) * chunk] = t_u * du_scale[rs, cs]
                out_ref[rs, 2 * d + g * chunk:2 * d + (g + 1) * chunk] = t_u * dz_scale[rs, cs]
        dvn = dvn_s[...]
        dgv_ref[...] += jnp.sum(dvn * vh, axis=0, keepdims=True)
        dvh = dvn * gv
        dvg = r * (dvh - vh * jnp.mean(dvh * vh, axis=-1, keepdims=True))
        out_ref[:, d:2 * d] = (dvg * dvg_dv.astype(F32)).astype(BF16)

        @pl.when(pl.program_id(0) == n // tr - 1)
        def _():
            for g in range(groups):
                dbias_ref[g:g + 1, :] = jnp.sum(db_ref[g].T, axis=0, keepdims=True)

    seg = lambda k: pl.BlockSpec((tr, d), lambda i: (i, k))
    return pl.pallas_call(
        body, name="branch_a_bwd", grid=(n // tr,),
        in_specs=[seg(0), seg(1), seg(2), seg(0),
                  pl.BlockSpec((1, d), lambda i: (0, 0)),
                  pl.BlockSpec((groups, chunk, chunk), lambda i: (0, 0, 0)),
                  pl.BlockSpec((groups, chunk, 1), lambda i: (0, 0, 0)),
                  pl.BlockSpec(memory_space=pl.ANY)],
        out_specs=[pl.BlockSpec((tr, 3 * d), lambda i: (i, 0)),
                   pl.BlockSpec((groups, chunk, chunk), lambda i: (0, 0, 0)),
                   pl.BlockSpec((groups, chunk), lambda i: (0, 0)),
                   pl.BlockSpec((1, d), lambda i: (0, 0))],
        out_shape=[SDS(dproj.shape, dproj.dtype), SDS((groups, chunk, chunk), F32),
                   SDS((groups, chunk), F32), SDS((1, d), F32)],
        input_output_aliases={7: 0},
        scratch_shapes=[pltpu.VMEM((tr, d), BF16), pltpu.VMEM((tr, d), BF16), pltpu.VMEM((tr, d), F32),
                        pltpu.VMEM((groups, chunk, chunk), F32)],
        compiler_params=_params(("arbitrary",)),
    )(proj, proj, proj, dya, norm_v, w_s, b_col, dproj)


def _dx(dproj, wg_in, x2d, dx2, norm_in):
    n, d = x2d.shape
    nsh = N_DEV // 2
    esh = wg_in.shape[1] // nsh
    tm = _tile(n, 1024)

    def body(dp_ref, w_ref, x_ref, dx2_ref, g_ref, gx_ref, dg_ref, acc):
        i, k = pl.program_id(0), pl.program_id(1)

        @pl.when(jnp.logical_and(i == 0, k == 0))
        def _():
            dg_ref[...] = jnp.zeros_like(dg_ref)

        @pl.when(k == 0)
        def _():
            acc[...] = jnp.zeros_like(acc)

        acc[...] += _dot_nt(dp_ref[...], w_ref[...])

        @pl.when(k == nsh - 1)
        def _():
            dh = acc[...]
            x = x_ref[...]
            r = _rms_scale(x)
            xh = x * r
            dg_ref[...] += jnp.sum(dh * xh, axis=0, keepdims=True)
            dxh = dh * g_ref[...]
            gx_ref[...] = dx2_ref[...] + r * (dxh - xh * jnp.mean(dxh * xh, axis=-1, keepdims=True))

    rows = pl.BlockSpec((tm, d), lambda i, k: (i, 0))
    vec = pl.BlockSpec((1, d), lambda i, k: (0, 0))
    return pl.pallas_call(
        body, name="dx", grid=(n // tm, nsh),
        in_specs=[pl.BlockSpec((tm, esh), lambda i, k: (i, k)),
                  pl.BlockSpec((d, esh), lambda i, k: (0, k)), rows, rows, vec],
        out_specs=[rows, vec],
        out_shape=[SDS((n, d), F32), SDS((1, d), F32)],
        scratch_shapes=[pltpu.VMEM((tm, d), F32)],
        compiler_params=_params(("arbitrary", "arbitrary")),
    )(dproj, wg_in, x2d, dx2, norm_in)


def _adamw_outputs(g_ref, d_ref, m_ref, v_ref, g, w, m, v):
    delta, m2, v2 = _adamw(w, g, m, v)
    g_ref[...] = g
    d_ref[...] = delta
    m_ref[...] = m2
    v_ref[...] = v2


def _reduce_adamw(slots, w, m, v, name, transposed=False):
    r, c = w.shape
    ns = slots.shape[0]
    tr = _tile(r, 128)

    def body(s_ref, w_ref, m_ref, v_ref, g_out, d_out, m_out, v_out):
        g = s_ref[0].astype(F32)
        for k in range(1, ns):
            g = g + s_ref[k].astype(F32)
        if transposed:
            g = g.T
        _adamw_outputs(g_out, d_out, m_out, v_out, g, w_ref[...], m_ref[...], v_ref[...])

    blk = pl.BlockSpec((tr, c), lambda i: (i, 0))
    slot_blk = (pl.BlockSpec((ns, c, tr), lambda i: (0, 0, i)) if transposed
                else pl.BlockSpec((ns, tr, c), lambda i: (0, i, 0)))
    return pl.pallas_call(
        body, name=name, grid=(r // tr,),
        in_specs=[slot_blk, blk, blk, blk],
        out_specs=[blk] * 4,
        out_shape=[SDS((r, c), F32)] * 4,
        compiler_params=_params(("parallel",)),
    )(slots, w, m, v)


def _adamw_small(g, w, m, v, name):
    def body(g_ref, w_ref, m_ref, v_ref, g_out, d_out, m_out, v_out):
        _adamw_outputs(g_out, d_out, m_out, v_out, g_ref[...], w_ref[...], m_ref[...], v_ref[...])

    return pl.pallas_call(
        body, name=name,
        out_shape=[SDS(g.shape, F32)] * 4,
        in_specs=[pl.BlockSpec(memory_space=pltpu.VMEM)] * 4,
        out_specs=[pl.BlockSpec(memory_space=pltpu.VMEM)] * 4,
    )(g, w, m, v)


def kernel(x, norm_in, w_in, norm_v, w_s, b_s, w_o_gmlp, w_o_sb, w_out, norm_final, loss_target, m_norm_in, m_w_in, m_norm_v, m_w_s, m_b_s, m_w_o_gmlp, m_w_o_sb, m_w_out, m_norm_final, v_norm_in, v_w_in, v_norm_v, v_w_s, v_b_s, v_w_o_gmlp, v_w_o_sb, v_w_out, v_norm_final):
    batch, seq, d = x.shape
    n = batch * seq
    groups, chunk = w_s.shape[1], w_s.shape[2]
    hd = LANE
    x2d = x.reshape(n, d)
    tgt = loss_target.reshape(n, d)
    b_col = b_s[0].reshape(groups, chunk, 1)
    norm_final2 = norm_final.reshape(1, d)

    my_slot = _slot(_me()).astype(jnp.int32).reshape(1)
    proj, h, wg_in, wg_oa, wg_ob, wg_out = _gather_in_proj(
        x2d, norm_in, w_in[0], [w_o_gmlp[0], w_o_sb[0], w_out[0]], my_slot)
    rsh = wg_oa.shape[1]
    wf_oa, wf_ob, wf_out = (w.reshape(N_DEV * rsh, d) for w in (wg_oa, wg_ob, wg_out))
    ya = _branch_a_fwd(proj, norm_v, w_s[0], b_col)
    yb, o, sb_tot = _sb_fwd(proj, batch, seq, d, hd)
    dproj, dx2, dya, dyb, merged, dpa, dpb, loss_vec, dgf = _tail(
        x2d, tgt, ya, yb, proj, wf_oa, wf_ob, wf_out, norm_final2)
    gp_wo = _dw_o([(ya, dpa), (yb, dpb), (merged, dx2)])
    dproj, gp_ws, gp_b, gp_nv = _branch_a_bwd(proj, dya, norm_v, w_s[0], b_col, dproj)

    slab = lambda a: a.reshape(d // LANE, LANE)
    gc = groups * chunk
    packed = jnp.concatenate([gp_ws.reshape(gc, chunk), gp_b, slab(gp_nv), slab(dgf), slab(loss_vec)], axis=0)
    dproj, s_oa, s_ob, s_out, packs = _sb_bwd(
        proj, o, dyb, sb_tot, dproj, gp_wo.reshape(3, N_DEV, rsh, d), packed, batch, seq, d, hd)
    grad_x, gp_nin = _dx(dproj, wg_in, x2d, dx2, norm_in)
    s_win, late_packs = _dw_in_exchange(h, dproj, my_slot, slab(gp_nin))
    tot, loss_slab = _finish_small(packs, late_packs, groups, chunk)
    ns = d // LANE
    g_ws = tot[:gc]
    g_b = tot[gc:gc + groups]
    g_nv, g_nf, _, g_nin = (tot[gc + groups + k * ns:gc + groups + (k + 1) * ns] for k in range(4))
    loss = loss_slab[0, 0]

    res = {}
    res["w_in"] = _reduce_adamw(s_win, w_in[0], m_w_in[0], v_w_in[0], "adamw_w_in", transposed=True)
    res["w_o_gmlp"] = _reduce_adamw(s_oa, w_o_gmlp[0], m_w_o_gmlp[0], v_w_o_gmlp[0], "adamw_w_o_gmlp")
    res["w_o_sb"] = _reduce_adamw(s_ob, w_o_sb[0], m_w_o_sb[0], v_w_o_sb[0], "adamw_w_o_sb")
    res["w_out"] = _reduce_adamw(s_out, w_out[0], m_w_out[0], v_w_out[0], "adamw_w_out")
    res["norm_in"] = _adamw_small(g_nin, slab(norm_in), slab(m_norm_in), slab(v_norm_in), "adamw_norm_in")
    res["norm_v"] = _adamw_small(g_nv, slab(norm_v), slab(m_norm_v), slab(v_norm_v), "adamw_norm_v")
    res["norm_final"] = _adamw_small(g_nf, slab(norm_final), slab(m_norm_final), slab(v_norm_final), "adamw_norm_final")
    res["w_s"] = _adamw_small(g_ws, w_s.reshape(gc, chunk), m_w_s.reshape(gc, chunk), v_w_s.reshape(gc, chunk), "adamw_w_s")
    res["b_s"] = _adamw_small(g_b, b_s[0], m_b_s[0], v_b_s[0], "adamw_b_s")

    shapes = {"norm_in": norm_in.shape, "w_in": w_in.shape, "norm_v": norm_v.shape, "w_s": w_s.shape,
              "b_s": b_s.shape, "w_o_gmlp": w_o_gmlp.shape, "w_o_sb": w_o_sb.shape, "w_out": w_out.shape,
              "norm_final": norm_final.shape}
    names = list(shapes)
    outs = [loss, grad_x.reshape(batch, seq, d)]
    for kind in range(4):
        outs += [res[name][kind].reshape(shapes[name]) for name in names]
    return tuple(outs)
```

```python
import functools
import math

import jax
import jax.numpy as jnp
from jax import lax
from jax.experimental import pallas as pl
from jax.experimental.pallas import tpu as pltpu

F32 = jnp.float32
BF16 = jnp.bfloat16
SDS = jax.ShapeDtypeStruct
MESH_ID = pl.DeviceIdType.MESH

N_DEV = 8
LANE = 128
SUBLANE = 8
VMEM_LIMIT = 56 * 1024 * 1024
SB_TILE = 512
SB_TILE_BWD = 512
SB_SCAN = 256
SB_HEADS = 2
MASKED_LOG = -1e30
RMS_EPS = 1e-6

ADAM_LR = 0.001
ADAM_B1 = 0.9
ADAM_B2 = 0.999
ADAM_EPS = 1e-08
ADAM_WD = 0.01
ADAM_STEP = 10

NT_DIMS = (((1,), (1,)), ((), ()))
TN_DIMS = (((0,), (0,)), ((), ()))


def _params(semantics=None):
    return pltpu.CompilerParams(dimension_semantics=semantics, vmem_limit_bytes=VMEM_LIMIT)


def _tile(n, preferred):
    t = min(n, preferred)
    assert n % t == 0, (n, t)
    return t


def _sigmoid(x):
    return 1.0 / (1.0 + jnp.exp(-x))


def _silu(x):
    s = _sigmoid(x)
    return x * s, s * (1.0 + x * (1.0 - s))


def _gelu(x):
    k = math.sqrt(2.0 / math.pi)
    x2 = x * x
    t = jnp.tanh(k * (x + 0.044715 * (x * x2)))
    cdf = 0.5 * (1.0 + t)
    return x * cdf, cdf + 0.5 * x * (1.0 - t * t) * (k * (1.0 + 3.0 * 0.044715 * x2))


def _rms_scale(x):
    return lax.rsqrt(jnp.mean(x * x, axis=-1, keepdims=True) + RMS_EPS)


def _iotas(n):
    return (lax.broadcasted_iota(jnp.int32, (n, n), 0), lax.broadcasted_iota(jnp.int32, (n, n), 1))


def _adamw(w, g, m, v):
    m = ADAM_B1 * m + (1.0 - ADAM_B1) * g
    v = ADAM_B2 * v + (1.0 - ADAM_B2) * (g * g)
    m_hat = m / (1.0 - ADAM_B1 ** ADAM_STEP)
    v_hat = v / (1.0 - ADAM_B2 ** ADAM_STEP)
    delta = -ADAM_LR * (m_hat / (jnp.sqrt(v_hat) + ADAM_EPS) + ADAM_WD * w)
    return delta, m, v


def _dot(a, b):
    return jnp.dot(a, b, preferred_element_type=F32)


def _dot_nt(a, b):
    return lax.dot_general(a, b, NT_DIMS, preferred_element_type=F32)


def _dot_tn(a, b):
    return lax.dot_general(a, b, TN_DIMS, preferred_element_type=F32)


def _sb_logs(raw, scale, valid):
    z = (raw * scale).astype(BF16)
    log_beta = jnp.minimum(z, 0) - jnp.log(1 + jnp.exp(-jnp.abs(z)))
    log_rest = log_beta - z
    if valid is not None:
        log_beta = jnp.where(valid, log_beta, MASKED_LOG)
        log_rest = jnp.where(valid, log_rest, 0)
    return log_beta, log_rest


def _me():
    return lax.axis_index("x"), lax.axis_index("y"), lax.axis_index("c")


def _slot(p):
    return 4 * p[0] + 2 * p[1] + p[2]


def _peer(me, k):
    flips = ((k >> 2) & 1, (k >> 1) & 1, k & 1)
    return tuple(1 - a if f else a for a, f in zip(me, flips))


def _stack_exchange(me, st_in, st_out, n_whole, send_sems, recv_sems, local_sems, arrivals=True):
    mine = _slot(me)
    ns = len(st_in)
    part = lambda a, dev: st_in[a] if a >= ns - n_whole else st_in[a].at[_slot(dev)]
    local = [pltpu.make_async_copy(part(a, me), st_out[a].at[mine], local_sems.at[a]) for a in range(ns)]
    remote, landed = [], []
    for k in range(1, N_DEV):
        peer = _peer(me, k)
        for a in range(ns):
            sems = dict(send_sem=send_sems.at[7 * a + k - 1], recv_sem=recv_sems.at[7 * a + k - 1])
            remote.append(pltpu.make_async_remote_copy(
                src_ref=part(a, peer), dst_ref=st_out[a].at[mine],
                device_id=peer, device_id_type=MESH_ID, **sems))
            if arrivals:
                got = st_out[a].at[_slot(peer)]
                landed.append(pltpu.make_async_remote_copy(
                    src_ref=got, dst_ref=got, device_id=me, device_id_type=MESH_ID, **sems))
    return local, remote, landed


def _gather_in_proj(x2d, norm_in, w_in_sh, wo_shards, my_slot):
    n, d = x2d.shape
    esh = w_in_sh.shape[1]
    pw = 2 * esh
    n_chip = N_DEV // 2
    tm = _tile(n, 1024)
    n_i = n // tm
    mid = n_i // 2
    no = len(wo_shards)
    flip_at = lambda st: jnp.where(st == 1, 2, jnp.where(st == 2, 1, jnp.where(st == 3, 3, 0)))

    def body(me_ref, x_ref, g_ref, win_ref, *refs):
        del me_ref
        wo_in = refs[:no]
        proj_ref, h_ref, wg_ref = refs[no:no + 3]
        wo_out = refs[no + 3:2 * no + 3]
        wv, stage = refs[2 * no + 3:2 * no + 5]
        wo_stage = refs[2 * no + 5:3 * no + 5]
        send_sems, recv_sems, pair_sems, own_sems, wo_send, wo_recv, wo_local = refs[3 * no + 5:]
        st, i = pl.program_id(0), pl.program_id(1)
        x, y, c = _me()
        me, sibling = (x, y, c), (x, y, 1 - c)
        chips = [(1 - x, y), (x, 1 - y), (1 - x, 1 - y)]
        chip_id = lambda p: 2 * p[0] + p[1]

        def window(chip, core):
            return wv.at[chip_id(chip), :, pl.ds(pl.multiple_of(core * esh, LANE), esh)]

        def copy(k, block, to, src=None):
            dst = window(block[:2], block[2])
            return pltpu.make_async_remote_copy(
                src_ref=dst if src is None else src, dst_ref=dst,
                send_sem=send_sems.at[k], recv_sem=recv_sems.at[k], device_id=to, device_id_type=MESH_ID)

        def wo_copy(a, k, block, to, src=None):
            dst = wo_out[a].at[_slot(block)]
            return pltpu.make_async_remote_copy(
                src_ref=dst if src is None else src, dst_ref=dst,
                send_sem=wo_send.at[7 * a + k], recv_sem=wo_recv.at[7 * a + k], device_id=to, device_id_type=MESH_ID)

        def own_copy():
            return pltpu.make_async_copy(stage, window((x, y), c), own_sems.at[0])

        def wo_own_copy(a):
            return pltpu.make_async_copy(wo_stage[a], wo_out[a].at[_slot(me)], wo_local.at[a])

        def pair_copy(step):
            chip = jnp.bitwise_xor(chip_id((x, y)), flip_at(step))
            return pltpu.make_async_copy(wv.at[chip], wg_ref.at[:, pl.ds(pl.multiple_of(chip * pw, LANE), pw)],
                                         pair_sems.at[step])

        first = jnp.logical_and(st == 0, i == 0)

        @pl.when(first)
        def _():
            stage[...] = win_ref[...].astype(BF16)
            own_copy().start()
            copy(0, me, sibling, src=stage).start()
            for j in range(2):
                copy(1 + j, me, (*chips[j], c), src=stage).start()
            own_copy().wait()
            copy(0, sibling, me).wait_recv()
            pair_copy(0).start()

        for s_ in range(n_chip - 1):
            @pl.when(jnp.logical_and(st == s_, i == mid))
            def _():
                copy(1 + s_, (*chips[s_], c), me).wait_recv()
                copy(4 + s_, (*chips[s_], c), sibling).start()
                if s_ == 0:
                    copy(3, me, (*chips[2], c), src=stage).start()
                if s_ == 1:
                    for a in range(no):
                        wo_stage[a][...] = wo_in[a][...].astype(BF16)
                        wo_own_copy(a).start()
                        wo_copy(a, 0, me, sibling, src=wo_stage[a]).start()
                        for j, chip in enumerate(chips):
                            wo_copy(a, 1 + j, me, (*chip, c), src=wo_stage[a]).start()
                if s_ == 2:
                    for a in range(no):
                        for j, chip in enumerate(chips):
                            wo_copy(a, 1 + j, (*chip, c), me).wait_recv()
                            wo_copy(a, 4 + j, (*chip, c), sibling).start()

        for s_ in range(1, n_chip):
            @pl.when(jnp.logical_and(st == s_, i == 0))
            def _():
                copy(3 + s_, (*chips[s_ - 1], 1 - c), me).wait_recv()
                pair_copy(s_).start()

        xv = x_ref[...]
        h = (xv * _rms_scale(xv) * g_ref[...]).astype(BF16)

        @pl.when(st == 0)
        def _():
            h_ref[...] = h

        chip_now = jnp.bitwise_xor(chip_id((x, y)), flip_at(st))
        proj_ref[...] = _dot(h, wv[chip_now]).astype(BF16)

        @pl.when(jnp.logical_and(st == n_chip - 1, i == n_i - 1))
        def _():
            copy(0, me, sibling, src=stage).wait_send()
            for j, chip in enumerate(chips):
                copy(1 + j, me, (*chip, c), src=stage).wait_send()
                copy(4 + j, (*chip, c), sibling).wait_send()
            for s_ in range(n_chip):
                pair_copy(s_).wait()
            for a in range(no):
                wo_copy(a, 0, me, sibling, src=wo_stage[a]).wait_send()
                wo_copy(a, 0, sibling, me).wait_recv()
                for j, chip in enumerate(chips):
                    wo_copy(a, 1 + j, me, (*chip, c), src=wo_stage[a]).wait_send()
                    wo_copy(a, 4 + j, (*chip, c), sibling).wait_send()
                    wo_copy(a, 4 + j, (*chip, 1 - c), me).wait_recv()
                wo_own_copy(a).wait()

    any_spec = pl.BlockSpec(memory_space=pl.ANY)
    vmem = pl.BlockSpec(memory_space=pltpu.VMEM)
    grid_spec = pltpu.PrefetchScalarGridSpec(
        num_scalar_prefetch=1, grid=(n_chip, n_i),
        in_specs=[pl.BlockSpec((tm, d), lambda st, i, me: (i, 0)),
                  pl.BlockSpec((1, d), lambda st, i, me: (0, 0)), vmem] + [vmem] * no,
        out_specs=[pl.BlockSpec((tm, pw), lambda st, i, me: (i, jnp.bitwise_xor(me[0] // 2, flip_at(st)))),
                   pl.BlockSpec((tm, d), lambda st, i, me: (jnp.where(st == 0, i, n_i - 1), 0)),
                   any_spec] + [any_spec] * no,
        scratch_shapes=[pltpu.VMEM((n_chip, d, pw), BF16), pltpu.VMEM((d, esh), BF16)] + [
            pltpu.VMEM(s.shape, BF16) for s in wo_shards] + [
            pltpu.SemaphoreType.DMA((7,)), pltpu.SemaphoreType.DMA((7,)),
            pltpu.SemaphoreType.DMA((n_chip,)), pltpu.SemaphoreType.DMA((1,)),
            pltpu.SemaphoreType.DMA((7 * no,)), pltpu.SemaphoreType.DMA((7 * no,)),
            pltpu.SemaphoreType.DMA((no,))])
    return pl.pallas_call(
        body, name="gather_in_proj", grid_spec=grid_spec,
        out_shape=[SDS((n, n_chip * pw), BF16), SDS((n, d), BF16), SDS((d, n_chip * pw), BF16)] + [
            SDS((N_DEV,) + s.shape, BF16) for s in wo_shards],
        compiler_params=pltpu.CompilerParams(dimension_semantics=("arbitrary", "arbitrary"),
                                             vmem_limit_bytes=VMEM_LIMIT),
    )(my_slot, x2d, norm_in, w_in_sh, *wo_shards)


N_CHIP = N_DEV // 2
CHIP_FLIPS = (3, 2, 1, 0)


def _owner_at(mine, j):
    flip = 0
    for pair, f in enumerate(CHIP_FLIPS):
        flip = jnp.where(j // 2 == pair, f, flip)
    return 2 * jnp.bitwise_xor(mine // 2, flip) + j % 2


def _dw_in_exchange(h, dproj, my_slot, packed):
    n, d = h.shape
    esh = dproj.shape[1] // N_DEV
    tk = _tile(n, 2048)
    nk = n // tk
    last_j = N_DEV - 1

    def body(me_ref, h_ref, dp_ref, pk_in, win_out, pk_out,
             acc, halfbuf, recvbuf, sendbuf, half_send, half_recv, win_send, win_recv,
             send_sems, recv_sems, local_sems):
        del me_ref
        j, k = pl.program_id(0), pl.program_id(1)
        x, y, c = _me()
        me, sibling = (x, y, c), (x, y, 1 - c)
        mine = _slot(me)
        my_chip = mine // 2

        def pack_copies():
            local = pltpu.make_async_copy(pk_in, pk_out.at[mine], local_sems.at[0])
            remote = [pltpu.make_async_remote_copy(
                src_ref=pk_in, dst_ref=pk_out.at[mine], send_sem=send_sems.at[kk - 1], recv_sem=recv_sems.at[kk - 1],
                device_id=_peer(me, kk), device_id_type=MESH_ID) for kk in range(1, N_DEV)]
            return local, remote

        def half_copy(jj):
            slot = (jj // 2) % 2
            return pltpu.make_async_remote_copy(
                src_ref=halfbuf.at[slot], dst_ref=recvbuf.at[slot],
                send_sem=half_send.at[slot], recv_sem=half_recv.at[slot],
                device_id=sibling, device_id_type=MESH_ID)

        def chip_copy(jj):
            slot = (jj // 2) % 2
            owner = _owner_at(mine, jj)
            return pltpu.make_async_remote_copy(
                src_ref=sendbuf.at[slot], dst_ref=win_out.at[my_chip],
                send_sem=win_send.at[slot], recv_sem=win_recv.at[my_chip],
                device_id=(owner // 4, (owner // 2) % 2, owner % 2), device_id_type=MESH_ID)

        def own_copy():
            return pltpu.make_async_copy(sendbuf.at[(last_j // 2) % 2], win_out.at[my_chip], local_sems.at[1])

        @pl.when(jnp.logical_and(j == 0, k == 0))
        def _():
            local, remote = pack_copies()
            for cp in [local] + remote:
                cp.start()

        @pl.when(k == 0)
        def _():
            acc[...] = jnp.zeros_like(acc)

        acc[...] += _dot_tn(dp_ref[...], h_ref[...])

        done = k == nk - 1
        combine = j % 2 == c
        slot = (j // 2) % 2

        @pl.when(jnp.logical_and(done, jnp.logical_not(combine)))
        def _():
            @pl.when(j >= 4)
            def _():
                half_copy(j - 4).wait_send()

            halfbuf[slot] = acc[...].astype(BF16)
            half_copy(j).start()

        @pl.when(jnp.logical_and(done, combine))
        def _():
            half_copy(j).wait_recv()

            @pl.when(j >= 4)
            def _():
                chip_copy(j - 4).wait_send()

            sendbuf[slot] = (acc[...] + recvbuf[slot].astype(F32)).astype(BF16)

            @pl.when(j < last_j - 1)
            def _():
                chip_copy(j).start()

            @pl.when(j >= last_j - 1)
            def _():
                own_copy().start()

        @pl.when(jnp.logical_and(j == last_j, done))
        def _():
            half_copy(5 - c).wait_send()
            half_copy(7 - c).wait_send()
            chip_copy(4 + c).wait_send()
            own_copy().wait()
            for chip in range(N_CHIP):
                @pl.when(chip != my_chip)
                def _():
                    landed = win_out.at[chip]
                    pltpu.make_async_remote_copy(
                        src_ref=landed, dst_ref=landed, send_sem=win_send.at[0], recv_sem=win_recv.at[chip],
                        device_id=me, device_id_type=MESH_ID).wait_recv()
            local, remote = pack_copies()
            for cp in remote:
                cp.wait_send()
            for kk in range(1, N_DEV):
                landed = pk_out.at[_slot(_peer(me, kk))]
                pltpu.make_async_remote_copy(
                    src_ref=landed, dst_ref=landed, send_sem=send_sems.at[kk - 1], recv_sem=recv_sems.at[kk - 1],
                    device_id=me, device_id_type=MESH_ID).wait_recv()
            local.wait()

    any_spec = pl.BlockSpec(memory_space=pl.ANY)
    grid_spec = pltpu.PrefetchScalarGridSpec(
        num_scalar_prefetch=1, grid=(N_DEV, nk),
        in_specs=[pl.BlockSpec((tk, d), lambda j, k, me: (k, 0)),
                  pl.BlockSpec((tk, esh), lambda j, k, me: (k, _owner_at(me[0], j))), any_spec],
        out_specs=[any_spec] * 2,
        scratch_shapes=[pltpu.VMEM((esh, d), F32)] + [pltpu.VMEM((2, esh, d), BF16)] * 3 + [
            pltpu.SemaphoreType.DMA((2,)), pltpu.SemaphoreType.DMA((2,)),
            pltpu.SemaphoreType.DMA((2,)), pltpu.SemaphoreType.DMA((N_CHIP,)),
            pltpu.SemaphoreType.DMA((N_DEV - 1,)), pltpu.SemaphoreType.DMA((N_DEV - 1,)),
            pltpu.SemaphoreType.DMA((2,))])
    return pl.pallas_call(
        body, name="dw_in_exchange", grid_spec=grid_spec,
        out_shape=[SDS((N_CHIP, esh, d), BF16), SDS((N_DEV,) + packed.shape, packed.dtype)],
        compiler_params=_params(("arbitrary", "arbitrary")),
    )(my_slot, h, dproj, packed)


def _finish_small(packs, late_packs, groups, chunk):
    rows = packs.shape[1]
    late = late_packs.shape[1]
    gc = groups * chunk

    def body(p_ref, l_ref, sum_ref, loss_ref):
        row, col = _iotas(chunk)
        tril = col <= row
        for g in range(groups):
            rs = slice(g * chunk, (g + 1) * chunk)
            tot = p_ref[0, rs, :]
            for dev in range(1, N_DEV):
                tot = tot + p_ref[dev, rs, :]
            sum_ref[rs, :] = jnp.where(tril, tot, 0.0)
        rs = slice(gc, rows)
        tot = p_ref[0, rs, :]
        for dev in range(1, N_DEV):
            tot = tot + p_ref[dev, rs, :]
        sum_ref[rs, :] = tot
        loss_ref[...] = jnp.full((SUBLANE, LANE), jnp.sum(tot[rows - gc - SUBLANE:, :]), F32)
        tot = l_ref[0]
        for dev in range(1, N_DEV):
            tot = tot + l_ref[dev]
        sum_ref[rows:rows + late, :] = tot

    return pl.pallas_call(
        body, name="finish_small",
        out_shape=[SDS((rows + late, LANE), F32), SDS((SUBLANE, LANE), F32)],
        in_specs=[pl.BlockSpec(memory_space=pltpu.VMEM)] * 2,
        out_specs=[pl.BlockSpec(memory_space=pltpu.VMEM)] * 2,
        compiler_params=pltpu.CompilerParams(vmem_limit_bytes=VMEM_LIMIT),
    )(packs, late_packs)


def _branch_a_fwd(proj, norm_v, w_s, b_col):
    n = proj.shape[0]
    d = norm_v.shape[1]
    groups, chunk, _ = w_s.shape
    tr = _tile(n, 8 * chunk)

    def body(u_ref, v_ref, z_ref, gv_ref, ws_ref, b_ref, ya_ref, vn_s, pre_s):
        row, col = _iotas(chunk)
        tril = col <= row
        vg = _gelu(v_ref[...])[0].astype(F32)
        vn_s[...] = (vg * _rms_scale(vg) * gv_ref[...]).astype(BF16)
        pre_s[...] = _gelu(u_ref[...])[0] * _silu(z_ref[...])[0]
        for g in range(groups):
            wm = jnp.where(tril, ws_ref[g], 0.0).astype(BF16)
            cs = slice(g * chunk, (g + 1) * chunk)
            for c in range(tr // chunk):
                rs = slice(c * chunk, (c + 1) * chunk)
                mixed = _dot(wm, vn_s[rs, cs]) + b_ref[g]
                ya_ref[rs, cs] = (pre_s[rs, cs].astype(F32) * mixed).astype(BF16)

    seg = lambda k: pl.BlockSpec((tr, d), lambda i: (i, k))
    return pl.pallas_call(
        body, name="branch_a_fwd", grid=(n // tr,),
        in_specs=[seg(0), seg(1), seg(2),
                  pl.BlockSpec((1, d), lambda i: (0, 0)),
                  pl.BlockSpec((groups, chunk, chunk), lambda i: (0, 0, 0)),
                  pl.BlockSpec((groups, chunk, 1), lambda i: (0, 0, 0))],
        out_specs=pl.BlockSpec((tr, d), lambda i: (i, 0)),
        out_shape=SDS((n, d), BF16),
        scratch_shapes=[pltpu.VMEM((tr, d), BF16), pltpu.VMEM((tr, d), BF16)],
        compiler_params=_params(("parallel",)),
    )(proj, proj, proj, norm_v, w_s, b_col)


def _sb_fwd(proj, batch, seq, d, hd):
    heads = d // hd
    t = _tile(seq, SB_TILE)
    sw = _tile(t, SB_SCAN)
    nb = t // sw
    scale = hd ** -0.5
    nblk = seq // t
    nh = SB_HEADS
    wide = nh * hd
    cols = [slice(hh * hd, (hh + 1) * hd) for hh in range(nh)]

    def body(qs, k_ref, vs, zb_ref, yb_ref, o_ref, tot_ref, kts, later, acc):
        for jb in range(nblk):
            kts[jb] = k_ref[jb * t:(jb + 1) * t, :].T
        row, col = _iotas(t)
        later[...] = (row[:sw, :sw] > col[:sw, :sw]).astype(BF16)

        def qblock(i, carry):
            r0 = pl.multiple_of(i * t, t)

            def tile(j, runs):
                c0 = pl.multiple_of(j * t, t)
                logs = [_sb_logs(_dot(qs[pl.ds(r0, t), cs], kts[j, cs, :]), scale, None) for cs in cols]
                scans = [_dot(jnp.concatenate([logs[hh][1][:, b * sw:(b + 1) * sw] for b in range(nb)], axis=0),
                              later[...]) for hh in range(nh)]
                new_runs = []
                for hh in range(nh):
                    after = runs[hh]
                    blocks = [None] * nb
                    for b in reversed(range(nb)):
                        ks_ = slice(b * sw, (b + 1) * sw)
                        inside = scans[hh][b * t:(b + 1) * t]
                        blocks[b] = jnp.exp(logs[hh][0][:, ks_].astype(F32) + inside + after).astype(BF16)
                        after = after + inside[:, 0:1] + logs[hh][1][:, b * sw:b * sw + 1].astype(F32)
                    new_runs.append(after)
                    acc[:, cols[hh]] += _dot(jnp.concatenate(blocks, axis=1), vs[pl.ds(c0, t), cols[hh]])
                return tuple(new_runs)

            def diagonal_tile():
                starts = [b * sw for b in range(nb)]
                logs = [[_sb_logs(_dot(qs[pl.ds(r0 + s, t - s), cs], kts[i, cs, s:s + sw]), scale,
                                  col[:t - s, :sw] < row[:t - s, :sw]) for s in starts] for cs in cols]
                scans = [_dot(jnp.concatenate([lr for _, lr in logs[hh]], axis=0), later[...]) for hh in range(nh)]
                new_runs = []
                offs = [sum(t - s for s in starts[:b]) for b in range(nb)]
                for hh in range(nh):
                    after = jnp.zeros((t, 1), F32)
                    ws = [None] * nb
                    for b in reversed(range(nb)):
                        s = starts[b]
                        lb, lr = logs[hh][b]
                        inside = scans[hh][offs[b]:offs[b] + t - s]
                        ws[b] = jnp.exp(lb.astype(F32) + inside + after[s:]).astype(BF16)
                        total = inside[:, 0:1] + lr[:, 0:1].astype(F32)
                        after = after + total if s == 0 else jnp.concatenate([after[:s], after[s:] + total], axis=0)
                    new_runs.append(after)
                    acc[:, cols[hh]] = _dot(ws[0], vs[pl.ds(r0, sw), cols[hh]])
                    for b in range(1, nb):
                        acc[starts[b]:, cols[hh]] += _dot(ws[b], vs[pl.ds(r0 + starts[b], sw), cols[hh]])
                return tuple(new_runs)

            runs = diagonal_tile()
            runs = lax.fori_loop(0, i, lambda jj, rs: tile(i - 1 - jj, rs), runs)
            for hh in range(nh):
                out = acc[:, cols[hh]]
                o_ref[pl.ds(r0, t), cols[hh]] = out.astype(BF16)
                tot_ref[hh, pl.ds(r0, t), :] = runs[hh]
                sz, _ = _silu(zb_ref[pl.ds(r0, t), cols[hh]].astype(F32))
                yb_ref[pl.ds(r0, t), cols[hh]] = (out * sz).astype(BF16)
            return carry

        lax.fori_loop(0, nblk, qblock, 0)

    col0 = d // wide
    seg = lambda k: pl.BlockSpec((seq, wide), lambda b, h: (b, k * col0 + h))
    return pl.pallas_call(
        body, name="sb_fwd", grid=(batch, heads // nh),
        in_specs=[seg(3), seg(4), seg(5), seg(6)],
        out_specs=[pl.BlockSpec((seq, wide), lambda b, h: (b, h))] * 2 + [
            pl.BlockSpec((nh, seq, 1), lambda b, h: (b * (heads // nh) + h, 0, 0))],
        out_shape=[SDS((batch * seq, d), BF16), SDS((batch * seq, d), BF16), SDS((batch * heads, seq, 1), F32)],
        scratch_shapes=[pltpu.VMEM((nblk, wide, t), BF16), pltpu.VMEM((sw, sw), BF16), pltpu.VMEM((t, wide), F32)],
        compiler_params=_params(("parallel", "parallel")),
    )(proj, proj, proj, proj)


def _tail(x2d, tgt, ya, yb, proj, w_oa, w_ob, w_out, norm_final):
    n, d = x2d.shape
    e = proj.shape[1]
    tm = _tile(n, 512)
    steps = n // tm

    def body(x_ref, t_ref, ya_ref, yb_ref, ga_ref, gb_ref, woa_ref, wob_ref, wout_ref, gf_ref,
             dproj_ref, dx2_ref, dya_ref, dyb_ref, mrg_ref, dpa_ref, dpb_ref, loss_ref, dgf_ref, dg_s, dg_sems):
        i = pl.program_id(0)

        def gate_copy(step):
            rows_ = pl.ds(pl.multiple_of(step * tm, tm), tm)
            return pltpu.make_async_copy(dg_s.at[step % 2], dproj_ref.at[rows_, pl.ds(7 * d, 2 * d)],
                                         dg_sems.at[step % 2])

        @pl.when(i == 0)
        def _():
            loss_ref[...] = jnp.zeros_like(loss_ref)
            dgf_ref[...] = jnp.zeros_like(dgf_ref)

        @pl.when(i >= 2)
        def _():
            gate_copy(i - 2).wait()

        pa = _dot(ya_ref[...], woa_ref[...])
        pb = _dot(yb_ref[...], wob_ref[...])
        sa = _sigmoid(ga_ref[...].astype(F32))
        sb = _sigmoid(gb_ref[...].astype(F32))
        merged = (sa * pa + sb * pb).astype(BF16)
        mrg_ref[...] = merged
        x2 =x_ref[...] + _dot(merged, wout_ref[...])
        r2 = _rms_scale(x2)
        xh = x2 * r2
        gf = gf_ref[...]
        diff = xh * gf - t_ref[...]
        loss_ref[...] += jnp.sum(diff * diff, axis=0, keepdims=True) * (0.5 / d)
        dy = diff * (1.0 / d)
        dgf_ref[...] += jnp.sum(dy * xh, axis=0, keepdims=True)
        dxh = dy * gf
        dx2 = r2 * (dxh - xh * jnp.mean(dxh * xh, axis=-1, keepdims=True))
        dx2_ref[...] = dx2
        dm = _dot_nt(dx2.astype(BF16), wout_ref[...])
        dpa = (dm * sa).astype(BF16)
        dpb = (dm * sb).astype(BF16)
        dpa_ref[...] = dpa
        dpb_ref[...] = dpb
        dg_s[i % 2, :, 0:d] = (dm * pa * (sa * (1.0 - sa))).astype(BF16)
        dg_s[i % 2, :, d:2 * d] = (dm * pb * (sb * (1.0 - sb))).astype(BF16)
        gate_copy(i).start()
        dya_ref[...] = _dot_nt(dpa, woa_ref[...]).astype(BF16)
        dyb_ref[...] = _dot_nt(dpb, wob_ref[...]).astype(BF16)

        @pl.when(i == steps - 1)
        def _():
            if steps >= 2:
                gate_copy(i - 1).wait()
            gate_copy(i).wait()

    rows = lambda k=0: pl.BlockSpec((tm, d), lambda i: (i, k))
    full = pl.BlockSpec((d, d), lambda i: (0, 0), pipeline_mode=pl.Buffered(1))
    vec = pl.BlockSpec((1, d), lambda i: (0, 0))
    return pl.pallas_call(
        body, name="tail", grid=(steps,),
        in_specs=[rows(), rows(), rows(), rows(), rows(7), rows(8), full, full, full, vec],
        out_specs=[pl.BlockSpec(memory_space=pl.ANY),
                   rows(), rows(), rows(), rows(), rows(), rows(), vec, vec],
        out_shape=[SDS((n, e), BF16), SDS((n, d), F32), SDS((n, d), BF16), SDS((n, d), BF16),
                   SDS((n, d), BF16), SDS((n, d), BF16), SDS((n, d), BF16),
                   SDS((1, d), F32), SDS((1, d), F32)],
        scratch_shapes=[pltpu.VMEM((2, tm, 2 * d), BF16), pltpu.SemaphoreType.DMA((2,))],
        compiler_params=_params(("arbitrary",)),
    )(x2d, tgt, ya, yb, proj, proj, w_oa, w_ob, w_out, norm_final)


def _dw_o(pairs):
    n, d = pairs[0][0].shape
    tk = _tile(n, 1024)
    nk = n // tk
    npair = len(pairs)

    def body(*refs):
        a_refs, b_refs = refs[:npair], refs[npair:2 * npair]
        o_ref, acc = refs[2 * npair], refs[2 * npair + 1]
        p, k = pl.program_id(0), pl.program_id(1)

        @pl.when(k == 0)
        def _():
            acc[...] = jnp.zeros_like(acc)

        for q in range(npair):
            @pl.when(p == q)
            def _():
                acc[...] += _dot_tn(a_refs[q][...], b_refs[q][...].astype(BF16))

        @pl.when(k == nk - 1)
        def _():
            o_ref[0] = acc[...].astype(BF16)

    def tiles(q):
        return pl.BlockSpec((tk, d), lambda p, k: (jnp.where(p == q, k, jnp.where(p < q, 0, nk - 1)), 0))

    return pl.pallas_call(
        body, name="dw_o", grid=(npair, nk),
        in_specs=[tiles(q) for q in range(npair)] * 2,
        out_specs=pl.BlockSpec((1, d, d), lambda p, k: (p, 0, 0)),
        out_shape=SDS((npair, d, d), BF16),
        scratch_shapes=[pltpu.VMEM((d, d), F32)],
        compiler_params=_params(("arbitrary", "arbitrary")),
    )(*[a for a, _ in pairs], *[b for _, b in pairs])


def _sb_bwd(proj, o, dyb, tot, dproj, dw_stack, packed, batch, seq, d, hd):
    heads = d // hd
    t = _tile(seq, SB_TILE_BWD)
    sw = _tile(t, SB_SCAN)
    nb = t // sw
    scale = hd ** -0.5
    nblk = seq // t
    nh = SB_HEADS
    wide = nh * hd
    hs = range(nh)
    cols = [slice(hh * hd, (hh + 1) * hd) for hh in hs]
    blocks = [slice(b * sw, (b + 1) * sw) for b in range(nb)]
    last = slice(sw - 1, sw)

    def compute(qs, ks, v_ref, zb_ref, o_ref, dyb_ref, tot_ref, kts, vts, dos, dzb, dq_all, dkv_t, qt_s, dot_s,
                upto, before, dq):
        for jb in range(nblk):
            rows = slice(jb * t, (jb + 1) * t)
            kts[jb] = ks[rows, :].T
            vts[jb] = v_ref[rows, :].T
        sz, dsz = _silu(zb_ref[...])
        dyb_v = dyb_ref[...]
        dos[...] = dyb_v * sz
        dzb[...] = dyb_v * o_ref[...] * dsz
        row, col = _iotas(t)
        upto[...] = (row[:sw, :sw] <= col[:sw, :sw]).astype(BF16)
        before[...] = (row[:sw, :sw] < col[:sw, :sw]).astype(BF16)

        def qblock(i, carry):
            r0 = pl.multiple_of(i * t, t)

            def tile(j, sums):
                c0 = pl.multiple_of(j * t, t)
                q_i = [qs[pl.ds(r0, t), cs] for cs in cols]
                do_i = [dos[pl.ds(r0, t), cs] for cs in cols]
                logs = [_sb_logs(_dot(q_i[hh], kts[j, cols[hh], :]), scale, None) for hh in hs]
                scans = [_dot(jnp.concatenate([logs[hh][1][:, ks_] for ks_ in blocks], axis=0), upto[...]) for hh in hs]
                dw = [_dot(do_i[hh], vts[j, cols[hh], :]) for hh in hs]
                ws, gs, new_runs = [], [], []
                for hh in hs:
                    left = tot_ref[hh, pl.ds(r0, t), :] - sums[hh][0]
                    w_b, g_b = [], []
                    for b, ks_ in enumerate(blocks):
                        inside = scans[hh][b * t:(b + 1) * t]
                        w = jnp.exp(logs[hh][0][:, ks_].astype(F32) + (left - inside))
                        w_b.append(w.astype(BF16))
                        g_b.append((dw[hh][:, ks_] * w).astype(BF16))
                        left = left - inside[:, last]
                    ws.append(jnp.concatenate(w_b, axis=1))
                    gs.append(g_b)
                    new_runs.append(tot_ref[hh, pl.ds(r0, t), :] - left)
                gscans = [_dot(jnp.concatenate(gs[hh], axis=0), before[...]) for hh in hs]
                dzs, new_gruns = [], []
                for hh in hs:
                    g_before = sums[hh][1]
                    dz_b = []
                    for b, ks_ in enumerate(blocks):
                        inside = gscans[hh][b * t:(b + 1) * t]
                        beta = jnp.exp(logs[hh][0][:, ks_]).astype(F32)
                        g = gs[hh][b].astype(F32)
                        dz_b.append((g - (g + inside + g_before) * beta).astype(BF16))
                        g_before = g_before + inside[:, last] + g[:, last]
                    dzs.append(jnp.concatenate(dz_b, axis=1))
                    new_gruns.append(g_before)
                for hh in hs:
                    dkv_t[1, j, cols[hh], :] += _dot(dot_s[cols[hh], :], ws[hh])
                for hh in hs:
                    dkv_t[0, j, cols[hh], :] += _dot(qt_s[cols[hh], :], dzs[hh])
                for hh in hs:
                    dq[:, cols[hh]] += _dot(dzs[hh], ks[pl.ds(c0, t), cols[hh]])
                return tuple((new_runs[hh], new_gruns[hh]) for hh in hs)

            def diagonal_tile(sums):
                starts = [b * sw for b in range(nb)]
                offs = [sum(t - s for s in starts[:b]) for b in range(nb)]
                q_b = [[qs[pl.ds(r0 + s, t - s), cs] for s in starts] for cs in cols]
                do_b = [[dos[pl.ds(r0 + s, t - s), cs] for s in starts] for cs in cols]
                logs = [[_sb_logs(_dot(q_b[hh][b], kts[i, cols[hh], s:s + sw]), scale,
                                  col[:t - s, :sw] < row[:t - s, :sw]) for b, s in enumerate(starts)] for hh in hs]
                dw = [[_dot(do_b[hh][b], vts[i, cols[hh], s:s + sw]) for b, s in enumerate(starts)] for hh in hs]
                scans = [_dot(jnp.concatenate([lr for _, lr in logs[hh]], axis=0), upto[...]) for hh in hs]
                ws, gs = [], []
                for hh in hs:
                    left = tot_ref[hh, pl.ds(r0, t), :] - sums[hh][0]
                    w_b, g_b = [], []
                    for b, s in enumerate(starts):
                        inside = scans[hh][offs[b]:offs[b] + t - s]
                        w = jnp.exp(logs[hh][b][0].astype(F32) + (left[s:] - inside))
                        w_b.append(w.astype(BF16))
                        g_b.append((dw[hh][b] * w).astype(BF16))
                        total = inside[:, last]
                        left = left - total if s == 0 else jnp.concatenate([left[:s], left[s:] - total], axis=0)
                    ws.append(w_b)
                    gs.append(g_b)
                gscans = [_dot(jnp.concatenate(gs[hh], axis=0), before[...]) for hh in hs]
                dzs = []
                for hh in hs:
                    g_before = sums[hh][1]
                    dz_b = []
                    for b, s in enumerate(starts):
                        inside = gscans[hh][offs[b]:offs[b] + t - s]
                        beta = jnp.exp(logs[hh][b][0]).astype(F32)
                        g = gs[hh][b].astype(F32)
                        dz_b.append((g - (g + inside + g_before[s:]) * beta).astype(BF16))
                        total = inside[:, last] + g[:, last]
                        g_before = g_before + total if s == 0 else jnp.concatenate(
                            [g_before[:s], g_before[s:] + total], axis=0)
                    dzs.append(dz_b)
                for hh in hs:
                    for b, s in enumerate(starts):
                        dkv_t[1, i, cols[hh], s:s + sw] = _dot(dot_s[cols[hh], s:], ws[hh][b])
                for hh in hs:
                    for b, s in enumerate(starts):
                        dkv_t[0, i, cols[hh], s:s + sw] = _dot(qt_s[cols[hh], s:], dzs[hh][b])
                for hh in hs:
                    for b, s in enumerate(starts):
                        dq[s:, cols[hh]] += _dot(dzs[hh][b], ks[pl.ds(r0 + s, sw), cols[hh]])

            qt_s[...] = qs[pl.ds(r0, t), :].T
            dot_s[...] = dos[pl.ds(r0, t), :].T
            zero = jnp.zeros((t, 1), F32)
            dq[...] = jnp.zeros_like(dq)
            sums = lax.fori_loop(0, i, tile, ((zero, zero),) * nh)
            diagonal_tile(sums)
            dq_all[pl.ds(r0, t), :] = dq[...]
            return carry

        lax.fori_loop(0, nblk, qblock, 0)

    pairs = heads // nh

    nst = dw_stack.shape[0]
    ns = nst + 1

    def body(qs, ks, v_ref, zb_ref, o_ref, dyb_ref, tot_ref, dproj_in, dw_ref, pk_ref, out_ref, *refs):
        del dproj_in
        st_in = [dw_ref.at[k] for k in range(nst)] + [pk_ref]
        st_out = refs[:ns]
        (kts, vts, dos, dzb, dq_all, dkv_t, qt_s, dot_s, upto, before, dq, stage, stage_sems,
         send_sems, recv_sems, local_sems) = refs[ns:]
        step = pl.program_id(0) * pairs + pl.program_id(1)
        exchange = functools.partial(_stack_exchange, _me(), st_in, st_out, 1, send_sems, recv_sems, local_sems)

        @pl.when(step == 0)
        def _():
            local, remote, _ = exchange(arrivals=False)
            for cp in local + remote:
                cp.start()

        def out_copies(s):
            rows_ = pl.ds(pl.multiple_of((s // pairs) * seq, seq), seq)
            return [pltpu.make_async_copy(
                stage.at[k], out_ref.at[rows_, pl.ds(pl.multiple_of((3 + k) * d + (s % pairs) * wide, wide), wide)],
                stage_sems.at[k]) for k in range(4)]

        compute(qs, ks, v_ref, zb_ref, o_ref, dyb_ref, tot_ref, kts, vts, dos, dzb, dq_all, dkv_t, qt_s, dot_s,
                upto, before, dq)

        @pl.when(step > 0)
        def _():
            for cp in out_copies(step - 1):
                cp.wait()

        stage[0] = (dq_all[...] * scale).astype(BF16)
        for jb in range(nblk):
            stage[1, jb * t:(jb + 1) * t, :] = (dkv_t[0, jb] * scale).astype(BF16).T
            stage[2, jb * t:(jb + 1) * t, :] = dkv_t[1, jb].astype(BF16).T
        stage[3] = dzb[...]
        for cp in out_copies(step):
            cp.start()

        @pl.when(step == batch * pairs - 1)
        def _():
            for cp in out_copies(step):
                cp.wait()
            local, remote, landed = exchange()
            for cp in remote:
                cp.wait_send()
            for cp in landed:
                cp.wait_recv()
            for cp in local:
                cp.wait()

    col0 = d // wide
    seg = lambda k: pl.BlockSpec((seq, wide), lambda b, h: (b, k * col0 + h))
    head = pl.BlockSpec((seq, wide), lambda b, h: (b, h))
    any_spec = pl.BlockSpec(memory_space=pl.ANY)
    return pl.pallas_call(
        body, name="sb_bwd", grid=(batch, pairs),
        in_specs=[seg(3), seg(4), seg(5), seg(6), head, head,
                  pl.BlockSpec((nh, seq, 1), lambda b, h: (b * pairs + h, 0, 0))] + [any_spec] * 3,
        out_specs=[any_spec] * (ns + 1),
        out_shape=[SDS(dproj.shape, dproj.dtype)] + [SDS(dw_stack.shape[1:], dw_stack.dtype)] * nst + [
            SDS((N_DEV,) + packed.shape, packed.dtype)],
        input_output_aliases={7: 0},
        scratch_shapes=[pltpu.VMEM((nblk, wide, t), BF16)] * 2 + [
            pltpu.VMEM((seq, wide), BF16), pltpu.VMEM((seq, wide), BF16),
            pltpu.VMEM((seq, wide), F32), pltpu.VMEM((2, nblk, wide, t), F32),
            pltpu.VMEM((wide, t), BF16), pltpu.VMEM((wide, t), BF16),
            pltpu.VMEM((sw, sw), BF16), pltpu.VMEM((sw, sw), BF16), pltpu.VMEM((t, wide), F32),
            pltpu.VMEM((4, seq, wide), BF16), pltpu.SemaphoreType.DMA((4,)),
            pltpu.SemaphoreType.DMA((7 * ns,)), pltpu.SemaphoreType.DMA((7 * ns,)),
            pltpu.SemaphoreType.DMA((ns,))],
        compiler_params=_params(("arbitrary", "arbitrary")),
    )(proj, proj, proj, proj, o, dyb, tot, dproj, dw_stack, packed)


def _branch_a_bwd(proj, dya, norm_v, w_s, b_col, dproj):
    n = proj.shape[0]
    d = norm_v.shape[1]
    groups, chunk, _ = w_s.shape
    tr = _tile(n, 4 * chunk)

    def body(u_ref, v_ref, z_ref, dya_ref, gv_ref, ws_ref, b_ref, dproj_in,
             out_ref, dws_ref, dbias_ref, dgv_ref, vn_s, dmix_s, dvn_s, db_ref):
        del dproj_in

        @pl.when(pl.program_id(0) == 0)
        def _():
            dws_ref[...] = jnp.zeros_like(dws_ref)
            db_ref[...] = jnp.zeros_like(db_ref)
            dgv_ref[...] = jnp.zeros_like(dgv_ref)

        row, col = _iotas(chunk)
        tril = col <= row
        gv = gv_ref[...]
        vg16, dvg_dv = _gelu(v_ref[...])
        vg = vg16.astype(F32)
        r = _rms_scale(vg)
        vh = vg * r
        vn_s[...] = (vh * gv).astype(BF16)
        ug, dug_du = _gelu(u_ref[...])
        sz, dsz = _silu(z_ref[...])
        dya_v = dya_ref[...]
        dmix_s[...] = dya_v * ug * sz
        du_scale = sz * dug_du
        dz_scale = ug * dsz
        for g in range(groups):
            wm = jnp.where(tril, ws_ref[g], 0.0).astype(BF16)
            cs = slice(g * chunk, (g + 1) * chunk)
            for c in range(tr // chunk):
                rs = slice(c * chunk, (c + 1) * chunk)
                vn = vn_s[rs, cs]
                mixed = _dot(wm, vn) + b_ref[g]
                dmix16 = dmix_s[rs, cs]
                dws_ref[g] += _dot_nt(dmix16, vn)
                db_ref[g] += dmix16.astype(F32)
                dvn_s[rs, cs] = _dot_tn(wm, dmix16)
                t_u = dya_v[rs, cs] * mixed.astype(BF16)
                out_ref[rs, g * chunk:(g + 1) * chunk] = t_u * du_scale[rs, cs]
                out_ref[rs, 2 * d + g * chunk:2 * d + (g + 1) * chunk] = t_u * dz_scale[rs, cs]
        dvn = dvn_s[...]
        dgv_ref[...] += jnp.sum(dvn * vh, axis=0, keepdims=True)
        dvh = dvn * gv
        dvg = r * (dvh - vh * jnp.mean(dvh * vh, axis=-1, keepdims=True))
        out_ref[:, d:2 * d] = (dvg * dvg_dv.astype(F32)).astype(BF16)

        @pl.when(pl.program_id(0) == n // tr - 1)
        def _():
            for g in range(groups):
                dbias_ref[g:g + 1, :] = jnp.sum(db_ref[g].T, axis=0, keepdims=True)

    seg = lambda k: pl.BlockSpec((tr, d), lambda i: (i, k))
    return pl.pallas_call(
        body, name="branch_a_bwd", grid=(n // tr,),
        in_specs=[seg(0), seg(1), seg(2), seg(0),
                  pl.BlockSpec((1, d), lambda i: (0, 0)),
                  pl.BlockSpec((groups, chunk, chunk), lambda i: (0, 0, 0)),
                  pl.BlockSpec((groups, chunk, 1), lambda i: (0, 0, 0)),
                  pl.BlockSpec(memory_space=pl.ANY)],
        out_specs=[pl.BlockSpec((tr, 3 * d), lambda i: (i, 0)),
                   pl.BlockSpec((groups, chunk, chunk), lambda i: (0, 0, 0)),
                   pl.BlockSpec((groups, chunk), lambda i: (0, 0)),
                   pl.BlockSpec((1, d), lambda i: (0, 0))],
        out_shape=[SDS(dproj.shape, dproj.dtype), SDS((groups, chunk, chunk), F32),
                   SDS((groups, chunk), F32), SDS((1, d), F32)],
        input_output_aliases={7: 0},
        scratch_shapes=[pltpu.VMEM((tr, d), BF16), pltpu.VMEM((tr, d), BF16), pltpu.VMEM((tr, d), F32),
                        pltpu.VMEM((groups, chunk, chunk), F32)],
        compiler_params=_params(("arbitrary",)),
    )(proj, proj, proj, dya, norm_v, w_s, b_col, dproj)


def _dx(dproj, wg_in, x2d, dx2, norm_in):
    n, d = x2d.shape
    nsh = N_DEV // 2
    esh = wg_in.shape[1] // nsh
    tm = _tile(n, 1024)

    def body(dp_ref, w_ref, x_ref, dx2_ref, g_ref, gx_ref, dg_ref, acc):
        i, k = pl.program_id(0), pl.program_id(1)

        @pl.when(jnp.logical_and(i == 0, k == 0))
        def _():
            dg_ref[...] = jnp.zeros_like(dg_ref)

        @pl.when(k == 0)
        def _():
            acc[...] = jnp.zeros_like(acc)

        acc[...] += _dot_nt(dp_ref[...], w_ref[...])

        @pl.when(k == nsh - 1)
        def _():
            dh = acc[...]
            x = x_ref[...]
            r = _rms_scale(x)
            xh = x * r
            dg_ref[...] += jnp.sum(dh * xh, axis=0, keepdims=True)
            dxh = dh * g_ref[...]
            gx_ref[...] = dx2_ref[...] + r * (dxh - xh * jnp.mean(dxh * xh, axis=-1, keepdims=True))

    rows = pl.BlockSpec((tm, d), lambda i, k: (i, 0))
    vec = pl.BlockSpec((1, d), lambda i, k: (0, 0))
    return pl.pallas_call(
        body, name="dx", grid=(n // tm, nsh),
        in_specs=[pl.BlockSpec((tm, esh), lambda i, k: (i, k)),
                  pl.BlockSpec((d, esh), lambda i, k: (0, k)), rows, rows, vec],
        out_specs=[rows, vec],
        out_shape=[SDS((n, d), F32), SDS((1, d), F32)],
        scratch_shapes=[pltpu.VMEM((tm, d), F32)],
        compiler_params=_params(("arbitrary", "arbitrary")),
    )(dproj, wg_in, x2d, dx2, norm_in)


def _adamw_outputs(g_ref, d_ref, m_ref, v_ref, g, w, m, v):
    delta, m2, v2 = _adamw(w, g, m, v)
    g_ref[...] = g
    d_ref[...] = delta
    m_ref[...] = m2
    v_ref[...] = v2


def _reduce_adamw(slots, w, m, v, name, transposed=False):
    r, c = w.shape
    ns = slots.shape[0]
    tr = _tile(r, 128)

    def body(s_ref, w_ref, m_ref, v_ref, g_out, d_out, m_out, v_out):
        g = s_ref[0].astype(F32)
        for k in range(1, ns):
            g = g + s_ref[k].astype(F32)
        if transposed:
            g = g.T
        _adamw_outputs(g_out, d_out, m_out, v_out, g, w_ref[...], m_ref[...], v_ref[...])

    blk = pl.BlockSpec((tr, c), lambda i: (i, 0))
    slot_blk = (pl.BlockSpec((ns, c, tr), lambda i: (0, 0, i)) if transposed
                else pl.BlockSpec((ns, tr, c), lambda i: (0, i, 0)))
    return pl.pallas_call(
        body, name=name, grid=(r // tr,),
        in_specs=[slot_blk, blk, blk, blk],
        out_specs=[blk] * 4,
        out_shape=[SDS((r, c), F32)] * 4,
        compiler_params=_params(("parallel",)),
    )(slots, w, m, v)


def _adamw_small(g, w, m, v, name):
    def body(g_ref, w_ref, m_ref, v_ref, g_out, d_out, m_out, v_out):
        _adamw_outputs(g_out, d_out, m_out, v_out, g_ref[...], w_ref[...], m_ref[...], v_ref[...])

    return pl.pallas_call(
        body, name=name,
        out_shape=[SDS(g.shape, F32)] * 4,
        in_specs=[pl.BlockSpec(memory_space=pltpu.VMEM)] * 4,
        out_specs=[pl.BlockSpec(memory_space=pltpu.VMEM)] * 4,
    )(g, w, m, v)


def kernel(x, norm_in, w_in, norm_v, w_s, b_s, w_o_gmlp, w_o_sb, w_out, norm_final, loss_target, m_norm_in, m_w_in, m_norm_v, m_w_s, m_b_s, m_w_o_gmlp, m_w_o_sb, m_w_out, m_norm_final, v_norm_in, v_w_in, v_norm_v, v_w_s, v_b_s, v_w_o_gmlp, v_w_o_sb, v_w_out, v_norm_final):
    batch, seq, d = x.shape
    n = batch * seq
    groups, chunk = w_s.shape[1], w_s.shape[2]
    hd = LANE
    x2d = x.reshape(n, d)
    tgt = loss_target.reshape(n, d)
    b_col = b_s[0].reshape(groups, chunk, 1)
    norm_final2 = norm_final.reshape(1, d)

    my_slot = _slot(_me()).astype(jnp.int32).reshape(1)
    proj, h, wg_in, wg_oa, wg_ob, wg_out = _gather_in_proj(
        x2d, norm_in, w_in[0], [w_o_gmlp[0], w_o_sb[0], w_out[0]], my_slot)
    rsh = wg_oa.shape[1]
    wf_oa, wf_ob, wf_out = (w.reshape(N_DEV * rsh, d) for w in (wg_oa, wg_ob, wg_out))
    ya = _branch_a_fwd(proj, norm_v, w_s[0], b_col)
    yb, o, sb_tot = _sb_fwd(proj, batch, seq, d, hd)
    dproj, dx2, dya, dyb, merged, dpa, dpb, loss_vec, dgf = _tail(
        x2d, tgt, ya, yb, proj, wf_oa, wf_ob, wf_out, norm_final2)
    gp_wo = _dw_o([(ya, dpa), (yb, dpb), (merged, dx2)])
    dproj, gp_ws, gp_b, gp_nv = _branch_a_bwd(proj, dya, norm_v, w_s[0], b_col, dproj)

    slab = lambda a: a.reshape(d // LANE, LANE)
    gc = groups * chunk
    packed = jnp.concatenate([gp_ws.reshape(gc, chunk), gp_b, slab(gp_nv), slab(dgf), slab(loss_vec)], axis=0)
    dproj, s_oa, s_ob, s_out, packs = _sb_bwd(
        proj, o, dyb, sb_tot, dproj, gp_wo.reshape(3, N_DEV, rsh, d), packed, batch, seq, d, hd)
    grad_x, gp_nin = _dx(dproj, wg_in, x2d, dx2, norm_in)
    s_win, late_packs = _dw_in_exchange(h, dproj, my_slot, slab(gp_nin))
    tot, loss_slab = _finish_small(packs, late_packs, groups, chunk)
    ns = d // LANE
    g_ws = tot[:gc]
    g_b = tot[gc:gc + groups]
    g_nv, g_nf, _, g_nin = (tot[gc + groups + k * ns:gc + groups + (k + 1) * ns] for k in range(4))
    loss = loss_slab[0, 0]

    res = {}
    res["w_in"] = _reduce_adamw(s_win, w_in[0], m_w_in[0], v_w_in[0], "adamw_w_in", transposed=True)
    res["w_o_gmlp"] = _reduce_adamw(s_oa, w_o_gmlp[0], m_w_o_gmlp[0], v_w_o_gmlp[0], "adamw_w_o_gmlp")
    res["w_o_sb"] = _reduce_adamw(s_ob, w_o_sb[0], m_w_o_sb[0], v_w_o_sb[0], "adamw_w_o_sb")
    res["w_out"] = _reduce_adamw(s_out, w_out[0], m_w_out[0], v_w_out[0], "adamw_w_out")
    res["norm_in"] = _adamw_small(g_nin, slab(norm_in), slab(m_norm_in), slab(v_norm_in), "adamw_norm_in")
    res["norm_v"] = _adamw_small(g_nv, slab(norm_v), slab(m_norm_v), slab(v_norm_v), "adamw_norm_v")
    res["norm_final"] = _adamw_small(g_nf, slab(norm_final), slab(m_norm_final), slab(v_norm_final), "adamw_norm_final")
    res["w_s"] = _adamw_small(g_ws, w_s.reshape(gc, chunk), m_w_s.reshape(gc, chunk), v_w_s.reshape(gc, chunk), "adamw_w_s")
    res["b_s"] = _adamw_small(g_b, b_s[0], m_b_s[0], v_b_s[0], "adamw_b_s")

    shapes = {"norm_in": norm_in.shape, "w_in": w_in.shape, "norm_v": norm_v.shape, "w_s": w_s.shape,
              "b_s": b_s.shape, "w_o_gmlp": w_o_gmlp.shape, "w_o_sb": w_o_sb.shape, "w_out": w_out.shape,
              "norm_final": norm_final.shape}
    names = list(shapes)
    outs = [loss, grad_x.reshape(batch, seq, d)]
    for kind in range(4):
        outs += [res[name][kind].reshape(shapes[name]) for name in names]
    return tuple(outs)
```

```python
import functools
import math

import jax
import jax.numpy as jnp
from jax import lax
from jax.experimental import pallas as pl
from jax.experimental.pallas import tpu as pltpu

F32 = jnp.float32
BF16 = jnp.bfloat16
SDS = jax.ShapeDtypeStruct
MESH_ID = pl.DeviceIdType.MESH

N_DEV = 8
LANE = 128
SUBLANE = 8
VMEM_LIMIT = 56 * 1024 * 1024
SB_TILE = 512
SB_TILE_BWD = 512
SB_SCAN = 256
SB_HEADS = 2
MASKED_LOG = -1e30
RMS_EPS = 1e-6

ADAM_LR = 0.001
ADAM_B1 = 0.9
ADAM_B2 = 0.999
ADAM_EPS = 1e-08
ADAM_WD = 0.01
ADAM_STEP = 10

NT_DIMS = (((1,), (1,)), ((), ()))
TN_DIMS = (((0,), (0,)), ((), ()))


def _params(semantics=None):
    return pltpu.CompilerParams(dimension_semantics=semantics, vmem_limit_bytes=VMEM_LIMIT)


def _tile(n, preferred):
    t = min(n, preferred)
    assert n % t == 0, (n, t)
    return t


def _sigmoid(x):
    return 1.0 / (1.0 + jnp.exp(-x))


def _silu(x):
    s = _sigmoid(x)
    return x * s, s * (1.0 + x * (1.0 - s))


def _gelu(x):
    k = math.sqrt(2.0 / math.pi)
    x2 = x * x
    t = jnp.tanh(k * (x + 0.044715 * (x * x2)))
    cdf = 0.5 * (1.0 + t)
    return x * cdf, cdf + 0.5 * x * (1.0 - t * t) * (k * (1.0 + 3.0 * 0.044715 * x2))


def _rms_scale(x):
    return lax.rsqrt(jnp.mean(x * x, axis=-1, keepdims=True) + RMS_EPS)


def _iotas(n):
    return (lax.broadcasted_iota(jnp.int32, (n, n), 0), lax.broadcasted_iota(jnp.int32, (n, n), 1))


def _adamw(w, g, m, v):
    m = ADAM_B1 * m + (1.0 - ADAM_B1) * g
    v = ADAM_B2 * v + (1.0 - ADAM_B2) * (g * g)
    m_hat = m / (1.0 - ADAM_B1 ** ADAM_STEP)
    v_hat = v / (1.0 - ADAM_B2 ** ADAM_STEP)
    delta = -ADAM_LR * (m_hat / (jnp.sqrt(v_hat) + ADAM_EPS) + ADAM_WD * w)
    return delta, m, v


def _dot(a, b):
    return jnp.dot(a, b, preferred_element_type=F32)


def _dot_nt(a, b):
    return lax.dot_general(a, b, NT_DIMS, preferred_element_type=F32)


def _dot_tn(a, b):
    return lax.dot_general(a, b, TN_DIMS, preferred_element_type=F32)


def _sb_logs(raw, scale, valid):
    z = (raw * scale).astype(BF16)
    log_beta = jnp.minimum(z, 0) - jnp.log(1 + jnp.exp(-jnp.abs(z)))
    log_rest = log_beta - z
    if valid is not None:
        log_beta = jnp.where(valid, log_beta, MASKED_LOG)
        log_rest = jnp.where(valid, log_rest, 0)
    return log_beta, log_rest


def _me():
    return lax.axis_index("x"), lax.axis_index("y"), lax.axis_index("c")


def _slot(p):
    return 4 * p[0] + 2 * p[1] + p[2]


def _peer(me, k):
    flips = ((k >> 2) & 1, (k >> 1) & 1, k & 1)
    return tuple(1 - a if f else a for a, f in zip(me, flips))


def _stack_exchange(me, st_in, st_out, n_whole, send_sems, recv_sems, local_sems, arrivals=True):
    mine = _slot(me)
    ns = len(st_in)
    part = lambda a, dev: st_in[a] if a >= ns - n_whole else st_in[a].at[_slot(dev)]
    local = [pltpu.make_async_copy(part(a, me), st_out[a].at[mine], local_sems.at[a]) for a in range(ns)]
    remote, landed = [], []
    for k in range(1, N_DEV):
        peer = _peer(me, k)
        for a in range(ns):
            sems = dict(send_sem=send_sems.at[7 * a + k - 1], recv_sem=recv_sems.at[7 * a + k - 1])
            remote.append(pltpu.make_async_remote_copy(
                src_ref=part(a, peer), dst_ref=st_out[a].at[mine],
                device_id=peer, device_id_type=MESH_ID, **sems))
            if arrivals:
                got = st_out[a].at[_slot(peer)]
                landed.append(pltpu.make_async_remote_copy(
                    src_ref=got, dst_ref=got, device_id=me, device_id_type=MESH_ID, **sems))
    return local, remote, landed


def _gather_in_proj(x2d, norm_in, w_in_sh, wo_shards, my_slot):
    n, d = x2d.shape
    esh = w_in_sh.shape[1]
    pw = 2 * esh
    n_chip = N_DEV // 2
    tm = _tile(n, 1024)
    n_i = n // tm
    mid = n_i // 2
    no = len(wo_shards)
    flip_at = lambda st: jnp.where(st == 1, 2, jnp.where(st == 2, 1, jnp.where(st == 3, 3, 0)))

    def body(me_ref, x_ref, g_ref, win_ref, *refs):
        del me_ref
        wo_in = refs[:no]
        proj_ref, h_ref, wg_ref = refs[no:no + 3]
        wo_out = refs[no + 3:2 * no + 3]
        wv, stage, h_s = refs[2 * no + 3:2 * no + 6]
        wo_stage = refs[2 * no + 6:3 * no + 6]
        send_sems, recv_sems, pair_sems, own_sems, wo_send, wo_recv, wo_local = refs[3 * no + 6:]
        st, i = pl.program_id(0), pl.program_id(1)
        x, y, c = _me()
        me, sibling = (x, y, c), (x, y, 1 - c)
        chips = [(1 - x, y), (x, 1 - y), (1 - x, 1 - y)]
        chip_id = lambda p: 2 * p[0] + p[1]

        def window(chip, core):
            return wv.at[chip_id(chip), :, pl.ds(pl.multiple_of(core * esh, LANE), esh)]

        def copy(k, block, to, src=None):
            dst = window(block[:2], block[2])
            return pltpu.make_async_remote_copy(
                src_ref=dst if src is None else src, dst_ref=dst,
                send_sem=send_sems.at[k], recv_sem=recv_sems.at[k], device_id=to, device_id_type=MESH_ID)

        def wo_copy(a, k, block, to, src=None):
            dst = wo_out[a].at[_slot(block)]
            return pltpu.make_async_remote_copy(
                src_ref=dst if src is None else src, dst_ref=dst,
                send_sem=wo_send.at[7 * a + k], recv_sem=wo_recv.at[7 * a + k], device_id=to, device_id_type=MESH_ID)

        def own_copy():
            return pltpu.make_async_copy(stage, window((x, y), c), own_sems.at[0])

        def wo_own_copy(a):
            return pltpu.make_async_copy(wo_stage[a], wo_out[a].at[_slot(me)], wo_local.at[a])

        def pair_copy(step):
            chip = jnp.bitwise_xor(chip_id((x, y)), flip_at(step))
            return pltpu.make_async_copy(wv.at[chip], wg_ref.at[:, pl.ds(pl.multiple_of(chip * pw, LANE), pw)],
                                         pair_sems.at[step])

        first = jnp.logical_and(st == 0, i == 0)

        @pl.when(first)
        def _():
            stage[...] = win_ref[...].astype(BF16)
            own_copy().start()
            copy(0, me, sibling, src=stage).start()
            for j in range(2):
                copy(1 + j, me, (*chips[j], c), src=stage).start()
            own_copy().wait()
            copy(0, sibling, me).wait_recv()
            pair_copy(0).start()

        for s_ in range(n_chip - 1):
            @pl.when(jnp.logical_and(st == s_, i == mid))
            def _():
                copy(1 + s_, (*chips[s_], c), me).wait_recv()
                copy(4 + s_, (*chips[s_], c), sibling).start()
                if s_ == 0:
                    copy(3, me, (*chips[2], c), src=stage).start()
                if s_ == 1:
                    for a in range(no):
                        wo_stage[a][...] = wo_in[a][...].astype(BF16)
                        wo_own_copy(a).start()
                        wo_copy(a, 0, me, sibling, src=wo_stage[a]).start()
                        for j, chip in enumerate(chips):
                            wo_copy(a, 1 + j, me, (*chip, c), src=wo_stage[a]).start()
                if s_ == 2:
                    for a in range(no):
                        for j, chip in enumerate(chips):
                            wo_copy(a, 1 + j, (*chip, c), me).wait_recv()
                            wo_copy(a, 4 + j, (*chip, c), sibling).start()

        for s_ in range(1, n_chip):
            @pl.when(jnp.logical_and(st == s_, i == 0))
            def _():
                copy(3 + s_, (*chips[s_ - 1], 1 - c), me).wait_recv()
                pair_copy(s_).start()

        xv = x_ref[...]
        h_s[...] = (xv * _rms_scale(xv) * g_ref[...]).astype(BF16)

        @pl.when(st == 0)
        def _():
            h_ref[...] = h_s[...]

        chip_now = jnp.bitwise_xor(chip_id((x, y)), flip_at(st))
        proj_ref[...] = _dot(h_s[...], wv[chip_now]).astype(BF16)

        @pl.when(jnp.logical_and(st == n_chip - 1, i == n_i - 1))
        def _():
            copy(0, me, sibling, src=stage).wait_send()
            for j, chip in enumerate(chips):
                copy(1 + j, me, (*chip, c), src=stage).wait_send()
                copy(4 + j, (*chip, c), sibling).wait_send()
            for s_ in range(n_chip):
                pair_copy(s_).wait()
            for a in range(no):
                wo_copy(a, 0, me, sibling, src=wo_stage[a]).wait_send()
                wo_copy(a, 0, sibling, me).wait_recv()
                for j, chip in enumerate(chips):
                    wo_copy(a, 1 + j, me, (*chip, c), src=wo_stage[a]).wait_send()
                    wo_copy(a, 4 + j, (*chip, c), sibling).wait_send()
                    wo_copy(a, 4 + j, (*chip, 1 - c), me).wait_recv()
                wo_own_copy(a).wait()

    any_spec = pl.BlockSpec(memory_space=pl.ANY)
    vmem = pl.BlockSpec(memory_space=pltpu.VMEM)
    grid_spec = pltpu.PrefetchScalarGridSpec(
        num_scalar_prefetch=1, grid=(n_chip, n_i),
        in_specs=[pl.BlockSpec((tm, d), lambda st, i, me: (i, 0)),
                  pl.BlockSpec((1, d), lambda st, i, me: (0, 0)), vmem] + [vmem] * no,
        out_specs=[pl.BlockSpec((tm, pw), lambda st, i, me: (i, jnp.bitwise_xor(me[0] // 2, flip_at(st)))),
                   pl.BlockSpec((tm, d), lambda st, i, me: (jnp.where(st == 0, i, n_i - 1), 0)),
                   any_spec] + [any_spec] * no,
        scratch_shapes=[pltpu.VMEM((n_chip, d, pw), BF16), pltpu.VMEM((d, esh), BF16), pltpu.VMEM((tm, d), BF16)] + [
            pltpu.VMEM(s.shape, BF16) for s in wo_shards] + [
            pltpu.SemaphoreType.DMA((7,)), pltpu.SemaphoreType.DMA((7,)),
            pltpu.SemaphoreType.DMA((n_chip,)), pltpu.SemaphoreType.DMA((1,)),
            pltpu.SemaphoreType.DMA((7 * no,)), pltpu.SemaphoreType.DMA((7 * no,)),
            pltpu.SemaphoreType.DMA((no,))])
    return pl.pallas_call(
        body, name="gather_in_proj", grid_spec=grid_spec,
        out_shape=[SDS((n, n_chip * pw), BF16), SDS((n, d), BF16), SDS((d, n_chip * pw), BF16)] + [
            SDS((N_DEV,) + s.shape, BF16) for s in wo_shards],
        compiler_params=pltpu.CompilerParams(dimension_semantics=("arbitrary", "arbitrary"),
                                             vmem_limit_bytes=VMEM_LIMIT),
    )(my_slot, x2d, norm_in, w_in_sh, *wo_shards)


N_CHIP = N_DEV // 2
CHIP_FLIPS = (3, 2, 1, 0)


def _owner_at(mine, j):
    flip = 0
    for pair, f in enumerate(CHIP_FLIPS):
        flip = jnp.where(j // 2 == pair, f, flip)
    return 2 * jnp.bitwise_xor(mine // 2, flip) + j % 2


def _dw_in_exchange(h, dproj, my_slot, packed):
    n, d = h.shape
    esh = dproj.shape[1] // N_DEV
    tk = _tile(n, 2048)
    nk = n // tk
    last_j = N_DEV - 1

    def body(me_ref, h_ref, dp_ref, pk_in, win_out, pk_out,
             acc, halfbuf, recvbuf, sendbuf, half_send, half_recv, win_send, win_recv,
             send_sems, recv_sems, local_sems):
        del me_ref
        j, k = pl.program_id(0), pl.program_id(1)
        x, y, c = _me()
        me, sibling = (x, y, c), (x, y, 1 - c)
        mine = _slot(me)
        my_chip = mine // 2

        def pack_copies():
            local = pltpu.make_async_copy(pk_in, pk_out.at[mine], local_sems.at[0])
            remote = [pltpu.make_async_remote_copy(
                src_ref=pk_in, dst_ref=pk_out.at[mine], send_sem=send_sems.at[kk - 1], recv_sem=recv_sems.at[kk - 1],
                device_id=_peer(me, kk), device_id_type=MESH_ID) for kk in range(1, N_DEV)]
            return local, remote

        def half_copy(jj):
            slot = (jj // 2) % 2
            return pltpu.make_async_remote_copy(
                src_ref=halfbuf.at[slot], dst_ref=recvbuf.at[slot],
                send_sem=half_send.at[slot], recv_sem=half_recv.at[slot],
                device_id=sibling, device_id_type=MESH_ID)

        def chip_copy(jj):
            slot = (jj // 2) % 2
            owner = _owner_at(mine, jj)
            return pltpu.make_async_remote_copy(
                src_ref=sendbuf.at[slot], dst_ref=win_out.at[my_chip],
                send_sem=win_send.at[slot], recv_sem=win_recv.at[my_chip],
                device_id=(owner // 4, (owner // 2) % 2, owner % 2), device_id_type=MESH_ID)

        def own_copy():
            return pltpu.make_async_copy(sendbuf.at[(last_j // 2) % 2], win_out.at[my_chip], local_sems.at[1])

        @pl.when(jnp.logical_and(j == 0, k == 0))
        def _():
            local, remote = pack_copies()
            for cp in [local] + remote:
                cp.start()

        @pl.when(k == 0)
        def _():
            acc[...] = jnp.zeros_like(acc)

        acc[...] += _dot_tn(dp_ref[...], h_ref[...])

        done = k == nk - 1
        combine = j % 2 == c
        slot = (j // 2) % 2

        @pl.when(jnp.logical_and(done, jnp.logical_not(combine)))
        def _():
            @pl.when(j >= 4)
            def _():
                half_copy(j - 4).wait_send()

            halfbuf[slot] = acc[...].astype(BF16)
            half_copy(j).start()

        @pl.when(jnp.logical_and(done, combine))
        def _():
            half_copy(j).wait_recv()

            @pl.when(j >= 4)
            def _():
                chip_copy(j - 4).wait_send()

            sendbuf[slot] = (acc[...] + recvbuf[slot].astype(F32)).astype(BF16)

            @pl.when(j < last_j - 1)
            def _():
                chip_copy(j).start()

            @pl.when(j >= last_j - 1)
            def _():
                own_copy().start()

        @pl.when(jnp.logical_and(j == last_j, done))
        def _():
            half_copy(5 - c).wait_send()
            half_copy(7 - c).wait_send()
            chip_copy(4 + c).wait_send()
            own_copy().wait()
            for chip in range(N_CHIP):
                @pl.when(chip != my_chip)
                def _():
                    landed = win_out.at[chip]
                    pltpu.make_async_remote_copy(
                        src_ref=landed, dst_ref=landed, send_sem=win_send.at[0], recv_sem=win_recv.at[chip],
                        device_id=me, device_id_type=MESH_ID).wait_recv()
            local, remote = pack_copies()
            for cp in remote:
                cp.wait_send()
            for kk in range(1, N_DEV):
                landed = pk_out.at[_slot(_peer(me, kk))]
                pltpu.make_async_remote_copy(
                    src_ref=landed, dst_ref=landed, send_sem=send_sems.at[kk - 1], recv_sem=recv_sems.at[kk - 1],
                    device_id=me, device_id_type=MESH_ID).wait_recv()
            local.wait()

    any_spec = pl.BlockSpec(memory_space=pl.ANY)
    grid_spec = pltpu.PrefetchScalarGridSpec(
        num_scalar_prefetch=1, grid=(N_DEV, nk),
        in_specs=[pl.BlockSpec((tk, d), lambda j, k, me: (k, 0)),
                  pl.BlockSpec((tk, esh), lambda j, k, me: (k, _owner_at(me[0], j))), any_spec],
        out_specs=[any_spec] * 2,
        scratch_shapes=[pltpu.VMEM((esh, d), F32)] + [pltpu.VMEM((2, esh, d), BF16)] * 3 + [
            pltpu.SemaphoreType.DMA((2,)), pltpu.SemaphoreType.DMA((2,)),
            pltpu.SemaphoreType.DMA((2,)), pltpu.SemaphoreType.DMA((N_CHIP,)),
            pltpu.SemaphoreType.DMA((N_DEV - 1,)), pltpu.SemaphoreType.DMA((N_DEV - 1,)),
            pltpu.SemaphoreType.DMA((2,))])
    return pl.pallas_call(
        body, name="dw_in_exchange", grid_spec=grid_spec,
        out_shape=[SDS((N_CHIP, esh, d), BF16), SDS((N_DEV,) + packed.shape, packed.dtype)],
        compiler_params=_params(("arbitrary", "arbitrary")),
    )(my_slot, h, dproj, packed)


def _finish_small(packs, late_packs, groups, chunk):
    rows = packs.shape[1]
    late = late_packs.shape[1]
    gc = groups * chunk

    def body(p_ref, l_ref, sum_ref, loss_ref):
        row, col = _iotas(chunk)
        tril = col <= row
        for g in range(groups):
            rs = slice(g * chunk, (g + 1) * chunk)
            tot = p_ref[0, rs, :]
            for dev in range(1, N_DEV):
                tot = tot + p_ref[dev, rs, :]
            sum_ref[rs, :] = jnp.where(tril, tot, 0.0)
        rs = slice(gc, rows)
        tot = p_ref[0, rs, :]
        for dev in range(1, N_DEV):
            tot = tot + p_ref[dev, rs, :]
        sum_ref[rs, :] = tot
        loss_ref[...] = jnp.full((SUBLANE, LANE), jnp.sum(tot[rows - gc - SUBLANE:, :]), F32)
        tot = l_ref[0]
        for dev in range(1, N_DEV):
            tot = tot + l_ref[dev]
        sum_ref[rows:rows + late, :] = tot

    return pl.pallas_call(
        body, name="finish_small",
        out_shape=[SDS((rows + late, LANE), F32), SDS((SUBLANE, LANE), F32)],
        in_specs=[pl.BlockSpec(memory_space=pltpu.VMEM)] * 2,
        out_specs=[pl.BlockSpec(memory_space=pltpu.VMEM)] * 2,
        compiler_params=pltpu.CompilerParams(vmem_limit_bytes=VMEM_LIMIT),
    )(packs, late_packs)


def _branch_a_fwd(proj, norm_v, w_s, b_col):
    n = proj.shape[0]
    d = norm_v.shape[1]
    groups, chunk, _ = w_s.shape
    tr = _tile(n, 8 * chunk)

    def body(u_ref, v_ref, z_ref, gv_ref, ws_ref, b_ref, ya_ref, vn_s, pre_s):
        row, col = _iotas(chunk)
        tril = col <= row
        vg = _gelu(v_ref[...])[0].astype(F32)
        vn_s[...] = (vg * _rms_scale(vg) * gv_ref[...]).astype(BF16)
        pre_s[...] = _gelu(u_ref[...])[0] * _silu(z_ref[...])[0]
        for g in range(groups):
            wm = jnp.where(tril, ws_ref[g], 0.0).astype(BF16)
            cs = slice(g * chunk, (g + 1) * chunk)
            for c in range(tr // chunk):
                rs = slice(c * chunk, (c + 1) * chunk)
                mixed = _dot(wm, vn_s[rs, cs]) + b_ref[g]
                ya_ref[rs, cs] = (pre_s[rs, cs].astype(F32) * mixed).astype(BF16)

    seg = lambda k: pl.BlockSpec((tr, d), lambda i: (i, k))
    return pl.pallas_call(
        body, name="branch_a_fwd", grid=(n // tr,),
        in_specs=[seg(0), seg(1), seg(2),
                  pl.BlockSpec((1, d), lambda i: (0, 0)),
                  pl.BlockSpec((groups, chunk, chunk), lambda i: (0, 0, 0)),
                  pl.BlockSpec((groups, chunk, 1), lambda i: (0, 0, 0))],
        out_specs=pl.BlockSpec((tr, d), lambda i: (i, 0)),
        out_shape=SDS((n, d), BF16),
        scratch_shapes=[pltpu.VMEM((tr, d), BF16), pltpu.VMEM((tr, d), BF16)],
        compiler_params=_params(("parallel",)),
    )(proj, proj, proj, norm_v, w_s, b_col)


def _sb_fwd(proj, batch, seq, d, hd):
    heads = d // hd
    t = _tile(seq, SB_TILE)
    sw = _tile(t, SB_SCAN)
    nb = t // sw
    scale = hd ** -0.5
    nblk = seq // t
    nh = SB_HEADS
    wide = nh * hd
    cols = [slice(hh * hd, (hh + 1) * hd) for hh in range(nh)]

    def body(qs, k_ref, vs, zb_ref, yb_ref, o_ref, tot_ref, kts, later, acc):
        for jb in range(nblk):
            kts[jb] = k_ref[jb * t:(jb + 1) * t, :].T
        row, col = _iotas(t)
        later[...] = (row[:sw, :sw] > col[:sw, :sw]).astype(BF16)

        def qblock(i, carry):
            r0 = pl.multiple_of(i * t, t)

            def tile(j, runs):
                c0 = pl.multiple_of(j * t, t)
                logs = [_sb_logs(_dot(qs[pl.ds(r0, t), cs], kts[j, cs, :]), scale, None) for cs in cols]
                scans = [_dot(jnp.concatenate([logs[hh][1][:, b * sw:(b + 1) * sw] for b in range(nb)], axis=0),
                              later[...]) for hh in range(nh)]
                new_runs = []
                for hh in range(nh):
                    after = runs[hh]
                    blocks = [None] * nb
                    for b in reversed(range(nb)):
                        ks_ = slice(b * sw, (b + 1) * sw)
                        inside = scans[hh][b * t:(b + 1) * t]
                        blocks[b] = jnp.exp(logs[hh][0][:, ks_].astype(F32) + inside + after).astype(BF16)
                        after = after + inside[:, 0:1] + logs[hh][1][:, b * sw:b * sw + 1].astype(F32)
                    new_runs.append(after)
                    acc[:, cols[hh]] += _dot(jnp.concatenate(blocks, axis=1), vs[pl.ds(c0, t), cols[hh]])
                return tuple(new_runs)

            def diagonal_tile():
                starts = [b * sw for b in range(nb)]
                logs = [[_sb_logs(_dot(qs[pl.ds(r0 + s, t - s), cs], kts[i, cs, s:s + sw]), scale,
                                  col[:t - s, :sw] < row[:t - s, :sw]) for s in starts] for cs in cols]
                scans = [_dot(jnp.concatenate([lr for _, lr in logs[hh]], axis=0), later[...]) for hh in range(nh)]
                new_runs = []
                offs = [sum(t - s for s in starts[:b]) for b in range(nb)]
                for hh in range(nh):
                    after = jnp.zeros((t, 1), F32)
                    ws = [None] * nb
                    for b in reversed(range(nb)):
                        s = starts[b]
                        lb, lr = logs[hh][b]
                        inside = scans[hh][offs[b]:offs[b] + t - s]
                        ws[b] = jnp.exp(lb.astype(F32) + inside + after[s:]).astype(BF16)
                        total = inside[:, 0:1] + lr[:, 0:1].astype(F32)
                        after = after + total if s == 0 else jnp.concatenate([after[:s], after[s:] + total], axis=0)
                    new_runs.append(after)
                    acc[:, cols[hh]] = _dot(ws[0], vs[pl.ds(r0, sw), cols[hh]])
                    for b in range(1, nb):
                        acc[starts[b]:, cols[hh]] += _dot(ws[b], vs[pl.ds(r0 + starts[b], sw), cols[hh]])
                return tuple(new_runs)

            runs = diagonal_tile()
            runs = lax.fori_loop(0, i, lambda jj, rs: tile(i - 1 - jj, rs), runs)
            for hh in range(nh):
                out = acc[:, cols[hh]]
                o_ref[pl.ds(r0, t), cols[hh]] = out.astype(BF16)
                tot_ref[hh, pl.ds(r0, t), :] = runs[hh]
                sz, _ = _silu(zb_ref[pl.ds(r0, t), cols[hh]].astype(F32))
                yb_ref[pl.ds(r0, t), cols[hh]] = (out * sz).astype(BF16)
            return carry

        lax.fori_loop(0, nblk, qblock, 0)

    col0 = d // wide
    seg = lambda k: pl.BlockSpec((seq, wide), lambda b, h: (b, k * col0 + h))
    return pl.pallas_call(
        body, name="sb_fwd", grid=(batch, heads // nh),
        in_specs=[seg(3), seg(4), seg(5), seg(6)],
        out_specs=[pl.BlockSpec((seq, wide), lambda b, h: (b, h))] * 2 + [
            pl.BlockSpec((nh, seq, 1), lambda b, h: (b * (heads // nh) + h, 0, 0))],
        out_shape=[SDS((batch * seq, d), BF16), SDS((batch * seq, d), BF16), SDS((batch * heads, seq, 1), F32)],
        scratch_shapes=[pltpu.VMEM((nblk, wide, t), BF16), pltpu.VMEM((sw, sw), BF16), pltpu.VMEM((t, wide), F32)],
        compiler_params=_params(("parallel", "parallel")),
    )(proj, proj, proj, proj)


def _tail(x2d, tgt, ya, yb, proj, w_oa, w_ob, w_out, norm_final):
    n, d = x2d.shape
    e = proj.shape[1]
    tm = _tile(n, 512)
    steps = n // tm

    def body(x_ref, t_ref, ya_ref, yb_ref, ga_ref, gb_ref, woa_ref, wob_ref, wout_ref, gf_ref,
             dproj_ref, dx2_ref, dya_ref, dyb_ref, mrg_ref, dpa_ref, dpb_ref, loss_ref, dgf_ref, dg_s, dg_sems):
        i = pl.program_id(0)

        def gate_copy(step):
            rows_ = pl.ds(pl.multiple_of(step * tm, tm), tm)
            return pltpu.make_async_copy(dg_s.at[step % 2], dproj_ref.at[rows_, pl.ds(7 * d, 2 * d)],
                                         dg_sems.at[step % 2])

        @pl.when(i == 0)
        def _():
            loss_ref[...] = jnp.zeros_like(loss_ref)
            dgf_ref[...] = jnp.zeros_like(dgf_ref)

        @pl.when(i >= 2)
        def _():
            gate_copy(i - 2).wait()

        halves = [slice(hf * (tm // 2), (hf + 1) * (tm // 2)) for hf in range(2)] if tm >= 512 else [slice(0, tm)]
        gf = gf_ref[...]
        pa = [_dot(ya_ref[rs, :], woa_ref[...]) for rs in halves]
        pb = [_dot(yb_ref[rs, :], wob_ref[...]) for rs in halves]
        sa = [_sigmoid(ga_ref[rs, :].astype(F32)) for rs in halves]
        sb = [_sigmoid(gb_ref[rs, :].astype(F32)) for rs in halves]
        merged = [(sa[k] * pa[k] + sb[k] * pb[k]).astype(BF16) for k in range(len(halves))]
        for k, rs in enumerate(halves):
            mrg_ref[rs, :] = merged[k]
        x2 = [x_ref[rs, :] + _dot(merged[k], wout_ref[...]) for k, rs in enumerate(halves)]
        dx2 = []
        for k, rs in enumerate(halves):
            r2 = _rms_scale(x2[k])
            xh = x2[k] * r2
            diff = xh * gf - t_ref[rs, :]
            loss_ref[...] += jnp.sum(diff * diff, axis=0, keepdims=True) * (0.5 / d)
            dy = diff * (1.0 / d)
            dgf_ref[...] += jnp.sum(dy * xh, axis=0, keepdims=True)
            dxh = dy * gf
            dx2.append(r2 * (dxh - xh * jnp.mean(dxh * xh, axis=-1, keepdims=True)))
            dx2_ref[rs, :] = dx2[k]
        dm = [_dot_nt(dx2[k].astype(BF16), wout_ref[...]) for k in range(len(halves))]
        dpa, dpb = [], []
        for k, rs in enumerate(halves):
            dpa.append((dm[k] * sa[k]).astype(BF16))
            dpb.append((dm[k] * sb[k]).astype(BF16))
            dpa_ref[rs, :] = dpa[k]
            dpb_ref[rs, :] = dpb[k]
            dg_s[i % 2, rs, 0:d] = (dm[k] * pa[k] * (sa[k] * (1.0 - sa[k]))).astype(BF16)
            dg_s[i % 2, rs, d:2 * d] = (dm[k] * pb[k] * (sb[k] * (1.0 - sb[k]))).astype(BF16)
        gate_copy(i).start()
        for k, rs in enumerate(halves):
            dya_ref[rs, :] = _dot_nt(dpa[k], woa_ref[...]).astype(BF16)
        for k, rs in enumerate(halves):
            dyb_ref[rs, :] = _dot_nt(dpb[k], wob_ref[...]).astype(BF16)

        @pl.when(i == steps - 1)
        def _():
            if steps >= 2:
                gate_copy(i - 1).wait()
            gate_copy(i).wait()

    rows = lambda k=0: pl.BlockSpec((tm, d), lambda i: (i, k))
    full = pl.BlockSpec((d, d), lambda i: (0, 0), pipeline_mode=pl.Buffered(1))
    vec = pl.BlockSpec((1, d), lambda i: (0, 0))
    return pl.pallas_call(
        body, name="tail", grid=(steps,),
        in_specs=[rows(), rows(), rows(), rows(), rows(7), rows(8), full, full, full, vec],
        out_specs=[pl.BlockSpec(memory_space=pl.ANY),
                   rows(), rows(), rows(), rows(), rows(), rows(), vec, vec],
        out_shape=[SDS((n, e), BF16), SDS((n, d), F32), SDS((n, d), BF16), SDS((n, d), BF16),
                   SDS((n, d), BF16), SDS((n, d), BF16), SDS((n, d), BF16),
                   SDS((1, d), F32), SDS((1, d), F32)],
        scratch_shapes=[pltpu.VMEM((2, tm, 2 * d), BF16), pltpu.SemaphoreType.DMA((2,))],
        compiler_params=_params(("arbitrary",)),
    )(x2d, tgt, ya, yb, proj, proj, w_oa, w_ob, w_out, norm_final)


def _dw_o(pairs):
    n, d = pairs[0][0].shape
    tk = _tile(n, 1024)
    nk = n // tk
    npair = len(pairs)

    def body(*refs):
        a_refs, b_refs = refs[:npair], refs[npair:2 * npair]
        o_ref, acc = refs[2 * npair], refs[2 * npair + 1]
        p, k = pl.program_id(0), pl.program_id(1)

        @pl.when(k == 0)
        def _():
            acc[...] = jnp.zeros_like(acc)

        for q in range(npair):
            @pl.when(p == q)
            def _():
                acc[...] += _dot_tn(a_refs[q][...], b_refs[q][...].astype(BF16))

        @pl.when(k == nk - 1)
        def _():
            o_ref[0] = acc[...].astype(BF16)

    def tiles(q):
        return pl.BlockSpec((tk, d), lambda p, k: (jnp.where(p == q, k, jnp.where(p < q, 0, nk - 1)), 0))

    return pl.pallas_call(
        body, name="dw_o", grid=(npair, nk),
        in_specs=[tiles(q) for q in range(npair)] * 2,
        out_specs=pl.BlockSpec((1, d, d), lambda p, k: (p, 0, 0)),
        out_shape=SDS((npair, d, d), BF16),
        scratch_shapes=[pltpu.VMEM((d, d), F32)],
        compiler_params=_params(("arbitrary", "arbitrary")),
    )(*[a for a, _ in pairs], *[b for _, b in pairs])


def _sb_bwd(proj, o, dyb, tot, dproj, dw_stack, packed, batch, seq, d, hd):
    heads = d // hd
    t = _tile(seq, SB_TILE_BWD)
    sw = _tile(t, SB_SCAN)
    nb = t // sw
    scale = hd ** -0.5
    nblk = seq // t
    nh = SB_HEADS
    wide = nh * hd
    hs = range(nh)
    cols = [slice(hh * hd, (hh + 1) * hd) for hh in hs]
    blocks = [slice(b * sw, (b + 1) * sw) for b in range(nb)]
    last = slice(sw - 1, sw)

    def compute(qs, ks, v_ref, zb_ref, o_ref, dyb_ref, tot_ref, kts, vts, dos, dzb, dq_all, dkv_t, qt_s, dot_s,
                upto, before, dq):
        for jb in range(nblk):
            rows = slice(jb * t, (jb + 1) * t)
            kts[jb] = ks[rows, :].T
            vts[jb] = v_ref[rows, :].T
        sz, dsz = _silu(zb_ref[...])
        dyb_v = dyb_ref[...]
        dos[...] = dyb_v * sz
        dzb[...] = dyb_v * o_ref[...] * dsz
        row, col = _iotas(t)
        upto[...] = (row[:sw, :sw] <= col[:sw, :sw]).astype(BF16)
        before[...] = (row[:sw, :sw] < col[:sw, :sw]).astype(BF16)

        def qblock(i, carry):
            r0 = pl.multiple_of(i * t, t)

            def tile(j, sums):
                c0 = pl.multiple_of(j * t, t)
                q_i = [qs[pl.ds(r0, t), cs] for cs in cols]
                do_i = [dos[pl.ds(r0, t), cs] for cs in cols]
                logs = [_sb_logs(_dot(q_i[hh], kts[j, cols[hh], :]), scale, None) for hh in hs]
                scans = [_dot(jnp.concatenate([logs[hh][1][:, ks_] for ks_ in blocks], axis=0), upto[...]) for hh in hs]
                dw = [_dot(do_i[hh], vts[j, cols[hh], :]) for hh in hs]
                ws, gs, new_runs = [], [], []
                for hh in hs:
                    left = tot_ref[hh, pl.ds(r0, t), :] - sums[hh][0]
                    w_b, g_b = [], []
                    for b, ks_ in enumerate(blocks):
                        inside = scans[hh][b * t:(b + 1) * t]
                        w = jnp.exp(logs[hh][0][:, ks_].astype(F32) + (left - inside))
                        w_b.append(w.astype(BF16))
                        g_b.append((dw[hh][:, ks_] * w).astype(BF16))
                        left = left - inside[:, last]
                    ws.append(jnp.concatenate(w_b, axis=1))
                    gs.append(g_b)
                    new_runs.append(tot_ref[hh, pl.ds(r0, t), :] - left)
                gscans = [_dot(jnp.concatenate(gs[hh], axis=0), before[...]) for hh in hs]
                dzs, new_gruns = [], []
                for hh in hs:
                    g_before = sums[hh][1]
                    dz_b = []
                    for b, ks_ in enumerate(blocks):
                        inside = gscans[hh][b * t:(b + 1) * t]
                        beta = jnp.exp(logs[hh][0][:, ks_]).astype(F32)
                        g = gs[hh][b].astype(F32)
                        dz_b.append((g - (g + inside + g_before) * beta).astype(BF16))
                        g_before = g_before + inside[:, last] + g[:, last]
                    dzs.append(jnp.concatenate(dz_b, axis=1))
                    new_gruns.append(g_before)
                for hh in hs:
                    dkv_t[1, j, cols[hh], :] += _dot(dot_s[cols[hh], :], ws[hh])
                for hh in hs:
                    dkv_t[0, j, cols[hh], :] += _dot(qt_s[cols[hh], :], dzs[hh])
                for hh in hs:
                    dq[:, cols[hh]] += _dot(dzs[hh], ks[pl.ds(c0, t), cols[hh]])
                return tuple((new_runs[hh], new_gruns[hh]) for hh in hs)

            def diagonal_tile(sums):
                starts = [b * sw for b in range(nb)]
                offs = [sum(t - s for s in starts[:b]) for b in range(nb)]
                q_b = [[qs[pl.ds(r0 + s, t - s), cs] for s in starts] for cs in cols]
                do_b = [[dos[pl.ds(r0 + s, t - s), cs] for s in starts] for cs in cols]
                logs = [[_sb_logs(_dot(q_b[hh][b], kts[i, cols[hh], s:s + sw]), scale,
                                  col[:t - s, :sw] < row[:t - s, :sw]) for b, s in enumerate(starts)] for hh in hs]
                dw = [[_dot(do_b[hh][b], vts[i, cols[hh], s:s + sw]) for b, s in enumerate(starts)] for hh in hs]
                scans = [_dot(jnp.concatenate([lr for _, lr in logs[hh]], axis=0), upto[...]) for hh in hs]
                ws, gs = [], []
                for hh in hs:
                    left = tot_ref[hh, pl.ds(r0, t), :] - sums[hh][0]
                    w_b, g_b = [], []
                    for b, s in enumerate(starts):
                        inside = scans[hh][offs[b]:offs[b] + t - s]
                        w = jnp.exp(logs[hh][b][0].astype(F32) + (left[s:] - inside))
                        w_b.append(w.astype(BF16))
                        g_b.append((dw[hh][b] * w).astype(BF16))
                        total = inside[:, last]
                        left = left - total if s == 0 else jnp.concatenate([left[:s], left[s:] - total], axis=0)
                    ws.append(w_b)
                    gs.append(g_b)
                gscans = [_dot(jnp.concatenate(gs[hh], axis=0), before[...]) for hh in hs]
                dzs = []
                for hh in hs:
                    g_before = sums[hh][1]
                    dz_b = []
                    for b, s in enumerate(starts):
                        inside = gscans[hh][offs[b]:offs[b] + t - s]
                        beta = jnp.exp(logs[hh][b][0]).astype(F32)
                        g = gs[hh][b].astype(F32)
                        dz_b.append((g - (g + inside + g_before[s:]) * beta).astype(BF16))
                        total = inside[:, last] + g[:, last]
                        g_before = g_before + total if s == 0 else jnp.concatenate(
                            [g_before[:s], g_before[s:] + total], axis=0)
                    dzs.append(dz_b)
                for hh in hs:
                    for b, s in enumerate(starts):
                        dkv_t[1, i, cols[hh], s:s + sw] = _dot(dot_s[cols[hh], s:], ws[hh][b])
                for hh in hs:
                    for b, s in enumerate(starts):
                        dkv_t[0, i, cols[hh], s:s + sw] = _dot(qt_s[cols[hh], s:], dzs[hh][b])
                for hh in hs:
                    for b, s in enumerate(starts):
                        dq[s:, cols[hh]] += _dot(dzs[hh][b], ks[pl.ds(r0 + s, sw), cols[hh]])

            qt_s[...] = qs[pl.ds(r0, t), :].T
            dot_s[...] = dos[pl.ds(r0, t), :].T
            zero = jnp.zeros((t, 1), F32)
            dq[...] = jnp.zeros_like(dq)
            sums = lax.fori_loop(0, i, tile, ((zero, zero),) * nh)
            diagonal_tile(sums)
            dq_all[pl.ds(r0, t), :] = dq[...]
            return carry

        lax.fori_loop(0, nblk, qblock, 0)

    pairs = heads // nh

    nst = dw_stack.shape[0]
    ns = nst + 1

    def body(qs, ks, v_ref, zb_ref, o_ref, dyb_ref, tot_ref, dproj_in, dw_ref, pk_ref, out_ref, *refs):
        del dproj_in
        st_in = [dw_ref.at[k] for k in range(nst)] + [pk_ref]
        st_out = refs[:ns]
        (kts, vts, dos, dzb, dq_all, dkv_t, qt_s, dot_s, upto, before, dq, stage, stage_sems,
         send_sems, recv_sems, local_sems) = refs[ns:]
        step = pl.program_id(0) * pairs + pl.program_id(1)
        exchange = functools.partial(_stack_exchange, _me(), st_in, st_out, 1, send_sems, recv_sems, local_sems)

        @pl.when(step == 0)
        def _():
            local, remote, _ = exchange(arrivals=False)
            for cp in local + remote:
                cp.start()

        def out_copies(s):
            rows_ = pl.ds(pl.multiple_of((s // pairs) * seq, seq), seq)
            return [pltpu.make_async_copy(
                stage.at[k], out_ref.at[rows_, pl.ds(pl.multiple_of((3 + k) * d + (s % pairs) * wide, wide), wide)],
                stage_sems.at[k]) for k in range(4)]

        compute(qs, ks, v_ref, zb_ref, o_ref, dyb_ref, tot_ref, kts, vts, dos, dzb, dq_all, dkv_t, qt_s, dot_s,
                upto, before, dq)

        @pl.when(step > 0)
        def _():
            for cp in out_copies(step - 1):
                cp.wait()

        stage[0] = (dq_all[...] * scale).astype(BF16)
        for jb in range(nblk):
            stage[1, jb * t:(jb + 1) * t, :] = (dkv_t[0, jb] * scale).astype(BF16).T
            stage[2, jb * t:(jb + 1) * t, :] = dkv_t[1, jb].astype(BF16).T
        stage[3] = dzb[...]
        for cp in out_copies(step):
            cp.start()

        @pl.when(step == batch * pairs - 1)
        def _():
            for cp in out_copies(step):
                cp.wait()
            local, remote, landed = exchange()
            for cp in remote:
                cp.wait_send()
            for cp in landed:
                cp.wait_recv()
            for cp in local:
                cp.wait()

    col0 = d // wide
    seg = lambda k: pl.BlockSpec((seq, wide), lambda b, h: (b, k * col0 + h))
    head = pl.BlockSpec((seq, wide), lambda b, h: (b, h))
    any_spec = pl.BlockSpec(memory_space=pl.ANY)
    return pl.pallas_call(
        body, name="sb_bwd", grid=(batch, pairs),
        in_specs=[seg(3), seg(4), seg(5), seg(6), head, head,
                  pl.BlockSpec((nh, seq, 1), lambda b, h: (b * pairs + h, 0, 0))] + [any_spec] * 3,
        out_specs=[any_spec] * (ns + 1),
        out_shape=[SDS(dproj.shape, dproj.dtype)] + [SDS(dw_stack.shape[1:], dw_stack.dtype)] * nst + [
            SDS((N_DEV,) + packed.shape, packed.dtype)],
        input_output_aliases={7: 0},
        scratch_shapes=[pltpu.VMEM((nblk, wide, t), BF16)] * 2 + [
            pltpu.VMEM((seq, wide), BF16), pltpu.VMEM((seq, wide), BF16),
            pltpu.VMEM((seq, wide), F32), pltpu.VMEM((2, nblk, wide, t), F32),
            pltpu.VMEM((wide, t), BF16), pltpu.VMEM((wide, t), BF16),
            pltpu.VMEM((sw, sw), BF16), pltpu.VMEM((sw, sw), BF16), pltpu.VMEM((t, wide), F32),
            pltpu.VMEM((4, seq, wide), BF16), pltpu.SemaphoreType.DMA((4,)),
            pltpu.SemaphoreType.DMA((7 * ns,)), pltpu.SemaphoreType.DMA((7 * ns,)),
            pltpu.SemaphoreType.DMA((ns,))],
        compiler_params=_params(("arbitrary", "arbitrary")),
    )(proj, proj, proj, proj, o, dyb, tot, dproj, dw_stack, packed)


def _branch_a_bwd(proj, dya, norm_v, w_s, b_col, dproj):
    n = proj.shape[0]
    d = norm_v.shape[1]
    groups, chunk, _ = w_s.shape
    tr = _tile(n, 4 * chunk)

    def body(u_ref, v_ref, z_ref, dya_ref, gv_ref, ws_ref, b_ref, dproj_in,
             out_ref, dws_ref, dbias_ref, dgv_ref, vn_s, dmix_s, dvn_s, db_ref):
        del dproj_in

        @pl.when(pl.program_id(0) == 0)
        def _():
            dws_ref[...] = jnp.zeros_like(dws_ref)
            db_ref[...] = jnp.zeros_like(db_ref)
            dgv_ref[...] = jnp.zeros_like(dgv_ref)

        row, col = _iotas(chunk)
        tril = col <= row
        gv = gv_ref[...]
        vg16, dvg_dv = _gelu(v_ref[...])
        vg = vg16.astype(F32)
        r = _rms_scale(vg)
        vh = vg * r
        vn_s[...] = (vh * gv).astype(BF16)
        ug, dug_du = _gelu(u_ref[...])
        sz, dsz = _silu(z_ref[...])
        dya_v = dya_ref[...]
        dmix_s[...] = dya_v * ug * sz
        du_scale = sz * dug_du
        dz_scale = ug * dsz
        for g in range(groups):
            wm = jnp.where(tril, ws_ref[g], 0.0).astype(BF16)
            cs = slice(g * chunk, (g + 1) * chunk)
            for c in range(tr // chunk):
                rs = slice(c * chunk, (c + 1) * chunk)
                vn = vn_s[rs, cs]
                mixed = _dot(wm, vn) + b_ref[g]
                dmix16 = dmix_s[rs, cs]
                dws_ref[g] += _dot_nt(dmix16, vn)
                db_ref[g] += dmix16.astype(F32)
                dvn_s[rs, cs] = _dot_tn(wm, dmix16)
                t_u = dya_v[rs, cs] * mixed.astype(BF16)
                out_ref[rs, g * chunk:(g + 1) * chunk] = t_u * du_scale[rs, cs]
                out_ref[rs, 2 * d + g * chunk:2 * d + (g + 1) * chunk] = t_u * dz_scale[rs, cs]
        dvn = dvn_s[...]
        dgv_ref[...] += jnp.sum(dvn * vh, axis=0, keepdims=True)
        dvh = dvn * gv
        dvg = r * (dvh - vh * jnp.mean(dvh * vh, axis=-1, keepdims=True))
        out_ref[:, d:2 * d] = (dvg * dvg_dv.astype(F32)).astype(BF16)

        @pl.when(pl.program_id(0) == n // tr - 1)
        def _():
            for g in range(groups):
                dbias_ref[g:g + 1, :] = jnp.sum(db_ref[g].T, axis=0, keepdims=True)

    seg = lambda k: pl.BlockSpec((tr, d), lambda i: (i, k))
    return pl.pallas_call(
        body, name="branch_a_bwd", grid=(n // tr,),
        in_specs=[seg(0), seg(1), seg(2), seg(0),
                  pl.BlockSpec((1, d), lambda i: (0, 0)),
                  pl.BlockSpec((groups, chunk, chunk), lambda i: (0, 0, 0)),
                  pl.BlockSpec((groups, chunk, 1), lambda i: (0, 0, 0)),
                  pl.BlockSpec(memory_space=pl.ANY)],
        out_specs=[pl.BlockSpec((tr, 3 * d), lambda i: (i, 0)),
                   pl.BlockSpec((groups, chunk, chunk), lambda i: (0, 0, 0)),
                   pl.BlockSpec((groups, chunk), lambda i: (0, 0)),
                   pl.BlockSpec((1, d), lambda i: (0, 0))],
        out_shape=[SDS(dproj.shape, dproj.dtype), SDS((groups, chunk, chunk), F32),
                   SDS((groups, chunk), F32), SDS((1, d), F32)],
        input_output_aliases={7: 0},
        scratch_shapes=[pltpu.VMEM((tr, d), BF16), pltpu.VMEM((tr, d), BF16), pltpu.VMEM((tr, d), F32),
                        pltpu.VMEM((groups, chunk, chunk), F32)],
        compiler_params=_params(("arbitrary",)),
    )(proj, proj, proj, dya, norm_v, w_s, b_col, dproj)


def _dx(dproj, wg_in, x2d, dx2, norm_in):
    n, d = x2d.shape
    nsh = N_DEV // 2
    esh = wg_in.shape[1] // nsh
    tm = _tile(n, 1024)

    def body(dp_ref, w_ref, x_ref, dx2_ref, g_ref, gx_ref, dg_ref, acc):
        i, k = pl.program_id(0), pl.program_id(1)

        @pl.when(jnp.logical_and(i == 0, k == 0))
        def _():
            dg_ref[...] = jnp.zeros_like(dg_ref)

        @pl.when(k == 0)
        def _():
            acc[...] = jnp.zeros_like(acc)

        acc[...] += _dot_nt(dp_ref[...], w_ref[...])

        @pl.when(k == nsh - 1)
        def _():
            dh = acc[...]
            x = x_ref[...]
            r = _rms_scale(x)
            xh = x * r
            dg_ref[...] += jnp.sum(dh * xh, axis=0, keepdims=True)
            dxh = dh * g_ref[...]
            gx_ref[...] = dx2_ref[...] + r * (dxh - xh * jnp.mean(dxh * xh, axis=-1, keepdims=True))

    rows = pl.BlockSpec((tm, d), lambda i, k: (i, 0))
    vec = pl.BlockSpec((1, d), lambda i, k: (0, 0))
    return pl.pallas_call(
        body, name="dx", grid=(n // tm, nsh),
        in_specs=[pl.BlockSpec((tm, esh), lambda i, k: (i, k)),
                  pl.BlockSpec((d, esh), lambda i, k: (0, k)), rows, rows, vec],
        out_specs=[rows, vec],
        out_shape=[SDS((n, d), F32), SDS((1, d), F32)],
        scratch_shapes=[pltpu.VMEM((tm, d), F32)],
        compiler_params=_params(("arbitrary", "arbitrary")),
    )(dproj, wg_in, x2d, dx2, norm_in)


def _adamw_outputs(g_ref, d_ref, m_ref, v_ref, g, w, m, v):
    delta, m2, v2 = _adamw(w, g, m, v)
    g_ref[...] = g
    d_ref[...] = delta
    m_ref[...] = m2
    v_ref[...] = v2


def _reduce_adamw(slots, w, m, v, name, transposed=False):
    r, c = w.shape
    ns = slots.shape[0]
    tr = _tile(r, 128)

    def body(s_ref, w_ref, m_ref, v_ref, g_out, d_out, m_out, v_out):
        g = s_ref[0].astype(F32)
        for k in range(1, ns):
            g = g + s_ref[k].astype(F32)
        if transposed:
            g = g.T
        _adamw_outputs(g_out, d_out, m_out, v_out, g, w_ref[...], m_ref[...], v_ref[...])

    blk = pl.BlockSpec((tr, c), lambda i: (i, 0))
    slot_blk = (pl.BlockSpec((ns, c, tr), lambda i: (0, 0, i)) if transposed
                else pl.BlockSpec((ns, tr, c), lambda i: (0, i, 0)))
    return pl.pallas_call(
        body, name=name, grid=(r // tr,),
        in_specs=[slot_blk, blk, blk, blk],
        out_specs=[blk] * 4,
        out_shape=[SDS((r, c), F32)] * 4,
        compiler_params=_params(("parallel",)),
    )(slots, w, m, v)


def _adamw_small(g, w, m, v, name):
    def body(g_ref, w_ref, m_ref, v_ref, g_out, d_out, m_out, v_out):
        _adamw_outputs(g_out, d_out, m_out, v_out, g_ref[...], w_ref[...], m_ref[...], v_ref[...])

    return pl.pallas_call(
        body, name=name,
        out_shape=[SDS(g.shape, F32)] * 4,
        in_specs=[pl.BlockSpec(memory_space=pltpu.VMEM)] * 4,
        out_specs=[pl.BlockSpec(memory_space=pltpu.VMEM)] * 4,
    )(g, w, m, v)


def kernel(x, norm_in, w_in, norm_v, w_s, b_s, w_o_gmlp, w_o_sb, w_out, norm_final, loss_target, m_norm_in, m_w_in, m_norm_v, m_w_s, m_b_s, m_w_o_gmlp, m_w_o_sb, m_w_out, m_norm_final, v_norm_in, v_w_in, v_norm_v, v_w_s, v_b_s, v_w_o_gmlp, v_w_o_sb, v_w_out, v_norm_final):
    batch, seq, d = x.shape
    n = batch * seq
    groups, chunk = w_s.shape[1], w_s.shape[2]
    hd = LANE
    x2d = x.reshape(n, d)
    tgt = loss_target.reshape(n, d)
    b_col = b_s[0].reshape(groups, chunk, 1)
    norm_final2 = norm_final.reshape(1, d)

    my_slot = _slot(_me()).astype(jnp.int32).reshape(1)
    proj, h, wg_in, wg_oa, wg_ob, wg_out = _gather_in_proj(
        x2d, norm_in, w_in[0], [w_o_gmlp[0], w_o_sb[0], w_out[0]], my_slot)
    rsh = wg_oa.shape[1]
    wf_oa, wf_ob, wf_out = (w.reshape(N_DEV * rsh, d) for w in (wg_oa, wg_ob, wg_out))
    ya = _branch_a_fwd(proj, norm_v, w_s[0], b_col)
    yb, o, sb_tot = _sb_fwd(proj, batch, seq, d, hd)
    dproj, dx2, dya, dyb, merged, dpa, dpb, loss_vec, dgf = _tail(
        x2d, tgt, ya, yb, proj, wf_oa, wf_ob, wf_out, norm_final2)
    gp_wo = _dw_o([(ya, dpa), (yb, dpb), (merged, dx2)])
    dproj, gp_ws, gp_b, gp_nv = _branch_a_bwd(proj, dya, norm_v, w_s[0], b_col, dproj)

    slab = lambda a: a.reshape(d // LANE, LANE)
    gc = groups * chunk
    packed = jnp.concatenate([gp_ws.reshape(gc, chunk), gp_b, slab(gp_nv), slab(dgf), slab(loss_vec)], axis=0)
    dproj, s_oa, s_ob, s_out, packs = _sb_bwd(
        proj, o, dyb, sb_tot, dproj, gp_wo.reshape(3, N_DEV, rsh, d), packed, batch, seq, d, hd)
    grad_x, gp_nin = _dx(dproj, wg_in, x2d, dx2, norm_in)
    s_win, late_packs = _dw_in_exchange(h, dproj, my_slot, slab(gp_nin))
    tot, loss_slab = _finish_small(packs, late_packs, groups, chunk)
    ns = d // LANE
    g_ws = tot[:gc]
    g_b = tot[gc:gc + groups]
    g_nv, g_nf, _, g_nin = (tot[gc + groups + k * ns:gc + groups + (k + 1) * ns] for k in range(4))
    loss = loss_slab[0, 0]

    res = {}
    res["w_in"] = _reduce_adamw(s_win, w_in[0], m_w_in[0], v_w_in[0], "adamw_w_in", transposed=True)
    res["w_o_gmlp"] = _reduce_adamw(s_oa, w_o_gmlp[0], m_w_o_gmlp[0], v_w_o_gmlp[0], "adamw_w_o_gmlp")
    res["w_o_sb"] = _reduce_adamw(s_ob, w_o_sb[0], m_w_o_sb[0], v_w_o_sb[0], "adamw_w_o_sb")
    res["w_out"] = _reduce_adamw(s_out, w_out[0], m_w_out[0], v_w_out[0], "adamw_w_out")
    res["norm_in"] = _adamw_small(g_nin, slab(norm_in), slab(m_norm_in), slab(v_norm_in), "adamw_norm_in")
    res["norm_v"] = _adamw_small(g_nv, slab(norm_v), slab(m_norm_v), slab(v_norm_v), "adamw_norm_v")
    res["norm_final"] = _adamw_small(g_nf, slab(norm_final), slab(m_norm_final), slab(v_norm_final), "adamw_norm_final")
    res["w_s"] = _adamw_small(g_ws, w_s.reshape(gc, chunk), m_w_s.reshape(gc, chunk), v_w_s.reshape(gc, chunk), "adamw_w_s")
    res["b_s"] = _adamw_small(g_b, b_s[0], m_b_s[0], v_b_s[0], "adamw_b_s")

    shapes = {"norm_in": norm_in.shape, "w_in": w_in.shape, "norm_v": norm_v.shape, "w_s": w_s.shape,
              "b_s": b_s.shape, "w_o_gmlp": w_o_gmlp.shape, "w_o_sb": w_o_sb.shape, "w_out": w_out.shape,
              "norm_final": norm_final.shape}
    names = list(shapes)
    outs = [loss, grad_x.reshape(batch, seq, d)]
    for kind in range(4):
        outs += [res[name][kind].reshape(shapes[name]) for name in names]
    return tuple(outs)
```

```python
import functools
import math

import jax
import jax.numpy as jnp
from jax import lax
from jax.experimental import pallas as pl
from jax.experimental.pallas import tpu as pltpu

F32 = jnp.float32
BF16 = jnp.bfloat16
SDS = jax.ShapeDtypeStruct
MESH_ID = pl.DeviceIdType.MESH

N_DEV = 8
LANE = 128
SUBLANE = 8
VMEM_LIMIT = 56 * 1024 * 1024
SB_TILE = 512
SB_TILE_BWD = 512
SB_SCAN = 256
SB_HEADS = 2
MASKED_LOG = -1e30
RMS_EPS = 1e-6

ADAM_LR = 0.001
ADAM_B1 = 0.9
ADAM_B2 = 0.999
ADAM_EPS = 1e-08
ADAM_WD = 0.01
ADAM_STEP = 10

NT_DIMS = (((1,), (1,)), ((), ()))
TN_DIMS = (((0,), (0,)), ((), ()))


def _params(semantics=None):
    return pltpu.CompilerParams(dimension_semantics=semantics, vmem_limit_bytes=VMEM_LIMIT)


def _tile(n, preferred):
    t = min(n, preferred)
    assert n % t == 0, (n, t)
    return t


def _sigmoid(x):
    return 1.0 / (1.0 + jnp.exp(-x))


def _silu(x):
    s = _sigmoid(x)
    return x * s, s * (1.0 + x * (1.0 - s))


def _gelu(x):
    k = math.sqrt(2.0 / math.pi)
    x2 = x * x
    t = jnp.tanh(k * (x + 0.044715 * (x * x2)))
    cdf = 0.5 * (1.0 + t)
    return x * cdf, cdf + 0.5 * x * (1.0 - t * t) * (k * (1.0 + 3.0 * 0.044715 * x2))


def _rms_scale(x):
    return lax.rsqrt(jnp.mean(x * x, axis=-1, keepdims=True) + RMS_EPS)


def _iotas(n):
    return (lax.broadcasted_iota(jnp.int32, (n, n), 0), lax.broadcasted_iota(jnp.int32, (n, n), 1))


def _adamw(w, g, m, v):
    m = ADAM_B1 * m + (1.0 - ADAM_B1) * g
    v = ADAM_B2 * v + (1.0 - ADAM_B2) * (g * g)
    m_hat = m / (1.0 - ADAM_B1 ** ADAM_STEP)
    v_hat = v / (1.0 - ADAM_B2 ** ADAM_STEP)
    delta = -ADAM_LR * (m_hat / (jnp.sqrt(v_hat) + ADAM_EPS) + ADAM_WD * w)
    return delta, m, v


def _dot(a, b):
    return jnp.dot(a, b, preferred_element_type=F32)


def _dot_nt(a, b):
    return lax.dot_general(a, b, NT_DIMS, preferred_element_type=F32)


def _dot_tn(a, b):
    return lax.dot_general(a, b, TN_DIMS, preferred_element_type=F32)


def _sb_logs(raw, scale, valid):
    z = (raw * scale).astype(BF16)
    log_beta = jnp.minimum(z, 0) - jnp.log(1 + jnp.exp(-jnp.abs(z)))
    log_rest = log_beta - z
    if valid is not None:
        log_beta = jnp.where(valid, log_beta, MASKED_LOG)
        log_rest = jnp.where(valid, log_rest, 0)
    return log_beta, log_rest


def _me():
    return lax.axis_index("x"), lax.axis_index("y"), lax.axis_index("c")


def _slot(p):
    return 4 * p[0] + 2 * p[1] + p[2]


def _peer(me, k):
    flips = ((k >> 2) & 1, (k >> 1) & 1, k & 1)
    return tuple(1 - a if f else a for a, f in zip(me, flips))


def _stack_exchange(me, st_in, st_out, n_whole, send_sems, recv_sems, local_sems, arrivals=True):
    mine = _slot(me)
    ns = len(st_in)
    part = lambda a, dev: st_in[a] if a >= ns - n_whole else st_in[a].at[_slot(dev)]
    local = [pltpu.make_async_copy(part(a, me), st_out[a].at[mine], local_sems.at[a]) for a in range(ns)]
    remote, landed = [], []
    for k in range(1, N_DEV):
        peer = _peer(me, k)
        for a in range(ns):
            sems = dict(send_sem=send_sems.at[7 * a + k - 1], recv_sem=recv_sems.at[7 * a + k - 1])
            remote.append(pltpu.make_async_remote_copy(
                src_ref=part(a, peer), dst_ref=st_out[a].at[mine],
                device_id=peer, device_id_type=MESH_ID, **sems))
            if arrivals:
                got = st_out[a].at[_slot(peer)]
                landed.append(pltpu.make_async_remote_copy(
                    src_ref=got, dst_ref=got, device_id=me, device_id_type=MESH_ID, **sems))
    return local, remote, landed


def _gather_in_proj(x2d, norm_in, w_in_sh, wo_shards, my_slot):
    n, d = x2d.shape
    esh = w_in_sh.shape[1]
    pw = 2 * esh
    n_chip = N_DEV // 2
    tm = _tile(n, 1024)
    n_i = n // tm
    mid = n_i // 2
    no = len(wo_shards)
    flip_at = lambda st: jnp.where(st == 1, 2, jnp.where(st == 2, 1, jnp.where(st == 3, 3, 0)))

    def body(me_ref, x_ref, g_ref, win_ref, *refs):
        del me_ref
        wo_in = refs[:no]
        proj_ref, h_ref, wg_ref = refs[no:no + 3]
        wo_out = refs[no + 3:2 * no + 3]
        wv, stage, h_s = refs[2 * no + 3:2 * no + 6]
        wo_stage = refs[2 * no + 6:3 * no + 6]
        send_sems, recv_sems, pair_sems, own_sems, wo_send, wo_recv, wo_local = refs[3 * no + 6:]
        st, i = pl.program_id(0), pl.program_id(1)
        x, y, c = _me()
        me, sibling = (x, y, c), (x, y, 1 - c)
        chips = [(1 - x, y), (x, 1 - y), (1 - x, 1 - y)]
        chip_id = lambda p: 2 * p[0] + p[1]

        def window(chip, core):
            return wv.at[chip_id(chip), :, pl.ds(pl.multiple_of(core * esh, LANE), esh)]

        def copy(k, block, to, src=None):
            dst = window(block[:2], block[2])
            return pltpu.make_async_remote_copy(
                src_ref=dst if src is None else src, dst_ref=dst,
                send_sem=send_sems.at[k], recv_sem=recv_sems.at[k], device_id=to, device_id_type=MESH_ID)

        def wo_copy(a, k, block, to, src=None):
            dst = wo_out[a].at[_slot(block)]
            return pltpu.make_async_remote_copy(
                src_ref=dst if src is None else src, dst_ref=dst,
                send_sem=wo_send.at[7 * a + k], recv_sem=wo_recv.at[7 * a + k], device_id=to, device_id_type=MESH_ID)

        def own_copy():
            return pltpu.make_async_copy(stage, window((x, y), c), own_sems.at[0])

        def wo_own_copy(a):
            return pltpu.make_async_copy(wo_stage[a], wo_out[a].at[_slot(me)], wo_local.at[a])

        def pair_copy(step):
            chip = jnp.bitwise_xor(chip_id((x, y)), flip_at(step))
            return pltpu.make_async_copy(wv.at[chip], wg_ref.at[:, pl.ds(pl.multiple_of(chip * pw, LANE), pw)],
                                         pair_sems.at[step])

        first = jnp.logical_and(st == 0, i == 0)

        @pl.when(first)
        def _():
            stage[...] = win_ref[...].astype(BF16)
            own_copy().start()
            copy(0, me, sibling, src=stage).start()
            for j in range(2):
                copy(1 + j, me, (*chips[j], c), src=stage).start()
            own_copy().wait()
            copy(0, sibling, me).wait_recv()
            pair_copy(0).start()

        for s_ in range(n_chip - 1):
            @pl.when(jnp.logical_and(st == s_, i == mid))
            def _():
                copy(1 + s_, (*chips[s_], c), me).wait_recv()
                copy(4 + s_, (*chips[s_], c), sibling).start()
                if s_ == 0:
                    copy(3, me, (*chips[2], c), src=stage).start()
                if s_ == 1:
                    for a in range(no):
                        wo_stage[a][...] = wo_in[a][...].astype(BF16)
                        wo_own_copy(a).start()
                        wo_copy(a, 0, me, sibling, src=wo_stage[a]).start()
                        for j, chip in enumerate(chips):
                            wo_copy(a, 1 + j, me, (*chip, c), src=wo_stage[a]).start()
                if s_ == 2:
                    for a in range(no):
                        for j, chip in enumerate(chips):
                            wo_copy(a, 1 + j, (*chip, c), me).wait_recv()
                            wo_copy(a, 4 + j, (*chip, c), sibling).start()

        for s_ in range(1, n_chip):
            @pl.when(jnp.logical_and(st == s_, i == 0))
            def _():
                copy(3 + s_, (*chips[s_ - 1], 1 - c), me).wait_recv()
                pair_copy(s_).start()

        xv = x_ref[...]
        h_s[...] = (xv * _rms_scale(xv) * g_ref[...]).astype(BF16)

        @pl.when(st == 0)
        def _():
            h_ref[...] = h_s[...]

        chip_now = jnp.bitwise_xor(chip_id((x, y)), flip_at(st))
        proj_ref[...] = _dot(h_s[...], wv[chip_now]).astype(BF16)

        @pl.when(jnp.logical_and(st == n_chip - 1, i == n_i - 1))
        def _():
            copy(0, me, sibling, src=stage).wait_send()
            for j, chip in enumerate(chips):
                copy(1 + j, me, (*chip, c), src=stage).wait_send()
                copy(4 + j, (*chip, c), sibling).wait_send()
            for s_ in range(n_chip):
                pair_copy(s_).wait()
            for a in range(no):
                wo_copy(a, 0, me, sibling, src=wo_stage[a]).wait_send()
                wo_copy(a, 0, sibling, me).wait_recv()
                for j, chip in enumerate(chips):
                    wo_copy(a, 1 + j, me, (*chip, c), src=wo_stage[a]).wait_send()
                    wo_copy(a, 4 + j, (*chip, c), sibling).wait_send()
                    wo_copy(a, 4 + j, (*chip, 1 - c), me).wait_recv()
                wo_own_copy(a).wait()

    any_spec = pl.BlockSpec(memory_space=pl.ANY)
    vmem = pl.BlockSpec(memory_space=pltpu.VMEM)
    grid_spec = pltpu.PrefetchScalarGridSpec(
        num_scalar_prefetch=1, grid=(n_chip, n_i),
        in_specs=[pl.BlockSpec((tm, d), lambda st, i, me: (i, 0)),
                  pl.BlockSpec((1, d), lambda st, i, me: (0, 0)), vmem] + [vmem] * no,
        out_specs=[pl.BlockSpec((tm, pw), lambda st, i, me: (i, jnp.bitwise_xor(me[0] // 2, flip_at(st)))),
                   pl.BlockSpec((tm, d), lambda st, i, me: (jnp.where(st == 0, i, n_i - 1), 0)),
                   any_spec] + [any_spec] * no,
        scratch_shapes=[pltpu.VMEM((n_chip, d, pw), BF16), pltpu.VMEM((d, esh), BF16), pltpu.VMEM((tm, d), BF16)] + [
            pltpu.VMEM(s.shape, BF16) for s in wo_shards] + [
            pltpu.SemaphoreType.DMA((7,)), pltpu.SemaphoreType.DMA((7,)),
            pltpu.SemaphoreType.DMA((n_chip,)), pltpu.SemaphoreType.DMA((1,)),
            pltpu.SemaphoreType.DMA((7 * no,)), pltpu.SemaphoreType.DMA((7 * no,)),
            pltpu.SemaphoreType.DMA((no,))])
    return pl.pallas_call(
        body, name="gather_in_proj", grid_spec=grid_spec,
        out_shape=[SDS((n, n_chip * pw), BF16), SDS((n, d), BF16), SDS((d, n_chip * pw), BF16)] + [
            SDS((N_DEV,) + s.shape, BF16) for s in wo_shards],
        compiler_params=pltpu.CompilerParams(dimension_semantics=("arbitrary", "arbitrary"),
                                             vmem_limit_bytes=VMEM_LIMIT),
    )(my_slot, x2d, norm_in, w_in_sh, *wo_shards)


N_CHIP = N_DEV // 2
CHIP_FLIPS = (3, 2, 1, 0)


def _owner_at(mine, j):
    flip = 0
    for pair, f in enumerate(CHIP_FLIPS):
        flip = jnp.where(j // 2 == pair, f, flip)
    return 2 * jnp.bitwise_xor(mine // 2, flip) + j % 2


def _dw_in_exchange(h, dproj, my_slot, packed):
    n, d = h.shape
    esh = dproj.shape[1] // N_DEV
    tk = _tile(n, 2048)
    nk = n // tk
    last_j = N_DEV - 1

    def body(me_ref, h_ref, dp_ref, pk_in, win_out, pk_out,
             acc, halfbuf, recvbuf, sendbuf, half_send, half_recv, win_send, win_recv,
             send_sems, recv_sems, local_sems):
        del me_ref
        j, k = pl.program_id(0), pl.program_id(1)
        x, y, c = _me()
        me, sibling = (x, y, c), (x, y, 1 - c)
        mine = _slot(me)
        my_chip = mine // 2

        def pack_copies():
            local = pltpu.make_async_copy(pk_in, pk_out.at[mine], local_sems.at[0])
            remote = [pltpu.make_async_remote_copy(
                src_ref=pk_in, dst_ref=pk_out.at[mine], send_sem=send_sems.at[kk - 1], recv_sem=recv_sems.at[kk - 1],
                device_id=_peer(me, kk), device_id_type=MESH_ID) for kk in range(1, N_DEV)]
            return local, remote

        def half_copy(jj):
            slot = (jj // 2) % 2
            return pltpu.make_async_remote_copy(
                src_ref=halfbuf.at[slot], dst_ref=recvbuf.at[slot],
                send_sem=half_send.at[slot], recv_sem=half_recv.at[slot],
                device_id=sibling, device_id_type=MESH_ID)

        def chip_copy(jj):
            slot = (jj // 2) % 2
            owner = _owner_at(mine, jj)
            return pltpu.make_async_remote_copy(
                src_ref=sendbuf.at[slot], dst_ref=win_out.at[my_chip],
                send_sem=win_send.at[slot], recv_sem=win_recv.at[my_chip],
                device_id=(owner // 4, (owner // 2) % 2, owner % 2), device_id_type=MESH_ID)

        def own_copy():
            return pltpu.make_async_copy(sendbuf.at[(last_j // 2) % 2], win_out.at[my_chip], local_sems.at[1])

        @pl.when(jnp.logical_and(j == 0, k == 0))
        def _():
            local, remote = pack_copies()
            for cp in [local] + remote:
                cp.start()

        @pl.when(k == 0)
        def _():
            acc[...] = jnp.zeros_like(acc)

        acc[...] += _dot_tn(dp_ref[...], h_ref[...])

        done = k == nk - 1
        combine = j % 2 == c
        slot = (j // 2) % 2

        @pl.when(jnp.logical_and(done, jnp.logical_not(combine)))
        def _():
            @pl.when(j >= 4)
            def _():
                half_copy(j - 4).wait_send()

            halfbuf[slot] = acc[...].astype(BF16)
            half_copy(j).start()

        @pl.when(jnp.logical_and(done, combine))
        def _():
            half_copy(j).wait_recv()

            @pl.when(j >= 4)
            def _():
                chip_copy(j - 4).wait_send()

            sendbuf[slot] = (acc[...] + recvbuf[slot].astype(F32)).astype(BF16)

            @pl.when(j < last_j - 1)
            def _():
                chip_copy(j).start()

            @pl.when(j >= last_j - 1)
            def _():
                own_copy().start()

        @pl.when(jnp.logical_and(j == last_j, done))
        def _():
            half_copy(5 - c).wait_send()
            half_copy(7 - c).wait_send()
            chip_copy(4 + c).wait_send()
            own_copy().wait()
            for chip in range(N_CHIP):
                @pl.when(chip != my_chip)
                def _():
                    landed = win_out.at[chip]
                    pltpu.make_async_remote_copy(
                        src_ref=landed, dst_ref=landed, send_sem=win_send.at[0], recv_sem=win_recv.at[chip],
                        device_id=me, device_id_type=MESH_ID).wait_recv()
            local, remote = pack_copies()
            for cp in remote:
                cp.wait_send()
            for kk in range(1, N_DEV):
                landed = pk_out.at[_slot(_peer(me, kk))]
                pltpu.make_async_remote_copy(
                    src_ref=landed, dst_ref=landed, send_sem=send_sems.at[kk - 1], recv_sem=recv_sems.at[kk - 1],
                    device_id=me, device_id_type=MESH_ID).wait_recv()
            local.wait()

    any_spec = pl.BlockSpec(memory_space=pl.ANY)
    grid_spec = pltpu.PrefetchScalarGridSpec(
        num_scalar_prefetch=1, grid=(N_DEV, nk),
        in_specs=[pl.BlockSpec((tk, d), lambda j, k, me: (k, 0)),
                  pl.BlockSpec((tk, esh), lambda j, k, me: (k, _owner_at(me[0], j))), any_spec],
        out_specs=[any_spec] * 2,
        scratch_shapes=[pltpu.VMEM((esh, d), F32)] + [pltpu.VMEM((2, esh, d), BF16)] * 3 + [
            pltpu.SemaphoreType.DMA((2,)), pltpu.SemaphoreType.DMA((2,)),
            pltpu.SemaphoreType.DMA((2,)), pltpu.SemaphoreType.DMA((N_CHIP,)),
            pltpu.SemaphoreType.DMA((N_DEV - 1,)), pltpu.SemaphoreType.DMA((N_DEV - 1,)),
            pltpu.SemaphoreType.DMA((2,))])
    return pl.pallas_call(
        body, name="dw_in_exchange", grid_spec=grid_spec,
        out_shape=[SDS((N_CHIP, esh, d), BF16), SDS((N_DEV,) + packed.shape, packed.dtype)],
        compiler_params=_params(("arbitrary", "arbitrary")),
    )(my_slot, h, dproj, packed)


def _finish_small(packs, late_packs, states, groups, chunk):
    gc = groups * chunk
    nw = len(states)

    def body(p_ref, l_ref, *refs):
        st = refs[:3 * nw]
        loss_ref = refs[3 * nw]
        outs = refs[3 * nw + 1:]
        row, col = _iotas(chunk)
        tril = col <= row

        def total(ref, rs):
            tot = ref[0, rs, :]
            for dev in range(1, N_DEV):
                tot = tot + ref[dev, rs, :]
            return tot

        def update(k, rs_w, g):
            w_ref, m_ref, v_ref = st[3 * k:3 * k + 3]
            _adamw_outputs(*[o.at[rs_w] for o in outs[4 * k:4 * k + 4]], g, w_ref[rs_w, :], m_ref[rs_w, :], v_ref[rs_w, :])

        for g in range(groups):
            rs = slice(g * chunk, (g + 1) * chunk)
            update(0, rs, jnp.where(tril, total(p_ref, rs), 0.0))
        slab = lambda k: slice(gc + k * SUBLANE, gc + (k + 1) * SUBLANE)
        for k in range(3):
            update(1 + k, slice(0, SUBLANE), total(p_ref, slab(k)))
        loss_ref[...] = jnp.full((SUBLANE, LANE), jnp.sum(total(p_ref, slab(3))), F32)
        update(4, slice(0, SUBLANE), total(l_ref, slice(0, SUBLANE)))

    flat = [a for s in states for a in s]
    vmem = pl.BlockSpec(memory_space=pltpu.VMEM)
    res = pl.pallas_call(
        body, name="finish_small",
        out_shape=[SDS((SUBLANE, LANE), F32)] + [SDS(s[0].shape, F32) for s in states for _ in range(4)],
        in_specs=[vmem] * (2 + len(flat)),
        out_specs=[vmem] * (1 + 4 * nw),
        compiler_params=pltpu.CompilerParams(vmem_limit_bytes=VMEM_LIMIT),
    )(packs, late_packs, *flat)
    return res[0], [res[1 + 4 * k:5 + 4 * k] for k in range(nw)]


def _branch_a_fwd(proj, norm_v, w_s, b_col):
    n = proj.shape[0]
    d = norm_v.shape[1]
    groups, chunk, _ = w_s.shape
    tr = _tile(n, 8 * chunk)

    def body(u_ref, v_ref, z_ref, gv_ref, ws_ref, b_ref, ya_ref, vn_s, pre_s):
        row, col = _iotas(chunk)
        tril = col <= row
        vg = _gelu(v_ref[...])[0].astype(F32)
        vn_s[...] = (vg * _rms_scale(vg) * gv_ref[...]).astype(BF16)
        pre_s[...] = _gelu(u_ref[...])[0] * _silu(z_ref[...])[0]
        for g in range(groups):
            wm = jnp.where(tril, ws_ref[g], 0.0).astype(BF16)
            cs = slice(g * chunk, (g + 1) * chunk)
            for c in range(tr // chunk):
                rs = slice(c * chunk, (c + 1) * chunk)
                mixed = _dot(wm, vn_s[rs, cs]) + b_ref[g]
                ya_ref[rs, cs] = (pre_s[rs, cs].astype(F32) * mixed).astype(BF16)

    seg = lambda k: pl.BlockSpec((tr, d), lambda i: (i, k))
    return pl.pallas_call(
        body, name="branch_a_fwd", grid=(n // tr,),
        in_specs=[seg(0), seg(1), seg(2),
                  pl.BlockSpec((1, d), lambda i: (0, 0)),
                  pl.BlockSpec((groups, chunk, chunk), lambda i: (0, 0, 0)),
                  pl.BlockSpec((groups, chunk, 1), lambda i: (0, 0, 0))],
        out_specs=pl.BlockSpec((tr, d), lambda i: (i, 0)),
        out_shape=SDS((n, d), BF16),
        scratch_shapes=[pltpu.VMEM((tr, d), BF16), pltpu.VMEM((tr, d), BF16)],
        compiler_params=_params(("parallel",)),
    )(proj, proj, proj, norm_v, w_s, b_col)


def _sb_fwd(proj, batch, seq, d, hd):
    heads = d // hd
    t = _tile(seq, SB_TILE)
    sw = _tile(t, SB_SCAN)
    nb = t // sw
    scale = hd ** -0.5
    nblk = seq // t
    nh = SB_HEADS
    wide = nh * hd
    cols = [slice(hh * hd, (hh + 1) * hd) for hh in range(nh)]

    def body(qs, k_ref, vs, zb_ref, yb_ref, o_ref, tot_ref, kts, later, acc):
        for jb in range(nblk):
            kts[jb] = k_ref[jb * t:(jb + 1) * t, :].T
        row, col = _iotas(t)
        later[...] = (row[:sw, :sw] > col[:sw, :sw]).astype(BF16)

        def qblock(i, carry):
            r0 = pl.multiple_of(i * t, t)

            def tile(j, runs):
                c0 = pl.multiple_of(j * t, t)
                logs = [_sb_logs(_dot(qs[pl.ds(r0, t), cs], kts[j, cs, :]), scale, None) for cs in cols]
                scans = [_dot(jnp.concatenate([logs[hh][1][:, b * sw:(b + 1) * sw] for b in range(nb)], axis=0),
                              later[...]) for hh in range(nh)]
                new_runs = []
                for hh in range(nh):
                    after = runs[hh]
                    blocks = [None] * nb
                    for b in reversed(range(nb)):
                        ks_ = slice(b * sw, (b + 1) * sw)
                        inside = scans[hh][b * t:(b + 1) * t]
                        blocks[b] = jnp.exp(logs[hh][0][:, ks_].astype(F32) + inside + after).astype(BF16)
                        after = after + inside[:, 0:1] + logs[hh][1][:, b * sw:b * sw + 1].astype(F32)
                    new_runs.append(after)
                    acc[:, cols[hh]] += _dot(jnp.concatenate(blocks, axis=1), vs[pl.ds(c0, t), cols[hh]])
                return tuple(new_runs)

            def diagonal_tile():
                starts = [b * sw for b in range(nb)]
                logs = [[_sb_logs(_dot(qs[pl.ds(r0 + s, t - s), cs], kts[i, cs, s:s + sw]), scale,
                                  col[:t - s, :sw] < row[:t - s, :sw]) for s in starts] for cs in cols]
                scans = [_dot(jnp.concatenate([lr for _, lr in logs[hh]], axis=0), later[...]) for hh in range(nh)]
                new_runs = []
                offs = [sum(t - s for s in starts[:b]) for b in range(nb)]
                for hh in range(nh):
                    after = jnp.zeros((t, 1), F32)
                    ws = [None] * nb
                    for b in reversed(range(nb)):
                        s = starts[b]
                        lb, lr = logs[hh][b]
                        inside = scans[hh][offs[b]:offs[b] + t - s]
                        ws[b] = jnp.exp(lb.astype(F32) + inside + after[s:]).astype(BF16)
                        total = inside[:, 0:1] + lr[:, 0:1].astype(F32)
                        after = after + total if s == 0 else jnp.concatenate([after[:s], after[s:] + total], axis=0)
                    new_runs.append(after)
                    acc[:, cols[hh]] = _dot(ws[0], vs[pl.ds(r0, sw), cols[hh]])
                    for b in range(1, nb):
                        acc[starts[b]:, cols[hh]] += _dot(ws[b], vs[pl.ds(r0 + starts[b], sw), cols[hh]])
                return tuple(new_runs)

            runs = diagonal_tile()
            runs = lax.fori_loop(0, i, lambda jj, rs: tile(i - 1 - jj, rs), runs)
            for hh in range(nh):
                out = acc[:, cols[hh]]
                o_ref[pl.ds(r0, t), cols[hh]] = out.astype(BF16)
                tot_ref[hh, pl.ds(r0, t), :] = runs[hh]
                sz, _ = _silu(zb_ref[pl.ds(r0, t), cols[hh]].astype(F32))
                yb_ref[pl.ds(r0, t), cols[hh]] = (out * sz).astype(BF16)
            return carry

        lax.fori_loop(0, nblk, qblock, 0)

    col0 = d // wide
    seg = lambda k: pl.BlockSpec((seq, wide), lambda b, h: (b, k * col0 + h))
    return pl.pallas_call(
        body, name="sb_fwd", grid=(batch, heads // nh),
        in_specs=[seg(3), seg(4), seg(5), seg(6)],
        out_specs=[pl.BlockSpec((seq, wide), lambda b, h: (b, h))] * 2 + [
            pl.BlockSpec((nh, seq, 1), lambda b, h: (b * (heads // nh) + h, 0, 0))],
        out_shape=[SDS((batch * seq, d), BF16), SDS((batch * seq, d), BF16), SDS((batch * heads, seq, 1), F32)],
        scratch_shapes=[pltpu.VMEM((nblk, wide, t), BF16), pltpu.VMEM((sw, sw), BF16), pltpu.VMEM((t, wide), F32)],
        compiler_params=_params(("parallel", "parallel")),
    )(proj, proj, proj, proj)


def _tail(x2d, tgt, ya, yb, proj, w_oa, w_ob, w_out, norm_final):
    n, d = x2d.shape
    e = proj.shape[1]
    tm = _tile(n, 512)
    steps = n // tm

    def body(x_ref, t_ref, ya_ref, yb_ref, ga_ref, gb_ref, woa_ref, wob_ref, wout_ref, gf_ref,
             dproj_ref, dx2_ref, dya_ref, dyb_ref, mrg_ref, dpa_ref, dpb_ref, loss_ref, dgf_ref, dg_s, dg_sems):
        i = pl.program_id(0)

        def gate_copy(step):
            rows_ = pl.ds(pl.multiple_of(step * tm, tm), tm)
            return pltpu.make_async_copy(dg_s.at[step % 2], dproj_ref.at[rows_, pl.ds(7 * d, 2 * d)],
                                         dg_sems.at[step % 2])

        @pl.when(i == 0)
        def _():
            loss_ref[...] = jnp.zeros_like(loss_ref)
            dgf_ref[...] = jnp.zeros_like(dgf_ref)

        @pl.when(i >= 2)
        def _():
            gate_copy(i - 2).wait()

        halves = [slice(hf * (tm // 2), (hf + 1) * (tm // 2)) for hf in range(2)] if tm >= 512 else [slice(0, tm)]
        gf = gf_ref[...]
        pa = [_dot(ya_ref[rs, :], woa_ref[...]) for rs in halves]
        pb = [_dot(yb_ref[rs, :], wob_ref[...]) for rs in halves]
        sa = [_sigmoid(ga_ref[rs, :].astype(F32)) for rs in halves]
        sb = [_sigmoid(gb_ref[rs, :].astype(F32)) for rs in halves]
        merged = [(sa[k] * pa[k] + sb[k] * pb[k]).astype(BF16) for k in range(len(halves))]
        for k, rs in enumerate(halves):
            mrg_ref[rs, :] = merged[k]
        x2 = [x_ref[rs, :] + _dot(merged[k], wout_ref[...]) for k, rs in enumerate(halves)]
        dx2 = []
        for k, rs in enumerate(halves):
            r2 = _rms_scale(x2[k])
            xh = x2[k] * r2
            diff = xh * gf - t_ref[rs, :]
            loss_ref[...] += jnp.sum(diff * diff, axis=0, keepdims=True) * (0.5 / d)
            dy = diff * (1.0 / d)
            dgf_ref[...] += jnp.sum(dy * xh, axis=0, keepdims=True)
            dxh = dy * gf
            dx2.append(r2 * (dxh - xh * jnp.mean(dxh * xh, axis=-1, keepdims=True)))
            dx2_ref[rs, :] = dx2[k]
        dm = [_dot_nt(dx2[k].astype(BF16), wout_ref[...]) for k in range(len(halves))]
        dpa, dpb = [], []
        for k, rs in enumerate(halves):
            dpa.append((dm[k] * sa[k]).astype(BF16))
            dpb.append((dm[k] * sb[k]).astype(BF16))
            dpa_ref[rs, :] = dpa[k]
            dpb_ref[rs, :] = dpb[k]
            dg_s[i % 2, rs, 0:d] = (dm[k] * pa[k] * (sa[k] * (1.0 - sa[k]))).astype(BF16)
            dg_s[i % 2, rs, d:2 * d] = (dm[k] * pb[k] * (sb[k] * (1.0 - sb[k]))).astype(BF16)
        gate_copy(i).start()
        for k, rs in enumerate(halves):
            dya_ref[rs, :] = _dot_nt(dpa[k], woa_ref[...]).astype(BF16)
        for k, rs in enumerate(halves):
            dyb_ref[rs, :] = _dot_nt(dpb[k], wob_ref[...]).astype(BF16)

        @pl.when(i == steps - 1)
        def _():
            if steps >= 2:
                gate_copy(i - 1).wait()
            gate_copy(i).wait()

    rows = lambda k=0: pl.BlockSpec((tm, d), lambda i: (i, k))
    full = pl.BlockSpec((d, d), lambda i: (0, 0), pipeline_mode=pl.Buffered(1))
    vec = pl.BlockSpec((1, d), lambda i: (0, 0))
    return pl.pallas_call(
        body, name="tail", grid=(steps,),
        in_specs=[rows(), rows(), rows(), rows(), rows(7), rows(8), full, full, full, vec],
        out_specs=[pl.BlockSpec(memory_space=pl.ANY),
                   rows(), rows(), rows(), rows(), rows(), rows(), vec, vec],
        out_shape=[SDS((n, e), BF16), SDS((n, d), F32), SDS((n, d), BF16), SDS((n, d), BF16),
                   SDS((n, d), BF16), SDS((n, d), BF16), SDS((n, d), BF16),
                   SDS((1, d), F32), SDS((1, d), F32)],
        scratch_shapes=[pltpu.VMEM((2, tm, 2 * d), BF16), pltpu.SemaphoreType.DMA((2,))],
        compiler_params=_params(("arbitrary",)),
    )(x2d, tgt, ya, yb, proj, proj, w_oa, w_ob, w_out, norm_final)


def _dw_o(pairs):
    n, d = pairs[0][0].shape
    tk = _tile(n, 1024)
    nk = n // tk
    npair = len(pairs)

    def body(*refs):
        a_refs, b_refs = refs[:npair], refs[npair:2 * npair]
        o_ref, acc = refs[2 * npair], refs[2 * npair + 1]
        p, k = pl.program_id(0), pl.program_id(1)

        @pl.when(k == 0)
        def _():
            acc[...] = jnp.zeros_like(acc)

        for q in range(npair):
            @pl.when(p == q)
            def _():
                acc[...] += _dot_tn(a_refs[q][...], b_refs[q][...].astype(BF16))

        @pl.when(k == nk - 1)
        def _():
            o_ref[0] = acc[...].astype(BF16)

    def tiles(q):
        return pl.BlockSpec((tk, d), lambda p, k: (jnp.where(p == q, k, jnp.where(p < q, 0, nk - 1)), 0))

    return pl.pallas_call(
        body, name="dw_o", grid=(npair, nk),
        in_specs=[tiles(q) for q in range(npair)] * 2,
        out_specs=pl.BlockSpec((1, d, d), lambda p, k: (p, 0, 0)),
        out_shape=SDS((npair, d, d), BF16),
        scratch_shapes=[pltpu.VMEM((d, d), F32)],
        compiler_params=_params(("arbitrary", "arbitrary")),
    )(*[a for a, _ in pairs], *[b for _, b in pairs])


def _sb_bwd(proj, o, dyb, tot, dproj, dw_stack, packed, batch, seq, d, hd):
    heads = d // hd
    t = _tile(seq, SB_TILE_BWD)
    sw = _tile(t, SB_SCAN)
    nb = t // sw
    scale = hd ** -0.5
    nblk = seq // t
    nh = SB_HEADS
    wide = nh * hd
    hs = range(nh)
    cols = [slice(hh * hd, (hh + 1) * hd) for hh in hs]
    blocks = [slice(b * sw, (b + 1) * sw) for b in range(nb)]
    last = slice(sw - 1, sw)

    def compute(qs, ks, v_ref, zb_ref, o_ref, dyb_ref, tot_ref, kts, vts, dos, dzb, dq_all, dkv_t, qt_s, dot_s,
                upto, before, dq):
        for jb in range(nblk):
            rows = slice(jb * t, (jb + 1) * t)
            kts[jb] = ks[rows, :].T
            vts[jb] = v_ref[rows, :].T
        sz, dsz = _silu(zb_ref[...])
        dyb_v = dyb_ref[...]
        dos[...] = dyb_v * sz
        dzb[...] = dyb_v * o_ref[...] * dsz
        row, col = _iotas(t)
        upto[...] = (row[:sw, :sw] <= col[:sw, :sw]).astype(BF16)
        before[...] = (row[:sw, :sw] < col[:sw, :sw]).astype(BF16)

        def qblock(i, carry):
            r0 = pl.multiple_of(i * t, t)

            def tile(j, sums):
                c0 = pl.multiple_of(j * t, t)
                q_i = [qs[pl.ds(r0, t), cs] for cs in cols]
                do_i = [dos[pl.ds(r0, t), cs] for cs in cols]
                logs = [_sb_logs(_dot(q_i[hh], kts[j, cols[hh], :]), scale, None) for hh in hs]
                scans = [_dot(jnp.concatenate([logs[hh][1][:, ks_] for ks_ in blocks], axis=0), upto[...]) for hh in hs]
                dw = [_dot(do_i[hh], vts[j, cols[hh], :]) for hh in hs]
                ws, gs, new_runs = [], [], []
                for hh in hs:
                    left = tot_ref[hh, pl.ds(r0, t), :] - sums[hh][0]
                    w_b, g_b = [], []
                    for b, ks_ in enumerate(blocks):
                        inside = scans[hh][b * t:(b + 1) * t]
                        w = jnp.exp(logs[hh][0][:, ks_].astype(F32) + (left - inside))
                        w_b.append(w.astype(BF16))
                        g_b.append((dw[hh][:, ks_] * w).astype(BF16))
                        left = left - inside[:, last]
                    ws.append(jnp.concatenate(w_b, axis=1))
                    gs.append(g_b)
                    new_runs.append(tot_ref[hh, pl.ds(r0, t), :] - left)
                gscans = [_dot(jnp.concatenate(gs[hh], axis=0), before[...]) for hh in hs]
                dzs, new_gruns = [], []
                for hh in hs:
                    g_before = sums[hh][1]
                    dz_b = []
                    for b, ks_ in enumerate(blocks):
                        inside = gscans[hh][b * t:(b + 1) * t]
                        beta = jnp.exp(logs[hh][0][:, ks_]).astype(F32)
                        g = gs[hh][b].astype(F32)
                        dz_b.append((g - (g + inside + g_before) * beta).astype(BF16))
                        g_before = g_before + inside[:, last] + g[:, last]
                    dzs.append(jnp.concatenate(dz_b, axis=1))
                    new_gruns.append(g_before)
                for hh in hs:
                    dkv_t[1, j, cols[hh], :] += _dot(dot_s[cols[hh], :], ws[hh])
                for hh in hs:
                    dkv_t[0, j, cols[hh], :] += _dot(qt_s[cols[hh], :], dzs[hh])
                for hh in hs:
                    dq[:, cols[hh]] += _dot(dzs[hh], ks[pl.ds(c0, t), cols[hh]])
                return tuple((new_runs[hh], new_gruns[hh]) for hh in hs)

            def diagonal_tile(sums):
                starts = [b * sw for b in range(nb)]
                offs = [sum(t - s for s in starts[:b]) for b in range(nb)]
                q_b = [[qs[pl.ds(r0 + s, t - s), cs] for s in starts] for cs in cols]
                do_b = [[dos[pl.ds(r0 + s, t - s), cs] for s in starts] for cs in cols]
                logs = [[_sb_logs(_dot(q_b[hh][b], kts[i, cols[hh], s:s + sw]), scale,
                                  col[:t - s, :sw] < row[:t - s, :sw]) for b, s in enumerate(starts)] for hh in hs]
                dw = [[_dot(do_b[hh][b], vts[i, cols[hh], s:s + sw]) for b, s in enumerate(starts)] for hh in hs]
                scans = [_dot(jnp.concatenate([lr for _, lr in logs[hh]], axis=0), upto[...]) for hh in hs]
                ws, gs = [], []
                for hh in hs:
                    left = tot_ref[hh, pl.ds(r0, t), :] - sums[hh][0]
                    w_b, g_b = [], []
                    for b, s in enumerate(starts):
                        inside = scans[hh][offs[b]:offs[b] + t - s]
                        w = jnp.exp(logs[hh][b][0].astype(F32) + (left[s:] - inside))
                        w_b.append(w.astype(BF16))
                        g_b.append((dw[hh][b] * w).astype(BF16))
                        total = inside[:, last]
                        left = left - total if s == 0 else jnp.concatenate([left[:s], left[s:] - total], axis=0)
                    ws.append(w_b)
                    gs.append(g_b)
                gscans = [_dot(jnp.concatenate(gs[hh], axis=0), before[...]) for hh in hs]
                dzs = []
                for hh in hs:
                    g_before = sums[hh][1]
                    dz_b = []
                    for b, s in enumerate(starts):
                        inside = gscans[hh][offs[b]:offs[b] + t - s]
                        beta = jnp.exp(logs[hh][b][0]).astype(F32)
                        g = gs[hh][b].astype(F32)
                        dz_b.append((g - (g + inside + g_before[s:]) * beta).astype(BF16))
                        total = inside[:, last] + g[:, last]
                        g_before = g_before + total if s == 0 else jnp.concatenate(
                            [g_before[:s], g_before[s:] + total], axis=0)
                    dzs.append(dz_b)
                for hh in hs:
                    for b, s in enumerate(starts):
                        dkv_t[1, i, cols[hh], s:s + sw] = _dot(dot_s[cols[hh], s:], ws[hh][b])
                for hh in hs:
                    for b, s in enumerate(starts):
                        dkv_t[0, i, cols[hh], s:s + sw] = _dot(qt_s[cols[hh], s:], dzs[hh][b])
                for hh in hs:
                    for b, s in enumerate(starts):
                        dq[s:, cols[hh]] += _dot(dzs[hh][b], ks[pl.ds(r0 + s, sw), cols[hh]])

            qt_s[...] = qs[pl.ds(r0, t), :].T
            dot_s[...] = dos[pl.ds(r0, t), :].T
            zero = jnp.zeros((t, 1), F32)
            dq[...] = jnp.zeros_like(dq)
            sums = lax.fori_loop(0, i, tile, ((zero, zero),) * nh)
            diagonal_tile(sums)
            dq_all[pl.ds(r0, t), :] = dq[...]
            return carry

        lax.fori_loop(0, nblk, qblock, 0)

    pairs = heads // nh

    nst = dw_stack.shape[0]
    ns = nst + 1

    def body(qs, ks, v_ref, zb_ref, o_ref, dyb_ref, tot_ref, dproj_in, dw_ref, pk_ref, out_ref, *refs):
        del dproj_in
        st_in = [dw_ref.at[k] for k in range(nst)] + [pk_ref]
        st_out = refs[:ns]
        (kts, vts, dos, dzb, dq_all, dkv_t, qt_s, dot_s, upto, before, dq, stage, stage_sems,
         send_sems, recv_sems, local_sems) = refs[ns:]
        step = pl.program_id(0) * pairs + pl.program_id(1)
        exchange = functools.partial(_stack_exchange, _me(), st_in, st_out, 1, send_sems, recv_sems, local_sems)

        @pl.when(step == 0)
        def _():
            local, remote, _ = exchange(arrivals=False)
            for cp in local + remote:
                cp.start()

        def out_copies(s):
            rows_ = pl.ds(pl.multiple_of((s // pairs) * seq, seq), seq)
            return [pltpu.make_async_copy(
                stage.at[k], out_ref.at[rows_, pl.ds(pl.multiple_of((3 + k) * d + (s % pairs) * wide, wide), wide)],
                stage_sems.at[k]) for k in range(4)]

        compute(qs, ks, v_ref, zb_ref, o_ref, dyb_ref, tot_ref, kts, vts, dos, dzb, dq_all, dkv_t, qt_s, dot_s,
                upto, before, dq)

        @pl.when(step > 0)
        def _():
            for cp in out_copies(step - 1):
                cp.wait()

        stage[0] = (dq_all[...] * scale).astype(BF16)
        for jb in range(nblk):
            stage[1, jb * t:(jb + 1) * t, :] = (dkv_t[0, jb] * scale).astype(BF16).T
            stage[2, jb * t:(jb + 1) * t, :] = dkv_t[1, jb].astype(BF16).T
        stage[3] = dzb[...]
        for cp in out_copies(step):
            cp.start()

        @pl.when(step == batch * pairs - 1)
        def _():
            for cp in out_copies(step):
                cp.wait()
            local, remote, landed = exchange()
            for cp in remote:
                cp.wait_send()
            for cp in landed:
                cp.wait_recv()
            for cp in local:
                cp.wait()

    col0 = d // wide
    seg = lambda k: pl.BlockSpec((seq, wide), lambda b, h: (b, k * col0 + h))
    head = pl.BlockSpec((seq, wide), lambda b, h: (b, h))
    any_spec = pl.BlockSpec(memory_space=pl.ANY)
    return pl.pallas_call(
        body, name="sb_bwd", grid=(batch, pairs),
        in_specs=[seg(3), seg(4), seg(5), seg(6), head, head,
                  pl.BlockSpec((nh, seq, 1), lambda b, h: (b * pairs + h, 0, 0))] + [any_spec] * 3,
        out_specs=[any_spec] * (ns + 1),
        out_shape=[SDS(dproj.shape, dproj.dtype)] + [SDS(dw_stack.shape[1:], dw_stack.dtype)] * nst + [
            SDS((N_DEV,) + packed.shape, packed.dtype)],
        input_output_aliases={7: 0},
        scratch_shapes=[pltpu.VMEM((nblk, wide, t), BF16)] * 2 + [
            pltpu.VMEM((seq, wide), BF16), pltpu.VMEM((seq, wide), BF16),
            pltpu.VMEM((seq, wide), F32), pltpu.VMEM((2, nblk, wide, t), F32),
            pltpu.VMEM((wide, t), BF16), pltpu.VMEM((wide, t), BF16),
            pltpu.VMEM((sw, sw), BF16), pltpu.VMEM((sw, sw), BF16), pltpu.VMEM((t, wide), F32),
            pltpu.VMEM((4, seq, wide), BF16), pltpu.SemaphoreType.DMA((4,)),
            pltpu.SemaphoreType.DMA((7 * ns,)), pltpu.SemaphoreType.DMA((7 * ns,)),
            pltpu.SemaphoreType.DMA((ns,))],
        compiler_params=_params(("arbitrary", "arbitrary")),
    )(proj, proj, proj, proj, o, dyb, tot, dproj, dw_stack, packed)


def _branch_a_bwd(proj, dya, norm_v, w_s, b_col, dproj):
    n = proj.shape[0]
    d = norm_v.shape[1]
    groups, chunk, _ = w_s.shape
    tr = _tile(n, 4 * chunk)

    def body(u_ref, v_ref, z_ref, dya_ref, gv_ref, ws_ref, b_ref, dproj_in,
             out_ref, dws_ref, dbias_ref, dgv_ref, vn_s, dmix_s, dvn_s, db_ref):
        del dproj_in

        @pl.when(pl.program_id(0) == 0)
        def _():
            dws_ref[...] = jnp.zeros_like(dws_ref)
            db_ref[...] = jnp.zeros_like(db_ref)
            dgv_ref[...] = jnp.zeros_like(dgv_ref)

        row, col = _iotas(chunk)
        tril = col <= row
        gv = gv_ref[...]
        vg16, dvg_dv = _gelu(v_ref[...])
        vg = vg16.astype(F32)
        r = _rms_scale(vg)
        vh = vg * r
        vn_s[...] = (vh * gv).astype(BF16)
        ug, dug_du = _gelu(u_ref[...])
        sz, dsz = _silu(z_ref[...])
        dya_v = dya_ref[...]
        dmix_s[...] = dya_v * ug * sz
        du_scale = sz * dug_du
        dz_scale = ug * dsz
        for g in range(groups):
            wm = jnp.where(tril, ws_ref[g], 0.0).astype(BF16)
            cs = slice(g * chunk, (g + 1) * chunk)
            for c in range(tr // chunk):
                rs = slice(c * chunk, (c + 1) * chunk)
                vn = vn_s[rs, cs]
                mixed = _dot(wm, vn) + b_ref[g]
                dmix16 = dmix_s[rs, cs]
                dws_ref[g] += _dot_nt(dmix16, vn)
                db_ref[g] += dmix16.astype(F32)
                dvn_s[rs, cs] = _dot_tn(wm, dmix16)
                t_u = dya_v[rs, cs] * mixed.astype(BF16)
                out_ref[rs, g * chunk:(g + 1) * chunk] = t_u * du_scale[rs, cs]
                out_ref[rs, 2 * d + g * chunk:2 * d + (g + 1) * chunk] = t_u * dz_scale[rs, cs]
        dvn = dvn_s[...]
        dgv_ref[...] += jnp.sum(dvn * vh, axis=0, keepdims=True)
        dvh = dvn * gv
        dvg = r * (dvh - vh * jnp.mean(dvh * vh, axis=-1, keepdims=True))
        out_ref[:, d:2 * d] = (dvg * dvg_dv.astype(F32)).astype(BF16)

        @pl.when(pl.program_id(0) == n // tr - 1)
        def _():
            for g in range(groups):
                dbias_ref[g:g + 1, :] = jnp.sum(db_ref[g].T, axis=0, keepdims=True)

    seg = lambda k: pl.BlockSpec((tr, d), lambda i: (i, k))
    return pl.pallas_call(
        body, name="branch_a_bwd", grid=(n // tr,),
        in_specs=[seg(0), seg(1), seg(2), seg(0),
                  pl.BlockSpec((1, d), lambda i: (0, 0)),
                  pl.BlockSpec((groups, chunk, chunk), lambda i: (0, 0, 0)),
                  pl.BlockSpec((groups, chunk, 1), lambda i: (0, 0, 0)),
                  pl.BlockSpec(memory_space=pl.ANY)],
        out_specs=[pl.BlockSpec((tr, 3 * d), lambda i: (i, 0)),
                   pl.BlockSpec((groups, chunk, chunk), lambda i: (0, 0, 0)),
                   pl.BlockSpec((groups, chunk), lambda i: (0, 0)),
                   pl.BlockSpec((1, d), lambda i: (0, 0))],
        out_shape=[SDS(dproj.shape, dproj.dtype), SDS((groups, chunk, chunk), F32),
                   SDS((groups, chunk), F32), SDS((1, d), F32)],
        input_output_aliases={7: 0},
        scratch_shapes=[pltpu.VMEM((tr, d), BF16), pltpu.VMEM((tr, d), BF16), pltpu.VMEM((tr, d), F32),
                        pltpu.VMEM((groups, chunk, chunk), F32)],
        compiler_params=_params(("arbitrary",)),
    )(proj, proj, proj, dya, norm_v, w_s, b_col, dproj)


def _dx(dproj, wg_in, x2d, dx2, norm_in):
    n, d = x2d.shape
    nsh = N_DEV // 2
    esh = wg_in.shape[1] // nsh
    tm = _tile(n, 1024)

    def body(dp_ref, w_ref, x_ref, dx2_ref, g_ref, gx_ref, dg_ref, acc):
        i, k = pl.program_id(0), pl.program_id(1)

        @pl.when(jnp.logical_and(i == 0, k == 0))
        def _():
            dg_ref[...] = jnp.zeros_like(dg_ref)

        @pl.when(k == 0)
        def _():
            acc[...] = jnp.zeros_like(acc)

        acc[...] += _dot_nt(dp_ref[...], w_ref[...])

        @pl.when(k == nsh - 1)
        def _():
            dh = acc[...]
            x = x_ref[...]
            r = _rms_scale(x)
            xh = x * r
            dg_ref[...] += jnp.sum(dh * xh, axis=0, keepdims=True)
            dxh = dh * g_ref[...]
            gx_ref[...] = dx2_ref[...] + r * (dxh - xh * jnp.mean(dxh * xh, axis=-1, keepdims=True))

    rows = pl.BlockSpec((tm, d), lambda i, k: (i, 0))
    vec = pl.BlockSpec((1, d), lambda i, k: (0, 0))
    return pl.pallas_call(
        body, name="dx", grid=(n // tm, nsh),
        in_specs=[pl.BlockSpec((tm, esh), lambda i, k: (i, k)),
                  pl.BlockSpec((d, esh), lambda i, k: (0, k)), rows, rows, vec],
        out_specs=[rows, vec],
        out_shape=[SDS((n, d), F32), SDS((1, d), F32)],
        scratch_shapes=[pltpu.VMEM((tm, d), F32)],
        compiler_params=_params(("arbitrary", "arbitrary")),
    )(dproj, wg_in, x2d, dx2, norm_in)


def _adamw_outputs(g_ref, d_ref, m_ref, v_ref, g, w, m, v):
    delta, m2, v2 = _adamw(w, g, m, v)
    g_ref[...] = g
    d_ref[...] = delta
    m_ref[...] = m2
    v_ref[...] = v2


def _reduce_adamw(slots, w, m, v, name, transposed=False):
    r, c = w.shape
    ns = slots.shape[0]
    tr = _tile(r, 128)

    def body(s_ref, w_ref, m_ref, v_ref, g_out, d_out, m_out, v_out):
        g = s_ref[0].astype(F32)
        for k in range(1, ns):
            g = g + s_ref[k].astype(F32)
        if transposed:
            g = g.T
        _adamw_outputs(g_out, d_out, m_out, v_out, g, w_ref[...], m_ref[...], v_ref[...])

    blk = pl.BlockSpec((tr, c), lambda i: (i, 0))
    slot_blk = (pl.BlockSpec((ns, c, tr), lambda i: (0, 0, i)) if transposed
                else pl.BlockSpec((ns, tr, c), lambda i: (0, i, 0)))
    return pl.pallas_call(
        body, name=name, grid=(r // tr,),
        in_specs=[slot_blk, blk, blk, blk],
        out_specs=[blk] * 4,
        out_shape=[SDS((r, c), F32)] * 4,
        compiler_params=_params(("parallel",)),
    )(slots, w, m, v)


def kernel(x, norm_in, w_in, norm_v, w_s, b_s, w_o_gmlp, w_o_sb, w_out, norm_final, loss_target, m_norm_in, m_w_in, m_norm_v, m_w_s, m_b_s, m_w_o_gmlp, m_w_o_sb, m_w_out, m_norm_final, v_norm_in, v_w_in, v_norm_v, v_w_s, v_b_s, v_w_o_gmlp, v_w_o_sb, v_w_out, v_norm_final):
    batch, seq, d = x.shape
    n = batch * seq
    groups, chunk = w_s.shape[1], w_s.shape[2]
    hd = LANE
    x2d = x.reshape(n, d)
    tgt = loss_target.reshape(n, d)
    b_col = b_s[0].reshape(groups, chunk, 1)
    norm_final2 = norm_final.reshape(1, d)

    my_slot = _slot(_me()).astype(jnp.int32).reshape(1)
    proj, h, wg_in, wg_oa, wg_ob, wg_out = _gather_in_proj(
        x2d, norm_in, w_in[0], [w_o_gmlp[0], w_o_sb[0], w_out[0]], my_slot)
    rsh = wg_oa.shape[1]
    wf_oa, wf_ob, wf_out = (w.reshape(N_DEV * rsh, d) for w in (wg_oa, wg_ob, wg_out))
    ya = _branch_a_fwd(proj, norm_v, w_s[0], b_col)
    yb, o, sb_tot = _sb_fwd(proj, batch, seq, d, hd)
    dproj, dx2, dya, dyb, merged, dpa, dpb, loss_vec, dgf = _tail(
        x2d, tgt, ya, yb, proj, wf_oa, wf_ob, wf_out, norm_final2)
    gp_wo = _dw_o([(ya, dpa), (yb, dpb), (merged, dx2)])
    dproj, gp_ws, gp_b, gp_nv = _branch_a_bwd(proj, dya, norm_v, w_s[0], b_col, dproj)

    slab = lambda a: a.reshape(d // LANE, LANE)
    gc = groups * chunk
    packed = jnp.concatenate([gp_ws.reshape(gc, chunk), gp_b, slab(gp_nv), slab(dgf), slab(loss_vec)], axis=0)
    dproj, s_oa, s_ob, s_out, packs = _sb_bwd(
        proj, o, dyb, sb_tot, dproj, gp_wo.reshape(3, N_DEV, rsh, d), packed, batch, seq, d, hd)
    grad_x, gp_nin = _dx(dproj, wg_in, x2d, dx2, norm_in)
    s_win, late_packs = _dw_in_exchange(h, dproj, my_slot, slab(gp_nin))
    small = {"w_s": lambda a: a.reshape(gc, chunk), "b_s": lambda a: a[0], "norm_v": slab, "norm_final": slab,
             "norm_in": slab}
    given = {"w_s": (w_s, m_w_s, v_w_s), "b_s": (b_s, m_b_s, v_b_s), "norm_v": (norm_v, m_norm_v, v_norm_v),
             "norm_final": (norm_final, m_norm_final, v_norm_final), "norm_in": (norm_in, m_norm_in, v_norm_in)}
    loss_slab, small_res = _finish_small(
        packs, late_packs, [tuple(small[k](a) for a in given[k]) for k in small], groups, chunk)
    loss = loss_slab[0, 0]

    res = dict(zip(small, small_res))
    res["w_in"] = _reduce_adamw(s_win, w_in[0], m_w_in[0], v_w_in[0], "adamw_w_in", transposed=True)
    res["w_o_gmlp"] = _reduce_adamw(s_oa, w_o_gmlp[0], m_w_o_gmlp[0], v_w_o_gmlp[0], "adamw_w_o_gmlp")
    res["w_o_sb"] = _reduce_adamw(s_ob, w_o_sb[0], m_w_o_sb[0], v_w_o_sb[0], "adamw_w_o_sb")
    res["w_out"] = _reduce_adamw(s_out, w_out[0], m_w_out[0], v_w_out[0], "adamw_w_out")

    shapes = {"norm_in": norm_in.shape, "w_in": w_in.shape, "norm_v": norm_v.shape, "w_s": w_s.shape,
              "b_s": b_s.shape, "w_o_gmlp": w_o_gmlp.shape, "w_o_sb": w_o_sb.shape, "w_out": w_out.shape,
              "norm_final": norm_final.shape}
    names = list(shapes)
    outs = [loss, grad_x.reshape(batch, seq, d)]
    for kind in range(4):
        outs += [res[name][kind].reshape(shapes[name]) for name in names]
    return tuple(outs)
```

```python
import functools
import math

import jax
import jax.numpy as jnp
from jax import lax
from jax.experimental import pallas as pl
from jax.experimental.pallas import tpu as pltpu

F32 = jnp.float32
BF16 = jnp.bfloat16
SDS = jax.ShapeDtypeStruct
MESH_ID = pl.DeviceIdType.MESH

N_DEV = 8
LANE = 128
SUBLANE = 8
VMEM_LIMIT = 56 * 1024 * 1024
SB_TILE = 512
SB_TILE_BWD = 512
SB_SCAN = 256
SB_HEADS = 2
MASKED_LOG = -1e30
RMS_EPS = 1e-6

ADAM_LR = 0.001
ADAM_B1 = 0.9
ADAM_B2 = 0.999
ADAM_EPS = 1e-08
ADAM_WD = 0.01
ADAM_STEP = 10

NT_DIMS = (((1,), (1,)), ((), ()))
TN_DIMS = (((0,), (0,)), ((), ()))


def _params(semantics=None):
    return pltpu.CompilerParams(dimension_semantics=semantics, vmem_limit_bytes=VMEM_LIMIT)


def _tile(n, preferred):
    t = min(n, preferred)
    assert n % t == 0, (n, t)
    return t


def _sigmoid(x):
    return 1.0 / (1.0 + jnp.exp(-x))


def _silu(x):
    s = _sigmoid(x)
    return x * s, s * (1.0 + x * (1.0 - s))


def _gelu(x):
    k = math.sqrt(2.0 / math.pi)
    x2 = x * x
    t = jnp.tanh(k * (x + 0.044715 * (x * x2)))
    cdf = 0.5 * (1.0 + t)
    return x * cdf, cdf + 0.5 * x * (1.0 - t * t) * (k * (1.0 + 3.0 * 0.044715 * x2))


def _rms_scale(x):
    return lax.rsqrt(jnp.mean(x * x, axis=-1, keepdims=True) + RMS_EPS)


def _iotas(n):
    return (lax.broadcasted_iota(jnp.int32, (n, n), 0), lax.broadcasted_iota(jnp.int32, (n, n), 1))


def _adamw(w, g, m, v):
    m = ADAM_B1 * m + (1.0 - ADAM_B1) * g
    v = ADAM_B2 * v + (1.0 - ADAM_B2) * (g * g)
    m_hat = m / (1.0 - ADAM_B1 ** ADAM_STEP)
    v_hat = v / (1.0 - ADAM_B2 ** ADAM_STEP)
    delta = -ADAM_LR * (m_hat / (jnp.sqrt(v_hat) + ADAM_EPS) + ADAM_WD * w)
    return delta, m, v


def _dot(a, b):
    return jnp.dot(a, b, preferred_element_type=F32)


def _dot_nt(a, b):
    return lax.dot_general(a, b, NT_DIMS, preferred_element_type=F32)


def _dot_tn(a, b):
    return lax.dot_general(a, b, TN_DIMS, preferred_element_type=F32)


def _sb_logs(raw, scale, valid):
    z = (raw * scale).astype(BF16)
    log_beta = jnp.minimum(z, 0) - jnp.log(1 + jnp.exp(-jnp.abs(z)))
    log_rest = log_beta - z
    if valid is not None:
        log_beta = jnp.where(valid, log_beta, MASKED_LOG)
        log_rest = jnp.where(valid, log_rest, 0)
    return log_beta, log_rest


def _me():
    return lax.axis_index("x"), lax.axis_index("y"), lax.axis_index("c")


def _slot(p):
    return 4 * p[0] + 2 * p[1] + p[2]


def _peer(me, k):
    flips = ((k >> 2) & 1, (k >> 1) & 1, k & 1)
    return tuple(1 - a if f else a for a, f in zip(me, flips))


def _stack_exchange(me, st_in, st_out, n_whole, send_sems, recv_sems, local_sems, arrivals=True):
    mine = _slot(me)
    ns = len(st_in)
    part = lambda a, dev: st_in[a] if a >= ns - n_whole else st_in[a].at[_slot(dev)]
    local = [pltpu.make_async_copy(part(a, me), st_out[a].at[mine], local_sems.at[a]) for a in range(ns)]
    remote, landed = [], []
    for k in range(1, N_DEV):
        peer = _peer(me, k)
        for a in range(ns):
            sems = dict(send_sem=send_sems.at[7 * a + k - 1], recv_sem=recv_sems.at[7 * a + k - 1])
            remote.append(pltpu.make_async_remote_copy(
                src_ref=part(a, peer), dst_ref=st_out[a].at[mine],
                device_id=peer, device_id_type=MESH_ID, **sems))
            if arrivals:
                got = st_out[a].at[_slot(peer)]
                landed.append(pltpu.make_async_remote_copy(
                    src_ref=got, dst_ref=got, device_id=me, device_id_type=MESH_ID, **sems))
    return local, remote, landed


def _gather_in_proj(x2d, norm_in, w_in_sh, wo_shards, my_slot):
    n, d = x2d.shape
    esh = w_in_sh.shape[1]
    pw = 2 * esh
    n_chip = N_DEV // 2
    tm = _tile(n, 1024)
    n_i = n // tm
    mid = n_i // 2
    no = len(wo_shards)
    flip_at = lambda st: jnp.where(st == 1, 2, jnp.where(st == 2, 1, jnp.where(st == 3, 3, 0)))

    def body(me_ref, x_ref, g_ref, win_ref, *refs):
        del me_ref
        wo_in = refs[:no]
        proj_ref, h_ref, wg_ref = refs[no:no + 3]
        wo_out = refs[no + 3:2 * no + 3]
        wv, stage, h_s = refs[2 * no + 3:2 * no + 6]
        wo_stage = refs[2 * no + 6:3 * no + 6]
        send_sems, recv_sems, pair_sems, own_sems, wo_send, wo_recv, wo_local = refs[3 * no + 6:]
        st, i = pl.program_id(0), pl.program_id(1)
        x, y, c = _me()
        me, sibling = (x, y, c), (x, y, 1 - c)
        chips = [(1 - x, y), (x, 1 - y), (1 - x, 1 - y)]
        chip_id = lambda p: 2 * p[0] + p[1]

        def window(chip, core):
            return wv.at[chip_id(chip), :, pl.ds(pl.multiple_of(core * esh, LANE), esh)]

        def copy(k, block, to, src=None):
            dst = window(block[:2], block[2])
            return pltpu.make_async_remote_copy(
                src_ref=dst if src is None else src, dst_ref=dst,
                send_sem=send_sems.at[k], recv_sem=recv_sems.at[k], device_id=to, device_id_type=MESH_ID)

        def wo_copy(a, k, block, to, src=None):
            dst = wo_out[a].at[_slot(block)]
            return pltpu.make_async_remote_copy(
                src_ref=dst if src is None else src, dst_ref=dst,
                send_sem=wo_send.at[7 * a + k], recv_sem=wo_recv.at[7 * a + k], device_id=to, device_id_type=MESH_ID)

        def own_copy():
            return pltpu.make_async_copy(stage, window((x, y), c), own_sems.at[0])

        def wo_own_copy(a):
            return pltpu.make_async_copy(wo_stage[a], wo_out[a].at[_slot(me)], wo_local.at[a])

        def pair_copy(step):
            chip = jnp.bitwise_xor(chip_id((x, y)), flip_at(step))
            return pltpu.make_async_copy(wv.at[chip], wg_ref.at[:, pl.ds(pl.multiple_of(chip * pw, LANE), pw)],
                                         pair_sems.at[step])

        first = jnp.logical_and(st == 0, i == 0)

        @pl.when(first)
        def _():
            stage[...] = win_ref[...].astype(BF16)
            own_copy().start()
            copy(0, me, sibling, src=stage).start()
            for j in range(2):
                copy(1 + j, me, (*chips[j], c), src=stage).start()
            own_copy().wait()
            copy(0, sibling, me).wait_recv()
            pair_copy(0).start()

        for s_ in range(n_chip - 1):
            @pl.when(jnp.logical_and(st == s_, i == mid))
            def _():
                copy(1 + s_, (*chips[s_], c), me).wait_recv()
                copy(4 + s_, (*chips[s_], c), sibling).start()
                if s_ == 0:
                    copy(3, me, (*chips[2], c), src=stage).start()
                if s_ == 1:
                    for a in range(no):
                        wo_stage[a][...] = wo_in[a][...].astype(BF16)
                        wo_own_copy(a).start()
                        wo_copy(a, 0, me, sibling, src=wo_stage[a]).start()
                        for j, chip in enumerate(chips):
                            wo_copy(a, 1 + j, me, (*chip, c), src=wo_stage[a]).start()
                if s_ == 2:
                    for a in range(no):
                        for j, chip in enumerate(chips):
                            wo_copy(a, 1 + j, (*chip, c), me).wait_recv()
                            wo_copy(a, 4 + j, (*chip, c), sibling).start()

        for s_ in range(1, n_chip):
            @pl.when(jnp.logical_and(st == s_, i == 0))
            def _():
                copy(3 + s_, (*chips[s_ - 1], 1 - c), me).wait_recv()
                pair_copy(s_).start()

        xv = x_ref[...]
        h_s[...] = (xv * _rms_scale(xv) * g_ref[...]).astype(BF16)

        @pl.when(st == 0)
        def _():
            h_ref[...] = h_s[...]

        chip_now = jnp.bitwise_xor(chip_id((x, y)), flip_at(st))
        proj_ref[...] = _dot(h_s[...], wv[chip_now]).astype(BF16)

        @pl.when(jnp.logical_and(st == n_chip - 1, i == n_i - 1))
        def _():
            copy(0, me, sibling, src=stage).wait_send()
            for j, chip in enumerate(chips):
                copy(1 + j, me, (*chip, c), src=stage).wait_send()
                copy(4 + j, (*chip, c), sibling).wait_send()
            for s_ in range(n_chip):
                pair_copy(s_).wait()
            for a in range(no):
                wo_copy(a, 0, me, sibling, src=wo_stage[a]).wait_send()
                wo_copy(a, 0, sibling, me).wait_recv()
                for j, chip in enumerate(chips):
                    wo_copy(a, 1 + j, me, (*chip, c), src=wo_stage[a]).wait_send()
                    wo_copy(a, 4 + j, (*chip, c), sibling).wait_send()
                    wo_copy(a, 4 + j, (*chip, 1 - c), me).wait_recv()
                wo_own_copy(a).wait()

    any_spec = pl.BlockSpec(memory_space=pl.ANY)
    vmem = pl.BlockSpec(memory_space=pltpu.VMEM)
    grid_spec = pltpu.PrefetchScalarGridSpec(
        num_scalar_prefetch=1, grid=(n_chip, n_i),
        in_specs=[pl.BlockSpec((tm, d), lambda st, i, me: (i, 0)),
                  pl.BlockSpec((1, d), lambda st, i, me: (0, 0)), vmem] + [vmem] * no,
        out_specs=[pl.BlockSpec((tm, pw), lambda st, i, me: (i, jnp.bitwise_xor(me[0] // 2, flip_at(st)))),
                   pl.BlockSpec((tm, d), lambda st, i, me: (jnp.where(st == 0, i, n_i - 1), 0)),
                   any_spec] + [any_spec] * no,
        scratch_shapes=[pltpu.VMEM((n_chip, d, pw), BF16), pltpu.VMEM((d, esh), BF16), pltpu.VMEM((tm, d), BF16)] + [
            pltpu.VMEM(s.shape, BF16) for s in wo_shards] + [
            pltpu.SemaphoreType.DMA((7,)), pltpu.SemaphoreType.DMA((7,)),
            pltpu.SemaphoreType.DMA((n_chip,)), pltpu.SemaphoreType.DMA((1,)),
            pltpu.SemaphoreType.DMA((7 * no,)), pltpu.SemaphoreType.DMA((7 * no,)),
            pltpu.SemaphoreType.DMA((no,))])
    return pl.pallas_call(
        body, name="gather_in_proj", grid_spec=grid_spec,
        out_shape=[SDS((n, n_chip * pw), BF16), SDS((n, d), BF16), SDS((d, n_chip * pw), BF16)] + [
            SDS((N_DEV,) + s.shape, BF16) for s in wo_shards],
        compiler_params=pltpu.CompilerParams(dimension_semantics=("arbitrary", "arbitrary"),
                                             vmem_limit_bytes=VMEM_LIMIT),
    )(my_slot, x2d, norm_in, w_in_sh, *wo_shards)


N_CHIP = N_DEV // 2
CHIP_FLIPS = (3, 2, 1, 0)


def _owner_at(mine, j):
    flip = 0
    for pair, f in enumerate(CHIP_FLIPS):
        flip = jnp.where(j // 2 == pair, f, flip)
    return 2 * jnp.bitwise_xor(mine // 2, flip) + j % 2


def _dw_in_exchange(h, dproj, my_slot, packed):
    n, d = h.shape
    esh = dproj.shape[1] // N_DEV
    tk = _tile(n, 2048)
    nk = n // tk
    last_j = N_DEV - 1

    def body(me_ref, h_ref, dp_ref, pk_in, win_out, pk_out,
             acc, halfbuf, recvbuf, sendbuf, half_send, half_recv, win_send, win_recv,
             send_sems, recv_sems, local_sems):
        del me_ref
        j, k = pl.program_id(0), pl.program_id(1)
        x, y, c = _me()
        me, sibling = (x, y, c), (x, y, 1 - c)
        mine = _slot(me)
        my_chip = mine // 2

        def pack_copies():
            local = pltpu.make_async_copy(pk_in, pk_out.at[mine], local_sems.at[0])
            remote = [pltpu.make_async_remote_copy(
                src_ref=pk_in, dst_ref=pk_out.at[mine], send_sem=send_sems.at[kk - 1], recv_sem=recv_sems.at[kk - 1],
                device_id=_peer(me, kk), device_id_type=MESH_ID) for kk in range(1, N_DEV)]
            return local, remote

        def half_copy(jj):
            slot = (jj // 2) % 2
            return pltpu.make_async_remote_copy(
                src_ref=halfbuf.at[slot], dst_ref=recvbuf.at[slot],
                send_sem=half_send.at[slot], recv_sem=half_recv.at[slot],
                device_id=sibling, device_id_type=MESH_ID)

        def chip_copy(jj):
            slot = (jj // 2) % 2
            owner = _owner_at(mine, jj)
            return pltpu.make_async_remote_copy(
                src_ref=sendbuf.at[slot], dst_ref=win_out.at[my_chip],
                send_sem=win_send.at[slot], recv_sem=win_recv.at[my_chip],
                device_id=(owner // 4, (owner // 2) % 2, owner % 2), device_id_type=MESH_ID)

        def own_copy():
            return pltpu.make_async_copy(sendbuf.at[(last_j // 2) % 2], win_out.at[my_chip], local_sems.at[1])

        @pl.when(jnp.logical_and(j == 0, k == 0))
        def _():
            local, remote = pack_copies()
            for cp in [local] + remote:
                cp.start()

        @pl.when(k == 0)
        def _():
            acc[...] = jnp.zeros_like(acc)

        acc[...] += _dot_tn(dp_ref[...], h_ref[...])

        done = k == nk - 1
        combine = j % 2 == c
        slot = (j // 2) % 2

        @pl.when(jnp.logical_and(done, jnp.logical_not(combine)))
        def _():
            @pl.when(j >= 4)
            def _():
                half_copy(j - 4).wait_send()

            halfbuf[slot] = acc[...].astype(BF16)
            half_copy(j).start()

        @pl.when(jnp.logical_and(done, combine))
        def _():
            half_copy(j).wait_recv()

            @pl.when(j >= 4)
            def _():
                chip_copy(j - 4).wait_send()

            sendbuf[slot] = (acc[...] + recvbuf[slot].astype(F32)).astype(BF16)

            @pl.when(j < last_j - 1)
            def _():
                chip_copy(j).start()

            @pl.when(j >= last_j - 1)
            def _():
                own_copy().start()

        @pl.when(jnp.logical_and(j == last_j, done))
        def _():
            half_copy(5 - c).wait_send()
            half_copy(7 - c).wait_send()
            chip_copy(4 + c).wait_send()
            own_copy().wait()
            for chip in range(N_CHIP):
                @pl.when(chip != my_chip)
                def _():
                    landed = win_out.at[chip]
                    pltpu.make_async_remote_copy(
                        src_ref=landed, dst_ref=landed, send_sem=win_send.at[0], recv_sem=win_recv.at[chip],
                        device_id=me, device_id_type=MESH_ID).wait_recv()
            local, remote = pack_copies()
            for cp in remote:
                cp.wait_send()
            for kk in range(1, N_DEV):
                landed = pk_out.at[_slot(_peer(me, kk))]
                pltpu.make_async_remote_copy(
                    src_ref=landed, dst_ref=landed, send_sem=send_sems.at[kk - 1], recv_sem=recv_sems.at[kk - 1],
                    device_id=me, device_id_type=MESH_ID).wait_recv()
            local.wait()

    any_spec = pl.BlockSpec(memory_space=pl.ANY)
    grid_spec = pltpu.PrefetchScalarGridSpec(
        num_scalar_prefetch=1, grid=(N_DEV, nk),
        in_specs=[pl.BlockSpec((tk, d), lambda j, k, me: (k, 0)),
                  pl.BlockSpec((tk, esh), lambda j, k, me: (k, _owner_at(me[0], j))), any_spec],
        out_specs=[any_spec] * 2,
        scratch_shapes=[pltpu.VMEM((esh, d), F32)] + [pltpu.VMEM((2, esh, d), BF16)] * 3 + [
            pltpu.SemaphoreType.DMA((2,)), pltpu.SemaphoreType.DMA((2,)),
            pltpu.SemaphoreType.DMA((2,)), pltpu.SemaphoreType.DMA((N_CHIP,)),
            pltpu.SemaphoreType.DMA((N_DEV - 1,)), pltpu.SemaphoreType.DMA((N_DEV - 1,)),
            pltpu.SemaphoreType.DMA((2,))])
    return pl.pallas_call(
        body, name="dw_in_exchange", grid_spec=grid_spec,
        out_shape=[SDS((N_CHIP, esh, d), BF16), SDS((N_DEV,) + packed.shape, packed.dtype)],
        compiler_params=_params(("arbitrary", "arbitrary")),
    )(my_slot, h, dproj, packed)


def _finish_small(packs, late_packs, states, groups, chunk):
    gc = groups * chunk
    nw = len(states)

    def body(p_ref, l_ref, *refs):
        st = refs[:3 * nw]
        loss_ref = refs[3 * nw]
        outs = refs[3 * nw + 1:]
        row, col = _iotas(chunk)
        tril = col <= row

        def total(ref, rs):
            tot = ref[0, rs, :]
            for dev in range(1, N_DEV):
                tot = tot + ref[dev, rs, :]
            return tot

        def update(k, rs_w, g):
            w_ref, m_ref, v_ref = st[3 * k:3 * k + 3]
            _adamw_outputs(*[o.at[rs_w] for o in outs[4 * k:4 * k + 4]], g, w_ref[rs_w, :], m_ref[rs_w, :], v_ref[rs_w, :])

        for g in range(groups):
            rs = slice(g * chunk, (g + 1) * chunk)
            update(0, rs, jnp.where(tril, total(p_ref, rs), 0.0))
        slab = lambda k: slice(gc + k * SUBLANE, gc + (k + 1) * SUBLANE)
        for k in range(3):
            update(1 + k, slice(0, SUBLANE), total(p_ref, slab(k)))
        loss_ref[...] = jnp.full((SUBLANE, LANE), jnp.sum(total(p_ref, slab(3))), F32)
        update(4, slice(0, SUBLANE), total(l_ref, slice(0, SUBLANE)))

    flat = [a for s in states for a in s]
    vmem = pl.BlockSpec(memory_space=pltpu.VMEM)
    res = pl.pallas_call(
        body, name="finish_small",
        out_shape=[SDS((SUBLANE, LANE), F32)] + [SDS(s[0].shape, F32) for s in states for _ in range(4)],
        in_specs=[vmem] * (2 + len(flat)),
        out_specs=[vmem] * (1 + 4 * nw),
        compiler_params=pltpu.CompilerParams(vmem_limit_bytes=VMEM_LIMIT),
    )(packs, late_packs, *flat)
    return res[0], [res[1 + 4 * k:5 + 4 * k] for k in range(nw)]


def _branch_a_fwd(proj, norm_v, w_s, b_col):
    n = proj.shape[0]
    d = norm_v.shape[1]
    groups, chunk, _ = w_s.shape
    tr = _tile(n, 8 * chunk)

    def body(u_ref, v_ref, z_ref, gv_ref, ws_ref, b_ref, ya_ref, vn_s, pre_s):
        row, col = _iotas(chunk)
        tril = col <= row
        vg = _gelu(v_ref[...])[0].astype(F32)
        vn_s[...] = (vg * _rms_scale(vg) * gv_ref[...]).astype(BF16)
        pre_s[...] = _gelu(u_ref[...])[0] * _silu(z_ref[...])[0]
        for g in range(groups):
            wm = jnp.where(tril, ws_ref[g], 0.0).astype(BF16)
            cs = slice(g * chunk, (g + 1) * chunk)
            for c in range(tr // chunk):
                rs = slice(c * chunk, (c + 1) * chunk)
                mixed = _dot(wm, vn_s[rs, cs]) + b_ref[g]
                ya_ref[rs, cs] = (pre_s[rs, cs].astype(F32) * mixed).astype(BF16)

    seg = lambda k: pl.BlockSpec((tr, d), lambda i: (i, k))
    return pl.pallas_call(
        body, name="branch_a_fwd", grid=(n // tr,),
        in_specs=[seg(0), seg(1), seg(2),
                  pl.BlockSpec((1, d), lambda i: (0, 0)),
                  pl.BlockSpec((groups, chunk, chunk), lambda i: (0, 0, 0)),
                  pl.BlockSpec((groups, chunk, 1), lambda i: (0, 0, 0))],
        out_specs=pl.BlockSpec((tr, d), lambda i: (i, 0)),
        out_shape=SDS((n, d), BF16),
        scratch_shapes=[pltpu.VMEM((tr, d), BF16), pltpu.VMEM((tr, d), BF16)],
        compiler_params=_params(("parallel",)),
    )(proj, proj, proj, norm_v, w_s, b_col)


def _sb_fwd(proj, batch, seq, d, hd):
    heads = d // hd
    t = _tile(seq, SB_TILE)
    sw = _tile(t, SB_SCAN)
    nb = t // sw
    scale = hd ** -0.5
    nblk = seq // t
    nh = SB_HEADS
    wide = nh * hd
    cols = [slice(hh * hd, (hh + 1) * hd) for hh in range(nh)]

    def body(qs, k_ref, vs, zb_ref, yb_ref, o_ref, tot_ref, kts, later, acc):
        for jb in range(nblk):
            kts[jb] = k_ref[jb * t:(jb + 1) * t, :].T
        row, col = _iotas(t)
        later[...] = (row[:sw, :sw] > col[:sw, :sw]).astype(BF16)

        def qblock(i, carry):
            r0 = pl.multiple_of(i * t, t)

            def tile(j, runs):
                c0 = pl.multiple_of(j * t, t)
                logs = [_sb_logs(_dot(qs[pl.ds(r0, t), cs], kts[j, cs, :]), scale, None) for cs in cols]
                scans = [_dot(jnp.concatenate([logs[hh][1][:, b * sw:(b + 1) * sw] for b in range(nb)], axis=0),
                              later[...]) for hh in range(nh)]
                new_runs = []
                for hh in range(nh):
                    after = runs[hh]
                    blocks = [None] * nb
                    for b in reversed(range(nb)):
                        ks_ = slice(b * sw, (b + 1) * sw)
                        inside = scans[hh][b * t:(b + 1) * t]
                        blocks[b] = jnp.exp(logs[hh][0][:, ks_].astype(F32) + inside + after).astype(BF16)
                        after = after + inside[:, 0:1] + logs[hh][1][:, b * sw:b * sw + 1].astype(F32)
                    new_runs.append(after)
                    acc[:, cols[hh]] += _dot(jnp.concatenate(blocks, axis=1), vs[pl.ds(c0, t), cols[hh]])
                return tuple(new_runs)

            def diagonal_tile():
                starts = [b * sw for b in range(nb)]
                logs = [[_sb_logs(_dot(qs[pl.ds(r0 + s, t - s), cs], kts[i, cs, s:s + sw]), scale,
                                  col[:t - s, :sw] < row[:t - s, :sw]) for s in starts] for cs in cols]
                scans = [_dot(jnp.concatenate([lr for _, lr in logs[hh]], axis=0), later[...]) for hh in range(nh)]
                new_runs = []
                offs = [sum(t - s for s in starts[:b]) for b in range(nb)]
                for hh in range(nh):
                    after = jnp.zeros((t, 1), F32)
                    ws = [None] * nb
                    for b in reversed(range(nb)):
                        s = starts[b]
                        lb, lr = logs[hh][b]
                        inside = scans[hh][offs[b]:offs[b] + t - s]
                        ws[b] = jnp.exp(lb.astype(F32) + inside + after[s:]).astype(BF16)
                        total = inside[:, 0:1] + lr[:, 0:1].astype(F32)
                        after = after + total if s == 0 else jnp.concatenate([after[:s], after[s:] + total], axis=0)
                    new_runs.append(after)
                    acc[:, cols[hh]] = _dot(ws[0], vs[pl.ds(r0, sw), cols[hh]])
                    for b in range(1, nb):
                        acc[starts[b]:, cols[hh]] += _dot(ws[b], vs[pl.ds(r0 + starts[b], sw), cols[hh]])
                return tuple(new_runs)

            runs = diagonal_tile()
            runs = lax.fori_loop(0, i, lambda jj, rs: tile(i - 1 - jj, rs), runs)
            for hh in range(nh):
                out = acc[:, cols[hh]]
                o_ref[pl.ds(r0, t), cols[hh]] = out.astype(BF16)
                tot_ref[hh, pl.ds(r0, t), :] = runs[hh]
                sz, _ = _silu(zb_ref[pl.ds(r0, t), cols[hh]].astype(F32))
                yb_ref[pl.ds(r0, t), cols[hh]] = (out * sz).astype(BF16)
            return carry

        lax.fori_loop(0, nblk, qblock, 0)

    col0 = d // wide
    seg = lambda k: pl.BlockSpec((seq, wide), lambda b, h: (b, k * col0 + h))
    return pl.pallas_call(
        body, name="sb_fwd", grid=(batch, heads // nh),
        in_specs=[seg(3), seg(4), seg(5), seg(6)],
        out_specs=[pl.BlockSpec((seq, wide), lambda b, h: (b, h))] * 2 + [
            pl.BlockSpec((nh, seq, 1), lambda b, h: (b * (heads // nh) + h, 0, 0))],
        out_shape=[SDS((batch * seq, d), BF16), SDS((batch * seq, d), BF16), SDS((batch * heads, seq, 1), F32)],
        scratch_shapes=[pltpu.VMEM((nblk, wide, t), BF16), pltpu.VMEM((sw, sw), BF16), pltpu.VMEM((t, wide), F32)],
        compiler_params=_params(("parallel", "parallel")),
    )(proj, proj, proj, proj)


def _tail(x2d, tgt, ya, yb, proj, w_oa, w_ob, w_out, norm_final):
    n, d = x2d.shape
    e = proj.shape[1]
    tm = _tile(n, 512)
    steps = n // tm

    def body(x_ref, t_ref, ya_ref, yb_ref, ga_ref, gb_ref, woa_ref, wob_ref, wout_ref, gf_ref,
             dproj_ref, dx2_ref, dya_ref, dyb_ref, mrg_ref, dpa_ref, dpb_ref, loss_ref, dgf_ref, dg_s, dg_sems):
        i = pl.program_id(0)

        def gate_copy(step):
            rows_ = pl.ds(pl.multiple_of(step * tm, tm), tm)
            return pltpu.make_async_copy(dg_s.at[step % 2], dproj_ref.at[rows_, pl.ds(7 * d, 2 * d)],
                                         dg_sems.at[step % 2])

        @pl.when(i == 0)
        def _():
            loss_ref[...] = jnp.zeros_like(loss_ref)
            dgf_ref[...] = jnp.zeros_like(dgf_ref)

        @pl.when(i >= 2)
        def _():
            gate_copy(i - 2).wait()

        halves = [slice(hf * (tm // 2), (hf + 1) * (tm // 2)) for hf in range(2)] if tm >= 512 else [slice(0, tm)]
        gf = gf_ref[...]
        pa = [_dot(ya_ref[rs, :], woa_ref[...]) for rs in halves]
        pb = [_dot(yb_ref[rs, :], wob_ref[...]) for rs in halves]
        sa = [_sigmoid(ga_ref[rs, :].astype(F32)) for rs in halves]
        sb = [_sigmoid(gb_ref[rs, :].astype(F32)) for rs in halves]
        merged = [(sa[k] * pa[k] + sb[k] * pb[k]).astype(BF16) for k in range(len(halves))]
        for k, rs in enumerate(halves):
            mrg_ref[rs, :] = merged[k]
        x2 = [x_ref[rs, :] + _dot(merged[k], wout_ref[...]) for k, rs in enumerate(halves)]
        dx2 = []
        for k, rs in enumerate(halves):
            r2 = _rms_scale(x2[k])
            xh = x2[k] * r2
            diff = xh * gf - t_ref[rs, :]
            loss_ref[...] += jnp.sum(diff * diff, axis=0, keepdims=True) * (0.5 / d)
            dy = diff * (1.0 / d)
            dgf_ref[...] += jnp.sum(dy * xh, axis=0, keepdims=True)
            dxh = dy * gf
            dx2.append(r2 * (dxh - xh * jnp.mean(dxh * xh, axis=-1, keepdims=True)))
            dx2_ref[rs, :] = dx2[k]
        dm = [_dot_nt(dx2[k].astype(BF16), wout_ref[...]) for k in range(len(halves))]
        dpa, dpb = [], []
        for k, rs in enumerate(halves):
            dpa.append((dm[k] * sa[k]).astype(BF16))
            dpb.append((dm[k] * sb[k]).astype(BF16))
            dpa_ref[rs, :] = dpa[k]
            dpb_ref[rs, :] = dpb[k]
            dg_s[i % 2, rs, 0:d] = (dm[k] * pa[k] * (sa[k] * (1.0 - sa[k]))).astype(BF16)
            dg_s[i % 2, rs, d:2 * d] = (dm[k] * pb[k] * (sb[k] * (1.0 - sb[k]))).astype(BF16)
        gate_copy(i).start()
        for k, rs in enumerate(halves):
            dya_ref[rs, :] = _dot_nt(dpa[k], woa_ref[...]).astype(BF16)
        for k, rs in enumerate(halves):
            dyb_ref[rs, :] = _dot_nt(dpb[k], wob_ref[...]).astype(BF16)

        @pl.when(i == steps - 1)
        def _():
            if steps >= 2:
                gate_copy(i - 1).wait()
            gate_copy(i).wait()

    rows = lambda k=0: pl.BlockSpec((tm, d), lambda i: (i, k))
    full = pl.BlockSpec((d, d), lambda i: (0, 0), pipeline_mode=pl.Buffered(1))
    vec = pl.BlockSpec((1, d), lambda i: (0, 0))
    return pl.pallas_call(
        body, name="tail", grid=(steps,),
        in_specs=[rows(), rows(), rows(), rows(), rows(7), rows(8), full, full, full, vec],
        out_specs=[pl.BlockSpec(memory_space=pl.ANY),
                   rows(), rows(), rows(), rows(), rows(), rows(), vec, vec],
        out_shape=[SDS((n, e), BF16), SDS((n, d), F32), SDS((n, d), BF16), SDS((n, d), BF16),
                   SDS((n, d), BF16), SDS((n, d), BF16), SDS((n, d), BF16),
                   SDS((1, d), F32), SDS((1, d), F32)],
        scratch_shapes=[pltpu.VMEM((2, tm, 2 * d), BF16), pltpu.SemaphoreType.DMA((2,))],
        compiler_params=_params(("arbitrary",)),
    )(x2d, tgt, ya, yb, proj, proj, w_oa, w_ob, w_out, norm_final)


def _dw_o(pairs):
    n, d = pairs[0][0].shape
    tk = _tile(n, 1024)
    nk = n // tk
    npair = len(pairs)

    def body(*refs):
        a_refs, b_refs = refs[:npair], refs[npair:2 * npair]
        o_ref, acc = refs[2 * npair], refs[2 * npair + 1]
        p, k = pl.program_id(0), pl.program_id(1)

        @pl.when(k == 0)
        def _():
            acc[...] = jnp.zeros_like(acc)

        for q in range(npair):
            @pl.when(p == q)
            def _():
                acc[...] += _dot_tn(a_refs[q][...], b_refs[q][...].astype(BF16))

        @pl.when(k == nk - 1)
        def _():
            o_ref[0] = acc[...].astype(BF16)

    def tiles(q):
        return pl.BlockSpec((tk, d), lambda p, k: (jnp.where(p == q, k, jnp.where(p < q, 0, nk - 1)), 0))

    return pl.pallas_call(
        body, name="dw_o", grid=(npair, nk),
        in_specs=[tiles(q) for q in range(npair)] * 2,
        out_specs=pl.BlockSpec((1, d, d), lambda p, k: (p, 0, 0)),
        out_shape=SDS((npair, d, d), BF16),
        scratch_shapes=[pltpu.VMEM((d, d), F32)],
        compiler_params=_params(("arbitrary", "arbitrary")),
    )(*[a for a, _ in pairs], *[b for _, b in pairs])


def _sb_bwd(proj, o, dyb, tot, dproj, dw_stack, packed, batch, seq, d, hd):
    heads = d // hd
    t = _tile(seq, SB_TILE_BWD)
    sw = _tile(t, SB_SCAN)
    nb = t // sw
    scale = hd ** -0.5
    nblk = seq // t
    nh = SB_HEADS
    wide = nh * hd
    hs = range(nh)
    cols = [slice(hh * hd, (hh + 1) * hd) for hh in hs]
    blocks = [slice(b * sw, (b + 1) * sw) for b in range(nb)]
    last = slice(sw - 1, sw)

    def compute(qs, ks, v_ref, zb_ref, o_ref, dyb_ref, tot_ref, kts, vts, dos, dzb, dq_all, dkv_t, qt_s, dot_s,
                upto, before, dq):
        for jb in range(nblk):
            rows = slice(jb * t, (jb + 1) * t)
            kts[jb] = ks[rows, :].T
            vts[jb] = v_ref[rows, :].T
        sz, dsz = _silu(zb_ref[...])
        dyb_v = dyb_ref[...]
        dos[...] = dyb_v * sz
        dzb[...] = dyb_v * o_ref[...] * dsz
        row, col = _iotas(t)
        upto[...] = (row[:sw, :sw] <= col[:sw, :sw]).astype(BF16)
        before[...] = (row[:sw, :sw] < col[:sw, :sw]).astype(BF16)

        def qblock(i, carry):
            r0 = pl.multiple_of(i * t, t)

            def tile(j, sums):
                c0 = pl.multiple_of(j * t, t)
                q_i = [qs[pl.ds(r0, t), cs] for cs in cols]
                do_i = [dos[pl.ds(r0, t), cs] for cs in cols]
                logs = [_sb_logs(_dot(q_i[hh], kts[j, cols[hh], :]), scale, None) for hh in hs]
                scans = [_dot(jnp.concatenate([logs[hh][1][:, ks_] for ks_ in blocks], axis=0), upto[...]) for hh in hs]
                dw = [_dot(do_i[hh], vts[j, cols[hh], :]) for hh in hs]
                ws, gs, new_runs = [], [], []
                for hh in hs:
                    left = tot_ref[hh, pl.ds(r0, t), :] - sums[hh][0]
                    w_b, g_b = [], []
                    for b, ks_ in enumerate(blocks):
                        inside = scans[hh][b * t:(b + 1) * t]
                        w = jnp.exp(logs[hh][0][:, ks_].astype(F32) + (left - inside))
                        w_b.append(w.astype(BF16))
                        g_b.append((dw[hh][:, ks_] * w).astype(BF16))
                        left = left - inside[:, last]
                    ws.append(jnp.concatenate(w_b, axis=1))
                    gs.append(g_b)
                    new_runs.append(tot_ref[hh, pl.ds(r0, t), :] - left)
                gscans = [_dot(jnp.concatenate(gs[hh], axis=0), before[...]) for hh in hs]
                dzs, new_gruns = [], []
                for hh in hs:
                    g_before = sums[hh][1]
                    dz_b = []
                    for b, ks_ in enumerate(blocks):
                        inside = gscans[hh][b * t:(b + 1) * t]
                        beta = jnp.exp(logs[hh][0][:, ks_]).astype(F32)
                        g = gs[hh][b].astype(F32)
                        dz_b.append((g - (g + inside + g_before) * beta).astype(BF16))
                        g_before = g_before + inside[:, last] + g[:, last]
                    dzs.append(jnp.concatenate(dz_b, axis=1))
                    new_gruns.append(g_before)
                for hh in hs:
                    dkv_t[1, j, cols[hh], :] += _dot(dot_s[cols[hh], :], ws[hh])
                for hh in hs:
                    dkv_t[0, j, cols[hh], :] += _dot(qt_s[cols[hh], :], dzs[hh])
                for hh in hs:
                    dq[:, cols[hh]] += _dot(dzs[hh], ks[pl.ds(c0, t), cols[hh]])
                return tuple((new_runs[hh], new_gruns[hh]) for hh in hs)

            def diagonal_tile(sums):
                starts = [b * sw for b in range(nb)]
                offs = [sum(t - s for s in starts[:b]) for b in range(nb)]
                q_b = [[qs[pl.ds(r0 + s, t - s), cs] for s in starts] for cs in cols]
                do_b = [[dos[pl.ds(r0 + s, t - s), cs] for s in starts] for cs in cols]
                logs = [[_sb_logs(_dot(q_b[hh][b], kts[i, cols[hh], s:s + sw]), scale,
                                  col[:t - s, :sw] < row[:t - s, :sw]) for b, s in enumerate(starts)] for hh in hs]
                dw = [[_dot(do_b[hh][b], vts[i, cols[hh], s:s + sw]) for b, s in enumerate(starts)] for hh in hs]
                scans = [_dot(jnp.concatenate([lr for _, lr in logs[hh]], axis=0), upto[...]) for hh in hs]
                ws, gs = [], []
                for hh in hs:
                    left = tot_ref[hh, pl.ds(r0, t), :] - sums[hh][0]
                    w_b, g_b = [], []
                    for b, s in enumerate(starts):
                        inside = scans[hh][offs[b]:offs[b] + t - s]
                        w = jnp.exp(logs[hh][b][0].astype(F32) + (left[s:] - inside))
                        w_b.append(w.astype(BF16))
                        g_b.append((dw[hh][b] * w).astype(BF16))
                        total = inside[:, last]
                        left = left - total if s == 0 else jnp.concatenate([left[:s], left[s:] - total], axis=0)
                    ws.append(w_b)
                    gs.append(g_b)
                gscans = [_dot(jnp.concatenate(gs[hh], axis=0), before[...]) for hh in hs]
                dzs = []
                for hh in hs:
                    g_before = sums[hh][1]
                    dz_b = []
                    for b, s in enumerate(starts):
                        inside = gscans[hh][offs[b]:offs[b] + t - s]
                        beta = jnp.exp(logs[hh][b][0]).astype(F32)
                        g = gs[hh][b].astype(F32)
                        dz_b.append((g - (g + inside + g_before[s:]) * beta).astype(BF16))
                        total = inside[:, last] + g[:, last]
                        g_before = g_before + total if s == 0 else jnp.concatenate(
                            [g_before[:s], g_before[s:] + total], axis=0)
                    dzs.append(dz_b)
                for hh in hs:
                    for b, s in enumerate(starts):
                        dkv_t[1, i, cols[hh], s:s + sw] = _dot(dot_s[cols[hh], s:], ws[hh][b])
                for hh in hs:
                    for b, s in enumerate(starts):
                        dkv_t[0, i, cols[hh], s:s + sw] = _dot(qt_s[cols[hh], s:], dzs[hh][b])
                for hh in hs:
                    for b, s in enumerate(starts):
                        dq[s:, cols[hh]] += _dot(dzs[hh][b], ks[pl.ds(r0 + s, sw), cols[hh]])

            qt_s[...] = qs[pl.ds(r0, t), :].T
            dot_s[...] = dos[pl.ds(r0, t), :].T
            zero = jnp.zeros((t, 1), F32)
            dq[...] = jnp.zeros_like(dq)
            sums = lax.fori_loop(0, i, tile, ((zero, zero),) * nh)
            diagonal_tile(sums)
            dq_all[pl.ds(r0, t), :] = dq[...]
            return carry

        lax.fori_loop(0, nblk, qblock, 0)

    pairs = heads // nh

    nst = dw_stack.shape[0]
    ns = nst + 1

    def body(qs, ks, v_ref, zb_ref, o_ref, dyb_ref, tot_ref, dproj_in, dw_ref, pk_ref, out_ref, *refs):
        del dproj_in
        st_in = [dw_ref.at[k] for k in range(nst)] + [pk_ref]
        st_out = refs[:ns]
        (kts, vts, dos, dzb, dq_all, dkv_t, qt_s, dot_s, upto, before, dq, stage, stage_sems,
         send_sems, recv_sems, local_sems) = refs[ns:]
        step = pl.program_id(0) * pairs + pl.program_id(1)
        exchange = functools.partial(_stack_exchange, _me(), st_in, st_out, 1, send_sems, recv_sems, local_sems)

        @pl.when(step == 0)
        def _():
            local, remote, _ = exchange(arrivals=False)
            for cp in local + remote:
                cp.start()

        def out_copies(s):
            rows_ = pl.ds(pl.multiple_of((s // pairs) * seq, seq), seq)
            return [pltpu.make_async_copy(
                stage.at[k], out_ref.at[rows_, pl.ds(pl.multiple_of((3 + k) * d + (s % pairs) * wide, wide), wide)],
                stage_sems.at[k]) for k in range(4)]

        compute(qs, ks, v_ref, zb_ref, o_ref, dyb_ref, tot_ref, kts, vts, dos, dzb, dq_all, dkv_t, qt_s, dot_s,
                upto, before, dq)

        @pl.when(step > 0)
        def _():
            for cp in out_copies(step - 1):
                cp.wait()

        stage[0] = (dq_all[...] * scale).astype(BF16)
        for jb in range(nblk):
            stage[1, jb * t:(jb + 1) * t, :] = (dkv_t[0, jb] * scale).astype(BF16).T
            stage[2, jb * t:(jb + 1) * t, :] = dkv_t[1, jb].astype(BF16).T
        stage[3] = dzb[...]
        for cp in out_copies(step):
            cp.start()

        @pl.when(step == batch * pairs - 1)
        def _():
            for cp in out_copies(step):
                cp.wait()
            local, remote, landed = exchange()
            for cp in remote:
                cp.wait_send()
            for cp in landed:
                cp.wait_recv()
            for cp in local:
                cp.wait()

    col0 = d // wide
    seg = lambda k: pl.BlockSpec((seq, wide), lambda b, h: (b, k * col0 + h))
    head = pl.BlockSpec((seq, wide), lambda b, h: (b, h))
    any_spec = pl.BlockSpec(memory_space=pl.ANY)
    return pl.pallas_call(
        body, name="sb_bwd", grid=(batch, pairs),
        in_specs=[seg(3), seg(4), seg(5), seg(6), head, head,
                  pl.BlockSpec((nh, seq, 1), lambda b, h: (b * pairs + h, 0, 0))] + [any_spec] * 3,
        out_specs=[any_spec] * (ns + 1),
        out_shape=[SDS(dproj.shape, dproj.dtype)] + [SDS(dw_stack.shape[1:], dw_stack.dtype)] * nst + [
            SDS((N_DEV,) + packed.shape, packed.dtype)],
        input_output_aliases={7: 0},
        scratch_shapes=[pltpu.VMEM((nblk, wide, t), BF16)] * 2 + [
            pltpu.VMEM((seq, wide), BF16), pltpu.VMEM((seq, wide), BF16),
            pltpu.VMEM((seq, wide), F32), pltpu.VMEM((2, nblk, wide, t), F32),
            pltpu.VMEM((wide, t), BF16), pltpu.VMEM((wide, t), BF16),
            pltpu.VMEM((sw, sw), BF16), pltpu.VMEM((sw, sw), BF16), pltpu.VMEM((t, wide), F32),
            pltpu.VMEM((4, seq, wide), BF16), pltpu.SemaphoreType.DMA((4,)),
            pltpu.SemaphoreType.DMA((7 * ns,)), pltpu.SemaphoreType.DMA((7 * ns,)),
            pltpu.SemaphoreType.DMA((ns,))],
        compiler_params=_params(("arbitrary", "arbitrary")),
    )(proj, proj, proj, proj, o, dyb, tot, dproj, dw_stack, packed)


def _branch_a_bwd(proj, dya, norm_v, w_s, b_col, dproj):
    n = proj.shape[0]
    d = norm_v.shape[1]
    groups, chunk, _ = w_s.shape
    tr = _tile(n, 4 * chunk)

    def body(u_ref, v_ref, z_ref, dya_ref, gv_ref, ws_ref, b_ref, dproj_in,
             out_ref, dws_ref, dbias_ref, dgv_ref, vn_s, dmix_s, dvn_s, db_ref):
        del dproj_in

        @pl.when(pl.program_id(0) == 0)
        def _():
            dws_ref[...] = jnp.zeros_like(dws_ref)
            db_ref[...] = jnp.zeros_like(db_ref)
            dgv_ref[...] = jnp.zeros_like(dgv_ref)

        row, col = _iotas(chunk)
        tril = col <= row
        gv = gv_ref[...]
        vg16, dvg_dv = _gelu(v_ref[...])
        vg = vg16.astype(F32)
        r = _rms_scale(vg)
        vh = vg * r
        vn_s[...] = (vh * gv).astype(BF16)
        ug, dug_du = _gelu(u_ref[...])
        sz, dsz = _silu(z_ref[...])
        dya_v = dya_ref[...]
        dmix_s[...] = dya_v * ug * sz
        du_scale = sz * dug_du
        dz_scale = ug * dsz
        for g in range(groups):
            wm = jnp.where(tril, ws_ref[g], 0.0).astype(BF16)
            cs = slice(g * chunk, (g + 1) * chunk)
            for c in range(tr // chunk):
                rs = slice(c * chunk, (c + 1) * chunk)
                vn = vn_s[rs, cs]
                mixed = _dot(wm, vn) + b_ref[g]
                dmix16 = dmix_s[rs, cs]
                dws_ref[g] += _dot_nt(dmix16, vn)
                db_ref[g] += dmix16.astype(F32)
                dvn_s[rs, cs] = _dot_tn(wm, dmix16)
                t_u = dya_v[rs, cs] * mixed.astype(BF16)
                out_ref[rs, g * chunk:(g + 1) * chunk] = t_u * du_scale[rs, cs]
                out_ref[rs, 2 * d + g * chunk:2 * d + (g + 1) * chunk] = t_u * dz_scale[rs, cs]
        dvn = dvn_s[...]
        dgv_ref[...] += jnp.sum(dvn * vh, axis=0, keepdims=True)
        dvh = dvn * gv
        dvg = r * (dvh - vh * jnp.mean(dvh * vh, axis=-1, keepdims=True))
        out_ref[:, d:2 * d] = (dvg * dvg_dv.astype(F32)).astype(BF16)

        @pl.when(pl.program_id(0) == n // tr - 1)
        def _():
            for g in range(groups):
                dbias_ref[g:g + 1, :] = jnp.sum(db_ref[g].T, axis=0, keepdims=True)

    seg = lambda k: pl.BlockSpec((tr, d), lambda i: (i, k))
    return pl.pallas_call(
        body, name="branch_a_bwd", grid=(n // tr,),
        in_specs=[seg(0), seg(1), seg(2), seg(0),
                  pl.BlockSpec((1, d), lambda i: (0, 0)),
                  pl.BlockSpec((groups, chunk, chunk), lambda i: (0, 0, 0)),
                  pl.BlockSpec((groups, chunk, 1), lambda i: (0, 0, 0)),
                  pl.BlockSpec(memory_space=pl.ANY)],
        out_specs=[pl.BlockSpec((tr, 3 * d), lambda i: (i, 0)),
                   pl.BlockSpec((groups, chunk, chunk), lambda i: (0, 0, 0)),
                   pl.BlockSpec((groups, chunk), lambda i: (0, 0)),
                   pl.BlockSpec((1, d), lambda i: (0, 0))],
        out_shape=[SDS(dproj.shape, dproj.dtype), SDS((groups, chunk, chunk), F32),
                   SDS((groups, chunk), F32), SDS((1, d), F32)],
        input_output_aliases={7: 0},
        scratch_shapes=[pltpu.VMEM((tr, d), BF16), pltpu.VMEM((tr, d), BF16), pltpu.VMEM((tr, d), F32),
                        pltpu.VMEM((groups, chunk, chunk), F32)],
        compiler_params=_params(("arbitrary",)),
    )(proj, proj, proj, dya, norm_v, w_s, b_col, dproj)


def _dx(dproj, wg_in, x2d, dx2, norm_in):
    n, d = x2d.shape
    e = wg_in.shape[1]
    tm = _tile(n, 256)

    def body(dp_ref, w_ref, x_ref, dx2_ref, g_ref, gx_ref, dg_ref):
        @pl.when(pl.program_id(0) == 0)
        def _():
            dg_ref[...] = jnp.zeros_like(dg_ref)

        dh = _dot_nt(dp_ref[...], w_ref[...])
        x = x_ref[...]
        r = _rms_scale(x)
        xh = x * r
        dg_ref[...] += jnp.sum(dh * xh, axis=0, keepdims=True)
        dxh = dh * g_ref[...]
        gx_ref[...] = dx2_ref[...] + r * (dxh - xh * jnp.mean(dxh * xh, axis=-1, keepdims=True))

    rows = pl.BlockSpec((tm, d), lambda i: (i, 0))
    vec = pl.BlockSpec((1, d), lambda i: (0, 0))
    return pl.pallas_call(
        body, name="dx", grid=(n // tm,),
        in_specs=[pl.BlockSpec((tm, e), lambda i: (i, 0)),
                  pl.BlockSpec((d, e), lambda i: (0, 0), pipeline_mode=pl.Buffered(1)), rows, rows, vec],
        out_specs=[rows, vec],
        out_shape=[SDS((n, d), F32), SDS((1, d), F32)],
        compiler_params=_params(("arbitrary",)),
    )(dproj, wg_in, x2d, dx2, norm_in)


def _adamw_outputs(g_ref, d_ref, m_ref, v_ref, g, w, m, v):
    delta, m2, v2 = _adamw(w, g, m, v)
    g_ref[...] = g
    d_ref[...] = delta
    m_ref[...] = m2
    v_ref[...] = v2


def _reduce_adamw(slots, w, m, v, name, transposed=False):
    r, c = w.shape
    ns = slots.shape[0]
    tr = _tile(r, 128)

    def body(s_ref, w_ref, m_ref, v_ref, g_out, d_out, m_out, v_out):
        g = s_ref[0].astype(F32)
        for k in range(1, ns):
            g = g + s_ref[k].astype(F32)
        if transposed:
            g = g.T
        _adamw_outputs(g_out, d_out, m_out, v_out, g, w_ref[...], m_ref[...], v_ref[...])

    blk = pl.BlockSpec((tr, c), lambda i: (i, 0))
    slot_blk = (pl.BlockSpec((ns, c, tr), lambda i: (0, 0, i)) if transposed
                else pl.BlockSpec((ns, tr, c), lambda i: (0, i, 0)))
    return pl.pallas_call(
        body, name=name, grid=(r // tr,),
        in_specs=[slot_blk, blk, blk, blk],
        out_specs=[blk] * 4,
        out_shape=[SDS((r, c), F32)] * 4,
        compiler_params=_params(("parallel",)),
    )(slots, w, m, v)


def kernel(x, norm_in, w_in, norm_v, w_s, b_s, w_o_gmlp, w_o_sb, w_out, norm_final, loss_target, m_norm_in, m_w_in, m_norm_v, m_w_s, m_b_s, m_w_o_gmlp, m_w_o_sb, m_w_out, m_norm_final, v_norm_in, v_w_in, v_norm_v, v_w_s, v_b_s, v_w_o_gmlp, v_w_o_sb, v_w_out, v_norm_final):
    batch, seq, d = x.shape
    n = batch * seq
    groups, chunk = w_s.shape[1], w_s.shape[2]
    hd = LANE
    x2d = x.reshape(n, d)
    tgt = loss_target.reshape(n, d)
    b_col = b_s[0].reshape(groups, chunk, 1)
    norm_final2 = norm_final.reshape(1, d)

    my_slot = _slot(_me()).astype(jnp.int32).reshape(1)
    proj, h, wg_in, wg_oa, wg_ob, wg_out = _gather_in_proj(
        x2d, norm_in, w_in[0], [w_o_gmlp[0], w_o_sb[0], w_out[0]], my_slot)
    rsh = wg_oa.shape[1]
    wf_oa, wf_ob, wf_out = (w.reshape(N_DEV * rsh, d) for w in (wg_oa, wg_ob, wg_out))
    ya = _branch_a_fwd(proj, norm_v, w_s[0], b_col)
    yb, o, sb_tot = _sb_fwd(proj, batch, seq, d, hd)
    dproj, dx2, dya, dyb, merged, dpa, dpb, loss_vec, dgf = _tail(
        x2d, tgt, ya, yb, proj, wf_oa, wf_ob, wf_out, norm_final2)
    gp_wo = _dw_o([(ya, dpa), (yb, dpb), (merged, dx2)])
    dproj, gp_ws, gp_b, gp_nv = _branch_a_bwd(proj, dya, norm_v, w_s[0], b_col, dproj)

    slab = lambda a: a.reshape(d // LANE, LANE)
    gc = groups * chunk
    packed = jnp.concatenate([gp_ws.reshape(gc, chunk), gp_b, slab(gp_nv), slab(dgf), slab(loss_vec)], axis=0)
    dproj, s_oa, s_ob, s_out, packs = _sb_bwd(
        proj, o, dyb, sb_tot, dproj, gp_wo.reshape(3, N_DEV, rsh, d), packed, batch, seq, d, hd)
    grad_x, gp_nin = _dx(dproj, wg_in, x2d, dx2, norm_in)
    s_win, late_packs = _dw_in_exchange(h, dproj, my_slot, slab(gp_nin))
    small = {"w_s": lambda a: a.reshape(gc, chunk), "b_s": lambda a: a[0], "norm_v": slab, "norm_final": slab,
             "norm_in": slab}
    given = {"w_s": (w_s, m_w_s, v_w_s), "b_s": (b_s, m_b_s, v_b_s), "norm_v": (norm_v, m_norm_v, v_norm_v),
             "norm_final": (norm_final, m_norm_final, v_norm_final), "norm_in": (norm_in, m_norm_in, v_norm_in)}
    loss_slab, small_res = _finish_small(
        packs, late_packs, [tuple(small[k](a) for a in given[k]) for k in small], groups, chunk)
    loss = loss_slab[0, 0]

    res = dict(zip(small, small_res))
    res["w_in"] = _reduce_adamw(s_win, w_in[0], m_w_in[0], v_w_in[0], "adamw_w_in", transposed=True)
    res["w_o_gmlp"] = _reduce_adamw(s_oa, w_o_gmlp[0], m_w_o_gmlp[0], v_w_o_gmlp[0], "adamw_w_o_gmlp")
    res["w_o_sb"] = _reduce_adamw(s_ob, w_o_sb[0], m_w_o_sb[0], v_w_o_sb[0], "adamw_w_o_sb")
    res["w_out"] = _reduce_adamw(s_out, w_out[0], m_w_out[0], v_w_out[0], "adamw_w_out")

    shapes = {"norm_in": norm_in.shape, "w_in": w_in.shape, "norm_v": norm_v.shape, "w_s": w_s.shape,
              "b_s": b_s.shape, "w_o_gmlp": w_o_gmlp.shape, "w_o_sb": w_o_sb.shape, "w_out": w_out.shape,
              "norm_final": norm_final.shape}
    names = list(shapes)
    outs = [loss, grad_x.reshape(batch, seq, d)]
    for kind in range(4):
        outs += [res[name][kind].reshape(shapes[name]) for name in names]
    return tuple(outs)
```

```python
import functools
import math

import jax
import jax.numpy as jnp
from jax import lax
from jax.experimental import pallas as pl
from jax.experimental.pallas import tpu as pltpu

F32 = jnp.float32
BF16 = jnp.bfloat16
SDS = jax.ShapeDtypeStruct
MESH_ID = pl.DeviceIdType.MESH

N_DEV = 8
LANE = 128
SUBLANE = 8
VMEM_LIMIT = 56 * 1024 * 1024
SB_TILE = 512
SB_TILE_BWD = 512
SB_SCAN = 256
SB_HEADS = 2
MASKED_LOG = -1e30
RMS_EPS = 1e-6

ADAM_LR = 0.001
ADAM_B1 = 0.9
ADAM_B2 = 0.999
ADAM_EPS = 1e-08
ADAM_WD = 0.01
ADAM_STEP = 10

NT_DIMS = (((1,), (1,)), ((), ()))
TN_DIMS = (((0,), (0,)), ((), ()))


def _params(semantics=None):
    return pltpu.CompilerParams(dimension_semantics=semantics, vmem_limit_bytes=VMEM_LIMIT)


def _tile(n, preferred):
    t = min(n, preferred)
    assert n % t == 0, (n, t)
    return t


def _sigmoid(x):
    return 1.0 / (1.0 + jnp.exp(-x))


def _silu(x):
    s = _sigmoid(x)
    return x * s, s * (1.0 + x * (1.0 - s))


def _gelu(x):
    k = math.sqrt(2.0 / math.pi)
    x2 = x * x
    t = jnp.tanh(k * (x + 0.044715 * (x * x2)))
    cdf = 0.5 * (1.0 + t)
    return x * cdf, cdf + 0.5 * x * (1.0 - t * t) * (k * (1.0 + 3.0 * 0.044715 * x2))


def _rms_scale(x):
    return lax.rsqrt(jnp.mean(x * x, axis=-1, keepdims=True) + RMS_EPS)


def _iotas(n):
    return (lax.broadcasted_iota(jnp.int32, (n, n), 0), lax.broadcasted_iota(jnp.int32, (n, n), 1))


def _adamw(w, g, m, v):
    m = ADAM_B1 * m + (1.0 - ADAM_B1) * g
    v = ADAM_B2 * v + (1.0 - ADAM_B2) * (g * g)
    m_hat = m / (1.0 - ADAM_B1 ** ADAM_STEP)
    v_hat = v / (1.0 - ADAM_B2 ** ADAM_STEP)
    delta = -ADAM_LR * (m_hat / (jnp.sqrt(v_hat) + ADAM_EPS) + ADAM_WD * w)
    return delta, m, v


def _dot(a, b):
    return jnp.dot(a, b, preferred_element_type=F32)


def _dot_nt(a, b):
    return lax.dot_general(a, b, NT_DIMS, preferred_element_type=F32)


def _dot_tn(a, b):
    return lax.dot_general(a, b, TN_DIMS, preferred_element_type=F32)


def _sb_logs(raw, scale, valid):
    z = (raw * scale).astype(BF16)
    log_beta = jnp.minimum(z, 0) - jnp.log(1 + jnp.exp(-jnp.abs(z)))
    log_rest = log_beta - z
    if valid is not None:
        log_beta = jnp.where(valid, log_beta, MASKED_LOG)
        log_rest = jnp.where(valid, log_rest, 0)
    return log_beta, log_rest


def _me():
    return lax.axis_index("x"), lax.axis_index("y"), lax.axis_index("c")


def _slot(p):
    return 4 * p[0] + 2 * p[1] + p[2]


def _peer(me, k):
    flips = ((k >> 2) & 1, (k >> 1) & 1, k & 1)
    return tuple(1 - a if f else a for a, f in zip(me, flips))


def _stack_exchange(me, st_in, st_out, n_whole, send_sems, recv_sems, local_sems, arrivals=True):
    mine = _slot(me)
    ns = len(st_in)
    part = lambda a, dev: st_in[a] if a >= ns - n_whole else st_in[a].at[_slot(dev)]
    local = [pltpu.make_async_copy(part(a, me), st_out[a].at[mine], local_sems.at[a]) for a in range(ns)]
    remote, landed = [], []
    for k in range(1, N_DEV):
        peer = _peer(me, k)
        for a in range(ns):
            sems = dict(send_sem=send_sems.at[7 * a + k - 1], recv_sem=recv_sems.at[7 * a + k - 1])
            remote.append(pltpu.make_async_remote_copy(
                src_ref=part(a, peer), dst_ref=st_out[a].at[mine],
                device_id=peer, device_id_type=MESH_ID, **sems))
            if arrivals:
                got = st_out[a].at[_slot(peer)]
                landed.append(pltpu.make_async_remote_copy(
                    src_ref=got, dst_ref=got, device_id=me, device_id_type=MESH_ID, **sems))
    return local, remote, landed


def _gather_in_proj(x2d, norm_in, w_in_sh, wo_shards, my_slot):
    n, d = x2d.shape
    esh = w_in_sh.shape[1]
    pw = 2 * esh
    n_chip = N_DEV // 2
    tm = _tile(n, 1024)
    n_i = n // tm
    mid = n_i // 2
    no = len(wo_shards)
    flip_at = lambda st: jnp.where(st == 1, 2, jnp.where(st == 2, 1, jnp.where(st == 3, 3, 0)))

    def body(me_ref, x_ref, g_ref, win_ref, *refs):
        del me_ref
        wo_in = refs[:no]
        proj_ref, h_ref, wg_ref = refs[no:no + 3]
        wo_out = refs[no + 3:2 * no + 3]
        wv, stage, h_s = refs[2 * no + 3:2 * no + 6]
        wo_stage = refs[2 * no + 6:3 * no + 6]
        send_sems, recv_sems, pair_sems, own_sems, wo_send, wo_recv, wo_local = refs[3 * no + 6:]
        st, i = pl.program_id(0), pl.program_id(1)
        x, y, c = _me()
        me, sibling = (x, y, c), (x, y, 1 - c)
        chips = [(1 - x, y), (x, 1 - y), (1 - x, 1 - y)]
        chip_id = lambda p: 2 * p[0] + p[1]

        def window(chip, core):
            return wv.at[chip_id(chip), :, pl.ds(pl.multiple_of(core * esh, LANE), esh)]

        def copy(k, block, to, src=None):
            dst = window(block[:2], block[2])
            return pltpu.make_async_remote_copy(
                src_ref=dst if src is None else src, dst_ref=dst,
                send_sem=send_sems.at[k], recv_sem=recv_sems.at[k], device_id=to, device_id_type=MESH_ID)

        def wo_copy(a, k, block, to, src=None):
            dst = wo_out[a].at[_slot(block)]
            return pltpu.make_async_remote_copy(
                src_ref=dst if src is None else src, dst_ref=dst,
                send_sem=wo_send.at[7 * a + k], recv_sem=wo_recv.at[7 * a + k], device_id=to, device_id_type=MESH_ID)

        def own_copy():
            return pltpu.make_async_copy(stage, window((x, y), c), own_sems.at[0])

        def wo_own_copy(a):
            return pltpu.make_async_copy(wo_stage[a], wo_out[a].at[_slot(me)], wo_local.at[a])

        def pair_copy(step):
            chip = jnp.bitwise_xor(chip_id((x, y)), flip_at(step))
            return pltpu.make_async_copy(wv.at[chip], wg_ref.at[:, pl.ds(pl.multiple_of(chip * pw, LANE), pw)],
                                         pair_sems.at[step])

        first = jnp.logical_and(st == 0, i == 0)

        @pl.when(first)
        def _():
            stage[...] = win_ref[...].astype(BF16)
            own_copy().start()
            copy(0, me, sibling, src=stage).start()
            for j in range(2):
                copy(1 + j, me, (*chips[j], c), src=stage).start()
            own_copy().wait()
            copy(0, sibling, me).wait_recv()
            pair_copy(0).start()

        for s_ in range(n_chip - 1):
            @pl.when(jnp.logical_and(st == s_, i == mid))
            def _():
                copy(1 + s_, (*chips[s_], c), me).wait_recv()
                copy(4 + s_, (*chips[s_], c), sibling).start()
                if s_ == 0:
                    copy(3, me, (*chips[2], c), src=stage).start()
                if s_ == 1:
                    for a in range(no):
                        wo_stage[a][...] = wo_in[a][...].astype(BF16)
                        wo_own_copy(a).start()
                        wo_copy(a, 0, me, sibling, src=wo_stage[a]).start()
                        for j, chip in enumerate(chips):
                            wo_copy(a, 1 + j, me, (*chip, c), src=wo_stage[a]).start()
                if s_ == 2:
                    for a in range(no):
                        for j, chip in enumerate(chips):
                            wo_copy(a, 1 + j, (*chip, c), me).wait_recv()
                            wo_copy(a, 4 + j, (*chip, c), sibling).start()

        for s_ in range(1, n_chip):
            @pl.when(jnp.logical_and(st == s_, i == 0))
            def _():
                copy(3 + s_, (*chips[s_ - 1], 1 - c), me).wait_recv()
                pair_copy(s_).start()

        xv = x_ref[...]
        h_s[...] = (xv * _rms_scale(xv) * g_ref[...]).astype(BF16)

        @pl.when(st == 0)
        def _():
            h_ref[...] = h_s[...]

        chip_now = jnp.bitwise_xor(chip_id((x, y)), flip_at(st))
        proj_ref[...] = _dot(h_s[...], wv[chip_now]).astype(BF16)

        @pl.when(jnp.logical_and(st == n_chip - 1, i == n_i - 1))
        def _():
            copy(0, me, sibling, src=stage).wait_send()
            for j, chip in enumerate(chips):
                copy(1 + j, me, (*chip, c), src=stage).wait_send()
                copy(4 + j, (*chip, c), sibling).wait_send()
            for s_ in range(n_chip):
                pair_copy(s_).wait()
            for a in range(no):
                wo_copy(a, 0, me, sibling, src=wo_stage[a]).wait_send()
                wo_copy(a, 0, sibling, me).wait_recv()
                for j, chip in enumerate(chips):
                    wo_copy(a, 1 + j, me, (*chip, c), src=wo_stage[a]).wait_send()
                    wo_copy(a, 4 + j, (*chip, c), sibling).wait_send()
                    wo_copy(a, 4 + j, (*chip, 1 - c), me).wait_recv()
                wo_own_copy(a).wait()

    any_spec = pl.BlockSpec(memory_space=pl.ANY)
    vmem = pl.BlockSpec(memory_space=pltpu.VMEM)
    grid_spec = pltpu.PrefetchScalarGridSpec(
        num_scalar_prefetch=1, grid=(n_chip, n_i),
        in_specs=[pl.BlockSpec((tm, d), lambda st, i, me: (i, 0)),
                  pl.BlockSpec((1, d), lambda st, i, me: (0, 0)), vmem] + [vmem] * no,
        out_specs=[pl.BlockSpec((tm, pw), lambda st, i, me: (i, jnp.bitwise_xor(me[0] // 2, flip_at(st)))),
                   pl.BlockSpec((tm, d), lambda st, i, me: (jnp.where(st == 0, i, n_i - 1), 0)),
                   any_spec] + [any_spec] * no,
        scratch_shapes=[pltpu.VMEM((n_chip, d, pw), BF16), pltpu.VMEM((d, esh), BF16), pltpu.VMEM((tm, d), BF16)] + [
            pltpu.VMEM(s.shape, BF16) for s in wo_shards] + [
            pltpu.SemaphoreType.DMA((7,)), pltpu.SemaphoreType.DMA((7,)),
            pltpu.SemaphoreType.DMA((n_chip,)), pltpu.SemaphoreType.DMA((1,)),
            pltpu.SemaphoreType.DMA((7 * no,)), pltpu.SemaphoreType.DMA((7 * no,)),
            pltpu.SemaphoreType.DMA((no,))])
    return pl.pallas_call(
        body, name="gather_in_proj", grid_spec=grid_spec,
        out_shape=[SDS((n, n_chip * pw), BF16), SDS((n, d), BF16), SDS((d, n_chip * pw), BF16)] + [
            SDS((N_DEV,) + s.shape, BF16) for s in wo_shards],
        compiler_params=pltpu.CompilerParams(dimension_semantics=("arbitrary", "arbitrary"),
                                             vmem_limit_bytes=VMEM_LIMIT),
    )(my_slot, x2d, norm_in, w_in_sh, *wo_shards)


N_CHIP = N_DEV // 2
CHIP_FLIPS = (3, 2, 1, 0)


def _owner_at(mine, j):
    flip = 0
    for pair, f in enumerate(CHIP_FLIPS):
        flip = jnp.where(j // 2 == pair, f, flip)
    return 2 * jnp.bitwise_xor(mine // 2, flip) + j % 2


def _dw_in_exchange(h, dproj, my_slot, packed):
    n, d = h.shape
    esh = dproj.shape[1] // N_DEV
    tk = _tile(n, 2048)
    nk = n // tk
    last_j = N_DEV - 1

    def body(me_ref, h_ref, dp_ref, pk_in, win_out, pk_out,
             acc, halfbuf, recvbuf, sendbuf, half_send, half_recv, win_send, win_recv,
             send_sems, recv_sems, local_sems):
        del me_ref
        j, k = pl.program_id(0), pl.program_id(1)
        x, y, c = _me()
        me, sibling = (x, y, c), (x, y, 1 - c)
        mine = _slot(me)
        my_chip = mine // 2

        def pack_copies():
            local = pltpu.make_async_copy(pk_in, pk_out.at[mine], local_sems.at[0])
            remote = [pltpu.make_async_remote_copy(
                src_ref=pk_in, dst_ref=pk_out.at[mine], send_sem=send_sems.at[kk - 1], recv_sem=recv_sems.at[kk - 1],
                device_id=_peer(me, kk), device_id_type=MESH_ID) for kk in range(1, N_DEV)]
            return local, remote

        def half_copy(jj):
            slot = (jj // 2) % 2
            return pltpu.make_async_remote_copy(
                src_ref=halfbuf.at[slot], dst_ref=recvbuf.at[slot],
                send_sem=half_send.at[slot], recv_sem=half_recv.at[slot],
                device_id=sibling, device_id_type=MESH_ID)

        def chip_copy(jj):
            slot = (jj // 2) % 2
            owner = _owner_at(mine, jj)
            return pltpu.make_async_remote_copy(
                src_ref=sendbuf.at[slot], dst_ref=win_out.at[my_chip],
                send_sem=win_send.at[slot], recv_sem=win_recv.at[my_chip],
                device_id=(owner // 4, (owner // 2) % 2, owner % 2), device_id_type=MESH_ID)

        def own_copy():
            return pltpu.make_async_copy(sendbuf.at[(last_j // 2) % 2], win_out.at[my_chip], local_sems.at[1])

        @pl.when(jnp.logical_and(j == 0, k == 0))
        def _():
            local, remote = pack_copies()
            for cp in [local] + remote:
                cp.start()

        @pl.when(k == 0)
        def _():
            acc[...] = jnp.zeros_like(acc)

        acc[...] += _dot_tn(dp_ref[...], h_ref[...])

        done = k == nk - 1
        combine = j % 2 == c
        slot = (j // 2) % 2

        @pl.when(jnp.logical_and(done, jnp.logical_not(combine)))
        def _():
            @pl.when(j >= 4)
            def _():
                half_copy(j - 4).wait_send()

            halfbuf[slot] = acc[...].astype(BF16)
            half_copy(j).start()

        @pl.when(jnp.logical_and(done, combine))
        def _():
            half_copy(j).wait_recv()

            @pl.when(j >= 4)
            def _():
                chip_copy(j - 4).wait_send()

            sendbuf[slot] = (acc[...] + recvbuf[slot].astype(F32)).astype(BF16)

            @pl.when(j < last_j - 1)
            def _():
                chip_copy(j).start()

            @pl.when(j >= last_j - 1)
            def _():
                own_copy().start()

        @pl.when(jnp.logical_and(j == last_j, done))
        def _():
            half_copy(5 - c).wait_send()
            half_copy(7 - c).wait_send()
            chip_copy(4 + c).wait_send()
            own_copy().wait()
            for chip in range(N_CHIP):
                @pl.when(chip != my_chip)
                def _():
                    landed = win_out.at[chip]
                    pltpu.make_async_remote_copy(
                        src_ref=landed, dst_ref=landed, send_sem=win_send.at[0], recv_sem=win_recv.at[chip],
                        device_id=me, device_id_type=MESH_ID).wait_recv()
            local, remote = pack_copies()
            for cp in remote:
                cp.wait_send()
            for kk in range(1, N_DEV):
                landed = pk_out.at[_slot(_peer(me, kk))]
                pltpu.make_async_remote_copy(
                    src_ref=landed, dst_ref=landed, send_sem=send_sems.at[kk - 1], recv_sem=recv_sems.at[kk - 1],
                    device_id=me, device_id_type=MESH_ID).wait_recv()
            local.wait()

    any_spec = pl.BlockSpec(memory_space=pl.ANY)
    grid_spec = pltpu.PrefetchScalarGridSpec(
        num_scalar_prefetch=1, grid=(N_DEV, nk),
        in_specs=[pl.BlockSpec((tk, d), lambda j, k, me: (k, 0)),
                  pl.BlockSpec((tk, esh), lambda j, k, me: (k, _owner_at(me[0], j))), any_spec],
        out_specs=[any_spec] * 2,
        scratch_shapes=[pltpu.VMEM((esh, d), F32)] + [pltpu.VMEM((2, esh, d), BF16)] * 3 + [
            pltpu.SemaphoreType.DMA((2,)), pltpu.SemaphoreType.DMA((2,)),
            pltpu.SemaphoreType.DMA((2,)), pltpu.SemaphoreType.DMA((N_CHIP,)),
            pltpu.SemaphoreType.DMA((N_DEV - 1,)), pltpu.SemaphoreType.DMA((N_DEV - 1,)),
            pltpu.SemaphoreType.DMA((2,))])
    return pl.pallas_call(
        body, name="dw_in_exchange", grid_spec=grid_spec,
        out_shape=[SDS((N_CHIP, esh, d), BF16), SDS((N_DEV,) + packed.shape, packed.dtype)],
        compiler_params=_params(("arbitrary", "arbitrary")),
    )(my_slot, h, dproj, packed)


def _finish_small(packs, late_packs, states, groups, chunk):
    gc = groups * chunk
    nw = len(states)

    def body(p_ref, l_ref, *refs):
        st = refs[:3 * nw]
        loss_ref = refs[3 * nw]
        outs = refs[3 * nw + 1:]
        row, col = _iotas(chunk)
        tril = col <= row

        def total(ref, rs):
            tot = ref[0, rs, :]
            for dev in range(1, N_DEV):
                tot = tot + ref[dev, rs, :]
            return tot

        def update(k, rs_w, g):
            w_ref, m_ref, v_ref = st[3 * k:3 * k + 3]
            _adamw_outputs(*[o.at[rs_w] for o in outs[4 * k:4 * k + 4]], g, w_ref[rs_w, :], m_ref[rs_w, :], v_ref[rs_w, :])

        for g in range(groups):
            rs = slice(g * chunk, (g + 1) * chunk)
            update(0, rs, jnp.where(tril, total(p_ref, rs), 0.0))
        slab = lambda k: slice(gc + k * SUBLANE, gc + (k + 1) * SUBLANE)
        for k in range(3):
            update(1 + k, slice(0, SUBLANE), total(p_ref, slab(k)))
        loss_ref[...] = jnp.full((SUBLANE, LANE), jnp.sum(total(p_ref, slab(3))), F32)
        update(4, slice(0, SUBLANE), total(l_ref, slice(0, SUBLANE)))

    flat = [a for s in states for a in s]
    vmem = pl.BlockSpec(memory_space=pltpu.VMEM)
    res = pl.pallas_call(
        body, name="finish_small",
        out_shape=[SDS((SUBLANE, LANE), F32)] + [SDS(s[0].shape, F32) for s in states for _ in range(4)],
        in_specs=[vmem] * (2 + len(flat)),
        out_specs=[vmem] * (1 + 4 * nw),
        compiler_params=pltpu.CompilerParams(vmem_limit_bytes=VMEM_LIMIT),
    )(packs, late_packs, *flat)
    return res[0], [res[1 + 4 * k:5 + 4 * k] for k in range(nw)]


def _branch_a_fwd(proj, norm_v, w_s, b_col):
    n = proj.shape[0]
    d = norm_v.shape[1]
    groups, chunk, _ = w_s.shape
    tr = _tile(n, 8 * chunk)

    def body(u_ref, v_ref, z_ref, gv_ref, ws_ref, b_ref, ya_ref, vn_s, pre_s):
        row, col = _iotas(chunk)
        tril = col <= row
        vg = _gelu(v_ref[...])[0].astype(F32)
        vn_s[...] = (vg * _rms_scale(vg) * gv_ref[...]).astype(BF16)
        pre_s[...] = _gelu(u_ref[...])[0] * _silu(z_ref[...])[0]
        for g in range(groups):
            wm = jnp.where(tril, ws_ref[g], 0.0).astype(BF16)
            cs = slice(g * chunk, (g + 1) * chunk)
            for c in range(tr // chunk):
                rs = slice(c * chunk, (c + 1) * chunk)
                mixed = _dot(wm, vn_s[rs, cs]) + b_ref[g]
                ya_ref[rs, cs] = (pre_s[rs, cs].astype(F32) * mixed).astype(BF16)

    seg = lambda k: pl.BlockSpec((tr, d), lambda i: (i, k))
    return pl.pallas_call(
        body, name="branch_a_fwd", grid=(n // tr,),
        in_specs=[seg(0), seg(1), seg(2),
                  pl.BlockSpec((1, d), lambda i: (0, 0)),
                  pl.BlockSpec((groups, chunk, chunk), lambda i: (0, 0, 0)),
                  pl.BlockSpec((groups, chunk, 1), lambda i: (0, 0, 0))],
        out_specs=pl.BlockSpec((tr, d), lambda i: (i, 0)),
        out_shape=SDS((n, d), BF16),
        scratch_shapes=[pltpu.VMEM((tr, d), BF16), pltpu.VMEM((tr, d), BF16)],
        compiler_params=_params(("parallel",)),
    )(proj, proj, proj, norm_v, w_s, b_col)


def _sb_fwd(proj, batch, seq, d, hd):
    heads = d // hd
    t = _tile(seq, SB_TILE)
    sw = _tile(t, SB_SCAN)
    nb = t // sw
    scale = hd ** -0.5
    nblk = seq // t
    nh = SB_HEADS
    wide = nh * hd
    cols = [slice(hh * hd, (hh + 1) * hd) for hh in range(nh)]

    def body(qs, k_ref, vs, zb_ref, yb_ref, o_ref, tot_ref, kts, later, acc):
        for jb in range(nblk):
            kts[jb] = k_ref[jb * t:(jb + 1) * t, :].T
        row, col = _iotas(t)
        later[...] = (row[:sw, :sw] > col[:sw, :sw]).astype(BF16)

        def qblock(i, carry):
            r0 = pl.multiple_of(i * t, t)

            def tile(j, runs):
                c0 = pl.multiple_of(j * t, t)
                logs = [_sb_logs(_dot(qs[pl.ds(r0, t), cs], kts[j, cs, :]), scale, None) for cs in cols]
                scans = [_dot(jnp.concatenate([logs[hh][1][:, b * sw:(b + 1) * sw] for b in range(nb)], axis=0),
                              later[...]) for hh in range(nh)]
                new_runs = []
                for hh in range(nh):
                    after = runs[hh]
                    blocks = [None] * nb
                    for b in reversed(range(nb)):
                        ks_ = slice(b * sw, (b + 1) * sw)
                        inside = scans[hh][b * t:(b + 1) * t]
                        blocks[b] = jnp.exp(logs[hh][0][:, ks_].astype(F32) + inside + after).astype(BF16)
                        after = after + inside[:, 0:1] + logs[hh][1][:, b * sw:b * sw + 1].astype(F32)
                    new_runs.append(after)
                    acc[:, cols[hh]] += _dot(jnp.concatenate(blocks, axis=1), vs[pl.ds(c0, t), cols[hh]])
                return tuple(new_runs)

            def diagonal_tile():
                starts = [b * sw for b in range(nb)]
                logs = [[_sb_logs(_dot(qs[pl.ds(r0 + s, t - s), cs], kts[i, cs, s:s + sw]), scale,
                                  col[:t - s, :sw] < row[:t - s, :sw]) for s in starts] for cs in cols]
                scans = [_dot(jnp.concatenate([lr for _, lr in logs[hh]], axis=0), later[...]) for hh in range(nh)]
                new_runs = []
                offs = [sum(t - s for s in starts[:b]) for b in range(nb)]
                for hh in range(nh):
                    after = jnp.zeros((t, 1), F32)
                    ws = [None] * nb
                    for b in reversed(range(nb)):
                        s = starts[b]
                        lb, lr = logs[hh][b]
                        inside = scans[hh][offs[b]:offs[b] + t - s]
                        ws[b] = jnp.exp(lb.astype(F32) + inside + after[s:]).astype(BF16)
                        total = inside[:, 0:1] + lr[:, 0:1].astype(F32)
                        after = after + total if s == 0 else jnp.concatenate([after[:s], after[s:] + total], axis=0)
                    new_runs.append(after)
                    acc[:, cols[hh]] = _dot(ws[0], vs[pl.ds(r0, sw), cols[hh]])
                    for b in range(1, nb):
                        acc[starts[b]:, cols[hh]] += _dot(ws[b], vs[pl.ds(r0 + starts[b], sw), cols[hh]])
                return tuple(new_runs)

            runs = diagonal_tile()
            runs = lax.fori_loop(0, i, lambda jj, rs: tile(i - 1 - jj, rs), runs)
            for hh in range(nh):
                out = acc[:, cols[hh]]
                o_ref[pl.ds(r0, t), cols[hh]] = out.astype(BF16)
                tot_ref[hh, pl.ds(r0, t), :] = runs[hh]
                sz, _ = _silu(zb_ref[pl.ds(r0, t), cols[hh]].astype(F32))
                yb_ref[pl.ds(r0, t), cols[hh]] = (out * sz).astype(BF16)
            return carry

        lax.fori_loop(0, nblk, qblock, 0)

    col0 = d // wide
    seg = lambda k: pl.BlockSpec((seq, wide), lambda b, h: (b, k * col0 + h))
    return pl.pallas_call(
        body, name="sb_fwd", grid=(batch, heads // nh),
        in_specs=[seg(3), seg(4), seg(5), seg(6)],
        out_specs=[pl.BlockSpec((seq, wide), lambda b, h: (b, h))] * 2 + [
            pl.BlockSpec((nh, seq, 1), lambda b, h: (b * (heads // nh) + h, 0, 0))],
        out_shape=[SDS((batch * seq, d), BF16), SDS((batch * seq, d), BF16), SDS((batch * heads, seq, 1), F32)],
        scratch_shapes=[pltpu.VMEM((nblk, wide, t), BF16), pltpu.VMEM((sw, sw), BF16), pltpu.VMEM((t, wide), F32)],
        compiler_params=_params(("parallel", "parallel")),
    )(proj, proj, proj, proj)


def _tail(x2d, tgt, ya, yb, proj, w_oa, w_ob, w_out, norm_final):
    n, d = x2d.shape
    e = proj.shape[1]
    tm = _tile(n, 512)
    steps = n // tm

    def body(x_ref, t_ref, ya_ref, yb_ref, ga_ref, gb_ref, woa_ref, wob_ref, wout_ref, gf_ref,
             dproj_ref, dx2_ref, dya_ref, dyb_ref, mrg_ref, dpa_ref, dpb_ref, loss_ref, dgf_ref, dg_s, dg_sems):
        i = pl.program_id(0)

        def gate_copy(step):
            rows_ = pl.ds(pl.multiple_of(step * tm, tm), tm)
            return pltpu.make_async_copy(dg_s.at[step % 2], dproj_ref.at[rows_, pl.ds(7 * d, 2 * d)],
                                         dg_sems.at[step % 2])

        @pl.when(i == 0)
        def _():
            loss_ref[...] = jnp.zeros_like(loss_ref)
            dgf_ref[...] = jnp.zeros_like(dgf_ref)

        @pl.when(i >= 2)
        def _():
            gate_copy(i - 2).wait()

        halves = [slice(hf * (tm // 2), (hf + 1) * (tm // 2)) for hf in range(2)] if tm >= 512 else [slice(0, tm)]
        gf = gf_ref[...]
        pa = [_dot(ya_ref[rs, :], woa_ref[...]) for rs in halves]
        pb = [_dot(yb_ref[rs, :], wob_ref[...]) for rs in halves]
        sa = [_sigmoid(ga_ref[rs, :].astype(F32)) for rs in halves]
        sb = [_sigmoid(gb_ref[rs, :].astype(F32)) for rs in halves]
        merged = [(sa[k] * pa[k] + sb[k] * pb[k]).astype(BF16) for k in range(len(halves))]
        for k, rs in enumerate(halves):
            mrg_ref[rs, :] = merged[k]
        x2 = [x_ref[rs, :] + _dot(merged[k], wout_ref[...]) for k, rs in enumerate(halves)]
        dx2 = []
        for k, rs in enumerate(halves):
            r2 = _rms_scale(x2[k])
            xh = x2[k] * r2
            diff = xh * gf - t_ref[rs, :]
            loss_ref[...] += jnp.sum(diff * diff, axis=0, keepdims=True) * (0.5 / d)
            dy = diff * (1.0 / d)
            dgf_ref[...] += jnp.sum(dy * xh, axis=0, keepdims=True)
            dxh = dy * gf
            dx2.append(r2 * (dxh - xh * jnp.mean(dxh * xh, axis=-1, keepdims=True)))
            dx2_ref[rs, :] = dx2[k]
        dm = [_dot_nt(dx2[k].astype(BF16), wout_ref[...]) for k in range(len(halves))]
        dpa, dpb = [], []
        for k, rs in enumerate(halves):
            dpa.append((dm[k] * sa[k]).astype(BF16))
            dpb.append((dm[k] * sb[k]).astype(BF16))
            dpa_ref[rs, :] = dpa[k]
            dpb_ref[rs, :] = dpb[k]
            dg_s[i % 2, rs, 0:d] = (dm[k] * pa[k] * (sa[k] * (1.0 - sa[k]))).astype(BF16)
            dg_s[i % 2, rs, d:2 * d] = (dm[k] * pb[k] * (sb[k] * (1.0 - sb[k]))).astype(BF16)
        gate_copy(i).start()
        for k, rs in enumerate(halves):
            dya_ref[rs, :] = _dot_nt(dpa[k], woa_ref[...]).astype(BF16)
        for k, rs in enumerate(halves):
            dyb_ref[rs, :] = _dot_nt(dpb[k], wob_ref[...]).astype(BF16)

        @pl.when(i == steps - 1)
        def _():
            if steps >= 2:
                gate_copy(i - 1).wait()
            gate_copy(i).wait()

    rows = lambda k=0: pl.BlockSpec((tm, d), lambda i: (i, k))
    full = pl.BlockSpec((d, d), lambda i: (0, 0), pipeline_mode=pl.Buffered(1))
    vec = pl.BlockSpec((1, d), lambda i: (0, 0))
    return pl.pallas_call(
        body, name="tail", grid=(steps,),
        in_specs=[rows(), rows(), rows(), rows(), rows(7), rows(8), full, full, full, vec],
        out_specs=[pl.BlockSpec(memory_space=pl.ANY),
                   rows(), rows(), rows(), rows(), rows(), rows(), vec, vec],
        out_shape=[SDS((n, e), BF16), SDS((n, d), F32), SDS((n, d), BF16), SDS((n, d), BF16),
                   SDS((n, d), BF16), SDS((n, d), BF16), SDS((n, d), BF16),
                   SDS((1, d), F32), SDS((1, d), F32)],
        scratch_shapes=[pltpu.VMEM((2, tm, 2 * d), BF16), pltpu.SemaphoreType.DMA((2,))],
        compiler_params=_params(("arbitrary",)),
    )(x2d, tgt, ya, yb, proj, proj, w_oa, w_ob, w_out, norm_final)


def _dw_o(pairs):
    n, d = pairs[0][0].shape
    tk = _tile(n, 1024)
    nk = n // tk
    npair = len(pairs)

    def body(*refs):
        a_refs, b_refs = refs[:npair], refs[npair:2 * npair]
        o_ref, acc = refs[2 * npair], refs[2 * npair + 1]
        p, k = pl.program_id(0), pl.program_id(1)

        @pl.when(k == 0)
        def _():
            acc[...] = jnp.zeros_like(acc)

        for q in range(npair):
            @pl.when(p == q)
            def _():
                acc[...] += _dot_tn(a_refs[q][...], b_refs[q][...].astype(BF16))

        @pl.when(k == nk - 1)
        def _():
            o_ref[0] = acc[...].astype(BF16)

    def tiles(q):
        return pl.BlockSpec((tk, d), lambda p, k: (jnp.where(p == q, k, jnp.where(p < q, 0, nk - 1)), 0))

    return pl.pallas_call(
        body, name="dw_o", grid=(npair, nk),
        in_specs=[tiles(q) for q in range(npair)] * 2,
        out_specs=pl.BlockSpec((1, d, d), lambda p, k: (p, 0, 0)),
        out_shape=SDS((npair, d, d), BF16),
        scratch_shapes=[pltpu.VMEM((d, d), F32)],
        compiler_params=_params(("arbitrary", "arbitrary")),
    )(*[a for a, _ in pairs], *[b for _, b in pairs])


def _sb_bwd(proj, o, dyb, tot, dproj, dw_stack, packed, batch, seq, d, hd):
    heads = d // hd
    t = _tile(seq, SB_TILE_BWD)
    sw = _tile(t, SB_SCAN)
    nb = t // sw
    scale = hd ** -0.5
    nblk = seq // t
    nh = SB_HEADS
    wide = nh * hd
    hs = range(nh)
    cols = [slice(hh * hd, (hh + 1) * hd) for hh in hs]
    blocks = [slice(b * sw, (b + 1) * sw) for b in range(nb)]
    last = slice(sw - 1, sw)

    def compute(qs, ks, v_ref, zb_ref, o_ref, dyb_ref, tot_ref, kts, vts, dos, dzb, dq_all, dkv_t, qt_s, dot_s,
                upto, before, dq):
        for jb in range(nblk):
            rows = slice(jb * t, (jb + 1) * t)
            kts[jb] = ks[rows, :].T
            vts[jb] = v_ref[rows, :].T
        sz, dsz = _silu(zb_ref[...])
        dyb_v = dyb_ref[...]
        dos[...] = dyb_v * sz
        dzb[...] = dyb_v * o_ref[...] * dsz
        row, col = _iotas(t)
        upto[...] = (row[:sw, :sw] <= col[:sw, :sw]).astype(BF16)
        before[...] = (row[:sw, :sw] < col[:sw, :sw]).astype(BF16)

        def qblock(i, carry):
            r0 = pl.multiple_of(i * t, t)

            def tile(j, sums):
                c0 = pl.multiple_of(j * t, t)
                q_i = [qs[pl.ds(r0, t), cs] for cs in cols]
                do_i = [dos[pl.ds(r0, t), cs] for cs in cols]
                logs = [_sb_logs(_dot(q_i[hh], kts[j, cols[hh], :]), scale, None) for hh in hs]
                scans = [_dot(jnp.concatenate([logs[hh][1][:, ks_] for ks_ in blocks], axis=0), upto[...]) for hh in hs]
                dw = [_dot(do_i[hh], vts[j, cols[hh], :]) for hh in hs]
                ws, gs, new_runs = [], [], []
                for hh in hs:
                    left = tot_ref[hh, pl.ds(r0, t), :] - sums[hh][0]
                    w_b, g_b = [], []
                    for b, ks_ in enumerate(blocks):
                        inside = scans[hh][b * t:(b + 1) * t]
                        w = jnp.exp(logs[hh][0][:, ks_].astype(F32) + (left - inside))
                        w_b.append(w.astype(BF16))
                        g_b.append((dw[hh][:, ks_] * w).astype(BF16))
                        left = left - inside[:, last]
                    ws.append(jnp.concatenate(w_b, axis=1))
                    gs.append(g_b)
                    new_runs.append(tot_ref[hh, pl.ds(r0, t), :] - left)
                gscans = [_dot(jnp.concatenate(gs[hh], axis=0), before[...]) for hh in hs]
                dzs, new_gruns = [], []
                for hh in hs:
                    g_before = sums[hh][1]
                    dz_b = []
                    for b, ks_ in enumerate(blocks):
                        inside = gscans[hh][b * t:(b + 1) * t]
                        beta = jnp.exp(logs[hh][0][:, ks_]).astype(F32)
                        g = gs[hh][b].astype(F32)
                        dz_b.append((g - (g + inside + g_before) * beta).astype(BF16))
                        g_before = g_before + inside[:, last] + g[:, last]
                    dzs.append(jnp.concatenate(dz_b, axis=1))
                    new_gruns.append(g_before)
                for hh in hs:
                    dkv_t[1, j, cols[hh], :] += _dot(dot_s[cols[hh], :], ws[hh])
                for hh in hs:
                    dkv_t[0, j, cols[hh], :] += _dot(qt_s[cols[hh], :], dzs[hh])
                for hh in hs:
                    dq[:, cols[hh]] += _dot(dzs[hh], ks[pl.ds(c0, t), cols[hh]])
                return tuple((new_runs[hh], new_gruns[hh]) for hh in hs)

            def diagonal_tile(sums):
                starts = [b * sw for b in range(nb)]
                offs = [sum(t - s for s in starts[:b]) for b in range(nb)]
                q_b = [[qs[pl.ds(r0 + s, t - s), cs] for s in starts] for cs in cols]
                do_b = [[dos[pl.ds(r0 + s, t - s), cs] for s in starts] for cs in cols]
                logs = [[_sb_logs(_dot(q_b[hh][b], kts[i, cols[hh], s:s + sw]), scale,
                                  col[:t - s, :sw] < row[:t - s, :sw]) for b, s in enumerate(starts)] for hh in hs]
                dw = [[_dot(do_b[hh][b], vts[i, cols[hh], s:s + sw]) for b, s in enumerate(starts)] for hh in hs]
                scans = [_dot(jnp.concatenate([lr for _, lr in logs[hh]], axis=0), upto[...]) for hh in hs]
                ws, gs = [], []
                for hh in hs:
                    left = tot_ref[hh, pl.ds(r0, t), :] - sums[hh][0]
                    w_b, g_b = [], []
                    for b, s in enumerate(starts):
                        inside = scans[hh][offs[b]:offs[b] + t - s]
                        w = jnp.exp(logs[hh][b][0].astype(F32) + (left[s:] - inside))
                        w_b.append(w.astype(BF16))
                        g_b.append((dw[hh][b] * w).astype(BF16))
                        total = inside[:, last]
                        left = left - total if s == 0 else jnp.concatenate([left[:s], left[s:] - total], axis=0)
                    ws.append(w_b)
                    gs.append(g_b)
                gscans = [_dot(jnp.concatenate(gs[hh], axis=0), before[...]) for hh in hs]
                dzs = []
                for hh in hs:
                    g_before = sums[hh][1]
                    dz_b = []
                    for b, s in enumerate(starts):
                        inside = gscans[hh][offs[b]:offs[b] + t - s]
                        beta = jnp.exp(logs[hh][b][0]).astype(F32)
                        g = gs[hh][b].astype(F32)
                        dz_b.append((g - (g + inside + g_before[s:]) * beta).astype(BF16))
                        total = inside[:, last] + g[:, last]
                        g_before = g_before + total if s == 0 else jnp.concatenate(
                            [g_before[:s], g_before[s:] + total], axis=0)
                    dzs.append(dz_b)
                for hh in hs:
                    for b, s in enumerate(starts):
                        dkv_t[1, i, cols[hh], s:s + sw] = _dot(dot_s[cols[hh], s:], ws[hh][b])
                for hh in hs:
                    for b, s in enumerate(starts):
                        dkv_t[0, i, cols[hh], s:s + sw] = _dot(qt_s[cols[hh], s:], dzs[hh][b])
                for hh in hs:
                    for b, s in enumerate(starts):
                        dq[s:, cols[hh]] += _dot(dzs[hh][b], ks[pl.ds(r0 + s, sw), cols[hh]])

            qt_s[...] = qs[pl.ds(r0, t), :].T
            dot_s[...] = dos[pl.ds(r0, t), :].T
            zero = jnp.zeros((t, 1), F32)
            dq[...] = jnp.zeros_like(dq)
            sums = lax.fori_loop(0, i, tile, ((zero, zero),) * nh)
            diagonal_tile(sums)
            dq_all[pl.ds(r0, t), :] = dq[...]
            return carry

        lax.fori_loop(0, nblk, qblock, 0)

    pairs = heads // nh

    nst = dw_stack.shape[0]
    ns = nst + 1

    def body(qs, ks, v_ref, zb_ref, o_ref, dyb_ref, tot_ref, dproj_in, dw_ref, pk_ref, out_ref, *refs):
        del dproj_in
        st_in = [dw_ref.at[k] for k in range(nst)] + [pk_ref]
        st_out = refs[:ns]
        (kts, vts, dos, dzb, dq_all, dkv_t, qt_s, dot_s, upto, before, dq, stage, stage_sems,
         send_sems, recv_sems, local_sems) = refs[ns:]
        step = pl.program_id(0) * pairs + pl.program_id(1)
        exchange = functools.partial(_stack_exchange, _me(), st_in, st_out, 1, send_sems, recv_sems, local_sems)

        @pl.when(step == 0)
        def _():
            local, remote, _ = exchange(arrivals=False)
            for cp in local + remote:
                cp.start()

        def out_copies(s):
            rows_ = pl.ds(pl.multiple_of((s // pairs) * seq, seq), seq)
            return [pltpu.make_async_copy(
                stage.at[k], out_ref.at[rows_, pl.ds(pl.multiple_of((3 + k) * d + (s % pairs) * wide, wide), wide)],
                stage_sems.at[k]) for k in range(4)]

        compute(qs, ks, v_ref, zb_ref, o_ref, dyb_ref, tot_ref, kts, vts, dos, dzb, dq_all, dkv_t, qt_s, dot_s,
                upto, before, dq)

        @pl.when(step > 0)
        def _():
            for cp in out_copies(step - 1):
                cp.wait()

        stage[0] = (dq_all[...] * scale).astype(BF16)
        for jb in range(nblk):
            stage[1, jb * t:(jb + 1) * t, :] = (dkv_t[0, jb] * scale).astype(BF16).T
            stage[2, jb * t:(jb + 1) * t, :] = dkv_t[1, jb].astype(BF16).T
        stage[3] = dzb[...]
        for cp in out_copies(step):
            cp.start()

        @pl.when(step == batch * pairs - 1)
        def _():
            for cp in out_copies(step):
                cp.wait()
            local, remote, landed = exchange()
            for cp in remote:
                cp.wait_send()
            for cp in landed:
                cp.wait_recv()
            for cp in local:
                cp.wait()

    col0 = d // wide
    seg = lambda k: pl.BlockSpec((seq, wide), lambda b, h: (b, k * col0 + h))
    head = pl.BlockSpec((seq, wide), lambda b, h: (b, h))
    any_spec = pl.BlockSpec(memory_space=pl.ANY)
    return pl.pallas_call(
        body, name="sb_bwd", grid=(batch, pairs),
        in_specs=[seg(3), seg(4), seg(5), seg(6), head, head,
                  pl.BlockSpec((nh, seq, 1), lambda b, h: (b * pairs + h, 0, 0))] + [any_spec] * 3,
        out_specs=[any_spec] * (ns + 1),
        out_shape=[SDS(dproj.shape, dproj.dtype)] + [SDS(dw_stack.shape[1:], dw_stack.dtype)] * nst + [
            SDS((N_DEV,) + packed.shape, packed.dtype)],
        input_output_aliases={7: 0},
        scratch_shapes=[pltpu.VMEM((nblk, wide, t), BF16)] * 2 + [
            pltpu.VMEM((seq, wide), BF16), pltpu.VMEM((seq, wide), BF16),
            pltpu.VMEM((seq, wide), F32), pltpu.VMEM((2, nblk, wide, t), F32),
            pltpu.VMEM((wide, t), BF16), pltpu.VMEM((wide, t), BF16),
            pltpu.VMEM((sw, sw), BF16), pltpu.VMEM((sw, sw), BF16), pltpu.VMEM((t, wide), F32),
            pltpu.VMEM((4, seq, wide), BF16), pltpu.SemaphoreType.DMA((4,)),
            pltpu.SemaphoreType.DMA((7 * ns,)), pltpu.SemaphoreType.DMA((7 * ns,)),
            pltpu.SemaphoreType.DMA((ns,))],
        compiler_params=_params(("arbitrary", "arbitrary")),
    )(proj, proj, proj, proj, o, dyb, tot, dproj, dw_stack, packed)


def _branch_a_bwd(proj, dya, norm_v, w_s, b_col, dproj):
    n = proj.shape[0]
    d = norm_v.shape[1]
    groups, chunk, _ = w_s.shape
    tr = _tile(n, 4 * chunk)

    def body(u_ref, v_ref, z_ref, dya_ref, gv_ref, ws_ref, b_ref, dproj_in,
             out_ref, dws_ref, dbias_ref, dgv_ref, vn_s, dmix_s, dvn_s, db_ref):
        del dproj_in

        @pl.when(pl.program_id(0) == 0)
        def _():
            dws_ref[...] = jnp.zeros_like(dws_ref)
            db_ref[...] = jnp.zeros_like(db_ref)
            dgv_ref[...] = jnp.zeros_like(dgv_ref)

        row, col = _iotas(chunk)
        tril = col <= row
        gv = gv_ref[...]
        vg16, dvg_dv = _gelu(v_ref[...])
        vg = vg16.astype(F32)
        r = _rms_scale(vg)
        vh = vg * r
        vn_s[...] = (vh * gv).astype(BF16)
        ug, dug_du = _gelu(u_ref[...])
        sz, dsz = _silu(z_ref[...])
        dya_v = dya_ref[...]
        dmix_s[...] = dya_v * ug * sz
        du_scale = sz * dug_du
        dz_scale = ug * dsz
        for g in range(groups):
            wm = jnp.where(tril, ws_ref[g], 0.0).astype(BF16)
            cs = slice(g * chunk, (g + 1) * chunk)
            for c in range(tr // chunk):
                rs = slice(c * chunk, (c + 1) * chunk)
                vn = vn_s[rs, cs]
                mixed = _dot(wm, vn) + b_ref[g]
                dmix16 = dmix_s[rs, cs]
                dws_ref[g] += _dot_nt(dmix16, vn)
                db_ref[g] += dmix16.astype(F32)
                dvn_s[rs, cs] = _dot_tn(wm, dmix16)
                t_u = dya_v[rs, cs] * mixed.astype(BF16)
                out_ref[rs, g * chunk:(g + 1) * chunk] = t_u * du_scale[rs, cs]
                out_ref[rs, 2 * d + g * chunk:2 * d + (g + 1) * chunk] = t_u * dz_scale[rs, cs]
        dvn = dvn_s[...]
        dgv_ref[...] += jnp.sum(dvn * vh, axis=0, keepdims=True)
        dvh = dvn * gv
        dvg = r * (dvh - vh * jnp.mean(dvh * vh, axis=-1, keepdims=True))
        out_ref[:, d:2 * d] = (dvg * dvg_dv.astype(F32)).astype(BF16)

        @pl.when(pl.program_id(0) == n // tr - 1)
        def _():
            for g in range(groups):
                dbias_ref[g:g + 1, :] = jnp.sum(db_ref[g].T, axis=0, keepdims=True)

    seg = lambda k: pl.BlockSpec((tr, d), lambda i: (i, k))
    return pl.pallas_call(
        body, name="branch_a_bwd", grid=(n // tr,),
        in_specs=[seg(0), seg(1), seg(2), seg(0),
                  pl.BlockSpec((1, d), lambda i: (0, 0)),
                  pl.BlockSpec((groups, chunk, chunk), lambda i: (0, 0, 0)),
                  pl.BlockSpec((groups, chunk, 1), lambda i: (0, 0, 0)),
                  pl.BlockSpec(memory_space=pl.ANY)],
        out_specs=[pl.BlockSpec((tr, 3 * d), lambda i: (i, 0)),
                   pl.BlockSpec((groups, chunk, chunk), lambda i: (0, 0, 0)),
                   pl.BlockSpec((groups, chunk), lambda i: (0, 0)),
                   pl.BlockSpec((1, d), lambda i: (0, 0))],
        out_shape=[SDS(dproj.shape, dproj.dtype), SDS((groups, chunk, chunk), F32),
                   SDS((groups, chunk), F32), SDS((1, d), F32)],
        input_output_aliases={7: 0},
        scratch_shapes=[pltpu.VMEM((tr, d), BF16), pltpu.VMEM((tr, d), BF16), pltpu.VMEM((tr, d), F32),
                        pltpu.VMEM((groups, chunk, chunk), F32)],
        compiler_params=_params(("arbitrary",)),
    )(proj, proj, proj, dya, norm_v, w_s, b_col, dproj)


def _dx(dproj, wg_in, x2d, dx2, norm_in):
    n, d = x2d.shape
    e = wg_in.shape[1]
    tm = _tile(n, 256)

    nload = N_DEV // 2
    ck = e // nload

    def body(dp_ref, w_hbm, x_ref, dx2_ref, g_ref, gx_ref, dg_ref, w_v, sems):
        def load(k):
            cols = slice(k * ck, (k + 1) * ck)
            return pltpu.make_async_copy(w_hbm.at[:, cols], w_v.at[:, cols], sems.at[k])

        def finish(dh):
            x = x_ref[...]
            r = _rms_scale(x)
            xh = x * r
            dg_ref[...] += jnp.sum(dh * xh, axis=0, keepdims=True)
            dxh = dh * g_ref[...]
            gx_ref[...] = dx2_ref[...] + r * (dxh - xh * jnp.mean(dxh * xh, axis=-1, keepdims=True))

        @pl.when(pl.program_id(0) == 0)
        def _():
            dg_ref[...] = jnp.zeros_like(dg_ref)
            for k in range(nload):
                load(k).start()
            dh = None
            for k in range(nload):
                load(k).wait()
                part = _dot_nt(dp_ref[:, k * ck:(k + 1) * ck], w_v[:, k * ck:(k + 1) * ck])
                dh = part if dh is None else dh + part
            finish(dh)

        @pl.when(pl.program_id(0) > 0)
        def _():
            finish(_dot_nt(dp_ref[...], w_v[...]))

    rows = pl.BlockSpec((tm, d), lambda i: (i, 0))
    vec = pl.BlockSpec((1, d), lambda i: (0, 0))
    return pl.pallas_call(
        body, name="dx", grid=(n // tm,),
        in_specs=[pl.BlockSpec((tm, e), lambda i: (i, 0)), pl.BlockSpec(memory_space=pl.ANY), rows, rows, vec],
        out_specs=[rows, vec],
        out_shape=[SDS((n, d), F32), SDS((1, d), F32)],
        scratch_shapes=[pltpu.VMEM((d, e), BF16), pltpu.SemaphoreType.DMA((nload,))],
        compiler_params=_params(("arbitrary",)),
    )(dproj, wg_in, x2d, dx2, norm_in)


def _adamw_outputs(g_ref, d_ref, m_ref, v_ref, g, w, m, v):
    delta, m2, v2 = _adamw(w, g, m, v)
    g_ref[...] = g
    d_ref[...] = delta
    m_ref[...] = m2
    v_ref[...] = v2


def _reduce_adamw(slots, w, m, v, name, transposed=False):
    r, c = w.shape
    ns = slots.shape[0]
    tr = _tile(r, 128)

    def body(s_ref, w_ref, m_ref, v_ref, g_out, d_out, m_out, v_out):
        g = s_ref[0].astype(F32)
        for k in range(1, ns):
            g = g + s_ref[k].astype(F32)
        if transposed:
            g = g.T
        _adamw_outputs(g_out, d_out, m_out, v_out, g, w_ref[...], m_ref[...], v_ref[...])

    blk = pl.BlockSpec((tr, c), lambda i: (i, 0))
    slot_blk = (pl.BlockSpec((ns, c, tr), lambda i: (0, 0, i)) if transposed
                else pl.BlockSpec((ns, tr, c), lambda i: (0, i, 0)))
    return pl.pallas_call(
        body, name=name, grid=(r // tr,),
        in_specs=[slot_blk, blk, blk, blk],
        out_specs=[blk] * 4,
        out_shape=[SDS((r, c), F32)] * 4,
        compiler_params=_params(("parallel",)),
    )(slots, w, m, v)


def kernel(x, norm_in, w_in, norm_v, w_s, b_s, w_o_gmlp, w_o_sb, w_out, norm_final, loss_target, m_norm_in, m_w_in, m_norm_v, m_w_s, m_b_s, m_w_o_gmlp, m_w_o_sb, m_w_out, m_norm_final, v_norm_in, v_w_in, v_norm_v, v_w_s, v_b_s, v_w_o_gmlp, v_w_o_sb, v_w_out, v_norm_final):
    batch, seq, d = x.shape
    n = batch * seq
    groups, chunk = w_s.shape[1], w_s.shape[2]
    hd = LANE
    x2d = x.reshape(n, d)
    tgt = loss_target.reshape(n, d)
    b_col = b_s[0].reshape(groups, chunk, 1)
    norm_final2 = norm_final.reshape(1, d)

    my_slot = _slot(_me()).astype(jnp.int32).reshape(1)
    proj, h, wg_in, wg_oa, wg_ob, wg_out = _gather_in_proj(
        x2d, norm_in, w_in[0], [w_o_gmlp[0], w_o_sb[0], w_out[0]], my_slot)
    rsh = wg_oa.shape[1]
    wf_oa, wf_ob, wf_out = (w.reshape(N_DEV * rsh, d) for w in (wg_oa, wg_ob, wg_out))
    ya = _branch_a_fwd(proj, norm_v, w_s[0], b_col)
    yb, o, sb_tot = _sb_fwd(proj, batch, seq, d, hd)
    dproj, dx2, dya, dyb, merged, dpa, dpb, loss_vec, dgf = _tail(
        x2d, tgt, ya, yb, proj, wf_oa, wf_ob, wf_out, norm_final2)
    gp_wo = _dw_o([(ya, dpa), (yb, dpb), (merged, dx2)])
    dproj, gp_ws, gp_b, gp_nv = _branch_a_bwd(proj, dya, norm_v, w_s[0], b_col, dproj)

    slab = lambda a: a.reshape(d // LANE, LANE)
    gc = groups * chunk
    packed = jnp.concatenate([gp_ws.reshape(gc, chunk), gp_b, slab(gp_nv), slab(dgf), slab(loss_vec)], axis=0)
    dproj, s_oa, s_ob, s_out, packs = _sb_bwd(
        proj, o, dyb, sb_tot, dproj, gp_wo.reshape(3, N_DEV, rsh, d), packed, batch, seq, d, hd)
    grad_x, gp_nin = _dx(dproj, wg_in, x2d, dx2, norm_in)
    s_win, late_packs = _dw_in_exchange(h, dproj, my_slot, slab(gp_nin))
    small = {"w_s": lambda a: a.reshape(gc, chunk), "b_s": lambda a: a[0], "norm_v": slab, "norm_final": slab,
             "norm_in": slab}
    given = {"w_s": (w_s, m_w_s, v_w_s), "b_s": (b_s, m_b_s, v_b_s), "norm_v": (norm_v, m_norm_v, v_norm_v),
             "norm_final": (norm_final, m_norm_final, v_norm_final), "norm_in": (norm_in, m_norm_in, v_norm_in)}
    loss_slab, small_res = _finish_small(
        packs, late_packs, [tuple(small[k](a) for a in given[k]) for k in small], groups, chunk)
    loss = loss_slab[0, 0]

    res = dict(zip(small, small_res))
    res["w_in"] = _reduce_adamw(s_win, w_in[0], m_w_in[0], v_w_in[0], "adamw_w_in", transposed=True)
    res["w_o_gmlp"] = _reduce_adamw(s_oa, w_o_gmlp[0], m_w_o_gmlp[0], v_w_o_gmlp[0], "adamw_w_o_gmlp")
    res["w_o_sb"] = _reduce_adamw(s_ob, w_o_sb[0], m_w_o_sb[0], v_w_o_sb[0], "adamw_w_o_sb")
    res["w_out"] = _reduce_adamw(s_out, w_out[0], m_w_out[0], v_w_out[0], "adamw_w_out")

    shapes = {"norm_in": norm_in.shape, "w_in": w_in.shape, "norm_v": norm_v.shape, "w_s": w_s.shape,
              "b_s": b_s.shape, "w_o_gmlp": w_o_gmlp.shape, "w_o_sb": w_o_sb.shape, "w_out": w_out.shape,
              "norm_final": norm_final.shape}
    names = list(shapes)
    outs = [loss, grad_x.reshape(batch, seq, d)]
    for kind in range(4):
        outs += [res[name][kind].reshape(shapes[name]) for name in names]
    return tuple(outs)
```

```python
import functools
import math

import jax
import jax.numpy as jnp
from jax import lax
from jax.experimental import pallas as pl
from jax.experimental.pallas import tpu as pltpu

F32 = jnp.float32
BF16 = jnp.bfloat16
SDS = jax.ShapeDtypeStruct
MESH_ID = pl.DeviceIdType.MESH

N_DEV = 8
LANE = 128
SUBLANE = 8
VMEM_LIMIT = 56 * 1024 * 1024
DX_VMEM_LIMIT = 60 * 1024 * 1024
SB_TILE = 512
SB_TILE_BWD = 512
SB_SCAN = 256
SB_HEADS = 2
MASKED_LOG = -1e30
RMS_EPS = 1e-6

ADAM_LR = 0.001
ADAM_B1 = 0.9
ADAM_B2 = 0.999
ADAM_EPS = 1e-08
ADAM_WD = 0.01
ADAM_STEP = 10

NT_DIMS = (((1,), (1,)), ((), ()))
TN_DIMS = (((0,), (0,)), ((), ()))


def _params(semantics=None, vmem_limit=VMEM_LIMIT):
    return pltpu.CompilerParams(dimension_semantics=semantics, vmem_limit_bytes=vmem_limit)


def _tile(n, preferred):
    t = min(n, preferred)
    assert n % t == 0, (n, t)
    return t


def _sigmoid(x):
    return 1.0 / (1.0 + jnp.exp(-x))


def _silu(x):
    s = _sigmoid(x)
    return x * s, s * (1.0 + x * (1.0 - s))


def _gelu(x):
    k = math.sqrt(2.0 / math.pi)
    x2 = x * x
    t = jnp.tanh(k * (x + 0.044715 * (x * x2)))
    cdf = 0.5 * (1.0 + t)
    return x * cdf, cdf + 0.5 * x * (1.0 - t * t) * (k * (1.0 + 3.0 * 0.044715 * x2))


def _rms_scale(x):
    return lax.rsqrt(jnp.mean(x * x, axis=-1, keepdims=True) + RMS_EPS)


def _iotas(n):
    return (lax.broadcasted_iota(jnp.int32, (n, n), 0), lax.broadcasted_iota(jnp.int32, (n, n), 1))


def _adamw(w, g, m, v):
    m = ADAM_B1 * m + (1.0 - ADAM_B1) * g
    v = ADAM_B2 * v + (1.0 - ADAM_B2) * (g * g)
    m_hat = m / (1.0 - ADAM_B1 ** ADAM_STEP)
    v_hat = v / (1.0 - ADAM_B2 ** ADAM_STEP)
    delta = -ADAM_LR * (m_hat / (jnp.sqrt(v_hat) + ADAM_EPS) + ADAM_WD * w)
    return delta, m, v


def _dot(a, b):
    return jnp.dot(a, b, preferred_element_type=F32)


def _dot_nt(a, b):
    return lax.dot_general(a, b, NT_DIMS, preferred_element_type=F32)


def _dot_tn(a, b):
    return lax.dot_general(a, b, TN_DIMS, preferred_element_type=F32)


def _sb_logs(raw, scale, valid):
    z = (raw * scale).astype(BF16)
    log_beta = jnp.minimum(z, 0) - jnp.log(1 + jnp.exp(-jnp.abs(z)))
    log_rest = log_beta - z
    if valid is not None:
        log_beta = jnp.where(valid, log_beta, MASKED_LOG)
        log_rest = jnp.where(valid, log_rest, 0)
    return log_beta, log_rest


def _me():
    return lax.axis_index("x"), lax.axis_index("y"), lax.axis_index("c")


def _slot(p):
    return 4 * p[0] + 2 * p[1] + p[2]


def _peer(me, k):
    flips = ((k >> 2) & 1, (k >> 1) & 1, k & 1)
    return tuple(1 - a if f else a for a, f in zip(me, flips))


def _stack_exchange(me, st_in, st_out, n_whole, send_sems, recv_sems, local_sems, arrivals=True):
    mine = _slot(me)
    ns = len(st_in)
    part = lambda a, dev: st_in[a] if a >= ns - n_whole else st_in[a].at[_slot(dev)]
    local = [pltpu.make_async_copy(part(a, me), st_out[a].at[mine], local_sems.at[a]) for a in range(ns)]
    remote, landed = [], []
    for k in range(1, N_DEV):
        peer = _peer(me, k)
        for a in range(ns):
            sems = dict(send_sem=send_sems.at[7 * a + k - 1], recv_sem=recv_sems.at[7 * a + k - 1])
            remote.append(pltpu.make_async_remote_copy(
                src_ref=part(a, peer), dst_ref=st_out[a].at[mine],
                device_id=peer, device_id_type=MESH_ID, **sems))
            if arrivals:
                got = st_out[a].at[_slot(peer)]
                landed.append(pltpu.make_async_remote_copy(
                    src_ref=got, dst_ref=got, device_id=me, device_id_type=MESH_ID, **sems))
    return local, remote, landed


def _gather_in_proj(x2d, norm_in, w_in_sh, wo_shards, my_slot):
    n, d = x2d.shape
    esh = w_in_sh.shape[1]
    pw = 2 * esh
    n_chip = N_DEV // 2
    tm = _tile(n, 1024)
    n_i = n // tm
    mid = n_i // 2
    no = len(wo_shards)
    flip_at = lambda st: jnp.where(st == 1, 2, jnp.where(st == 2, 1, jnp.where(st == 3, 3, 0)))

    def body(me_ref, x_ref, g_ref, win_ref, *refs):
        del me_ref
        wo_in = refs[:no]
        proj_ref, h_ref, wg_ref = refs[no:no + 3]
        wo_out = refs[no + 3:2 * no + 3]
        wv, stage, h_s = refs[2 * no + 3:2 * no + 6]
        wo_stage = refs[2 * no + 6:3 * no + 6]
        send_sems, recv_sems, pair_sems, own_sems, wo_send, wo_recv, wo_local = refs[3 * no + 6:]
        st, i = pl.program_id(0), pl.program_id(1)
        x, y, c = _me()
        me, sibling = (x, y, c), (x, y, 1 - c)
        chips = [(1 - x, y), (x, 1 - y), (1 - x, 1 - y)]
        chip_id = lambda p: 2 * p[0] + p[1]

        def window(chip, core):
            return wv.at[chip_id(chip), :, pl.ds(pl.multiple_of(core * esh, LANE), esh)]

        def copy(k, block, to, src=None):
            dst = window(block[:2], block[2])
            return pltpu.make_async_remote_copy(
                src_ref=dst if src is None else src, dst_ref=dst,
                send_sem=send_sems.at[k], recv_sem=recv_sems.at[k], device_id=to, device_id_type=MESH_ID)

        def wo_copy(a, k, block, to, src=None):
            dst = wo_out[a].at[_slot(block)]
            return pltpu.make_async_remote_copy(
                src_ref=dst if src is None else src, dst_ref=dst,
                send_sem=wo_send.at[7 * a + k], recv_sem=wo_recv.at[7 * a + k], device_id=to, device_id_type=MESH_ID)

        def own_copy():
            return pltpu.make_async_copy(stage, window((x, y), c), own_sems.at[0])

        def wo_own_copy(a):
            return pltpu.make_async_copy(wo_stage[a], wo_out[a].at[_slot(me)], wo_local.at[a])

        def pair_copy(step):
            chip = jnp.bitwise_xor(chip_id((x, y)), flip_at(step))
            return pltpu.make_async_copy(wv.at[chip], wg_ref.at[:, pl.ds(pl.multiple_of(chip * pw, LANE), pw)],
                                         pair_sems.at[step])

        first = jnp.logical_and(st == 0, i == 0)

        @pl.when(first)
        def _():
            stage[...] = win_ref[...].astype(BF16)
            own_copy().start()
            copy(0, me, sibling, src=stage).start()
            for j in range(2):
                copy(1 + j, me, (*chips[j], c), src=stage).start()
            own_copy().wait()
            copy(0, sibling, me).wait_recv()
            pair_copy(0).start()

        for s_ in range(n_chip - 1):
            @pl.when(jnp.logical_and(st == s_, i == mid))
            def _():
                copy(1 + s_, (*chips[s_], c), me).wait_recv()
                copy(4 + s_, (*chips[s_], c), sibling).start()
                if s_ == 0:
                    copy(3, me, (*chips[2], c), src=stage).start()
                if s_ == 1:
                    for a in range(no):
                        wo_stage[a][...] = wo_in[a][...].astype(BF16)
                        wo_own_copy(a).start()
                        wo_copy(a, 0, me, sibling, src=wo_stage[a]).start()
                        for j, chip in enumerate(chips):
                            wo_copy(a, 1 + j, me, (*chip, c), src=wo_stage[a]).start()
                if s_ == 2:
                    for a in range(no):
                        for j, chip in enumerate(chips):
                            wo_copy(a, 1 + j, (*chip, c), me).wait_recv()
                            wo_copy(a, 4 + j, (*chip, c), sibling).start()

        for s_ in range(1, n_chip):
            @pl.when(jnp.logical_and(st == s_, i == 0))
            def _():
                copy(3 + s_, (*chips[s_ - 1], 1 - c), me).wait_recv()
                pair_copy(s_).start()

        xv = x_ref[...]
        h_s[...] = (xv * _rms_scale(xv) * g_ref[...]).astype(BF16)

        @pl.when(st == 0)
        def _():
            h_ref[...] = h_s[...]

        chip_now = jnp.bitwise_xor(chip_id((x, y)), flip_at(st))
        proj_ref[...] = _dot(h_s[...], wv[chip_now]).astype(BF16)

        @pl.when(jnp.logical_and(st == n_chip - 1, i == n_i - 1))
        def _():
            copy(0, me, sibling, src=stage).wait_send()
            for j, chip in enumerate(chips):
                copy(1 + j, me, (*chip, c), src=stage).wait_send()
                copy(4 + j, (*chip, c), sibling).wait_send()
            for s_ in range(n_chip):
                pair_copy(s_).wait()
            for a in range(no):
                wo_copy(a, 0, me, sibling, src=wo_stage[a]).wait_send()
                wo_copy(a, 0, sibling, me).wait_recv()
                for j, chip in enumerate(chips):
                    wo_copy(a, 1 + j, me, (*chip, c), src=wo_stage[a]).wait_send()
                    wo_copy(a, 4 + j, (*chip, c), sibling).wait_send()
                    wo_copy(a, 4 + j, (*chip, 1 - c), me).wait_recv()
                wo_own_copy(a).wait()

    any_spec = pl.BlockSpec(memory_space=pl.ANY)
    vmem = pl.BlockSpec(memory_space=pltpu.VMEM)
    grid_spec = pltpu.PrefetchScalarGridSpec(
        num_scalar_prefetch=1, grid=(n_chip, n_i),
        in_specs=[pl.BlockSpec((tm, d), lambda st, i, me: (i, 0)),
                  pl.BlockSpec((1, d), lambda st, i, me: (0, 0)), vmem] + [vmem] * no,
        out_specs=[pl.BlockSpec((tm, pw), lambda st, i, me: (i, jnp.bitwise_xor(me[0] // 2, flip_at(st)))),
                   pl.BlockSpec((tm, d), lambda st, i, me: (jnp.where(st == 0, i, n_i - 1), 0)),
                   any_spec] + [any_spec] * no,
        scratch_shapes=[pltpu.VMEM((n_chip, d, pw), BF16), pltpu.VMEM((d, esh), BF16), pltpu.VMEM((tm, d), BF16)] + [
            pltpu.VMEM(s.shape, BF16) for s in wo_shards] + [
            pltpu.SemaphoreType.DMA((7,)), pltpu.SemaphoreType.DMA((7,)),
            pltpu.SemaphoreType.DMA((n_chip,)), pltpu.SemaphoreType.DMA((1,)),
            pltpu.SemaphoreType.DMA((7 * no,)), pltpu.SemaphoreType.DMA((7 * no,)),
            pltpu.SemaphoreType.DMA((no,))])
    return pl.pallas_call(
        body, name="gather_in_proj", grid_spec=grid_spec,
        out_shape=[SDS((n, n_chip * pw), BF16), SDS((n, d), BF16), SDS((d, n_chip * pw), BF16)] + [
            SDS((N_DEV,) + s.shape, BF16) for s in wo_shards],
        compiler_params=pltpu.CompilerParams(dimension_semantics=("arbitrary", "arbitrary"),
                                             vmem_limit_bytes=VMEM_LIMIT),
    )(my_slot, x2d, norm_in, w_in_sh, *wo_shards)


N_CHIP = N_DEV // 2
CHIP_FLIPS = (3, 2, 1, 0)


def _owner_at(mine, j):
    flip = 0
    for pair, f in enumerate(CHIP_FLIPS):
        flip = jnp.where(j // 2 == pair, f, flip)
    return 2 * jnp.bitwise_xor(mine // 2, flip) + j % 2


def _dw_in_exchange(h, dproj, my_slot, packed):
    n, d = h.shape
    esh = dproj.shape[1] // N_DEV
    tk = _tile(n, 2048)
    nk = n // tk
    last_j = N_DEV - 1

    def body(me_ref, h_ref, dp_ref, pk_in, win_out, pk_out,
             acc, halfbuf, recvbuf, sendbuf, half_send, half_recv, win_send, win_recv,
             send_sems, recv_sems, local_sems):
        del me_ref
        j, k = pl.program_id(0), pl.program_id(1)
        x, y, c = _me()
        me, sibling = (x, y, c), (x, y, 1 - c)
        mine = _slot(me)
        my_chip = mine // 2

        def pack_copies():
            local = pltpu.make_async_copy(pk_in, pk_out.at[mine], local_sems.at[0])
            remote = [pltpu.make_async_remote_copy(
                src_ref=pk_in, dst_ref=pk_out.at[mine], send_sem=send_sems.at[kk - 1], recv_sem=recv_sems.at[kk - 1],
                device_id=_peer(me, kk), device_id_type=MESH_ID) for kk in range(1, N_DEV)]
            return local, remote

        def half_copy(jj):
            slot = (jj // 2) % 2
            return pltpu.make_async_remote_copy(
                src_ref=halfbuf.at[slot], dst_ref=recvbuf.at[slot],
                send_sem=half_send.at[slot], recv_sem=half_recv.at[slot],
                device_id=sibling, device_id_type=MESH_ID)

        def chip_copy(jj):
            slot = (jj // 2) % 2
            owner = _owner_at(mine, jj)
            return pltpu.make_async_remote_copy(
                src_ref=sendbuf.at[slot], dst_ref=win_out.at[my_chip],
                send_sem=win_send.at[slot], recv_sem=win_recv.at[my_chip],
                device_id=(owner // 4, (owner // 2) % 2, owner % 2), device_id_type=MESH_ID)

        def own_copy():
            return pltpu.make_async_copy(sendbuf.at[(last_j // 2) % 2], win_out.at[my_chip], local_sems.at[1])

        @pl.when(jnp.logical_and(j == 0, k == 0))
        def _():
            local, remote = pack_copies()
            for cp in [local] + remote:
                cp.start()

        @pl.when(k == 0)
        def _():
            acc[...] = jnp.zeros_like(acc)

        acc[...] += _dot_tn(dp_ref[...], h_ref[...])

        done = k == nk - 1
        combine = j % 2 == c
        slot = (j // 2) % 2

        @pl.when(jnp.logical_and(done, jnp.logical_not(combine)))
        def _():
            @pl.when(j >= 4)
            def _():
                half_copy(j - 4).wait_send()

            halfbuf[slot] = acc[...].astype(BF16)
            half_copy(j).start()

        @pl.when(jnp.logical_and(done, combine))
        def _():
            half_copy(j).wait_recv()

            @pl.when(j >= 4)
            def _():
                chip_copy(j - 4).wait_send()

            sendbuf[slot] = (acc[...] + recvbuf[slot].astype(F32)).astype(BF16)

            @pl.when(j < last_j - 1)
            def _():
                chip_copy(j).start()

            @pl.when(j >= last_j - 1)
            def _():
                own_copy().start()

        @pl.when(jnp.logical_and(j == last_j, done))
        def _():
            half_copy(5 - c).wait_send()
            half_copy(7 - c).wait_send()
            chip_copy(4 + c).wait_send()
            own_copy().wait()
            for chip in range(N_CHIP):
                @pl.when(chip != my_chip)
                def _():
                    landed = win_out.at[chip]
                    pltpu.make_async_remote_copy(
                        src_ref=landed, dst_ref=landed, send_sem=win_send.at[0], recv_sem=win_recv.at[chip],
                        device_id=me, device_id_type=MESH_ID).wait_recv()
            local, remote = pack_copies()
            for cp in remote:
                cp.wait_send()
            for kk in range(1, N_DEV):
                landed = pk_out.at[_slot(_peer(me, kk))]
                pltpu.make_async_remote_copy(
                    src_ref=landed, dst_ref=landed, send_sem=send_sems.at[kk - 1], recv_sem=recv_sems.at[kk - 1],
                    device_id=me, device_id_type=MESH_ID).wait_recv()
            local.wait()

    any_spec = pl.BlockSpec(memory_space=pl.ANY)
    grid_spec = pltpu.PrefetchScalarGridSpec(
        num_scalar_prefetch=1, grid=(N_DEV, nk),
        in_specs=[pl.BlockSpec((tk, d), lambda j, k, me: (k, 0)),
                  pl.BlockSpec((tk, esh), lambda j, k, me: (k, _owner_at(me[0], j))), any_spec],
        out_specs=[any_spec] * 2,
        scratch_shapes=[pltpu.VMEM((esh, d), F32)] + [pltpu.VMEM((2, esh, d), BF16)] * 3 + [
            pltpu.SemaphoreType.DMA((2,)), pltpu.SemaphoreType.DMA((2,)),
            pltpu.SemaphoreType.DMA((2,)), pltpu.SemaphoreType.DMA((N_CHIP,)),
            pltpu.SemaphoreType.DMA((N_DEV - 1,)), pltpu.SemaphoreType.DMA((N_DEV - 1,)),
            pltpu.SemaphoreType.DMA((2,))])
    return pl.pallas_call(
        body, name="dw_in_exchange", grid_spec=grid_spec,
        out_shape=[SDS((N_CHIP, esh, d), BF16), SDS((N_DEV,) + packed.shape, packed.dtype)],
        compiler_params=_params(("arbitrary", "arbitrary")),
    )(my_slot, h, dproj, packed)


def _finish_small(packs, late_packs, states, groups, chunk):
    gc = groups * chunk
    nw = len(states)

    def body(p_ref, l_ref, *refs):
        st = refs[:3 * nw]
        loss_ref = refs[3 * nw]
        outs = refs[3 * nw + 1:]
        row, col = _iotas(chunk)
        tril = col <= row

        def total(ref, rs):
            tot = ref[0, rs, :]
            for dev in range(1, N_DEV):
                tot = tot + ref[dev, rs, :]
            return tot

        def update(k, rs_w, g):
            w_ref, m_ref, v_ref = st[3 * k:3 * k + 3]
            _adamw_outputs(*[o.at[rs_w] for o in outs[4 * k:4 * k + 4]], g, w_ref[rs_w, :], m_ref[rs_w, :], v_ref[rs_w, :])

        for g in range(groups):
            rs = slice(g * chunk, (g + 1) * chunk)
            update(0, rs, jnp.where(tril, total(p_ref, rs), 0.0))
        slab = lambda k: slice(gc + k * SUBLANE, gc + (k + 1) * SUBLANE)
        for k in range(3):
            update(1 + k, slice(0, SUBLANE), total(p_ref, slab(k)))
        loss_ref[...] = jnp.full((SUBLANE, LANE), jnp.sum(total(p_ref, slab(3))), F32)
        update(4, slice(0, SUBLANE), total(l_ref, slice(0, SUBLANE)))

    flat = [a for s in states for a in s]
    vmem = pl.BlockSpec(memory_space=pltpu.VMEM)
    res = pl.pallas_call(
        body, name="finish_small",
        out_shape=[SDS((SUBLANE, LANE), F32)] + [SDS(s[0].shape, F32) for s in states for _ in range(4)],
        in_specs=[vmem] * (2 + len(flat)),
        out_specs=[vmem] * (1 + 4 * nw),
        compiler_params=pltpu.CompilerParams(vmem_limit_bytes=VMEM_LIMIT),
    )(packs, late_packs, *flat)
    return res[0], [res[1 + 4 * k:5 + 4 * k] for k in range(nw)]


def _branch_a_fwd(proj, norm_v, w_s, b_col):
    n = proj.shape[0]
    d = norm_v.shape[1]
    groups, chunk, _ = w_s.shape
    tr = _tile(n, 8 * chunk)

    def body(u_ref, v_ref, z_ref, gv_ref, ws_ref, b_ref, ya_ref, vn_s, pre_s):
        row, col = _iotas(chunk)
        tril = col <= row
        vg = _gelu(v_ref[...])[0].astype(F32)
        vn_s[...] = (vg * _rms_scale(vg) * gv_ref[...]).astype(BF16)
        pre_s[...] = _gelu(u_ref[...])[0] * _silu(z_ref[...])[0]
        for g in range(groups):
            wm = jnp.where(tril, ws_ref[g], 0.0).astype(BF16)
            cs = slice(g * chunk, (g + 1) * chunk)
            for c in range(tr // chunk):
                rs = slice(c * chunk, (c + 1) * chunk)
                mixed = _dot(wm, vn_s[rs, cs]) + b_ref[g]
                ya_ref[rs, cs] = (pre_s[rs, cs].astype(F32) * mixed).astype(BF16)

    seg = lambda k: pl.BlockSpec((tr, d), lambda i: (i, k))
    return pl.pallas_call(
        body, name="branch_a_fwd", grid=(n // tr,),
        in_specs=[seg(0), seg(1), seg(2),
                  pl.BlockSpec((1, d), lambda i: (0, 0)),
                  pl.BlockSpec((groups, chunk, chunk), lambda i: (0, 0, 0)),
                  pl.BlockSpec((groups, chunk, 1), lambda i: (0, 0, 0))],
        out_specs=pl.BlockSpec((tr, d), lambda i: (i, 0)),
        out_shape=SDS((n, d), BF16),
        scratch_shapes=[pltpu.VMEM((tr, d), BF16), pltpu.VMEM((tr, d), BF16)],
        compiler_params=_params(("parallel",)),
    )(proj, proj, proj, norm_v, w_s, b_col)


def _sb_fwd(proj, batch, seq, d, hd):
    heads = d // hd
    t = _tile(seq, SB_TILE)
    sw = _tile(t, SB_SCAN)
    nb = t // sw
    scale = hd ** -0.5
    nblk = seq // t
    nh = SB_HEADS
    wide = nh * hd
    cols = [slice(hh * hd, (hh + 1) * hd) for hh in range(nh)]

    def body(qs, k_ref, vs, zb_ref, yb_ref, o_ref, tot_ref, kts, later, acc):
        for jb in range(nblk):
            kts[jb] = k_ref[jb * t:(jb + 1) * t, :].T
        row, col = _iotas(t)
        later[...] = (row[:sw, :sw] > col[:sw, :sw]).astype(BF16)

        def qblock(i, carry):
            r0 = pl.multiple_of(i * t, t)

            def tile(j, runs):
                c0 = pl.multiple_of(j * t, t)
                logs = [_sb_logs(_dot(qs[pl.ds(r0, t), cs], kts[j, cs, :]), scale, None) for cs in cols]
                scans = [_dot(jnp.concatenate([logs[hh][1][:, b * sw:(b + 1) * sw] for b in range(nb)], axis=0),
                              later[...]) for hh in range(nh)]
                new_runs = []
                for hh in range(nh):
                    after = runs[hh]
                    blocks = [None] * nb
                    for b in reversed(range(nb)):
                        ks_ = slice(b * sw, (b + 1) * sw)
                        inside = scans[hh][b * t:(b + 1) * t]
                        blocks[b] = jnp.exp(logs[hh][0][:, ks_].astype(F32) + inside + after).astype(BF16)
                        after = after + inside[:, 0:1] + logs[hh][1][:, b * sw:b * sw + 1].astype(F32)
                    new_runs.append(after)
                    acc[:, cols[hh]] += _dot(jnp.concatenate(blocks, axis=1), vs[pl.ds(c0, t), cols[hh]])
                return tuple(new_runs)

            def diagonal_tile():
                starts = [b * sw for b in range(nb)]
                logs = [[_sb_logs(_dot(qs[pl.ds(r0 + s, t - s), cs], kts[i, cs, s:s + sw]), scale,
                                  col[:t - s, :sw] < row[:t - s, :sw]) for s in starts] for cs in cols]
                scans = [_dot(jnp.concatenate([lr for _, lr in logs[hh]], axis=0), later[...]) for hh in range(nh)]
                new_runs = []
                offs = [sum(t - s for s in starts[:b]) for b in range(nb)]
                for hh in range(nh):
                    after = jnp.zeros((t, 1), F32)
                    ws = [None] * nb
                    for b in reversed(range(nb)):
                        s = starts[b]
                        lb, lr = logs[hh][b]
                        inside = scans[hh][offs[b]:offs[b] + t - s]
                        ws[b] = jnp.exp(lb.astype(F32) + inside + after[s:]).astype(BF16)
                        total = inside[:, 0:1] + lr[:, 0:1].astype(F32)
                        after = after + total if s == 0 else jnp.concatenate([after[:s], after[s:] + total], axis=0)
                    new_runs.append(after)
                    acc[:, cols[hh]] = _dot(ws[0], vs[pl.ds(r0, sw), cols[hh]])
                    for b in range(1, nb):
                        acc[starts[b]:, cols[hh]] += _dot(ws[b], vs[pl.ds(r0 + starts[b], sw), cols[hh]])
                return tuple(new_runs)

            runs = diagonal_tile()
            runs = lax.fori_loop(0, i, lambda jj, rs: tile(i - 1 - jj, rs), runs)
            for hh in range(nh):
                out = acc[:, cols[hh]]
                o_ref[pl.ds(r0, t), cols[hh]] = out.astype(BF16)
                tot_ref[hh, pl.ds(r0, t), :] = runs[hh]
                sz, _ = _silu(zb_ref[pl.ds(r0, t), cols[hh]].astype(F32))
                yb_ref[pl.ds(r0, t), cols[hh]] = (out * sz).astype(BF16)
            return carry

        lax.fori_loop(0, nblk, qblock, 0)

    col0 = d // wide
    seg = lambda k: pl.BlockSpec((seq, wide), lambda b, h: (b, k * col0 + h))
    return pl.pallas_call(
        body, name="sb_fwd", grid=(batch, heads // nh),
        in_specs=[seg(3), seg(4), seg(5), seg(6)],
        out_specs=[pl.BlockSpec((seq, wide), lambda b, h: (b, h))] * 2 + [
            pl.BlockSpec((nh, seq, 1), lambda b, h: (b * (heads // nh) + h, 0, 0))],
        out_shape=[SDS((batch * seq, d), BF16), SDS((batch * seq, d), BF16), SDS((batch * heads, seq, 1), F32)],
        scratch_shapes=[pltpu.VMEM((nblk, wide, t), BF16), pltpu.VMEM((sw, sw), BF16), pltpu.VMEM((t, wide), F32)],
        compiler_params=_params(("parallel", "parallel")),
    )(proj, proj, proj, proj)


def _tail(x2d, tgt, ya, yb, proj, w_oa, w_ob, w_out, norm_final):
    n, d = x2d.shape
    e = proj.shape[1]
    tm = _tile(n, 512)
    steps = n // tm

    def body(x_ref, t_ref, ya_ref, yb_ref, ga_ref, gb_ref, woa_ref, wob_ref, wout_ref, gf_ref,
             dproj_ref, dx2_ref, dya_ref, dyb_ref, mrg_ref, dpa_ref, dpb_ref, loss_ref, dgf_ref, dg_s, dg_sems):
        i = pl.program_id(0)

        def gate_copy(step):
            rows_ = pl.ds(pl.multiple_of(step * tm, tm), tm)
            return pltpu.make_async_copy(dg_s.at[step % 2], dproj_ref.at[rows_, pl.ds(7 * d, 2 * d)],
                                         dg_sems.at[step % 2])

        @pl.when(i == 0)
        def _():
            loss_ref[...] = jnp.zeros_like(loss_ref)
            dgf_ref[...] = jnp.zeros_like(dgf_ref)

        @pl.when(i >= 2)
        def _():
            gate_copy(i - 2).wait()

        halves = [slice(hf * (tm // 2), (hf + 1) * (tm // 2)) for hf in range(2)] if tm >= 512 else [slice(0, tm)]
        gf = gf_ref[...]
        pa = [_dot(ya_ref[rs, :], woa_ref[...]) for rs in halves]
        pb = [_dot(yb_ref[rs, :], wob_ref[...]) for rs in halves]
        sa = [_sigmoid(ga_ref[rs, :].astype(F32)) for rs in halves]
        sb = [_sigmoid(gb_ref[rs, :].astype(F32)) for rs in halves]
        merged = [(sa[k] * pa[k] + sb[k] * pb[k]).astype(BF16) for k in range(len(halves))]
        for k, rs in enumerate(halves):
            mrg_ref[rs, :] = merged[k]
        x2 = [x_ref[rs, :] + _dot(merged[k], wout_ref[...]) for k, rs in enumerate(halves)]
        dx2 = []
        for k, rs in enumerate(halves):
            r2 = _rms_scale(x2[k])
            xh = x2[k] * r2
            diff = xh * gf - t_ref[rs, :]
            loss_ref[...] += jnp.sum(diff * diff, axis=0, keepdims=True) * (0.5 / d)
            dy = diff * (1.0 / d)
            dgf_ref[...] += jnp.sum(dy * xh, axis=0, keepdims=True)
            dxh = dy * gf
            dx2.append(r2 * (dxh - xh * jnp.mean(dxh * xh, axis=-1, keepdims=True)))
            dx2_ref[rs, :] = dx2[k]
        dm = [_dot_nt(dx2[k].astype(BF16), wout_ref[...]) for k in range(len(halves))]
        dpa, dpb = [], []
        for k, rs in enumerate(halves):
            dpa.append((dm[k] * sa[k]).astype(BF16))
            dpb.append((dm[k] * sb[k]).astype(BF16))
            dpa_ref[rs, :] = dpa[k]
            dpb_ref[rs, :] = dpb[k]
            dg_s[i % 2, rs, 0:d] = (dm[k] * pa[k] * (sa[k] * (1.0 - sa[k]))).astype(BF16)
            dg_s[i % 2, rs, d:2 * d] = (dm[k] * pb[k] * (sb[k] * (1.0 - sb[k]))).astype(BF16)
        gate_copy(i).start()
        for k, rs in enumerate(halves):
            dya_ref[rs, :] = _dot_nt(dpa[k], woa_ref[...]).astype(BF16)
        for k, rs in enumerate(halves):
            dyb_ref[rs, :] = _dot_nt(dpb[k], wob_ref[...]).astype(BF16)

        @pl.when(i == steps - 1)
        def _():
            if steps >= 2:
                gate_copy(i - 1).wait()
            gate_copy(i).wait()

    rows = lambda k=0: pl.BlockSpec((tm, d), lambda i: (i, k))
    full = pl.BlockSpec((d, d), lambda i: (0, 0), pipeline_mode=pl.Buffered(1))
    vec = pl.BlockSpec((1, d), lambda i: (0, 0))
    return pl.pallas_call(
        body, name="tail", grid=(steps,),
        in_specs=[rows(), rows(), rows(), rows(), rows(7), rows(8), full, full, full, vec],
        out_specs=[pl.BlockSpec(memory_space=pl.ANY),
                   rows(), rows(), rows(), rows(), rows(), rows(), vec, vec],
        out_shape=[SDS((n, e), BF16), SDS((n, d), F32), SDS((n, d), BF16), SDS((n, d), BF16),
                   SDS((n, d), BF16), SDS((n, d), BF16), SDS((n, d), BF16),
                   SDS((1, d), F32), SDS((1, d), F32)],
        scratch_shapes=[pltpu.VMEM((2, tm, 2 * d), BF16), pltpu.SemaphoreType.DMA((2,))],
        compiler_params=_params(("arbitrary",)),
    )(x2d, tgt, ya, yb, proj, proj, w_oa, w_ob, w_out, norm_final)


def _dw_o(pairs):
    n, d = pairs[0][0].shape
    tk = _tile(n, 1024)
    nk = n // tk
    npair = len(pairs)

    def body(*refs):
        a_refs, b_refs = refs[:npair], refs[npair:2 * npair]
        o_ref, acc = refs[2 * npair], refs[2 * npair + 1]
        p, k = pl.program_id(0), pl.program_id(1)

        @pl.when(k == 0)
        def _():
            acc[...] = jnp.zeros_like(acc)

        for q in range(npair):
            @pl.when(p == q)
            def _():
                acc[...] += _dot_tn(a_refs[q][...], b_refs[q][...].astype(BF16))

        @pl.when(k == nk - 1)
        def _():
            o_ref[0] = acc[...].astype(BF16)

    def tiles(q):
        return pl.BlockSpec((tk, d), lambda p, k: (jnp.where(p == q, k, jnp.where(p < q, 0, nk - 1)), 0))

    return pl.pallas_call(
        body, name="dw_o", grid=(npair, nk),
        in_specs=[tiles(q) for q in range(npair)] * 2,
        out_specs=pl.BlockSpec((1, d, d), lambda p, k: (p, 0, 0)),
        out_shape=SDS((npair, d, d), BF16),
        scratch_shapes=[pltpu.VMEM((d, d), F32)],
        compiler_params=_params(("arbitrary", "arbitrary")),
    )(*[a for a, _ in pairs], *[b for _, b in pairs])


def _sb_bwd(proj, o, dyb, tot, dproj, dw_stack, packed, batch, seq, d, hd):
    heads = d // hd
    t = _tile(seq, SB_TILE_BWD)
    sw = _tile(t, SB_SCAN)
    nb = t // sw
    scale = hd ** -0.5
    nblk = seq // t
    nh = SB_HEADS
    wide = nh * hd
    hs = range(nh)
    cols = [slice(hh * hd, (hh + 1) * hd) for hh in hs]
    blocks = [slice(b * sw, (b + 1) * sw) for b in range(nb)]
    last = slice(sw - 1, sw)

    def compute(qs, ks, v_ref, zb_ref, o_ref, dyb_ref, tot_ref, kts, vts, dos, dzb, dq_all, dkv_t, qt_s, dot_s,
                upto, before, dq):
        for jb in range(nblk):
            rows = slice(jb * t, (jb + 1) * t)
            kts[jb] = ks[rows, :].T
            vts[jb] = v_ref[rows, :].T
        sz, dsz = _silu(zb_ref[...])
        dyb_v = dyb_ref[...]
        dos[...] = dyb_v * sz
        dzb[...] = dyb_v * o_ref[...] * dsz
        row, col = _iotas(t)
        upto[...] = (row[:sw, :sw] <= col[:sw, :sw]).astype(BF16)
        before[...] = (row[:sw, :sw] < col[:sw, :sw]).astype(BF16)

        def qblock(i, carry):
            r0 = pl.multiple_of(i * t, t)

            def tile(j, sums):
                c0 = pl.multiple_of(j * t, t)
                q_i = [qs[pl.ds(r0, t), cs] for cs in cols]
                do_i = [dos[pl.ds(r0, t), cs] for cs in cols]
                logs = [_sb_logs(_dot(q_i[hh], kts[j, cols[hh], :]), scale, None) for hh in hs]
                scans = [_dot(jnp.concatenate([logs[hh][1][:, ks_] for ks_ in blocks], axis=0), upto[...]) for hh in hs]
                dw = [_dot(do_i[hh], vts[j, cols[hh], :]) for hh in hs]
                ws, gs, new_runs = [], [], []
                for hh in hs:
                    left = tot_ref[hh, pl.ds(r0, t), :] - sums[hh][0]
                    w_b, g_b = [], []
                    for b, ks_ in enumerate(blocks):
                        inside = scans[hh][b * t:(b + 1) * t]
                        w = jnp.exp(logs[hh][0][:, ks_].astype(F32) + (left - inside))
                        w_b.append(w.astype(BF16))
                        g_b.append((dw[hh][:, ks_] * w).astype(BF16))
                        left = left - inside[:, last]
                    ws.append(jnp.concatenate(w_b, axis=1))
                    gs.append(g_b)
                    new_runs.append(tot_ref[hh, pl.ds(r0, t), :] - left)
                gscans = [_dot(jnp.concatenate(gs[hh], axis=0), before[...]) for hh in hs]
                dzs, new_gruns = [], []
                for hh in hs:
                    g_before = sums[hh][1]
                    dz_b = []
                    for b, ks_ in enumerate(blocks):
                        inside = gscans[hh][b * t:(b + 1) * t]
                        beta = jnp.exp(logs[hh][0][:, ks_]).astype(F32)
                        g = gs[hh][b].astype(F32)
                        dz_b.append((g - (g + inside + g_before) * beta).astype(BF16))
                        g_before = g_before + inside[:, last] + g[:, last]
                    dzs.append(jnp.concatenate(dz_b, axis=1))
                    new_gruns.append(g_before)
                for hh in hs:
                    dkv_t[1, j, cols[hh], :] += _dot(dot_s[cols[hh], :], ws[hh])
                for hh in hs:
                    dkv_t[0, j, cols[hh], :] += _dot(qt_s[cols[hh], :], dzs[hh])
                for hh in hs:
                    dq[:, cols[hh]] += _dot(dzs[hh], ks[pl.ds(c0, t), cols[hh]])
                return tuple((new_runs[hh], new_gruns[hh]) for hh in hs)

            def diagonal_tile(sums):
                starts = [b * sw for b in range(nb)]
                offs = [sum(t - s for s in starts[:b]) for b in range(nb)]
                q_b = [[qs[pl.ds(r0 + s, t - s), cs] for s in starts] for cs in cols]
                do_b = [[dos[pl.ds(r0 + s, t - s), cs] for s in starts] for cs in cols]
                logs = [[_sb_logs(_dot(q_b[hh][b], kts[i, cols[hh], s:s + sw]), scale,
                                  col[:t - s, :sw] < row[:t - s, :sw]) for b, s in enumerate(starts)] for hh in hs]
                dw = [[_dot(do_b[hh][b], vts[i, cols[hh], s:s + sw]) for b, s in enumerate(starts)] for hh in hs]
                scans = [_dot(jnp.concatenate([lr for _, lr in logs[hh]], axis=0), upto[...]) for hh in hs]
                ws, gs = [], []
                for hh in hs:
                    left = tot_ref[hh, pl.ds(r0, t), :] - sums[hh][0]
                    w_b, g_b = [], []
                    for b, s in enumerate(starts):
                        inside = scans[hh][offs[b]:offs[b] + t - s]
                        w = jnp.exp(logs[hh][b][0].astype(F32) + (left[s:] - inside))
                        w_b.append(w.astype(BF16))
                        g_b.append((dw[hh][b] * w).astype(BF16))
                        total = inside[:, last]
                        left = left - total if s == 0 else jnp.concatenate([left[:s], left[s:] - total], axis=0)
                    ws.append(w_b)
                    gs.append(g_b)
                gscans = [_dot(jnp.concatenate(gs[hh], axis=0), before[...]) for hh in hs]
                dzs = []
                for hh in hs:
                    g_before = sums[hh][1]
                    dz_b = []
                    for b, s in enumerate(starts):
                        inside = gscans[hh][offs[b]:offs[b] + t - s]
                        beta = jnp.exp(logs[hh][b][0]).astype(F32)
                        g = gs[hh][b].astype(F32)
                        dz_b.append((g - (g + inside + g_before[s:]) * beta).astype(BF16))
                        total = inside[:, last] + g[:, last]
                        g_before = g_before + total if s == 0 else jnp.concatenate(
                            [g_before[:s], g_before[s:] + total], axis=0)
                    dzs.append(dz_b)
                for hh in hs:
                    for b, s in enumerate(starts):
                        dkv_t[1, i, cols[hh], s:s + sw] = _dot(dot_s[cols[hh], s:], ws[hh][b])
                for hh in hs:
                    for b, s in enumerate(starts):
                        dkv_t[0, i, cols[hh], s:s + sw] = _dot(qt_s[cols[hh], s:], dzs[hh][b])
                for hh in hs:
                    for b, s in enumerate(starts):
                        dq[s:, cols[hh]] += _dot(dzs[hh][b], ks[pl.ds(r0 + s, sw), cols[hh]])

            qt_s[...] = qs[pl.ds(r0, t), :].T
            dot_s[...] = dos[pl.ds(r0, t), :].T
            zero = jnp.zeros((t, 1), F32)
            dq[...] = jnp.zeros_like(dq)
            sums = lax.fori_loop(0, i, tile, ((zero, zero),) * nh)
            diagonal_tile(sums)
            dq_all[pl.ds(r0, t), :] = dq[...]
            return carry

        lax.fori_loop(0, nblk, qblock, 0)

    pairs = heads // nh

    nst = dw_stack.shape[0]
    ns = nst + 1

    def body(qs, ks, v_ref, zb_ref, o_ref, dyb_ref, tot_ref, dproj_in, dw_ref, pk_ref, out_ref, *refs):
        del dproj_in
        st_in = [dw_ref.at[k] for k in range(nst)] + [pk_ref]
        st_out = refs[:ns]
        (kts, vts, dos, dzb, dq_all, dkv_t, qt_s, dot_s, upto, before, dq, stage, stage_sems,
         send_sems, recv_sems, local_sems) = refs[ns:]
        step = pl.program_id(0) * pairs + pl.program_id(1)
        exchange = functools.partial(_stack_exchange, _me(), st_in, st_out, 1, send_sems, recv_sems, local_sems)

        @pl.when(step == 0)
        def _():
            local, remote, _ = exchange(arrivals=False)
            for cp in local + remote:
                cp.start()

        def out_copies(s):
            rows_ = pl.ds(pl.multiple_of((s // pairs) * seq, seq), seq)
            return [pltpu.make_async_copy(
                stage.at[k], out_ref.at[rows_, pl.ds(pl.multiple_of((3 + k) * d + (s % pairs) * wide, wide), wide)],
                stage_sems.at[k]) for k in range(4)]

        compute(qs, ks, v_ref, zb_ref, o_ref, dyb_ref, tot_ref, kts, vts, dos, dzb, dq_all, dkv_t, qt_s, dot_s,
                upto, before, dq)

        @pl.when(step > 0)
        def _():
            for cp in out_copies(step - 1):
                cp.wait()

        stage[0] = (dq_all[...] * scale).astype(BF16)
        for jb in range(nblk):
            stage[1, jb * t:(jb + 1) * t, :] = (dkv_t[0, jb] * scale).astype(BF16).T
            stage[2, jb * t:(jb + 1) * t, :] = dkv_t[1, jb].astype(BF16).T
        stage[3] = dzb[...]
        for cp in out_copies(step):
            cp.start()

        @pl.when(step == batch * pairs - 1)
        def _():
            for cp in out_copies(step):
                cp.wait()
            local, remote, landed = exchange()
            for cp in remote:
                cp.wait_send()
            for cp in landed:
                cp.wait_recv()
            for cp in local:
                cp.wait()

    col0 = d // wide
    seg = lambda k: pl.BlockSpec((seq, wide), lambda b, h: (b, k * col0 + h))
    head = pl.BlockSpec((seq, wide), lambda b, h: (b, h))
    any_spec = pl.BlockSpec(memory_space=pl.ANY)
    return pl.pallas_call(
        body, name="sb_bwd", grid=(batch, pairs),
        in_specs=[seg(3), seg(4), seg(5), seg(6), head, head,
                  pl.BlockSpec((nh, seq, 1), lambda b, h: (b * pairs + h, 0, 0))] + [any_spec] * 3,
        out_specs=[any_spec] * (ns + 1),
        out_shape=[SDS(dproj.shape, dproj.dtype)] + [SDS(dw_stack.shape[1:], dw_stack.dtype)] * nst + [
            SDS((N_DEV,) + packed.shape, packed.dtype)],
        input_output_aliases={7: 0},
        scratch_shapes=[pltpu.VMEM((nblk, wide, t), BF16)] * 2 + [
            pltpu.VMEM((seq, wide), BF16), pltpu.VMEM((seq, wide), BF16),
            pltpu.VMEM((seq, wide), F32), pltpu.VMEM((2, nblk, wide, t), F32),
            pltpu.VMEM((wide, t), BF16), pltpu.VMEM((wide, t), BF16),
            pltpu.VMEM((sw, sw), BF16), pltpu.VMEM((sw, sw), BF16), pltpu.VMEM((t, wide), F32),
            pltpu.VMEM((4, seq, wide), BF16), pltpu.SemaphoreType.DMA((4,)),
            pltpu.SemaphoreType.DMA((7 * ns,)), pltpu.SemaphoreType.DMA((7 * ns,)),
            pltpu.SemaphoreType.DMA((ns,))],
        compiler_params=_params(("arbitrary", "arbitrary")),
    )(proj, proj, proj, proj, o, dyb, tot, dproj, dw_stack, packed)


def _branch_a_bwd(proj, dya, norm_v, w_s, b_col, dproj):
    n = proj.shape[0]
    d = norm_v.shape[1]
    groups, chunk, _ = w_s.shape
    tr = _tile(n, 4 * chunk)

    def body(u_ref, v_ref, z_ref, dya_ref, gv_ref, ws_ref, b_ref, dproj_in,
             out_ref, dws_ref, dbias_ref, dgv_ref, vn_s, dmix_s, dvn_s, db_ref):
        del dproj_in

        @pl.when(pl.program_id(0) == 0)
        def _():
            dws_ref[...] = jnp.zeros_like(dws_ref)
            db_ref[...] = jnp.zeros_like(db_ref)
            dgv_ref[...] = jnp.zeros_like(dgv_ref)

        row, col = _iotas(chunk)
        tril = col <= row
        gv = gv_ref[...]
        vg16, dvg_dv = _gelu(v_ref[...])
        vg = vg16.astype(F32)
        r = _rms_scale(vg)
        vh = vg * r
        vn_s[...] = (vh * gv).astype(BF16)
        ug, dug_du = _gelu(u_ref[...])
        sz, dsz = _silu(z_ref[...])
        dya_v = dya_ref[...]
        dmix_s[...] = dya_v * ug * sz
        du_scale = sz * dug_du
        dz_scale = ug * dsz
        for g in range(groups):
            wm = jnp.where(tril, ws_ref[g], 0.0).astype(BF16)
            cs = slice(g * chunk, (g + 1) * chunk)
            for c in range(tr // chunk):
                rs = slice(c * chunk, (c + 1) * chunk)
                vn = vn_s[rs, cs]
                mixed = _dot(wm, vn) + b_ref[g]
                dmix16 = dmix_s[rs, cs]
                dws_ref[g] += _dot_nt(dmix16, vn)
                db_ref[g] += dmix16.astype(F32)
                dvn_s[rs, cs] = _dot_tn(wm, dmix16)
                t_u = dya_v[rs, cs] * mixed.astype(BF16)
                out_ref[rs, g * chunk:(g + 1) * chunk] = t_u * du_scale[rs, cs]
                out_ref[rs, 2 * d + g * chunk:2 * d + (g + 1) * chunk] = t_u * dz_scale[rs, cs]
        dvn = dvn_s[...]
        dgv_ref[...] += jnp.sum(dvn * vh, axis=0, keepdims=True)
        dvh = dvn * gv
        dvg = r * (dvh - vh * jnp.mean(dvh * vh, axis=-1, keepdims=True))
        out_ref[:, d:2 * d] = (dvg * dvg_dv.astype(F32)).astype(BF16)

        @pl.when(pl.program_id(0) == n // tr - 1)
        def _():
            for g in range(groups):
                dbias_ref[g:g + 1, :] = jnp.sum(db_ref[g].T, axis=0, keepdims=True)

    seg = lambda k: pl.BlockSpec((tr, d), lambda i: (i, k))
    return pl.pallas_call(
        body, name="branch_a_bwd", grid=(n // tr,),
        in_specs=[seg(0), seg(1), seg(2), seg(0),
                  pl.BlockSpec((1, d), lambda i: (0, 0)),
                  pl.BlockSpec((groups, chunk, chunk), lambda i: (0, 0, 0)),
                  pl.BlockSpec((groups, chunk, 1), lambda i: (0, 0, 0)),
                  pl.BlockSpec(memory_space=pl.ANY)],
        out_specs=[pl.BlockSpec((tr, 3 * d), lambda i: (i, 0)),
                   pl.BlockSpec((groups, chunk, chunk), lambda i: (0, 0, 0)),
                   pl.BlockSpec((groups, chunk), lambda i: (0, 0)),
                   pl.BlockSpec((1, d), lambda i: (0, 0))],
        out_shape=[SDS(dproj.shape, dproj.dtype), SDS((groups, chunk, chunk), F32),
                   SDS((groups, chunk), F32), SDS((1, d), F32)],
        input_output_aliases={7: 0},
        scratch_shapes=[pltpu.VMEM((tr, d), BF16), pltpu.VMEM((tr, d), BF16), pltpu.VMEM((tr, d), F32),
                        pltpu.VMEM((groups, chunk, chunk), F32)],
        compiler_params=_params(("arbitrary",)),
    )(proj, proj, proj, dya, norm_v, w_s, b_col, dproj)


def _dx(dproj, wg_in, x2d, dx2, norm_in):
    n, d = x2d.shape
    e = wg_in.shape[1]
    tm = _tile(n, 512)

    def body(dp_ref, w_ref, x_ref, dx2_ref, g_ref, gx_ref, dg_ref):
        @pl.when(pl.program_id(0) == 0)
        def _():
            dg_ref[...] = jnp.zeros_like(dg_ref)

        dh = _dot_nt(dp_ref[...], w_ref[...])
        x = x_ref[...]
        r = _rms_scale(x)
        xh = x * r
        dg_ref[...] += jnp.sum(dh * xh, axis=0, keepdims=True)
        dxh = dh * g_ref[...]
        gx_ref[...] = dx2_ref[...] + r * (dxh - xh * jnp.mean(dxh * xh, axis=-1, keepdims=True))

    rows = pl.BlockSpec((tm, d), lambda i: (i, 0))
    vec = pl.BlockSpec((1, d), lambda i: (0, 0))
    return pl.pallas_call(
        body, name="dx", grid=(n // tm,),
        in_specs=[pl.BlockSpec((tm, e), lambda i: (i, 0)),
                  pl.BlockSpec((d, e), lambda i: (0, 0), pipeline_mode=pl.Buffered(1)), rows, rows, vec],
        out_specs=[rows, vec],
        out_shape=[SDS((n, d), F32), SDS((1, d), F32)],
        compiler_params=_params(("arbitrary",), DX_VMEM_LIMIT),
    )(dproj, wg_in, x2d, dx2, norm_in)


def _adamw_outputs(g_ref, d_ref, m_ref, v_ref, g, w, m, v):
    delta, m2, v2 = _adamw(w, g, m, v)
    g_ref[...] = g
    d_ref[...] = delta
    m_ref[...] = m2
    v_ref[...] = v2


def _reduce_adamw(slots, w, m, v, name, transposed=False):
    r, c = w.shape
    ns = slots.shape[0]
    tr = _tile(r, 128)

    def body(s_ref, w_ref, m_ref, v_ref, g_out, d_out, m_out, v_out):
        g = s_ref[0].astype(F32)
        for k in range(1, ns):
            g = g + s_ref[k].astype(F32)
        if transposed:
            g = g.T
        _adamw_outputs(g_out, d_out, m_out, v_out, g, w_ref[...], m_ref[...], v_ref[...])

    blk = pl.BlockSpec((tr, c), lambda i: (i, 0))
    slot_blk = (pl.BlockSpec((ns, c, tr), lambda i: (0, 0, i)) if transposed
                else pl.BlockSpec((ns, tr, c), lambda i: (0, i, 0)))
    return pl.pallas_call(
        body, name=name, grid=(r // tr,),
        in_specs=[slot_blk, blk, blk, blk],
        out_specs=[blk] * 4,
        out_shape=[SDS((r, c), F32)] * 4,
        compiler_params=_params(("parallel",)),
    )(slots, w, m, v)


def kernel(x, norm_in, w_in, norm_v, w_s, b_s, w_o_gmlp, w_o_sb, w_out, norm_final, loss_target, m_norm_in, m_w_in, m_norm_v, m_w_s, m_b_s, m_w_o_gmlp, m_w_o_sb, m_w_out, m_norm_final, v_norm_in, v_w_in, v_norm_v, v_w_s, v_b_s, v_w_o_gmlp, v_w_o_sb, v_w_out, v_norm_final):
    batch, seq, d = x.shape
    n = batch * seq
    groups, chunk = w_s.shape[1], w_s.shape[2]
    hd = LANE
    x2d = x.reshape(n, d)
    tgt = loss_target.reshape(n, d)
    b_col = b_s[0].reshape(groups, chunk, 1)
    norm_final2 = norm_final.reshape(1, d)

    my_slot = _slot(_me()).astype(jnp.int32).reshape(1)
    proj, h, wg_in, wg_oa, wg_ob, wg_out = _gather_in_proj(
        x2d, norm_in, w_in[0], [w_o_gmlp[0], w_o_sb[0], w_out[0]], my_slot)
    rsh = wg_oa.shape[1]
    wf_oa, wf_ob, wf_out = (w.reshape(N_DEV * rsh, d) for w in (wg_oa, wg_ob, wg_out))
    ya = _branch_a_fwd(proj, norm_v, w_s[0], b_col)
    yb, o, sb_tot = _sb_fwd(proj, batch, seq, d, hd)
    dproj, dx2, dya, dyb, merged, dpa, dpb, loss_vec, dgf = _tail(
        x2d, tgt, ya, yb, proj, wf_oa, wf_ob, wf_out, norm_final2)
    gp_wo = _dw_o([(ya, dpa), (yb, dpb), (merged, dx2)])
    dproj, gp_ws, gp_b, gp_nv = _branch_a_bwd(proj, dya, norm_v, w_s[0], b_col, dproj)

    slab = lambda a: a.reshape(d // LANE, LANE)
    gc = groups * chunk
    packed = jnp.concatenate([gp_ws.reshape(gc, chunk), gp_b, slab(gp_nv), slab(dgf), slab(loss_vec)], axis=0)
    dproj, s_oa, s_ob, s_out, packs = _sb_bwd(
        proj, o, dyb, sb_tot, dproj, gp_wo.reshape(3, N_DEV, rsh, d), packed, batch, seq, d, hd)
    grad_x, gp_nin = _dx(dproj, wg_in, x2d, dx2, norm_in)
    s_win, late_packs = _dw_in_exchange(h, dproj, my_slot, slab(gp_nin))
    small = {"w_s": lambda a: a.reshape(gc, chunk), "b_s": lambda a: a[0], "norm_v": slab, "norm_final": slab,
             "norm_in": slab}
    given = {"w_s": (w_s, m_w_s, v_w_s), "b_s": (b_s, m_b_s, v_b_s), "norm_v": (norm_v, m_norm_v, v_norm_v),
             "norm_final": (norm_final, m_norm_final, v_norm_final), "norm_in": (norm_in, m_norm_in, v_norm_in)}
    loss_slab, small_res = _finish_small(
        packs, late_packs, [tuple(small[k](a) for a in given[k]) for k in small], groups, chunk)
    loss = loss_slab[0, 0]

    res = dict(zip(small, small_res))
    res["w_in"] = _reduce_adamw(s_win, w_in[0], m_w_in[0], v_w_in[0], "adamw_w_in", transposed=True)
    res["w_o_gmlp"] = _reduce_adamw(s_oa, w_o_gmlp[0], m_w_o_gmlp[0], v_w_o_gmlp[0], "adamw_w_o_gmlp")
    res["w_o_sb"] = _reduce_adamw(s_ob, w_o_sb[0], m_w_o_sb[0], v_w_o_sb[0], "adamw_w_o_sb")
    res["w_out"] = _reduce_adamw(s_out, w_out[0], m_w_out[0], v_w_out[0], "adamw_w_out")

    shapes = {"norm_in": norm_in.shape, "w_in": w_in.shape, "norm_v": norm_v.shape, "w_s": w_s.shape,
              "b_s": b_s.shape, "w_o_gmlp": w_o_gmlp.shape, "w_o_sb": w_o_sb.shape, "w_out": w_out.shape,
              "norm_final": norm_final.shape}
    names = list(shapes)
    outs = [loss, grad_x.reshape(batch, seq, d)]
    for kind in range(4):
        outs += [res[name][kind].reshape(shapes[name]) for name in names]
    return tuple(outs)
```

```python
import functools
import math

import jax
import jax.numpy as jnp
from jax import lax
from jax.experimental import pallas as pl
from jax.experimental.pallas import tpu as pltpu

F32 = jnp.float32
BF16 = jnp.bfloat16
SDS = jax.ShapeDtypeStruct
MESH_ID = pl.DeviceIdType.MESH

N_DEV = 8
LANE = 128
SUBLANE = 8
VMEM_LIMIT = 56 * 1024 * 1024
SB_TILE = 512
SB_TILE_BWD = 512
SB_SCAN = 256
SB_HEADS = 2
MASKED_LOG = -1e30
RMS_EPS = 1e-6

ADAM_LR = 0.001
ADAM_B1 = 0.9
ADAM_B2 = 0.999
ADAM_EPS = 1e-08
ADAM_WD = 0.01
ADAM_STEP = 10

NT_DIMS = (((1,), (1,)), ((), ()))
TN_DIMS = (((0,), (0,)), ((), ()))


def _params(semantics=None):
    return pltpu.CompilerParams(dimension_semantics=semantics, vmem_limit_bytes=VMEM_LIMIT)


def _tile(n, preferred):
    t = min(n, preferred)
    assert n % t == 0, (n, t)
    return t


def _sigmoid(x):
    return 1.0 / (1.0 + jnp.exp(-x))


def _silu(x):
    s = _sigmoid(x)
    return x * s, s * (1.0 + x * (1.0 - s))


def _gelu(x):
    k = math.sqrt(2.0 / math.pi)
    x2 = x * x
    t = jnp.tanh(k * (x + 0.044715 * (x * x2)))
    cdf = 0.5 * (1.0 + t)
    return x * cdf, cdf + 0.5 * x * (1.0 - t * t) * (k * (1.0 + 3.0 * 0.044715 * x2))


def _rms_scale(x):
    return lax.rsqrt(jnp.mean(x * x, axis=-1, keepdims=True) + RMS_EPS)


def _iotas(n):
    return (lax.broadcasted_iota(jnp.int32, (n, n), 0), lax.broadcasted_iota(jnp.int32, (n, n), 1))


def _adamw(w, g, m, v):
    m = ADAM_B1 * m + (1.0 - ADAM_B1) * g
    v = ADAM_B2 * v + (1.0 - ADAM_B2) * (g * g)
    m_hat = m / (1.0 - ADAM_B1 ** ADAM_STEP)
    v_hat = v / (1.0 - ADAM_B2 ** ADAM_STEP)
    delta = -ADAM_LR * (m_hat / (jnp.sqrt(v_hat) + ADAM_EPS) + ADAM_WD * w)
    return delta, m, v


def _dot(a, b):
    return jnp.dot(a, b, preferred_element_type=F32)


def _dot_nt(a, b):
    return lax.dot_general(a, b, NT_DIMS, preferred_element_type=F32)


def _dot_tn(a, b):
    return lax.dot_general(a, b, TN_DIMS, preferred_element_type=F32)


def _sb_logs(raw, scale, valid):
    z = (raw * scale).astype(BF16)
    log_beta = jnp.minimum(z, 0) - jnp.log(1 + jnp.exp(-jnp.abs(z)))
    log_rest = log_beta - z
    if valid is not None:
        log_beta = jnp.where(valid, log_beta, MASKED_LOG)
        log_rest = jnp.where(valid, log_rest, 0)
    return log_beta, log_rest


def _me():
    return lax.axis_index("x"), lax.axis_index("y"), lax.axis_index("c")


def _slot(p):
    return 4 * p[0] + 2 * p[1] + p[2]


def _peer(me, k):
    flips = ((k >> 2) & 1, (k >> 1) & 1, k & 1)
    return tuple(1 - a if f else a for a, f in zip(me, flips))


def _stack_exchange(me, st_in, st_out, n_whole, send_sems, recv_sems, local_sems, arrivals=True):
    mine = _slot(me)
    ns = len(st_in)
    part = lambda a, dev: st_in[a] if a >= ns - n_whole else st_in[a].at[_slot(dev)]
    local = [pltpu.make_async_copy(part(a, me), st_out[a].at[mine], local_sems.at[a]) for a in range(ns)]
    remote, landed = [], []
    for k in range(1, N_DEV):
        peer = _peer(me, k)
        for a in range(ns):
            sems = dict(send_sem=send_sems.at[7 * a + k - 1], recv_sem=recv_sems.at[7 * a + k - 1])
            remote.append(pltpu.make_async_remote_copy(
                src_ref=part(a, peer), dst_ref=st_out[a].at[mine],
                device_id=peer, device_id_type=MESH_ID, **sems))
            if arrivals:
                got = st_out[a].at[_slot(peer)]
                landed.append(pltpu.make_async_remote_copy(
                    src_ref=got, dst_ref=got, device_id=me, device_id_type=MESH_ID, **sems))
    return local, remote, landed


def _gather_in_proj(x2d, norm_in, w_in_sh, wo_shards, my_slot):
    n, d = x2d.shape
    esh = w_in_sh.shape[1]
    pw = 2 * esh
    n_chip = N_DEV // 2
    tm = _tile(n, 1024)
    n_i = n // tm
    mid = n_i // 2
    no = len(wo_shards)
    flip_at = lambda st: jnp.where(st == 1, 2, jnp.where(st == 2, 1, jnp.where(st == 3, 3, 0)))

    def body(me_ref, x_ref, g_ref, win_ref, *refs):
        del me_ref
        wo_in = refs[:no]
        proj_ref, h_ref, wg_ref = refs[no:no + 3]
        wo_out = refs[no + 3:2 * no + 3]
        wv, stage, h_s = refs[2 * no + 3:2 * no + 6]
        wo_stage = refs[2 * no + 6:3 * no + 6]
        send_sems, recv_sems, pair_sems, own_sems, wo_send, wo_recv, wo_local = refs[3 * no + 6:]
        st, i = pl.program_id(0), pl.program_id(1)
        x, y, c = _me()
        me, sibling = (x, y, c), (x, y, 1 - c)
        chips = [(1 - x, y), (x, 1 - y), (1 - x, 1 - y)]
        chip_id = lambda p: 2 * p[0] + p[1]

        def window(chip, core):
            return wv.at[chip_id(chip), :, pl.ds(pl.multiple_of(core * esh, LANE), esh)]

        def copy(k, block, to, src=None):
            dst = window(block[:2], block[2])
            return pltpu.make_async_remote_copy(
                src_ref=dst if src is None else src, dst_ref=dst,
                send_sem=send_sems.at[k], recv_sem=recv_sems.at[k], device_id=to, device_id_type=MESH_ID)

        def wo_copy(a, k, block, to, src=None):
            dst = wo_out[a].at[_slot(block)]
            return pltpu.make_async_remote_copy(
                src_ref=dst if src is None else src, dst_ref=dst,
                send_sem=wo_send.at[7 * a + k], recv_sem=wo_recv.at[7 * a + k], device_id=to, device_id_type=MESH_ID)

        def own_copy():
            return pltpu.make_async_copy(stage, window((x, y), c), own_sems.at[0])

        def wo_own_copy(a):
            return pltpu.make_async_copy(wo_stage[a], wo_out[a].at[_slot(me)], wo_local.at[a])

        def pair_copy(step):
            chip = jnp.bitwise_xor(chip_id((x, y)), flip_at(step))
            return pltpu.make_async_copy(wv.at[chip], wg_ref.at[:, pl.ds(pl.multiple_of(chip * pw, LANE), pw)],
                                         pair_sems.at[step])

        first = jnp.logical_and(st == 0, i == 0)

        @pl.when(first)
        def _():
            stage[...] = win_ref[...].astype(BF16)
            own_copy().start()
            copy(0, me, sibling, src=stage).start()
            for j in range(2):
                copy(1 + j, me, (*chips[j], c), src=stage).start()
            own_copy().wait()
            copy(0, sibling, me).wait_recv()
            pair_copy(0).start()

        for s_ in range(n_chip - 1):
            @pl.when(jnp.logical_and(st == s_, i == mid))
            def _():
                copy(1 + s_, (*chips[s_], c), me).wait_recv()
                copy(4 + s_, (*chips[s_], c), sibling).start()
                if s_ == 0:
                    copy(3, me, (*chips[2], c), src=stage).start()
                if s_ == 1:
                    for a in range(no):
                        wo_stage[a][...] = wo_in[a][...].astype(BF16)
                        wo_own_copy(a).start()
                        wo_copy(a, 0, me, sibling, src=wo_stage[a]).start()
                        for j, chip in enumerate(chips):
                            wo_copy(a, 1 + j, me, (*chip, c), src=wo_stage[a]).start()
                if s_ == 2:
                    for a in range(no):
                        for j, chip in enumerate(chips):
                            wo_copy(a, 1 + j, (*chip, c), me).wait_recv()
                            wo_copy(a, 4 + j, (*chip, c), sibling).start()

        for s_ in range(1, n_chip):
            @pl.when(jnp.logical_and(st == s_, i == 0))
            def _():
                copy(3 + s_, (*chips[s_ - 1], 1 - c), me).wait_recv()
                pair_copy(s_).start()

        chip_now = jnp.bitwise_xor(chip_id((x, y)), flip_at(st))
        nq = 4
        tq = tm // nq

        def norm_rows(q):
            rs = slice(q * tq, (q + 1) * tq)
            xv = x_ref[rs, :]
            h_s[rs, :] = (xv * _rms_scale(xv) * g_ref[...]).astype(BF16)

        norm_rows(0)
        norm_rows(1)
        for q in range(nq):
            rs = slice(q * tq, (q + 1) * tq)
            proj_ref[rs, :] = _dot(h_s[rs, :], wv[chip_now]).astype(BF16)
            if q + 2 < nq:
                norm_rows(q + 2)

        @pl.when(st == 0)
        def _():
            h_ref[...] = h_s[...]

        @pl.when(jnp.logical_and(st == n_chip - 1, i == n_i - 1))
        def _():
            copy(0, me, sibling, src=stage).wait_send()
            for j, chip in enumerate(chips):
                copy(1 + j, me, (*chip, c), src=stage).wait_send()
                copy(4 + j, (*chip, c), sibling).wait_send()
            for s_ in range(n_chip):
                pair_copy(s_).wait()
            for a in range(no):
                wo_copy(a, 0, me, sibling, src=wo_stage[a]).wait_send()
                wo_copy(a, 0, sibling, me).wait_recv()
                for j, chip in enumerate(chips):
                    wo_copy(a, 1 + j, me, (*chip, c), src=wo_stage[a]).wait_send()
                    wo_copy(a, 4 + j, (*chip, c), sibling).wait_send()
                    wo_copy(a, 4 + j, (*chip, 1 - c), me).wait_recv()
                wo_own_copy(a).wait()

    any_spec = pl.BlockSpec(memory_space=pl.ANY)
    vmem = pl.BlockSpec(memory_space=pltpu.VMEM)
    grid_spec = pltpu.PrefetchScalarGridSpec(
        num_scalar_prefetch=1, grid=(n_chip, n_i),
        in_specs=[pl.BlockSpec((tm, d), lambda st, i, me: (i, 0)),
                  pl.BlockSpec((1, d), lambda st, i, me: (0, 0)), vmem] + [vmem] * no,
        out_specs=[pl.BlockSpec((tm, pw), lambda st, i, me: (i, jnp.bitwise_xor(me[0] // 2, flip_at(st)))),
                   pl.BlockSpec((tm, d), lambda st, i, me: (jnp.where(st == 0, i, n_i - 1), 0)),
                   any_spec] + [any_spec] * no,
        scratch_shapes=[pltpu.VMEM((n_chip, d, pw), BF16), pltpu.VMEM((d, esh), BF16), pltpu.VMEM((tm, d), BF16)] + [
            pltpu.VMEM(s.shape, BF16) for s in wo_shards] + [
            pltpu.SemaphoreType.DMA((7,)), pltpu.SemaphoreType.DMA((7,)),
            pltpu.SemaphoreType.DMA((n_chip,)), pltpu.SemaphoreType.DMA((1,)),
            pltpu.SemaphoreType.DMA((7 * no,)), pltpu.SemaphoreType.DMA((7 * no,)),
            pltpu.SemaphoreType.DMA((no,))])
    return pl.pallas_call(
        body, name="gather_in_proj", grid_spec=grid_spec,
        out_shape=[SDS((n, n_chip * pw), BF16), SDS((n, d), BF16), SDS((d, n_chip * pw), BF16)] + [
            SDS((N_DEV,) + s.shape, BF16) for s in wo_shards],
        compiler_params=pltpu.CompilerParams(dimension_semantics=("arbitrary", "arbitrary"),
                                             vmem_limit_bytes=VMEM_LIMIT),
    )(my_slot, x2d, norm_in, w_in_sh, *wo_shards)


N_CHIP = N_DEV // 2
CHIP_FLIPS = (3, 2, 1, 0)


def _owner_at(mine, j):
    flip = 0
    for pair, f in enumerate(CHIP_FLIPS):
        flip = jnp.where(j // 2 == pair, f, flip)
    return 2 * jnp.bitwise_xor(mine // 2, flip) + j % 2


def _dw_in_exchange(h, dproj, my_slot, packed):
    n, d = h.shape
    esh = dproj.shape[1] // N_DEV
    tk = _tile(n, 2048)
    nk = n // tk
    last_j = N_DEV - 1

    def body(me_ref, h_ref, dp_ref, pk_in, win_out, pk_out,
             acc, halfbuf, recvbuf, sendbuf, half_send, half_recv, win_send, win_recv,
             send_sems, recv_sems, local_sems):
        del me_ref
        j, k = pl.program_id(0), pl.program_id(1)
        x, y, c = _me()
        me, sibling = (x, y, c), (x, y, 1 - c)
        mine = _slot(me)
        my_chip = mine // 2

        def pack_copies():
            local = pltpu.make_async_copy(pk_in, pk_out.at[mine], local_sems.at[0])
            remote = [pltpu.make_async_remote_copy(
                src_ref=pk_in, dst_ref=pk_out.at[mine], send_sem=send_sems.at[kk - 1], recv_sem=recv_sems.at[kk - 1],
                device_id=_peer(me, kk), device_id_type=MESH_ID) for kk in range(1, N_DEV)]
            return local, remote

        def half_copy(jj):
            slot = (jj // 2) % 2
            return pltpu.make_async_remote_copy(
                src_ref=halfbuf.at[slot], dst_ref=recvbuf.at[slot],
                send_sem=half_send.at[slot], recv_sem=half_recv.at[slot],
                device_id=sibling, device_id_type=MESH_ID)

        def chip_copy(jj):
            slot = (jj // 2) % 2
            owner = _owner_at(mine, jj)
            return pltpu.make_async_remote_copy(
                src_ref=sendbuf.at[slot], dst_ref=win_out.at[my_chip],
                send_sem=win_send.at[slot], recv_sem=win_recv.at[my_chip],
                device_id=(owner // 4, (owner // 2) % 2, owner % 2), device_id_type=MESH_ID)

        def own_copy():
            return pltpu.make_async_copy(sendbuf.at[(last_j // 2) % 2], win_out.at[my_chip], local_sems.at[1])

        @pl.when(jnp.logical_and(j == 0, k == 0))
        def _():
            local, remote = pack_copies()
            for cp in [local] + remote:
                cp.start()

        @pl.when(k == 0)
        def _():
            acc[...] = jnp.zeros_like(acc)

        acc[...] += _dot_tn(dp_ref[...], h_ref[...])

        done = k == nk - 1
        combine = j % 2 == c
        slot = (j // 2) % 2

        @pl.when(jnp.logical_and(done, jnp.logical_not(combine)))
        def _():
            @pl.when(j >= 4)
            def _():
                half_copy(j - 4).wait_send()

            halfbuf[slot] = acc[...].astype(BF16)
            half_copy(j).start()

        @pl.when(jnp.logical_and(done, combine))
        def _():
            half_copy(j).wait_recv()

            @pl.when(j >= 4)
            def _():
                chip_copy(j - 4).wait_send()

            sendbuf[slot] = (acc[...] + recvbuf[slot].astype(F32)).astype(BF16)

            @pl.when(j < last_j - 1)
            def _():
                chip_copy(j).start()

            @pl.when(j >= last_j - 1)
            def _():
                own_copy().start()

        @pl.when(jnp.logical_and(j == last_j, done))
        def _():
            half_copy(5 - c).wait_send()
            half_copy(7 - c).wait_send()
            chip_copy(4 + c).wait_send()
            own_copy().wait()
            for chip in range(N_CHIP):
                @pl.when(chip != my_chip)
                def _():
                    landed = win_out.at[chip]
                    pltpu.make_async_remote_copy(
                        src_ref=landed, dst_ref=landed, send_sem=win_send.at[0], recv_sem=win_recv.at[chip],
                        device_id=me, device_id_type=MESH_ID).wait_recv()
            local, remote = pack_copies()
            for cp in remote:
                cp.wait_send()
            for kk in range(1, N_DEV):
                landed = pk_out.at[_slot(_peer(me, kk))]
                pltpu.make_async_remote_copy(
                    src_ref=landed, dst_ref=landed, send_sem=send_sems.at[kk - 1], recv_sem=recv_sems.at[kk - 1],
                    device_id=me, device_id_type=MESH_ID).wait_recv()
            local.wait()

    any_spec = pl.BlockSpec(memory_space=pl.ANY)
    grid_spec = pltpu.PrefetchScalarGridSpec(
        num_scalar_prefetch=1, grid=(N_DEV, nk),
        in_specs=[pl.BlockSpec((tk, d), lambda j, k, me: (k, 0)),
                  pl.BlockSpec((tk, esh), lambda j, k, me: (k, _owner_at(me[0], j))), any_spec],
        out_specs=[any_spec] * 2,
        scratch_shapes=[pltpu.VMEM((esh, d), F32)] + [pltpu.VMEM((2, esh, d), BF16)] * 3 + [
            pltpu.SemaphoreType.DMA((2,)), pltpu.SemaphoreType.DMA((2,)),
            pltpu.SemaphoreType.DMA((2,)), pltpu.SemaphoreType.DMA((N_CHIP,)),
            pltpu.SemaphoreType.DMA((N_DEV - 1,)), pltpu.SemaphoreType.DMA((N_DEV - 1,)),
            pltpu.SemaphoreType.DMA((2,))])
    return pl.pallas_call(
        body, name="dw_in_exchange", grid_spec=grid_spec,
        out_shape=[SDS((N_CHIP, esh, d), BF16), SDS((N_DEV,) + packed.shape, packed.dtype)],
        compiler_params=_params(("arbitrary", "arbitrary")),
    )(my_slot, h, dproj, packed)


def _finish_small(packs, late_packs, states, groups, chunk):
    gc = groups * chunk
    nw = len(states)

    def body(p_ref, l_ref, *refs):
        st = refs[:3 * nw]
        loss_ref = refs[3 * nw]
        outs = refs[3 * nw + 1:]
        row, col = _iotas(chunk)
        tril = col <= row

        def total(ref, rs):
            tot = ref[0, rs, :]
            for dev in range(1, N_DEV):
                tot = tot + ref[dev, rs, :]
            return tot

        def update(k, rs_w, g):
            w_ref, m_ref, v_ref = st[3 * k:3 * k + 3]
            _adamw_outputs(*[o.at[rs_w] for o in outs[4 * k:4 * k + 4]], g, w_ref[rs_w, :], m_ref[rs_w, :], v_ref[rs_w, :])

        for g in range(groups):
            rs = slice(g * chunk, (g + 1) * chunk)
            update(0, rs, jnp.where(tril, total(p_ref, rs), 0.0))
        slab = lambda k: slice(gc + k * SUBLANE, gc + (k + 1) * SUBLANE)
        for k in range(3):
            update(1 + k, slice(0, SUBLANE), total(p_ref, slab(k)))
        loss_ref[...] = jnp.full((SUBLANE, LANE), jnp.sum(total(p_ref, slab(3))), F32)
        update(4, slice(0, SUBLANE), total(l_ref, slice(0, SUBLANE)))

    flat = [a for s in states for a in s]
    vmem = pl.BlockSpec(memory_space=pltpu.VMEM)
    res = pl.pallas_call(
        body, name="finish_small",
        out_shape=[SDS((SUBLANE, LANE), F32)] + [SDS(s[0].shape, F32) for s in states for _ in range(4)],
        in_specs=[vmem] * (2 + len(flat)),
        out_specs=[vmem] * (1 + 4 * nw),
        compiler_params=pltpu.CompilerParams(vmem_limit_bytes=VMEM_LIMIT),
    )(packs, late_packs, *flat)
    return res[0], [res[1 + 4 * k:5 + 4 * k] for k in range(nw)]


def _branch_a_fwd(proj, norm_v, w_s, b_col):
    n = proj.shape[0]
    d = norm_v.shape[1]
    groups, chunk, _ = w_s.shape
    tr = _tile(n, 8 * chunk)

    def body(u_ref, v_ref, z_ref, gv_ref, ws_ref, b_ref, ya_ref, vn_s, pre_s):
        row, col = _iotas(chunk)
        tril = col <= row
        vg = _gelu(v_ref[...])[0].astype(F32)
        vn_s[...] = (vg * _rms_scale(vg) * gv_ref[...]).astype(BF16)
        pre_s[...] = _gelu(u_ref[...])[0] * _silu(z_ref[...])[0]
        for g in range(groups):
            wm = jnp.where(tril, ws_ref[g], 0.0).astype(BF16)
            cs = slice(g * chunk, (g + 1) * chunk)
            for c in range(tr // chunk):
                rs = slice(c * chunk, (c + 1) * chunk)
                mixed = _dot(wm, vn_s[rs, cs]) + b_ref[g]
                ya_ref[rs, cs] = (pre_s[rs, cs].astype(F32) * mixed).astype(BF16)

    seg = lambda k: pl.BlockSpec((tr, d), lambda i: (i, k))
    return pl.pallas_call(
        body, name="branch_a_fwd", grid=(n // tr,),
        in_specs=[seg(0), seg(1), seg(2),
                  pl.BlockSpec((1, d), lambda i: (0, 0)),
                  pl.BlockSpec((groups, chunk, chunk), lambda i: (0, 0, 0)),
                  pl.BlockSpec((groups, chunk, 1), lambda i: (0, 0, 0))],
        out_specs=pl.BlockSpec((tr, d), lambda i: (i, 0)),
        out_shape=SDS((n, d), BF16),
        scratch_shapes=[pltpu.VMEM((tr, d), BF16), pltpu.VMEM((tr, d), BF16)],
        compiler_params=_params(("parallel",)),
    )(proj, proj, proj, norm_v, w_s, b_col)


def _sb_fwd(proj, batch, seq, d, hd):
    heads = d // hd
    t = _tile(seq, SB_TILE)
    sw = _tile(t, SB_SCAN)
    nb = t // sw
    scale = hd ** -0.5
    nblk = seq // t
    nh = SB_HEADS
    wide = nh * hd
    cols = [slice(hh * hd, (hh + 1) * hd) for hh in range(nh)]

    def body(qs, k_ref, vs, zb_ref, yb_ref, o_ref, tot_ref, kts, later, acc):
        for jb in range(nblk):
            kts[jb] = k_ref[jb * t:(jb + 1) * t, :].T
        row, col = _iotas(t)
        later[...] = (row[:sw, :sw] > col[:sw, :sw]).astype(BF16)

        def qblock(i, carry):
            r0 = pl.multiple_of(i * t, t)

            def tile(j, runs):
                c0 = pl.multiple_of(j * t, t)
                logs = [_sb_logs(_dot(qs[pl.ds(r0, t), cs], kts[j, cs, :]), scale, None) for cs in cols]
                scans = [_dot(jnp.concatenate([logs[hh][1][:, b * sw:(b + 1) * sw] for b in range(nb)], axis=0),
                              later[...]) for hh in range(nh)]
                new_runs = []
                for hh in range(nh):
                    after = runs[hh]
                    blocks = [None] * nb
                    for b in reversed(range(nb)):
                        ks_ = slice(b * sw, (b + 1) * sw)
                        inside = scans[hh][b * t:(b + 1) * t]
                        blocks[b] = jnp.exp(logs[hh][0][:, ks_].astype(F32) + inside + after).astype(BF16)
                        after = after + inside[:, 0:1] + logs[hh][1][:, b * sw:b * sw + 1].astype(F32)
                    new_runs.append(after)
                    acc[:, cols[hh]] += _dot(jnp.concatenate(blocks, axis=1), vs[pl.ds(c0, t), cols[hh]])
                return tuple(new_runs)

            def diagonal_tile():
                starts = [b * sw for b in range(nb)]
                logs = [[_sb_logs(_dot(qs[pl.ds(r0 + s, t - s), cs], kts[i, cs, s:s + sw]), scale,
                                  col[:t - s, :sw] < row[:t - s, :sw]) for s in starts] for cs in cols]
                scans = [_dot(jnp.concatenate([lr for _, lr in logs[hh]], axis=0), later[...]) for hh in range(nh)]
                new_runs = []
                offs = [sum(t - s for s in starts[:b]) for b in range(nb)]
                for hh in range(nh):
                    after = jnp.zeros((t, 1), F32)
                    ws = [None] * nb
                    for b in reversed(range(nb)):
                        s = starts[b]
                        lb, lr = logs[hh][b]
                        inside = scans[hh][offs[b]:offs[b] + t - s]
                        ws[b] = jnp.exp(lb.astype(F32) + inside + after[s:]).astype(BF16)
                        total = inside[:, 0:1] + lr[:, 0:1].astype(F32)
                        after = after + total if s == 0 else jnp.concatenate([after[:s], after[s:] + total], axis=0)
                    new_runs.append(after)
                    acc[:, cols[hh]] = _dot(ws[0], vs[pl.ds(r0, sw), cols[hh]])
                    for b in range(1, nb):
                        acc[starts[b]:, cols[hh]] += _dot(ws[b], vs[pl.ds(r0 + starts[b], sw), cols[hh]])
                return tuple(new_runs)

            runs = diagonal_tile()
            runs = lax.fori_loop(0, i, lambda jj, rs: tile(i - 1 - jj, rs), runs)
            for hh in range(nh):
                out = acc[:, cols[hh]]
                o_ref[pl.ds(r0, t), cols[hh]] = out.astype(BF16)
                tot_ref[hh, pl.ds(r0, t), :] = runs[hh]
                sz, _ = _silu(zb_ref[pl.ds(r0, t), cols[hh]].astype(F32))
                yb_ref[pl.ds(r0, t), cols[hh]] = (out * sz).astype(BF16)
            return carry

        lax.fori_loop(0, nblk, qblock, 0)

    col0 = d // wide
    seg = lambda k: pl.BlockSpec((seq, wide), lambda b, h: (b, k * col0 + h))
    return pl.pallas_call(
        body, name="sb_fwd", grid=(batch, heads // nh),
        in_specs=[seg(3), seg(4), seg(5), seg(6)],
        out_specs=[pl.BlockSpec((seq, wide), lambda b, h: (b, h))] * 2 + [
            pl.BlockSpec((nh, seq, 1), lambda b, h: (b * (heads // nh) + h, 0, 0))],
        out_shape=[SDS((batch * seq, d), BF16), SDS((batch * seq, d), BF16), SDS((batch * heads, seq, 1), F32)],
        scratch_shapes=[pltpu.VMEM((nblk, wide, t), BF16), pltpu.VMEM((sw, sw), BF16), pltpu.VMEM((t, wide), F32)],
        compiler_params=_params(("parallel", "parallel")),
    )(proj, proj, proj, proj)


def _tail(x2d, tgt, ya, yb, proj, w_oa, w_ob, w_out, norm_final):
    n, d = x2d.shape
    e = proj.shape[1]
    tm = _tile(n, 512)
    steps = n // tm

    def body(x_ref, t_ref, ya_ref, yb_ref, ga_ref, gb_ref, woa_ref, wob_ref, wout_ref, gf_ref,
             dproj_ref, dx2_ref, dya_ref, dyb_ref, mrg_ref, dpa_ref, dpb_ref, loss_ref, dgf_ref, dg_s, dg_sems):
        i = pl.program_id(0)

        def gate_copy(step):
            rows_ = pl.ds(pl.multiple_of(step * tm, tm), tm)
            return pltpu.make_async_copy(dg_s.at[step % 2], dproj_ref.at[rows_, pl.ds(7 * d, 2 * d)],
                                         dg_sems.at[step % 2])

        @pl.when(i == 0)
        def _():
            loss_ref[...] = jnp.zeros_like(loss_ref)
            dgf_ref[...] = jnp.zeros_like(dgf_ref)

        @pl.when(i >= 2)
        def _():
            gate_copy(i - 2).wait()

        halves = [slice(hf * (tm // 2), (hf + 1) * (tm // 2)) for hf in range(2)] if tm >= 512 else [slice(0, tm)]
        gf = gf_ref[...]
        pa = [_dot(ya_ref[rs, :], woa_ref[...]) for rs in halves]
        pb = [_dot(yb_ref[rs, :], wob_ref[...]) for rs in halves]
        sa = [_sigmoid(ga_ref[rs, :].astype(F32)) for rs in halves]
        sb = [_sigmoid(gb_ref[rs, :].astype(F32)) for rs in halves]
        merged = [(sa[k] * pa[k] + sb[k] * pb[k]).astype(BF16) for k in range(len(halves))]
        for k, rs in enumerate(halves):
            mrg_ref[rs, :] = merged[k]
        x2 = [x_ref[rs, :] + _dot(merged[k], wout_ref[...]) for k, rs in enumerate(halves)]
        dx2 = []
        for k, rs in enumerate(halves):
            r2 = _rms_scale(x2[k])
            xh = x2[k] * r2
            diff = xh * gf - t_ref[rs, :]
            loss_ref[...] += jnp.sum(diff * diff, axis=0, keepdims=True) * (0.5 / d)
            dy = diff * (1.0 / d)
            dgf_ref[...] += jnp.sum(dy * xh, axis=0, keepdims=True)
            dxh = dy * gf
            dx2.append(r2 * (dxh - xh * jnp.mean(dxh * xh, axis=-1, keepdims=True)))
            dx2_ref[rs, :] = dx2[k]
        dm = [_dot_nt(dx2[k].astype(BF16), wout_ref[...]) for k in range(len(halves))]
        dpa, dpb = [], []
        for k, rs in enumerate(halves):
            dpa.append((dm[k] * sa[k]).astype(BF16))
            dpb.append((dm[k] * sb[k]).astype(BF16))
            dpa_ref[rs, :] = dpa[k]
            dpb_ref[rs, :] = dpb[k]
            dg_s[i % 2, rs, 0:d] = (dm[k] * pa[k] * (sa[k] * (1.0 - sa[k]))).astype(BF16)
            dg_s[i % 2, rs, d:2 * d] = (dm[k] * pb[k] * (sb[k] * (1.0 - sb[k]))).astype(BF16)
        gate_copy(i).start()
        for k, rs in enumerate(halves):
            dya_ref[rs, :] = _dot_nt(dpa[k], woa_ref[...]).astype(BF16)
        for k, rs in enumerate(halves):
            dyb_ref[rs, :] = _dot_nt(dpb[k], wob_ref[...]).astype(BF16)

        @pl.when(i == steps - 1)
        def _():
            if steps >= 2:
                gate_copy(i - 1).wait()
            gate_copy(i).wait()

    rows = lambda k=0: pl.BlockSpec((tm, d), lambda i: (i, k))
    full = pl.BlockSpec((d, d), lambda i: (0, 0), pipeline_mode=pl.Buffered(1))
    vec = pl.BlockSpec((1, d), lambda i: (0, 0))
    return pl.pallas_call(
        body, name="tail", grid=(steps,),
        in_specs=[rows(), rows(), rows(), rows(), rows(7), rows(8), full, full, full, vec],
        out_specs=[pl.BlockSpec(memory_space=pl.ANY),
                   rows(), rows(), rows(), rows(), rows(), rows(), vec, vec],
        out_shape=[SDS((n, e), BF16), SDS((n, d), F32), SDS((n, d), BF16), SDS((n, d), BF16),
                   SDS((n, d), BF16), SDS((n, d), BF16), SDS((n, d), BF16),
                   SDS((1, d), F32), SDS((1, d), F32)],
        scratch_shapes=[pltpu.VMEM((2, tm, 2 * d), BF16), pltpu.SemaphoreType.DMA((2,))],
        compiler_params=_params(("arbitrary",)),
    )(x2d, tgt, ya, yb, proj, proj, w_oa, w_ob, w_out, norm_final)


def _dw_o(pairs):
    n, d = pairs[0][0].shape
    tk = _tile(n, 1024)
    nk = n // tk
    npair = len(pairs)

    def body(*refs):
        a_refs, b_refs = refs[:npair], refs[npair:2 * npair]
        o_ref, acc = refs[2 * npair], refs[2 * npair + 1]
        p, k = pl.program_id(0), pl.program_id(1)

        @pl.when(k == 0)
        def _():
            acc[...] = jnp.zeros_like(acc)

        for q in range(npair):
            @pl.when(p == q)
            def _():
                acc[...] += _dot_tn(a_refs[q][...], b_refs[q][...].astype(BF16))

        @pl.when(k == nk - 1)
        def _():
            o_ref[0] = acc[...].astype(BF16)

    def tiles(q):
        return pl.BlockSpec((tk, d), lambda p, k: (jnp.where(p == q, k, jnp.where(p < q, 0, nk - 1)), 0))

    return pl.pallas_call(
        body, name="dw_o", grid=(npair, nk),
        in_specs=[tiles(q) for q in range(npair)] * 2,
        out_specs=pl.BlockSpec((1, d, d), lambda p, k: (p, 0, 0)),
        out_shape=SDS((npair, d, d), BF16),
        scratch_shapes=[pltpu.VMEM((d, d), F32)],
        compiler_params=_params(("arbitrary", "arbitrary")),
    )(*[a for a, _ in pairs], *[b for _, b in pairs])


def _sb_bwd(proj, o, dyb, tot, dproj, dw_stack, packed, batch, seq, d, hd):
    heads = d // hd
    t = _tile(seq, SB_TILE_BWD)
    sw = _tile(t, SB_SCAN)
    nb = t // sw
    scale = hd ** -0.5
    nblk = seq // t
    nh = SB_HEADS
    wide = nh * hd
    hs = range(nh)
    cols = [slice(hh * hd, (hh + 1) * hd) for hh in hs]
    blocks = [slice(b * sw, (b + 1) * sw) for b in range(nb)]
    last = slice(sw - 1, sw)

    def compute(qs, ks, v_ref, zb_ref, o_ref, dyb_ref, tot_ref, kts, vts, dos, dzb, dq_all, dkv_t, qt_s, dot_s,
                upto, before, dq):
        for jb in range(nblk):
            rows = slice(jb * t, (jb + 1) * t)
            kts[jb] = ks[rows, :].T
            vts[jb] = v_ref[rows, :].T
        sz, dsz = _silu(zb_ref[...])
        dyb_v = dyb_ref[...]
        dos[...] = dyb_v * sz
        dzb[...] = dyb_v * o_ref[...] * dsz
        row, col = _iotas(t)
        upto[...] = (row[:sw, :sw] <= col[:sw, :sw]).astype(BF16)
        before[...] = (row[:sw, :sw] < col[:sw, :sw]).astype(BF16)

        def qblock(i, carry):
            r0 = pl.multiple_of(i * t, t)

            def tile(j, sums):
                c0 = pl.multiple_of(j * t, t)
                q_i = [qs[pl.ds(r0, t), cs] for cs in cols]
                do_i = [dos[pl.ds(r0, t), cs] for cs in cols]
                logs = [_sb_logs(_dot(q_i[hh], kts[j, cols[hh], :]), scale, None) for hh in hs]
                scans = [_dot(jnp.concatenate([logs[hh][1][:, ks_] for ks_ in blocks], axis=0), upto[...]) for hh in hs]
                dw = [_dot(do_i[hh], vts[j, cols[hh], :]) for hh in hs]
                ws, gs, new_runs = [], [], []
                for hh in hs:
                    left = tot_ref[hh, pl.ds(r0, t), :] - sums[hh][0]
                    w_b, g_b = [], []
                    for b, ks_ in enumerate(blocks):
                        inside = scans[hh][b * t:(b + 1) * t]
                        w = jnp.exp(logs[hh][0][:, ks_].astype(F32) + (left - inside))
                        w_b.append(w.astype(BF16))
                        g_b.append((dw[hh][:, ks_] * w).astype(BF16))
                        left = left - inside[:, last]
                    ws.append(jnp.concatenate(w_b, axis=1))
                    gs.append(g_b)
                    new_runs.append(tot_ref[hh, pl.ds(r0, t), :] - left)
                gscans = [_dot(jnp.concatenate(gs[hh], axis=0), before[...]) for hh in hs]
                dzs, new_gruns = [], []
                for hh in hs:
                    g_before = sums[hh][1]
                    dz_b = []
                    for b, ks_ in enumerate(blocks):
                        inside = gscans[hh][b * t:(b + 1) * t]
                        beta = jnp.exp(logs[hh][0][:, ks_]).astype(F32)
                        g = gs[hh][b].astype(F32)
                        dz_b.append((g - (g + inside + g_before) * beta).astype(BF16))
                        g_before = g_before + inside[:, last] + g[:, last]
                    dzs.append(jnp.concatenate(dz_b, axis=1))
                    new_gruns.append(g_before)
                for hh in hs:
                    dkv_t[1, j, cols[hh], :] += _dot(dot_s[cols[hh], :], ws[hh])
                for hh in hs:
                    dkv_t[0, j, cols[hh], :] += _dot(qt_s[cols[hh], :], dzs[hh])
                for hh in hs:
                    dq[:, cols[hh]] += _dot(dzs[hh], ks[pl.ds(c0, t), cols[hh]])
                return tuple((new_runs[hh], new_gruns[hh]) for hh in hs)

            def diagonal_tile(sums):
                starts = [b * sw for b in range(nb)]
                offs = [sum(t - s for s in starts[:b]) for b in range(nb)]
                q_b = [[qs[pl.ds(r0 + s, t - s), cs] for s in starts] for cs in cols]
                do_b = [[dos[pl.ds(r0 + s, t - s), cs] for s in starts] for cs in cols]
                logs = [[_sb_logs(_dot(q_b[hh][b], kts[i, cols[hh], s:s + sw]), scale,
                                  col[:t - s, :sw] < row[:t - s, :sw]) for b, s in enumerate(starts)] for hh in hs]
                dw = [[_dot(do_b[hh][b], vts[i, cols[hh], s:s + sw]) for b, s in enumerate(starts)] for hh in hs]
                scans = [_dot(jnp.concatenate([lr for _, lr in logs[hh]], axis=0), upto[...]) for hh in hs]
                ws, gs = [], []
                for hh in hs:
                    left = tot_ref[hh, pl.ds(r0, t), :] - sums[hh][0]
                    w_b, g_b = [], []
                    for b, s in enumerate(starts):
                        inside = scans[hh][offs[b]:offs[b] + t - s]
                        w = jnp.exp(logs[hh][b][0].astype(F32) + (left[s:] - inside))
                        w_b.append(w.astype(BF16))
                        g_b.append((dw[hh][b] * w).astype(BF16))
                        total = inside[:, last]
                        left = left - total if s == 0 else jnp.concatenate([left[:s], left[s:] - total], axis=0)
                    ws.append(w_b)
                    gs.append(g_b)
                gscans = [_dot(jnp.concatenate(gs[hh], axis=0), before[...]) for hh in hs]
                dzs = []
                for hh in hs:
                    g_before = sums[hh][1]
                    dz_b = []
                    for b, s in enumerate(starts):
                        inside = gscans[hh][offs[b]:offs[b] + t - s]
                        beta = jnp.exp(logs[hh][b][0]).astype(F32)
                        g = gs[hh][b].astype(F32)
                        dz_b.append((g - (g + inside + g_before[s:]) * beta).astype(BF16))
                        total = inside[:, last] + g[:, last]
                        g_before = g_before + total if s == 0 else jnp.concatenate(
                            [g_before[:s], g_before[s:] + total], axis=0)
                    dzs.append(dz_b)
                for hh in hs:
                    for b, s in enumerate(starts):
                        dkv_t[1, i, cols[hh], s:s + sw] = _dot(dot_s[cols[hh], s:], ws[hh][b])
                for hh in hs:
                    for b, s in enumerate(starts):
                        dkv_t[0, i, cols[hh], s:s + sw] = _dot(qt_s[cols[hh], s:], dzs[hh][b])
                for hh in hs:
                    for b, s in enumerate(starts):
                        dq[s:, cols[hh]] += _dot(dzs[hh][b], ks[pl.ds(r0 + s, sw), cols[hh]])

            qt_s[...] = qs[pl.ds(r0, t), :].T
            dot_s[...] = dos[pl.ds(r0, t), :].T
            zero = jnp.zeros((t, 1), F32)
            dq[...] = jnp.zeros_like(dq)
            sums = lax.fori_loop(0, i, tile, ((zero, zero),) * nh)
            diagonal_tile(sums)
            dq_all[pl.ds(r0, t), :] = dq[...]
            return carry

        lax.fori_loop(0, nblk, qblock, 0)

    pairs = heads // nh

    nst = dw_stack.shape[0]
    ns = nst + 1

    def body(qs, ks, v_ref, zb_ref, o_ref, dyb_ref, tot_ref, dproj_in, dw_ref, pk_ref, out_ref, *refs):
        del dproj_in
        st_in = [dw_ref.at[k] for k in range(nst)] + [pk_ref]
        st_out = refs[:ns]
        (kts, vts, dos, dzb, dq_all, dkv_t, qt_s, dot_s, upto, before, dq, stage, stage_sems,
         send_sems, recv_sems, local_sems) = refs[ns:]
        step = pl.program_id(0) * pairs + pl.program_id(1)
        exchange = functools.partial(_stack_exchange, _me(), st_in, st_out, 1, send_sems, recv_sems, local_sems)

        @pl.when(step == 0)
        def _():
            local, remote, _ = exchange(arrivals=False)
            for cp in local + remote:
                cp.start()

        def out_copies(s):
            rows_ = pl.ds(pl.multiple_of((s // pairs) * seq, seq), seq)
            return [pltpu.make_async_copy(
                stage.at[k], out_ref.at[rows_, pl.ds(pl.multiple_of((3 + k) * d + (s % pairs) * wide, wide), wide)],
                stage_sems.at[k]) for k in range(4)]

        compute(qs, ks, v_ref, zb_ref, o_ref, dyb_ref, tot_ref, kts, vts, dos, dzb, dq_all, dkv_t, qt_s, dot_s,
                upto, before, dq)

        @pl.when(step > 0)
        def _():
            for cp in out_copies(step - 1):
                cp.wait()

        stage[0] = (dq_all[...] * scale).astype(BF16)
        for jb in range(nblk):
            stage[1, jb * t:(jb + 1) * t, :] = (dkv_t[0, jb] * scale).astype(BF16).T
            stage[2, jb * t:(jb + 1) * t, :] = dkv_t[1, jb].astype(BF16).T
        stage[3] = dzb[...]
        for cp in out_copies(step):
            cp.start()

        @pl.when(step == batch * pairs - 1)
        def _():
            for cp in out_copies(step):
                cp.wait()
            local, remote, landed = exchange()
            for cp in remote:
                cp.wait_send()
            for cp in landed:
                cp.wait_recv()
            for cp in local:
                cp.wait()

    col0 = d // wide
    seg = lambda k: pl.BlockSpec((seq, wide), lambda b, h: (b, k * col0 + h))
    head = pl.BlockSpec((seq, wide), lambda b, h: (b, h))
    any_spec = pl.BlockSpec(memory_space=pl.ANY)
    return pl.pallas_call(
        body, name="sb_bwd", grid=(batch, pairs),
        in_specs=[seg(3), seg(4), seg(5), seg(6), head, head,
                  pl.BlockSpec((nh, seq, 1), lambda b, h: (b * pairs + h, 0, 0))] + [any_spec] * 3,
        out_specs=[any_spec] * (ns + 1),
        out_shape=[SDS(dproj.shape, dproj.dtype)] + [SDS(dw_stack.shape[1:], dw_stack.dtype)] * nst + [
            SDS((N_DEV,) + packed.shape, packed.dtype)],
        input_output_aliases={7: 0},
        scratch_shapes=[pltpu.VMEM((nblk, wide, t), BF16)] * 2 + [
            pltpu.VMEM((seq, wide), BF16), pltpu.VMEM((seq, wide), BF16),
            pltpu.VMEM((seq, wide), F32), pltpu.VMEM((2, nblk, wide, t), F32),
            pltpu.VMEM((wide, t), BF16), pltpu.VMEM((wide, t), BF16),
            pltpu.VMEM((sw, sw), BF16), pltpu.VMEM((sw, sw), BF16), pltpu.VMEM((t, wide), F32),
            pltpu.VMEM((4, seq, wide), BF16), pltpu.SemaphoreType.DMA((4,)),
            pltpu.SemaphoreType.DMA((7 * ns,)), pltpu.SemaphoreType.DMA((7 * ns,)),
            pltpu.SemaphoreType.DMA((ns,))],
        compiler_params=_params(("arbitrary", "arbitrary")),
    )(proj, proj, proj, proj, o, dyb, tot, dproj, dw_stack, packed)


def _branch_a_bwd(proj, dya, norm_v, w_s, b_col, dproj):
    n = proj.shape[0]
    d = norm_v.shape[1]
    groups, chunk, _ = w_s.shape
    tr = _tile(n, 4 * chunk)

    def body(u_ref, v_ref, z_ref, dya_ref, gv_ref, ws_ref, b_ref, dproj_in,
             out_ref, dws_ref, dbias_ref, dgv_ref, vn_s, dmix_s, dvn_s, db_ref):
        del dproj_in

        @pl.when(pl.program_id(0) == 0)
        def _():
            dws_ref[...] = jnp.zeros_like(dws_ref)
            db_ref[...] = jnp.zeros_like(db_ref)
            dgv_ref[...] = jnp.zeros_like(dgv_ref)

        row, col = _iotas(chunk)
        tril = col <= row
        gv = gv_ref[...]
        vg16, dvg_dv = _gelu(v_ref[...])
        vg = vg16.astype(F32)
        r = _rms_scale(vg)
        vh = vg * r
        vn_s[...] = (vh * gv).astype(BF16)
        ug, dug_du = _gelu(u_ref[...])
        sz, dsz = _silu(z_ref[...])
        dya_v = dya_ref[...]
        dmix_s[...] = dya_v * ug * sz
        du_scale = sz * dug_du
        dz_scale = ug * dsz
        for g in range(groups):
            wm = jnp.where(tril, ws_ref[g], 0.0).astype(BF16)
            cs = slice(g * chunk, (g + 1) * chunk)
            for c in range(tr // chunk):
                rs = slice(c * chunk, (c + 1) * chunk)
                vn = vn_s[rs, cs]
                mixed = _dot(wm, vn) + b_ref[g]
                dmix16 = dmix_s[rs, cs]
                dws_ref[g] += _dot_nt(dmix16, vn)
                db_ref[g] += dmix16.astype(F32)
                dvn_s[rs, cs] = _dot_tn(wm, dmix16)
                t_u = dya_v[rs, cs] * mixed.astype(BF16)
                out_ref[rs, g * chunk:(g + 1) * chunk] = t_u * du_scale[rs, cs]
                out_ref[rs, 2 * d + g * chunk:2 * d + (g + 1) * chunk] = t_u * dz_scale[rs, cs]
        dvn = dvn_s[...]
        dgv_ref[...] += jnp.sum(dvn * vh, axis=0, keepdims=True)
        dvh = dvn * gv
        dvg = r * (dvh - vh * jnp.mean(dvh * vh, axis=-1, keepdims=True))
        out_ref[:, d:2 * d] = (dvg * dvg_dv.astype(F32)).astype(BF16)

        @pl.when(pl.program_id(0) == n // tr - 1)
        def _():
            for g in range(groups):
                dbias_ref[g:g + 1, :] = jnp.sum(db_ref[g].T, axis=0, keepdims=True)

    seg = lambda k: pl.BlockSpec((tr, d), lambda i: (i, k))
    return pl.pallas_call(
        body, name="branch_a_bwd", grid=(n // tr,),
        in_specs=[seg(0), seg(1), seg(2), seg(0),
                  pl.BlockSpec((1, d), lambda i: (0, 0)),
                  pl.BlockSpec((groups, chunk, chunk), lambda i: (0, 0, 0)),
                  pl.BlockSpec((groups, chunk, 1), lambda i: (0, 0, 0)),
                  pl.BlockSpec(memory_space=pl.ANY)],
        out_specs=[pl.BlockSpec((tr, 3 * d), lambda i: (i, 0)),
                   pl.BlockSpec((groups, chunk, chunk), lambda i: (0, 0, 0)),
                   pl.BlockSpec((groups, chunk), lambda i: (0, 0)),
                   pl.BlockSpec((1, d), lambda i: (0, 0))],
        out_shape=[SDS(dproj.shape, dproj.dtype), SDS((groups, chunk, chunk), F32),
                   SDS((groups, chunk), F32), SDS((1, d), F32)],
        input_output_aliases={7: 0},
        scratch_shapes=[pltpu.VMEM((tr, d), BF16), pltpu.VMEM((tr, d), BF16), pltpu.VMEM((tr, d), F32),
                        pltpu.VMEM((groups, chunk, chunk), F32)],
        compiler_params=_params(("arbitrary",)),
    )(proj, proj, proj, dya, norm_v, w_s, b_col, dproj)


def _dx(dproj, wg_in, x2d, dx2, norm_in):
    n, d = x2d.shape
    e = wg_in.shape[1]
    tm = _tile(n, 256)

    def body(dp_ref, w_ref, x_ref, dx2_ref, g_ref, gx_ref, dg_ref):
        @pl.when(pl.program_id(0) == 0)
        def _():
            dg_ref[...] = jnp.zeros_like(dg_ref)

        dh = _dot_nt(dp_ref[...], w_ref[...])
        x = x_ref[...]
        r = _rms_scale(x)
        xh = x * r
        dg_ref[...] += jnp.sum(dh * xh, axis=0, keepdims=True)
        dxh = dh * g_ref[...]
        gx_ref[...] = dx2_ref[...] + r * (dxh - xh * jnp.mean(dxh * xh, axis=-1, keepdims=True))

    rows = pl.BlockSpec((tm, d), lambda i: (i, 0))
    vec = pl.BlockSpec((1, d), lambda i: (0, 0))
    return pl.pallas_call(
        body, name="dx", grid=(n // tm,),
        in_specs=[pl.BlockSpec((tm, e), lambda i: (i, 0)),
                  pl.BlockSpec((d, e), lambda i: (0, 0), pipeline_mode=pl.Buffered(1)), rows, rows, vec],
        out_specs=[rows, vec],
        out_shape=[SDS((n, d), F32), SDS((1, d), F32)],
        compiler_params=_params(("arbitrary",)),
    )(dproj, wg_in, x2d, dx2, norm_in)


def _adamw_outputs(g_ref, d_ref, m_ref, v_ref, g, w, m, v):
    delta, m2, v2 = _adamw(w, g, m, v)
    g_ref[...] = g
    d_ref[...] = delta
    m_ref[...] = m2
    v_ref[...] = v2


def _reduce_adamw(slots, w, m, v, name, transposed=False):
    r, c = w.shape
    ns = slots.shape[0]
    tr = _tile(r, 128)

    def body(s_ref, w_ref, m_ref, v_ref, g_out, d_out, m_out, v_out):
        g = s_ref[0].astype(F32)
        for k in range(1, ns):
            g = g + s_ref[k].astype(F32)
        if transposed:
            g = g.T
        _adamw_outputs(g_out, d_out, m_out, v_out, g, w_ref[...], m_ref[...], v_ref[...])

    blk = pl.BlockSpec((tr, c), lambda i: (i, 0))
    slot_blk = (pl.BlockSpec((ns, c, tr), lambda i: (0, 0, i)) if transposed
                else pl.BlockSpec((ns, tr, c), lambda i: (0, i, 0)))
    return pl.pallas_call(
        body, name=name, grid=(r // tr,),
        in_specs=[slot_blk, blk, blk, blk],
        out_specs=[blk] * 4,
        out_shape=[SDS((r, c), F32)] * 4,
        compiler_params=_params(("parallel",)),
    )(slots, w, m, v)


def kernel(x, norm_in, w_in, norm_v, w_s, b_s, w_o_gmlp, w_o_sb, w_out, norm_final, loss_target, m_norm_in, m_w_in, m_norm_v, m_w_s, m_b_s, m_w_o_gmlp, m_w_o_sb, m_w_out, m_norm_final, v_norm_in, v_w_in, v_norm_v, v_w_s, v_b_s, v_w_o_gmlp, v_w_o_sb, v_w_out, v_norm_final):
    batch, seq, d = x.shape
    n = batch * seq
    groups, chunk = w_s.shape[1], w_s.shape[2]
    hd = LANE
    x2d = x.reshape(n, d)
    tgt = loss_target.reshape(n, d)
    b_col = b_s[0].reshape(groups, chunk, 1)
    norm_final2 = norm_final.reshape(1, d)

    my_slot = _slot(_me()).astype(jnp.int32).reshape(1)
    proj, h, wg_in, wg_oa, wg_ob, wg_out = _gather_in_proj(
        x2d, norm_in, w_in[0], [w_o_gmlp[0], w_o_sb[0], w_out[0]], my_slot)
    rsh = wg_oa.shape[1]
    wf_oa, wf_ob, wf_out = (w.reshape(N_DEV * rsh, d) for w in (wg_oa, wg_ob, wg_out))
    ya = _branch_a_fwd(proj, norm_v, w_s[0], b_col)
    yb, o, sb_tot = _sb_fwd(proj, batch, seq, d, hd)
    dproj, dx2, dya, dyb, merged, dpa, dpb, loss_vec, dgf = _tail(
        x2d, tgt, ya, yb, proj, wf_oa, wf_ob, wf_out, norm_final2)
    gp_wo = _dw_o([(ya, dpa), (yb, dpb), (merged, dx2)])
    dproj, gp_ws, gp_b, gp_nv = _branch_a_bwd(proj, dya, norm_v, w_s[0], b_col, dproj)

    slab = lambda a: a.reshape(d // LANE, LANE)
    gc = groups * chunk
    packed = jnp.concatenate([gp_ws.reshape(gc, chunk), gp_b, slab(gp_nv), slab(dgf), slab(loss_vec)], axis=0)
    dproj, s_oa, s_ob, s_out, packs = _sb_bwd(
        proj, o, dyb, sb_tot, dproj, gp_wo.reshape(3, N_DEV, rsh, d), packed, batch, seq, d, hd)
    grad_x, gp_nin = _dx(dproj, wg_in, x2d, dx2, norm_in)
    s_win, late_packs = _dw_in_exchange(h, dproj, my_slot, slab(gp_nin))
    small = {"w_s": lambda a: a.reshape(gc, chunk), "b_s": lambda a: a[0], "norm_v": slab, "norm_final": slab,
             "norm_in": slab}
    given = {"w_s": (w_s, m_w_s, v_w_s), "b_s": (b_s, m_b_s, v_b_s), "norm_v": (norm_v, m_norm_v, v_norm_v),
             "norm_final": (norm_final, m_norm_final, v_norm_final), "norm_in": (norm_in, m_norm_in, v_norm_in)}
    loss_slab, small_res = _finish_small(
        packs, late_packs, [tuple(small[k](a) for a in given[k]) for k in small], groups, chunk)
    loss = loss_slab[0, 0]

    res = dict(zip(small, small_res))
    res["w_in"] = _reduce_adamw(s_win, w_in[0], m_w_in[0], v_w_in[0], "adamw_w_in", transposed=True)
    res["w_o_gmlp"] = _reduce_adamw(s_oa, w_o_gmlp[0], m_w_o_gmlp[0], v_w_o_gmlp[0], "adamw_w_o_gmlp")
    res["w_o_sb"] = _reduce_adamw(s_ob, w_o_sb[0], m_w_o_sb[0], v_w_o_sb[0], "adamw_w_o_sb")
    res["w_out"] = _reduce_adamw(s_out, w_out[0], m_w_out[0], v_w_out[0], "adamw_w_out")

    shapes = {"norm_in": norm_in.shape, "w_in": w_in.shape, "norm_v": norm_v.shape, "w_s": w_s.shape,
              "b_s": b_s.shape, "w_o_gmlp": w_o_gmlp.shape, "w_o_sb": w_o_sb.shape, "w_out": w_out.shape,
              "norm_final": norm_final.shape}
    names = list(shapes)
    outs = [loss, grad_x.reshape(batch, seq, d)]
    for kind in range(4):
        outs += [res[name][kind].reshape(shapes[name]) for name in names]
    return tuple(outs)
```

```python
import functools
import math

import jax
import jax.numpy as jnp
from jax import lax
from jax.experimental import pallas as pl
from jax.experimental.pallas import tpu as pltpu

F32 = jnp.float32
BF16 = jnp.bfloat16
SDS = jax.ShapeDtypeStruct
MESH_ID = pl.DeviceIdType.MESH

N_DEV = 8
LANE = 128
SUBLANE = 8
VMEM_LIMIT = 56 * 1024 * 1024
SB_TILE = 512
SB_TILE_BWD = 512
SB_SCAN = 256
SB_HEADS = 2
MASKED_LOG = -1e30
RMS_EPS = 1e-6

ADAM_LR = 0.001
ADAM_B1 = 0.9
ADAM_B2 = 0.999
ADAM_EPS = 1e-08
ADAM_WD = 0.01
ADAM_STEP = 10

NT_DIMS = (((1,), (1,)), ((), ()))
TN_DIMS = (((0,), (0,)), ((), ()))


def _params(semantics=None):
    return pltpu.CompilerParams(dimension_semantics=semantics, vmem_limit_bytes=VMEM_LIMIT)


def _tile(n, preferred):
    t = min(n, preferred)
    assert n % t == 0, (n, t)
    return t


def _sigmoid(x):
    return 1.0 / (1.0 + jnp.exp(-x))


def _silu(x):
    s = _sigmoid(x)
    return x * s, s * (1.0 + x * (1.0 - s))


def _gelu(x):
    k = math.sqrt(2.0 / math.pi)
    x2 = x * x
    t = jnp.tanh(k * (x + 0.044715 * (x * x2)))
    cdf = 0.5 * (1.0 + t)
    return x * cdf, cdf + 0.5 * x * (1.0 - t * t) * (k * (1.0 + 3.0 * 0.044715 * x2))


def _rms_scale(x):
    return lax.rsqrt(jnp.mean(x * x, axis=-1, keepdims=True) + RMS_EPS)


def _iotas(n):
    return (lax.broadcasted_iota(jnp.int32, (n, n), 0), lax.broadcasted_iota(jnp.int32, (n, n), 1))


def _adamw(w, g, m, v):
    m = ADAM_B1 * m + (1.0 - ADAM_B1) * g
    v = ADAM_B2 * v + (1.0 - ADAM_B2) * (g * g)
    m_hat = m / (1.0 - ADAM_B1 ** ADAM_STEP)
    v_hat = v / (1.0 - ADAM_B2 ** ADAM_STEP)
    delta = -ADAM_LR * (m_hat / (jnp.sqrt(v_hat) + ADAM_EPS) + ADAM_WD * w)
    return delta, m, v


def _dot(a, b):
    return jnp.dot(a, b, preferred_element_type=F32)


def _dot_nt(a, b):
    return lax.dot_general(a, b, NT_DIMS, preferred_element_type=F32)


def _dot_tn(a, b):
    return lax.dot_general(a, b, TN_DIMS, preferred_element_type=F32)


def _sb_logs(raw, scale, valid):
    z = (raw * scale).astype(BF16)
    log_beta = jnp.minimum(z, 0) - jnp.log(1 + jnp.exp(-jnp.abs(z)))
    log_rest = log_beta - z
    if valid is not None:
        log_beta = jnp.where(valid, log_beta, MASKED_LOG)
        log_rest = jnp.where(valid, log_rest, 0)
    return log_beta, log_rest


def _me():
    return lax.axis_index("x"), lax.axis_index("y"), lax.axis_index("c")


def _slot(p):
    return 4 * p[0] + 2 * p[1] + p[2]


def _peer(me, k):
    flips = ((k >> 2) & 1, (k >> 1) & 1, k & 1)
    return tuple(1 - a if f else a for a, f in zip(me, flips))


def _stack_exchange(me, st_in, st_out, n_whole, send_sems, recv_sems, local_sems, arrivals=True):
    mine = _slot(me)
    ns = len(st_in)
    part = lambda a, dev: st_in[a] if a >= ns - n_whole else st_in[a].at[_slot(dev)]
    local = [pltpu.make_async_copy(part(a, me), st_out[a].at[mine], local_sems.at[a]) for a in range(ns)]
    remote, landed = [], []
    for k in range(1, N_DEV):
        peer = _peer(me, k)
        for a in range(ns):
            sems = dict(send_sem=send_sems.at[7 * a + k - 1], recv_sem=recv_sems.at[7 * a + k - 1])
            remote.append(pltpu.make_async_remote_copy(
                src_ref=part(a, peer), dst_ref=st_out[a].at[mine],
                device_id=peer, device_id_type=MESH_ID, **sems))
            if arrivals:
                got = st_out[a].at[_slot(peer)]
                landed.append(pltpu.make_async_remote_copy(
                    src_ref=got, dst_ref=got, device_id=me, device_id_type=MESH_ID, **sems))
    return local, remote, landed


def _gather_in_proj(x2d, norm_in, w_in_sh, wo_shards, my_slot):
    n, d = x2d.shape
    esh = w_in_sh.shape[1]
    pw = 2 * esh
    n_chip = N_DEV // 2
    tm = _tile(n, 1024)
    n_i = n // tm
    mid = n_i // 2
    no = len(wo_shards)
    flip_at = lambda st: jnp.where(st == 1, 2, jnp.where(st == 2, 1, jnp.where(st == 3, 3, 0)))

    def body(me_ref, x_ref, g_ref, win_ref, *refs):
        del me_ref
        wo_in = refs[:no]
        proj_ref, h_ref, wg_ref = refs[no:no + 3]
        wo_out = refs[no + 3:2 * no + 3]
        wv, stage, h_s = refs[2 * no + 3:2 * no + 6]
        wo_stage = refs[2 * no + 6:3 * no + 6]
        send_sems, recv_sems, pair_sems, own_sems, wo_send, wo_recv, wo_local = refs[3 * no + 6:]
        st, i = pl.program_id(0), pl.program_id(1)
        x, y, c = _me()
        me, sibling = (x, y, c), (x, y, 1 - c)
        chips = [(1 - x, y), (x, 1 - y), (1 - x, 1 - y)]
        chip_id = lambda p: 2 * p[0] + p[1]

        def window(chip, core):
            return wv.at[chip_id(chip), :, pl.ds(pl.multiple_of(core * esh, LANE), esh)]

        def copy(k, block, to, src=None):
            dst = window(block[:2], block[2])
            return pltpu.make_async_remote_copy(
                src_ref=dst if src is None else src, dst_ref=dst,
                send_sem=send_sems.at[k], recv_sem=recv_sems.at[k], device_id=to, device_id_type=MESH_ID)

        def wo_copy(a, k, block, to, src=None):
            dst = wo_out[a].at[_slot(block)]
            return pltpu.make_async_remote_copy(
                src_ref=dst if src is None else src, dst_ref=dst,
                send_sem=wo_send.at[7 * a + k], recv_sem=wo_recv.at[7 * a + k], device_id=to, device_id_type=MESH_ID)

        def own_copy():
            return pltpu.make_async_copy(stage, window((x, y), c), own_sems.at[0])

        def wo_own_copy(a):
            return pltpu.make_async_copy(wo_stage[a], wo_out[a].at[_slot(me)], wo_local.at[a])

        def pair_copy(step):
            chip = jnp.bitwise_xor(chip_id((x, y)), flip_at(step))
            return pltpu.make_async_copy(wv.at[chip], wg_ref.at[:, pl.ds(pl.multiple_of(chip * pw, LANE), pw)],
                                         pair_sems.at[step])

        first = jnp.logical_and(st == 0, i == 0)

        @pl.when(first)
        def _():
            stage[...] = win_ref[...].astype(BF16)
            own_copy().start()
            copy(0, me, sibling, src=stage).start()
            for j in range(2):
                copy(1 + j, me, (*chips[j], c), src=stage).start()
            own_copy().wait()
            copy(0, sibling, me).wait_recv()
            pair_copy(0).start()

        for s_ in range(n_chip - 1):
            @pl.when(jnp.logical_and(st == s_, i == mid))
            def _():
                copy(1 + s_, (*chips[s_], c), me).wait_recv()
                copy(4 + s_, (*chips[s_], c), sibling).start()
                if s_ == 0:
                    copy(3, me, (*chips[2], c), src=stage).start()
                if s_ == 1:
                    for a in range(no):
                        wo_stage[a][...] = wo_in[a][...].astype(BF16)
                        wo_own_copy(a).start()
                        wo_copy(a, 0, me, sibling, src=wo_stage[a]).start()
                        for j, chip in enumerate(chips):
                            wo_copy(a, 1 + j, me, (*chip, c), src=wo_stage[a]).start()
                if s_ == 2:
                    for a in range(no):
                        for j, chip in enumerate(chips):
                            wo_copy(a, 1 + j, (*chip, c), me).wait_recv()
                            wo_copy(a, 4 + j, (*chip, c), sibling).start()

        for s_ in range(1, n_chip):
            @pl.when(jnp.logical_and(st == s_, i == 0))
            def _():
                copy(3 + s_, (*chips[s_ - 1], 1 - c), me).wait_recv()
                pair_copy(s_).start()

        chip_now = jnp.bitwise_xor(chip_id((x, y)), flip_at(st))
        nq = 4
        tq = tm // nq

        def norm_rows(q):
            rs = slice(q * tq, (q + 1) * tq)
            xv = x_ref[rs, :]
            h_s[rs, :] = (xv * _rms_scale(xv) * g_ref[...]).astype(BF16)

        norm_rows(0)
        norm_rows(1)
        for q in range(nq):
            rs = slice(q * tq, (q + 1) * tq)
            proj_ref[rs, :] = _dot(h_s[rs, :], wv[chip_now]).astype(BF16)
            if q + 2 < nq:
                norm_rows(q + 2)

        @pl.when(st == 0)
        def _():
            h_ref[...] = h_s[...]

        @pl.when(jnp.logical_and(st == n_chip - 1, i == n_i - 1))
        def _():
            copy(0, me, sibling, src=stage).wait_send()
            for j, chip in enumerate(chips):
                copy(1 + j, me, (*chip, c), src=stage).wait_send()
                copy(4 + j, (*chip, c), sibling).wait_send()
            for s_ in range(n_chip):
                pair_copy(s_).wait()
            for a in range(no):
                wo_copy(a, 0, me, sibling, src=wo_stage[a]).wait_send()
                wo_copy(a, 0, sibling, me).wait_recv()
                for j, chip in enumerate(chips):
                    wo_copy(a, 1 + j, me, (*chip, c), src=wo_stage[a]).wait_send()
                    wo_copy(a, 4 + j, (*chip, c), sibling).wait_send()
                    wo_copy(a, 4 + j, (*chip, 1 - c), me).wait_recv()
                wo_own_copy(a).wait()

    any_spec = pl.BlockSpec(memory_space=pl.ANY)
    vmem = pl.BlockSpec(memory_space=pltpu.VMEM)
    grid_spec = pltpu.PrefetchScalarGridSpec(
        num_scalar_prefetch=1, grid=(n_chip, n_i),
        in_specs=[pl.BlockSpec((tm, d), lambda st, i, me: (i, 0)),
                  pl.BlockSpec((1, d), lambda st, i, me: (0, 0)), vmem] + [vmem] * no,
        out_specs=[pl.BlockSpec((tm, pw), lambda st, i, me: (i, jnp.bitwise_xor(me[0] // 2, flip_at(st)))),
                   pl.BlockSpec((tm, d), lambda st, i, me: (jnp.where(st == 0, i, n_i - 1), 0)),
                   any_spec] + [any_spec] * no,
        scratch_shapes=[pltpu.VMEM((n_chip, d, pw), BF16), pltpu.VMEM((d, esh), BF16), pltpu.VMEM((tm, d), BF16)] + [
            pltpu.VMEM(s.shape, BF16) for s in wo_shards] + [
            pltpu.SemaphoreType.DMA((7,)), pltpu.SemaphoreType.DMA((7,)),
            pltpu.SemaphoreType.DMA((n_chip,)), pltpu.SemaphoreType.DMA((1,)),
            pltpu.SemaphoreType.DMA((7 * no,)), pltpu.SemaphoreType.DMA((7 * no,)),
            pltpu.SemaphoreType.DMA((no,))])
    return pl.pallas_call(
        body, name="gather_in_proj", grid_spec=grid_spec,
        out_shape=[SDS((n, n_chip * pw), BF16), SDS((n, d), BF16), SDS((d, n_chip * pw), BF16)] + [
            SDS((N_DEV,) + s.shape, BF16) for s in wo_shards],
        compiler_params=pltpu.CompilerParams(dimension_semantics=("arbitrary", "arbitrary"),
                                             vmem_limit_bytes=VMEM_LIMIT),
    )(my_slot, x2d, norm_in, w_in_sh, *wo_shards)


N_CHIP = N_DEV // 2
CHIP_FLIPS = (3, 2, 1, 0)


def _owner_at(mine, j):
    flip = 0
    for pair, f in enumerate(CHIP_FLIPS):
        flip = jnp.where(j // 2 == pair, f, flip)
    return 2 * jnp.bitwise_xor(mine // 2, flip) + j % 2


def _dw_in_exchange(h, dproj, my_slot, packed):
    n, d = h.shape
    esh = dproj.shape[1] // N_DEV
    tk = _tile(n, 2048)
    nk = n // tk
    last_j = N_DEV - 1

    def body(me_ref, h_ref, dp_ref, pk_in, win_out, pk_out,
             acc, halfbuf, recvbuf, sendbuf, half_send, half_recv, win_send, win_recv,
             send_sems, recv_sems, local_sems):
        del me_ref
        j, k = pl.program_id(0), pl.program_id(1)
        x, y, c = _me()
        me, sibling = (x, y, c), (x, y, 1 - c)
        mine = _slot(me)
        my_chip = mine // 2

        def pack_copies():
            local = pltpu.make_async_copy(pk_in, pk_out.at[mine], local_sems.at[0])
            remote = [pltpu.make_async_remote_copy(
                src_ref=pk_in, dst_ref=pk_out.at[mine], send_sem=send_sems.at[kk - 1], recv_sem=recv_sems.at[kk - 1],
                device_id=_peer(me, kk), device_id_type=MESH_ID) for kk in range(1, N_DEV)]
            return local, remote

        def half_copy(jj):
            slot = (jj // 2) % 2
            return pltpu.make_async_remote_copy(
                src_ref=halfbuf.at[slot], dst_ref=recvbuf.at[slot],
                send_sem=half_send.at[slot], recv_sem=half_recv.at[slot],
                device_id=sibling, device_id_type=MESH_ID)

        def chip_copy(jj):
            slot = (jj // 2) % 2
            owner = _owner_at(mine, jj)
            return pltpu.make_async_remote_copy(
                src_ref=sendbuf.at[slot], dst_ref=win_out.at[my_chip],
                send_sem=win_send.at[slot], recv_sem=win_recv.at[my_chip],
                device_id=(owner // 4, (owner // 2) % 2, owner % 2), device_id_type=MESH_ID)

        def own_copy():
            return pltpu.make_async_copy(sendbuf.at[(last_j // 2) % 2], win_out.at[my_chip], local_sems.at[1])

        @pl.when(jnp.logical_and(j == 0, k == 0))
        def _():
            local, remote = pack_copies()
            for cp in [local] + remote:
                cp.start()

        @pl.when(k == 0)
        def _():
            acc[...] = jnp.zeros_like(acc)

        acc[...] += _dot_tn(dp_ref[...], h_ref[...])

        done = k == nk - 1
        combine = j % 2 == c
        slot = (j // 2) % 2

        @pl.when(jnp.logical_and(done, jnp.logical_not(combine)))
        def _():
            @pl.when(j >= 4)
            def _():
                half_copy(j - 4).wait_send()

            halfbuf[slot] = acc[...].astype(BF16)
            half_copy(j).start()

        @pl.when(jnp.logical_and(done, combine))
        def _():
            half_copy(j).wait_recv()

            @pl.when(j >= 4)
            def _():
                chip_copy(j - 4).wait_send()

            sendbuf[slot] = (acc[...] + recvbuf[slot].astype(F32)).astype(BF16)

            @pl.when(j < last_j - 1)
            def _():
                chip_copy(j).start()

            @pl.when(j >= last_j - 1)
            def _():
                own_copy().start()

        @pl.when(jnp.logical_and(j == last_j, done))
        def _():
            half_copy(5 - c).wait_send()
            half_copy(7 - c).wait_send()
            chip_copy(4 + c).wait_send()
            own_copy().wait()
            for chip in range(N_CHIP):
                @pl.when(chip != my_chip)
                def _():
                    landed = win_out.at[chip]
                    pltpu.make_async_remote_copy(
                        src_ref=landed, dst_ref=landed, send_sem=win_send.at[0], recv_sem=win_recv.at[chip],
                        device_id=me, device_id_type=MESH_ID).wait_recv()
            local, remote = pack_copies()
            for cp in remote:
                cp.wait_send()
            for kk in range(1, N_DEV):
                landed = pk_out.at[_slot(_peer(me, kk))]
                pltpu.make_async_remote_copy(
                    src_ref=landed, dst_ref=landed, send_sem=send_sems.at[kk - 1], recv_sem=recv_sems.at[kk - 1],
                    device_id=me, device_id_type=MESH_ID).wait_recv()
            local.wait()

    any_spec = pl.BlockSpec(memory_space=pl.ANY)
    grid_spec = pltpu.PrefetchScalarGridSpec(
        num_scalar_prefetch=1, grid=(N_DEV, nk),
        in_specs=[pl.BlockSpec((tk, d), lambda j, k, me: (k, 0)),
                  pl.BlockSpec((tk, esh), lambda j, k, me: (k, _owner_at(me[0], j))), any_spec],
        out_specs=[any_spec] * 2,
        scratch_shapes=[pltpu.VMEM((esh, d), F32)] + [pltpu.VMEM((2, esh, d), BF16)] * 3 + [
            pltpu.SemaphoreType.DMA((2,)), pltpu.SemaphoreType.DMA((2,)),
            pltpu.SemaphoreType.DMA((2,)), pltpu.SemaphoreType.DMA((N_CHIP,)),
            pltpu.SemaphoreType.DMA((N_DEV - 1,)), pltpu.SemaphoreType.DMA((N_DEV - 1,)),
            pltpu.SemaphoreType.DMA((2,))])
    return pl.pallas_call(
        body, name="dw_in_exchange", grid_spec=grid_spec,
        out_shape=[SDS((N_CHIP, esh, d), BF16), SDS((N_DEV,) + packed.shape, packed.dtype)],
        compiler_params=_params(("arbitrary", "arbitrary")),
    )(my_slot, h, dproj, packed)


def _finish_small(packs, late_packs, states, groups, chunk):
    gc = groups * chunk
    nw = len(states)

    def body(p_ref, l_ref, *refs):
        st = refs[:3 * nw]
        loss_ref = refs[3 * nw]
        outs = refs[3 * nw + 1:]
        row, col = _iotas(chunk)
        tril = col <= row

        def total(ref, rs):
            tot = ref[0, rs, :]
            for dev in range(1, N_DEV):
                tot = tot + ref[dev, rs, :]
            return tot

        def update(k, rs_w, g):
            w_ref, m_ref, v_ref = st[3 * k:3 * k + 3]
            _adamw_outputs(*[o.at[rs_w] for o in outs[4 * k:4 * k + 4]], g, w_ref[rs_w, :], m_ref[rs_w, :], v_ref[rs_w, :])

        for g in range(groups):
            rs = slice(g * chunk, (g + 1) * chunk)
            update(0, rs, jnp.where(tril, total(p_ref, rs), 0.0))
        slab = lambda k: slice(gc + k * SUBLANE, gc + (k + 1) * SUBLANE)
        for k in range(3):
            update(1 + k, slice(0, SUBLANE), total(p_ref, slab(k)))
        loss_ref[...] = jnp.full((SUBLANE, LANE), jnp.sum(total(p_ref, slab(3))), F32)
        update(4, slice(0, SUBLANE), total(l_ref, slice(0, SUBLANE)))

    flat = [a for s in states for a in s]
    vmem = pl.BlockSpec(memory_space=pltpu.VMEM)
    res = pl.pallas_call(
        body, name="finish_small",
        out_shape=[SDS((SUBLANE, LANE), F32)] + [SDS(s[0].shape, F32) for s in states for _ in range(4)],
        in_specs=[vmem] * (2 + len(flat)),
        out_specs=[vmem] * (1 + 4 * nw),
        compiler_params=pltpu.CompilerParams(vmem_limit_bytes=VMEM_LIMIT),
    )(packs, late_packs, *flat)
    return res[0], [res[1 + 4 * k:5 + 4 * k] for k in range(nw)]


def _branch_a_fwd(proj, norm_v, w_s, b_col):
    n = proj.shape[0]
    d = norm_v.shape[1]
    groups, chunk, _ = w_s.shape
    tr = _tile(n, 8 * chunk)

    def body(u_ref, v_ref, z_ref, gv_ref, ws_ref, b_ref, ya_ref, vn_s, pre_s):
        row, col = _iotas(chunk)
        tril = col <= row
        vg = _gelu(v_ref[...])[0].astype(F32)
        vn_s[...] = (vg * _rms_scale(vg) * gv_ref[...]).astype(BF16)
        pre_s[...] = _gelu(u_ref[...])[0] * _silu(z_ref[...])[0]
        for g in range(groups):
            wm = jnp.where(tril, ws_ref[g], 0.0).astype(BF16)
            cs = slice(g * chunk, (g + 1) * chunk)
            for c in range(tr // chunk):
                rs = slice(c * chunk, (c + 1) * chunk)
                mixed = _dot(wm, vn_s[rs, cs]) + b_ref[g]
                ya_ref[rs, cs] = (pre_s[rs, cs].astype(F32) * mixed).astype(BF16)

    seg = lambda k: pl.BlockSpec((tr, d), lambda i: (i, k))
    return pl.pallas_call(
        body, name="branch_a_fwd", grid=(n // tr,),
        in_specs=[seg(0), seg(1), seg(2),
                  pl.BlockSpec((1, d), lambda i: (0, 0)),
                  pl.BlockSpec((groups, chunk, chunk), lambda i: (0, 0, 0)),
                  pl.BlockSpec((groups, chunk, 1), lambda i: (0, 0, 0))],
        out_specs=pl.BlockSpec((tr, d), lambda i: (i, 0)),
        out_shape=SDS((n, d), BF16),
        scratch_shapes=[pltpu.VMEM((tr, d), BF16), pltpu.VMEM((tr, d), BF16)],
        compiler_params=_params(("parallel",)),
    )(proj, proj, proj, norm_v, w_s, b_col)


def _sb_fwd(proj, batch, seq, d, hd):
    heads = d // hd
    t = _tile(seq, SB_TILE)
    sw = _tile(t, SB_SCAN)
    nb = t // sw
    scale = hd ** -0.5
    nblk = seq // t
    nh = SB_HEADS
    wide = nh * hd
    cols = [slice(hh * hd, (hh + 1) * hd) for hh in range(nh)]

    def body(qs, k_ref, vs, zb_ref, yb_ref, o_ref, tot_ref, kts, later, acc):
        for jb in range(nblk):
            kts[jb] = k_ref[jb * t:(jb + 1) * t, :].T
        row, col = _iotas(t)
        later[...] = (row[:sw, :sw] > col[:sw, :sw]).astype(BF16)

        def qblock(i, carry):
            r0 = pl.multiple_of(i * t, t)

            def tile(j, runs):
                c0 = pl.multiple_of(j * t, t)
                logs = [_sb_logs(_dot(qs[pl.ds(r0, t), cs], kts[j, cs, :]), scale, None) for cs in cols]
                scans = [_dot(jnp.concatenate([logs[hh][1][:, b * sw:(b + 1) * sw] for b in range(nb)], axis=0),
                              later[...]) for hh in range(nh)]
                new_runs = []
                for hh in range(nh):
                    after = runs[hh]
                    blocks = [None] * nb
                    for b in reversed(range(nb)):
                        ks_ = slice(b * sw, (b + 1) * sw)
                        inside = scans[hh][b * t:(b + 1) * t]
                        blocks[b] = jnp.exp(logs[hh][0][:, ks_].astype(F32) + inside + after).astype(BF16)
                        after = after + inside[:, 0:1] + logs[hh][1][:, b * sw:b * sw + 1].astype(F32)
                    new_runs.append(after)
                    acc[:, cols[hh]] += _dot(jnp.concatenate(blocks, axis=1), vs[pl.ds(c0, t), cols[hh]])
                return tuple(new_runs)

            def diagonal_tile():
                starts = [b * sw for b in range(nb)]
                logs = [[_sb_logs(_dot(qs[pl.ds(r0 + s, t - s), cs], kts[i, cs, s:s + sw]), scale,
                                  col[:t - s, :sw] < row[:t - s, :sw]) for s in starts] for cs in cols]
                scans = [_dot(jnp.concatenate([lr for _, lr in logs[hh]], axis=0), later[...]) for hh in range(nh)]
                new_runs = []
                offs = [sum(t - s for s in starts[:b]) for b in range(nb)]
                for hh in range(nh):
                    after = jnp.zeros((t, 1), F32)
                    ws = [None] * nb
                    for b in reversed(range(nb)):
                        s = starts[b]
                        lb, lr = logs[hh][b]
                        inside = scans[hh][offs[b]:offs[b] + t - s]
                        ws[b] = jnp.exp(lb.astype(F32) + inside + after[s:]).astype(BF16)
                        total = inside[:, 0:1] + lr[:, 0:1].astype(F32)
                        after = after + total if s == 0 else jnp.concatenate([after[:s], after[s:] + total], axis=0)
                    new_runs.append(after)
                    acc[:, cols[hh]] = _dot(ws[0], vs[pl.ds(r0, sw), cols[hh]])
                    for b in range(1, nb):
                        acc[starts[b]:, cols[hh]] += _dot(ws[b], vs[pl.ds(r0 + starts[b], sw), cols[hh]])
                return tuple(new_runs)

            runs = diagonal_tile()
            runs = lax.fori_loop(0, i, lambda jj, rs: tile(i - 1 - jj, rs), runs)
            for hh in range(nh):
                out = acc[:, cols[hh]]
                o_ref[pl.ds(r0, t), cols[hh]] = out.astype(BF16)
                tot_ref[hh, pl.ds(r0, t), :] = runs[hh]
                sz, _ = _silu(zb_ref[pl.ds(r0, t), cols[hh]].astype(F32))
                yb_ref[pl.ds(r0, t), cols[hh]] = (out * sz).astype(BF16)
            return carry

        lax.fori_loop(0, nblk, qblock, 0)

    col0 = d // wide
    seg = lambda k: pl.BlockSpec((seq, wide), lambda b, h: (b, k * col0 + h))
    return pl.pallas_call(
        body, name="sb_fwd", grid=(batch, heads // nh),
        in_specs=[seg(3), seg(4), seg(5), seg(6)],
        out_specs=[pl.BlockSpec((seq, wide), lambda b, h: (b, h))] * 2 + [
            pl.BlockSpec((nh, seq, 1), lambda b, h: (b * (heads // nh) + h, 0, 0))],
        out_shape=[SDS((batch * seq, d), BF16), SDS((batch * seq, d), BF16), SDS((batch * heads, seq, 1), F32)],
        scratch_shapes=[pltpu.VMEM((nblk, wide, t), BF16), pltpu.VMEM((sw, sw), BF16), pltpu.VMEM((t, wide), F32)],
        compiler_params=_params(("parallel", "parallel")),
    )(proj, proj, proj, proj)


def _tail(x2d, tgt, ya, yb, proj, w_oa, w_ob, w_out, norm_final):
    n, d = x2d.shape
    e = proj.shape[1]
    tm = _tile(n, 512)
    steps = n // tm

    def body(x_ref, t_ref, ya_ref, yb_ref, ga_ref, gb_ref, woa_ref, wob_ref, wout_ref, gf_ref,
             dproj_ref, dx2_ref, dya_ref, dyb_ref, mrg_ref, dpa_ref, dpb_ref, loss_ref, dgf_ref, dg_s, dg_sems):
        i = pl.program_id(0)

        def gate_copy(step):
            rows_ = pl.ds(pl.multiple_of(step * tm, tm), tm)
            return pltpu.make_async_copy(dg_s.at[step % 2], dproj_ref.at[rows_, pl.ds(7 * d, 2 * d)],
                                         dg_sems.at[step % 2])

        @pl.when(i == 0)
        def _():
            loss_ref[...] = jnp.zeros_like(loss_ref)
            dgf_ref[...] = jnp.zeros_like(dgf_ref)

        @pl.when(i >= 2)
        def _():
            gate_copy(i - 2).wait()

        halves = [slice(hf * (tm // 2), (hf + 1) * (tm // 2)) for hf in range(2)] if tm >= 512 else [slice(0, tm)]
        gf = gf_ref[...]
        pa = [_dot(ya_ref[rs, :], woa_ref[...]) for rs in halves]
        pb = [_dot(yb_ref[rs, :], wob_ref[...]) for rs in halves]
        sa = [_sigmoid(ga_ref[rs, :].astype(F32)) for rs in halves]
        sb = [_sigmoid(gb_ref[rs, :].astype(F32)) for rs in halves]
        merged = [(sa[k] * pa[k] + sb[k] * pb[k]).astype(BF16) for k in range(len(halves))]
        for k, rs in enumerate(halves):
            mrg_ref[rs, :] = merged[k]
        x2 = [x_ref[rs, :] + _dot(merged[k], wout_ref[...]) for k, rs in enumerate(halves)]
        dx2 = []
        for k, rs in enumerate(halves):
            r2 = _rms_scale(x2[k])
            xh = x2[k] * r2
            diff = xh * gf - t_ref[rs, :]
            loss_ref[...] += jnp.sum(diff * diff, axis=0, keepdims=True) * (0.5 / d)
            dy = diff * (1.0 / d)
            dgf_ref[...] += jnp.sum(dy * xh, axis=0, keepdims=True)
            dxh = dy * gf
            dx2.append(r2 * (dxh - xh * jnp.mean(dxh * xh, axis=-1, keepdims=True)))
            dx2_ref[rs, :] = dx2[k]
        dm = [_dot_nt(dx2[k].astype(BF16), wout_ref[...]) for k in range(len(halves))]
        dpa, dpb = [], []
        for k, rs in enumerate(halves):
            dpa.append((dm[k] * sa[k]).astype(BF16))
            dpb.append((dm[k] * sb[k]).astype(BF16))
            dpa_ref[rs, :] = dpa[k]
            dpb_ref[rs, :] = dpb[k]
            dg_s[i % 2, rs, 0:d] = (dm[k] * pa[k] * (sa[k] * (1.0 - sa[k]))).astype(BF16)
            dg_s[i % 2, rs, d:2 * d] = (dm[k] * pb[k] * (sb[k] * (1.0 - sb[k]))).astype(BF16)
        gate_copy(i).start()
        for k, rs in enumerate(halves):
            dya_ref[rs, :] = _dot_nt(dpa[k], woa_ref[...]).astype(BF16)
        for k, rs in enumerate(halves):
            dyb_ref[rs, :] = _dot_nt(dpb[k], wob_ref[...]).astype(BF16)

        @pl.when(i == steps - 1)
        def _():
            if steps >= 2:
                gate_copy(i - 1).wait()
            gate_copy(i).wait()

    rows = lambda k=0: pl.BlockSpec((tm, d), lambda i: (i, k))
    full = pl.BlockSpec((d, d), lambda i: (0, 0), pipeline_mode=pl.Buffered(1))
    vec = pl.BlockSpec((1, d), lambda i: (0, 0))
    return pl.pallas_call(
        body, name="tail", grid=(steps,),
        in_specs=[rows(), rows(), rows(), rows(), rows(7), rows(8), full, full, full, vec],
        out_specs=[pl.BlockSpec(memory_space=pl.ANY),
                   rows(), rows(), rows(), rows(), rows(), rows(), vec, vec],
        out_shape=[SDS((n, e), BF16), SDS((n, d), F32), SDS((n, d), BF16), SDS((n, d), BF16),
                   SDS((n, d), BF16), SDS((n, d), BF16), SDS((n, d), BF16),
                   SDS((1, d), F32), SDS((1, d), F32)],
        scratch_shapes=[pltpu.VMEM((2, tm, 2 * d), BF16), pltpu.SemaphoreType.DMA((2,))],
        compiler_params=_params(("arbitrary",)),
    )(x2d, tgt, ya, yb, proj, proj, w_oa, w_ob, w_out, norm_final)


def _dw_o(pairs):
    n, d = pairs[0][0].shape
    tk = _tile(n, 1024)
    nk = n // tk
    npair = len(pairs)

    def body(*refs):
        a_refs, b_refs = refs[:npair], refs[npair:2 * npair]
        o_ref, acc = refs[2 * npair], refs[2 * npair + 1]
        p, k = pl.program_id(0), pl.program_id(1)

        @pl.when(k == 0)
        def _():
            acc[...] = jnp.zeros_like(acc)

        for q in range(npair):
            @pl.when(p == q)
            def _():
                acc[...] += _dot_tn(a_refs[q][...], b_refs[q][...].astype(BF16))

        @pl.when(k == nk - 1)
        def _():
            o_ref[0] = acc[...].astype(BF16)

    def tiles(q):
        return pl.BlockSpec((tk, d), lambda p, k: (jnp.where(p == q, k, jnp.where(p < q, 0, nk - 1)), 0))

    return pl.pallas_call(
        body, name="dw_o", grid=(npair, nk),
        in_specs=[tiles(q) for q in range(npair)] * 2,
        out_specs=pl.BlockSpec((1, d, d), lambda p, k: (p, 0, 0)),
        out_shape=SDS((npair, d, d), BF16),
        scratch_shapes=[pltpu.VMEM((d, d), F32)],
        compiler_params=_params(("arbitrary", "arbitrary")),
    )(*[a for a, _ in pairs], *[b for _, b in pairs])


def _sb_bwd(proj, o, dyb, tot, dproj, dw_stack, packed, batch, seq, d, hd):
    heads = d // hd
    t = _tile(seq, SB_TILE_BWD)
    sw = _tile(t, SB_SCAN)
    nb = t // sw
    scale = hd ** -0.5
    nblk = seq // t
    nh = SB_HEADS
    wide = nh * hd
    hs = range(nh)
    cols = [slice(hh * hd, (hh + 1) * hd) for hh in hs]
    blocks = [slice(b * sw, (b + 1) * sw) for b in range(nb)]
    last = slice(sw - 1, sw)

    def compute(qs, ks, v_ref, zb_ref, o_ref, dyb_ref, tot_ref, kts, vts, dos, dzb, dq_all, dkv_t, qt_s, dot_s,
                upto, before, dq):
        for jb in range(nblk):
            rows = slice(jb * t, (jb + 1) * t)
            kts[jb] = ks[rows, :].T
            vts[jb] = v_ref[rows, :].T
        sz, dsz = _silu(zb_ref[...])
        dyb_v = dyb_ref[...]
        dos[...] = dyb_v * sz
        dzb[...] = dyb_v * o_ref[...] * dsz
        row, col = _iotas(t)
        upto[...] = (row[:sw, :sw] <= col[:sw, :sw]).astype(BF16)
        before[...] = (row[:sw, :sw] < col[:sw, :sw]).astype(BF16)

        def qblock(i, carry):
            r0 = pl.multiple_of(i * t, t)

            def tile(j, sums):
                c0 = pl.multiple_of(j * t, t)
                q_i = [qs[pl.ds(r0, t), cs] for cs in cols]
                do_i = [dos[pl.ds(r0, t), cs] for cs in cols]
                logs = [_sb_logs(_dot(q_i[hh], kts[j, cols[hh], :]), scale, None) for hh in hs]
                scans = [_dot(jnp.concatenate([logs[hh][1][:, ks_] for ks_ in blocks], axis=0), upto[...]) for hh in hs]
                dw = [_dot(do_i[hh], vts[j, cols[hh], :]) for hh in hs]
                ws, gs, new_runs = [], [], []
                for hh in hs:
                    left = tot_ref[hh, pl.ds(r0, t), :] - sums[hh][0]
                    w_b, g_b = [], []
                    for b, ks_ in enumerate(blocks):
                        inside = scans[hh][b * t:(b + 1) * t]
                        w = jnp.exp(logs[hh][0][:, ks_].astype(F32) + (left - inside))
                        w_b.append(w.astype(BF16))
                        g_b.append((dw[hh][:, ks_] * w).astype(BF16))
                        left = left - inside[:, last]
                    ws.append(jnp.concatenate(w_b, axis=1))
                    gs.append(g_b)
                    new_runs.append(tot_ref[hh, pl.ds(r0, t), :] - left)
                gscans = [_dot(jnp.concatenate(gs[hh], axis=0), before[...]) for hh in hs]
                dzs, new_gruns = [], []
                for hh in hs:
                    g_before = sums[hh][1]
                    dz_b = []
                    for b, ks_ in enumerate(blocks):
                        inside = gscans[hh][b * t:(b + 1) * t]
                        beta = jnp.exp(logs[hh][0][:, ks_]).astype(F32)
                        g = gs[hh][b].astype(F32)
                        dz_b.append((g - (g + inside + g_before) * beta).astype(BF16))
                        g_before = g_before + inside[:, last] + g[:, last]
                    dzs.append(jnp.concatenate(dz_b, axis=1))
                    new_gruns.append(g_before)
                for hh in hs:
                    dkv_t[1, j, cols[hh], :] += _dot(dot_s[cols[hh], :], ws[hh])
                for hh in hs:
                    dkv_t[0, j, cols[hh], :] += _dot(qt_s[cols[hh], :], dzs[hh])
                for hh in hs:
                    dq[:, cols[hh]] += _dot(dzs[hh], ks[pl.ds(c0, t), cols[hh]])
                return tuple((new_runs[hh], new_gruns[hh]) for hh in hs)

            def diagonal_tile(sums):
                starts = [b * sw for b in range(nb)]
                offs = [sum(t - s for s in starts[:b]) for b in range(nb)]
                q_b = [[qs[pl.ds(r0 + s, t - s), cs] for s in starts] for cs in cols]
                do_b = [[dos[pl.ds(r0 + s, t - s), cs] for s in starts] for cs in cols]
                logs = [[_sb_logs(_dot(q_b[hh][b], kts[i, cols[hh], s:s + sw]), scale,
                                  col[:t - s, :sw] < row[:t - s, :sw]) for b, s in enumerate(starts)] for hh in hs]
                dw = [[_dot(do_b[hh][b], vts[i, cols[hh], s:s + sw]) for b, s in enumerate(starts)] for hh in hs]
                scans = [_dot(jnp.concatenate([lr for _, lr in logs[hh]], axis=0), upto[...]) for hh in hs]
                ws, gs = [], []
                for hh in hs:
                    left = tot_ref[hh, pl.ds(r0, t), :] - sums[hh][0]
                    w_b, g_b = [], []
                    for b, s in enumerate(starts):
                        inside = scans[hh][offs[b]:offs[b] + t - s]
                        w = jnp.exp(logs[hh][b][0].astype(F32) + (left[s:] - inside))
                        w_b.append(w.astype(BF16))
                        g_b.append((dw[hh][b] * w).astype(BF16))
                        total = inside[:, last]
                        left = left - total if s == 0 else jnp.concatenate([left[:s], left[s:] - total], axis=0)
                    ws.append(w_b)
                    gs.append(g_b)
                gscans = [_dot(jnp.concatenate(gs[hh], axis=0), before[...]) for hh in hs]
                dzs = []
                for hh in hs:
                    g_before = sums[hh][1]
                    dz_b = []
                    for b, s in enumerate(starts):
                        inside = gscans[hh][offs[b]:offs[b] + t - s]
                        beta = jnp.exp(logs[hh][b][0]).astype(F32)
                        g = gs[hh][b].astype(F32)
                        dz_b.append((g - (g + inside + g_before[s:]) * beta).astype(BF16))
                        total = inside[:, last] + g[:, last]
                        g_before = g_before + total if s == 0 else jnp.concatenate(
                            [g_before[:s], g_before[s:] + total], axis=0)
                    dzs.append(dz_b)
                for hh in hs:
                    for b, s in enumerate(starts):
                        dkv_t[1, i, cols[hh], s:s + sw] = _dot(dot_s[cols[hh], s:], ws[hh][b])
                for hh in hs:
                    for b, s in enumerate(starts):
                        dkv_t[0, i, cols[hh], s:s + sw] = _dot(qt_s[cols[hh], s:], dzs[hh][b])
                for hh in hs:
                    for b, s in enumerate(starts):
                        dq[s:, cols[hh]] += _dot(dzs[hh][b], ks[pl.ds(r0 + s, sw), cols[hh]])

            qt_s[...] = qs[pl.ds(r0, t), :].T
            dot_s[...] = dos[pl.ds(r0, t), :].T
            zero = jnp.zeros((t, 1), F32)
            dq[...] = jnp.zeros_like(dq)
            sums = lax.fori_loop(0, i, tile, ((zero, zero),) * nh)
            diagonal_tile(sums)
            dq_all[pl.ds(r0, t), :] = dq[...]
            return carry

        lax.fori_loop(0, nblk, qblock, 0)

    pairs = heads // nh

    nst = dw_stack.shape[0]
    ns = nst + 1

    def body(qs, ks, v_ref, zb_ref, o_ref, dyb_ref, tot_ref, dproj_in, dw_ref, pk_ref, out_ref, *refs):
        del dproj_in
        st_in = [dw_ref.at[k] for k in range(nst)] + [pk_ref]
        st_out = refs[:ns]
        (kts, vts, dos, dzb, dq_all, dkv_t, qt_s, dot_s, upto, before, dq, stage, stage_sems,
         send_sems, recv_sems, local_sems) = refs[ns:]
        step = pl.program_id(0) * pairs + pl.program_id(1)
        exchange = functools.partial(_stack_exchange, _me(), st_in, st_out, 1, send_sems, recv_sems, local_sems)

        @pl.when(step == 0)
        def _():
            local, remote, _ = exchange(arrivals=False)
            for cp in local + remote:
                cp.start()

        def out_copies(s):
            rows_ = pl.ds(pl.multiple_of((s // pairs) * seq, seq), seq)
            return [pltpu.make_async_copy(
                stage.at[k], out_ref.at[rows_, pl.ds(pl.multiple_of((3 + k) * d + (s % pairs) * wide, wide), wide)],
                stage_sems.at[k]) for k in range(4)]

        compute(qs, ks, v_ref, zb_ref, o_ref, dyb_ref, tot_ref, kts, vts, dos, dzb, dq_all, dkv_t, qt_s, dot_s,
                upto, before, dq)

        @pl.when(step > 0)
        def _():
            for cp in out_copies(step - 1):
                cp.wait()

        stage[0] = (dq_all[...] * scale).astype(BF16)
        for jb in range(nblk):
            stage[1, jb * t:(jb + 1) * t, :] = (dkv_t[0, jb] * scale).astype(BF16).T
            stage[2, jb * t:(jb + 1) * t, :] = dkv_t[1, jb].astype(BF16).T
        stage[3] = dzb[...]
        for cp in out_copies(step):
            cp.start()

        @pl.when(step == batch * pairs - 1)
        def _():
            for cp in out_copies(step):
                cp.wait()
            local, remote, landed = exchange()
            for cp in remote:
                cp.wait_send()
            for cp in landed:
                cp.wait_recv()
            for cp in local:
                cp.wait()

    col0 = d // wide
    seg = lambda k: pl.BlockSpec((seq, wide), lambda b, h: (b, k * col0 + h))
    head = pl.BlockSpec((seq, wide), lambda b, h: (b, h))
    any_spec = pl.BlockSpec(memory_space=pl.ANY)
    return pl.pallas_call(
        body, name="sb_bwd", grid=(batch, pairs),
        in_specs=[seg(3), seg(4), seg(5), seg(6), head, head,
                  pl.BlockSpec((nh, seq, 1), lambda b, h: (b * pairs + h, 0, 0))] + [any_spec] * 3,
        out_specs=[any_spec] * (ns + 1),
        out_shape=[SDS(dproj.shape, dproj.dtype)] + [SDS(dw_stack.shape[1:], dw_stack.dtype)] * nst + [
            SDS((N_DEV,) + packed.shape, packed.dtype)],
        input_output_aliases={7: 0},
        scratch_shapes=[pltpu.VMEM((nblk, wide, t), BF16)] * 2 + [
            pltpu.VMEM((seq, wide), BF16), pltpu.VMEM((seq, wide), BF16),
            pltpu.VMEM((seq, wide), F32), pltpu.VMEM((2, nblk, wide, t), F32),
            pltpu.VMEM((wide, t), BF16), pltpu.VMEM((wide, t), BF16),
            pltpu.VMEM((sw, sw), BF16), pltpu.VMEM((sw, sw), BF16), pltpu.VMEM((t, wide), F32),
            pltpu.VMEM((4, seq, wide), BF16), pltpu.SemaphoreType.DMA((4,)),
            pltpu.SemaphoreType.DMA((7 * ns,)), pltpu.SemaphoreType.DMA((7 * ns,)),
            pltpu.SemaphoreType.DMA((ns,))],
        compiler_params=_params(("arbitrary", "arbitrary")),
    )(proj, proj, proj, proj, o, dyb, tot, dproj, dw_stack, packed)


def _branch_a_bwd(proj, dya, norm_v, w_s, b_col, dproj):
    n = proj.shape[0]
    d = norm_v.shape[1]
    groups, chunk, _ = w_s.shape
    tr = _tile(n, 4 * chunk)

    def body(u_ref, v_ref, z_ref, dya_ref, gv_ref, ws_ref, b_ref, dproj_in,
             out_ref, dws_ref, dbias_ref, dgv_ref, vn_s, dmix_s, dvn_s, db_ref):
        del dproj_in

        @pl.when(pl.program_id(0) == 0)
        def _():
            dws_ref[...] = jnp.zeros_like(dws_ref)
            db_ref[...] = jnp.zeros_like(db_ref)
            dgv_ref[...] = jnp.zeros_like(dgv_ref)

        row, col = _iotas(chunk)
        tril = col <= row
        gv = gv_ref[...]
        vg16, dvg_dv = _gelu(v_ref[...])
        vg = vg16.astype(F32)
        r = _rms_scale(vg)
        vh = vg * r
        vn_s[...] = (vh * gv).astype(BF16)
        ug, dug_du = _gelu(u_ref[...])
        sz, dsz = _silu(z_ref[...])
        dya_v = dya_ref[...]
        dmix_s[...] = dya_v * ug * sz
        du_scale = sz * dug_du
        dz_scale = ug * dsz
        for g in range(groups):
            wm = jnp.where(tril, ws_ref[g], 0.0).astype(BF16)
            cs = slice(g * chunk, (g + 1) * chunk)
            for c in range(tr // chunk):
                rs = slice(c * chunk, (c + 1) * chunk)
                vn = vn_s[rs, cs]
                mixed = _dot(wm, vn) + b_ref[g]
                dmix16 = dmix_s[rs, cs]
                dws_ref[g] += _dot_nt(dmix16, vn)
                db_ref[g] += dmix16.astype(F32)
                dvn_s[rs, cs] = _dot_tn(wm, dmix16)
                t_u = dya_v[rs, cs] * mixed.astype(BF16)
                out_ref[rs, g * chunk:(g + 1) * chunk] = t_u * du_scale[rs, cs]
                out_ref[rs, 2 * d + g * chunk:2 * d + (g + 1) * chunk] = t_u * dz_scale[rs, cs]
        dvn = dvn_s[...]
        dgv_ref[...] += jnp.sum(dvn * vh, axis=0, keepdims=True)
        dvh = dvn * gv
        dvg = r * (dvh - vh * jnp.mean(dvh * vh, axis=-1, keepdims=True))
        out_ref[:, d:2 * d] = (dvg * dvg_dv.astype(F32)).astype(BF16)

        @pl.when(pl.program_id(0) == n // tr - 1)
        def _():
            for g in range(groups):
                dbias_ref[g:g + 1, :] = jnp.sum(db_ref[g].T, axis=0, keepdims=True)

    seg = lambda k: pl.BlockSpec((tr, d), lambda i: (i, k))
    return pl.pallas_call(
        body, name="branch_a_bwd", grid=(n // tr,),
        in_specs=[seg(0), seg(1), seg(2), seg(0),
                  pl.BlockSpec((1, d), lambda i: (0, 0)),
                  pl.BlockSpec((groups, chunk, chunk), lambda i: (0, 0, 0)),
                  pl.BlockSpec((groups, chunk, 1), lambda i: (0, 0, 0)),
                  pl.BlockSpec(memory_space=pl.ANY)],
        out_specs=[pl.BlockSpec((tr, 3 * d), lambda i: (i, 0)),
                   pl.BlockSpec((groups, chunk, chunk), lambda i: (0, 0, 0)),
                   pl.BlockSpec((groups, chunk), lambda i: (0, 0)),
                   pl.BlockSpec((1, d), lambda i: (0, 0))],
        out_shape=[SDS(dproj.shape, dproj.dtype), SDS((groups, chunk, chunk), F32),
                   SDS((groups, chunk), F32), SDS((1, d), F32)],
        input_output_aliases={7: 0},
        scratch_shapes=[pltpu.VMEM((tr, d), BF16), pltpu.VMEM((tr, d), BF16), pltpu.VMEM((tr, d), F32),
                        pltpu.VMEM((groups, chunk, chunk), F32)],
        compiler_params=_params(("arbitrary",)),
    )(proj, proj, proj, dya, norm_v, w_s, b_col, dproj)


def _dx(dproj, wg_in, x2d, dx2, norm_in):
    n, d = x2d.shape
    e = wg_in.shape[1]
    tm = _tile(n, 256)

    def body(dp_ref, w_ref, x_ref, dx2_ref, g_ref, gx_ref, dg_ref):
        @pl.when(pl.program_id(0) == 0)
        def _():
            dg_ref[...] = jnp.zeros_like(dg_ref)

        dh = _dot_nt(dp_ref[...], w_ref[...])
        x = x_ref[...]
        r = _rms_scale(x)
        xh = x * r
        dg_ref[...] += jnp.sum(dh * xh, axis=0, keepdims=True)
        dxh = dh * g_ref[...]
        gx_ref[...] = dx2_ref[...] + r * (dxh - xh * jnp.mean(dxh * xh, axis=-1, keepdims=True))

    rows = pl.BlockSpec((tm, d), lambda i: (i, 0))
    vec = pl.BlockSpec((1, d), lambda i: (0, 0))
    return pl.pallas_call(
        body, name="dx", grid=(n // tm,),
        in_specs=[pl.BlockSpec((tm, e), lambda i: (i, 0)),
                  pl.BlockSpec((d, e), lambda i: (0, 0), pipeline_mode=pl.Buffered(1)), rows, rows, vec],
        out_specs=[rows, vec],
        out_shape=[SDS((n, d), F32), SDS((1, d), F32)],
        compiler_params=_params(("arbitrary",)),
    )(dproj, wg_in, x2d, dx2, norm_in)


def _adamw_outputs(g_ref, d_ref, m_ref, v_ref, g, w, m, v):
    delta, m2, v2 = _adamw(w, g, m, v)
    g_ref[...] = g
    d_ref[...] = delta
    m_ref[...] = m2
    v_ref[...] = v2


def _reduce_adamw(slots, w, m, v, name, transposed=False):
    r, c = w.shape
    ns = slots.shape[0]
    tr = _tile(r, 128)

    def body(s_ref, w_ref, m_ref, v_ref, g_out, d_out, m_out, v_out):
        g = s_ref[0].astype(F32)
        for k in range(1, ns):
            g = g + s_ref[k].astype(F32)
        if transposed:
            g = g.T
        _adamw_outputs(g_out, d_out, m_out, v_out, g, w_ref[...], m_ref[...], v_ref[...])

    blk = pl.BlockSpec((tr, c), lambda i: (i, 0))
    slot_blk = (pl.BlockSpec((ns, c, tr), lambda i: (0, 0, i)) if transposed
                else pl.BlockSpec((ns, tr, c), lambda i: (0, i, 0)))
    return pl.pallas_call(
        body, name=name, grid=(r // tr,),
        in_specs=[slot_blk, blk, blk, blk],
        out_specs=[blk] * 4,
        out_shape=[SDS((r, c), F32)] * 4,
        compiler_params=_params(("parallel",)),
    )(slots, w, m, v)


def _reduce_adamw_group(stacks, states, name):
    nw = len(stacks)
    r, c = states[0][0].shape
    ns = stacks[0].shape[0]
    tr = _tile(r, 32)

    def body(*refs):
        s_refs, st_refs, out_refs = refs[:nw], refs[nw:4 * nw], refs[4 * nw:]
        for k in range(nw):
            g = s_refs[k][0].astype(F32)
            for j in range(1, ns):
                g = g + s_refs[k][j].astype(F32)
            w_ref, m_ref, v_ref = st_refs[3 * k:3 * k + 3]
            _adamw_outputs(*out_refs[4 * k:4 * k + 4], g, w_ref[...], m_ref[...], v_ref[...])

    blk = pl.BlockSpec((tr, c), lambda i: (i, 0))
    slot_blk = pl.BlockSpec((ns, tr, c), lambda i: (0, i, 0))
    res = pl.pallas_call(
        body, name=name, grid=(r // tr,),
        in_specs=[slot_blk] * nw + [blk] * (3 * nw),
        out_specs=[blk] * (4 * nw),
        out_shape=[SDS((r, c), F32)] * (4 * nw),
        compiler_params=_params(("parallel",)),
    )(*stacks, *[a for st in states for a in st])
    return [res[4 * k:4 * k + 4] for k in range(nw)]


def kernel(x, norm_in, w_in, norm_v, w_s, b_s, w_o_gmlp, w_o_sb, w_out, norm_final, loss_target, m_norm_in, m_w_in, m_norm_v, m_w_s, m_b_s, m_w_o_gmlp, m_w_o_sb, m_w_out, m_norm_final, v_norm_in, v_w_in, v_norm_v, v_w_s, v_b_s, v_w_o_gmlp, v_w_o_sb, v_w_out, v_norm_final):
    batch, seq, d = x.shape
    n = batch * seq
    groups, chunk = w_s.shape[1], w_s.shape[2]
    hd = LANE
    x2d = x.reshape(n, d)
    tgt = loss_target.reshape(n, d)
    b_col = b_s[0].reshape(groups, chunk, 1)
    norm_final2 = norm_final.reshape(1, d)

    my_slot = _slot(_me()).astype(jnp.int32).reshape(1)
    proj, h, wg_in, wg_oa, wg_ob, wg_out = _gather_in_proj(
        x2d, norm_in, w_in[0], [w_o_gmlp[0], w_o_sb[0], w_out[0]], my_slot)
    rsh = wg_oa.shape[1]
    wf_oa, wf_ob, wf_out = (w.reshape(N_DEV * rsh, d) for w in (wg_oa, wg_ob, wg_out))
    ya = _branch_a_fwd(proj, norm_v, w_s[0], b_col)
    yb, o, sb_tot = _sb_fwd(proj, batch, seq, d, hd)
    dproj, dx2, dya, dyb, merged, dpa, dpb, loss_vec, dgf = _tail(
        x2d, tgt, ya, yb, proj, wf_oa, wf_ob, wf_out, norm_final2)
    gp_wo = _dw_o([(ya, dpa), (yb, dpb), (merged, dx2)])
    dproj, gp_ws, gp_b, gp_nv = _branch_a_bwd(proj, dya, norm_v, w_s[0], b_col, dproj)

    slab = lambda a: a.reshape(d // LANE, LANE)
    gc = groups * chunk
    packed = jnp.concatenate([gp_ws.reshape(gc, chunk), gp_b, slab(gp_nv), slab(dgf), slab(loss_vec)], axis=0)
    dproj, s_oa, s_ob, s_out, packs = _sb_bwd(
        proj, o, dyb, sb_tot, dproj, gp_wo.reshape(3, N_DEV, rsh, d), packed, batch, seq, d, hd)
    grad_x, gp_nin = _dx(dproj, wg_in, x2d, dx2, norm_in)
    s_win, late_packs = _dw_in_exchange(h, dproj, my_slot, slab(gp_nin))
    small = {"w_s": lambda a: a.reshape(gc, chunk), "b_s": lambda a: a[0], "norm_v": slab, "norm_final": slab,
             "norm_in": slab}
    given = {"w_s": (w_s, m_w_s, v_w_s), "b_s": (b_s, m_b_s, v_b_s), "norm_v": (norm_v, m_norm_v, v_norm_v),
             "norm_final": (norm_final, m_norm_final, v_norm_final), "norm_in": (norm_in, m_norm_in, v_norm_in)}
    loss_slab, small_res = _finish_small(
        packs, late_packs, [tuple(small[k](a) for a in given[k]) for k in small], groups, chunk)
    loss = loss_slab[0, 0]

    res = dict(zip(small, small_res))
    res["w_in"] = _reduce_adamw(s_win, w_in[0], m_w_in[0], v_w_in[0], "adamw_w_in", transposed=True)
    res["w_o_gmlp"], res["w_o_sb"], res["w_out"] = _reduce_adamw_group(
        [s_oa, s_ob, s_out],
        [(w_o_gmlp[0], m_w_o_gmlp[0], v_w_o_gmlp[0]), (w_o_sb[0], m_w_o_sb[0], v_w_o_sb[0]),
         (w_out[0], m_w_out[0], v_w_out[0])], "adamw_w_o")

    shapes = {"norm_in": norm_in.shape, "w_in": w_in.shape, "norm_v": norm_v.shape, "w_s": w_s.shape,
              "b_s": b_s.shape, "w_o_gmlp": w_o_gmlp.shape, "w_o_sb": w_o_sb.shape, "w_out": w_out.shape,
              "norm_final": norm_final.shape}
    names = list(shapes)
    outs = [loss, grad_x.reshape(batch, seq, d)]
    for kind in range(4):
        outs += [res[name][kind].reshape(shapes[name]) for name in names]
    return tuple(outs)
```

```python
import functools
import math

import jax
import jax.numpy as jnp
from jax import lax
from jax.experimental import pallas as pl
from jax.experimental.pallas import tpu as pltpu

F32 = jnp.float32
BF16 = jnp.bfloat16
SDS = jax.ShapeDtypeStruct
MESH_ID = pl.DeviceIdType.MESH

N_DEV = 8
LANE = 128
SUBLANE = 8
VMEM_LIMIT = 56 * 1024 * 1024
SB_TILE = 512
SB_TILE_BWD = 512
SB_SCAN = 256
SB_HEADS = 2
MASKED_LOG = -1e30
RMS_EPS = 1e-6

ADAM_LR = 0.001
ADAM_B1 = 0.9
ADAM_B2 = 0.999
ADAM_EPS = 1e-08
ADAM_WD = 0.01
ADAM_STEP = 10

NT_DIMS = (((1,), (1,)), ((), ()))
TN_DIMS = (((0,), (0,)), ((), ()))


def _params(semantics=None):
    return pltpu.CompilerParams(dimension_semantics=semantics, vmem_limit_bytes=VMEM_LIMIT)


def _tile(n, preferred):
    t = min(n, preferred)
    assert n % t == 0, (n, t)
    return t


def _sigmoid(x):
    return 1.0 / (1.0 + jnp.exp(-x))


def _silu(x):
    s = _sigmoid(x)
    return x * s, s * (1.0 + x * (1.0 - s))


def _gelu(x):
    k = math.sqrt(2.0 / math.pi)
    x2 = x * x
    t = jnp.tanh(k * (x + 0.044715 * (x * x2)))
    cdf = 0.5 * (1.0 + t)
    return x * cdf, cdf + 0.5 * x * (1.0 - t * t) * (k * (1.0 + 3.0 * 0.044715 * x2))


def _rms_scale(x):
    return lax.rsqrt(jnp.mean(x * x, axis=-1, keepdims=True) + RMS_EPS)


def _iotas(n):
    return (lax.broadcasted_iota(jnp.int32, (n, n), 0), lax.broadcasted_iota(jnp.int32, (n, n), 1))


def _adamw(w, g, m, v):
    m = ADAM_B1 * m + (1.0 - ADAM_B1) * g
    v = ADAM_B2 * v + (1.0 - ADAM_B2) * (g * g)
    m_hat = m / (1.0 - ADAM_B1 ** ADAM_STEP)
    v_hat = v / (1.0 - ADAM_B2 ** ADAM_STEP)
    delta = -ADAM_LR * (m_hat / (jnp.sqrt(v_hat) + ADAM_EPS) + ADAM_WD * w)
    return delta, m, v


def _dot(a, b):
    return jnp.dot(a, b, preferred_element_type=F32)


def _dot_nt(a, b):
    return lax.dot_general(a, b, NT_DIMS, preferred_element_type=F32)


def _dot_tn(a, b):
    return lax.dot_general(a, b, TN_DIMS, preferred_element_type=F32)


def _sb_logs(raw, scale, valid):
    z = (raw * scale).astype(BF16)
    log_beta = jnp.minimum(z, 0) - jnp.log(1 + jnp.exp(-jnp.abs(z)))
    log_rest = log_beta - z
    if valid is not None:
        log_beta = jnp.where(valid, log_beta, MASKED_LOG)
        log_rest = jnp.where(valid, log_rest, 0)
    return log_beta, log_rest


def _me():
    return lax.axis_index("x"), lax.axis_index("y"), lax.axis_index("c")


def _slot(p):
    return 4 * p[0] + 2 * p[1] + p[2]


def _peer(me, k):
    flips = ((k >> 2) & 1, (k >> 1) & 1, k & 1)
    return tuple(1 - a if f else a for a, f in zip(me, flips))


def _stack_exchange(me, st_in, st_out, n_whole, send_sems, recv_sems, local_sems, arrivals=True):
    mine = _slot(me)
    ns = len(st_in)
    part = lambda a, dev: st_in[a] if a >= ns - n_whole else st_in[a].at[_slot(dev)]
    local = [pltpu.make_async_copy(part(a, me), st_out[a].at[mine], local_sems.at[a]) for a in range(ns)]
    remote, landed = [], []
    for k in range(1, N_DEV):
        peer = _peer(me, k)
        for a in range(ns):
            sems = dict(send_sem=send_sems.at[7 * a + k - 1], recv_sem=recv_sems.at[7 * a + k - 1])
            remote.append(pltpu.make_async_remote_copy(
                src_ref=part(a, peer), dst_ref=st_out[a].at[mine],
                device_id=peer, device_id_type=MESH_ID, **sems))
            if arrivals:
                got = st_out[a].at[_slot(peer)]
                landed.append(pltpu.make_async_remote_copy(
                    src_ref=got, dst_ref=got, device_id=me, device_id_type=MESH_ID, **sems))
    return local, remote, landed


def _gather_in_proj(x2d, norm_in, w_in_sh, wo_shards, my_slot):
    n, d = x2d.shape
    esh = w_in_sh.shape[1]
    pw = 2 * esh
    n_chip = N_DEV // 2
    tm = _tile(n, 1024)
    n_i = n // tm
    mid = n_i // 2
    no = len(wo_shards)
    flip_at = lambda st: jnp.where(st == 1, 2, jnp.where(st == 2, 1, jnp.where(st == 3, 3, 0)))

    def body(me_ref, x_ref, g_ref, win_ref, *refs):
        del me_ref
        wo_in = refs[:no]
        proj_ref, h_ref, wg_ref = refs[no:no + 3]
        wo_out = refs[no + 3:2 * no + 3]
        wv, stage, h_s = refs[2 * no + 3:2 * no + 6]
        wo_stage = refs[2 * no + 6:3 * no + 6]
        send_sems, recv_sems, pair_sems, own_sems, wo_send, wo_recv, wo_local = refs[3 * no + 6:]
        st, i = pl.program_id(0), pl.program_id(1)
        x, y, c = _me()
        me, sibling = (x, y, c), (x, y, 1 - c)
        chips = [(1 - x, y), (x, 1 - y), (1 - x, 1 - y)]
        chip_id = lambda p: 2 * p[0] + p[1]

        def window(chip, core):
            return wv.at[chip_id(chip), :, pl.ds(pl.multiple_of(core * esh, LANE), esh)]

        def copy(k, block, to, src=None):
            dst = window(block[:2], block[2])
            return pltpu.make_async_remote_copy(
                src_ref=dst if src is None else src, dst_ref=dst,
                send_sem=send_sems.at[k], recv_sem=recv_sems.at[k], device_id=to, device_id_type=MESH_ID)

        def wo_copy(a, k, block, to, src=None):
            dst = wo_out[a].at[_slot(block)]
            return pltpu.make_async_remote_copy(
                src_ref=dst if src is None else src, dst_ref=dst,
                send_sem=wo_send.at[7 * a + k], recv_sem=wo_recv.at[7 * a + k], device_id=to, device_id_type=MESH_ID)

        def own_copy():
            return pltpu.make_async_copy(stage, window((x, y), c), own_sems.at[0])

        def wo_own_copy(a):
            return pltpu.make_async_copy(wo_stage[a], wo_out[a].at[_slot(me)], wo_local.at[a])

        def pair_copy(step):
            chip = jnp.bitwise_xor(chip_id((x, y)), flip_at(step))
            return pltpu.make_async_copy(wv.at[chip], wg_ref.at[:, pl.ds(pl.multiple_of(chip * pw, LANE), pw)],
                                         pair_sems.at[step])

        first = jnp.logical_and(st == 0, i == 0)

        @pl.when(first)
        def _():
            stage[...] = win_ref[...].astype(BF16)
            own_copy().start()
            copy(0, me, sibling, src=stage).start()
            for j in range(2):
                copy(1 + j, me, (*chips[j], c), src=stage).start()
            own_copy().wait()
            copy(0, sibling, me).wait_recv()
            pair_copy(0).start()

        for s_ in range(n_chip - 1):
            @pl.when(jnp.logical_and(st == s_, i == mid))
            def _():
                copy(1 + s_, (*chips[s_], c), me).wait_recv()
                copy(4 + s_, (*chips[s_], c), sibling).start()
                if s_ == 0:
                    copy(3, me, (*chips[2], c), src=stage).start()
                if s_ == 1:
                    for a in range(no):
                        wo_stage[a][...] = wo_in[a][...].astype(BF16)
                        wo_own_copy(a).start()
                        wo_copy(a, 0, me, sibling, src=wo_stage[a]).start()
                        for j, chip in enumerate(chips):
                            wo_copy(a, 1 + j, me, (*chip, c), src=wo_stage[a]).start()
                if s_ == 2:
                    for a in range(no):
                        for j, chip in enumerate(chips):
                            wo_copy(a, 1 + j, (*chip, c), me).wait_recv()
                            wo_copy(a, 4 + j, (*chip, c), sibling).start()

        for s_ in range(1, n_chip):
            @pl.when(jnp.logical_and(st == s_, i == 0))
            def _():
                copy(3 + s_, (*chips[s_ - 1], 1 - c), me).wait_recv()
                pair_copy(s_).start()

        chip_now = jnp.bitwise_xor(chip_id((x, y)), flip_at(st))
        nq = 4
        tq = tm // nq

        def norm_rows(q):
            rs = slice(q * tq, (q + 1) * tq)
            xv = x_ref[rs, :]
            h_s[rs, :] = (xv * _rms_scale(xv) * g_ref[...]).astype(BF16)

        norm_rows(0)
        norm_rows(1)
        for q in range(nq):
            rs = slice(q * tq, (q + 1) * tq)
            proj_ref[rs, :] = _dot(h_s[rs, :], wv[chip_now]).astype(BF16)
            if q + 2 < nq:
                norm_rows(q + 2)

        @pl.when(st == 0)
        def _():
            h_ref[...] = h_s[...]

        @pl.when(jnp.logical_and(st == n_chip - 1, i == n_i - 1))
        def _():
            copy(0, me, sibling, src=stage).wait_send()
            for j, chip in enumerate(chips):
                copy(1 + j, me, (*chip, c), src=stage).wait_send()
                copy(4 + j, (*chip, c), sibling).wait_send()
            for s_ in range(n_chip):
                pair_copy(s_).wait()
            for a in range(no):
                wo_copy(a, 0, me, sibling, src=wo_stage[a]).wait_send()
                wo_copy(a, 0, sibling, me).wait_recv()
                for j, chip in enumerate(chips):
                    wo_copy(a, 1 + j, me, (*chip, c), src=wo_stage[a]).wait_send()
                    wo_copy(a, 4 + j, (*chip, c), sibling).wait_send()
                    wo_copy(a, 4 + j, (*chip, 1 - c), me).wait_recv()
                wo_own_copy(a).wait()

    any_spec = pl.BlockSpec(memory_space=pl.ANY)
    vmem = pl.BlockSpec(memory_space=pltpu.VMEM)
    grid_spec = pltpu.PrefetchScalarGridSpec(
        num_scalar_prefetch=1, grid=(n_chip, n_i),
        in_specs=[pl.BlockSpec((tm, d), lambda st, i, me: (i, 0)),
                  pl.BlockSpec((1, d), lambda st, i, me: (0, 0)), vmem] + [vmem] * no,
        out_specs=[pl.BlockSpec((tm, pw), lambda st, i, me: (i, jnp.bitwise_xor(me[0] // 2, flip_at(st)))),
                   pl.BlockSpec((tm, d), lambda st, i, me: (jnp.where(st == 0, i, n_i - 1), 0)),
                   any_spec] + [any_spec] * no,
        scratch_shapes=[pltpu.VMEM((n_chip, d, pw), BF16), pltpu.VMEM((d, esh), BF16), pltpu.VMEM((tm, d), BF16)] + [
            pltpu.VMEM(s.shape, BF16) for s in wo_shards] + [
            pltpu.SemaphoreType.DMA((7,)), pltpu.SemaphoreType.DMA((7,)),
            pltpu.SemaphoreType.DMA((n_chip,)), pltpu.SemaphoreType.DMA((1,)),
            pltpu.SemaphoreType.DMA((7 * no,)), pltpu.SemaphoreType.DMA((7 * no,)),
            pltpu.SemaphoreType.DMA((no,))])
    return pl.pallas_call(
        body, name="gather_in_proj", grid_spec=grid_spec,
        out_shape=[SDS((n, n_chip * pw), BF16), SDS((n, d), BF16), SDS((d, n_chip * pw), BF16)] + [
            SDS((N_DEV,) + s.shape, BF16) for s in wo_shards],
        compiler_params=pltpu.CompilerParams(dimension_semantics=("arbitrary", "arbitrary"),
                                             vmem_limit_bytes=VMEM_LIMIT),
    )(my_slot, x2d, norm_in, w_in_sh, *wo_shards)


N_CHIP = N_DEV // 2
CHIP_FLIPS = (3, 2, 1, 0)


def _owner_at(mine, j):
    flip = 0
    for pair, f in enumerate(CHIP_FLIPS):
        flip = jnp.where(j // 2 == pair, f, flip)
    return 2 * jnp.bitwise_xor(mine // 2, flip) + j % 2


def _dw_in_exchange(h, dproj, my_slot, packed):
    n, d = h.shape
    esh = dproj.shape[1] // N_DEV
    tk = _tile(n, 2048)
    nk = n // tk
    last_j = N_DEV - 1

    def body(me_ref, h_ref, dp_ref, pk_in, win_out, pk_out,
             acc, halfbuf, recvbuf, sendbuf, half_send, half_recv, win_send, win_recv,
             send_sems, recv_sems, local_sems):
        del me_ref
        j, k = pl.program_id(0), pl.program_id(1)
        x, y, c = _me()
        me, sibling = (x, y, c), (x, y, 1 - c)
        mine = _slot(me)
        my_chip = mine // 2

        def pack_copies():
            local = pltpu.make_async_copy(pk_in, pk_out.at[mine], local_sems.at[0])
            remote = [pltpu.make_async_remote_copy(
                src_ref=pk_in, dst_ref=pk_out.at[mine], send_sem=send_sems.at[kk - 1], recv_sem=recv_sems.at[kk - 1],
                device_id=_peer(me, kk), device_id_type=MESH_ID) for kk in range(1, N_DEV)]
            return local, remote

        def half_copy(jj):
            slot = (jj // 2) % 2
            return pltpu.make_async_remote_copy(
                src_ref=halfbuf.at[slot], dst_ref=recvbuf.at[slot],
                send_sem=half_send.at[slot], recv_sem=half_recv.at[slot],
                device_id=sibling, device_id_type=MESH_ID)

        def chip_copy(jj):
            slot = (jj // 2) % 2
            owner = _owner_at(mine, jj)
            return pltpu.make_async_remote_copy(
                src_ref=sendbuf.at[slot], dst_ref=win_out.at[my_chip],
                send_sem=win_send.at[slot], recv_sem=win_recv.at[my_chip],
                device_id=(owner // 4, (owner // 2) % 2, owner % 2), device_id_type=MESH_ID)

        def own_copy():
            return pltpu.make_async_copy(sendbuf.at[(last_j // 2) % 2], win_out.at[my_chip], local_sems.at[1])

        @pl.when(jnp.logical_and(j == 0, k == 0))
        def _():
            local, remote = pack_copies()
            for cp in [local] + remote:
                cp.start()

        @pl.when(k == 0)
        def _():
            acc[...] = jnp.zeros_like(acc)

        acc[...] += _dot_tn(dp_ref[...], h_ref[...])

        done = k == nk - 1
        combine = j % 2 == c
        slot = (j // 2) % 2

        @pl.when(jnp.logical_and(done, jnp.logical_not(combine)))
        def _():
            @pl.when(j >= 4)
            def _():
                half_copy(j - 4).wait_send()

            halfbuf[slot] = acc[...].astype(BF16)
            half_copy(j).start()

        @pl.when(jnp.logical_and(done, combine))
        def _():
            half_copy(j).wait_recv()

            @pl.when(j >= 4)
            def _():
                chip_copy(j - 4).wait_send()

            sendbuf[slot] = (acc[...] + recvbuf[slot].astype(F32)).astype(BF16)

            @pl.when(j < last_j - 1)
            def _():
                chip_copy(j).start()

            @pl.when(j >= last_j - 1)
            def _():
                own_copy().start()

        @pl.when(jnp.logical_and(j == last_j, done))
        def _():
            half_copy(5 - c).wait_send()
            half_copy(7 - c).wait_send()
            chip_copy(4 + c).wait_send()
            own_copy().wait()
            for chip in range(N_CHIP):
                @pl.when(chip != my_chip)
                def _():
                    landed = win_out.at[chip]
                    pltpu.make_async_remote_copy(
                        src_ref=landed, dst_ref=landed, send_sem=win_send.at[0], recv_sem=win_recv.at[chip],
                        device_id=me, device_id_type=MESH_ID).wait_recv()
            local, remote = pack_copies()
            for cp in remote:
                cp.wait_send()
            for kk in range(1, N_DEV):
                landed = pk_out.at[_slot(_peer(me, kk))]
                pltpu.make_async_remote_copy(
                    src_ref=landed, dst_ref=landed, send_sem=send_sems.at[kk - 1], recv_sem=recv_sems.at[kk - 1],
                    device_id=me, device_id_type=MESH_ID).wait_recv()
            local.wait()

    any_spec = pl.BlockSpec(memory_space=pl.ANY)
    grid_spec = pltpu.PrefetchScalarGridSpec(
        num_scalar_prefetch=1, grid=(N_DEV, nk),
        in_specs=[pl.BlockSpec((tk, d), lambda j, k, me: (k, 0)),
                  pl.BlockSpec((tk, esh), lambda j, k, me: (k, _owner_at(me[0], j))), any_spec],
        out_specs=[any_spec] * 2,
        scratch_shapes=[pltpu.VMEM((esh, d), F32)] + [pltpu.VMEM((2, esh, d), BF16)] * 3 + [
            pltpu.SemaphoreType.DMA((2,)), pltpu.SemaphoreType.DMA((2,)),
            pltpu.SemaphoreType.DMA((2,)), pltpu.SemaphoreType.DMA((N_CHIP,)),
            pltpu.SemaphoreType.DMA((N_DEV - 1,)), pltpu.SemaphoreType.DMA((N_DEV - 1,)),
            pltpu.SemaphoreType.DMA((2,))])
    return pl.pallas_call(
        body, name="dw_in_exchange", grid_spec=grid_spec,
        out_shape=[SDS((N_CHIP, esh, d), BF16), SDS((N_DEV,) + packed.shape, packed.dtype)],
        compiler_params=_params(("arbitrary", "arbitrary")),
    )(my_slot, h, dproj, packed)


def _finish_small(packs, late_packs, states, groups, chunk):
    gc = groups * chunk
    nw = len(states)

    def body(p_ref, l_ref, *refs):
        st = refs[:3 * nw]
        loss_ref = refs[3 * nw]
        outs = refs[3 * nw + 1:]
        row, col = _iotas(chunk)
        tril = col <= row

        def total(ref, rs):
            tot = ref[0, rs, :]
            for dev in range(1, N_DEV):
                tot = tot + ref[dev, rs, :]
            return tot

        def update(k, rs_w, g):
            w_ref, m_ref, v_ref = st[3 * k:3 * k + 3]
            _adamw_outputs(*[o.at[rs_w] for o in outs[4 * k:4 * k + 4]], g, w_ref[rs_w, :], m_ref[rs_w, :], v_ref[rs_w, :])

        for g in range(groups):
            rs = slice(g * chunk, (g + 1) * chunk)
            update(0, rs, jnp.where(tril, total(p_ref, rs), 0.0))
        slab = lambda k: slice(gc + k * SUBLANE, gc + (k + 1) * SUBLANE)
        for k in range(3):
            update(1 + k, slice(0, SUBLANE), total(p_ref, slab(k)))
        loss_ref[...] = jnp.full((SUBLANE, LANE), jnp.sum(total(p_ref, slab(3))), F32)
        update(4, slice(0, SUBLANE), total(l_ref, slice(0, SUBLANE)))

    flat = [a for s in states for a in s]
    vmem = pl.BlockSpec(memory_space=pltpu.VMEM)
    res = pl.pallas_call(
        body, name="finish_small",
        out_shape=[SDS((SUBLANE, LANE), F32)] + [SDS(s[0].shape, F32) for s in states for _ in range(4)],
        in_specs=[vmem] * (2 + len(flat)),
        out_specs=[vmem] * (1 + 4 * nw),
        compiler_params=pltpu.CompilerParams(vmem_limit_bytes=VMEM_LIMIT),
    )(packs, late_packs, *flat)
    return res[0], [res[1 + 4 * k:5 + 4 * k] for k in range(nw)]


def _branch_a_fwd(proj, norm_v, w_s, b_col):
    n = proj.shape[0]
    d = norm_v.shape[1]
    groups, chunk, _ = w_s.shape
    tr = _tile(n, 8 * chunk)

    def body(u_ref, v_ref, z_ref, gv_ref, ws_ref, b_ref, ya_ref, vn_s, pre_s):
        row, col = _iotas(chunk)
        tril = col <= row
        vg = _gelu(v_ref[...])[0].astype(F32)
        vn_s[...] = (vg * _rms_scale(vg) * gv_ref[...]).astype(BF16)
        pre_s[...] = _gelu(u_ref[...])[0] * _silu(z_ref[...])[0]
        for g in range(groups):
            wm = jnp.where(tril, ws_ref[g], 0.0).astype(BF16)
            cs = slice(g * chunk, (g + 1) * chunk)
            for c in range(tr // chunk):
                rs = slice(c * chunk, (c + 1) * chunk)
                mixed = _dot(wm, vn_s[rs, cs]) + b_ref[g]
                ya_ref[rs, cs] = (pre_s[rs, cs].astype(F32) * mixed).astype(BF16)

    seg = lambda k: pl.BlockSpec((tr, d), lambda i: (i, k))
    return pl.pallas_call(
        body, name="branch_a_fwd", grid=(n // tr,),
        in_specs=[seg(0), seg(1), seg(2),
                  pl.BlockSpec((1, d), lambda i: (0, 0)),
                  pl.BlockSpec((groups, chunk, chunk), lambda i: (0, 0, 0)),
                  pl.BlockSpec((groups, chunk, 1), lambda i: (0, 0, 0))],
        out_specs=pl.BlockSpec((tr, d), lambda i: (i, 0)),
        out_shape=SDS((n, d), BF16),
        scratch_shapes=[pltpu.VMEM((tr, d), BF16), pltpu.VMEM((tr, d), BF16)],
        compiler_params=_params(("parallel",)),
    )(proj, proj, proj, norm_v, w_s, b_col)


def _sb_fwd(proj, batch, seq, d, hd):
    heads = d // hd
    t = _tile(seq, SB_TILE)
    sw = _tile(t, SB_SCAN)
    nb = t // sw
    scale = hd ** -0.5
    nblk = seq // t
    nh = SB_HEADS
    wide = nh * hd
    cols = [slice(hh * hd, (hh + 1) * hd) for hh in range(nh)]

    def body(qs, k_ref, vs, zb_ref, yb_ref, o_ref, tot_ref, kts, later, acc):
        for jb in range(nblk):
            kts[jb] = k_ref[jb * t:(jb + 1) * t, :].T
        row, col = _iotas(t)
        later[...] = (row[:sw, :sw] > col[:sw, :sw]).astype(BF16)

        def qblock(i, carry):
            r0 = pl.multiple_of(i * t, t)

            def tile(j, runs):
                c0 = pl.multiple_of(j * t, t)
                logs = [_sb_logs(_dot(qs[pl.ds(r0, t), cs], kts[j, cs, :]), scale, None) for cs in cols]
                scans = [_dot(jnp.concatenate([logs[hh][1][:, b * sw:(b + 1) * sw] for b in range(nb)], axis=0),
                              later[...]) for hh in range(nh)]
                new_runs = []
                for hh in range(nh):
                    after = runs[hh]
                    blocks = [None] * nb
                    for b in reversed(range(nb)):
                        ks_ = slice(b * sw, (b + 1) * sw)
                        inside = scans[hh][b * t:(b + 1) * t]
                        blocks[b] = jnp.exp(logs[hh][0][:, ks_].astype(F32) + inside + after).astype(BF16)
                        after = after + inside[:, 0:1] + logs[hh][1][:, b * sw:b * sw + 1].astype(F32)
                    new_runs.append(after)
                    acc[:, cols[hh]] += _dot(jnp.concatenate(blocks, axis=1), vs[pl.ds(c0, t), cols[hh]])
                return tuple(new_runs)

            def diagonal_tile():
                starts = [b * sw for b in range(nb)]
                logs = [[_sb_logs(_dot(qs[pl.ds(r0 + s, t - s), cs], kts[i, cs, s:s + sw]), scale,
                                  col[:t - s, :sw] < row[:t - s, :sw]) for s in starts] for cs in cols]
                scans = [_dot(jnp.concatenate([lr for _, lr in logs[hh]], axis=0), later[...]) for hh in range(nh)]
                new_runs = []
                offs = [sum(t - s for s in starts[:b]) for b in range(nb)]
                for hh in range(nh):
                    after = jnp.zeros((t, 1), F32)
                    ws = [None] * nb
                    for b in reversed(range(nb)):
                        s = starts[b]
                        lb, lr = logs[hh][b]
                        inside = scans[hh][offs[b]:offs[b] + t - s]
                        ws[b] = jnp.exp(lb.astype(F32) + inside + after[s:]).astype(BF16)
                        total = inside[:, 0:1] + lr[:, 0:1].astype(F32)
                        after = after + total if s == 0 else jnp.concatenate([after[:s], after[s:] + total], axis=0)
                    new_runs.append(after)
                    acc[:, cols[hh]] = _dot(ws[0], vs[pl.ds(r0, sw), cols[hh]])
                    for b in range(1, nb):
                        acc[starts[b]:, cols[hh]] += _dot(ws[b], vs[pl.ds(r0 + starts[b], sw), cols[hh]])
                return tuple(new_runs)

            runs = diagonal_tile()
            runs = lax.fori_loop(0, i, lambda jj, rs: tile(i - 1 - jj, rs), runs)
            for hh in range(nh):
                out = acc[:, cols[hh]]
                o_ref[pl.ds(r0, t), cols[hh]] = out.astype(BF16)
                tot_ref[hh, pl.ds(r0, t), :] = runs[hh]
                sz, _ = _silu(zb_ref[pl.ds(r0, t), cols[hh]].astype(F32))
                yb_ref[pl.ds(r0, t), cols[hh]] = (out * sz).astype(BF16)
            return carry

        lax.fori_loop(0, nblk, qblock, 0)

    col0 = d // wide
    seg = lambda k: pl.BlockSpec((seq, wide), lambda b, h: (b, k * col0 + h))
    return pl.pallas_call(
        body, name="sb_fwd", grid=(batch, heads // nh),
        in_specs=[seg(3), seg(4), seg(5), seg(6)],
        out_specs=[pl.BlockSpec((seq, wide), lambda b, h: (b, h))] * 2 + [
            pl.BlockSpec((nh, seq, 1), lambda b, h: (b * (heads // nh) + h, 0, 0))],
        out_shape=[SDS((batch * seq, d), BF16), SDS((batch * seq, d), BF16), SDS((batch * heads, seq, 1), F32)],
        scratch_shapes=[pltpu.VMEM((nblk, wide, t), BF16), pltpu.VMEM((sw, sw), BF16), pltpu.VMEM((t, wide), F32)],
        compiler_params=_params(("parallel", "parallel")),
    )(proj, proj, proj, proj)


def _tail(x2d, tgt, ya, yb, proj, w_oa, w_ob, w_out, norm_final):
    n, d = x2d.shape
    e = proj.shape[1]
    tm = _tile(n, 512)
    steps = n // tm

    def body(x_ref, t_ref, ya_ref, yb_ref, ga_ref, gb_ref, woa_ref, wob_ref, wout_ref, gf_ref,
             dproj_ref, dx2_ref, dya_ref, dyb_ref, mrg_ref, dpa_ref, dpb_ref, loss_ref, dgf_ref, dg_s, dg_sems):
        i = pl.program_id(0)

        def gate_copy(step):
            rows_ = pl.ds(pl.multiple_of(step * tm, tm), tm)
            return pltpu.make_async_copy(dg_s.at[step % 2], dproj_ref.at[rows_, pl.ds(7 * d, 2 * d)],
                                         dg_sems.at[step % 2])

        @pl.when(i == 0)
        def _():
            loss_ref[...] = jnp.zeros_like(loss_ref)
            dgf_ref[...] = jnp.zeros_like(dgf_ref)

        @pl.when(i >= 2)
        def _():
            gate_copy(i - 2).wait()

        halves = [slice(hf * (tm // 2), (hf + 1) * (tm // 2)) for hf in range(2)] if tm >= 512 else [slice(0, tm)]
        gf = gf_ref[...]
        pa = [_dot(ya_ref[rs, :], woa_ref[...]) for rs in halves]
        pb = [_dot(yb_ref[rs, :], wob_ref[...]) for rs in halves]
        sa = [_sigmoid(ga_ref[rs, :].astype(F32)) for rs in halves]
        sb = [_sigmoid(gb_ref[rs, :].astype(F32)) for rs in halves]
        merged = [(sa[k] * pa[k] + sb[k] * pb[k]).astype(BF16) for k in range(len(halves))]
        for k, rs in enumerate(halves):
            mrg_ref[rs, :] = merged[k]
        x2 = [x_ref[rs, :] + _dot(merged[k], wout_ref[...]) for k, rs in enumerate(halves)]
        dx2 = []
        for k, rs in enumerate(halves):
            r2 = _rms_scale(x2[k])
            xh = x2[k] * r2
            diff = xh * gf - t_ref[rs, :]
            loss_ref[...] += jnp.sum(diff * diff, axis=0, keepdims=True) * (0.5 / d)
            dy = diff * (1.0 / d)
            dgf_ref[...] += jnp.sum(dy * xh, axis=0, keepdims=True)
            dxh = dy * gf
            dx2.append(r2 * (dxh - xh * jnp.mean(dxh * xh, axis=-1, keepdims=True)))
            dx2_ref[rs, :] = dx2[k]
        dm = [_dot_nt(dx2[k].astype(BF16), wout_ref[...]) for k in range(len(halves))]
        dpa, dpb = [], []
        for k, rs in enumerate(halves):
            dpa.append((dm[k] * sa[k]).astype(BF16))
            dpb.append((dm[k] * sb[k]).astype(BF16))
            dpa_ref[rs, :] = dpa[k]
            dpb_ref[rs, :] = dpb[k]
            dg_s[i % 2, rs, 0:d] = (dm[k] * pa[k] * (sa[k] * (1.0 - sa[k]))).astype(BF16)
            dg_s[i % 2, rs, d:2 * d] = (dm[k] * pb[k] * (sb[k] * (1.0 - sb[k]))).astype(BF16)
        gate_copy(i).start()
        for k, rs in enumerate(halves):
            dya_ref[rs, :] = _dot_nt(dpa[k], woa_ref[...]).astype(BF16)
        for k, rs in enumerate(halves):
            dyb_ref[rs, :] = _dot_nt(dpb[k], wob_ref[...]).astype(BF16)

        @pl.when(i == steps - 1)
        def _():
            if steps >= 2:
                gate_copy(i - 1).wait()
            gate_copy(i).wait()

    rows = lambda k=0: pl.BlockSpec((tm, d), lambda i: (i, k))
    full = pl.BlockSpec((d, d), lambda i: (0, 0), pipeline_mode=pl.Buffered(1))
    vec = pl.BlockSpec((1, d), lambda i: (0, 0))
    return pl.pallas_call(
        body, name="tail", grid=(steps,),
        in_specs=[rows(), rows(), rows(), rows(), rows(7), rows(8), full, full, full, vec],
        out_specs=[pl.BlockSpec(memory_space=pl.ANY),
                   rows(), rows(), rows(), rows(), rows(), rows(), vec, vec],
        out_shape=[SDS((n, e), BF16), SDS((n, d), F32), SDS((n, d), BF16), SDS((n, d), BF16),
                   SDS((n, d), BF16), SDS((n, d), BF16), SDS((n, d), BF16),
                   SDS((1, d), F32), SDS((1, d), F32)],
        scratch_shapes=[pltpu.VMEM((2, tm, 2 * d), BF16), pltpu.SemaphoreType.DMA((2,))],
        compiler_params=_params(("arbitrary",)),
    )(x2d, tgt, ya, yb, proj, proj, w_oa, w_ob, w_out, norm_final)


def _dw_o(pairs):
    n, d = pairs[0][0].shape
    tk = _tile(n, 1024)
    nk = n // tk
    npair = len(pairs)

    def body(*refs):
        a_refs, b_refs = refs[:npair], refs[npair:2 * npair]
        o_ref, acc = refs[2 * npair], refs[2 * npair + 1]
        p, k = pl.program_id(0), pl.program_id(1)

        @pl.when(k == 0)
        def _():
            acc[...] = jnp.zeros_like(acc)

        for q in range(npair):
            @pl.when(p == q)
            def _():
                acc[...] += _dot_tn(a_refs[q][...], b_refs[q][...].astype(BF16))

        @pl.when(k == nk - 1)
        def _():
            o_ref[0] = acc[...].astype(BF16)

    def tiles(q):
        return pl.BlockSpec((tk, d), lambda p, k: (jnp.where(p == q, k, jnp.where(p < q, 0, nk - 1)), 0))

    return pl.pallas_call(
        body, name="dw_o", grid=(npair, nk),
        in_specs=[tiles(q) for q in range(npair)] * 2,
        out_specs=pl.BlockSpec((1, d, d), lambda p, k: (p, 0, 0)),
        out_shape=SDS((npair, d, d), BF16),
        scratch_shapes=[pltpu.VMEM((d, d), F32)],
        compiler_params=_params(("arbitrary", "arbitrary")),
    )(*[a for a, _ in pairs], *[b for _, b in pairs])


def _sb_bwd(proj, o, dyb, tot, dproj, dw_stack, packed, batch, seq, d, hd):
    heads = d // hd
    t = _tile(seq, SB_TILE_BWD)
    sw = _tile(t, SB_SCAN)
    nb = t // sw
    scale = hd ** -0.5
    nblk = seq // t
    nh = SB_HEADS
    wide = nh * hd
    hs = range(nh)
    cols = [slice(hh * hd, (hh + 1) * hd) for hh in hs]
    blocks = [slice(b * sw, (b + 1) * sw) for b in range(nb)]
    last = slice(sw - 1, sw)

    def compute(qs, ks, v_ref, zb_ref, o_ref, dyb_ref, tot_ref, kts, vts, dos, dzb, dq_all, dkv_t, qt_s, dot_s,
                upto, before, dq):
        for jb in range(nblk):
            rows = slice(jb * t, (jb + 1) * t)
            kts[jb] = ks[rows, :].T
            vts[jb] = v_ref[rows, :].T
        sz, dsz = _silu(zb_ref[...])
        dyb_v = dyb_ref[...]
        dos[...] = dyb_v * sz
        dzb[...] = dyb_v * o_ref[...] * dsz
        row, col = _iotas(t)
        upto[...] = (row[:sw, :sw] <= col[:sw, :sw]).astype(BF16)
        before[...] = (row[:sw, :sw] < col[:sw, :sw]).astype(BF16)

        def qblock(i, carry):
            r0 = pl.multiple_of(i * t, t)

            def tile(j, sums):
                c0 = pl.multiple_of(j * t, t)
                q_i = [qs[pl.ds(r0, t), cs] for cs in cols]
                do_i = [dos[pl.ds(r0, t), cs] for cs in cols]
                logs = [_sb_logs(_dot(q_i[hh], kts[j, cols[hh], :]), scale, None) for hh in hs]
                scans = [_dot(jnp.concatenate([logs[hh][1][:, ks_] for ks_ in blocks], axis=0), upto[...]) for hh in hs]
                dw = [_dot(do_i[hh], vts[j, cols[hh], :]) for hh in hs]
                ws, gs, new_runs = [], [], []
                for hh in hs:
                    left = tot_ref[hh, pl.ds(r0, t), :] - sums[hh][0]
                    w_b, g_b = [], []
                    for b, ks_ in enumerate(blocks):
                        inside = scans[hh][b * t:(b + 1) * t]
                        w = jnp.exp(logs[hh][0][:, ks_].astype(F32) + (left - inside))
                        w_b.append(w.astype(BF16))
                        g_b.append((dw[hh][:, ks_] * w).astype(BF16))
                        left = left - inside[:, last]
                    ws.append(jnp.concatenate(w_b, axis=1))
                    gs.append(g_b)
                    new_runs.append(tot_ref[hh, pl.ds(r0, t), :] - left)
                gscans = [_dot(jnp.concatenate(gs[hh], axis=0), before[...]) for hh in hs]
                dzs, new_gruns = [], []
                for hh in hs:
                    g_before = sums[hh][1]
                    dz_b = []
                    for b, ks_ in enumerate(blocks):
                        inside = gscans[hh][b * t:(b + 1) * t]
                        beta = jnp.exp(logs[hh][0][:, ks_]).astype(F32)
                        g = gs[hh][b].astype(F32)
                        dz_b.append((g - (g + inside + g_before) * beta).astype(BF16))
                        g_before = g_before + inside[:, last] + g[:, last]
                    dzs.append(jnp.concatenate(dz_b, axis=1))
                    new_gruns.append(g_before)
                for hh in hs:
                    dkv_t[1, j, cols[hh], :] += _dot(dot_s[cols[hh], :], ws[hh])
                for hh in hs:
                    dkv_t[0, j, cols[hh], :] += _dot(qt_s[cols[hh], :], dzs[hh])
                for hh in hs:
                    dq[:, cols[hh]] += _dot(dzs[hh], ks[pl.ds(c0, t), cols[hh]])
                return tuple((new_runs[hh], new_gruns[hh]) for hh in hs)

            def diagonal_tile(sums):
                starts = [b * sw for b in range(nb)]
                offs = [sum(t - s for s in starts[:b]) for b in range(nb)]
                q_b = [[qs[pl.ds(r0 + s, t - s), cs] for s in starts] for cs in cols]
                do_b = [[dos[pl.ds(r0 + s, t - s), cs] for s in starts] for cs in cols]
                logs = [[_sb_logs(_dot(q_b[hh][b], kts[i, cols[hh], s:s + sw]), scale,
                                  col[:t - s, :sw] < row[:t - s, :sw]) for b, s in enumerate(starts)] for hh in hs]
                dw = [[_dot(do_b[hh][b], vts[i, cols[hh], s:s + sw]) for b, s in enumerate(starts)] for hh in hs]
                scans = [_dot(jnp.concatenate([lr for _, lr in logs[hh]], axis=0), upto[...]) for hh in hs]
                ws, gs = [], []
                for hh in hs:
                    left = tot_ref[hh, pl.ds(r0, t), :] - sums[hh][0]
                    w_b, g_b = [], []
                    for b, s in enumerate(starts):
                        inside = scans[hh][offs[b]:offs[b] + t - s]
                        w = jnp.exp(logs[hh][b][0].astype(F32) + (left[s:] - inside))
                        w_b.append(w.astype(BF16))
                        g_b.append((dw[hh][b] * w).astype(BF16))
                        total = inside[:, last]
                        left = left - total if s == 0 else jnp.concatenate([left[:s], left[s:] - total], axis=0)
                    ws.append(w_b)
                    gs.append(g_b)
                gscans = [_dot(jnp.concatenate(gs[hh], axis=0), before[...]) for hh in hs]
                dzs = []
                for hh in hs:
                    g_before = sums[hh][1]
                    dz_b = []
                    for b, s in enumerate(starts):
                        inside = gscans[hh][offs[b]:offs[b] + t - s]
                        beta = jnp.exp(logs[hh][b][0]).astype(F32)
                        g = gs[hh][b].astype(F32)
                        dz_b.append((g - (g + inside + g_before[s:]) * beta).astype(BF16))
                        total = inside[:, last] + g[:, last]
                        g_before = g_before + total if s == 0 else jnp.concatenate(
                            [g_before[:s], g_before[s:] + total], axis=0)
                    dzs.append(dz_b)
                for hh in hs:
                    for b, s in enumerate(starts):
                        dkv_t[1, i, cols[hh], s:s + sw] = _dot(dot_s[cols[hh], s:], ws[hh][b])
                for hh in hs:
                    for b, s in enumerate(starts):
                        dkv_t[0, i, cols[hh], s:s + sw] = _dot(qt_s[cols[hh], s:], dzs[hh][b])
                for hh in hs:
                    for b, s in enumerate(starts):
                        dq[s:, cols[hh]] += _dot(dzs[hh][b], ks[pl.ds(r0 + s, sw), cols[hh]])

            qt_s[...] = qs[pl.ds(r0, t), :].T
            dot_s[...] = dos[pl.ds(r0, t), :].T
            zero = jnp.zeros((t, 1), F32)
            dq[...] = jnp.zeros_like(dq)
            sums = lax.fori_loop(0, i, tile, ((zero, zero),) * nh)
            diagonal_tile(sums)
            dq_all[pl.ds(r0, t), :] = dq[...]
            return carry

        lax.fori_loop(0, nblk, qblock, 0)

    pairs = heads // nh

    nst = dw_stack.shape[0]
    ns = nst + 1

    def body(qs, ks, v_ref, zb_ref, o_ref, dyb_ref, tot_ref, dproj_in, dw_ref, pk_ref, out_ref, *refs):
        del dproj_in
        st_in = [dw_ref.at[k] for k in range(nst)] + [pk_ref]
        st_out = refs[:ns]
        (kts, vts, dos, dzb, dq_all, dkv_t, qt_s, dot_s, upto, before, dq, stage, stage_sems,
         send_sems, recv_sems, local_sems) = refs[ns:]
        step = pl.program_id(0) * pairs + pl.program_id(1)
        exchange = functools.partial(_stack_exchange, _me(), st_in, st_out, 1, send_sems, recv_sems, local_sems)

        @pl.when(step == 0)
        def _():
            local, remote, _ = exchange(arrivals=False)
            for cp in local + remote:
                cp.start()

        def out_copies(s):
            rows_ = pl.ds(pl.multiple_of((s // pairs) * seq, seq), seq)
            return [pltpu.make_async_copy(
                stage.at[k], out_ref.at[rows_, pl.ds(pl.multiple_of((3 + k) * d + (s % pairs) * wide, wide), wide)],
                stage_sems.at[k]) for k in range(4)]

        compute(qs, ks, v_ref, zb_ref, o_ref, dyb_ref, tot_ref, kts, vts, dos, dzb, dq_all, dkv_t, qt_s, dot_s,
                upto, before, dq)

        @pl.when(step > 0)
        def _():
            for cp in out_copies(step - 1):
                cp.wait()

        stage[0] = (dq_all[...] * scale).astype(BF16)
        for jb in range(nblk):
            stage[1, jb * t:(jb + 1) * t, :] = (dkv_t[0, jb] * scale).astype(BF16).T
            stage[2, jb * t:(jb + 1) * t, :] = dkv_t[1, jb].astype(BF16).T
        stage[3] = dzb[...]
        for cp in out_copies(step):
            cp.start()

        @pl.when(step == batch * pairs - 1)
        def _():
            for cp in out_copies(step):
                cp.wait()
            local, remote, landed = exchange()
            for cp in remote:
                cp.wait_send()
            for cp in landed:
                cp.wait_recv()
            for cp in local:
                cp.wait()

    col0 = d // wide
    seg = lambda k: pl.BlockSpec((seq, wide), lambda b, h: (b, k * col0 + h))
    head = pl.BlockSpec((seq, wide), lambda b, h: (b, h))
    any_spec = pl.BlockSpec(memory_space=pl.ANY)
    return pl.pallas_call(
        body, name="sb_bwd", grid=(batch, pairs),
        in_specs=[seg(3), seg(4), seg(5), seg(6), head, head,
                  pl.BlockSpec((nh, seq, 1), lambda b, h: (b * pairs + h, 0, 0))] + [any_spec] * 3,
        out_specs=[any_spec] * (ns + 1),
        out_shape=[SDS(dproj.shape, dproj.dtype)] + [SDS(dw_stack.shape[1:], dw_stack.dtype)] * nst + [
            SDS((N_DEV,) + packed.shape, packed.dtype)],
        input_output_aliases={7: 0},
        scratch_shapes=[pltpu.VMEM((nblk, wide, t), BF16)] * 2 + [
            pltpu.VMEM((seq, wide), BF16), pltpu.VMEM((seq, wide), BF16),
            pltpu.VMEM((seq, wide), F32), pltpu.VMEM((2, nblk, wide, t), F32),
            pltpu.VMEM((wide, t), BF16), pltpu.VMEM((wide, t), BF16),
            pltpu.VMEM((sw, sw), BF16), pltpu.VMEM((sw, sw), BF16), pltpu.VMEM((t, wide), F32),
            pltpu.VMEM((4, seq, wide), BF16), pltpu.SemaphoreType.DMA((4,)),
            pltpu.SemaphoreType.DMA((7 * ns,)), pltpu.SemaphoreType.DMA((7 * ns,)),
            pltpu.SemaphoreType.DMA((ns,))],
        compiler_params=_params(("arbitrary", "arbitrary")),
    )(proj, proj, proj, proj, o, dyb, tot, dproj, dw_stack, packed)


def _branch_a_bwd(proj, dya, norm_v, w_s, b_col, dproj):
    n = proj.shape[0]
    d = norm_v.shape[1]
    groups, chunk, _ = w_s.shape
    tr = _tile(n, 4 * chunk)

    def body(u_ref, v_ref, z_ref, dya_ref, gv_ref, ws_ref, b_ref, dproj_in,
             out_ref, dws_ref, dbias_ref, dgv_ref, vn_s, dmix_s, dvn_s, db_ref):
        del dproj_in

        @pl.when(pl.program_id(0) == 0)
        def _():
            dws_ref[...] = jnp.zeros_like(dws_ref)
            db_ref[...] = jnp.zeros_like(db_ref)
            dgv_ref[...] = jnp.zeros_like(dgv_ref)

        row, col = _iotas(chunk)
        tril = col <= row
        gv = gv_ref[...]
        vg16, dvg_dv = _gelu(v_ref[...])
        vg = vg16.astype(F32)
        r = _rms_scale(vg)
        vh = vg * r
        vn_s[...] = (vh * gv).astype(BF16)
        ug, dug_du = _gelu(u_ref[...])
        sz, dsz = _silu(z_ref[...])
        dya_v = dya_ref[...]
        dmix_s[...] = dya_v * ug * sz
        du_scale = sz * dug_du
        dz_scale = ug * dsz
        for g in range(groups):
            wm = jnp.where(tril, ws_ref[g], 0.0).astype(BF16)
            cs = slice(g * chunk, (g + 1) * chunk)
            for c in range(tr // chunk):
                rs = slice(c * chunk, (c + 1) * chunk)
                vn = vn_s[rs, cs]
                mixed = _dot(wm, vn) + b_ref[g]
                dmix16 = dmix_s[rs, cs]
                dws_ref[g] += _dot_nt(dmix16, vn)
                db_ref[g] += dmix16.astype(F32)
                dvn_s[rs, cs] = _dot_tn(wm, dmix16)
                t_u = dya_v[rs, cs] * mixed.astype(BF16)
                out_ref[rs, g * chunk:(g + 1) * chunk] = t_u * du_scale[rs, cs]
                out_ref[rs, 2 * d + g * chunk:2 * d + (g + 1) * chunk] = t_u * dz_scale[rs, cs]
        dvn = dvn_s[...]
        dgv_ref[...] += jnp.sum(dvn * vh, axis=0, keepdims=True)
        dvh = dvn * gv
        dvg = r * (dvh - vh * jnp.mean(dvh * vh, axis=-1, keepdims=True))
        out_ref[:, d:2 * d] = (dvg * dvg_dv.astype(F32)).astype(BF16)

        @pl.when(pl.program_id(0) == n // tr - 1)
        def _():
            for g in range(groups):
                dbias_ref[g:g + 1, :] = jnp.sum(db_ref[g].T, axis=0, keepdims=True)

    seg = lambda k: pl.BlockSpec((tr, d), lambda i: (i, k))
    return pl.pallas_call(
        body, name="branch_a_bwd", grid=(n // tr,),
        in_specs=[seg(0), seg(1), seg(2), seg(0),
                  pl.BlockSpec((1, d), lambda i: (0, 0)),
                  pl.BlockSpec((groups, chunk, chunk), lambda i: (0, 0, 0)),
                  pl.BlockSpec((groups, chunk, 1), lambda i: (0, 0, 0)),
                  pl.BlockSpec(memory_space=pl.ANY)],
        out_specs=[pl.BlockSpec((tr, 3 * d), lambda i: (i, 0)),
                   pl.BlockSpec((groups, chunk, chunk), lambda i: (0, 0, 0)),
                   pl.BlockSpec((groups, chunk), lambda i: (0, 0)),
                   pl.BlockSpec((1, d), lambda i: (0, 0))],
        out_shape=[SDS(dproj.shape, dproj.dtype), SDS((groups, chunk, chunk), F32),
                   SDS((groups, chunk), F32), SDS((1, d), F32)],
        input_output_aliases={7: 0},
        scratch_shapes=[pltpu.VMEM((tr, d), BF16), pltpu.VMEM((tr, d), BF16), pltpu.VMEM((tr, d), F32),
                        pltpu.VMEM((groups, chunk, chunk), F32)],
        compiler_params=_params(("arbitrary",)),
    )(proj, proj, proj, dya, norm_v, w_s, b_col, dproj)


def _dx(dproj, wg_in, x2d, dx2, norm_in, stacks, states):
    n, d = x2d.shape
    e = wg_in.shape[1]
    tm = _tile(n, 256)
    nw = len(stacks)
    ns, r_o, c_o = stacks[0].shape
    tr_o = _tile(r_o, 16)
    n_o = r_o // tr_o
    assert n_o <= n // tm

    def body(dp_ref, w_ref, x_ref, dx2_ref, g_ref, *rest):
        s_refs, st_refs = rest[:nw], rest[nw:4 * nw]
        gx_ref, dg_ref = rest[4 * nw:4 * nw + 2]
        out_refs = rest[4 * nw + 2:]
        i = pl.program_id(0)

        @pl.when(i == 0)
        def _():
            dg_ref[...] = jnp.zeros_like(dg_ref)

        dh = _dot_nt(dp_ref[...], w_ref[...])
        x = x_ref[...]
        r = _rms_scale(x)
        xh = x * r
        dg_ref[...] += jnp.sum(dh * xh, axis=0, keepdims=True)
        dxh = dh * g_ref[...]
        gx_ref[...] = dx2_ref[...] + r * (dxh - xh * jnp.mean(dxh * xh, axis=-1, keepdims=True))

        @pl.when(i < n_o)
        def _():
            for k in range(nw):
                g = s_refs[k][0].astype(F32)
                for j in range(1, ns):
                    g = g + s_refs[k][j].astype(F32)
                w_o, m_o, v_o = st_refs[3 * k:3 * k + 3]
                _adamw_outputs(*out_refs[4 * k:4 * k + 4], g, w_o[...], m_o[...], v_o[...])

    rows = pl.BlockSpec((tm, d), lambda i: (i, 0))
    vec = pl.BlockSpec((1, d), lambda i: (0, 0))
    o_blk = pl.BlockSpec((tr_o, c_o), lambda i: (jnp.minimum(i, n_o - 1), 0))
    o_slots = pl.BlockSpec((ns, tr_o, c_o), lambda i: (0, jnp.minimum(i, n_o - 1), 0))
    res = pl.pallas_call(
        body, name="dx", grid=(n // tm,),
        in_specs=[pl.BlockSpec((tm, e), lambda i: (i, 0)),
                  pl.BlockSpec((d, e), lambda i: (0, 0), pipeline_mode=pl.Buffered(1)), rows, rows, vec]
        + [o_slots] * nw + [o_blk] * (3 * nw),
        out_specs=[rows, vec] + [o_blk] * (4 * nw),
        out_shape=[SDS((n, d), F32), SDS((1, d), F32)] + [SDS((r_o, c_o), F32)] * (4 * nw),
        compiler_params=_params(("arbitrary",)),
    )(dproj, wg_in, x2d, dx2, norm_in, *stacks, *[a for st in states for a in st])
    return res[0], res[1], [res[2 + 4 * k:6 + 4 * k] for k in range(nw)]


def _adamw_outputs(g_ref, d_ref, m_ref, v_ref, g, w, m, v):
    delta, m2, v2 = _adamw(w, g, m, v)
    g_ref[...] = g
    d_ref[...] = delta
    m_ref[...] = m2
    v_ref[...] = v2


def _reduce_adamw(slots, w, m, v, name, transposed=False):
    r, c = w.shape
    ns = slots.shape[0]
    tr = _tile(r, 128)

    def body(s_ref, w_ref, m_ref, v_ref, g_out, d_out, m_out, v_out):
        g = s_ref[0].astype(F32)
        for k in range(1, ns):
            g = g + s_ref[k].astype(F32)
        if transposed:
            g = g.T
        _adamw_outputs(g_out, d_out, m_out, v_out, g, w_ref[...], m_ref[...], v_ref[...])

    blk = pl.BlockSpec((tr, c), lambda i: (i, 0))
    slot_blk = (pl.BlockSpec((ns, c, tr), lambda i: (0, 0, i)) if transposed
                else pl.BlockSpec((ns, tr, c), lambda i: (0, i, 0)))
    return pl.pallas_call(
        body, name=name, grid=(r // tr,),
        in_specs=[slot_blk, blk, blk, blk],
        out_specs=[blk] * 4,
        out_shape=[SDS((r, c), F32)] * 4,
        compiler_params=_params(("parallel",)),
    )(slots, w, m, v)


def kernel(x, norm_in, w_in, norm_v, w_s, b_s, w_o_gmlp, w_o_sb, w_out, norm_final, loss_target, m_norm_in, m_w_in, m_norm_v, m_w_s, m_b_s, m_w_o_gmlp, m_w_o_sb, m_w_out, m_norm_final, v_norm_in, v_w_in, v_norm_v, v_w_s, v_b_s, v_w_o_gmlp, v_w_o_sb, v_w_out, v_norm_final):
    batch, seq, d = x.shape
    n = batch * seq
    groups, chunk = w_s.shape[1], w_s.shape[2]
    hd = LANE
    x2d = x.reshape(n, d)
    tgt = loss_target.reshape(n, d)
    b_col = b_s[0].reshape(groups, chunk, 1)
    norm_final2 = norm_final.reshape(1, d)

    my_slot = _slot(_me()).astype(jnp.int32).reshape(1)
    proj, h, wg_in, wg_oa, wg_ob, wg_out = _gather_in_proj(
        x2d, norm_in, w_in[0], [w_o_gmlp[0], w_o_sb[0], w_out[0]], my_slot)
    rsh = wg_oa.shape[1]
    wf_oa, wf_ob, wf_out = (w.reshape(N_DEV * rsh, d) for w in (wg_oa, wg_ob, wg_out))
    ya = _branch_a_fwd(proj, norm_v, w_s[0], b_col)
    yb, o, sb_tot = _sb_fwd(proj, batch, seq, d, hd)
    dproj, dx2, dya, dyb, merged, dpa, dpb, loss_vec, dgf = _tail(
        x2d, tgt, ya, yb, proj, wf_oa, wf_ob, wf_out, norm_final2)
    gp_wo = _dw_o([(ya, dpa), (yb, dpb), (merged, dx2)])
    dproj, gp_ws, gp_b, gp_nv = _branch_a_bwd(proj, dya, norm_v, w_s[0], b_col, dproj)

    slab = lambda a: a.reshape(d // LANE, LANE)
    gc = groups * chunk
    packed = jnp.concatenate([gp_ws.reshape(gc, chunk), gp_b, slab(gp_nv), slab(dgf), slab(loss_vec)], axis=0)
    dproj, s_oa, s_ob, s_out, packs = _sb_bwd(
        proj, o, dyb, sb_tot, dproj, gp_wo.reshape(3, N_DEV, rsh, d), packed, batch, seq, d, hd)
    grad_x, gp_nin, wo_res = _dx(
        dproj, wg_in, x2d, dx2, norm_in, [s_oa, s_ob, s_out],
        [(w_o_gmlp[0], m_w_o_gmlp[0], v_w_o_gmlp[0]), (w_o_sb[0], m_w_o_sb[0], v_w_o_sb[0]),
         (w_out[0], m_w_out[0], v_w_out[0])])
    s_win, late_packs = _dw_in_exchange(h, dproj, my_slot, slab(gp_nin))
    small = {"w_s": lambda a: a.reshape(gc, chunk), "b_s": lambda a: a[0], "norm_v": slab, "norm_final": slab,
             "norm_in": slab}
    given = {"w_s": (w_s, m_w_s, v_w_s), "b_s": (b_s, m_b_s, v_b_s), "norm_v": (norm_v, m_norm_v, v_norm_v),
             "norm_final": (norm_final, m_norm_final, v_norm_final), "norm_in": (norm_in, m_norm_in, v_norm_in)}
    loss_slab, small_res = _finish_small(
        packs, late_packs, [tuple(small[k](a) for a in given[k]) for k in small], groups, chunk)
    loss = loss_slab[0, 0]

    res = dict(zip(small, small_res))
    res["w_in"] = _reduce_adamw(s_win, w_in[0], m_w_in[0], v_w_in[0], "adamw_w_in", transposed=True)
    res["w_o_gmlp"], res["w_o_sb"], res["w_out"] = wo_res

    shapes = {"norm_in": norm_in.shape, "w_in": w_in.shape, "norm_v": norm_v.shape, "w_s": w_s.shape,
              "b_s": b_s.shape, "w_o_gmlp": w_o_gmlp.shape, "w_o_sb": w_o_sb.shape, "w_out": w_out.shape,
              "norm_final": norm_final.shape}
    names = list(shapes)
    outs = [loss, grad_x.reshape(batch, seq, d)]
    for kind in range(4):
        outs += [res[name][kind].reshape(shapes[name]) for name in names]
    return tuple(outs)
```
